```python
import math
import jax, jax.numpy as jnp
from jax import lax
import numpy as np


D_MODEL = 1024
BATCH = 8
SEQ = 8192
DEPTH = 1

MEM_TOKENS = 256
S5_WIDTH = D_MODEL // 4
S5_GROUP_CH = 16
S5_GROUPS = S5_WIDTH // S5_GROUP_CH
S5_STATE = 64
S5_MAX_RE = -1e-4
S5_DT_MIN = 1e-3
S5_DT_MAX = 1e-1
MLA_HEADS = 8
MLA_NOPE_DIM = 64
MLA_ROPE_DIM = 32
MLA_QK_DIM = MLA_NOPE_DIM + MLA_ROPE_DIM
MLA_V_DIM = 64
MLA_Q_RANK = D_MODEL // 4
MLA_KV_RANK = D_MODEL // 4
ROPE_THETA = 10000.0
Q_BLOCK = 128
XATTN_HEADS = 4
XATTN_HEAD_DIM = D_MODEL // XATTN_HEADS
MLP_HIDDEN = 4 * D_MODEL
LN_EPS = 1e-5
RMS_EPS = 1e-6
NEG_INF = -1e30
POS_OFFSET_MAX = 4096
DN_ALPHA = (2.0 * DEPTH) ** 0.25
DN_BETA = (8.0 * DEPTH) ** -0.25
IN_S5 = S5_WIDTH
IN_Q = MLA_Q_RANK
IN_KV = MLA_KV_RANK
IN_KR = MLA_ROPE_DIM
IN_GATE = 2 * D_MODEL
IN_WIDTH = IN_S5 + IN_Q + IN_KV + IN_KR + IN_GATE

kernel_name = "hybrid_s5_mla_gated_deepnorm_layer"


def layer_norm(x, g, b):
    xf = x.astype(jnp.float32)
    mu = jnp.mean(xf, axis=-1, keepdims=True)
    xc = xf - mu
    var = jnp.mean(xc * xc, axis=-1, keepdims=True)
    return (xc * lax.rsqrt(var + LN_EPS) * g.astype(jnp.float32) + b.astype(jnp.float32)).astype(x.dtype)


def rms_norm(x, g):
    xf = x.astype(jnp.float32)
    return (xf * lax.rsqrt(jnp.mean(xf * xf, axis=-1, keepdims=True) + RMS_EPS) * g.astype(jnp.float32)).astype(x.dtype)


def rope_tables(positions):
    inv = ROPE_THETA ** (-jnp.arange(0, MLA_ROPE_DIM, 2, dtype=jnp.float32) / MLA_ROPE_DIM)
    ang = positions.astype(jnp.float32)[..., None] * inv
    return jnp.cos(ang)[:, :, None, :], jnp.sin(ang)[:, :, None, :]


def apply_rope(x, cos, sin):
    xf = x.astype(jnp.float32)
    x1, x2 = jnp.split(xf, 2, axis=-1)
    return jnp.concatenate([x1 * cos - x2 * sin, x1 * sin + x2 * cos], axis=-1).astype(x.dtype)


def _complex_scan_op(e1, e2):
    a1r, a1i, b1r, b1i = e1
    a2r, a2i, b2r, b2i = e2
    ar = a1r * a2r - a1i * a2i
    ai = a1r * a2i + a1i * a2r
    br = a2r * b1r - a2i * b1i + b2r
    bi = a2r * b1i + a2i * b1r + b2i
    return (ar, ai, br, bi)


def s5_ssm(u, lam_re, lam_im, log_dt, b_re, b_im, c_re, c_im, d_skip):
    bsz, seq, _ = u.shape
    uf = u.astype(jnp.float32).reshape(bsz, seq, S5_GROUPS, S5_GROUP_CH)
    lr = jnp.minimum(lam_re.astype(jnp.float32), S5_MAX_RE)
    li = lam_im.astype(jnp.float32)
    dt = jnp.exp(log_dt.astype(jnp.float32))[:, None]
    mag = jnp.exp(lr * dt)
    ang = li * dt
    ab_re = mag * jnp.cos(ang)
    ab_im = mag * jnp.sin(ang)
    den = lr * lr + li * li
    nr = ab_re - 1.0
    f_re = ((nr * lr + ab_im * li) / den)[..., None]
    f_im = ((ab_im * lr - nr * li) / den)[..., None]
    br = b_re.astype(jnp.float32)
    bi = b_im.astype(jnp.float32)
    bb_re = f_re * br - f_im * bi
    bb_im = f_re * bi + f_im * br
    bu_re = jnp.einsum('bsgh,gph->bsgp', uf, bb_re)
    bu_im = jnp.einsum('bsgh,gph->bsgp', uf, bb_im)
    a_re = jnp.broadcast_to(ab_re[None, None], (1, seq, S5_GROUPS, S5_STATE))
    a_im = jnp.broadcast_to(ab_im[None, None], (1, seq, S5_GROUPS, S5_STATE))
    _, _, h_re, h_im = lax.associative_scan(_complex_scan_op, (a_re, a_im, bu_re, bu_im), axis=1)
    y = (jnp.einsum('bsgp,ghp->bsgh', h_re, c_re.astype(jnp.float32))
         - jnp.einsum('bsgp,ghp->bsgh', h_im, c_im.astype(jnp.float32)))
    y = y + d_skip.astype(jnp.float32).reshape(S5_GROUPS, S5_GROUP_CH) * uf
    return y.reshape(bsz, seq, S5_WIDTH)


def causal_block_attention(q, k, v):
    bsz, seq, heads, dqk = q.shape
    nblk = seq // Q_BLOCK
    scale = dqk ** -0.5
    qb = jnp.moveaxis(q.reshape(bsz, nblk, Q_BLOCK, heads, dqk), 1, 0)
    starts = jnp.arange(nblk, dtype=jnp.int32) * Q_BLOCK
    kpos = jnp.arange(seq, dtype=jnp.int32)

    def block(args):
        qi, start = args
        s = jnp.einsum('bqhd,bkhd->bhqk', qi, k).astype(jnp.float32) * scale
        qpos = start + jnp.arange(Q_BLOCK, dtype=jnp.int32)
        s = jnp.where(kpos[None, :] <= qpos[:, None], s, NEG_INF)
        p = jax.nn.softmax(s, axis=-1).astype(v.dtype)
        return jnp.einsum('bhqk,bkhd->bqhd', p, v)

    o = lax.map(block, (qb, starts))
    return jnp.moveaxis(o, 0, 1).reshape(bsz, seq, heads * v.shape[-1])


def hybrid_mixer(h, cos, sin, w_in, s5_lam_re, s5_lam_im, s5_log_dt, s5_b_re, s5_b_im,
                 s5_c_re, s5_c_im, s5_d, w_glu, q_norm_g, w_uq, kv_norm_g, w_ukv, w_oa, w_o):
    bsz, seq, _ = h.shape
    z = h @ w_in
    o1 = IN_S5
    o2 = o1 + IN_Q
    o3 = o2 + IN_KV
    o4 = o3 + IN_KR
    u = z[..., :o1]
    c_q = z[..., o1:o2]
    c_kv = z[..., o2:o3]
    k_r = z[..., o3:o4]
    gate = z[..., o4:]
    y = s5_ssm(u, s5_lam_re, s5_lam_im, s5_log_dt, s5_b_re, s5_b_im, s5_c_re, s5_c_im, s5_d).astype(h.dtype)
    y = jax.nn.gelu(y, approximate=False) @ w_glu
    s_out = y[..., :D_MODEL] * jax.nn.sigmoid(y[..., D_MODEL:])
    q = (rms_norm(c_q, q_norm_g) @ w_uq).reshape(bsz, seq, MLA_HEADS, MLA_QK_DIM)
    q = jnp.concatenate([q[..., :MLA_NOPE_DIM], apply_rope(q[..., MLA_NOPE_DIM:], cos, sin)], axis=-1)
    kv = (rms_norm(c_kv, kv_norm_g) @ w_ukv).reshape(bsz, seq, MLA_HEADS, MLA_NOPE_DIM + MLA_V_DIM)
    k_rope = apply_rope(k_r[:, :, None, :], cos, sin)
    k = jnp.concatenate([kv[..., :MLA_NOPE_DIM],
                         jnp.broadcast_to(k_rope, (bsz, seq, MLA_HEADS, MLA_ROPE_DIM))], axis=-1)
    v = kv[..., MLA_NOPE_DIM:]
    a_out = causal_block_attention(q, k, v) @ w_oa
    g_s = jax.nn.sigmoid(gate[..., :D_MODEL])
    g_a = jax.nn.sigmoid(gate[..., D_MODEL:])
    return (g_s * s_out + g_a * a_out) @ w_o


def memory_cross_attention(h, mem, w_xq, w_xk, w_xv, w_xo):
    bsz, seq, _ = h.shape
    m = mem.shape[1]
    q = (h @ w_xq).reshape(bsz, seq, XATTN_HEADS, XATTN_HEAD_DIM)
    k = (mem @ w_xk).reshape(bsz, m, XATTN_HEADS, XATTN_HEAD_DIM)
    v = (mem @ w_xv).reshape(bsz, m, XATTN_HEADS, XATTN_HEAD_DIM)
    s = jnp.einsum('bshd,bmhd->bhsm', q, k).astype(jnp.float32) * (XATTN_HEAD_DIM ** -0.5)
    p = jax.nn.softmax(s, axis=-1).astype(v.dtype)
    o = jnp.einsum('bhsm,bmhd->bshd', p, v).reshape(bsz, seq, D_MODEL)
    return o @ w_xo


def squared_relu_mlp(h, w_up, w_down):
    return jnp.square(jax.nn.relu(h @ w_up)) @ w_down


def _fwd_setup_inputs(seed: int = 0) -> dict:
    key = jax.random.key(seed)
    ks = jax.random.split(key, 40)
    f32 = jnp.float32

    def nrm(k, shape, scale):
        return jax.random.normal(k, shape, f32) * scale

    L = DEPTH
    G, P, H = S5_GROUPS, S5_STATE, S5_GROUP_CH
    n = jnp.arange(P, dtype=f32)
    positions = (jax.random.randint(ks[2], (BATCH, 1), 0, POS_OFFSET_MAX, dtype=jnp.int32)
                 + jnp.arange(SEQ, dtype=jnp.int32)[None, :])
    return {
        "x": nrm(ks[0], (BATCH, SEQ, D_MODEL), 1.0),
        "mem": nrm(ks[1], (BATCH, MEM_TOKENS, D_MODEL), 1.0),
        "positions": positions,
        "ln_in_g": 1.0 + nrm(ks[3], (D_MODEL,), 0.02),
        "ln_in_b": nrm(ks[4], (D_MODEL,), 0.02),
        "w_in": nrm(ks[5], (L, D_MODEL, IN_WIDTH), D_MODEL ** -0.5),
        "s5_lam_re": -0.5 + nrm(ks[6], (L, G, P), 0.01),
        "s5_lam_im": math.pi * n + nrm(ks[7], (L, G, P), 0.01),
        "s5_log_dt": jax.random.uniform(ks[8], (L, G), f32, math.log(S5_DT_MIN), math.log(S5_DT_MAX)),
        "s5_b_re": nrm(ks[9], (L, G, P, H), (2.0 * H) ** -0.5),
        "s5_b_im": nrm(ks[10], (L, G, P, H), (2.0 * H) ** -0.5),
        "s5_c_re": nrm(ks[11], (L, G, H, P), P ** -0.5),
        "s5_c_im": nrm(ks[12], (L, G, H, P), P ** -0.5),
        "s5_d": nrm(ks[13], (L, S5_WIDTH), 1.0),
        "w_glu": nrm(ks[14], (L, S5_WIDTH, 2 * D_MODEL), S5_WIDTH ** -0.5),
        "q_norm_g": 1.0 + nrm(ks[15], (L, MLA_Q_RANK), 0.02),
        "w_uq": nrm(ks[16], (L, MLA_Q_RANK, MLA_HEADS * MLA_QK_DIM), MLA_Q_RANK ** -0.5),
        "kv_norm_g": 1.0 + nrm(ks[17], (L, MLA_KV_RANK), 0.02),
        "w_ukv": nrm(ks[18], (L, MLA_KV_RANK, MLA_HEADS * (MLA_NOPE_DIM + MLA_V_DIM)), MLA_KV_RANK ** -0.5),
        "w_oa": nrm(ks[19], (L, MLA_HEADS * MLA_V_DIM, D_MODEL), (MLA_HEADS * MLA_V_DIM) ** -0.5),
        "w_o": nrm(ks[20], (L, D_MODEL, D_MODEL), DN_BETA * D_MODEL ** -0.5),
        "ln1_g": 1.0 + nrm(ks[21], (L, D_MODEL), 0.02),
        "ln1_b": nrm(ks[22], (L, D_MODEL), 0.02),
        "w_xq": nrm(ks[23], (L, D_MODEL, D_MODEL), D_MODEL ** -0.5),
        "w_xk": nrm(ks[24], (L, D_MODEL, D_MODEL), D_MODEL ** -0.5),
        "w_xv": nrm(ks[25], (L, D_MODEL, D_MODEL), DN_BETA * D_MODEL ** -0.5),
        "w_xo": nrm(ks[26], (L, D_MODEL, D_MODEL), DN_BETA * D_MODEL ** -0.5),
        "ln2_g": 1.0 + nrm(ks[27], (L, D_MODEL), 0.02),
        "ln2_b": nrm(ks[28], (L, D_MODEL), 0.02),
        "w_up": nrm(ks[29], (L, D_MODEL, MLP_HIDDEN), DN_BETA * D_MODEL ** -0.5),
        "w_down": nrm(ks[30], (L, MLP_HIDDEN, D_MODEL), DN_BETA * MLP_HIDDEN ** -0.5),
        "ln3_g": 1.0 + nrm(ks[31], (L, D_MODEL), 0.02),
        "ln3_b": nrm(ks[32], (L, D_MODEL), 0.02),
    }


def _fwd_reference(x, mem, positions, ln_in_g, ln_in_b, w_in, s5_lam_re, s5_lam_im, s5_log_dt,
              s5_b_re, s5_b_im, s5_c_re, s5_c_im, s5_d, w_glu, q_norm_g, w_uq, kv_norm_g, w_ukv,
              w_oa, w_o, ln1_g, ln1_b, w_xq, w_xk, w_xv, w_xo, ln2_g, ln2_b, w_up, w_down,
              ln3_g, ln3_b):
    cos, sin = rope_tables(positions)
    h = layer_norm(x, ln_in_g, ln_in_b)
    for l in range(DEPTH):
        mix = hybrid_mixer(h, cos, sin, w_in[l], s5_lam_re[l], s5_lam_im[l], s5_log_dt[l],
                           s5_b_re[l], s5_b_im[l], s5_c_re[l], s5_c_im[l], s5_d[l], w_glu[l],
                           q_norm_g[l], w_uq[l], kv_norm_g[l], w_ukv[l], w_oa[l], w_o[l])
        h = layer_norm(DN_ALPHA * h + mix, ln1_g[l], ln1_b[l])
        xa = memory_cross_attention(h, mem, w_xq[l], w_xk[l], w_xv[l], w_xo[l])
        h = layer_norm(DN_ALPHA * h + xa, ln2_g[l], ln2_b[l])
        ff = squared_relu_mlp(h, w_up[l], w_down[l])
        h = layer_norm(DN_ALPHA * h + ff, ln3_g[l], ln3_b[l])
    return h


import jax as _jax
import jax.numpy as _jnp

TWIN_FORMAT = 'train_step'
FWD_PARAMS = ['x', 'mem', 'positions', 'ln_in_g', 'ln_in_b', 'w_in', 's5_lam_re', 's5_lam_im', 's5_log_dt', 's5_b_re', 's5_b_im', 's5_c_re', 's5_c_im', 's5_d', 'w_glu', 'q_norm_g', 'w_uq', 'kv_norm_g', 'w_ukv', 'w_oa', 'w_o', 'ln1_g', 'ln1_b', 'w_xq', 'w_xk', 'w_xv', 'w_xo', 'ln2_g', 'ln2_b', 'w_up', 'w_down', 'ln3_g', 'ln3_b']
TWIN_WEIGHTS = ['ln_in_g', 'ln_in_b', 'w_in', 's5_lam_re', 's5_lam_im', 's5_log_dt', 's5_b_re', 's5_b_im', 's5_c_re', 's5_c_im', 's5_d', 'w_glu', 'q_norm_g', 'w_uq', 'kv_norm_g', 'w_ukv', 'w_oa', 'w_o', 'ln1_g', 'ln1_b', 'w_xq', 'w_xk', 'w_xv', 'w_xo', 'ln2_g', 'ln2_b', 'w_up', 'w_down', 'ln3_g', 'ln3_b']
TWIN_DIFF_INPUT = 'x'
TWIN_INPUTS = ['x', 'mem', 'positions', 'ln_in_g', 'ln_in_b', 'w_in', 's5_lam_re', 's5_lam_im', 's5_log_dt', 's5_b_re', 's5_b_im', 's5_c_re', 's5_c_im', 's5_d', 'w_glu', 'q_norm_g', 'w_uq', 'kv_norm_g', 'w_ukv', 'w_oa', 'w_o', 'ln1_g', 'ln1_b', 'w_xq', 'w_xk', 'w_xv', 'w_xo', 'ln2_g', 'ln2_b', 'w_up', 'w_down', 'ln3_g', 'ln3_b', 'loss_target', 'm_ln_in_g', 'm_ln_in_b', 'm_w_in', 'm_s5_lam_re', 'm_s5_lam_im', 'm_s5_log_dt', 'm_s5_b_re', 'm_s5_b_im', 'm_s5_c_re', 'm_s5_c_im', 'm_s5_d', 'm_w_glu', 'm_q_norm_g', 'm_w_uq', 'm_kv_norm_g', 'm_w_ukv', 'm_w_oa', 'm_w_o', 'm_ln1_g', 'm_ln1_b', 'm_w_xq', 'm_w_xk', 'm_w_xv', 'm_w_xo', 'm_ln2_g', 'm_ln2_b', 'm_w_up', 'm_w_down', 'm_ln3_g', 'm_ln3_b', 'v_ln_in_g', 'v_ln_in_b', 'v_w_in', 'v_s5_lam_re', 'v_s5_lam_im', 'v_s5_log_dt', 'v_s5_b_re', 'v_s5_b_im', 'v_s5_c_re', 'v_s5_c_im', 'v_s5_d', 'v_w_glu', 'v_q_norm_g', 'v_w_uq', 'v_kv_norm_g', 'v_w_ukv', 'v_w_oa', 'v_w_o', 'v_ln1_g', 'v_ln1_b', 'v_w_xq', 'v_w_xk', 'v_w_xv', 'v_w_xo', 'v_ln2_g', 'v_ln2_b', 'v_w_up', 'v_w_down', 'v_ln3_g', 'v_ln3_b']
TWIN_OUTPUTS = ['loss', 'grad_x', 'grad_ln_in_g', 'grad_ln_in_b', 'grad_w_in', 'grad_s5_lam_re', 'grad_s5_lam_im', 'grad_s5_log_dt', 'grad_s5_b_re', 'grad_s5_b_im', 'grad_s5_c_re', 'grad_s5_c_im', 'grad_s5_d', 'grad_w_glu', 'grad_q_norm_g', 'grad_w_uq', 'grad_kv_norm_g', 'grad_w_ukv', 'grad_w_oa', 'grad_w_o', 'grad_ln1_g', 'grad_ln1_b', 'grad_w_xq', 'grad_w_xk', 'grad_w_xv', 'grad_w_xo', 'grad_ln2_g', 'grad_ln2_b', 'grad_w_up', 'grad_w_down', 'grad_ln3_g', 'grad_ln3_b', 'delta_ln_in_g', 'delta_ln_in_b', 'delta_w_in', 'delta_s5_lam_re', 'delta_s5_lam_im', 'delta_s5_log_dt', 'delta_s5_b_re', 'delta_s5_b_im', 'delta_s5_c_re', 'delta_s5_c_im', 'delta_s5_d', 'delta_w_glu', 'delta_q_norm_g', 'delta_w_uq', 'delta_kv_norm_g', 'delta_w_ukv', 'delta_w_oa', 'delta_w_o', 'delta_ln1_g', 'delta_ln1_b', 'delta_w_xq', 'delta_w_xk', 'delta_w_xv', 'delta_w_xo', 'delta_ln2_g', 'delta_ln2_b', 'delta_w_up', 'delta_w_down', 'delta_ln3_g', 'delta_ln3_b', 'new_m_ln_in_g', 'new_m_ln_in_b', 'new_m_w_in', 'new_m_s5_lam_re', 'new_m_s5_lam_im', 'new_m_s5_log_dt', 'new_m_s5_b_re', 'new_m_s5_b_im', 'new_m_s5_c_re', 'new_m_s5_c_im', 'new_m_s5_d', 'new_m_w_glu', 'new_m_q_norm_g', 'new_m_w_uq', 'new_m_kv_norm_g', 'new_m_w_ukv', 'new_m_w_oa', 'new_m_w_o', 'new_m_ln1_g', 'new_m_ln1_b', 'new_m_w_xq', 'new_m_w_xk', 'new_m_w_xv', 'new_m_w_xo', 'new_m_ln2_g', 'new_m_ln2_b', 'new_m_w_up', 'new_m_w_down', 'new_m_ln3_g', 'new_m_ln3_b', 'new_v_ln_in_g', 'new_v_ln_in_b', 'new_v_w_in', 'new_v_s5_lam_re', 'new_v_s5_lam_im', 'new_v_s5_log_dt', 'new_v_s5_b_re', 'new_v_s5_b_im', 'new_v_s5_c_re', 'new_v_s5_c_im', 'new_v_s5_d', 'new_v_w_glu', 'new_v_q_norm_g', 'new_v_w_uq', 'new_v_kv_norm_g', 'new_v_w_ukv', 'new_v_w_oa', 'new_v_w_o', 'new_v_ln1_g', 'new_v_ln1_b', 'new_v_w_xq', 'new_v_w_xk', 'new_v_w_xv', 'new_v_w_xo', 'new_v_ln2_g', 'new_v_ln2_b', 'new_v_w_up', 'new_v_w_down', 'new_v_ln3_g', 'new_v_ln3_b']
TWIN_LEAF_KINDS = {'loss': 'loss', 'grad_x': 'grad_x', 'grad_ln_in_g': 'grad_w', 'grad_ln_in_b': 'grad_w', 'grad_w_in': 'grad_w', 'grad_s5_lam_re': 'grad_w', 'grad_s5_lam_im': 'grad_w', 'grad_s5_log_dt': 'grad_w', 'grad_s5_b_re': 'grad_w', 'grad_s5_b_im': 'grad_w', 'grad_s5_c_re': 'grad_w', 'grad_s5_c_im': 'grad_w', 'grad_s5_d': 'grad_w', 'grad_w_glu': 'grad_w', 'grad_q_norm_g': 'grad_w', 'grad_w_uq': 'grad_w', 'grad_kv_norm_g': 'grad_w', 'grad_w_ukv': 'grad_w', 'grad_w_oa': 'grad_w', 'grad_w_o': 'grad_w', 'grad_ln1_g': 'grad_w', 'grad_ln1_b': 'grad_w', 'grad_w_xq': 'grad_w', 'grad_w_xk': 'grad_w', 'grad_w_xv': 'grad_w', 'grad_w_xo': 'grad_w', 'grad_ln2_g': 'grad_w', 'grad_ln2_b': 'grad_w', 'grad_w_up': 'grad_w', 'grad_w_down': 'grad_w', 'grad_ln3_g': 'grad_w', 'grad_ln3_b': 'grad_w', 'delta_ln_in_g': 'delta_w', 'delta_ln_in_b': 'delta_w', 'delta_w_in': 'delta_w', 'delta_s5_lam_re': 'delta_w', 'delta_s5_lam_im': 'delta_w', 'delta_s5_log_dt': 'delta_w', 'delta_s5_b_re': 'delta_w', 'delta_s5_b_im': 'delta_w', 'delta_s5_c_re': 'delta_w', 'delta_s5_c_im': 'delta_w', 'delta_s5_d': 'delta_w', 'delta_w_glu': 'delta_w', 'delta_q_norm_g': 'delta_w', 'delta_w_uq': 'delta_w', 'delta_kv_norm_g': 'delta_w', 'delta_w_ukv': 'delta_w', 'delta_w_oa': 'delta_w', 'delta_w_o': 'delta_w', 'delta_ln1_g': 'delta_w', 'delta_ln1_b': 'delta_w', 'delta_w_xq': 'delta_w', 'delta_w_xk': 'delta_w', 'delta_w_xv': 'delta_w', 'delta_w_xo': 'delta_w', 'delta_ln2_g': 'delta_w', 'delta_ln2_b': 'delta_w', 'delta_w_up': 'delta_w', 'delta_w_down': 'delta_w', 'delta_ln3_g': 'delta_w', 'delta_ln3_b': 'delta_w', 'new_m_ln_in_g': 'new_m', 'new_m_ln_in_b': 'new_m', 'new_m_w_in': 'new_m', 'new_m_s5_lam_re': 'new_m', 'new_m_s5_lam_im': 'new_m', 'new_m_s5_log_dt': 'new_m', 'new_m_s5_b_re': 'new_m', 'new_m_s5_b_im': 'new_m', 'new_m_s5_c_re': 'new_m', 'new_m_s5_c_im': 'new_m', 'new_m_s5_d': 'new_m', 'new_m_w_glu': 'new_m', 'new_m_q_norm_g': 'new_m', 'new_m_w_uq': 'new_m', 'new_m_kv_norm_g': 'new_m', 'new_m_w_ukv': 'new_m', 'new_m_w_oa': 'new_m', 'new_m_w_o': 'new_m', 'new_m_ln1_g': 'new_m', 'new_m_ln1_b': 'new_m', 'new_m_w_xq': 'new_m', 'new_m_w_xk': 'new_m', 'new_m_w_xv': 'new_m', 'new_m_w_xo': 'new_m', 'new_m_ln2_g': 'new_m', 'new_m_ln2_b': 'new_m', 'new_m_w_up': 'new_m', 'new_m_w_down': 'new_m', 'new_m_ln3_g': 'new_m', 'new_m_ln3_b': 'new_m', 'new_v_ln_in_g': 'new_v', 'new_v_ln_in_b': 'new_v', 'new_v_w_in': 'new_v', 'new_v_s5_lam_re': 'new_v', 'new_v_s5_lam_im': 'new_v', 'new_v_s5_log_dt': 'new_v', 'new_v_s5_b_re': 'new_v', 'new_v_s5_b_im': 'new_v', 'new_v_s5_c_re': 'new_v', 'new_v_s5_c_im': 'new_v', 'new_v_s5_d': 'new_v', 'new_v_w_glu': 'new_v', 'new_v_q_norm_g': 'new_v', 'new_v_w_uq': 'new_v', 'new_v_kv_norm_g': 'new_v', 'new_v_w_ukv': 'new_v', 'new_v_w_oa': 'new_v', 'new_v_w_o': 'new_v', 'new_v_ln1_g': 'new_v', 'new_v_ln1_b': 'new_v', 'new_v_w_xq': 'new_v', 'new_v_w_xk': 'new_v', 'new_v_w_xv': 'new_v', 'new_v_w_xo': 'new_v', 'new_v_ln2_g': 'new_v', 'new_v_ln2_b': 'new_v', 'new_v_w_up': 'new_v', 'new_v_w_down': 'new_v', 'new_v_ln3_g': 'new_v', 'new_v_ln3_b': 'new_v'}


def _forward(args):
    return _fwd_reference(*[args[k] for k in FWD_PARAMS])


def _output_shape():
    out = _jax.eval_shape(lambda: _forward(_fwd_setup_inputs(0)))
    return out.shape, out.dtype

N_MICROBATCH = 1
ADAM_LR = 0.001
ADAM_B1 = 0.9
ADAM_B2 = 0.999
ADAM_EPS = 1e-08
ADAM_WD = 0.01
ADAM_STEP = 10
PER_EXAMPLE_BATCH_AXIS = {'x': 0, 'mem': 0, 'positions': 0, 'loss_target': 0}
SHARED_INPUTS = []
_WEIGHT_DTYPES = {'ln_in_g': _jnp.float32, 'ln_in_b': _jnp.float32, 'w_in': _jnp.float32, 's5_lam_re': _jnp.float32, 's5_lam_im': _jnp.float32, 's5_log_dt': _jnp.float32, 's5_b_re': _jnp.float32, 's5_b_im': _jnp.float32, 's5_c_re': _jnp.float32, 's5_c_im': _jnp.float32, 's5_d': _jnp.float32, 'w_glu': _jnp.float32, 'q_norm_g': _jnp.float32, 'w_uq': _jnp.float32, 'kv_norm_g': _jnp.float32, 'w_ukv': _jnp.float32, 'w_oa': _jnp.float32, 'w_o': _jnp.float32, 'ln1_g': _jnp.float32, 'ln1_b': _jnp.float32, 'w_xq': _jnp.float32, 'w_xk': _jnp.float32, 'w_xv': _jnp.float32, 'w_xo': _jnp.float32, 'ln2_g': _jnp.float32, 'ln2_b': _jnp.float32, 'w_up': _jnp.float32, 'w_down': _jnp.float32, 'ln3_g': _jnp.float32, 'ln3_b': _jnp.float32}
MOMENT_SCALE = {'ln_in_g': 1.744407e+00, 'ln_in_b': 7.753244e-01, 'w_in': 2.178345e-02, 's5_lam_re': 4.047231e-03, 's5_lam_im': 4.580301e-03, 's5_log_dt': 2.669360e+00, 's5_b_re': 2.352255e-03, 's5_b_im': 2.226709e-03, 's5_c_re': 3.127310e-03, 's5_c_im': 3.206330e-03, 's5_d': 8.997964e-02, 'w_glu': 2.610487e-02, 'q_norm_g': 2.494471e-02, 'w_uq': 1.408088e-02, 'kv_norm_g': 3.785365e-02, 'w_ukv': 1.797152e-02, 'w_oa': 1.475772e-02, 'w_o': 5.007329e-02, 'ln1_g': 1.773603e+00, 'ln1_b': 7.351714e-01, 'w_xq': 7.821186e-03, 'w_xk': 7.859110e-03, 'w_xv': 1.514021e-02, 'w_xo': 1.515631e-02, 'ln2_g': 1.779077e+00, 'ln2_b': 7.372264e-01, 'w_up': 5.158884e-02, 'w_down': 1.210016e-01, 'ln3_g': 6.405614e+01, 'ln3_b': 5.934436e+00}


def _to_microbatches(a, axis):
    t = _jnp.moveaxis(a, axis, 0)
    t = t.reshape((N_MICROBATCH, t.shape[0] // N_MICROBATCH) + t.shape[1:])
    return _jnp.moveaxis(t, 1, axis + 1)


def setup_inputs(seed: int = 0) -> dict:
    inp = _fwd_setup_inputs(seed)
    key = _jax.random.fold_in(_jax.random.key(seed), 7919)
    shape, _ = _output_shape()
    out = dict(inp)
    out["loss_target"] = _jax.random.normal(_jax.random.fold_in(key, 0), shape, _jnp.float32)
    for i, name in enumerate(TWIN_WEIGHTS):
        w = inp[name].astype(_jnp.float32)
        if MOMENT_SCALE is None:
            s = _jnp.sqrt(_jnp.mean(_jnp.square(w)) + 1e-30)
        else:
            s = MOMENT_SCALE[name]
        km, kv = _jax.random.split(_jax.random.fold_in(key, i + 1))
        out[name] = w
        out["m_" + name] = s * _jax.random.normal(km, w.shape, _jnp.float32)
        out["v_" + name] = (s * s) * _jax.random.uniform(kv, w.shape, _jnp.float32, 0.5, 1.5)
    if N_MICROBATCH > 1:
        for name, axis in PER_EXAMPLE_BATCH_AXIS.items():
            out[name] = _to_microbatches(out[name], axis)
    return {'x': out['x'], 'mem': out['mem'], 'positions': out['positions'], 'ln_in_g': out['ln_in_g'], 'ln_in_b': out['ln_in_b'], 'w_in': out['w_in'], 's5_lam_re': out['s5_lam_re'], 's5_lam_im': out['s5_lam_im'], 's5_log_dt': out['s5_log_dt'], 's5_b_re': out['s5_b_re'], 's5_b_im': out['s5_b_im'], 's5_c_re': out['s5_c_re'], 's5_c_im': out['s5_c_im'], 's5_d': out['s5_d'], 'w_glu': out['w_glu'], 'q_norm_g': out['q_norm_g'], 'w_uq': out['w_uq'], 'kv_norm_g': out['kv_norm_g'], 'w_ukv': out['w_ukv'], 'w_oa': out['w_oa'], 'w_o': out['w_o'], 'ln1_g': out['ln1_g'], 'ln1_b': out['ln1_b'], 'w_xq': out['w_xq'], 'w_xk': out['w_xk'], 'w_xv': out['w_xv'], 'w_xo': out['w_xo'], 'ln2_g': out['ln2_g'], 'ln2_b': out['ln2_b'], 'w_up': out['w_up'], 'w_down': out['w_down'], 'ln3_g': out['ln3_g'], 'ln3_b': out['ln3_b'], 'loss_target': out['loss_target'], 'm_ln_in_g': out['m_ln_in_g'], 'm_ln_in_b': out['m_ln_in_b'], 'm_w_in': out['m_w_in'], 'm_s5_lam_re': out['m_s5_lam_re'], 'm_s5_lam_im': out['m_s5_lam_im'], 'm_s5_log_dt': out['m_s5_log_dt'], 'm_s5_b_re': out['m_s5_b_re'], 'm_s5_b_im': out['m_s5_b_im'], 'm_s5_c_re': out['m_s5_c_re'], 'm_s5_c_im': out['m_s5_c_im'], 'm_s5_d': out['m_s5_d'], 'm_w_glu': out['m_w_glu'], 'm_q_norm_g': out['m_q_norm_g'], 'm_w_uq': out['m_w_uq'], 'm_kv_norm_g': out['m_kv_norm_g'], 'm_w_ukv': out['m_w_ukv'], 'm_w_oa': out['m_w_oa'], 'm_w_o': out['m_w_o'], 'm_ln1_g': out['m_ln1_g'], 'm_ln1_b': out['m_ln1_b'], 'm_w_xq': out['m_w_xq'], 'm_w_xk': out['m_w_xk'], 'm_w_xv': out['m_w_xv'], 'm_w_xo': out['m_w_xo'], 'm_ln2_g': out['m_ln2_g'], 'm_ln2_b': out['m_ln2_b'], 'm_w_up': out['m_w_up'], 'm_w_down': out['m_w_down'], 'm_ln3_g': out['m_ln3_g'], 'm_ln3_b': out['m_ln3_b'], 'v_ln_in_g': out['v_ln_in_g'], 'v_ln_in_b': out['v_ln_in_b'], 'v_w_in': out['v_w_in'], 'v_s5_lam_re': out['v_s5_lam_re'], 'v_s5_lam_im': out['v_s5_lam_im'], 'v_s5_log_dt': out['v_s5_log_dt'], 'v_s5_b_re': out['v_s5_b_re'], 'v_s5_b_im': out['v_s5_b_im'], 'v_s5_c_re': out['v_s5_c_re'], 'v_s5_c_im': out['v_s5_c_im'], 'v_s5_d': out['v_s5_d'], 'v_w_glu': out['v_w_glu'], 'v_q_norm_g': out['v_q_norm_g'], 'v_w_uq': out['v_w_uq'], 'v_kv_norm_g': out['v_kv_norm_g'], 'v_w_ukv': out['v_w_ukv'], 'v_w_oa': out['v_w_oa'], 'v_w_o': out['v_w_o'], 'v_ln1_g': out['v_ln1_g'], 'v_ln1_b': out['v_ln1_b'], 'v_w_xq': out['v_w_xq'], 'v_w_xk': out['v_w_xk'], 'v_w_xv': out['v_w_xv'], 'v_w_xo': out['v_w_xo'], 'v_ln2_g': out['v_ln2_g'], 'v_ln2_b': out['v_ln2_b'], 'v_w_up': out['v_w_up'], 'v_w_down': out['v_w_down'], 'v_ln3_g': out['v_ln3_g'], 'v_ln3_b': out['v_ln3_b']}


def _loss(weights, diff, rest, loss_target):
    with _jax.named_scope("forward"):
        args = {**rest, TWIN_DIFF_INPUT: diff, **{k: w.astype(_WEIGHT_DTYPES[k]) for k, w in weights.items()}}
        y = _forward(args)
    with _jax.named_scope("loss_head"):
        err = _jnp.square(y.astype(_jnp.float32) - loss_target)
        return 0.5 * _jnp.sum(_jnp.mean(err, axis=-1)) if err.ndim else 0.5 * err


def _adamw(w, g, m, v):
    m = ADAM_B1 * m + (1.0 - ADAM_B1) * g
    v = ADAM_B2 * v + (1.0 - ADAM_B2) * _jnp.square(g)
    m_hat = m / (1.0 - ADAM_B1 ** ADAM_STEP)
    v_hat = v / (1.0 - ADAM_B2 ** ADAM_STEP)
    delta = -ADAM_LR * (m_hat / (_jnp.sqrt(v_hat) + ADAM_EPS) + ADAM_WD * w)
    return delta, m, v


def reference(x, mem, positions, ln_in_g, ln_in_b, w_in, s5_lam_re, s5_lam_im, s5_log_dt, s5_b_re, s5_b_im, s5_c_re, s5_c_im, s5_d, w_glu, q_norm_g, w_uq, kv_norm_g, w_ukv, w_oa, w_o, ln1_g, ln1_b, w_xq, w_xk, w_xv, w_xo, ln2_g, ln2_b, w_up, w_down, ln3_g, ln3_b, loss_target, m_ln_in_g, m_ln_in_b, m_w_in, m_s5_lam_re, m_s5_lam_im, m_s5_log_dt, m_s5_b_re, m_s5_b_im, m_s5_c_re, m_s5_c_im, m_s5_d, m_w_glu, m_q_norm_g, m_w_uq, m_kv_norm_g, m_w_ukv, m_w_oa, m_w_o, m_ln1_g, m_ln1_b, m_w_xq, m_w_xk, m_w_xv, m_w_xo, m_ln2_g, m_ln2_b, m_w_up, m_w_down, m_ln3_g, m_ln3_b, v_ln_in_g, v_ln_in_b, v_w_in, v_s5_lam_re, v_s5_lam_im, v_s5_log_dt, v_s5_b_re, v_s5_b_im, v_s5_c_re, v_s5_c_im, v_s5_d, v_w_glu, v_q_norm_g, v_w_uq, v_kv_norm_g, v_w_ukv, v_w_oa, v_w_o, v_ln1_g, v_ln1_b, v_w_xq, v_w_xk, v_w_xv, v_w_xo, v_ln2_g, v_ln2_b, v_w_up, v_w_down, v_ln3_g, v_ln3_b):
    given = dict(x=x, mem=mem, positions=positions, ln_in_g=ln_in_g, ln_in_b=ln_in_b, w_in=w_in, s5_lam_re=s5_lam_re, s5_lam_im=s5_lam_im, s5_log_dt=s5_log_dt, s5_b_re=s5_b_re, s5_b_im=s5_b_im, s5_c_re=s5_c_re, s5_c_im=s5_c_im, s5_d=s5_d, w_glu=w_glu, q_norm_g=q_norm_g, w_uq=w_uq, kv_norm_g=kv_norm_g, w_ukv=w_ukv, w_oa=w_oa, w_o=w_o, ln1_g=ln1_g, ln1_b=ln1_b, w_xq=w_xq, w_xk=w_xk, w_xv=w_xv, w_xo=w_xo, ln2_g=ln2_g, ln2_b=ln2_b, w_up=w_up, w_down=w_down, ln3_g=ln3_g, ln3_b=ln3_b, loss_target=loss_target, m_ln_in_g=m_ln_in_g, m_ln_in_b=m_ln_in_b, m_w_in=m_w_in, m_s5_lam_re=m_s5_lam_re, m_s5_lam_im=m_s5_lam_im, m_s5_log_dt=m_s5_log_dt, m_s5_b_re=m_s5_b_re, m_s5_b_im=m_s5_b_im, m_s5_c_re=m_s5_c_re, m_s5_c_im=m_s5_c_im, m_s5_d=m_s5_d, m_w_glu=m_w_glu, m_q_norm_g=m_q_norm_g, m_w_uq=m_w_uq, m_kv_norm_g=m_kv_norm_g, m_w_ukv=m_w_ukv, m_w_oa=m_w_oa, m_w_o=m_w_o, m_ln1_g=m_ln1_g, m_ln1_b=m_ln1_b, m_w_xq=m_w_xq, m_w_xk=m_w_xk, m_w_xv=m_w_xv, m_w_xo=m_w_xo, m_ln2_g=m_ln2_g, m_ln2_b=m_ln2_b, m_w_up=m_w_up, m_w_down=m_w_down, m_ln3_g=m_ln3_g, m_ln3_b=m_ln3_b, v_ln_in_g=v_ln_in_g, v_ln_in_b=v_ln_in_b, v_w_in=v_w_in, v_s5_lam_re=v_s5_lam_re, v_s5_lam_im=v_s5_lam_im, v_s5_log_dt=v_s5_log_dt, v_s5_b_re=v_s5_b_re, v_s5_b_im=v_s5_b_im, v_s5_c_re=v_s5_c_re, v_s5_c_im=v_s5_c_im, v_s5_d=v_s5_d, v_w_glu=v_w_glu, v_q_norm_g=v_q_norm_g, v_w_uq=v_w_uq, v_kv_norm_g=v_kv_norm_g, v_w_ukv=v_w_ukv, v_w_oa=v_w_oa, v_w_o=v_w_o, v_ln1_g=v_ln1_g, v_ln1_b=v_ln1_b, v_w_xq=v_w_xq, v_w_xk=v_w_xk, v_w_xv=v_w_xv, v_w_xo=v_w_xo, v_ln2_g=v_ln2_g, v_ln2_b=v_ln2_b, v_w_up=v_w_up, v_w_down=v_w_down, v_ln3_g=v_ln3_g, v_ln3_b=v_ln3_b)
    weights = {n: given[n] for n in TWIN_WEIGHTS}
    shared = {n: given[n] for n in SHARED_INPUTS}
    per_example = {n: given[n] for n in ['x', 'mem', 'positions']}
    grad_fn = _jax.value_and_grad(_loss, argnums=(0, 1))

    def one_microbatch(ex, loss_target):
        ex = dict(ex)
        diff = ex.pop(TWIN_DIFF_INPUT)
        return grad_fn(weights, diff, {**shared, **ex}, loss_target)

    if N_MICROBATCH == 1:
        loss, (grad_w, grad_x) = one_microbatch(per_example, given["loss_target"])
    else:
        def body(carry, xs):
            loss_sum, grad_sum = carry
            l_k, (gw_k, gx_k) = one_microbatch(xs[0], xs[1])
            with _jax.named_scope("update"):
                return (loss_sum + l_k, _jax.tree.map(_jnp.add, grad_sum, gw_k)), gx_k

        init = (_jnp.zeros((), _jnp.float32), _jax.tree.map(_jnp.zeros_like, weights))
        (loss, grad_w), grad_x = _jax.lax.scan(body, init, (per_example, given["loss_target"]))
    with _jax.named_scope("update"):
        delta_w, new_m, new_v = {}, {}, {}
        for n in TWIN_WEIGHTS:
            delta_w[n], new_m[n], new_v[n] = _adamw(weights[n], grad_w[n], given["m_" + n], given["v_" + n])
    return (loss, grad_x, *[grad_w[n] for n in TWIN_WEIGHTS], *[delta_w[n] for n in TWIN_WEIGHTS],
            *[new_m[n] for n in TWIN_WEIGHTS], *[new_v[n] for n in TWIN_WEIGHTS])
```

```python
import functools
import math

import jax
import jax.numpy as jnp
from jax import lax
from jax.experimental import pallas as pl
from jax.experimental.pallas import tpu as pltpu

F32 = jnp.float32
BF16 = jnp.bfloat16

D_MODEL = 1024
S5_WIDTH = 256
S5_GROUP_CH = 16
S5_GROUPS = 16
S5_STATE = 64
NS = S5_GROUPS * S5_STATE
S5_MAX_RE = -1e-4
MLA_HEADS = 8
MLA_NOPE = 64
MLA_ROPE = 32
MLA_QK = 96
MLA_V = 64
HEAD_PAD = 128
ROPE_THETA = 10000.0
XATTN_HEADS = 4
XATTN_HD = 256
MLP_HIDDEN = 4096
LN_EPS = 1e-5
RMS_EPS = 1e-6
NEG_INF = -1e30
DN_ALPHA = 2.0 ** 0.25
ADAM_LR = 0.001
ADAM_B1 = 0.9
ADAM_B2 = 0.999
ADAM_EPS = 1e-08
ADAM_WD = 0.01
ADAM_STEP = 10

N_DEV = 8
AXES = ("x", "y", "c")
MESH = pl.DeviceIdType.MESH
LANES = 1024
VMEM_LIMIT = 60 * 1024 * 1024

ROW_TILE = 256
SCAN_TILE = 256
ATT_TILE = 512

SHARDED = (
    ("w_in", "col", 1024, 2848), ("w_glu", "col", 256, 2048), ("w_uq", "col", 256, 768),
    ("w_ukv", "col", 256, 1024), ("w_oa", "col", 512, 1024), ("w_o", "row", 1024, 1024),
    ("w_xq", "row", 1024, 1024), ("w_xk", "row", 1024, 1024), ("w_xv", "row", 1024, 1024),
    ("w_xo", "row", 1024, 1024), ("w_up", "col", 1024, 4096), ("w_down", "row", 4096, 1024),
)
SMALL = ("ln_in_g", "ln_in_b", "s5_lam_re", "s5_lam_im", "s5_log_dt", "s5_b_re", "s5_b_im", "s5_c_re",
         "s5_c_im", "s5_d", "q_norm_g", "kv_norm_g", "ln1_g", "ln1_b", "ln2_g", "ln2_b", "ln3_g", "ln3_b")
WEIGHTS = ("ln_in_g", "ln_in_b", "w_in", "s5_lam_re", "s5_lam_im", "s5_log_dt", "s5_b_re", "s5_b_im",
           "s5_c_re", "s5_c_im", "s5_d", "w_glu", "q_norm_g", "w_uq", "kv_norm_g", "w_ukv", "w_oa", "w_o",
           "ln1_g", "ln1_b", "w_xq", "w_xk", "w_xv", "w_xo", "ln2_g", "ln2_b", "w_up", "w_down", "ln3_g", "ln3_b")


def _round_up(n, m):
    return (n + m - 1) // m * m


def _bf(a):
    return a.astype(BF16)


def _mm(a, b):
    return jnp.dot(_bf(a), _bf(b), preferred_element_type=F32)


def _mm_nt(a, b):
    return lax.dot_general(_bf(a), _bf(b), (((1,), (1,)), ((), ())), preferred_element_type=F32)


def _mm_tn(a, b):
    return lax.dot_general(_bf(a), _bf(b), (((0,), (0,)), ((), ())), preferred_element_type=F32)


def _sigmoid(a):
    return 1.0 / (1.0 + jnp.exp(-a))


def _gelu(a):
    return 0.5 * a * (1.0 + lax.erf(a * (2.0 ** -0.5)))


def _gelu_grad(a):
    return 0.5 * (1.0 + lax.erf(a * (2.0 ** -0.5))) + a * jnp.exp(-0.5 * a * a) * (1.0 / math.sqrt(2.0 * math.pi))


def _ln_fwd(a, g, b):
    mu = jnp.mean(a, axis=-1, keepdims=True)
    ac = a - mu
    var = jnp.mean(ac * ac, axis=-1, keepdims=True)
    rstd = lax.rsqrt(var + LN_EPS)
    xhat = ac * rstd
    return xhat * g + b, xhat, rstd


def _ln_bwd(dy, xhat, rstd, g):
    dxh = dy * g
    m1 = jnp.mean(dxh, axis=-1, keepdims=True)
    m2 = jnp.mean(dxh * xhat, axis=-1, keepdims=True)
    dx = rstd * (dxh - m1 - xhat * m2)
    return dx, jnp.sum(dy * xhat, axis=0, keepdims=True), jnp.sum(dy, axis=0, keepdims=True)


def _rms_fwd(a, g):
    r = lax.rsqrt(jnp.mean(a * a, axis=-1, keepdims=True) + RMS_EPS)
    xn = a * r
    return xn * g, xn, r


def _rms_bwd(dy, xn, r, g):
    dxn = dy * g
    dx = r * (dxn - xn * jnp.mean(dxn * xn, axis=-1, keepdims=True))
    return dx, jnp.sum(dy * xn, axis=0, keepdims=True)


def _tile_heads(a):
    return jnp.concatenate([a] * MLA_HEADS, axis=1)


def _row_call(name, body, n_rows, ts, tiled_in, full_in, weights, tiled_out, acc_out, reverse=False, scratch=()):
    n = n_rows // ts
    assert n * ts == n_rows, (name, n_rows, ts)
    nt, nf, nw = len(tiled_in), len(full_in), len(weights)
    nto, nao = len(tiled_out), len(acc_out)
    if reverse:
        imap = lambda i: (n - 1 - i, 0)
    else:
        imap = lambda i: (i, 0)
    const = lambda i: (0, 0)

    def kern(*refs):
        ins = refs[:nt + nf]
        w_hbm = refs[nt + nf:nt + nf + nw]
        outs = refs[nt + nf + nw:nt + nf + nw + nto + nao]
        scr = refs[nt + nf + nw + nto + nao:]
        w_vmem = scr[:nw]
        extra = scr[nw + 1:] if nw else scr
        if nw:
            sem = scr[nw]

            @pl.when(pl.program_id(0) == 0)
            def _():
                cps = [pltpu.make_async_copy(w_hbm[k], w_vmem[k], sem.at[k]) for k in range(nw)]
                for cp in cps:
                    cp.start()
                for cp in cps:
                    cp.wait()
        body(*ins, *w_vmem, *outs, *extra)

    in_specs = [pl.BlockSpec((ts, a.shape[1]), imap) for a in tiled_in]
    in_specs += [pl.BlockSpec(a.shape, const) for a in full_in]
    in_specs += [pl.BlockSpec(memory_space=pl.ANY) for _ in weights]
    out_shape = [jax.ShapeDtypeStruct((n_rows, c), dt) for (c, dt) in tiled_out]
    out_shape += [jax.ShapeDtypeStruct(shp, F32) for shp in acc_out]
    out_specs = [pl.BlockSpec((ts, c), imap) for (c, dt) in tiled_out]
    out_specs += [pl.BlockSpec(shp, const) for shp in acc_out]
    scratch_shapes = [pltpu.VMEM(w.shape, w.dtype) for w in weights]
    if nw:
        scratch_shapes.append(pltpu.SemaphoreType.DMA((nw,)))
    scratch_shapes += list(scratch)
    return pl.pallas_call(
        kern, name=name, grid=(n,), in_specs=in_specs, out_specs=out_specs, out_shape=out_shape,
        scratch_shapes=scratch_shapes,
        compiler_params=pltpu.CompilerParams(dimension_semantics=("arbitrary",), vmem_limit_bytes=VMEM_LIMIT),
    )(*tiled_in, *full_in, *weights)


def _mem_kv(mem, w_xk, w_xv):
    m = mem.shape[0]

    def body(mem_ref, wk_ref, wv_ref, k_ref, v_ref):
        mb = mem_ref[...]
        k_ref[...] = _mm(mb, wk_ref[...]).astype(BF16)
        v_ref[...] = _mm(mb, wv_ref[...]).astype(BF16)

    return pl.pallas_call(
        body, name="mem_kv",
        out_shape=[jax.ShapeDtypeStruct((m, D_MODEL), BF16)] * 2,
        compiler_params=pltpu.CompilerParams(vmem_limit_bytes=VMEM_LIMIT),
    )(mem, w_xk, w_xv)


def _in_fwd(x, tab, ln_g, ln_b, qg, kvg, w_inx, w_b, w_q2, w_kv2):
    def body(x_ref, tab_ref, lng_ref, lnb_ref, qg_ref, kvg_ref, winx, wb, wq2, wkv2,
             h0_ref, z3_ref, gate_ref, bu_ref, q_ref, k_ref, v_ref):
        h0, _, _ = _ln_fwd(x_ref[...], lng_ref[...], lnb_ref[...])
        h0_ref[...] = h0
        z = _mm(h0, winx[...])
        z3_ref[...] = z[:, :768]
        gate_ref[...] = z[:, 1024:]
        bu_ref[...] = _mm(z[:, :256], wb[...])
        tab_v = tab_ref[...]
        cq1, cq2, ck1 = tab_v[:, :128], tab_v[:, 128:256], tab_v[:, 256:384]
        cqn, _, _ = _rms_fwd(z[:, 256:512], qg_ref[...])
        q2 = _mm(cqn, wq2[...])
        q_ref[...] = (q2[:, :1024] * _tile_heads(cq1) + q2[:, 1024:] * _tile_heads(cq2)).astype(BF16)
        ckvn, _, _ = _rms_fwd(z[:, 512:768], kvg_ref[...])
        kv2 = _mm(ckvn, wkv2[...])
        krp = z[:, 768:896] * ck1 + z[:, 896:1024] * cq2
        k_ref[...] = (kv2[:, :1024] + _tile_heads(krp)).astype(BF16)
        v_ref[...] = kv2[:, 1024:].astype(BF16)

    s = x.shape[0]
    return _row_call(
        "in_fwd", body, s, min(ROW_TILE, s), [x, tab], [ln_g, ln_b, qg, kvg], [w_inx, w_b, w_q2, w_kv2],
        [(1024, F32), (768, F32), (2048, F32), (2048, F32), (1024, BF16), (1024, BF16), (1024, BF16)], [])


def _scan_level(row, xr, xi, ar, ai, k, forward):
    if forward:
        keep = row >= k
        sr, si = pltpu.roll(xr, k, 0), pltpu.roll(xi, k, 0)
    else:
        keep = row < 8 - k
        sr, si = pltpu.roll(xr, 8 - k, 0), pltpu.roll(xi, 8 - k, 0)
    sr = jnp.where(keep, sr, 0.0)
    si = jnp.where(keep, si, 0.0)
    return xr + ar * sr - ai * si, xi + ar * si + ai * sr


def _scan_fwd(pw, bu):
    s = bu.shape[0]
    ts = min(SCAN_TILE, s)
    nblk = ts // 8

    def body(bu_ref, pw_ref, h_ref, carry):
        @pl.when(pl.program_id(0) == 0)
        def _():
            carry[...] = jnp.zeros_like(carry)

        row = lax.broadcasted_iota(jnp.int32, (8, NS), 0)
        a1 = (pw_ref[0:1, :NS], pw_ref[0:1, NS:])
        a2 = (pw_ref[1:2, :NS], pw_ref[1:2, NS:])
        a4 = (pw_ref[3:4, :NS], pw_ref[3:4, NS:])

        def blk(b, _):
            r0 = pl.multiple_of(b * 8, 8)
            xr = bu_ref[pl.ds(r0, 8), :NS]
            xi = bu_ref[pl.ds(r0, 8), NS:]
            xr, xi = _scan_level(row, xr, xi, a1[0], a1[1], 1, True)
            xr, xi = _scan_level(row, xr, xi, a2[0], a2[1], 2, True)
            xr, xi = _scan_level(row, xr, xi, a4[0], a4[1], 4, True)
            cr, ci = carry[7:8, :NS], carry[7:8, NS:]
            pr, pi = pw_ref[:, :NS], pw_ref[:, NS:]
            hr = xr + pr * cr - pi * ci
            hi = xi + pr * ci + pi * cr
            h_ref[pl.ds(r0, 8), :NS] = hr
            h_ref[pl.ds(r0, 8), NS:] = hi
            carry[:, :NS] = hr
            carry[:, NS:] = hi
            return 0

        lax.fori_loop(0, nblk, blk, 0)

    return _row_call("scan_fwd", body, s, ts, [bu], [pw], [], [(2 * NS, F32)], [],
                     scratch=[pltpu.VMEM((8, 2 * NS), F32)])[0]


def _scan_bwd(pwb, gh, h):
    s = gh.shape[0]
    ts = min(SCAN_TILE, s)
    nblk = ts // 8
    n_tiles = s // ts

    def body(g_ref, h_ref, pw_ref, lam_ref, da_ref, carry, acc):
        @pl.when(pl.program_id(0) == 0)
        def _():
            carry[...] = jnp.zeros_like(carry)
            acc[...] = jnp.zeros_like(acc)

        row = lax.broadcasted_iota(jnp.int32, (8, NS), 0)
        a1 = (pw_ref[7:8, :NS], pw_ref[7:8, NS:])
        a2 = (pw_ref[6:7, :NS], pw_ref[6:7, NS:])
        a4 = (pw_ref[4:5, :NS], pw_ref[4:5, NS:])

        def blk(bb, _):
            r0 = pl.multiple_of((nblk - 1 - bb) * 8, 8)
            xr = g_ref[pl.ds(r0, 8), :NS]
            xi = g_ref[pl.ds(r0, 8), NS:]
            xr, xi = _scan_level(row, xr, xi, a1[0], a1[1], 1, False)
            xr, xi = _scan_level(row, xr, xi, a2[0], a2[1], 2, False)
            xr, xi = _scan_level(row, xr, xi, a4[0], a4[1], 4, False)
            cr, ci = carry[0:1, :NS], carry[0:1, NS:]
            pr, pi = pw_ref[:, :NS], pw_ref[:, NS:]
            lr = xr + pr * cr - pi * ci
            li = xi + pr * ci + pi * cr
            lam_ref[pl.ds(r0, 8), :NS] = lr
            lam_ref[pl.ds(r0, 8), NS:] = li
            nr = jnp.where(row < 7, pltpu.roll(lr, 7, 0), cr)
            ni = jnp.where(row < 7, pltpu.roll(li, 7, 0), ci)
            hr = h_ref[pl.ds(r0, 8), :NS]
            hi = h_ref[pl.ds(r0, 8), NS:]
            acc[:, :NS] += nr * hr + ni * hi
            acc[:, NS:] += ni * hr - nr * hi
            carry[:, :NS] = lr
            carry[:, NS:] = li
            return 0

        lax.fori_loop(0, nblk, blk, 0)

        @pl.when(pl.program_id(0) == n_tiles - 1)
        def _():
            da_ref[...] = jnp.sum(acc[...], axis=0, keepdims=True)

    return _row_call("scan_bwd", body, s, ts, [gh, h], [pwb], [], [(2 * NS, F32)], [(1, 2 * NS)], reverse=True,
                     scratch=[pltpu.VMEM((8, 2 * NS), F32), pltpu.VMEM((8, 2 * NS), F32)])


def _s5out_fwd(h, z3, d_skip, w_c, w_glu):
    def body(h_ref, z3_ref, d_ref, wc, wglu, yl_ref, so_ref):
        yl = _mm(h_ref[...], wc[...]) + d_ref[...] * z3_ref[:, :256]
        yl_ref[...] = yl
        y12 = _mm(_gelu(yl), wglu[...])
        so_ref[...] = y12[:, :1024] * _sigmoid(y12[:, 1024:])

    s = h.shape[0]
    return _row_call("s5out_fwd", body, s, min(ROW_TILE, s), [h, z3], [d_skip], [w_c, w_glu],
                     [(256, F32), (1024, F32)], [])


def _causal_mask(t):
    row = lax.broadcasted_iota(jnp.int32, (t, t), 0)
    col = lax.broadcasted_iota(jnp.int32, (t, t), 1)
    return col <= row


def _attn_fwd(q, k, v):
    s = q.shape[0]
    t = min(ATT_TILE, s)
    nq = s // t
    scale = MLA_QK ** -0.5

    def body(q_ref, k_ref, v_ref, o_ref, lse_ref):
        i = pl.program_id(1)
        qb = q_ref[...]

        def step(kb, carry, masked):
            m, l, acc = carry
            r0 = pl.multiple_of(kb * t, t)
            kk = k_ref[pl.ds(r0, t), :]
            vv = v_ref[pl.ds(r0, t), :]
            sc = _mm_nt(qb, kk) * scale
            if masked:
                sc = jnp.where(_causal_mask(t), sc, NEG_INF)
            m_new = jnp.maximum(m, jnp.max(sc, axis=1, keepdims=True))
            p = jnp.exp(sc - m_new)
            a = jnp.exp(m - m_new)
            l = a * l + jnp.sum(p, axis=1, keepdims=True)
            acc = a * acc + _mm(p, vv)
            return m_new, l, acc

        init = (jnp.full((t, 1), NEG_INF, F32), jnp.zeros((t, 1), F32), jnp.zeros((t, HEAD_PAD), F32))
        carry = lax.fori_loop(0, i, lambda kb, c: step(kb, c, False), init)
        m, l, acc = step(i, carry, True)
        o_ref[...] = acc / l
        lse_ref[...] = jnp.broadcast_to(m + jnp.log(l), (t, HEAD_PAD))

    return pl.pallas_call(
        body, name="attn_fwd", grid=(MLA_HEADS, nq),
        in_specs=[pl.BlockSpec((t, HEAD_PAD), lambda h, i: (i, h)),
                  pl.BlockSpec((s, HEAD_PAD), lambda h, i: (0, h)),
                  pl.BlockSpec((s, HEAD_PAD), lambda h, i: (0, h))],
        out_specs=[pl.BlockSpec((t, HEAD_PAD), lambda h, i: (i, h)),
                   pl.BlockSpec((t, HEAD_PAD), lambda h, i: (i, h))],
        out_shape=[jax.ShapeDtypeStruct((s, MLA_HEADS * HEAD_PAD), F32)] * 2,
        compiler_params=pltpu.CompilerParams(dimension_semantics=("arbitrary", "arbitrary"),
                                             vmem_limit_bytes=VMEM_LIMIT),
    )(q, k, v)


def _attn_bwd(q, k, v, do, lse, delta):
    s = q.shape[0]
    t = min(ATT_TILE, s)
    nq = s // t
    scale = MLA_QK ** -0.5

    def body(q_ref, k_ref, v_ref, do_ref, lse_ref, dl_ref, dq_ref, dk_ref, dv_ref):
        j = pl.program_id(1)

        @pl.when(j == 0)
        def _():
            dq_ref[...] = jnp.zeros_like(dq_ref)

        kk = k_ref[...]
        vv = v_ref[...]

        def step(ib, carry, masked):
            dk, dv = carry
            r0 = pl.multiple_of(ib * t, t)
            qq = q_ref[pl.ds(r0, t), :]
            dd = do_ref[pl.ds(r0, t), :]
            ls = lse_ref[pl.ds(r0, t), :][:, :1]
            dl = dl_ref[pl.ds(r0, t), :][:, :1]
            p = jnp.exp(_mm_nt(qq, kk) * scale - ls)
            if masked:
                p = jnp.where(_causal_mask(t), p, 0.0)
            dv = dv + _mm_tn(p, dd)
            ds = p * (_mm_nt(dd, vv) - dl) * scale
            dk = dk + _mm_tn(ds, qq)
            dq_ref[pl.ds(r0, t), :] += _mm(ds, kk)
            return dk, dv

        zero = jnp.zeros((t, HEAD_PAD), F32)
        carry = step(j, (zero, zero), True)
        dk, dv = lax.fori_loop(j + 1, nq, lambda ib, c: step(ib, c, False), carry)
        dk_ref[...] = dk
        dv_ref[...] = dv

    full = pl.BlockSpec((s, HEAD_PAD), lambda h, j: (0, h))
    tile = pl.BlockSpec((t, HEAD_PAD), lambda h, j: (j, h))
    return pl.pallas_call(
        body, name="attn_bwd", grid=(MLA_HEADS, nq),
        in_specs=[full, tile, tile, full, full, full],
        out_specs=[full, tile, tile],
        out_shape=[jax.ShapeDtypeStruct((s, MLA_HEADS * HEAD_PAD), F32)] * 3,
        compiler_params=pltpu.CompilerParams(dimension_semantics=("arbitrary", "arbitrary"),
                                             vmem_limit_bytes=VMEM_LIMIT),
    )(q, k, v, do, lse, delta)


def _post_forward(o, s_out, gate, h0, memk, memv, w_oa, w_o, w_xq, w_xo, g1, b1, g2, b2):
    a_out = _mm(o, w_oa)
    sg_s = _sigmoid(gate[:, :1024])
    sg_a = _sigmoid(gate[:, 1024:])
    mixin = sg_s * s_out + sg_a * a_out
    r1 = DN_ALPHA * h0 + _mm(mixin, w_o)
    h1, xh1, rs1 = _ln_fwd(r1, g1, b1)
    qxb = _mm(h1, w_xq).astype(BF16)
    ps, oxs = [], []
    for hh in range(XATTN_HEADS):
        sl = slice(hh * XATTN_HD, (hh + 1) * XATTN_HD)
        sc = _mm_nt(qxb[:, sl], memk[:, sl]) * (XATTN_HD ** -0.5)
        e = jnp.exp(sc - jnp.max(sc, axis=1, keepdims=True))
        p = e / jnp.sum(e, axis=1, keepdims=True)
        ps.append(p)
        oxs.append(_mm(p, memv[:, sl]))
    ox = jnp.concatenate(oxs, axis=1)
    r2 = DN_ALPHA * h1 + _mm(ox, w_xo)
    return dict(a_out=a_out, sg_s=sg_s, sg_a=sg_a, mixin=mixin, h1=h1, xh1=xh1, rs1=rs1, qxb=qxb, ps=ps, ox=ox, r2=r2)


def _post_fwd(o, s_out, gate, h0, memk, memv, lns, w_oa, w_o, w_xq, w_xo):
    def body(o_ref, so_ref, gate_ref, h0_ref, mk_ref, mv_ref, g1, b1, g2, b2, woa, wo, wxq, wxo,
             r2_ref, mixin_ref, h1_ref, ox_ref):
        f = _post_forward(o_ref[...], so_ref[...], gate_ref[...], h0_ref[...], mk_ref[...], mv_ref[...],
                          woa[...], wo[...], wxq[...], wxo[...], g1[...], b1[...], g2[...], b2[...])
        r2_ref[...] = f["r2"]
        mixin_ref[...] = f["mixin"].astype(BF16)
        h1_ref[...] = f["h1"].astype(BF16)
        ox_ref[...] = f["ox"].astype(BF16)

    s = o.shape[0]
    return _row_call("post_fwd", body, s, min(ROW_TILE, s), [o, s_out, gate, h0], [memk, memv, *lns],
                     [w_oa, w_o, w_xq, w_xo], [(1024, F32), (1024, BF16), (1024, BF16), (1024, BF16)], [])


def _mlp(r2, target, g2, b2, g3, b3, w_up, w_down):
    n_chunk = MLP_HIDDEN // 1024

    def body(r2_ref, tgt_ref, g2_ref, b2_ref, g3_ref, b3_ref, wup, wdn,
             dh2_ref, h2b_ref, dup_ref, act_ref, dff_ref, loss_ref, dg3_ref, db3_ref, up_scr):
        @pl.when(pl.program_id(0) == 0)
        def _():
            loss_ref[...] = jnp.zeros_like(loss_ref)
            dg3_ref[...] = jnp.zeros_like(dg3_ref)
            db3_ref[...] = jnp.zeros_like(db3_ref)

        h2, _, _ = _ln_fwd(r2_ref[...], g2_ref[...], b2_ref[...])
        h2b = h2.astype(BF16)
        h2b_ref[...] = h2b
        ff = jnp.zeros(h2.shape, F32)
        for c in range(n_chunk):
            sl = slice(c * 1024, (c + 1) * 1024)
            a = jnp.maximum(_mm(h2b, wup[:, sl]), 0.0)
            up_scr[:, sl] = a
            actb = (a * a).astype(BF16)
            act_ref[:, sl] = actb
            ff = ff + _mm(actb, wdn[sl, :])
        h3, xh3, rs3 = _ln_fwd(DN_ALPHA * h2 + ff, g3_ref[...], b3_ref[...])
        err = h3 - tgt_ref[...]
        loss_ref[...] += 0.5 * jnp.sum(err * err) * (1.0 / D_MODEL)
        dr3, dg3, db3 = _ln_bwd(err * (1.0 / D_MODEL), xh3, rs3, g3_ref[...])
        dg3_ref[...] += dg3
        db3_ref[...] += db3
        dffb = dr3.astype(BF16)
        dff_ref[...] = dffb
        dh2 = DN_ALPHA * dr3
        for c in range(n_chunk):
            sl = slice(c * 1024, (c + 1) * 1024)
            dupb = (_mm_nt(dffb, wdn[sl, :]) * (2.0 * up_scr[:, sl])).astype(BF16)
            dup_ref[:, sl] = dupb
            dh2 = dh2 + _mm_nt(dupb, wup[:, sl])
        dh2_ref[...] = dh2

    s = r2.shape[0]
    ts = min(ROW_TILE, s)
    return _row_call("mlp", body, s, ts, [r2, target], [g2, b2, g3, b3], [w_up, w_down],
                     [(1024, F32), (1024, BF16), (MLP_HIDDEN, BF16), (MLP_HIDDEN, BF16), (1024, BF16)],
                     [(8, 128), (1, 1024), (1, 1024)], scratch=[pltpu.VMEM((ts, MLP_HIDDEN), F32)])


def _post_bwd(dh2, o, s_out, gate, h0, memk, memv, lns, w_oa, w_o, w_xq, w_xo):
    def body(dh2_ref, o_ref, so_ref, gate_ref, h0_ref, mk_ref, mv_ref, g1, b1, g2, b2, woa, wo, wxq, wxo,
             do_ref, dl_ref, dso_ref, dgate_ref, dh0_ref, daout_ref, dmix_ref, dqx_ref, dxa_ref,
             dmk_ref, dmv_ref, dg1_ref, db1_ref, dg2_ref, db2_ref):
        @pl.when(pl.program_id(0) == 0)
        def _():
            for r in (dmk_ref, dmv_ref, dg1_ref, db1_ref, dg2_ref, db2_ref):
                r[...] = jnp.zeros_like(r)

        o = o_ref[...]
        s_out = so_ref[...]
        memk, memv = mk_ref[...], mv_ref[...]
        f = _post_forward(o, s_out, gate_ref[...], h0_ref[...], memk, memv,
                          woa[...], wo[...], wxq[...], wxo[...], g1[...], b1[...], g2[...], b2[...])
        _, xh2, rs2 = _ln_fwd(f["r2"], g2[...], b2[...])
        dr2, dg2, db2 = _ln_bwd(dh2_ref[...], xh2, rs2, g2[...])
        dg2_ref[...] += dg2
        db2_ref[...] += db2
        dxab = dr2.astype(BF16)
        dxa_ref[...] = dxab
        dox = _mm_nt(dxab, wxo[...])
        dqs = []
        for hh in range(XATTN_HEADS):
            sl = slice(hh * XATTN_HD, (hh + 1) * XATTN_HD)
            p = f["ps"][hh]
            doxh = dox[:, sl].astype(BF16)
            dp = _mm_nt(doxh, memv[:, sl])
            ds = (p * (dp - jnp.sum(dp * p, axis=1, keepdims=True)) * (XATTN_HD ** -0.5)).astype(BF16)
            dqs.append(_mm(ds, memk[:, sl]))
            dmk_ref[:, sl] += _mm_tn(ds, f["qxb"][:, sl])
            dmv_ref[:, sl] += _mm_tn(p, doxh)
        dqxb = jnp.concatenate(dqs, axis=1).astype(BF16)
        dqx_ref[...] = dqxb
        dh1 = DN_ALPHA * dr2 + _mm_nt(dqxb, wxq[...])
        dr1, dg1, db1 = _ln_bwd(dh1, f["xh1"], f["rs1"], g1[...])
        dg1_ref[...] += dg1
        db1_ref[...] += db1
        dh0_ref[...] = DN_ALPHA * dr1
        dmixb = dr1.astype(BF16)
        dmix_ref[...] = dmixb
        dmixin = _mm_nt(dmixb, wo[...])
        sg_s, sg_a = f["sg_s"], f["sg_a"]
        dso_ref[...] = dmixin * sg_s
        daoutb = (dmixin * sg_a).astype(BF16)
        daout_ref[...] = daoutb
        dgate_ref[:, :1024] = dmixin * s_out * sg_s * (1.0 - sg_s)
        dgate_ref[:, 1024:] = dmixin * f["a_out"] * sg_a * (1.0 - sg_a)
        d_o = _mm_nt(daoutb, woa[...])
        do_ref[...] = d_o.astype(BF16)
        for hh in range(MLA_HEADS):
            sl = slice(hh * HEAD_PAD, (hh + 1) * HEAD_PAD)
            dl = jnp.sum(d_o[:, sl] * o[:, sl], axis=1, keepdims=True)
            dl_ref[:, sl] = jnp.broadcast_to(dl, (dl.shape[0], HEAD_PAD))

    s = o.shape[0]
    m = memk.shape[0]
    return _row_call(
        "post_bwd", body, s, min(ROW_TILE, s), [dh2, o, s_out, gate, h0], [memk, memv, *lns],
        [w_oa, w_o, w_xq, w_xo],
        [(1024, BF16), (1024, F32), (1024, F32), (2048, F32), (1024, F32),
         (1024, BF16), (1024, BF16), (1024, BF16), (1024, BF16)],
        [(m, 1024), (m, 1024), (1, 1024), (1, 1024), (1, 1024), (1, 1024)])


def _s5out_bwd(d_so, yl, z3, d_skip, w_c, w_glu):
    def body(dso_ref, yl_ref, z3_ref, d_ref, wc, wglu, gh_ref, dup_ref, yg_ref, dy12_ref, dyl_ref, dd_ref):
        @pl.when(pl.program_id(0) == 0)
        def _():
            dd_ref[...] = jnp.zeros_like(dd_ref)

        yl = yl_ref[...]
        ygb = _gelu(yl).astype(BF16)
        yg_ref[...] = ygb
        y12 = _mm(ygb, wglu[...])
        sg = _sigmoid(y12[:, 1024:])
        dso = dso_ref[...]
        dy12b = jnp.concatenate([dso * sg, dso * y12[:, :1024] * sg * (1.0 - sg)], axis=1).astype(BF16)
        dy12_ref[...] = dy12b
        dyl = _mm_nt(dy12b, wglu[...]) * _gelu_grad(yl)
        dylb = dyl.astype(BF16)
        dyl_ref[...] = dylb
        gh_ref[...] = _mm_nt(dylb, wc[...])
        dup_ref[...] = dyl * d_ref[...]
        dd_ref[...] += jnp.sum(dyl * z3_ref[:, :256], axis=0, keepdims=True)

    s = d_so.shape[0]
    return _row_call("s5out_bwd", body, s, min(ROW_TILE, s), [d_so, yl, z3], [d_skip], [w_c, w_glu],
                     [(2 * NS, F32), (256, F32), (256, BF16), (2048, BF16), (256, BF16)], [(1, 256)])


def _in_bwd(x, tab, z3, dq, dk, dv, lam, du_p, dgate, dh0p, ln_g, ln_b, qg, kvg, w_inx, w_b, w_q2, w_kv2):
    def body(x_ref, tab_ref, z3_ref, dq_ref, dk_ref, dv_ref, lam_ref, dup_ref, dgate_ref, dh0p_ref,
             lng_ref, lnb_ref, qg_ref, kvg_ref, winx, wb, wq2, wkv2,
             dx_ref, h0b_ref, dz_ref, ub_ref, cqn_ref, ckvn_ref, dq2_ref, dkv2_ref,
             dlng_ref, dlnb_ref, dqg_ref, dkvg_ref):
        @pl.when(pl.program_id(0) == 0)
        def _():
            for r in (dlng_ref, dlnb_ref, dqg_ref, dkvg_ref):
                r[...] = jnp.zeros_like(r)

        h0, xh0, rs0 = _ln_fwd(x_ref[...], lng_ref[...], lnb_ref[...])
        h0b_ref[...] = h0.astype(BF16)
        z3 = z3_ref[...]
        ub_ref[...] = z3[:, :256].astype(BF16)
        tab_v = tab_ref[...]
        cq1, cq2, ck1 = tab_v[:, :128], tab_v[:, 128:256], tab_v[:, 256:384]
        du = _mm_nt(lam_ref[...], wb[...]) + dup_ref[...]
        dq = dq_ref[...]
        dq2b = jnp.concatenate([dq * _tile_heads(cq1), dq * _tile_heads(cq2)], axis=1).astype(BF16)
        dq2_ref[...] = dq2b
        cqg, cqn, rq = _rms_fwd(z3[:, 256:512], qg_ref[...])
        cqn_ref[...] = cqg.astype(BF16)
        dcq, dqg = _rms_bwd(_mm_nt(dq2b, wq2[...]), cqn, rq, qg_ref[...])
        dqg_ref[...] += dqg
        dk = dk_ref[...]
        dkv2b = jnp.concatenate([dk, dv_ref[...]], axis=1).astype(BF16)
        dkv2_ref[...] = dkv2b
        ckvg, ckvn, rkv = _rms_fwd(z3[:, 512:768], kvg_ref[...])
        ckvn_ref[...] = ckvg.astype(BF16)
        dckv, dkvg = _rms_bwd(_mm_nt(dkv2b, wkv2[...]), ckvn, rkv, kvg_ref[...])
        dkvg_ref[...] += dkvg
        dkrp = dk[:, :HEAD_PAD]
        for hh in range(1, MLA_HEADS):
            dkrp = dkrp + dk[:, hh * HEAD_PAD:(hh + 1) * HEAD_PAD]
        dzb = jnp.concatenate([du, dcq, dckv, dkrp * ck1, dkrp * cq2, dgate_ref[...]], axis=1).astype(BF16)
        dz_ref[...] = dzb
        dh0 = _mm_nt(dzb, winx[...]) + dh0p_ref[...]
        dx, dg, db = _ln_bwd(dh0, xh0, rs0, lng_ref[...])
        dx_ref[...] = dx
        dlng_ref[...] += dg
        dlnb_ref[...] += db

    s = x.shape[0]
    return _row_call(
        "in_bwd", body, s, min(ROW_TILE, s), [x, tab, z3, dq, dk, dv, lam, du_p, dgate, dh0p],
        [ln_g, ln_b, qg, kvg], [w_inx, w_b, w_q2, w_kv2],
        [(1024, F32), (1024, BF16), (3072, BF16), (256, BF16), (256, BF16), (256, BF16), (2048, BF16), (2048, BF16)],
        [(1, 1024), (1, 1024), (1, 256), (1, 256)])


def _wgrad(name, xs, dy):
    s, k = xs.shape
    n = dy.shape[1]
    ts, tk, tn = min(s, 1024), min(k, 512), min(n, 1024)
    assert s % ts == 0 and k % tk == 0 and n % tn == 0, (name, s, k, n)

    def body(x_ref, dy_ref, o_ref):
        @pl.when(pl.program_id(2) == 0)
        def _():
            o_ref[...] = jnp.zeros_like(o_ref)

        o_ref[...] += _mm_tn(x_ref[...], dy_ref[...])

    return pl.pallas_call(
        body, name=name, grid=(k // tk, n // tn, s // ts),
        in_specs=[pl.BlockSpec((ts, tk), lambda a, b, c: (c, a)), pl.BlockSpec((ts, tn), lambda a, b, c: (c, b))],
        out_specs=pl.BlockSpec((tk, tn), lambda a, b, c: (a, b)),
        out_shape=jax.ShapeDtypeStruct((k, n), F32),
        compiler_params=pltpu.CompilerParams(dimension_semantics=("arbitrary", "arbitrary", "arbitrary"),
                                             vmem_limit_bytes=VMEM_LIMIT),
    )(xs, dy)


def _mesh_pos():
    x, y, c = lax.axis_index("x"), lax.axis_index("y"), lax.axis_index("c")
    return x, y, c


def _peer(x, y, c, k):
    px = 1 - x if k & 4 else x
    py = 1 - y if k & 2 else y
    pc = 1 - c if k & 1 else c
    return (px, py, pc), 4 * px + 2 * py + pc


def _all_gather(shard):
    r = shard.shape[0]

    def body(x_ref, out_ref, send_sems, recv_sems, local_sem):
        x, y, c = _mesh_pos()
        me = 4 * x + 2 * y + c
        mine = pltpu.make_async_copy(x_ref, out_ref.at[me], local_sem)
        mine.start()
        sends = []
        for k in range(1, N_DEV):
            peer, _ = _peer(x, y, c, k)
            cp = pltpu.make_async_remote_copy(src_ref=x_ref, dst_ref=out_ref.at[me], send_sem=send_sems.at[k - 1],
                                              recv_sem=recv_sems.at[k - 1], device_id=peer, device_id_type=MESH)
            cp.start()
            sends.append(cp)
        for k in range(1, N_DEV):
            peer, pid = _peer(x, y, c, k)
            pltpu.make_async_remote_copy(src_ref=x_ref, dst_ref=out_ref.at[pid], send_sem=send_sems.at[k - 1],
                                         recv_sem=recv_sems.at[k - 1], device_id=peer, device_id_type=MESH).wait_recv()
        for cp in sends:
            cp.wait_send()
        mine.wait()

    return pl.pallas_call(
        body, name="all_gather",
        in_specs=[pl.BlockSpec(memory_space=pl.ANY)],
        out_specs=pl.BlockSpec(memory_space=pl.ANY),
        out_shape=jax.ShapeDtypeStruct((N_DEV, r, LANES), shard.dtype),
        scratch_shapes=[pltpu.SemaphoreType.DMA((N_DEV - 1,)), pltpu.SemaphoreType.DMA((N_DEV - 1,)),
                        pltpu.SemaphoreType.DMA],
    )(shard)


def _exchange(big, small):
    r, rs = big.shape[1], small.shape[0]

    def body(big_ref, small_ref, rbig_ref, rsmall_ref, send_sems, recv_sems, local_sems):
        x, y, c = _mesh_pos()
        me = 4 * x + 2 * y + c
        own = [pltpu.make_async_copy(big_ref.at[me], rbig_ref.at[me], local_sems.at[0]),
               pltpu.make_async_copy(small_ref, rsmall_ref.at[me], local_sems.at[1])]
        for cp in own:
            cp.start()
        sends = []
        for k in range(1, N_DEV):
            peer, pid = _peer(x, y, c, k)
            for which, (src, dst) in enumerate(((big_ref.at[pid], rbig_ref.at[me]), (small_ref, rsmall_ref.at[me]))):
                idx = 2 * (k - 1) + which
                cp = pltpu.make_async_remote_copy(src_ref=src, dst_ref=dst, send_sem=send_sems.at[idx],
                                                  recv_sem=recv_sems.at[idx], device_id=peer, device_id_type=MESH)
                cp.start()
                sends.append(cp)
        for k in range(1, N_DEV):
            peer, pid = _peer(x, y, c, k)
            for which, (src, dst) in enumerate(((big_ref.at[pid], rbig_ref.at[pid]), (small_ref, rsmall_ref.at[pid]))):
                idx = 2 * (k - 1) + which
                pltpu.make_async_remote_copy(src_ref=src, dst_ref=dst, send_sem=send_sems.at[idx],
                                             recv_sem=recv_sems.at[idx], device_id=peer, device_id_type=MESH).wait_recv()
        for cp in sends:
            cp.wait_send()
        for cp in own:
            cp.wait()

    n_sem = 2 * (N_DEV - 1)
    return pl.pallas_call(
        body, name="grad_exchange",
        in_specs=[pl.BlockSpec(memory_space=pl.ANY)] * 2,
        out_specs=[pl.BlockSpec(memory_space=pl.ANY)] * 2,
        out_shape=[jax.ShapeDtypeStruct((N_DEV, r, LANES), big.dtype),
                   jax.ShapeDtypeStruct((N_DEV, rs, LANES), small.dtype)],
        scratch_shapes=[pltpu.SemaphoreType.DMA((n_sem,)), pltpu.SemaphoreType.DMA((n_sem,)),
                        pltpu.SemaphoreType.DMA((2,))],
    )(big, small)


def _adamw(name, parts, w, m, v, tr):
    r = w.shape[0]
    assert r % tr == 0
    c1 = 1.0 - ADAM_B1 ** ADAM_STEP
    c2 = 1.0 - ADAM_B2 ** ADAM_STEP

    def body(p_ref, w_ref, m_ref, v_ref, g_ref, d_ref, mo_ref, vo_ref):
        g = p_ref[0].astype(F32)
        for d in range(1, N_DEV):
            g = g + p_ref[d].astype(F32)
        g_ref[...] = g
        mn = ADAM_B1 * m_ref[...] + (1.0 - ADAM_B1) * g
        vn = ADAM_B2 * v_ref[...] + (1.0 - ADAM_B2) * (g * g)
        mo_ref[...] = mn
        vo_ref[...] = vn
        d_ref[...] = -ADAM_LR * ((mn / c1) / (jnp.sqrt(vn / c2) + ADAM_EPS) + ADAM_WD * w_ref[...])

    row = pl.BlockSpec((tr, LANES), lambda i: (i, 0))
    return pl.pallas_call(
        body, name=name, grid=(r // tr,),
        in_specs=[pl.BlockSpec((N_DEV, tr, LANES), lambda i: (0, i, 0)), row, row, row],
        out_specs=[row] * 4,
        out_shape=[jax.ShapeDtypeStruct((r, LANES), F32)] * 4,
        compiler_params=pltpu.CompilerParams(dimension_semantics=("arbitrary",), vmem_limit_bytes=VMEM_LIMIT),
    )(parts, w, m, v)


def _pack_rows(arrays, rows):
    flat = jnp.concatenate([a.reshape(-1) for a in arrays])
    return jnp.pad(flat, (0, rows * LANES - flat.shape[0])).reshape(rows, LANES)


def _shard_sizes():
    return [kk * nn // N_DEV for (_, _, kk, nn) in SHARDED]


def _unpack_full(g):
    flat = g.reshape(N_DEV, -1)
    out, off = {}, 0
    for (name, kind, kk, nn), sz in zip(SHARDED, _shard_sizes()):
        blk = flat[:, off:off + sz]
        off += sz
        if kind == "col":
            out[name] = blk.reshape(N_DEV, kk, nn // N_DEV).transpose(1, 0, 2).reshape(kk, nn)
        else:
            out[name] = blk.reshape(kk, nn)
    return out


def _pack_grads(grads, rows):
    parts = []
    for (name, kind, kk, nn) in SHARDED:
        gw = grads[name]
        if kind == "col":
            parts.append(gw.reshape(kk, N_DEV, nn // N_DEV).transpose(1, 0, 2).reshape(N_DEV, -1))
        else:
            parts.append(gw.reshape(N_DEV, -1))
    flat = jnp.concatenate(parts, axis=1)
    return jnp.pad(flat, ((0, 0), (0, rows * LANES - flat.shape[1]))).reshape(N_DEV, rows, LANES)


def _unpack_flat(flat2d, shapes):
    flat = flat2d.reshape(-1)
    out, off = [], 0
    for shp in shapes:
        sz = math.prod(shp)
        out.append(flat[off:off + sz].reshape(shp))
        off += sz
    return out


def _s5_mats(lam_re, lam_im, log_dt, b_re, b_im, c_re, c_im):
    lr = jnp.minimum(lam_re, S5_MAX_RE)
    li = lam_im
    dt = jnp.exp(log_dt)[:, None]
    mag = jnp.exp(lr * dt)
    ang = li * dt
    ab_re = mag * jnp.cos(ang)
    ab_im = mag * jnp.sin(ang)
    den = lr * lr + li * li
    nr = ab_re - 1.0
    f_re = ((nr * lr + ab_im * li) / den)[..., None]
    f_im = ((ab_im * lr - nr * li) / den)[..., None]
    bb_re = f_re * b_re - f_im * b_im
    bb_im = f_re * b_im + f_im * b_re
    eye = jnp.eye(S5_GROUPS, dtype=F32)
    a = jnp.stack([ab_re.reshape(-1), ab_im.reshape(-1)])
    wb = jnp.concatenate([jnp.einsum("gph,gk->ghkp", bb_re, eye).reshape(S5_WIDTH, NS),
                          jnp.einsum("gph,gk->ghkp", bb_im, eye).reshape(S5_WIDTH, NS)], axis=1)
    wc = jnp.concatenate([jnp.einsum("ghp,gk->gpkh", c_re, eye).reshape(NS, S5_WIDTH),
                          -jnp.einsum("ghp,gk->gpkh", c_im, eye).reshape(NS, S5_WIDTH)], axis=0)
    return a, wb, wc


def _power_table(a):
    ar, ai = a[0], a[1]
    rows_r, rows_i = [ar], [ai]
    for _ in range(7):
        pr, pi = rows_r[-1], rows_i[-1]
        rows_r.append(pr * ar - pi * ai)
        rows_i.append(pr * ai + pi * ar)
    return jnp.concatenate([jnp.stack(rows_r), jnp.stack(rows_i)], axis=1)


def _rope_table(positions):
    inv = ROPE_THETA ** (-jnp.arange(0, MLA_ROPE, 2, dtype=F32) / MLA_ROPE)
    ang = positions.astype(F32)[:, None] * inv
    cos, sin = jnp.cos(ang), jnp.sin(ang)
    s = positions.shape[0]
    one, zero, pad = jnp.ones((s, MLA_NOPE), F32), jnp.zeros((s, MLA_NOPE), F32), jnp.zeros((s, 32), F32)
    return jnp.concatenate([one, cos, cos, pad, zero, -sin, sin, pad, zero, cos, cos, pad], axis=1)


def _derived_weights(full):
    w_in = full["w_in"]
    k1, k2 = w_in[:, 768:784], w_in[:, 784:800]
    z64, z32 = jnp.zeros((1024, 64), BF16), jnp.zeros((1024, 32), BF16)
    w_inx = jnp.concatenate([w_in[:, :768], z64, k1, k2, z32, z64, k2, k1, z32, w_in[:, 800:]], axis=1)
    uq = full["w_uq"].reshape(256, MLA_HEADS, MLA_QK)
    nope, r1, r2 = uq[:, :, :64], uq[:, :, 64:80], uq[:, :, 80:]
    zq64, zq32 = jnp.zeros((256, MLA_HEADS, 64), BF16), jnp.zeros((256, MLA_HEADS, 32), BF16)
    w_q2 = jnp.concatenate([jnp.concatenate([nope, r1, r2, zq32], axis=2).reshape(256, 1024),
                            jnp.concatenate([zq64, r2, r1, zq32], axis=2).reshape(256, 1024)], axis=1)
    ukv = full["w_ukv"].reshape(256, MLA_HEADS, 128)
    w_kv2 = jnp.concatenate([jnp.concatenate([ukv[:, :, :64], zq64], axis=2).reshape(256, 1024),
                             jnp.concatenate([ukv[:, :, 64:], zq64], axis=2).reshape(256, 1024)], axis=1)
    oa = full["w_oa"].reshape(MLA_HEADS, MLA_V, 1024)
    w_oap = jnp.concatenate([oa, jnp.zeros_like(oa)], axis=1).reshape(1024, 1024)
    return w_inx, w_q2, w_kv2, w_oap


def _fold_grads(d_inx, d_q2, d_kv2, d_oap):
    d_k1 = d_inx[:, 832:848] + d_inx[:, 976:992]
    d_k2 = d_inx[:, 848:864] + d_inx[:, 960:976]
    d_w_in = jnp.concatenate([d_inx[:, :768], d_k1, d_k2, d_inx[:, 1024:]], axis=1)
    a = d_q2[:, :1024].reshape(256, MLA_HEADS, 128)
    b = d_q2[:, 1024:].reshape(256, MLA_HEADS, 128)
    d_w_uq = jnp.concatenate([a[:, :, :64], a[:, :, 64:80] + b[:, :, 80:96], a[:, :, 80:96] + b[:, :, 64:80]],
                             axis=2).reshape(256, MLA_HEADS * MLA_QK)
    kk = d_kv2[:, :1024].reshape(256, MLA_HEADS, 128)
    vv = d_kv2[:, 1024:].reshape(256, MLA_HEADS, 128)
    d_w_ukv = jnp.concatenate([kk[:, :, :64], vv[:, :, :64]], axis=2).reshape(256, 1024)
    d_w_oa = d_oap.reshape(MLA_HEADS, 128, 1024)[:, :64].reshape(512, 1024)
    return d_w_in, d_w_uq, d_w_ukv, d_w_oa


def kernel(x, mem, positions, ln_in_g, ln_in_b, w_in, s5_lam_re, s5_lam_im, s5_log_dt, s5_b_re, s5_b_im, s5_c_re, s5_c_im, s5_d, w_glu, q_norm_g, w_uq, kv_norm_g, w_ukv, w_oa, w_o, ln1_g, ln1_b, w_xq, w_xk, w_xv, w_xo, ln2_g, ln2_b, w_up, w_down, ln3_g, ln3_b, loss_target, m_ln_in_g, m_ln_in_b, m_w_in, m_s5_lam_re, m_s5_lam_im, m_s5_log_dt, m_s5_b_re, m_s5_b_im, m_s5_c_re, m_s5_c_im, m_s5_d, m_w_glu, m_q_norm_g, m_w_uq, m_kv_norm_g, m_w_ukv, m_w_oa, m_w_o, m_ln1_g, m_ln1_b, m_w_xq, m_w_xk, m_w_xv, m_w_xo, m_ln2_g, m_ln2_b, m_w_up, m_w_down, m_ln3_g, m_ln3_b, v_ln_in_g, v_ln_in_b, v_w_in, v_s5_lam_re, v_s5_lam_im, v_s5_log_dt, v_s5_b_re, v_s5_b_im, v_s5_c_re, v_s5_c_im, v_s5_d, v_w_glu, v_q_norm_g, v_w_uq, v_kv_norm_g, v_w_ukv, v_w_oa, v_w_o, v_ln1_g, v_ln1_b, v_w_xq, v_w_xk, v_w_xv, v_w_xo, v_ln2_g, v_ln2_b, v_w_up, v_w_down, v_ln3_g, v_ln3_b):
    args = dict(locals())
    wts = {n: args[n] for n in WEIGHTS}
    mom = {n: args["m_" + n] for n in WEIGHTS}
    vel = {n: args["v_" + n] for n in WEIGHTS}
    xs, mems, tgt = x[0], mem[0], loss_target[0]
    s = xs.shape[0]

    big_rows = _round_up(sum(_shard_sizes()), 16 * LANES) // LANES
    small_rows = _round_up(sum(math.prod(wts[n].shape) for n in SMALL), 8 * LANES) // LANES

    gathered = _all_gather(_pack_rows([wts[n].astype(BF16) for (n, _, _, _) in SHARDED], big_rows))
    full = _unpack_full(gathered)
    w_inx, w_q2, w_kv2, w_oap = _derived_weights(full)
    s5_args = (s5_lam_re[0], s5_lam_im[0], s5_log_dt[0], s5_b_re[0], s5_b_im[0], s5_c_re[0], s5_c_im[0])
    (a_mat, w_b, w_c), s5_vjp = jax.vjp(_s5_mats, *s5_args)
    w_bb, w_cb = w_b.astype(BF16), w_c.astype(BF16)
    pw = _power_table(a_mat)
    pwb = _power_table(a_mat * jnp.array([[1.0], [-1.0]], F32))[::-1]
    tab = _rope_table(positions[0])
    row = lambda a: a.reshape(1, -1)
    ln_g, ln_b = row(ln_in_g), row(ln_in_b)
    lns = [ln1_g, ln1_b, ln2_g, ln2_b]

    memk, memv = _mem_kv(mems, full["w_xk"], full["w_xv"])
    h0, z3, gate, bu, q, k, v = _in_fwd(xs, tab, ln_g, ln_b, q_norm_g, kv_norm_g, w_inx, w_bb, w_q2, w_kv2)
    h = _scan_fwd(pw, bu)
    yl, s_out = _s5out_fwd(h, z3, s5_d, w_cb, full["w_glu"])
    o, lse = _attn_fwd(q, k, v)
    r2, mixin_b, h1_b, ox_b = _post_fwd(o, s_out, gate, h0, memk, memv, lns,
                                        w_oap, full["w_o"], full["w_xq"], full["w_xo"])
    dh2, h2_b, dup_b, act_b, dff_b, loss_acc, d_ln3_g, d_ln3_b = _mlp(
        r2, tgt, ln2_g, ln2_b, ln3_g, ln3_b, full["w_up"], full["w_down"])
    (do_b, delta, d_so, dgate, dh0p, daout_b, dmix_b, dqx_b, dxa_b,
     dmk, dmv, d_ln1_g, d_ln1_b, d_ln2_g, d_ln2_b) = _post_bwd(
        dh2, o, s_out, gate, h0, memk, memv, lns, w_oap, full["w_o"], full["w_xq"], full["w_xo"])
    dq, dk, dv = _attn_bwd(q, k, v, do_b, lse, delta)
    gh, du_p, yg_b, dy12_b, dyl_b, d_s5_d = _s5out_bwd(d_so, yl, z3, s5_d, w_cb, full["w_glu"])
    lam, d_a = _scan_bwd(pwb, gh, h)
    (dx, h0_b, dz_b, u_b, cqn_b, ckvn_b, dq2_b, dkv2_b, d_ln_g, d_ln_b, d_qg, d_kvg) = _in_bwd(
        xs, tab, z3, dq, dk, dv, lam, du_p, dgate, dh0p, ln_g, ln_b, q_norm_g, kv_norm_g, w_inx, w_bb, w_q2, w_kv2)

    d_inx = _wgrad("wg_in", h0_b, dz_b)
    d_q2 = _wgrad("wg_q", cqn_b, dq2_b)
    d_kv2 = _wgrad("wg_kv", ckvn_b, dkv2_b)
    d_oap = _wgrad("wg_oa", o, daout_b)
    d_w_in, d_w_uq, d_w_ukv, d_w_oa = _fold_grads(d_inx, d_q2, d_kv2, d_oap)
    grads = {
        "w_in": d_w_in, "w_uq": d_w_uq, "w_ukv": d_w_ukv, "w_oa": d_w_oa,
        "w_glu": _wgrad("wg_glu", yg_b, dy12_b),
        "w_o": _wgrad("wg_o", mixin_b, dmix_b),
        "w_xq": _wgrad("wg_xq", h1_b, dqx_b),
        "w_xk": _wgrad("wg_xk", mems, dmk),
        "w_xv": _wgrad("wg_xv", mems, dmv),
        "w_xo": _wgrad("wg_xo", ox_b, dxa_b),
        "w_up": _wgrad("wg_up", h2_b, dup_b),
        "w_down": _wgrad("wg_down", act_b, dff_b),
    }
    d_wb = _wgrad("wg_s5b", u_b, lam)
    d_wc = _wgrad("wg_s5c", h, dyl_b)
    d_s5 = s5_vjp((d_a.reshape(2, NS), d_wb, d_wc))
    small_grads = {
        "ln_in_g": d_ln_g, "ln_in_b": d_ln_b, "s5_lam_re": d_s5[0], "s5_lam_im": d_s5[1], "s5_log_dt": d_s5[2],
        "s5_b_re": d_s5[3], "s5_b_im": d_s5[4], "s5_c_re": d_s5[5], "s5_c_im": d_s5[6], "s5_d": d_s5_d,
        "q_norm_g": d_qg, "kv_norm_g": d_kvg, "ln1_g": d_ln1_g, "ln1_b": d_ln1_b, "ln2_g": d_ln2_g,
        "ln2_b": d_ln2_b, "ln3_g": d_ln3_g, "ln3_b": d_ln3_b,
    }

    rbig, rsmall = _exchange(_pack_grads(grads, big_rows).astype(BF16),
                             _pack_rows([small_grads[n] for n in SMALL], small_rows))
    names_big = [n for (n, _, _, _) in SHARDED]
    tr = max(d for d in range(16, 385, 16) if big_rows % d == 0)
    big_out = _adamw("adamw_big", rbig, *[_pack_rows([t[n] for n in names_big], big_rows) for t in (wts, mom, vel)], tr)
    small_out = _adamw("adamw_small", rsmall, *[_pack_rows([t[n] for n in SMALL], small_rows) for t in (wts, mom, vel)],
                       small_rows)
    results = [dict(), dict(), dict(), dict()]
    for res, fb, fs in zip(results, big_out, small_out):
        for n, a in zip(names_big, _unpack_flat(fb, [wts[n].shape for n in names_big])):
            res[n] = a
        for n, a in zip(SMALL, _unpack_flat(fs, [wts[n].shape for n in SMALL])):
            res[n] = a

    loss = lax.psum(loss_acc[0, 0], AXES)
    return (loss, dx[None], *[res[n] for res in results for n in WEIGHTS])
```

```python
import math

import jax
import jax.numpy as jnp
from jax import lax
from jax.experimental import pallas as pl
from jax.experimental.pallas import tpu as pltpu

F32 = jnp.float32
BF16 = jnp.bfloat16

D_MODEL = 1024
S5_WIDTH = 256
S5_GROUP_CH = 16
S5_GROUPS = 16
S5_STATE = 64
NS = S5_GROUPS * S5_STATE
S5_MAX_RE = -1e-4
MLA_HEADS = 8
MLA_NOPE = 64
MLA_ROPE = 32
MLA_QK = 96
MLA_V = 64
HEAD_PAD = 128
ROPE_THETA = 10000.0
XATTN_HEADS = 4
XATTN_HD = 256
MLP_HIDDEN = 4096
LN_EPS = 1e-5
RMS_EPS = 1e-6
NEG_INF = -1e30
LOG2E = 1.4426950408889634
DN_ALPHA = 2.0 ** 0.25
ADAM_LR = 0.001
ADAM_B1 = 0.9
ADAM_B2 = 0.999
ADAM_EPS = 1e-08
ADAM_WD = 0.01
ADAM_STEP = 10

N_DEV = 8
AXES = ("x", "y", "c")
MESH = pl.DeviceIdType.MESH
LANES = 1024
VMEM_LIMIT = 60 * 1024 * 1024

ROW_TILE = 256
SCAN_TILE = 256
ATT_TILE = 512
ADAM_TILE = 128

SHARDED = (
    ("w_in", "col", 1024, 2848), ("w_glu", "col", 256, 2048), ("w_uq", "col", 256, 768),
    ("w_ukv", "col", 256, 1024), ("w_oa", "col", 512, 1024), ("w_o", "row", 1024, 1024),
    ("w_xq", "row", 1024, 1024), ("w_xk", "row", 1024, 1024), ("w_xv", "row", 1024, 1024),
    ("w_xo", "row", 1024, 1024), ("w_up", "col", 1024, 4096), ("w_down", "row", 4096, 1024),
)
SMALL = ("ln_in_g", "ln_in_b", "s5_lam_re", "s5_lam_im", "s5_log_dt", "s5_b_re", "s5_b_im", "s5_c_re",
         "s5_c_im", "s5_d", "q_norm_g", "kv_norm_g", "ln1_g", "ln1_b", "ln2_g", "ln2_b", "ln3_g", "ln3_b")
WEIGHTS = ("ln_in_g", "ln_in_b", "w_in", "s5_lam_re", "s5_lam_im", "s5_log_dt", "s5_b_re", "s5_b_im",
           "s5_c_re", "s5_c_im", "s5_d", "w_glu", "q_norm_g", "w_uq", "kv_norm_g", "w_ukv", "w_oa", "w_o",
           "ln1_g", "ln1_b", "w_xq", "w_xk", "w_xv", "w_xo", "ln2_g", "ln2_b", "w_up", "w_down", "ln3_g", "ln3_b")


def _round_up(n, m):
    return (n + m - 1) // m * m


def _bf(a):
    return a.astype(BF16)


def _mm(a, b):
    return jnp.dot(_bf(a), _bf(b), preferred_element_type=F32)


def _mm_nt(a, b):
    return lax.dot_general(_bf(a), _bf(b), (((1,), (1,)), ((), ())), preferred_element_type=F32)


def _mm_tn(a, b):
    return lax.dot_general(_bf(a), _bf(b), (((0,), (0,)), ((), ())), preferred_element_type=F32)


def _sigmoid(a):
    return 1.0 / (1.0 + jnp.exp(-a))


def _gelu(a):
    return 0.5 * a * (1.0 + lax.erf(a * (2.0 ** -0.5)))


def _gelu_grad(a):
    return 0.5 * (1.0 + lax.erf(a * (2.0 ** -0.5))) + a * jnp.exp(-0.5 * a * a) * (1.0 / math.sqrt(2.0 * math.pi))


def _ln_fwd(a, g, b):
    mu = jnp.mean(a, axis=-1, keepdims=True)
    ac = a - mu
    var = jnp.mean(ac * ac, axis=-1, keepdims=True)
    rstd = lax.rsqrt(var + LN_EPS)
    xhat = ac * rstd
    return xhat * g + b, xhat, rstd


def _ln_bwd(dy, xhat, rstd, g):
    dxh = dy * g
    m1 = jnp.mean(dxh, axis=-1, keepdims=True)
    m2 = jnp.mean(dxh * xhat, axis=-1, keepdims=True)
    dx = rstd * (dxh - m1 - xhat * m2)
    return dx, jnp.sum(dy * xhat, axis=0, keepdims=True), jnp.sum(dy, axis=0, keepdims=True)


def _rms_fwd(a, g):
    r = lax.rsqrt(jnp.mean(a * a, axis=-1, keepdims=True) + RMS_EPS)
    xn = a * r
    return xn * g, xn, r


def _rms_bwd(dy, xn, r, g):
    dxn = dy * g
    dx = r * (dxn - xn * jnp.mean(dxn * xn, axis=-1, keepdims=True))
    return dx, jnp.sum(dy * xn, axis=0, keepdims=True)


def _tile_heads(a):
    return jnp.concatenate([a] * MLA_HEADS, axis=1)


def _row_call(name, body, n_rows, ts, tiled_in, full_in, weights, tiled_out, acc_out, reverse=False, scratch=()):
    n = n_rows // ts
    assert n * ts == n_rows, (name, n_rows, ts)
    nt, nf, nw = len(tiled_in), len(full_in), len(weights)
    nto, nao = len(tiled_out), len(acc_out)
    if reverse:
        imap = lambda i: (n - 1 - i, 0)
    else:
        imap = lambda i: (i, 0)
    const = lambda i: (0, 0)

    def kern(*refs):
        ins = refs[:nt + nf]
        w_hbm = refs[nt + nf:nt + nf + nw]
        outs = refs[nt + nf + nw:nt + nf + nw + nto + nao]
        scr = refs[nt + nf + nw + nto + nao:]
        w_vmem = scr[:nw]
        extra = scr[nw + 1:] if nw else scr
        if nw:
            sem = scr[nw]

            @pl.when(pl.program_id(0) == 0)
            def _():
                cps = [pltpu.make_async_copy(w_hbm[k], w_vmem[k], sem.at[k]) for k in range(nw)]
                for cp in cps:
                    cp.start()
                for cp in cps:
                    cp.wait()
        body(*ins, *w_vmem, *outs, *extra)

    in_specs = [pl.BlockSpec((ts, a.shape[1]), imap) for a in tiled_in]
    in_specs += [pl.BlockSpec(a.shape, const) for a in full_in]
    in_specs += [pl.BlockSpec(memory_space=pl.ANY) for _ in weights]
    out_shape = [jax.ShapeDtypeStruct((n_rows, c), dt) for (c, dt) in tiled_out]
    out_shape += [jax.ShapeDtypeStruct(shp, F32) for shp in acc_out]
    out_specs = [pl.BlockSpec((ts, c), imap) for (c, dt) in tiled_out]
    out_specs += [pl.BlockSpec(shp, const) for shp in acc_out]
    scratch_shapes = [pltpu.VMEM(w.shape, w.dtype) for w in weights]
    if nw:
        scratch_shapes.append(pltpu.SemaphoreType.DMA((nw,)))
    scratch_shapes += list(scratch)
    return pl.pallas_call(
        kern, name=name, grid=(n,), in_specs=in_specs, out_specs=out_specs, out_shape=out_shape,
        scratch_shapes=scratch_shapes,
        compiler_params=pltpu.CompilerParams(dimension_semantics=("arbitrary",), vmem_limit_bytes=VMEM_LIMIT),
    )(*tiled_in, *full_in, *weights)


def _mem_kv(mem, w_xk, w_xv):
    m = mem.shape[0]

    def body(mem_ref, wk_ref, wv_ref, k_ref, v_ref):
        mb = mem_ref[...]
        k_ref[...] = _mm(mb, wk_ref[...]).astype(BF16)
        v_ref[...] = _mm(mb, wv_ref[...]).astype(BF16)

    return pl.pallas_call(
        body, name="mem_kv",
        out_shape=[jax.ShapeDtypeStruct((m, D_MODEL), BF16)] * 2,
        compiler_params=pltpu.CompilerParams(vmem_limit_bytes=VMEM_LIMIT),
    )(mem, w_xk, w_xv)


def _in_fwd(x, tab, ln_g, ln_b, qg, kvg, w_inx, w_b, w_q2, w_kv2):
    def body(x_ref, tab_ref, lng_ref, lnb_ref, qg_ref, kvg_ref, winx, wb, wq2, wkv2,
             h0_ref, z3_ref, gate_ref, bu_ref, q_ref, k_ref, v_ref):
        h0, _, _ = _ln_fwd(x_ref[...], lng_ref[...], lnb_ref[...])
        h0_ref[...] = h0
        z = _mm(h0, winx[...])
        z3_ref[...] = z[:, :768]
        gate_ref[...] = z[:, 1024:]
        bu_ref[...] = _mm(z[:, :256], wb[...])
        tab_v = tab_ref[...]
        cq1, cq2, ck1 = tab_v[:, :128], tab_v[:, 128:256], tab_v[:, 256:384]
        cqn, _, _ = _rms_fwd(z[:, 256:512], qg_ref[...])
        q2 = _mm(cqn, wq2[...])
        q_ref[...] = (q2[:, :1024] * _tile_heads(cq1) + q2[:, 1024:] * _tile_heads(cq2)).astype(BF16)
        ckvn, _, _ = _rms_fwd(z[:, 512:768], kvg_ref[...])
        kv2 = _mm(ckvn, wkv2[...])
        krp = z[:, 768:896] * ck1 + z[:, 896:1024] * cq2
        k_ref[...] = (kv2[:, :1024] + _tile_heads(krp)).astype(BF16)
        v_ref[...] = kv2[:, 1024:].astype(BF16)

    s = x.shape[0]
    return _row_call(
        "in_fwd", body, s, min(ROW_TILE, s), [x, tab], [ln_g, ln_b, qg, kvg], [w_inx, w_b, w_q2, w_kv2],
        [(1024, F32), (768, F32), (2048, F32), (2048, F32), (1024, BF16), (1024, BF16), (1024, BF16)], [])


def _scan_level(row, xr, xi, ar, ai, k, forward):
    if forward:
        keep = row >= k
        sr, si = pltpu.roll(xr, k, 0), pltpu.roll(xi, k, 0)
    else:
        keep = row < 8 - k
        sr, si = pltpu.roll(xr, 8 - k, 0), pltpu.roll(xi, 8 - k, 0)
    sr = jnp.where(keep, sr, 0.0)
    si = jnp.where(keep, si, 0.0)
    return xr + ar * sr - ai * si, xi + ar * si + ai * sr


def _scan_fwd(pw, bu):
    s = bu.shape[0]
    ts = min(SCAN_TILE, s)
    nblk = ts // 8

    def body(bu_ref, pw_ref, h_ref, carry):
        @pl.when(pl.program_id(0) == 0)
        def _():
            carry[...] = jnp.zeros_like(carry)

        row = lax.broadcasted_iota(jnp.int32, (8, NS), 0)
        a1 = (pw_ref[0:1, :NS], pw_ref[0:1, NS:])
        a2 = (pw_ref[1:2, :NS], pw_ref[1:2, NS:])
        a4 = (pw_ref[3:4, :NS], pw_ref[3:4, NS:])

        def blk(b, _):
            r0 = pl.multiple_of(b * 8, 8)
            xr = bu_ref[pl.ds(r0, 8), :NS]
            xi = bu_ref[pl.ds(r0, 8), NS:]
            xr, xi = _scan_level(row, xr, xi, a1[0], a1[1], 1, True)
            xr, xi = _scan_level(row, xr, xi, a2[0], a2[1], 2, True)
            xr, xi = _scan_level(row, xr, xi, a4[0], a4[1], 4, True)
            cr, ci = carry[7:8, :NS], carry[7:8, NS:]
            pr, pi = pw_ref[:, :NS], pw_ref[:, NS:]
            hr = xr + pr * cr - pi * ci
            hi = xi + pr * ci + pi * cr
            h_ref[pl.ds(r0, 8), :NS] = hr
            h_ref[pl.ds(r0, 8), NS:] = hi
            carry[:, :NS] = hr
            carry[:, NS:] = hi
            return 0

        lax.fori_loop(0, nblk, blk, 0)

    return _row_call("scan_fwd", body, s, ts, [bu], [pw], [], [(2 * NS, F32)], [],
                     scratch=[pltpu.VMEM((8, 2 * NS), F32)])[0]


def _scan_bwd(pwb, gh, h):
    s = gh.shape[0]
    ts = min(SCAN_TILE, s)
    nblk = ts // 8
    n_tiles = s // ts

    def body(g_ref, h_ref, pw_ref, lam_ref, da_ref, carry, acc):
        @pl.when(pl.program_id(0) == 0)
        def _():
            carry[...] = jnp.zeros_like(carry)
            acc[...] = jnp.zeros_like(acc)

        row = lax.broadcasted_iota(jnp.int32, (8, NS), 0)
        a1 = (pw_ref[7:8, :NS], pw_ref[7:8, NS:])
        a2 = (pw_ref[6:7, :NS], pw_ref[6:7, NS:])
        a4 = (pw_ref[4:5, :NS], pw_ref[4:5, NS:])

        def blk(bb, _):
            r0 = pl.multiple_of((nblk - 1 - bb) * 8, 8)
            xr = g_ref[pl.ds(r0, 8), :NS]
            xi = g_ref[pl.ds(r0, 8), NS:]
            xr, xi = _scan_level(row, xr, xi, a1[0], a1[1], 1, False)
            xr, xi = _scan_level(row, xr, xi, a2[0], a2[1], 2, False)
            xr, xi = _scan_level(row, xr, xi, a4[0], a4[1], 4, False)
            cr, ci = carry[0:1, :NS], carry[0:1, NS:]
            pr, pi = pw_ref[:, :NS], pw_ref[:, NS:]
            lr = xr + pr * cr - pi * ci
            li = xi + pr * ci + pi * cr
            lam_ref[pl.ds(r0, 8), :NS] = lr
            lam_ref[pl.ds(r0, 8), NS:] = li
            nr = jnp.where(row < 7, pltpu.roll(lr, 7, 0), cr)
            ni = jnp.where(row < 7, pltpu.roll(li, 7, 0), ci)
            hr = h_ref[pl.ds(r0, 8), :NS]
            hi = h_ref[pl.ds(r0, 8), NS:]
            acc[:, :NS] += nr * hr + ni * hi
            acc[:, NS:] += ni * hr - nr * hi
            carry[:, :NS] = lr
            carry[:, NS:] = li
            return 0

        lax.fori_loop(0, nblk, blk, 0)

        @pl.when(pl.program_id(0) == n_tiles - 1)
        def _():
            da_ref[...] = jnp.sum(acc[...], axis=0, keepdims=True)

    return _row_call("scan_bwd", body, s, ts, [gh, h], [pwb], [], [(2 * NS, F32)], [(1, 2 * NS)], reverse=True,
                     scratch=[pltpu.VMEM((8, 2 * NS), F32), pltpu.VMEM((8, 2 * NS), F32)])


def _s5out_fwd(h, z3, d_skip, w_c, w_glu):
    def body(h_ref, z3_ref, d_ref, wc, wglu, yl_ref, so_ref):
        yl = _mm(h_ref[...], wc[...]) + d_ref[...] * z3_ref[:, :256]
        yl_ref[...] = yl
        ygb = _gelu(yl).astype(BF16)
        for d in range(4):
            so_ref[:, d * 256:(d + 1) * 256] = _mm(ygb, wglu[d]) * _sigmoid(_mm(ygb, wglu[d + 4]))

    s = h.shape[0]
    return _row_call("s5out_fwd", body, s, min(ROW_TILE, s), [h, z3], [d_skip], [w_c, w_glu],
                     [(256, F32), (1024, F32)], [])


def _causal_mask_t(t):
    row = lax.broadcasted_iota(jnp.int32, (t, t), 0)
    col = lax.broadcasted_iota(jnp.int32, (t, t), 1)
    return row <= col


def _attn_fwd(q, k, vt):
    s = q.shape[0]
    t = min(ATT_TILE, s)
    nq = s // t
    scale = MLA_QK ** -0.5
    c2 = scale * LOG2E

    def body(q_ref, k_ref, vt_ref, o_ref, lse_ref):
        i = pl.program_id(1)
        qb = q_ref[...]

        def step(kb, state, masked):
            m, l, acc = state
            r0 = pl.multiple_of(kb * t, t)
            st = _mm_nt(k_ref[pl.ds(r0, t), :], qb)
            if masked:
                st = jnp.where(_causal_mask_t(t), st, NEG_INF)
            m_new = jnp.maximum(m, jnp.max(st, axis=0, keepdims=True))
            p = jnp.exp2((st - m_new) * c2)
            a = jnp.exp2((m - m_new) * c2)
            l = a * l + jnp.sum(p, axis=0, keepdims=True)
            acc = a * acc + _mm(vt_ref[:, pl.ds(r0, t)], p)
            return m_new, l, acc

        init = (jnp.full((1, t), NEG_INF, F32), jnp.zeros((1, t), F32), jnp.zeros((HEAD_PAD, t), F32))
        state = lax.fori_loop(0, i, lambda kb, c: step(kb, c, False), init)
        m, l, acc = step(i, state, True)
        o_ref[...] = (acc / l).T
        lse_ref[...] = m * scale + jnp.log(l)

    return pl.pallas_call(
        body, name="attn_fwd", grid=(MLA_HEADS, nq),
        in_specs=[pl.BlockSpec((t, HEAD_PAD), lambda h, i: (i, h)),
                  pl.BlockSpec((s, HEAD_PAD), lambda h, i: (0, h)),
                  pl.BlockSpec((HEAD_PAD, s), lambda h, i: (h, 0))],
        out_specs=[pl.BlockSpec((t, HEAD_PAD), lambda h, i: (i, h)),
                   pl.BlockSpec((None, 1, t), lambda h, i: (h, 0, i))],
        out_shape=[jax.ShapeDtypeStruct((s, MLA_HEADS * HEAD_PAD), F32),
                   jax.ShapeDtypeStruct((MLA_HEADS, 1, s), F32)],
        compiler_params=pltpu.CompilerParams(dimension_semantics=("arbitrary", "arbitrary"),
                                             vmem_limit_bytes=VMEM_LIMIT),
    )(q, k, vt)


def _attn_bwd(q, k, v, kt, do, lse_t, dl_t):
    s = q.shape[0]
    t = min(ATT_TILE, s)
    nq = s // t
    scale = MLA_QK ** -0.5
    c2 = scale * LOG2E

    def body(q_ref, k_ref, v_ref, kt_ref, do_ref, lse_ref, dl_ref, dq_ref, dk_ref, dv_ref, dqt):
        j = pl.program_id(1)

        @pl.when(j == 0)
        def _():
            dqt[...] = jnp.zeros_like(dqt)

        kk = k_ref[...]
        vv = v_ref[...]
        ktb = kt_ref[...]

        def step(ib, carry, masked):
            dk, dv = carry
            r0 = pl.multiple_of(ib * t, t)
            qq = q_ref[pl.ds(r0, t), :]
            dd = do_ref[pl.ds(r0, t), :]
            ls = lse_ref[:, pl.ds(r0, t)] * LOG2E
            dl = dl_ref[:, pl.ds(r0, t)]
            pt = jnp.exp2(_mm_nt(kk, qq) * c2 - ls)
            if masked:
                pt = jnp.where(_causal_mask_t(t), pt, 0.0)
            dv = dv + _mm(pt, dd)
            dst = (pt * (_mm_nt(vv, dd) - dl)).astype(BF16)
            dk = dk + _mm(dst, qq)
            dqt[:, pl.ds(r0, t)] += _mm(ktb, dst)
            return dk, dv

        zero = jnp.zeros((t, HEAD_PAD), F32)
        carry = step(j, (zero, zero), True)
        dk, dv = lax.fori_loop(j + 1, nq, lambda ib, c: step(ib, c, False), carry)
        dk_ref[...] = dk * scale
        dv_ref[...] = dv

        @pl.when(j == nq - 1)
        def _():
            for cc in range(nq):
                dq_ref[cc * t:(cc + 1) * t, :] = dqt[:, cc * t:(cc + 1) * t].T * scale

    full = pl.BlockSpec((s, HEAD_PAD), lambda h, j: (0, h))
    tile = pl.BlockSpec((t, HEAD_PAD), lambda h, j: (j, h))
    stat = pl.BlockSpec((None, 1, s), lambda h, j: (h, 0, 0))
    return pl.pallas_call(
        body, name="attn_bwd", grid=(MLA_HEADS, nq),
        in_specs=[full, tile, tile, pl.BlockSpec((HEAD_PAD, t), lambda h, j: (h, j)), full, stat, stat],
        out_specs=[full, tile, tile],
        out_shape=[jax.ShapeDtypeStruct((s, MLA_HEADS * HEAD_PAD), F32)] * 3,
        scratch_shapes=[pltpu.VMEM((HEAD_PAD, s), F32)],
        compiler_params=pltpu.CompilerParams(dimension_semantics=("arbitrary", "arbitrary"),
                                             vmem_limit_bytes=VMEM_LIMIT),
    )(q, k, v, kt, do, lse_t, dl_t)


def _post_forward(o, s_out, gate, h0, memk, memv, w_oa, w_o, w_xq, w_xo, g1, b1, g2, b2):
    a_out = _mm(o, w_oa)
    sg_s = _sigmoid(gate[:, :1024])
    sg_a = _sigmoid(gate[:, 1024:])
    mixin = sg_s * s_out + sg_a * a_out
    r1 = DN_ALPHA * h0 + _mm(mixin, w_o)
    h1, xh1, rs1 = _ln_fwd(r1, g1, b1)
    qxb = _mm(h1, w_xq).astype(BF16)
    ps, oxs = [], []
    for hh in range(XATTN_HEADS):
        sl = slice(hh * XATTN_HD, (hh + 1) * XATTN_HD)
        sc = _mm_nt(qxb[:, sl], memk[:, sl]) * (XATTN_HD ** -0.5)
        e = jnp.exp(sc - jnp.max(sc, axis=1, keepdims=True))
        p = e / jnp.sum(e, axis=1, keepdims=True)
        ps.append(p)
        oxs.append(_mm(p, memv[:, sl]))
    ox = jnp.concatenate(oxs, axis=1)
    r2 = DN_ALPHA * h1 + _mm(ox, w_xo)
    return dict(a_out=a_out, sg_s=sg_s, sg_a=sg_a, mixin=mixin, h1=h1, xh1=xh1, rs1=rs1, qxb=qxb, ps=ps, ox=ox, r2=r2)


def _post_fwd(o, s_out, gate, h0, memk, memv, lns, w_oa, w_o, w_xq, w_xo):
    def body(o_ref, so_ref, gate_ref, h0_ref, mk_ref, mv_ref, g1, b1, g2, b2, woa, wo, wxq, wxo,
             r2_ref, mixin_ref, h1_ref, ox_ref):
        f = _post_forward(o_ref[...], so_ref[...], gate_ref[...], h0_ref[...], mk_ref[...], mv_ref[...],
                          woa[...], wo[...], wxq[...], wxo[...], g1[...], b1[...], g2[...], b2[...])
        r2_ref[...] = f["r2"]
        mixin_ref[...] = f["mixin"].astype(BF16)
        h1_ref[...] = f["h1"].astype(BF16)
        ox_ref[...] = f["ox"].astype(BF16)

    s = o.shape[0]
    return _row_call("post_fwd", body, s, min(ROW_TILE, s), [o, s_out, gate, h0], [memk, memv, *lns],
                     [w_oa, w_o, w_xq, w_xo], [(1024, F32), (1024, BF16), (1024, BF16), (1024, BF16)], [])


def _mlp(r2, target, g2, b2, g3, b3, w_up, w_down):
    n_chunk, cw = N_DEV, MLP_HIDDEN // N_DEV

    def body(r2_ref, tgt_ref, g2_ref, b2_ref, g3_ref, b3_ref, wup, wdn,
             dh2_ref, h2b_ref, dup_ref, act_ref, dff_ref, loss_ref, dg3_ref, db3_ref, up_scr):
        @pl.when(pl.program_id(0) == 0)
        def _():
            loss_ref[...] = jnp.zeros_like(loss_ref)
            dg3_ref[...] = jnp.zeros_like(dg3_ref)
            db3_ref[...] = jnp.zeros_like(db3_ref)

        h2, _, _ = _ln_fwd(r2_ref[...], g2_ref[...], b2_ref[...])
        h2b = h2.astype(BF16)
        h2b_ref[...] = h2b
        ff = jnp.zeros(h2.shape, F32)
        for c in range(n_chunk):
            sl = slice(c * cw, (c + 1) * cw)
            a = jnp.maximum(_mm(h2b, wup[c]), 0.0)
            up_scr[:, sl] = a
            actb = (a * a).astype(BF16)
            act_ref[:, sl] = actb
            ff = ff + _mm(actb, wdn[sl, :])
        h3, xh3, rs3 = _ln_fwd(DN_ALPHA * h2 + ff, g3_ref[...], b3_ref[...])
        err = h3 - tgt_ref[...]
        loss_ref[...] += 0.5 * jnp.sum(err * err) * (1.0 / D_MODEL)
        dr3, dg3, db3 = _ln_bwd(err * (1.0 / D_MODEL), xh3, rs3, g3_ref[...])
        dg3_ref[...] += dg3
        db3_ref[...] += db3
        dffb = dr3.astype(BF16)
        dff_ref[...] = dffb
        dh2 = DN_ALPHA * dr3
        for c in range(n_chunk):
            sl = slice(c * cw, (c + 1) * cw)
            dupb = (_mm_nt(dffb, wdn[sl, :]) * (2.0 * up_scr[:, sl])).astype(BF16)
            dup_ref[:, sl] = dupb
            dh2 = dh2 + _mm_nt(dupb, wup[c])
        dh2_ref[...] = dh2

    s = r2.shape[0]
    ts = min(ROW_TILE, s)
    return _row_call("mlp", body, s, ts, [r2, target], [g2, b2, g3, b3], [w_up, w_down],
                     [(1024, F32), (1024, BF16), (MLP_HIDDEN, BF16), (MLP_HIDDEN, BF16), (1024, BF16)],
                     [(8, 128), (1, 1024), (1, 1024)], scratch=[pltpu.VMEM((ts, MLP_HIDDEN), F32)])


def _post_bwd(dh2, o, s_out, gate, h0, memk, memv, lns, w_oa, w_o, w_xq, w_xo):
    def body(dh2_ref, o_ref, so_ref, gate_ref, h0_ref, mk_ref, mv_ref, g1, b1, g2, b2, woa, wo, wxq, wxo,
             do_ref, dl_ref, dso_ref, dgate_ref, dh0_ref, daout_ref, dmix_ref, dqx_ref, dxa_ref,
             dmk_ref, dmv_ref, dg1_ref, db1_ref, dg2_ref, db2_ref):
        @pl.when(pl.program_id(0) == 0)
        def _():
            for r in (dmk_ref, dmv_ref, dg1_ref, db1_ref, dg2_ref, db2_ref):
                r[...] = jnp.zeros_like(r)

        o = o_ref[...]
        s_out = so_ref[...]
        memk, memv = mk_ref[...], mv_ref[...]
        f = _post_forward(o, s_out, gate_ref[...], h0_ref[...], memk, memv,
                          woa[...], wo[...], wxq[...], wxo[...], g1[...], b1[...], g2[...], b2[...])
        _, xh2, rs2 = _ln_fwd(f["r2"], g2[...], b2[...])
        dr2, dg2, db2 = _ln_bwd(dh2_ref[...], xh2, rs2, g2[...])
        dg2_ref[...] += dg2
        db2_ref[...] += db2
        dxab = dr2.astype(BF16)
        dxa_ref[...] = dxab
        dox = _mm_nt(dxab, wxo[...])
        dqs = []
        for hh in range(XATTN_HEADS):
            sl = slice(hh * XATTN_HD, (hh + 1) * XATTN_HD)
            p = f["ps"][hh]
            doxh = dox[:, sl].astype(BF16)
            dp = _mm_nt(doxh, memv[:, sl])
            ds = (p * (dp - jnp.sum(dp * p, axis=1, keepdims=True)) * (XATTN_HD ** -0.5)).astype(BF16)
            dqs.append(_mm(ds, memk[:, sl]))
            dmk_ref[:, sl] += _mm_tn(ds, f["qxb"][:, sl])
            dmv_ref[:, sl] += _mm_tn(p, doxh)
        dqxb = jnp.concatenate(dqs, axis=1).astype(BF16)
        dqx_ref[...] = dqxb
        dh1 = DN_ALPHA * dr2 + _mm_nt(dqxb, wxq[...])
        dr1, dg1, db1 = _ln_bwd(dh1, f["xh1"], f["rs1"], g1[...])
        dg1_ref[...] += dg1
        db1_ref[...] += db1
        dh0_ref[...] = DN_ALPHA * dr1
        dmixb = dr1.astype(BF16)
        dmix_ref[...] = dmixb
        dmixin = _mm_nt(dmixb, wo[...])
        sg_s, sg_a = f["sg_s"], f["sg_a"]
        dso_ref[...] = dmixin * sg_s
        daoutb = (dmixin * sg_a).astype(BF16)
        daout_ref[...] = daoutb
        dgate_ref[:, :1024] = dmixin * s_out * sg_s * (1.0 - sg_s)
        dgate_ref[:, 1024:] = dmixin * f["a_out"] * sg_a * (1.0 - sg_a)
        d_o = _mm_nt(daoutb, woa[...])
        do_ref[...] = d_o.astype(BF16)
        lane = lax.broadcasted_iota(jnp.int32, (o.shape[0], HEAD_PAD), 1)
        dlc = jnp.zeros((o.shape[0], HEAD_PAD), F32)
        for hh in range(MLA_HEADS):
            sl = slice(hh * HEAD_PAD, (hh + 1) * HEAD_PAD)
            dl = jnp.sum(d_o[:, sl] * o[:, sl], axis=1, keepdims=True)
            dlc = dlc + jnp.where(lane == hh, dl, 0.0)
        dl_ref[...] = dlc

    s = o.shape[0]
    m = memk.shape[0]
    return _row_call(
        "post_bwd", body, s, min(ROW_TILE, s), [dh2, o, s_out, gate, h0], [memk, memv, *lns],
        [w_oa, w_o, w_xq, w_xo],
        [(1024, BF16), (HEAD_PAD, F32), (1024, F32), (2048, F32), (1024, F32),
         (1024, BF16), (1024, BF16), (1024, BF16), (1024, BF16)],
        [(m, 1024), (m, 1024), (1, 1024), (1, 1024), (1, 1024), (1, 1024)])


def _s5out_bwd(d_so, yl, z3, d_skip, w_c, w_glu):
    def body(dso_ref, yl_ref, z3_ref, d_ref, wc, wglu, gh_ref, dup_ref, yg_ref, dy12_ref, dyl_ref, dd_ref):
        @pl.when(pl.program_id(0) == 0)
        def _():
            dd_ref[...] = jnp.zeros_like(dd_ref)

        yl = yl_ref[...]
        ygb = _gelu(yl).astype(BF16)
        yg_ref[...] = ygb
        dyg = jnp.zeros(yl.shape, F32)
        for d in range(4):
            sl = slice(d * 256, (d + 1) * 256)
            y1 = _mm(ygb, wglu[d])
            sg = _sigmoid(_mm(ygb, wglu[d + 4]))
            dso = dso_ref[:, sl]
            dy1b = (dso * sg).astype(BF16)
            dy2b = (dso * y1 * sg * (1.0 - sg)).astype(BF16)
            dy12_ref[:, sl] = dy1b
            dy12_ref[:, 1024 + d * 256:1024 + (d + 1) * 256] = dy2b
            dyg = dyg + _mm_nt(dy1b, wglu[d]) + _mm_nt(dy2b, wglu[d + 4])
        dyl = dyg * _gelu_grad(yl)
        dylb = dyl.astype(BF16)
        dyl_ref[...] = dylb
        gh_ref[...] = _mm_nt(dylb, wc[...])
        dup_ref[...] = dyl * d_ref[...]
        dd_ref[...] += jnp.sum(dyl * z3_ref[:, :256], axis=0, keepdims=True)

    s = d_so.shape[0]
    return _row_call("s5out_bwd", body, s, min(ROW_TILE, s), [d_so, yl, z3], [d_skip], [w_c, w_glu],
                     [(2 * NS, F32), (256, F32), (256, BF16), (2048, BF16), (256, BF16)], [(1, 256)])


def _in_bwd(x, tab, z3, dq, dk, dv, lam, du_p, dgate, dh0p, ln_g, ln_b, qg, kvg, w_inx, w_b, w_q2, w_kv2):
    def body(x_ref, tab_ref, z3_ref, dq_ref, dk_ref, dv_ref, lam_ref, dup_ref, dgate_ref, dh0p_ref,
             lng_ref, lnb_ref, qg_ref, kvg_ref, winx, wb, wq2, wkv2,
             dx_ref, h0b_ref, dz_ref, ub_ref, cqn_ref, ckvn_ref, dq2_ref, dkv2_ref,
             dlng_ref, dlnb_ref, dqg_ref, dkvg_ref):
        @pl.when(pl.program_id(0) == 0)
        def _():
            for r in (dlng_ref, dlnb_ref, dqg_ref, dkvg_ref):
                r[...] = jnp.zeros_like(r)

        h0, xh0, rs0 = _ln_fwd(x_ref[...], lng_ref[...], lnb_ref[...])
        h0b_ref[...] = h0.astype(BF16)
        z3 = z3_ref[...]
        ub_ref[...] = z3[:, :256].astype(BF16)
        tab_v = tab_ref[...]
        cq1, cq2, ck1 = tab_v[:, :128], tab_v[:, 128:256], tab_v[:, 256:384]
        du = _mm_nt(lam_ref[...], wb[...]) + dup_ref[...]
        dq = dq_ref[...]
        dq2b = jnp.concatenate([dq * _tile_heads(cq1), dq * _tile_heads(cq2)], axis=1).astype(BF16)
        dq2_ref[...] = dq2b
        cqg, cqn, rq = _rms_fwd(z3[:, 256:512], qg_ref[...])
        cqn_ref[...] = cqg.astype(BF16)
        dcq, dqg = _rms_bwd(_mm_nt(dq2b, wq2[...]), cqn, rq, qg_ref[...])
        dqg_ref[...] += dqg
        dk = dk_ref[...]
        dkv2b = jnp.concatenate([dk, dv_ref[...]], axis=1).astype(BF16)
        dkv2_ref[...] = dkv2b
        ckvg, ckvn, rkv = _rms_fwd(z3[:, 512:768], kvg_ref[...])
        ckvn_ref[...] = ckvg.astype(BF16)
        dckv, dkvg = _rms_bwd(_mm_nt(dkv2b, wkv2[...]), ckvn, rkv, kvg_ref[...])
        dkvg_ref[...] += dkvg
        dkrp = dk[:, :HEAD_PAD]
        for hh in range(1, MLA_HEADS):
            dkrp = dkrp + dk[:, hh * HEAD_PAD:(hh + 1) * HEAD_PAD]
        dzb = jnp.concatenate([du, dcq, dckv, dkrp * ck1, dkrp * cq2, dgate_ref[...]], axis=1).astype(BF16)
        dz_ref[...] = dzb
        dh0 = _mm_nt(dzb, winx[...]) + dh0p_ref[...]
        dx, dg, db = _ln_bwd(dh0, xh0, rs0, lng_ref[...])
        dx_ref[...] = dx
        dlng_ref[...] += dg
        dlnb_ref[...] += db

    s = x.shape[0]
    return _row_call(
        "in_bwd", body, s, min(ROW_TILE, s), [x, tab, z3, dq, dk, dv, lam, du_p, dgate, dh0p],
        [ln_g, ln_b, qg, kvg], [w_inx, w_b, w_q2, w_kv2],
        [(1024, F32), (1024, BF16), (3072, BF16), (256, BF16), (256, BF16), (256, BF16), (2048, BF16), (2048, BF16)],
        [(1, 1024), (1, 1024), (1, 256), (1, 256)])


def _wgrad(name, xs, dy, out_dtype=F32, blocks=1):
    s, k = xs.shape
    n = dy.shape[1]
    nb = n // blocks
    ts, tk, tn = min(s, 1024), min(k, 512), min(nb, 1024)
    assert s % ts == 0 and k % tk == 0 and nb % tn == 0 and (blocks == 1 or tn == nb), (name, s, k, n)
    last = s // ts - 1

    def body(x_ref, dy_ref, o_ref, acc):
        @pl.when(pl.program_id(2) == 0)
        def _():
            acc[...] = jnp.zeros_like(acc)

        acc[...] += _mm_tn(x_ref[...], dy_ref[...])

        @pl.when(pl.program_id(2) == last)
        def _():
            o_ref[...] = acc[...].astype(out_dtype)

    if blocks == 1:
        out_shape = jax.ShapeDtypeStruct((k, n), out_dtype)
        out_spec = pl.BlockSpec((tk, tn), lambda a, b, c: (a, b))
    else:
        out_shape = jax.ShapeDtypeStruct((blocks, k, nb), out_dtype)
        out_spec = pl.BlockSpec((None, tk, nb), lambda a, b, c: (b, a, 0))
    return pl.pallas_call(
        body, name=name, grid=(k // tk, n // tn, s // ts),
        in_specs=[pl.BlockSpec((ts, tk), lambda a, b, c: (c, a)), pl.BlockSpec((ts, tn), lambda a, b, c: (c, b))],
        out_specs=out_spec, out_shape=out_shape, scratch_shapes=[pltpu.VMEM((tk, tn), F32)],
        compiler_params=pltpu.CompilerParams(dimension_semantics=("arbitrary", "arbitrary", "arbitrary"),
                                             vmem_limit_bytes=VMEM_LIMIT),
    )(xs, dy)


def _mesh_pos():
    x, y, c = lax.axis_index("x"), lax.axis_index("y"), lax.axis_index("c")
    return x, y, c


def _peer(x, y, c, k):
    px = 1 - x if k & 4 else x
    py = 1 - y if k & 2 else y
    pc = 1 - c if k & 1 else c
    return (px, py, pc), 4 * px + 2 * py + pc


def _all_to_all(name, arrays, gather):
    n = len(arrays)

    def body(*refs):
        src, dst = refs[:n], refs[n:2 * n]
        send_sems, recv_sems, local_sems = refs[2 * n:]
        x, y, c = _mesh_pos()
        me = 4 * x + 2 * y + c

        def block(i, d):
            return src[i] if gather else src[i].at[d]

        own = [pltpu.make_async_copy(block(i, me), dst[i].at[me], local_sems.at[i]) for i in range(n)]
        for cp in own:
            cp.start()
        sends = []
        for k in range(1, N_DEV):
            peer, pid = _peer(x, y, c, k)
            for i in range(n):
                idx = (k - 1) * n + i
                cp = pltpu.make_async_remote_copy(src_ref=block(i, pid), dst_ref=dst[i].at[me],
                                                  send_sem=send_sems.at[idx], recv_sem=recv_sems.at[idx],
                                                  device_id=peer, device_id_type=MESH)
                cp.start()
                sends.append(cp)
        for k in range(1, N_DEV):
            peer, pid = _peer(x, y, c, k)
            for i in range(n):
                idx = (k - 1) * n + i
                pltpu.make_async_remote_copy(src_ref=block(i, pid), dst_ref=dst[i].at[pid],
                                             send_sem=send_sems.at[idx], recv_sem=recv_sems.at[idx],
                                             device_id=peer, device_id_type=MESH).wait_recv()
        for cp in sends:
            cp.wait_send()
        for cp in own:
            cp.wait()

    n_sem = n * (N_DEV - 1)
    return pl.pallas_call(
        body, name=name,
        in_specs=[pl.BlockSpec(memory_space=pl.ANY)] * n,
        out_specs=[pl.BlockSpec(memory_space=pl.ANY)] * n,
        out_shape=[jax.ShapeDtypeStruct((N_DEV,) + a.shape[-2:], a.dtype) for a in arrays],
        scratch_shapes=[pltpu.SemaphoreType.DMA((n_sem,)), pltpu.SemaphoreType.DMA((n_sem,)),
                        pltpu.SemaphoreType.DMA((n,))],
    )(*arrays)


def _adamw(name, parts, w, m, v):
    a_rows, b_cols = w.shape
    ta = min(a_rows, ADAM_TILE)
    assert a_rows % ta == 0
    c1 = 1.0 - ADAM_B1 ** ADAM_STEP
    c2 = 1.0 - ADAM_B2 ** ADAM_STEP

    def body(p_ref, w_ref, m_ref, v_ref, g_ref, d_ref, mo_ref, vo_ref):
        g = p_ref[0].astype(F32)
        for d in range(1, N_DEV):
            g = g + p_ref[d].astype(F32)
        g_ref[...] = g
        mn = ADAM_B1 * m_ref[...] + (1.0 - ADAM_B1) * g
        vn = ADAM_B2 * v_ref[...] + (1.0 - ADAM_B2) * (g * g)
        mo_ref[...] = mn
        vo_ref[...] = vn
        d_ref[...] = -ADAM_LR * ((mn / c1) / (jnp.sqrt(vn / c2) + ADAM_EPS) + ADAM_WD * w_ref[...])

    row = pl.BlockSpec((ta, b_cols), lambda i: (i, 0))
    return pl.pallas_call(
        body, name=name, grid=(a_rows // ta,),
        in_specs=[pl.BlockSpec((N_DEV, ta, b_cols), lambda i: (0, i, 0)), row, row, row],
        out_specs=[row] * 4,
        out_shape=[jax.ShapeDtypeStruct((a_rows, b_cols), F32)] * 4,
        compiler_params=pltpu.CompilerParams(dimension_semantics=("arbitrary",), vmem_limit_bytes=VMEM_LIMIT),
    )(parts, w, m, v)


def _pack_rows(arrays, rows):
    flat = jnp.concatenate([a.reshape(-1) for a in arrays])
    return jnp.pad(flat, (0, rows * LANES - flat.shape[0])).reshape(rows, LANES)


def _cols_from_blocks(g):
    return g.transpose(1, 0, 2).reshape(g.shape[1], N_DEV * g.shape[2])


def _blocks_from_cols(w):
    return w.reshape(w.shape[0], N_DEV, w.shape[1] // N_DEV).transpose(1, 0, 2)


def _unpack_flat(flat2d, shapes):
    flat = flat2d.reshape(-1)
    out, off = [], 0
    for shp in shapes:
        sz = math.prod(shp)
        out.append(flat[off:off + sz].reshape(shp))
        off += sz
    return out


def _s5_mats(lam_re, lam_im, log_dt, b_re, b_im, c_re, c_im):
    lr = jnp.minimum(lam_re, S5_MAX_RE)
    li = lam_im
    dt = jnp.exp(log_dt)[:, None]
    mag = jnp.exp(lr * dt)
    ang = li * dt
    ab_re = mag * jnp.cos(ang)
    ab_im = mag * jnp.sin(ang)
    den = lr * lr + li * li
    nr = ab_re - 1.0
    f_re = ((nr * lr + ab_im * li) / den)[..., None]
    f_im = ((ab_im * lr - nr * li) / den)[..., None]
    bb_re = f_re * b_re - f_im * b_im
    bb_im = f_re * b_im + f_im * b_re
    eye = jnp.eye(S5_GROUPS, dtype=F32)
    a = jnp.stack([ab_re.reshape(-1), ab_im.reshape(-1)])
    wb = jnp.concatenate([jnp.einsum("gph,gk->ghkp", bb_re, eye).reshape(S5_WIDTH, NS),
                          jnp.einsum("gph,gk->ghkp", bb_im, eye).reshape(S5_WIDTH, NS)], axis=1)
    wc = jnp.concatenate([jnp.einsum("ghp,gk->gpkh", c_re, eye).reshape(NS, S5_WIDTH),
                          -jnp.einsum("ghp,gk->gpkh", c_im, eye).reshape(NS, S5_WIDTH)], axis=0)
    return a, wb, wc


def _power_table(a):
    ar, ai = a[0], a[1]
    rows_r, rows_i = [ar], [ai]
    for _ in range(7):
        pr, pi = rows_r[-1], rows_i[-1]
        rows_r.append(pr * ar - pi * ai)
        rows_i.append(pr * ai + pi * ar)
    return jnp.concatenate([jnp.stack(rows_r), jnp.stack(rows_i)], axis=1)


def _rope_table(positions):
    inv = ROPE_THETA ** (-jnp.arange(0, MLA_ROPE, 2, dtype=F32) / MLA_ROPE)
    ang = positions.astype(F32)[:, None] * inv
    cos, sin = jnp.cos(ang), jnp.sin(ang)
    s = positions.shape[0]
    one, zero, pad = jnp.ones((s, MLA_NOPE), F32), jnp.zeros((s, MLA_NOPE), F32), jnp.zeros((s, 32), F32)
    return jnp.concatenate([one, cos, cos, pad, zero, -sin, sin, pad, zero, cos, cos, pad], axis=1)


def _derived_weights(full):
    w_in = full["w_in"]
    k1, k2 = w_in[:, 768:784], w_in[:, 784:800]
    z64, z32 = jnp.zeros((1024, 64), BF16), jnp.zeros((1024, 32), BF16)
    w_inx = jnp.concatenate([w_in[:, :768], z64, k1, k2, z32, z64, k2, k1, z32, w_in[:, 800:]], axis=1)
    uq = full["w_uq"].reshape(256, MLA_HEADS, MLA_QK)
    nope, r1, r2 = uq[:, :, :64], uq[:, :, 64:80], uq[:, :, 80:]
    zq64, zq32 = jnp.zeros((256, MLA_HEADS, 64), BF16), jnp.zeros((256, MLA_HEADS, 32), BF16)
    w_q2 = jnp.concatenate([jnp.concatenate([nope, r1, r2, zq32], axis=2).reshape(256, 1024),
                            jnp.concatenate([zq64, r2, r1, zq32], axis=2).reshape(256, 1024)], axis=1)
    ukv = full["w_ukv"].reshape(256, MLA_HEADS, 128)
    w_kv2 = jnp.concatenate([jnp.concatenate([ukv[:, :, :64], zq64], axis=2).reshape(256, 1024),
                             jnp.concatenate([ukv[:, :, 64:], zq64], axis=2).reshape(256, 1024)], axis=1)
    oa = full["w_oa"].reshape(MLA_HEADS, MLA_V, 1024)
    w_oap = jnp.concatenate([oa, jnp.zeros_like(oa)], axis=1).reshape(1024, 1024)
    return w_inx, w_q2, w_kv2, w_oap


def _fold_grads(d_inx, d_q2, d_kv2, d_oap):
    d_k1 = d_inx[:, 832:848] + d_inx[:, 976:992]
    d_k2 = d_inx[:, 848:864] + d_inx[:, 960:976]
    d_w_in = jnp.concatenate([d_inx[:, :768], d_k1, d_k2, d_inx[:, 1024:]], axis=1)
    a = d_q2[:, :1024].reshape(256, MLA_HEADS, 128)
    b = d_q2[:, 1024:].reshape(256, MLA_HEADS, 128)
    d_w_uq = jnp.concatenate([a[:, :, :64], a[:, :, 64:80] + b[:, :, 80:96], a[:, :, 80:96] + b[:, :, 64:80]],
                             axis=2).reshape(256, MLA_HEADS * MLA_QK)
    kk = d_kv2[:, :1024].reshape(256, MLA_HEADS, 128)
    vv = d_kv2[:, 1024:].reshape(256, MLA_HEADS, 128)
    d_w_ukv = jnp.concatenate([kk[:, :, :64], vv[:, :, :64]], axis=2).reshape(256, 1024)
    d_w_oa = d_oap.reshape(MLA_HEADS, 128, 1024)[:, :64].reshape(512, 1024)
    return d_w_in, d_w_uq, d_w_ukv, d_w_oa


def kernel(x, mem, positions, ln_in_g, ln_in_b, w_in, s5_lam_re, s5_lam_im, s5_log_dt, s5_b_re, s5_b_im, s5_c_re, s5_c_im, s5_d, w_glu, q_norm_g, w_uq, kv_norm_g, w_ukv, w_oa, w_o, ln1_g, ln1_b, w_xq, w_xk, w_xv, w_xo, ln2_g, ln2_b, w_up, w_down, ln3_g, ln3_b, loss_target, m_ln_in_g, m_ln_in_b, m_w_in, m_s5_lam_re, m_s5_lam_im, m_s5_log_dt, m_s5_b_re, m_s5_b_im, m_s5_c_re, m_s5_c_im, m_s5_d, m_w_glu, m_q_norm_g, m_w_uq, m_kv_norm_g, m_w_ukv, m_w_oa, m_w_o, m_ln1_g, m_ln1_b, m_w_xq, m_w_xk, m_w_xv, m_w_xo, m_ln2_g, m_ln2_b, m_w_up, m_w_down, m_ln3_g, m_ln3_b, v_ln_in_g, v_ln_in_b, v_w_in, v_s5_lam_re, v_s5_lam_im, v_s5_log_dt, v_s5_b_re, v_s5_b_im, v_s5_c_re, v_s5_c_im, v_s5_d, v_w_glu, v_q_norm_g, v_w_uq, v_kv_norm_g, v_w_ukv, v_w_oa, v_w_o, v_ln1_g, v_ln1_b, v_w_xq, v_w_xk, v_w_xv, v_w_xo, v_ln2_g, v_ln2_b, v_w_up, v_w_down, v_ln3_g, v_ln3_b):
    args = dict(locals())
    wts = {n: args[n] for n in WEIGHTS}
    mom = {n: args["m_" + n] for n in WEIGHTS}
    vel = {n: args["v_" + n] for n in WEIGHTS}
    xs, mems, tgt = x[0], mem[0], loss_target[0]
    s = xs.shape[0]
    names_big = [n for (n, _, _, _) in SHARDED]
    kind = {n: kd for (n, kd, _, _) in SHARDED}
    small_rows = _round_up(sum(math.prod(wts[n].shape) for n in SMALL), 8 * LANES) // LANES

    gathered = dict(zip(names_big, _all_to_all("all_gather", [wts[n][0].astype(BF16) for n in names_big], True)))
    full = {n: gathered[n].reshape(-1, gathered[n].shape[2]) for n in names_big if kind[n] == "row"}
    full.update({n: _cols_from_blocks(gathered[n]) for n in ("w_in", "w_uq", "w_ukv", "w_oa")})
    w_inx, w_q2, w_kv2, w_oap = _derived_weights(full)
    w_glu8, w_up8 = gathered["w_glu"], gathered["w_up"]
    s5_args = (s5_lam_re[0], s5_lam_im[0], s5_log_dt[0], s5_b_re[0], s5_b_im[0], s5_c_re[0], s5_c_im[0])
    (a_mat, w_b, w_c), s5_vjp = jax.vjp(_s5_mats, *s5_args)
    w_bb, w_cb = w_b.astype(BF16), w_c.astype(BF16)
    pw = _power_table(a_mat)
    pwb = _power_table(a_mat * jnp.array([[1.0], [-1.0]], F32))[::-1]
    tab = _rope_table(positions[0])
    row = lambda a: a.reshape(1, -1)
    ln_g, ln_b = row(ln_in_g), row(ln_in_b)
    lns = [ln1_g, ln1_b, ln2_g, ln2_b]

    memk, memv = _mem_kv(mems, full["w_xk"], full["w_xv"])
    h0, z3, gate, bu, q, k, v = _in_fwd(xs, tab, ln_g, ln_b, q_norm_g, kv_norm_g, w_inx, w_bb, w_q2, w_kv2)
    h = _scan_fwd(pw, bu)
    yl, s_out = _s5out_fwd(h, z3, s5_d, w_cb, w_glu8)
    o, lse_t = _attn_fwd(q, k, v.T)
    r2, mixin_b, h1_b, ox_b = _post_fwd(o, s_out, gate, h0, memk, memv, lns,
                                        w_oap, full["w_o"], full["w_xq"], full["w_xo"])
    dh2, h2_b, dup_b, act_b, dff_b, loss_acc, d_ln3_g, d_ln3_b = _mlp(
        r2, tgt, ln2_g, ln2_b, ln3_g, ln3_b, w_up8, full["w_down"])
    (do_b, delta, d_so, dgate, dh0p, daout_b, dmix_b, dqx_b, dxa_b,
     dmk, dmv, d_ln1_g, d_ln1_b, d_ln2_g, d_ln2_b) = _post_bwd(
        dh2, o, s_out, gate, h0, memk, memv, lns, w_oap, full["w_o"], full["w_xq"], full["w_xo"])
    dl_t = delta.T[:MLA_HEADS].reshape(MLA_HEADS, 1, s)
    dq, dk, dv = _attn_bwd(q, k, v, k.T, do_b, lse_t, dl_t)
    gh, du_p, yg_b, dy12_b, dyl_b, d_s5_d = _s5out_bwd(d_so, yl, z3, s5_d, w_cb, w_glu8)
    lam, d_a = _scan_bwd(pwb, gh, h)
    (dx, h0_b, dz_b, u_b, cqn_b, ckvn_b, dq2_b, dkv2_b, d_ln_g, d_ln_b, d_qg, d_kvg) = _in_bwd(
        xs, tab, z3, dq, dk, dv, lam, du_p, dgate, dh0p, ln_g, ln_b, q_norm_g, kv_norm_g, w_inx, w_bb, w_q2, w_kv2)

    d_w_in, d_w_uq, d_w_ukv, d_w_oa = _fold_grads(
        _wgrad("wg_in", h0_b, dz_b), _wgrad("wg_q", cqn_b, dq2_b), _wgrad("wg_kv", ckvn_b, dkv2_b),
        _wgrad("wg_oa", o, daout_b))
    rows8 = lambda g: g.reshape(N_DEV, g.shape[0] // N_DEV, g.shape[1])
    send = {
        "w_in": _blocks_from_cols(d_w_in).astype(BF16), "w_uq": _blocks_from_cols(d_w_uq).astype(BF16),
        "w_ukv": _blocks_from_cols(d_w_ukv).astype(BF16), "w_oa": _blocks_from_cols(d_w_oa).astype(BF16),
        "w_glu": _wgrad("wg_glu", yg_b, dy12_b, BF16, N_DEV),
        "w_up": _wgrad("wg_up", h2_b, dup_b, BF16, N_DEV),
        "w_o": rows8(_wgrad("wg_o", mixin_b, dmix_b, BF16)),
        "w_xq": rows8(_wgrad("wg_xq", h1_b, dqx_b, BF16)),
        "w_xk": rows8(_wgrad("wg_xk", mems, dmk, BF16)),
        "w_xv": rows8(_wgrad("wg_xv", mems, dmv, BF16)),
        "w_xo": rows8(_wgrad("wg_xo", ox_b, dxa_b, BF16)),
        "w_down": rows8(_wgrad("wg_down", act_b, dff_b, BF16)),
    }
    d_s5 = s5_vjp((d_a.reshape(2, NS), _wgrad("wg_s5b", u_b, lam), _wgrad("wg_s5c", h, dyl_b)))
    small_grads = {
        "ln_in_g": d_ln_g, "ln_in_b": d_ln_b, "s5_lam_re": d_s5[0], "s5_lam_im": d_s5[1], "s5_log_dt": d_s5[2],
        "s5_b_re": d_s5[3], "s5_b_im": d_s5[4], "s5_c_re": d_s5[5], "s5_c_im": d_s5[6], "s5_d": d_s5_d,
        "q_norm_g": d_qg, "kv_norm_g": d_kvg, "ln1_g": d_ln1_g, "ln1_b": d_ln1_b, "ln2_g": d_ln2_g,
        "ln2_b": d_ln2_b, "ln3_g": d_ln3_g, "ln3_b": d_ln3_b,
    }
    small_send = jnp.broadcast_to(_pack_rows([small_grads[n] for n in SMALL], small_rows)[None],
                                  (N_DEV, small_rows, LANES))

    recv = _all_to_all("grad_exchange", [send[n] for n in names_big] + [small_send], False)
    results = [dict(), dict(), dict(), dict()]
    for n, parts in zip(names_big, recv):
        outs = _adamw("adamw_" + n, parts, wts[n][0], mom[n][0], vel[n][0])
        for res, a in zip(results, outs):
            res[n] = a[None]
    small_out = _adamw("adamw_small", recv[-1],
                       *[_pack_rows([t[n] for n in SMALL], small_rows) for t in (wts, mom, vel)])
    for res, fs in zip(results, small_out):
        for n, a in zip(SMALL, _unpack_flat(fs, [wts[n].shape for n in SMALL])):
            res[n] = a

    loss = lax.psum(loss_acc[0, 0], AXES)
    return (loss, dx[None], *[res[n] for res in results for n in WEIGHTS])
```

```python
import math

import jax
import jax.numpy as jnp
from jax import lax
from jax.experimental import pallas as pl
from jax.experimental.pallas import tpu as pltpu

F32 = jnp.float32
BF16 = jnp.bfloat16

D_MODEL = 1024
S5_WIDTH = 256
S5_GROUP_CH = 16
S5_GROUPS = 16
S5_STATE = 64
NS = S5_GROUPS * S5_STATE
S5_MAX_RE = -1e-4
MLA_HEADS = 8
MLA_NOPE = 64
MLA_ROPE = 32
MLA_QK = 96
MLA_V = 64
HEAD_PAD = 128
ROPE_THETA = 10000.0
XATTN_HEADS = 4
XATTN_HD = 256
MLP_HIDDEN = 4096
LN_EPS = 1e-5
RMS_EPS = 1e-6
NEG_INF = -1e30
LOG2E = 1.4426950408889634
DN_ALPHA = 2.0 ** 0.25
ADAM_LR = 0.001
ADAM_B1 = 0.9
ADAM_B2 = 0.999
ADAM_EPS = 1e-08
ADAM_WD = 0.01
ADAM_STEP = 10

N_DEV = 8
AXES = ("x", "y", "c")
MESH = pl.DeviceIdType.MESH
LANES = 1024
VMEM_LIMIT = 60 * 1024 * 1024

ROW_TILE = 256
SCAN_TILE = 256
ATT_TILE = 512
ATT_SPLIT = 1
ADAM_TILE = 128

SHARDED = (
    ("w_in", "col", 1024, 2848), ("w_glu", "col", 256, 2048), ("w_uq", "col", 256, 768),
    ("w_ukv", "col", 256, 1024), ("w_oa", "col", 512, 1024), ("w_o", "row", 1024, 1024),
    ("w_xq", "row", 1024, 1024), ("w_xk", "row", 1024, 1024), ("w_xv", "row", 1024, 1024),
    ("w_xo", "row", 1024, 1024), ("w_up", "col", 1024, 4096), ("w_down", "row", 4096, 1024),
)
EARLY = ("w_in", "w_glu", "w_uq", "w_ukv")
SMALL = ("ln_in_g", "ln_in_b", "s5_lam_re", "s5_lam_im", "s5_log_dt", "s5_b_re", "s5_b_im", "s5_c_re",
         "s5_c_im", "s5_d", "q_norm_g", "kv_norm_g", "ln1_g", "ln1_b", "ln2_g", "ln2_b", "ln3_g", "ln3_b")
WEIGHTS = ("ln_in_g", "ln_in_b", "w_in", "s5_lam_re", "s5_lam_im", "s5_log_dt", "s5_b_re", "s5_b_im",
           "s5_c_re", "s5_c_im", "s5_d", "w_glu", "q_norm_g", "w_uq", "kv_norm_g", "w_ukv", "w_oa", "w_o",
           "ln1_g", "ln1_b", "w_xq", "w_xk", "w_xv", "w_xo", "ln2_g", "ln2_b", "w_up", "w_down", "ln3_g", "ln3_b")


def _round_up(n, m):
    return (n + m - 1) // m * m


def _bf(a):
    return a.astype(BF16)


def _mm(a, b):
    return jnp.dot(_bf(a), _bf(b), preferred_element_type=F32)


def _mm_nt(a, b):
    return lax.dot_general(_bf(a), _bf(b), (((1,), (1,)), ((), ())), preferred_element_type=F32)


def _mm_tn(a, b):
    return lax.dot_general(_bf(a), _bf(b), (((0,), (0,)), ((), ())), preferred_element_type=F32)


def _sigmoid(a):
    return 1.0 / (1.0 + jnp.exp(-a))


def _gelu(a):
    return 0.5 * a * (1.0 + lax.erf(a * (2.0 ** -0.5)))


def _gelu_grad(a):
    return 0.5 * (1.0 + lax.erf(a * (2.0 ** -0.5))) + a * jnp.exp(-0.5 * a * a) * (1.0 / math.sqrt(2.0 * math.pi))


def _ln_fwd(a, g, b):
    mu = jnp.mean(a, axis=-1, keepdims=True)
    ac = a - mu
    var = jnp.mean(ac * ac, axis=-1, keepdims=True)
    rstd = lax.rsqrt(var + LN_EPS)
    xhat = ac * rstd
    return xhat * g + b, xhat, rstd


def _ln_bwd(dy, xhat, rstd, g):
    dxh = dy * g
    m1 = jnp.mean(dxh, axis=-1, keepdims=True)
    m2 = jnp.mean(dxh * xhat, axis=-1, keepdims=True)
    dx = rstd * (dxh - m1 - xhat * m2)
    return dx, jnp.sum(dy * xhat, axis=0, keepdims=True), jnp.sum(dy, axis=0, keepdims=True)


def _rms_fwd(a, g):
    r = lax.rsqrt(jnp.mean(a * a, axis=-1, keepdims=True) + RMS_EPS)
    xn = a * r
    return xn * g, xn, r


def _rms_bwd(dy, xn, r, g):
    dxn = dy * g
    dx = r * (dxn - xn * jnp.mean(dxn * xn, axis=-1, keepdims=True))
    return dx, jnp.sum(dy * xn, axis=0, keepdims=True)


def _tile_heads(a):
    return jnp.concatenate([a] * MLA_HEADS, axis=1)


def _row_call(name, body, n_rows, ts, tiled_in, full_in, weights, tiled_out, acc_out, reverse=False, scratch=()):
    n = n_rows // ts
    assert n * ts == n_rows, (name, n_rows, ts)
    nt, nf, nw = len(tiled_in), len(full_in), len(weights)
    nto, nao = len(tiled_out), len(acc_out)
    if reverse:
        imap = lambda i: (n - 1 - i, 0)
    else:
        imap = lambda i: (i, 0)
    const = lambda i: (0, 0)

    def kern(*refs):
        ins = refs[:nt + nf]
        w_hbm = refs[nt + nf:nt + nf + nw]
        outs = refs[nt + nf + nw:nt + nf + nw + nto + nao]
        scr = refs[nt + nf + nw + nto + nao:]
        w_vmem = scr[:nw]
        extra = scr[nw + 1:] if nw else scr
        if nw:
            sem = scr[nw]

            @pl.when(pl.program_id(0) == 0)
            def _():
                cps = [pltpu.make_async_copy(w_hbm[k], w_vmem[k], sem.at[k]) for k in range(nw)]
                for cp in cps:
                    cp.start()
                for cp in cps:
                    cp.wait()
        body(*ins, *w_vmem, *outs, *extra)

    in_specs = [pl.BlockSpec((ts, a.shape[1]), imap) for a in tiled_in]
    in_specs += [pl.BlockSpec(a.shape, const) for a in full_in]
    in_specs += [pl.BlockSpec(memory_space=pl.ANY) for _ in weights]
    out_shape = [jax.ShapeDtypeStruct((n_rows, c), dt) for (c, dt) in tiled_out]
    out_shape += [jax.ShapeDtypeStruct(shp, F32) for shp in acc_out]
    out_specs = [pl.BlockSpec((ts, c), imap) for (c, dt) in tiled_out]
    out_specs += [pl.BlockSpec(shp, const) for shp in acc_out]
    scratch_shapes = [pltpu.VMEM(w.shape, w.dtype) for w in weights]
    if nw:
        scratch_shapes.append(pltpu.SemaphoreType.DMA((nw,)))
    scratch_shapes += list(scratch)
    return pl.pallas_call(
        kern, name=name, grid=(n,), in_specs=in_specs, out_specs=out_specs, out_shape=out_shape,
        scratch_shapes=scratch_shapes,
        compiler_params=pltpu.CompilerParams(dimension_semantics=("arbitrary",), vmem_limit_bytes=VMEM_LIMIT),
    )(*tiled_in, *full_in, *weights)


def _mem_kv(mem, w_xk, w_xv):
    m = mem.shape[0]

    def body(mem_ref, wk_ref, wv_ref, k_ref, v_ref):
        mb = mem_ref[...]
        k_ref[...] = _mm(mb, wk_ref[...]).astype(BF16)
        v_ref[...] = _mm(mb, wv_ref[...]).astype(BF16)

    return pl.pallas_call(
        body, name="mem_kv",
        out_shape=[jax.ShapeDtypeStruct((m, D_MODEL), BF16)] * 2,
        compiler_params=pltpu.CompilerParams(vmem_limit_bytes=VMEM_LIMIT),
    )(mem, w_xk, w_xv)


def _in_fwd(x, tab, ln_g, ln_b, qg, kvg, w_inx, w_b, w_q2, w_kv2):
    def body(x_ref, tab_ref, lng_ref, lnb_ref, qg_ref, kvg_ref, winx, wb, wq2, wkv2,
             h0_ref, z3_ref, gate_ref, bu_ref, q_ref, k_ref, v_ref):
        h0, _, _ = _ln_fwd(x_ref[...], lng_ref[...], lnb_ref[...])
        h0_ref[...] = h0
        z = _mm(h0, winx[...])
        z3_ref[...] = z[:, :768]
        gate_ref[...] = z[:, 1024:]
        bu_ref[...] = _mm(z[:, :256], wb[...])
        tab_v = tab_ref[...]
        cq1, cq2, ck1 = tab_v[:, :128], tab_v[:, 128:256], tab_v[:, 256:384]
        cqn, _, _ = _rms_fwd(z[:, 256:512], qg_ref[...])
        q2 = _mm(cqn, wq2[...])
        q_ref[...] = (q2[:, :1024] * _tile_heads(cq1) + q2[:, 1024:] * _tile_heads(cq2)).astype(BF16)
        ckvn, _, _ = _rms_fwd(z[:, 512:768], kvg_ref[...])
        kv2 = _mm(ckvn, wkv2[...])
        krp = z[:, 768:896] * ck1 + z[:, 896:1024] * cq2
        k_ref[...] = (kv2[:, :1024] + _tile_heads(krp)).astype(BF16)
        v_ref[...] = kv2[:, 1024:].astype(BF16)

    s = x.shape[0]
    return _row_call(
        "in_fwd", body, s, min(ROW_TILE, s), [x, tab], [ln_g, ln_b, qg, kvg], [w_inx, w_b, w_q2, w_kv2],
        [(1024, F32), (768, F32), (2048, F32), (2048, F32), (1024, BF16), (1024, BF16), (1024, BF16)], [])


def _scan_level(row, xr, xi, ar, ai, k, forward):
    if forward:
        keep = row >= k
        sr, si = pltpu.roll(xr, k, 0), pltpu.roll(xi, k, 0)
    else:
        keep = row < 8 - k
        sr, si = pltpu.roll(xr, 8 - k, 0), pltpu.roll(xi, 8 - k, 0)
    sr = jnp.where(keep, sr, 0.0)
    si = jnp.where(keep, si, 0.0)
    return xr + ar * sr - ai * si, xi + ar * si + ai * sr


def _scan_fwd(pw, bu):
    s = bu.shape[0]
    ts = min(SCAN_TILE, s)
    nblk = ts // 8

    def body(bu_ref, pw_ref, h_ref, carry):
        @pl.when(pl.program_id(0) == 0)
        def _():
            carry[...] = jnp.zeros_like(carry)

        row = lax.broadcasted_iota(jnp.int32, (8, NS), 0)
        a1 = (pw_ref[0:1, :NS], pw_ref[0:1, NS:])
        a2 = (pw_ref[1:2, :NS], pw_ref[1:2, NS:])
        a4 = (pw_ref[3:4, :NS], pw_ref[3:4, NS:])

        def blk(b, _):
            r0 = pl.multiple_of(b * 8, 8)
            xr = bu_ref[pl.ds(r0, 8), :NS]
            xi = bu_ref[pl.ds(r0, 8), NS:]
            xr, xi = _scan_level(row, xr, xi, a1[0], a1[1], 1, True)
            xr, xi = _scan_level(row, xr, xi, a2[0], a2[1], 2, True)
            xr, xi = _scan_level(row, xr, xi, a4[0], a4[1], 4, True)
            cr, ci = carry[7:8, :NS], carry[7:8, NS:]
            pr, pi = pw_ref[:, :NS], pw_ref[:, NS:]
            hr = xr + pr * cr - pi * ci
            hi = xi + pr * ci + pi * cr
            h_ref[pl.ds(r0, 8), :NS] = hr
            h_ref[pl.ds(r0, 8), NS:] = hi
            carry[:, :NS] = hr
            carry[:, NS:] = hi
            return 0

        lax.fori_loop(0, nblk, blk, 0)

    return _row_call("scan_fwd", body, s, ts, [bu], [pw], [], [(2 * NS, F32)], [],
                     scratch=[pltpu.VMEM((8, 2 * NS), F32)])[0]


def _scan_bwd(pwb, gh, h):
    s = gh.shape[0]
    ts = min(SCAN_TILE, s)
    nblk = ts // 8
    n_tiles = s // ts

    def body(g_ref, h_ref, pw_ref, lam_ref, da_ref, carry, acc):
        @pl.when(pl.program_id(0) == 0)
        def _():
            carry[...] = jnp.zeros_like(carry)
            acc[...] = jnp.zeros_like(acc)

        row = lax.broadcasted_iota(jnp.int32, (8, NS), 0)
        a1 = (pw_ref[7:8, :NS], pw_ref[7:8, NS:])
        a2 = (pw_ref[6:7, :NS], pw_ref[6:7, NS:])
        a4 = (pw_ref[4:5, :NS], pw_ref[4:5, NS:])

        def blk(bb, _):
            r0 = pl.multiple_of((nblk - 1 - bb) * 8, 8)
            xr = g_ref[pl.ds(r0, 8), :NS]
            xi = g_ref[pl.ds(r0, 8), NS:]
            xr, xi = _scan_level(row, xr, xi, a1[0], a1[1], 1, False)
            xr, xi = _scan_level(row, xr, xi, a2[0], a2[1], 2, False)
            xr, xi = _scan_level(row, xr, xi, a4[0], a4[1], 4, False)
            cr, ci = carry[0:1, :NS], carry[0:1, NS:]
            pr, pi = pw_ref[:, :NS], pw_ref[:, NS:]
            lr = xr + pr * cr - pi * ci
            li = xi + pr * ci + pi * cr
            lam_ref[pl.ds(r0, 8), :NS] = lr
            lam_ref[pl.ds(r0, 8), NS:] = li
            nr = jnp.where(row < 7, pltpu.roll(lr, 7, 0), cr)
            ni = jnp.where(row < 7, pltpu.roll(li, 7, 0), ci)
            hr = h_ref[pl.ds(r0, 8), :NS]
            hi = h_ref[pl.ds(r0, 8), NS:]
            acc[:, :NS] += nr * hr + ni * hi
            acc[:, NS:] += ni * hr - nr * hi
            carry[:, :NS] = lr
            carry[:, NS:] = li
            return 0

        lax.fori_loop(0, nblk, blk, 0)

        @pl.when(pl.program_id(0) == n_tiles - 1)
        def _():
            da_ref[...] = jnp.sum(acc[...], axis=0, keepdims=True)

    return _row_call("scan_bwd", body, s, ts, [gh, h], [pwb], [], [(2 * NS, F32)], [(1, 2 * NS)], reverse=True,
                     scratch=[pltpu.VMEM((8, 2 * NS), F32), pltpu.VMEM((8, 2 * NS), F32)])


def _s5out_fwd(h, z3, d_skip, w_c, w_glu):
    def body(h_ref, z3_ref, d_ref, wc, wglu, yl_ref, so_ref):
        yl = _mm(h_ref[...], wc[...]) + d_ref[...] * z3_ref[:, :256]
        yl_ref[...] = yl
        y12 = _mm(_gelu(yl), wglu[...])
        so_ref[...] = y12[:, :1024] * _sigmoid(y12[:, 1024:])

    s = h.shape[0]
    return _row_call("s5out_fwd", body, s, min(ROW_TILE, s), [h, z3], [d_skip], [w_c, w_glu],
                     [(256, F32), (1024, F32)], [])


def _causal_mask_t(t, w, off):
    row = lax.broadcasted_iota(jnp.int32, (t, w), 0)
    col = lax.broadcasted_iota(jnp.int32, (t, w), 1)
    return row <= col + off


def _attn_fwd(q, k, vt):
    s = q.shape[0]
    t = min(ATT_TILE, s)
    nq = s // t
    scale = MLA_QK ** -0.5
    c2 = scale * LOG2E

    hw = t // ATT_SPLIT

    def body(q_ref, k_ref, vt_ref, o_ref, lse_ref):
        i = pl.program_id(1)
        qs = [q_ref[g * hw:(g + 1) * hw, :] for g in range(ATT_SPLIT)]

        def step(kb, state, masked):
            r0 = pl.multiple_of(kb * t, t)
            kblk = k_ref[pl.ds(r0, t), :]
            vtb = vt_ref[:, pl.ds(r0, t)]
            sts = [_mm_nt(kblk, qs[g]) for g in range(ATT_SPLIT)]
            out = []
            for g in range(ATT_SPLIT):
                m, l, acc = state[g]
                st = sts[g]
                if masked:
                    st = jnp.where(_causal_mask_t(t, hw, g * hw), st, NEG_INF)
                m_new = jnp.maximum(m, jnp.max(st, axis=0, keepdims=True))
                p = jnp.exp2((st - m_new) * c2)
                a = jnp.exp2((m - m_new) * c2)
                l = a * l + jnp.sum(p, axis=0, keepdims=True)
                acc = a * acc + _mm(vtb, p)
                out.append((m_new, l, acc))
            return tuple(out)

        init = tuple((jnp.full((1, hw), NEG_INF, F32), jnp.zeros((1, hw), F32), jnp.zeros((HEAD_PAD, hw), F32))
                     for _ in range(ATT_SPLIT))
        state = lax.fori_loop(0, i, lambda kb, c: step(kb, c, False), init)
        state = step(i, state, True)
        for g, (m, l, acc) in enumerate(state):
            o_ref[g * hw:(g + 1) * hw, :] = (acc / l).T
            lse_ref[:, g * hw:(g + 1) * hw] = m * scale + jnp.log(l)

    return pl.pallas_call(
        body, name="attn_fwd", grid=(MLA_HEADS, nq),
        in_specs=[pl.BlockSpec((t, HEAD_PAD), lambda h, i: (i, h)),
                  pl.BlockSpec((s, HEAD_PAD), lambda h, i: (0, h)),
                  pl.BlockSpec((HEAD_PAD, s), lambda h, i: (h, 0))],
        out_specs=[pl.BlockSpec((t, HEAD_PAD), lambda h, i: (i, h)),
                   pl.BlockSpec((None, 1, t), lambda h, i: (h, 0, i))],
        out_shape=[jax.ShapeDtypeStruct((s, MLA_HEADS * HEAD_PAD), F32),
                   jax.ShapeDtypeStruct((MLA_HEADS, 1, s), F32)],
        compiler_params=pltpu.CompilerParams(dimension_semantics=("arbitrary", "arbitrary"),
                                             vmem_limit_bytes=VMEM_LIMIT),
    )(q, k, vt)


def _attn_bwd(q, k, v, kt, do, lse_t, dl_t):
    s = q.shape[0]
    t = min(ATT_TILE, s)
    nq = s // t
    scale = MLA_QK ** -0.5
    c2 = scale * LOG2E
    hw = t // ATT_SPLIT

    def body(q_ref, k_ref, v_ref, kt_ref, do_ref, lse_ref, dl_ref, dq_ref, dk_ref, dv_ref, dqt):
        j = pl.program_id(1)

        @pl.when(j == 0)
        def _():
            dqt[...] = jnp.zeros_like(dqt)

        kk = k_ref[...]
        vv = v_ref[...]
        ktb = kt_ref[...]

        def step(ib, carry, masked):
            dk, dv = carry
            cols = [pl.ds(pl.multiple_of(ib * t + g * hw, hw), hw) for g in range(ATT_SPLIT)]
            qq = [q_ref[c, :] for c in cols]
            dd = [do_ref[c, :] for c in cols]
            sts = [_mm_nt(kk, qq[g]) for g in range(ATT_SPLIT)]
            dps = [_mm_nt(vv, dd[g]) for g in range(ATT_SPLIT)]
            for g in range(ATT_SPLIT):
                pt = jnp.exp2(sts[g] * c2 - lse_ref[:, cols[g]] * LOG2E)
                if masked:
                    pt = jnp.where(_causal_mask_t(t, hw, g * hw), pt, 0.0)
                dv = dv + _mm(pt, dd[g])
                dst = (pt * (dps[g] - dl_ref[:, cols[g]])).astype(BF16)
                dk = dk + _mm(dst, qq[g])
                dqt[:, cols[g]] += _mm(ktb, dst)
            return dk, dv

        zero = jnp.zeros((t, HEAD_PAD), F32)
        carry = step(j, (zero, zero), True)
        dk, dv = lax.fori_loop(j + 1, nq, lambda ib, c: step(ib, c, False), carry)
        dk_ref[...] = dk * scale
        dv_ref[...] = dv

        @pl.when(j == nq - 1)
        def _():
            for cc in range(nq):
                dq_ref[cc * t:(cc + 1) * t, :] = dqt[:, cc * t:(cc + 1) * t].T * scale

    full = pl.BlockSpec((s, HEAD_PAD), lambda h, j: (0, h))
    tile = pl.BlockSpec((t, HEAD_PAD), lambda h, j: (j, h))
    stat = pl.BlockSpec((None, 1, s), lambda h, j: (h, 0, 0))
    return pl.pallas_call(
        body, name="attn_bwd", grid=(MLA_HEADS, nq),
        in_specs=[full, tile, tile, pl.BlockSpec((HEAD_PAD, t), lambda h, j: (h, j)), full, stat, stat],
        out_specs=[full, tile, tile],
        out_shape=[jax.ShapeDtypeStruct((s, MLA_HEADS * HEAD_PAD), F32)] * 3,
        scratch_shapes=[pltpu.VMEM((HEAD_PAD, s), F32)],
        compiler_params=pltpu.CompilerParams(dimension_semantics=("arbitrary", "arbitrary"),
                                             vmem_limit_bytes=VMEM_LIMIT),
    )(q, k, v, kt, do, lse_t, dl_t)


def _post_forward(o, s_out, gate, h0, memk, memv, w_oa, w_o, w_xq, w_xo, g1, b1, g2, b2):
    a_out = _mm(o, w_oa)
    sg_s = _sigmoid(gate[:, :1024])
    sg_a = _sigmoid(gate[:, 1024:])
    mixin = sg_s * s_out + sg_a * a_out
    r1 = DN_ALPHA * h0 + _mm(mixin, w_o)
    h1, xh1, rs1 = _ln_fwd(r1, g1, b1)
    qxb = _mm(h1, w_xq).astype(BF16)
    ps, oxs = [], []
    for hh in range(XATTN_HEADS):
        sl = slice(hh * XATTN_HD, (hh + 1) * XATTN_HD)
        sc = _mm_nt(qxb[:, sl], memk[:, sl]) * (XATTN_HD ** -0.5)
        e = jnp.exp(sc - jnp.max(sc, axis=1, keepdims=True))
        p = e / jnp.sum(e, axis=1, keepdims=True)
        ps.append(p)
        oxs.append(_mm(p, memv[:, sl]))
    ox = jnp.concatenate(oxs, axis=1)
    r2 = DN_ALPHA * h1 + _mm(ox, w_xo)
    return dict(a_out=a_out, sg_s=sg_s, sg_a=sg_a, mixin=mixin, h1=h1, xh1=xh1, rs1=rs1, qxb=qxb, ps=ps, ox=ox, r2=r2)


def _post_fwd(o, s_out, gate, h0, memk, memv, lns, w_oa, w_o, w_xq, w_xo):
    def body(o_ref, so_ref, gate_ref, h0_ref, mk_ref, mv_ref, g1, b1, g2, b2, woa, wo, wxq, wxo,
             r2_ref, mixin_ref, h1_ref, ox_ref):
        f = _post_forward(o_ref[...], so_ref[...], gate_ref[...], h0_ref[...], mk_ref[...], mv_ref[...],
                          woa[...], wo[...], wxq[...], wxo[...], g1[...], b1[...], g2[...], b2[...])
        r2_ref[...] = f["r2"]
        mixin_ref[...] = f["mixin"].astype(BF16)
        h1_ref[...] = f["h1"].astype(BF16)
        ox_ref[...] = f["ox"].astype(BF16)

    s = o.shape[0]
    return _row_call("post_fwd", body, s, min(ROW_TILE, s), [o, s_out, gate, h0], [memk, memv, *lns],
                     [w_oa, w_o, w_xq, w_xo], [(1024, F32), (1024, BF16), (1024, BF16), (1024, BF16)], [])


def _mlp(r2, target, g2, b2, g3, b3, w_up, w_down):
    cw = 1024
    n_chunk = MLP_HIDDEN // cw

    def body(r2_ref, tgt_ref, g2_ref, b2_ref, g3_ref, b3_ref, wup, wdn,
             dh2_ref, h2b_ref, dup_ref, act_ref, dff_ref, loss_ref, dg3_ref, db3_ref, up_scr):
        @pl.when(pl.program_id(0) == 0)
        def _():
            loss_ref[...] = jnp.zeros_like(loss_ref)
            dg3_ref[...] = jnp.zeros_like(dg3_ref)
            db3_ref[...] = jnp.zeros_like(db3_ref)

        h2, _, _ = _ln_fwd(r2_ref[...], g2_ref[...], b2_ref[...])
        h2b = h2.astype(BF16)
        h2b_ref[...] = h2b
        ff = jnp.zeros(h2.shape, F32)
        for c in range(n_chunk):
            sl = slice(c * cw, (c + 1) * cw)
            a = jnp.maximum(_mm(h2b, wup[:, sl]), 0.0)
            up_scr[:, sl] = a
            actb = (a * a).astype(BF16)
            act_ref[:, sl] = actb
            ff = ff + _mm(actb, wdn[sl, :])
        h3, xh3, rs3 = _ln_fwd(DN_ALPHA * h2 + ff, g3_ref[...], b3_ref[...])
        err = h3 - tgt_ref[...]
        loss_ref[...] += 0.5 * jnp.sum(err * err) * (1.0 / D_MODEL)
        dr3, dg3, db3 = _ln_bwd(err * (1.0 / D_MODEL), xh3, rs3, g3_ref[...])
        dg3_ref[...] += dg3
        db3_ref[...] += db3
        dffb = dr3.astype(BF16)
        dff_ref[...] = dffb
        dh2 = DN_ALPHA * dr3
        for c in range(n_chunk):
            sl = slice(c * cw, (c + 1) * cw)
            dupb = (_mm_nt(dffb, wdn[sl, :]) * (2.0 * up_scr[:, sl])).astype(BF16)
            dup_ref[:, sl] = dupb
            dh2 = dh2 + _mm_nt(dupb, wup[:, sl])
        dh2_ref[...] = dh2

    s = r2.shape[0]
    ts = min(ROW_TILE, s)
    return _row_call("mlp", body, s, ts, [r2, target], [g2, b2, g3, b3], [w_up, w_down],
                     [(1024, F32), (1024, BF16), (MLP_HIDDEN, BF16), (MLP_HIDDEN, BF16), (1024, BF16)],
                     [(8, 128), (1, 1024), (1, 1024)], scratch=[pltpu.VMEM((ts, MLP_HIDDEN), F32)])


def _post_bwd(dh2, o, s_out, gate, h0, memk, memv, lns, w_oa, w_o, w_xq, w_xo):
    def body(dh2_ref, o_ref, so_ref, gate_ref, h0_ref, mk_ref, mv_ref, g1, b1, g2, b2, woa, wo, wxq, wxo,
             do_ref, dl_ref, dso_ref, dgate_ref, dh0_ref, daout_ref, dmix_ref, dqx_ref, dxa_ref,
             dmk_ref, dmv_ref, dg1_ref, db1_ref, dg2_ref, db2_ref):
        @pl.when(pl.program_id(0) == 0)
        def _():
            for r in (dmk_ref, dmv_ref, dg1_ref, db1_ref, dg2_ref, db2_ref):
                r[...] = jnp.zeros_like(r)

        o = o_ref[...]
        s_out = so_ref[...]
        memk, memv = mk_ref[...], mv_ref[...]
        f = _post_forward(o, s_out, gate_ref[...], h0_ref[...], memk, memv,
                          woa[...], wo[...], wxq[...], wxo[...], g1[...], b1[...], g2[...], b2[...])
        _, xh2, rs2 = _ln_fwd(f["r2"], g2[...], b2[...])
        dr2, dg2, db2 = _ln_bwd(dh2_ref[...], xh2, rs2, g2[...])
        dg2_ref[...] += dg2
        db2_ref[...] += db2
        dxab = dr2.astype(BF16)
        dxa_ref[...] = dxab
        dox = _mm_nt(dxab, wxo[...])
        dqs = []
        for hh in range(XATTN_HEADS):
            sl = slice(hh * XATTN_HD, (hh + 1) * XATTN_HD)
            p = f["ps"][hh]
            doxh = dox[:, sl].astype(BF16)
            dp = _mm_nt(doxh, memv[:, sl])
            ds = (p * (dp - jnp.sum(dp * p, axis=1, keepdims=True)) * (XATTN_HD ** -0.5)).astype(BF16)
            dqs.append(_mm(ds, memk[:, sl]))
            dmk_ref[:, sl] += _mm_tn(ds, f["qxb"][:, sl])
            dmv_ref[:, sl] += _mm_tn(p, doxh)
        dqxb = jnp.concatenate(dqs, axis=1).astype(BF16)
        dqx_ref[...] = dqxb
        dh1 = DN_ALPHA * dr2 + _mm_nt(dqxb, wxq[...])
        dr1, dg1, db1 = _ln_bwd(dh1, f["xh1"], f["rs1"], g1[...])
        dg1_ref[...] += dg1
        db1_ref[...] += db1
        dh0_ref[...] = DN_ALPHA * dr1
        dmixb = dr1.astype(BF16)
        dmix_ref[...] = dmixb
        dmixin = _mm_nt(dmixb, wo[...])
        sg_s, sg_a = f["sg_s"], f["sg_a"]
        dso_ref[...] = dmixin * sg_s
        daoutb = (dmixin * sg_a).astype(BF16)
        daout_ref[...] = daoutb
        dgate_ref[:, :1024] = dmixin * s_out * sg_s * (1.0 - sg_s)
        dgate_ref[:, 1024:] = dmixin * f["a_out"] * sg_a * (1.0 - sg_a)
        d_o = _mm_nt(daoutb, woa[...])
        do_ref[...] = d_o.astype(BF16)
        lane = lax.broadcasted_iota(jnp.int32, (o.shape[0], HEAD_PAD), 1)
        dlc = jnp.zeros((o.shape[0], HEAD_PAD), F32)
        for hh in range(MLA_HEADS):
            sl = slice(hh * HEAD_PAD, (hh + 1) * HEAD_PAD)
            dl = jnp.sum(d_o[:, sl] * o[:, sl], axis=1, keepdims=True)
            dlc = dlc + jnp.where(lane == hh, dl, 0.0)
        dl_ref[...] = dlc

    s = o.shape[0]
    m = memk.shape[0]
    return _row_call(
        "post_bwd", body, s, min(ROW_TILE, s), [dh2, o, s_out, gate, h0], [memk, memv, *lns],
        [w_oa, w_o, w_xq, w_xo],
        [(1024, BF16), (HEAD_PAD, F32), (1024, F32), (2048, F32), (1024, F32),
         (1024, BF16), (1024, BF16), (1024, BF16), (1024, BF16)],
        [(m, 1024), (m, 1024), (1, 1024), (1, 1024), (1, 1024), (1, 1024)])


def _s5out_bwd(d_so, yl, z3, d_skip, w_c, w_glu):
    def body(dso_ref, yl_ref, z3_ref, d_ref, wc, wglu, gh_ref, dup_ref, yg_ref, dy12_ref, dyl_ref, dd_ref):
        @pl.when(pl.program_id(0) == 0)
        def _():
            dd_ref[...] = jnp.zeros_like(dd_ref)

        yl = yl_ref[...]
        ygb = _gelu(yl).astype(BF16)
        yg_ref[...] = ygb
        y12 = _mm(ygb, wglu[...])
        sg = _sigmoid(y12[:, 1024:])
        dso = dso_ref[...]
        dy12b = jnp.concatenate([dso * sg, dso * y12[:, :1024] * sg * (1.0 - sg)], axis=1).astype(BF16)
        dy12_ref[...] = dy12b
        dyl = _mm_nt(dy12b, wglu[...]) * _gelu_grad(yl)
        dylb = dyl.astype(BF16)
        dyl_ref[...] = dylb
        gh_ref[...] = _mm_nt(dylb, wc[...])
        dup_ref[...] = dyl * d_ref[...]
        dd_ref[...] += jnp.sum(dyl * z3_ref[:, :256], axis=0, keepdims=True)

    s = d_so.shape[0]
    return _row_call("s5out_bwd", body, s, min(ROW_TILE, s), [d_so, yl, z3], [d_skip], [w_c, w_glu],
                     [(2 * NS, F32), (256, F32), (256, BF16), (2048, BF16), (256, BF16)], [(1, 256)])


def _in_bwd(x, tab, z3, dq, dk, dv, lam, du_p, dgate, dh0p, ln_g, ln_b, qg, kvg, w_inx, w_b, w_q2, w_kv2):
    def body(x_ref, tab_ref, z3_ref, dq_ref, dk_ref, dv_ref, lam_ref, dup_ref, dgate_ref, dh0p_ref,
             lng_ref, lnb_ref, qg_ref, kvg_ref, winx, wb, wq2, wkv2,
             dx_ref, h0b_ref, dz_ref, ub_ref, cqn_ref, ckvn_ref, dq2_ref, dkv2_ref,
             dlng_ref, dlnb_ref, dqg_ref, dkvg_ref):
        @pl.when(pl.program_id(0) == 0)
        def _():
            for r in (dlng_ref, dlnb_ref, dqg_ref, dkvg_ref):
                r[...] = jnp.zeros_like(r)

        h0, xh0, rs0 = _ln_fwd(x_ref[...], lng_ref[...], lnb_ref[...])
        h0b_ref[...] = h0.astype(BF16)
        z3 = z3_ref[...]
        ub_ref[...] = z3[:, :256].astype(BF16)
        tab_v = tab_ref[...]
        cq1, cq2, ck1 = tab_v[:, :128], tab_v[:, 128:256], tab_v[:, 256:384]
        du = _mm_nt(lam_ref[...], wb[...]) + dup_ref[...]
        dq = dq_ref[...]
        dq2b = jnp.concatenate([dq * _tile_heads(cq1), dq * _tile_heads(cq2)], axis=1).astype(BF16)
        dq2_ref[...] = dq2b
        cqg, cqn, rq = _rms_fwd(z3[:, 256:512], qg_ref[...])
        cqn_ref[...] = cqg.astype(BF16)
        dcq, dqg = _rms_bwd(_mm_nt(dq2b, wq2[...]), cqn, rq, qg_ref[...])
        dqg_ref[...] += dqg
        dk = dk_ref[...]
        dkv2b = jnp.concatenate([dk, dv_ref[...]], axis=1).astype(BF16)
        dkv2_ref[...] = dkv2b
        ckvg, ckvn, rkv = _rms_fwd(z3[:, 512:768], kvg_ref[...])
        ckvn_ref[...] = ckvg.astype(BF16)
        dckv, dkvg = _rms_bwd(_mm_nt(dkv2b, wkv2[...]), ckvn, rkv, kvg_ref[...])
        dkvg_ref[...] += dkvg
        dkrp = dk[:, :HEAD_PAD]
        for hh in range(1, MLA_HEADS):
            dkrp = dkrp + dk[:, hh * HEAD_PAD:(hh + 1) * HEAD_PAD]
        dzb = jnp.concatenate([du, dcq, dckv, dkrp * ck1, dkrp * cq2, dgate_ref[...]], axis=1).astype(BF16)
        dz_ref[...] = dzb
        dh0 = _mm_nt(dzb, winx[...]) + dh0p_ref[...]
        dx, dg, db = _ln_bwd(dh0, xh0, rs0, lng_ref[...])
        dx_ref[...] = dx
        dlng_ref[...] += dg
        dlnb_ref[...] += db

    s = x.shape[0]
    return _row_call(
        "in_bwd", body, s, min(ROW_TILE, s), [x, tab, z3, dq, dk, dv, lam, du_p, dgate, dh0p],
        [ln_g, ln_b, qg, kvg], [w_inx, w_b, w_q2, w_kv2],
        [(1024, F32), (1024, BF16), (3072, BF16), (256, BF16), (256, BF16), (256, BF16), (2048, BF16), (2048, BF16)],
        [(1, 1024), (1, 1024), (1, 256), (1, 256)])


def _wgrad(name, xs, dy, out_dtype=F32, blocks=1):
    s, k = xs.shape
    n = dy.shape[1]
    nb = n // blocks
    ts, tk, tn = min(s, 1024), min(k, 512), min(n, 1024)
    per_tile = tn // nb
    assert s % ts == 0 and k % tk == 0 and n % tn == 0 and (blocks == 1 or tn % nb == 0), (name, s, k, n)
    last = s // ts - 1

    def body(x_ref, dy_ref, o_ref, acc):
        @pl.when(pl.program_id(2) == 0)
        def _():
            acc[...] = jnp.zeros_like(acc)

        acc[...] += _mm_tn(x_ref[...], dy_ref[...])

        @pl.when(pl.program_id(2) == last)
        def _():
            if blocks == 1:
                o_ref[...] = acc[...].astype(out_dtype)
            else:
                for d in range(per_tile):
                    o_ref[d] = acc[:, d * nb:(d + 1) * nb].astype(out_dtype)

    if blocks == 1:
        out_shape = jax.ShapeDtypeStruct((k, n), out_dtype)
        out_spec = pl.BlockSpec((tk, tn), lambda a, b, c: (a, b))
    else:
        out_shape = jax.ShapeDtypeStruct((blocks, k, nb), out_dtype)
        out_spec = pl.BlockSpec((per_tile, tk, nb), lambda a, b, c: (b, a, 0))
    return pl.pallas_call(
        body, name=name, grid=(k // tk, n // tn, s // ts),
        in_specs=[pl.BlockSpec((ts, tk), lambda a, b, c: (c, a)), pl.BlockSpec((ts, tn), lambda a, b, c: (c, b))],
        out_specs=out_spec, out_shape=out_shape, scratch_shapes=[pltpu.VMEM((tk, tn), F32)],
        compiler_params=pltpu.CompilerParams(dimension_semantics=("arbitrary", "arbitrary", "arbitrary"),
                                             vmem_limit_bytes=VMEM_LIMIT),
    )(xs, dy)


def _mesh_pos():
    x, y, c = lax.axis_index("x"), lax.axis_index("y"), lax.axis_index("c")
    return x, y, c


def _peer(x, y, c, k):
    px = 1 - x if k & 4 else x
    py = 1 - y if k & 2 else y
    pc = 1 - c if k & 1 else c
    return (px, py, pc), 4 * px + 2 * py + pc


def _all_to_all(name, arrays, gather):
    n = len(arrays)

    def body(*refs):
        src, dst = refs[:n], refs[n:2 * n]
        send_sems, recv_sems, local_sems = refs[2 * n:]
        x, y, c = _mesh_pos()
        me = 4 * x + 2 * y + c

        def block(i, d):
            return src[i] if gather else src[i].at[d]

        own = [pltpu.make_async_copy(block(i, me), dst[i].at[me], local_sems.at[i]) for i in range(n)]
        for cp in own:
            cp.start()
        sends = []
        for k in range(1, N_DEV):
            peer, pid = _peer(x, y, c, k)
            for i in range(n):
                idx = (k - 1) * n + i
                cp = pltpu.make_async_remote_copy(src_ref=block(i, pid), dst_ref=dst[i].at[me],
                                                  send_sem=send_sems.at[idx], recv_sem=recv_sems.at[idx],
                                                  device_id=peer, device_id_type=MESH)
                cp.start()
                sends.append(cp)
        for k in range(1, N_DEV):
            peer, pid = _peer(x, y, c, k)
            for i in range(n):
                idx = (k - 1) * n + i
                pltpu.make_async_remote_copy(src_ref=block(i, pid), dst_ref=dst[i].at[pid],
                                             send_sem=send_sems.at[idx], recv_sem=recv_sems.at[idx],
                                             device_id=peer, device_id_type=MESH).wait_recv()
        for cp in sends:
            cp.wait_send()
        for cp in own:
            cp.wait()

    n_sem = n * (N_DEV - 1)
    return pl.pallas_call(
        body, name=name,
        in_specs=[pl.BlockSpec(memory_space=pl.ANY)] * n,
        out_specs=[pl.BlockSpec(memory_space=pl.ANY)] * n,
        out_shape=[jax.ShapeDtypeStruct((N_DEV,) + a.shape[-2:], a.dtype) for a in arrays],
        scratch_shapes=[pltpu.SemaphoreType.DMA((n_sem,)), pltpu.SemaphoreType.DMA((n_sem,)),
                        pltpu.SemaphoreType.DMA((n,))],
    )(*arrays)


_HBM = pl.BlockSpec(memory_space=pltpu.HBM)
_SEM = pl.BlockSpec(memory_space=pltpu.SEMAPHORE)


def _exchange_start(name, arrays, gather, after=None):
    n = len(arrays)
    n_sem = n * (N_DEV - 1)
    me = 4 * lax.axis_index("x") + 2 * lax.axis_index("y") + lax.axis_index("c")
    lands = []
    for a in arrays:
        own = a[None] if gather else lax.dynamic_slice_in_dim(a, me, 1, 0)
        lands.append(lax.dynamic_update_slice(lax.empty((N_DEV,) + a.shape[-2:], a.dtype), own, (me, 0, 0)))
    n_after = 0 if after is None else 1

    def body(*refs):
        src, land = refs[:n], refs[n:2 * n]
        send_sems, recv_sems = refs[2 * n + n_after], refs[2 * n + n_after + 1]
        token = refs[-1]
        x, y, c = _mesh_pos()
        me_in = 4 * x + 2 * y + c
        for k in range(1, N_DEV):
            peer, pid = _peer(x, y, c, k)
            for i in range(n):
                idx = (k - 1) * n + i
                pltpu.make_async_remote_copy(src_ref=src[i] if gather else src[i].at[pid], dst_ref=land[i].at[me_in],
                                             send_sem=send_sems.at[idx], recv_sem=recv_sems.at[idx],
                                             device_id=peer, device_id_type=MESH).start()
        token[...] = jnp.zeros_like(token)

    operands = [pltpu.with_memory_space_constraint(a, pltpu.HBM) for a in list(arrays) + lands]
    outs = pl.pallas_call(
        body, name=name,
        out_shape=(pltpu.SemaphoreType.DMA((n_sem,)), pltpu.SemaphoreType.DMA((n_sem,)),
                   *[pltpu.HBM(a.shape, a.dtype) for a in list(arrays) + lands],
                   jax.ShapeDtypeStruct((8, 128), F32)),
        in_specs=[_HBM] * (2 * n) + [pl.BlockSpec(memory_space=pl.ANY)] * n_after,
        out_specs=(_SEM, _SEM, *[_HBM] * (2 * n), pl.BlockSpec(memory_space=pltpu.VMEM)),
        input_output_aliases={i: 2 + i for i in range(2 * n)},
        compiler_params=pltpu.CompilerParams(has_side_effects=pltpu.SideEffectType.DATAFLOW_SIDE_EFFECTING),
    )(*operands, *([after] if n_after else []))
    return (gather, outs[0], outs[1], outs[2:2 + n], outs[2 + n:2 + 2 * n]), outs[-1]


def _exchange_wait(name, handle, after):
    gather, send_sems, recv_sems, srcs, lands = handle
    n = len(srcs)

    def body(*refs):
        src, land = refs[:n], refs[n:2 * n]
        s_sems, r_sems = refs[2 * n], refs[2 * n + 1]
        x, y, c = _mesh_pos()
        for k in range(1, N_DEV):
            peer, pid = _peer(x, y, c, k)
            for i in range(n):
                idx = (k - 1) * n + i
                cp = pltpu.make_async_remote_copy(src_ref=src[i] if gather else src[i].at[pid], dst_ref=land[i].at[pid],
                                                  send_sem=s_sems.at[idx], recv_sem=r_sems.at[idx],
                                                  device_id=peer, device_id_type=MESH)
                cp.wait_send()
                cp.wait_recv()

    outs = pl.pallas_call(
        body, name=name,
        out_shape=tuple(pltpu.HBM(a.shape, a.dtype) for a in list(srcs) + list(lands)),
        in_specs=[_HBM] * (2 * n) + [_SEM, _SEM, pl.BlockSpec(memory_space=pl.ANY)],
        out_specs=tuple([_HBM] * (2 * n)),
        input_output_aliases={i: i for i in range(2 * n)},
        compiler_params=pltpu.CompilerParams(has_side_effects=pltpu.SideEffectType.DATAFLOW_SIDE_EFFECTING),
    )(*srcs, *lands, send_sems, recv_sems, after)
    return list(outs[n:])


def _adamw(name, parts, w, m, v):
    a_rows, b_cols = w.shape
    ta = min(a_rows, ADAM_TILE)
    assert a_rows % ta == 0
    c1 = 1.0 - ADAM_B1 ** ADAM_STEP
    c2 = 1.0 - ADAM_B2 ** ADAM_STEP

    def body(p_ref, w_ref, m_ref, v_ref, g_ref, d_ref, mo_ref, vo_ref):
        g = p_ref[0].astype(F32)
        for d in range(1, N_DEV):
            g = g + p_ref[d].astype(F32)
        g_ref[...] = g
        mn = ADAM_B1 * m_ref[...] + (1.0 - ADAM_B1) * g
        vn = ADAM_B2 * v_ref[...] + (1.0 - ADAM_B2) * (g * g)
        mo_ref[...] = mn
        vo_ref[...] = vn
        d_ref[...] = -ADAM_LR * ((mn / c1) / (jnp.sqrt(vn / c2) + ADAM_EPS) + ADAM_WD * w_ref[...])

    row = pl.BlockSpec((ta, b_cols), lambda i: (i, 0))
    return pl.pallas_call(
        body, name=name, grid=(a_rows // ta,),
        in_specs=[pl.BlockSpec((N_DEV, ta, b_cols), lambda i: (0, i, 0)), row, row, row],
        out_specs=[row] * 4,
        out_shape=[jax.ShapeDtypeStruct((a_rows, b_cols), F32)] * 4,
        compiler_params=pltpu.CompilerParams(dimension_semantics=("arbitrary",), vmem_limit_bytes=VMEM_LIMIT),
    )(parts, w, m, v)


def _pack_rows(arrays, rows):
    flat = jnp.concatenate([a.reshape(-1) for a in arrays])
    return jnp.pad(flat, (0, rows * LANES - flat.shape[0])).reshape(rows, LANES)


def _cols_from_blocks(g):
    return g.transpose(1, 0, 2).reshape(g.shape[1], N_DEV * g.shape[2])


def _blocks_from_cols(w):
    return w.reshape(w.shape[0], N_DEV, w.shape[1] // N_DEV).transpose(1, 0, 2)


def _unpack_flat(flat2d, shapes):
    flat = flat2d.reshape(-1)
    out, off = [], 0
    for shp in shapes:
        sz = math.prod(shp)
        out.append(flat[off:off + sz].reshape(shp))
        off += sz
    return out


def _s5_mats(lam_re, lam_im, log_dt, b_re, b_im, c_re, c_im):
    lr = jnp.minimum(lam_re, S5_MAX_RE)
    li = lam_im
    dt = jnp.exp(log_dt)[:, None]
    mag = jnp.exp(lr * dt)
    ang = li * dt
    ab_re = mag * jnp.cos(ang)
    ab_im = mag * jnp.sin(ang)
    den = lr * lr + li * li
    nr = ab_re - 1.0
    f_re = ((nr * lr + ab_im * li) / den)[..., None]
    f_im = ((ab_im * lr - nr * li) / den)[..., None]
    bb_re = f_re * b_re - f_im * b_im
    bb_im = f_re * b_im + f_im * b_re
    eye = jnp.eye(S5_GROUPS, dtype=F32)
    a = jnp.stack([ab_re.reshape(-1), ab_im.reshape(-1)])
    wb = jnp.concatenate([jnp.einsum("gph,gk->ghkp", bb_re, eye).reshape(S5_WIDTH, NS),
                          jnp.einsum("gph,gk->ghkp", bb_im, eye).reshape(S5_WIDTH, NS)], axis=1)
    wc = jnp.concatenate([jnp.einsum("ghp,gk->gpkh", c_re, eye).reshape(NS, S5_WIDTH),
                          -jnp.einsum("ghp,gk->gpkh", c_im, eye).reshape(NS, S5_WIDTH)], axis=0)
    return a, wb, wc


def _power_table(a):
    ar, ai = a[0], a[1]
    rows_r, rows_i = [ar], [ai]
    for _ in range(7):
        pr, pi = rows_r[-1], rows_i[-1]
        rows_r.append(pr * ar - pi * ai)
        rows_i.append(pr * ai + pi * ar)
    return jnp.concatenate([jnp.stack(rows_r), jnp.stack(rows_i)], axis=1)


def _rope_table(positions):
    inv = ROPE_THETA ** (-jnp.arange(0, MLA_ROPE, 2, dtype=F32) / MLA_ROPE)
    ang = positions.astype(F32)[:, None] * inv
    cos, sin = jnp.cos(ang), jnp.sin(ang)
    s = positions.shape[0]
    one, zero, pad = jnp.ones((s, MLA_NOPE), F32), jnp.zeros((s, MLA_NOPE), F32), jnp.zeros((s, 32), F32)
    return jnp.concatenate([one, cos, cos, pad, zero, -sin, sin, pad, zero, cos, cos, pad], axis=1)


def _derived_weights(full):
    w_in = full["w_in"]
    k1, k2 = w_in[:, 768:784], w_in[:, 784:800]
    z64, z32 = jnp.zeros((1024, 64), BF16), jnp.zeros((1024, 32), BF16)
    w_inx = jnp.concatenate([w_in[:, :768], z64, k1, k2, z32, z64, k2, k1, z32, w_in[:, 800:]], axis=1)
    uq = full["w_uq"].reshape(256, MLA_HEADS, MLA_QK)
    nope, r1, r2 = uq[:, :, :64], uq[:, :, 64:80], uq[:, :, 80:]
    zq64, zq32 = jnp.zeros((256, MLA_HEADS, 64), BF16), jnp.zeros((256, MLA_HEADS, 32), BF16)
    w_q2 = jnp.concatenate([jnp.concatenate([nope, r1, r2, zq32], axis=2).reshape(256, 1024),
                            jnp.concatenate([zq64, r2, r1, zq32], axis=2).reshape(256, 1024)], axis=1)
    ukv = full["w_ukv"].reshape(256, MLA_HEADS, 128)
    w_kv2 = jnp.concatenate([jnp.concatenate([ukv[:, :, :64], zq64], axis=2).reshape(256, 1024),
                             jnp.concatenate([ukv[:, :, 64:], zq64], axis=2).reshape(256, 1024)], axis=1)
    return w_inx, w_q2, w_kv2


def _oa_padded(w_oa):
    oa = w_oa.reshape(MLA_HEADS, MLA_V, 1024)
    return jnp.concatenate([oa, jnp.zeros_like(oa)], axis=1).reshape(1024, 1024)


def _fold_grads(d_inx, d_q2, d_kv2):
    d_k1 = d_inx[:, 832:848] + d_inx[:, 976:992]
    d_k2 = d_inx[:, 848:864] + d_inx[:, 960:976]
    d_w_in = jnp.concatenate([d_inx[:, :768], d_k1, d_k2, d_inx[:, 1024:]], axis=1)
    a = d_q2[:, :1024].reshape(256, MLA_HEADS, 128)
    b = d_q2[:, 1024:].reshape(256, MLA_HEADS, 128)
    d_w_uq = jnp.concatenate([a[:, :, :64], a[:, :, 64:80] + b[:, :, 80:96], a[:, :, 80:96] + b[:, :, 64:80]],
                             axis=2).reshape(256, MLA_HEADS * MLA_QK)
    kk = d_kv2[:, :1024].reshape(256, MLA_HEADS, 128)
    vv = d_kv2[:, 1024:].reshape(256, MLA_HEADS, 128)
    d_w_ukv = jnp.concatenate([kk[:, :, :64], vv[:, :, :64]], axis=2).reshape(256, 1024)
    return d_w_in, d_w_uq, d_w_ukv


def kernel(x, mem, positions, ln_in_g, ln_in_b, w_in, s5_lam_re, s5_lam_im, s5_log_dt, s5_b_re, s5_b_im, s5_c_re, s5_c_im, s5_d, w_glu, q_norm_g, w_uq, kv_norm_g, w_ukv, w_oa, w_o, ln1_g, ln1_b, w_xq, w_xk, w_xv, w_xo, ln2_g, ln2_b, w_up, w_down, ln3_g, ln3_b, loss_target, m_ln_in_g, m_ln_in_b, m_w_in, m_s5_lam_re, m_s5_lam_im, m_s5_log_dt, m_s5_b_re, m_s5_b_im, m_s5_c_re, m_s5_c_im, m_s5_d, m_w_glu, m_q_norm_g, m_w_uq, m_kv_norm_g, m_w_ukv, m_w_oa, m_w_o, m_ln1_g, m_ln1_b, m_w_xq, m_w_xk, m_w_xv, m_w_xo, m_ln2_g, m_ln2_b, m_w_up, m_w_down, m_ln3_g, m_ln3_b, v_ln_in_g, v_ln_in_b, v_w_in, v_s5_lam_re, v_s5_lam_im, v_s5_log_dt, v_s5_b_re, v_s5_b_im, v_s5_c_re, v_s5_c_im, v_s5_d, v_w_glu, v_q_norm_g, v_w_uq, v_kv_norm_g, v_w_ukv, v_w_oa, v_w_o, v_ln1_g, v_ln1_b, v_w_xq, v_w_xk, v_w_xv, v_w_xo, v_ln2_g, v_ln2_b, v_w_up, v_w_down, v_ln3_g, v_ln3_b):
    args = dict(locals())
    wts = {n: args[n] for n in WEIGHTS}
    mom = {n: args["m_" + n] for n in WEIGHTS}
    vel = {n: args["v_" + n] for n in WEIGHTS}
    xs, mems, tgt = x[0], mem[0], loss_target[0]
    s = xs.shape[0]
    names_big = [n for (n, _, _, _) in SHARDED]
    kind = {n: kd for (n, kd, _, _) in SHARDED}
    small_rows = _round_up(sum(math.prod(wts[n].shape) for n in SMALL), 8 * LANES) // LANES

    late = [n for n in names_big if n not in EARLY]
    shard = {n: wts[n][0].astype(BF16) for n in names_big}
    g_early = dict(zip(EARLY, _all_to_all("all_gather", [shard[n] for n in EARLY], True)))
    ag_handle, ag_token = _exchange_start("all_gather_late", [shard[n] for n in late], True, after=g_early["w_in"])
    full = {n: _cols_from_blocks(g_early[n]) for n in EARLY}
    w_inx, w_q2, w_kv2 = _derived_weights(full)
    s5_args = (s5_lam_re[0], s5_lam_im[0], s5_log_dt[0], s5_b_re[0], s5_b_im[0], s5_c_re[0], s5_c_im[0])
    (a_mat, w_b, w_c), s5_vjp = jax.vjp(_s5_mats, *s5_args)
    w_bb, w_cb = w_b.astype(BF16), w_c.astype(BF16)
    pw = _power_table(a_mat)
    pwb = _power_table(a_mat * jnp.array([[1.0], [-1.0]], F32))[::-1]
    tab = _rope_table(positions[0])
    row = lambda a: a.reshape(1, -1)
    ln_g, ln_b = row(ln_in_g) + ag_token[0:1, 0:1], row(ln_in_b)
    lns = [ln1_g, ln1_b, ln2_g, ln2_b]

    h0, z3, gate, bu, q, k, v = _in_fwd(xs, tab, ln_g, ln_b, q_norm_g, kv_norm_g, w_inx, w_bb, w_q2, w_kv2)
    h = _scan_fwd(pw, bu)
    yl, s_out = _s5out_fwd(h, z3, s5_d, w_cb, full["w_glu"])
    o, lse_t = _attn_fwd(q, k, v.T)
    g_late = dict(zip(late, _exchange_wait("all_gather_late_wait", ag_handle, after=lse_t)))
    full.update({n: g_late[n].reshape(-1, g_late[n].shape[2]) if kind[n] == "row" else _cols_from_blocks(g_late[n])
                 for n in late})
    w_oap = _oa_padded(full["w_oa"])
    memk, memv = _mem_kv(mems, full["w_xk"], full["w_xv"])
    r2, mixin_b, h1_b, ox_b = _post_fwd(o, s_out, gate, h0, memk, memv, lns,
                                        w_oap, full["w_o"], full["w_xq"], full["w_xo"])
    dh2, h2_b, dup_b, act_b, dff_b, loss_acc, d_ln3_g, d_ln3_b = _mlp(
        r2, tgt, ln2_g, ln2_b, ln3_g, ln3_b, full["w_up"], full["w_down"])
    (do_b, delta, d_so, dgate, dh0p, daout_b, dmix_b, dqx_b, dxa_b,
     dmk, dmv, d_ln1_g, d_ln1_b, d_ln2_g, d_ln2_b) = _post_bwd(
        dh2, o, s_out, gate, h0, memk, memv, lns, w_oap, full["w_o"], full["w_xq"], full["w_xo"])
    rows8 = lambda g: g.reshape(N_DEV, g.shape[0] // N_DEV, g.shape[1])
    d_w_oa = _wgrad("wg_oa", o, daout_b).reshape(MLA_HEADS, 128, 1024)[:, :64].reshape(512, 1024)
    send = {
        "w_oa": _blocks_from_cols(d_w_oa).astype(BF16),
        "w_up": _wgrad("wg_up", h2_b, dup_b, BF16, N_DEV),
        "w_o": rows8(_wgrad("wg_o", mixin_b, dmix_b, BF16)),
        "w_xq": rows8(_wgrad("wg_xq", h1_b, dqx_b, BF16)),
        "w_xk": rows8(_wgrad("wg_xk", mems, dmk, BF16)),
        "w_xv": rows8(_wgrad("wg_xv", mems, dmv, BF16)),
        "w_xo": rows8(_wgrad("wg_xo", ox_b, dxa_b, BF16)),
        "w_down": rows8(_wgrad("wg_down", act_b, dff_b, BF16)),
    }
    rs_handle, rs_token = _exchange_start("grad_exchange_late", [send[n] for n in late], False)
    dl_t = delta.T[:MLA_HEADS].reshape(MLA_HEADS, 1, s) + rs_token[0, 0]
    dq, dk, dv = _attn_bwd(q, k, v, k.T, do_b, lse_t, dl_t)
    gh, du_p, yg_b, dy12_b, dyl_b, d_s5_d = _s5out_bwd(d_so, yl, z3, s5_d, w_cb, full["w_glu"])
    lam, d_a = _scan_bwd(pwb, gh, h)
    (dx, h0_b, dz_b, u_b, cqn_b, ckvn_b, dq2_b, dkv2_b, d_ln_g, d_ln_b, d_qg, d_kvg) = _in_bwd(
        xs, tab, z3, dq, dk, dv, lam, du_p, dgate, dh0p, ln_g, ln_b, q_norm_g, kv_norm_g, w_inx, w_bb, w_q2, w_kv2)

    d_w_in, d_w_uq, d_w_ukv = _fold_grads(
        _wgrad("wg_in", h0_b, dz_b), _wgrad("wg_q", cqn_b, dq2_b), _wgrad("wg_kv", ckvn_b, dkv2_b))
    send.update({
        "w_in": _blocks_from_cols(d_w_in).astype(BF16), "w_uq": _blocks_from_cols(d_w_uq).astype(BF16),
        "w_ukv": _blocks_from_cols(d_w_ukv).astype(BF16),
        "w_glu": _wgrad("wg_glu", yg_b, dy12_b, BF16, N_DEV),
    })
    d_s5 = s5_vjp((d_a.reshape(2, NS), _wgrad("wg_s5b", u_b, lam), _wgrad("wg_s5c", h, dyl_b)))
    small_grads = {
        "ln_in_g": d_ln_g, "ln_in_b": d_ln_b, "s5_lam_re": d_s5[0], "s5_lam_im": d_s5[1], "s5_log_dt": d_s5[2],
        "s5_b_re": d_s5[3], "s5_b_im": d_s5[4], "s5_c_re": d_s5[5], "s5_c_im": d_s5[6], "s5_d": d_s5_d,
        "q_norm_g": d_qg, "kv_norm_g": d_kvg, "ln1_g": d_ln1_g, "ln1_b": d_ln1_b, "ln2_g": d_ln2_g,
        "ln2_b": d_ln2_b, "ln3_g": d_ln3_g, "ln3_b": d_ln3_b,
    }
    small_send = jnp.broadcast_to(_pack_rows([small_grads[n] for n in SMALL], small_rows)[None],
                                  (N_DEV, small_rows, LANES))

    recv_early = _all_to_all("grad_exchange", [send[n] for n in EARLY] + [small_send], False)
    recv_late = _exchange_wait("grad_exchange_late_wait", rs_handle, after=recv_early[-1])
    recv = dict(zip(list(EARLY) + late, list(recv_early[:-1]) + recv_late))
    results = [dict(), dict(), dict(), dict()]
    for n in names_big:
        parts = recv[n]
        outs = _adamw("adamw_" + n, parts, wts[n][0], mom[n][0], vel[n][0])
        for res, a in zip(results, outs):
            res[n] = a[None]
    small_out = _adamw("adamw_small", recv_early[-1],
                       *[_pack_rows([t[n] for n in SMALL], small_rows) for t in (wts, mom, vel)])
    for res, fs in zip(results, small_out):
        for n, a in zip(SMALL, _unpack_flat(fs, [wts[n].shape for n in SMALL])):
            res[n] = a

    loss = lax.psum(loss_acc[0, 0], AXES)
    return (loss, dx[None], *[res[n] for res in results for n in WEIGHTS])
```

```python
import math

import jax
import jax.numpy as jnp
from jax import lax
from jax.experimental import pallas as pl
from jax.experimental.pallas import tpu as pltpu

F32 = jnp.float32
BF16 = jnp.bfloat16

D_MODEL = 1024
S5_WIDTH = 256
S5_GROUP_CH = 16
S5_GROUPS = 16
S5_STATE = 64
NS = S5_GROUPS * S5_STATE
S5_MAX_RE = -1e-4
MLA_HEADS = 8
MLA_NOPE = 64
MLA_ROPE = 32
MLA_QK = 96
MLA_V = 64
HEAD_PAD = 128
ROPE_THETA = 10000.0
XATTN_HEADS = 4
XATTN_HD = 256
MLP_HIDDEN = 4096
LN_EPS = 1e-5
RMS_EPS = 1e-6
NEG_INF = -1e30
LOG2E = 1.4426950408889634
DN_ALPHA = 2.0 ** 0.25
ADAM_LR = 0.001
ADAM_B1 = 0.9
ADAM_B2 = 0.999
ADAM_EPS = 1e-08
ADAM_WD = 0.01
ADAM_STEP = 10

N_DEV = 8
AXES = ("x", "y", "c")
MESH = pl.DeviceIdType.MESH
LANES = 1024
VMEM_LIMIT = 60 * 1024 * 1024

ROW_TILE = 256
SCAN_TILE = 1024
ATT_TILE = 512
ATT_SPLIT = 1
ADAM_TILE = 128

SHARDED = (
    ("w_in", "col", 1024, 2848), ("w_glu", "col", 256, 2048), ("w_uq", "col", 256, 768),
    ("w_ukv", "col", 256, 1024), ("w_oa", "col", 512, 1024), ("w_o", "row", 1024, 1024),
    ("w_xq", "row", 1024, 1024), ("w_xk", "row", 1024, 1024), ("w_xv", "row", 1024, 1024),
    ("w_xo", "row", 1024, 1024), ("w_up", "col", 1024, 4096), ("w_down", "row", 4096, 1024),
)
EARLY = ("w_in", "w_glu", "w_uq", "w_ukv")
SMALL = ("ln_in_g", "ln_in_b", "s5_lam_re", "s5_lam_im", "s5_log_dt", "s5_b_re", "s5_b_im", "s5_c_re",
         "s5_c_im", "s5_d", "q_norm_g", "kv_norm_g", "ln1_g", "ln1_b", "ln2_g", "ln2_b", "ln3_g", "ln3_b")
WEIGHTS = ("ln_in_g", "ln_in_b", "w_in", "s5_lam_re", "s5_lam_im", "s5_log_dt", "s5_b_re", "s5_b_im",
           "s5_c_re", "s5_c_im", "s5_d", "w_glu", "q_norm_g", "w_uq", "kv_norm_g", "w_ukv", "w_oa", "w_o",
           "ln1_g", "ln1_b", "w_xq", "w_xk", "w_xv", "w_xo", "ln2_g", "ln2_b", "w_up", "w_down", "ln3_g", "ln3_b")


def _round_up(n, m):
    return (n + m - 1) // m * m


def _bf(a):
    return a.astype(BF16)


def _mm(a, b):
    return jnp.dot(_bf(a), _bf(b), preferred_element_type=F32)


def _mm_nt(a, b):
    return lax.dot_general(_bf(a), _bf(b), (((1,), (1,)), ((), ())), preferred_element_type=F32)


def _mm_tn(a, b):
    return lax.dot_general(_bf(a), _bf(b), (((0,), (0,)), ((), ())), preferred_element_type=F32)


def _sigmoid(a):
    return 1.0 / (1.0 + jnp.exp(-a))


def _gelu(a):
    return 0.5 * a * (1.0 + lax.erf(a * (2.0 ** -0.5)))


def _gelu_grad(a):
    return 0.5 * (1.0 + lax.erf(a * (2.0 ** -0.5))) + a * jnp.exp(-0.5 * a * a) * (1.0 / math.sqrt(2.0 * math.pi))


def _ln_fwd(a, g, b):
    mu = jnp.mean(a, axis=-1, keepdims=True)
    ac = a - mu
    var = jnp.mean(ac * ac, axis=-1, keepdims=True)
    rstd = lax.rsqrt(var + LN_EPS)
    xhat = ac * rstd
    return xhat * g + b, xhat, rstd


def _ln_bwd(dy, xhat, rstd, g):
    dxh = dy * g
    m1 = jnp.mean(dxh, axis=-1, keepdims=True)
    m2 = jnp.mean(dxh * xhat, axis=-1, keepdims=True)
    dx = rstd * (dxh - m1 - xhat * m2)
    return dx, jnp.sum(dy * xhat, axis=0, keepdims=True), jnp.sum(dy, axis=0, keepdims=True)


def _rms_fwd(a, g):
    r = lax.rsqrt(jnp.mean(a * a, axis=-1, keepdims=True) + RMS_EPS)
    xn = a * r
    return xn * g, xn, r


def _rms_bwd(dy, xn, r, g):
    dxn = dy * g
    dx = r * (dxn - xn * jnp.mean(dxn * xn, axis=-1, keepdims=True))
    return dx, jnp.sum(dy * xn, axis=0, keepdims=True)


def _tile_heads(a):
    return jnp.concatenate([a] * MLA_HEADS, axis=1)


def _row_call(name, body, n_rows, ts, tiled_in, full_in, weights, tiled_out, acc_out, reverse=False, scratch=(),
              tiled_out_t=()):
    n = n_rows // ts
    assert n * ts == n_rows, (name, n_rows, ts)
    nt, nf, nw = len(tiled_in), len(full_in), len(weights)
    nto, nao = len(tiled_out) + len(tiled_out_t), len(acc_out)
    if reverse:
        imap = lambda i: (n - 1 - i, 0)
    else:
        imap = lambda i: (i, 0)
    const = lambda i: (0, 0)

    def kern(*refs):
        ins = refs[:nt + nf]
        w_hbm = refs[nt + nf:nt + nf + nw]
        outs = refs[nt + nf + nw:nt + nf + nw + nto + nao]
        scr = refs[nt + nf + nw + nto + nao:]
        w_vmem = scr[:nw]
        extra = scr[nw + 1:] if nw else scr
        if nw:
            sem = scr[nw]

            @pl.when(pl.program_id(0) == 0)
            def _():
                cps = [pltpu.make_async_copy(w_hbm[k], w_vmem[k], sem.at[k]) for k in range(nw)]
                for cp in cps:
                    cp.start()
                for cp in cps:
                    cp.wait()
        body(*ins, *w_vmem, *outs, *extra)

    in_specs = [pl.BlockSpec((ts, a.shape[1]), imap) for a in tiled_in]
    in_specs += [pl.BlockSpec(a.shape, const) for a in full_in]
    in_specs += [pl.BlockSpec(memory_space=pl.ANY) for _ in weights]
    assert not (reverse and tiled_out_t)
    out_shape = [jax.ShapeDtypeStruct((n_rows, c), dt) for (c, dt) in tiled_out]
    out_shape += [jax.ShapeDtypeStruct((c, n_rows), dt) for (c, dt) in tiled_out_t]
    out_shape += [jax.ShapeDtypeStruct(shp, F32) for shp in acc_out]
    out_specs = [pl.BlockSpec((ts, c), imap) for (c, dt) in tiled_out]
    out_specs += [pl.BlockSpec((c, ts), lambda i: (0, i)) for (c, dt) in tiled_out_t]
    out_specs += [pl.BlockSpec(shp, const) for shp in acc_out]
    scratch_shapes = [pltpu.VMEM(w.shape, w.dtype) for w in weights]
    if nw:
        scratch_shapes.append(pltpu.SemaphoreType.DMA((nw,)))
    scratch_shapes += list(scratch)
    return pl.pallas_call(
        kern, name=name, grid=(n,), in_specs=in_specs, out_specs=out_specs, out_shape=out_shape,
        scratch_shapes=scratch_shapes,
        compiler_params=pltpu.CompilerParams(dimension_semantics=("arbitrary",), vmem_limit_bytes=VMEM_LIMIT),
    )(*tiled_in, *full_in, *weights)


def _mem_kv(mem, w_xk, w_xv):
    m = mem.shape[0]

    def body(mem_ref, wk_ref, wv_ref, k_ref, v_ref):
        mb = mem_ref[...]
        k_ref[...] = _mm(mb, wk_ref[...]).astype(BF16)
        v_ref[...] = _mm(mb, wv_ref[...]).astype(BF16)

    return pl.pallas_call(
        body, name="mem_kv",
        out_shape=[jax.ShapeDtypeStruct((m, D_MODEL), BF16)] * 2,
        compiler_params=pltpu.CompilerParams(vmem_limit_bytes=VMEM_LIMIT),
    )(mem, w_xk, w_xv)


def _in_fwd(x, tab, ln_g, ln_b, qg, kvg, w_inx, w_b, w_q2, w_kv2):
    def body(x_ref, tab_ref, lng_ref, lnb_ref, qg_ref, kvg_ref, winx, wb, wq2, wkv2,
             h0_ref, z3_ref, gate_ref, bu_ref, q_ref, k_ref, v_ref, kt_ref, vt_ref):
        h0, _, _ = _ln_fwd(x_ref[...], lng_ref[...], lnb_ref[...])
        h0_ref[...] = h0
        z = _mm(h0, winx[...])
        z3_ref[...] = z[:, :768]
        gate_ref[...] = z[:, 1024:]
        bu_ref[...] = _mm(z[:, :256], wb[...])
        tab_v = tab_ref[...]
        cq1, cq2, ck1 = tab_v[:, :128], tab_v[:, 128:256], tab_v[:, 256:384]
        cqn, _, _ = _rms_fwd(z[:, 256:512], qg_ref[...])
        q2 = _mm(cqn, wq2[...])
        q_ref[...] = (q2[:, :1024] * _tile_heads(cq1) + q2[:, 1024:] * _tile_heads(cq2)).astype(BF16)
        ckvn, _, _ = _rms_fwd(z[:, 512:768], kvg_ref[...])
        kv2 = _mm(ckvn, wkv2[...])
        krp = z[:, 768:896] * ck1 + z[:, 896:1024] * cq2
        kf = kv2[:, :1024] + _tile_heads(krp)
        k_ref[...] = kf.astype(BF16)
        v_ref[...] = kv2[:, 1024:].astype(BF16)
        kt_ref[...] = kf.T.astype(BF16)
        vt_ref[...] = kv2[:, 1024:].T.astype(BF16)

    s = x.shape[0]
    return _row_call(
        "in_fwd", body, s, min(ROW_TILE, s), [x, tab], [ln_g, ln_b, qg, kvg], [w_inx, w_b, w_q2, w_kv2],
        [(1024, F32), (768, F32), (2048, F32), (2048, F32), (1024, BF16), (1024, BF16), (1024, BF16)], [],
        tiled_out_t=[(1024, BF16), (1024, BF16)])


def _scan_level(row, xr, xi, ar, ai, k, forward):
    if forward:
        keep = row >= k
        sr, si = pltpu.roll(xr, k, 0), pltpu.roll(xi, k, 0)
    else:
        keep = row < 8 - k
        sr, si = pltpu.roll(xr, 8 - k, 0), pltpu.roll(xi, 8 - k, 0)
    sr = jnp.where(keep, sr, 0.0)
    si = jnp.where(keep, si, 0.0)
    return xr + ar * sr - ai * si, xi + ar * si + ai * sr


def _scan_fwd(pw, bu):
    s = bu.shape[0]
    ts = min(SCAN_TILE, s)
    nblk = ts // 8

    def body(bu_ref, pw_ref, h_ref, carry):
        @pl.when(pl.program_id(0) == 0)
        def _():
            carry[...] = jnp.zeros_like(carry)

        row = lax.broadcasted_iota(jnp.int32, (8, NS), 0)
        a1 = (pw_ref[0:1, :NS], pw_ref[0:1, NS:])
        a2 = (pw_ref[1:2, :NS], pw_ref[1:2, NS:])
        a4 = (pw_ref[3:4, :NS], pw_ref[3:4, NS:])

        def blk(b, _):
            r0 = pl.multiple_of(b * 8, 8)
            xr = bu_ref[pl.ds(r0, 8), :NS]
            xi = bu_ref[pl.ds(r0, 8), NS:]
            xr, xi = _scan_level(row, xr, xi, a1[0], a1[1], 1, True)
            xr, xi = _scan_level(row, xr, xi, a2[0], a2[1], 2, True)
            xr, xi = _scan_level(row, xr, xi, a4[0], a4[1], 4, True)
            cr, ci = carry[7:8, :NS], carry[7:8, NS:]
            pr, pi = pw_ref[:, :NS], pw_ref[:, NS:]
            hr = xr + pr * cr - pi * ci
            hi = xi + pr * ci + pi * cr
            h_ref[pl.ds(r0, 8), :NS] = hr
            h_ref[pl.ds(r0, 8), NS:] = hi
            carry[:, :NS] = hr
            carry[:, NS:] = hi
            return 0

        lax.fori_loop(0, nblk, blk, 0)

    return _row_call("scan_fwd", body, s, ts, [bu], [pw], [], [(2 * NS, F32)], [],
                     scratch=[pltpu.VMEM((8, 2 * NS), F32)])[0]


def _scan_bwd(pwb, gh, h):
    s = gh.shape[0]
    ts = min(SCAN_TILE, s)
    nblk = ts // 8
    n_tiles = s // ts

    def body(g_ref, h_ref, pw_ref, lam_ref, da_ref, carry, acc):
        @pl.when(pl.program_id(0) == 0)
        def _():
            carry[...] = jnp.zeros_like(carry)
            acc[...] = jnp.zeros_like(acc)

        row = lax.broadcasted_iota(jnp.int32, (8, NS), 0)
        a1 = (pw_ref[7:8, :NS], pw_ref[7:8, NS:])
        a2 = (pw_ref[6:7, :NS], pw_ref[6:7, NS:])
        a4 = (pw_ref[4:5, :NS], pw_ref[4:5, NS:])

        def blk(bb, _):
            r0 = pl.multiple_of((nblk - 1 - bb) * 8, 8)
            xr = g_ref[pl.ds(r0, 8), :NS]
            xi = g_ref[pl.ds(r0, 8), NS:]
            xr, xi = _scan_level(row, xr, xi, a1[0], a1[1], 1, False)
            xr, xi = _scan_level(row, xr, xi, a2[0], a2[1], 2, False)
            xr, xi = _scan_level(row, xr, xi, a4[0], a4[1], 4, False)
            cr, ci = carry[0:1, :NS], carry[0:1, NS:]
            pr, pi = pw_ref[:, :NS], pw_ref[:, NS:]
            lr = xr + pr * cr - pi * ci
            li = xi + pr * ci + pi * cr
            lam_ref[pl.ds(r0, 8), :NS] = lr
            lam_ref[pl.ds(r0, 8), NS:] = li
            nr = jnp.where(row < 7, pltpu.roll(lr, 7, 0), cr)
            ni = jnp.where(row < 7, pltpu.roll(li, 7, 0), ci)
            hr = h_ref[pl.ds(r0, 8), :NS]
            hi = h_ref[pl.ds(r0, 8), NS:]
            acc[:, :NS] += nr * hr + ni * hi
            acc[:, NS:] += ni * hr - nr * hi
            carry[:, :NS] = lr
            carry[:, NS:] = li
            return 0

        lax.fori_loop(0, nblk, blk, 0)

        @pl.when(pl.program_id(0) == n_tiles - 1)
        def _():
            da_ref[...] = jnp.sum(acc[...], axis=0, keepdims=True)

    return _row_call("scan_bwd", body, s, ts, [gh, h], [pwb], [], [(2 * NS, F32)], [(1, 2 * NS)], reverse=True,
                     scratch=[pltpu.VMEM((8, 2 * NS), F32), pltpu.VMEM((8, 2 * NS), F32)])


def _s5out_fwd(h, z3, d_skip, w_c, w_glu):
    def body(h_ref, z3_ref, d_ref, wc, wglu, yl_ref, so_ref):
        yl = _mm(h_ref[...], wc[...]) + d_ref[...] * z3_ref[:, :256]
        yl_ref[...] = yl
        y12 = _mm(_gelu(yl), wglu[...])
        so_ref[...] = y12[:, :1024] * _sigmoid(y12[:, 1024:])

    s = h.shape[0]
    return _row_call("s5out_fwd", body, s, min(ROW_TILE, s), [h, z3], [d_skip], [w_c, w_glu],
                     [(256, F32), (1024, F32)], [])


def _causal_mask_t(t, w, off):
    row = lax.broadcasted_iota(jnp.int32, (t, w), 0)
    col = lax.broadcasted_iota(jnp.int32, (t, w), 1)
    return row <= col + off


def _attn_fwd(q, k, vt):
    s = q.shape[0]
    t = min(ATT_TILE, s)
    nq = s // t
    scale = MLA_QK ** -0.5
    c2 = scale * LOG2E

    hw = t // ATT_SPLIT

    def body(q_ref, k_ref, vt_ref, o_ref, lse_ref):
        i = pl.program_id(1)
        qs = [q_ref[g * hw:(g + 1) * hw, :] for g in range(ATT_SPLIT)]

        def step(kb, state, masked):
            r0 = pl.multiple_of(kb * t, t)
            kblk = k_ref[pl.ds(r0, t), :]
            vtb = vt_ref[:, pl.ds(r0, t)]
            out = []
            for g in range(ATT_SPLIT):
                m, l, acc = state[g]
                qg = qs[g]
                if g:
                    qg = jnp.where(out[g - 1][0][0:1, 0:1] > -3e38, qg, jnp.zeros_like(qg))
                st = _mm_nt(kblk, qg)
                if masked:
                    st = jnp.where(_causal_mask_t(t, hw, g * hw), st, NEG_INF)
                m_new = jnp.maximum(m, jnp.max(st, axis=0, keepdims=True))
                p = jnp.exp2((st - m_new) * c2)
                a = jnp.exp2((m - m_new) * c2)
                l = a * l + jnp.sum(p, axis=0, keepdims=True)
                acc = a * acc + _mm(vtb, p)
                out.append((m_new, l, acc))
            return tuple(out)

        init = tuple((jnp.full((1, hw), NEG_INF, F32), jnp.zeros((1, hw), F32), jnp.zeros((HEAD_PAD, hw), F32))
                     for _ in range(ATT_SPLIT))
        state = lax.fori_loop(0, i, lambda kb, c: step(kb, c, False), init)
        state = step(i, state, True)
        for g, (m, l, acc) in enumerate(state):
            o_ref[g * hw:(g + 1) * hw, :] = (acc / l).T
            lse_ref[:, g * hw:(g + 1) * hw] = m * scale + jnp.log(l)

    return pl.pallas_call(
        body, name="attn_fwd", grid=(MLA_HEADS, nq),
        in_specs=[pl.BlockSpec((t, HEAD_PAD), lambda h, i: (i, h)),
                  pl.BlockSpec((s, HEAD_PAD), lambda h, i: (0, h)),
                  pl.BlockSpec((HEAD_PAD, s), lambda h, i: (h, 0))],
        out_specs=[pl.BlockSpec((t, HEAD_PAD), lambda h, i: (i, h)),
                   pl.BlockSpec((None, 1, t), lambda h, i: (h, 0, i))],
        out_shape=[jax.ShapeDtypeStruct((s, MLA_HEADS * HEAD_PAD), F32),
                   jax.ShapeDtypeStruct((MLA_HEADS, 1, s), F32)],
        compiler_params=pltpu.CompilerParams(dimension_semantics=("arbitrary", "arbitrary"),
                                             vmem_limit_bytes=VMEM_LIMIT),
    )(q, k, vt)


def _attn_bwd(q, k, v, kt, do, lse_t, dl_t):
    s = q.shape[0]
    t = min(ATT_TILE, s)
    nq = s // t
    scale = MLA_QK ** -0.5
    c2 = scale * LOG2E
    hw = t // ATT_SPLIT

    def body(q_ref, k_ref, v_ref, kt_ref, do_ref, lse_ref, dl_ref, dq_ref, dk_ref, dv_ref, dqt):
        j = pl.program_id(1)

        @pl.when(j == 0)
        def _():
            dqt[...] = jnp.zeros_like(dqt)

        kk = k_ref[...]
        vv = v_ref[...]
        ktb = kt_ref[...]

        def step(ib, carry, masked):
            dk, dv = carry
            cols = [pl.ds(pl.multiple_of(ib * t + g * hw, hw), hw) for g in range(ATT_SPLIT)]
            pt = None
            for g in range(ATT_SPLIT):
                qq, dd = q_ref[cols[g], :], do_ref[cols[g], :]
                if g:
                    ready = pt[0:1, 0:1] >= 0.0
                    qq = jnp.where(ready, qq, jnp.zeros_like(qq))
                    dd = jnp.where(ready, dd, jnp.zeros_like(dd))
                pt = jnp.exp2(_mm_nt(kk, qq) * c2 - lse_ref[:, cols[g]] * LOG2E)
                if masked:
                    pt = jnp.where(_causal_mask_t(t, hw, g * hw), pt, 0.0)
                dv = dv + _mm(pt, dd)
                dst = (pt * (_mm_nt(vv, dd) - dl_ref[:, cols[g]])).astype(BF16)
                dk = dk + _mm(dst, qq)
                dqt[:, cols[g]] += _mm(ktb, dst)
            return dk, dv

        zero = jnp.zeros((t, HEAD_PAD), F32)
        carry = step(j, (zero, zero), True)
        dk, dv = lax.fori_loop(j + 1, nq, lambda ib, c: step(ib, c, False), carry)
        dk_ref[...] = dk * scale
        dv_ref[...] = dv

        @pl.when(j == nq - 1)
        def _():
            for cc in range(nq):
                dq_ref[cc * t:(cc + 1) * t, :] = dqt[:, cc * t:(cc + 1) * t].T * scale

    full = pl.BlockSpec((s, HEAD_PAD), lambda h, j: (0, h))
    tile = pl.BlockSpec((t, HEAD_PAD), lambda h, j: (j, h))
    stat = pl.BlockSpec((None, 1, s), lambda h, j: (h, 0, 0))
    return pl.pallas_call(
        body, name="attn_bwd", grid=(MLA_HEADS, nq),
        in_specs=[full, tile, tile, pl.BlockSpec((HEAD_PAD, t), lambda h, j: (h, j)), full, stat, stat],
        out_specs=[full, tile, tile],
        out_shape=[jax.ShapeDtypeStruct((s, MLA_HEADS * HEAD_PAD), F32)] * 3,
        scratch_shapes=[pltpu.VMEM((HEAD_PAD, s), F32)],
        compiler_params=pltpu.CompilerParams(dimension_semantics=("arbitrary", "arbitrary"),
                                             vmem_limit_bytes=VMEM_LIMIT),
    )(q, k, v, kt, do, lse_t, dl_t)


def _post_forward(o, s_out, gate, h0, memk, memv, w_oa, w_o, w_xq, w_xo, g1, b1, g2, b2):
    a_out = _mm(o, w_oa)
    sg_s = _sigmoid(gate[:, :1024])
    sg_a = _sigmoid(gate[:, 1024:])
    mixin = sg_s * s_out + sg_a * a_out
    r1 = DN_ALPHA * h0 + _mm(mixin, w_o)
    h1, xh1, rs1 = _ln_fwd(r1, g1, b1)
    qxb = _mm(h1, w_xq).astype(BF16)
    ps, oxs = [], []
    for hh in range(XATTN_HEADS):
        sl = slice(hh * XATTN_HD, (hh + 1) * XATTN_HD)
        sc = _mm_nt(qxb[:, sl], memk[:, sl]) * (XATTN_HD ** -0.5)
        e = jnp.exp(sc - jnp.max(sc, axis=1, keepdims=True))
        p = e / jnp.sum(e, axis=1, keepdims=True)
        ps.append(p)
        oxs.append(_mm(p, memv[:, sl]))
    ox = jnp.concatenate(oxs, axis=1)
    r2 = DN_ALPHA * h1 + _mm(ox, w_xo)
    return dict(a_out=a_out, sg_s=sg_s, sg_a=sg_a, mixin=mixin, h1=h1, xh1=xh1, rs1=rs1, qxb=qxb, ps=ps, ox=ox, r2=r2)


def _post_fwd(o, s_out, gate, h0, memk, memv, lns, w_oa, w_o, w_xq, w_xo):
    def body(o_ref, so_ref, gate_ref, h0_ref, mk_ref, mv_ref, g1, b1, g2, b2, woa, wo, wxq, wxo,
             r2_ref, mixin_ref, h1_ref, ox_ref):
        f = _post_forward(o_ref[...], so_ref[...], gate_ref[...], h0_ref[...], mk_ref[...], mv_ref[...],
                          woa[...], wo[...], wxq[...], wxo[...], g1[...], b1[...], g2[...], b2[...])
        r2_ref[...] = f["r2"]
        mixin_ref[...] = f["mixin"].astype(BF16)
        h1_ref[...] = f["h1"].astype(BF16)
        ox_ref[...] = f["ox"].astype(BF16)

    s = o.shape[0]
    return _row_call("post_fwd", body, s, min(ROW_TILE, s), [o, s_out, gate, h0], [memk, memv, *lns],
                     [w_oa, w_o, w_xq, w_xo], [(1024, F32), (1024, BF16), (1024, BF16), (1024, BF16)], [])


def _mlp(r2, target, g2, b2, g3, b3, w_up, w_down):
    cw = 1024
    n_chunk = MLP_HIDDEN // cw

    def body(r2_ref, tgt_ref, g2_ref, b2_ref, g3_ref, b3_ref, wup, wdn,
             dh2_ref, h2b_ref, dup_ref, act_ref, dff_ref, loss_ref, dg3_ref, db3_ref, up_scr):
        @pl.when(pl.program_id(0) == 0)
        def _():
            loss_ref[...] = jnp.zeros_like(loss_ref)
            dg3_ref[...] = jnp.zeros_like(dg3_ref)
            db3_ref[...] = jnp.zeros_like(db3_ref)

        h2, _, _ = _ln_fwd(r2_ref[...], g2_ref[...], b2_ref[...])
        h2b = h2.astype(BF16)
        h2b_ref[...] = h2b
        ff = jnp.zeros(h2.shape, F32)
        for c in range(n_chunk):
            sl = slice(c * cw, (c + 1) * cw)
            a = jnp.maximum(_mm(h2b, wup[:, sl]), 0.0)
            up_scr[:, sl] = a
            actb = (a * a).astype(BF16)
            act_ref[:, sl] = actb
            ff = ff + _mm(actb, wdn[sl, :])
        h3, xh3, rs3 = _ln_fwd(DN_ALPHA * h2 + ff, g3_ref[...], b3_ref[...])
        err = h3 - tgt_ref[...]
        loss_ref[...] += 0.5 * jnp.sum(err * err) * (1.0 / D_MODEL)
        dr3, dg3, db3 = _ln_bwd(err * (1.0 / D_MODEL), xh3, rs3, g3_ref[...])
        dg3_ref[...] += dg3
        db3_ref[...] += db3
        dffb = dr3.astype(BF16)
        dff_ref[...] = dffb
        dh2 = DN_ALPHA * dr3
        for c in range(n_chunk):
            sl = slice(c * cw, (c + 1) * cw)
            dupb = (_mm_nt(dffb, wdn[sl, :]) * (2.0 * up_scr[:, sl])).astype(BF16)
            dup_ref[:, sl] = dupb
            dh2 = dh2 + _mm_nt(dupb, wup[:, sl])
        dh2_ref[...] = dh2

    s = r2.shape[0]
    ts = min(ROW_TILE, s)
    return _row_call("mlp", body, s, ts, [r2, target], [g2, b2, g3, b3], [w_up, w_down],
                     [(1024, F32), (1024, BF16), (MLP_HIDDEN, BF16), (MLP_HIDDEN, BF16), (1024, BF16)],
                     [(8, 128), (1, 1024), (1, 1024)], scratch=[pltpu.VMEM((ts, MLP_HIDDEN), F32)])


def _post_bwd(dh2, o, s_out, gate, h0, memk, memv, lns, w_oa, w_o, w_xq, w_xo):
    def body(dh2_ref, o_ref, so_ref, gate_ref, h0_ref, mk_ref, mv_ref, g1, b1, g2, b2, woa, wo, wxq, wxo,
             do_ref, dl_ref, dso_ref, dgate_ref, dh0_ref, daout_ref, dmix_ref, dqx_ref, dxa_ref,
             dmk_ref, dmv_ref, dg1_ref, db1_ref, dg2_ref, db2_ref):
        @pl.when(pl.program_id(0) == 0)
        def _():
            for r in (dmk_ref, dmv_ref, dg1_ref, db1_ref, dg2_ref, db2_ref):
                r[...] = jnp.zeros_like(r)

        o = o_ref[...]
        s_out = so_ref[...]
        memk, memv = mk_ref[...], mv_ref[...]
        f = _post_forward(o, s_out, gate_ref[...], h0_ref[...], memk, memv,
                          woa[...], wo[...], wxq[...], wxo[...], g1[...], b1[...], g2[...], b2[...])
        _, xh2, rs2 = _ln_fwd(f["r2"], g2[...], b2[...])
        dr2, dg2, db2 = _ln_bwd(dh2_ref[...], xh2, rs2, g2[...])
        dg2_ref[...] += dg2
        db2_ref[...] += db2
        dxab = dr2.astype(BF16)
        dxa_ref[...] = dxab
        dox = _mm_nt(dxab, wxo[...])
        dqs = []
        for hh in range(XATTN_HEADS):
            sl = slice(hh * XATTN_HD, (hh + 1) * XATTN_HD)
            p = f["ps"][hh]
            doxh = dox[:, sl].astype(BF16)
            dp = _mm_nt(doxh, memv[:, sl])
            ds = (p * (dp - jnp.sum(dp * p, axis=1, keepdims=True)) * (XATTN_HD ** -0.5)).astype(BF16)
            dqs.append(_mm(ds, memk[:, sl]))
            dmk_ref[:, sl] += _mm_tn(ds, f["qxb"][:, sl])
            dmv_ref[:, sl] += _mm_tn(p, doxh)
        dqxb = jnp.concatenate(dqs, axis=1).astype(BF16)
        dqx_ref[...] = dqxb
        dh1 = DN_ALPHA * dr2 + _mm_nt(dqxb, wxq[...])
        dr1, dg1, db1 = _ln_bwd(dh1, f["xh1"], f["rs1"], g1[...])
        dg1_ref[...] += dg1
        db1_ref[...] += db1
        dh0_ref[...] = DN_ALPHA * dr1
        dmixb = dr1.astype(BF16)
        dmix_ref[...] = dmixb
        dmixin = _mm_nt(dmixb, wo[...])
        sg_s, sg_a = f["sg_s"], f["sg_a"]
        dso_ref[...] = dmixin * sg_s
        daoutb = (dmixin * sg_a).astype(BF16)
        daout_ref[...] = daoutb
        dgate_ref[:, :1024] = dmixin * s_out * sg_s * (1.0 - sg_s)
        dgate_ref[:, 1024:] = dmixin * f["a_out"] * sg_a * (1.0 - sg_a)
        d_o = _mm_nt(daoutb, woa[...])
        do_ref[...] = d_o.astype(BF16)
        lane = lax.broadcasted_iota(jnp.int32, (o.shape[0], HEAD_PAD), 1)
        dlc = jnp.zeros((o.shape[0], HEAD_PAD), F32)
        for hh in range(MLA_HEADS):
            sl = slice(hh * HEAD_PAD, (hh + 1) * HEAD_PAD)
            dl = jnp.sum(d_o[:, sl] * o[:, sl], axis=1, keepdims=True)
            dlc = dlc + jnp.where(lane == hh, dl, 0.0)
        dl_ref[...] = dlc

    s = o.shape[0]
    m = memk.shape[0]
    return _row_call(
        "post_bwd", body, s, min(ROW_TILE, s), [dh2, o, s_out, gate, h0], [memk, memv, *lns],
        [w_oa, w_o, w_xq, w_xo],
        [(1024, BF16), (HEAD_PAD, F32), (1024, F32), (2048, F32), (1024, F32),
         (1024, BF16), (1024, BF16), (1024, BF16), (1024, BF16)],
        [(m, 1024), (m, 1024), (1, 1024), (1, 1024), (1, 1024), (1, 1024)])


def _s5out_bwd(d_so, yl, z3, d_skip, w_c, w_glu):
    def body(dso_ref, yl_ref, z3_ref, d_ref, wc, wglu, gh_ref, dup_ref, yg_ref, dy12_ref, dyl_ref, dd_ref):
        @pl.when(pl.program_id(0) == 0)
        def _():
            dd_ref[...] = jnp.zeros_like(dd_ref)

        yl = yl_ref[...]
        ygb = _gelu(yl).astype(BF16)
        yg_ref[...] = ygb
        y12 = _mm(ygb, wglu[...])
        sg = _sigmoid(y12[:, 1024:])
        dso = dso_ref[...]
        dy12b = jnp.concatenate([dso * sg, dso * y12[:, :1024] * sg * (1.0 - sg)], axis=1).astype(BF16)
        dy12_ref[...] = dy12b
        dyl = _mm_nt(dy12b, wglu[...]) * _gelu_grad(yl)
        dylb = dyl.astype(BF16)
        dyl_ref[...] = dylb
        gh_ref[...] = _mm_nt(dylb, wc[...])
        dup_ref[...] = dyl * d_ref[...]
        dd_ref[...] += jnp.sum(dyl * z3_ref[:, :256], axis=0, keepdims=True)

    s = d_so.shape[0]
    return _row_call("s5out_bwd", body, s, min(ROW_TILE, s), [d_so, yl, z3], [d_skip], [w_c, w_glu],
                     [(2 * NS, F32), (256, F32), (256, BF16), (2048, BF16), (256, BF16)], [(1, 256)])


def _in_bwd(x, tab, z3, dq, dk, dv, lam, du_p, dgate, dh0p, ln_g, ln_b, qg, kvg, w_inx, w_b, w_q2, w_kv2):
    def body(x_ref, tab_ref, z3_ref, dq_ref, dk_ref, dv_ref, lam_ref, dup_ref, dgate_ref, dh0p_ref,
             lng_ref, lnb_ref, qg_ref, kvg_ref, winx, wb, wq2, wkv2,
             dx_ref, h0b_ref, dz_ref, ub_ref, cqn_ref, ckvn_ref, dq2_ref, dkv2_ref,
             dlng_ref, dlnb_ref, dqg_ref, dkvg_ref):
        @pl.when(pl.program_id(0) == 0)
        def _():
            for r in (dlng_ref, dlnb_ref, dqg_ref, dkvg_ref):
                r[...] = jnp.zeros_like(r)

        h0, xh0, rs0 = _ln_fwd(x_ref[...], lng_ref[...], lnb_ref[...])
        h0b_ref[...] = h0.astype(BF16)
        z3 = z3_ref[...]
        ub_ref[...] = z3[:, :256].astype(BF16)
        tab_v = tab_ref[...]
        cq1, cq2, ck1 = tab_v[:, :128], tab_v[:, 128:256], tab_v[:, 256:384]
        du = _mm_nt(lam_ref[...], wb[...]) + dup_ref[...]
        dq = dq_ref[...]
        dq2b = jnp.concatenate([dq * _tile_heads(cq1), dq * _tile_heads(cq2)], axis=1).astype(BF16)
        dq2_ref[...] = dq2b
        cqg, cqn, rq = _rms_fwd(z3[:, 256:512], qg_ref[...])
        cqn_ref[...] = cqg.astype(BF16)
        dcq, dqg = _rms_bwd(_mm_nt(dq2b, wq2[...]), cqn, rq, qg_ref[...])
        dqg_ref[...] += dqg
        dk = dk_ref[...]
        dkv2b = jnp.concatenate([dk, dv_ref[...]], axis=1).astype(BF16)
        dkv2_ref[...] = dkv2b
        ckvg, ckvn, rkv = _rms_fwd(z3[:, 512:768], kvg_ref[...])
        ckvn_ref[...] = ckvg.astype(BF16)
        dckv, dkvg = _rms_bwd(_mm_nt(dkv2b, wkv2[...]), ckvn, rkv, kvg_ref[...])
        dkvg_ref[...] += dkvg
        dkrp = dk[:, :HEAD_PAD]
        for hh in range(1, MLA_HEADS):
            dkrp = dkrp + dk[:, hh * HEAD_PAD:(hh + 1) * HEAD_PAD]
        dzb = jnp.concatenate([du, dcq, dckv, dkrp * ck1, dkrp * cq2, dgate_ref[...]], axis=1).astype(BF16)
        dz_ref[...] = dzb
        dh0 = _mm_nt(dzb, winx[...]) + dh0p_ref[...]
        dx, dg, db = _ln_bwd(dh0, xh0, rs0, lng_ref[...])
        dx_ref[...] = dx
        dlng_ref[...] += dg
        dlnb_ref[...] += db

    s = x.shape[0]
    return _row_call(
        "in_bwd", body, s, min(ROW_TILE, s), [x, tab, z3, dq, dk, dv, lam, du_p, dgate, dh0p],
        [ln_g, ln_b, qg, kvg], [w_inx, w_b, w_q2, w_kv2],
        [(1024, F32), (1024, BF16), (3072, BF16), (256, BF16), (256, BF16), (256, BF16), (2048, BF16), (2048, BF16)],
        [(1, 1024), (1, 1024), (1, 256), (1, 256)])


def _wgrad(name, xs, dy, out_dtype=F32, blocks=1):
    s, k = xs.shape
    n = dy.shape[1]
    nb = n // blocks
    tk, tn = min(k, 1024), min(n, 1024)
    ts = min(s, 2048 if tk <= 256 else 1024)
    per_tile = tn // nb
    assert s % ts == 0 and k % tk == 0 and n % tn == 0 and (blocks == 1 or tn % nb == 0), (name, s, k, n)
    last = s // ts - 1

    def body(x_ref, dy_ref, o_ref, acc):
        @pl.when(pl.program_id(2) == 0)
        def _():
            acc[...] = jnp.zeros_like(acc)

        acc[...] += _mm_tn(x_ref[...], dy_ref[...])

        @pl.when(pl.program_id(2) == last)
        def _():
            if blocks == 1:
                o_ref[...] = acc[...].astype(out_dtype)
            else:
                for d in range(per_tile):
                    o_ref[d] = acc[:, d * nb:(d + 1) * nb].astype(out_dtype)

    if blocks == 1:
        out_shape = jax.ShapeDtypeStruct((k, n), out_dtype)
        out_spec = pl.BlockSpec((tk, tn), lambda a, b, c: (a, b))
    else:
        out_shape = jax.ShapeDtypeStruct((blocks, k, nb), out_dtype)
        out_spec = pl.BlockSpec((per_tile, tk, nb), lambda a, b, c: (b, a, 0))
    return pl.pallas_call(
        body, name=name, grid=(k // tk, n // tn, s // ts),
        in_specs=[pl.BlockSpec((ts, tk), lambda a, b, c: (c, a)), pl.BlockSpec((ts, tn), lambda a, b, c: (c, b))],
        out_specs=out_spec, out_shape=out_shape, scratch_shapes=[pltpu.VMEM((tk, tn), F32)],
        compiler_params=pltpu.CompilerParams(dimension_semantics=("arbitrary", "arbitrary", "arbitrary"),
                                             vmem_limit_bytes=VMEM_LIMIT),
    )(xs, dy)


def _mesh_pos():
    x, y, c = lax.axis_index("x"), lax.axis_index("y"), lax.axis_index("c")
    return x, y, c


def _peer(x, y, c, k):
    px = 1 - x if k & 4 else x
    py = 1 - y if k & 2 else y
    pc = 1 - c if k & 1 else c
    return (px, py, pc), 4 * px + 2 * py + pc


def _all_to_all(name, arrays, gather):
    n = len(arrays)

    def body(*refs):
        src, dst = refs[:n], refs[n:2 * n]
        send_sems, recv_sems, local_sems = refs[2 * n:]
        x, y, c = _mesh_pos()
        me = 4 * x + 2 * y + c

        def block(i, d):
            return src[i] if gather else src[i].at[d]

        own = [pltpu.make_async_copy(block(i, me), dst[i].at[me], local_sems.at[i]) for i in range(n)]
        for cp in own:
            cp.start()
        sends = []
        for k in range(1, N_DEV):
            peer, pid = _peer(x, y, c, k)
            for i in range(n):
                idx = (k - 1) * n + i
                cp = pltpu.make_async_remote_copy(src_ref=block(i, pid), dst_ref=dst[i].at[me],
                                                  send_sem=send_sems.at[idx], recv_sem=recv_sems.at[idx],
                                                  device_id=peer, device_id_type=MESH)
                cp.start()
                sends.append(cp)
        for k in range(1, N_DEV):
            peer, pid = _peer(x, y, c, k)
            for i in range(n):
                idx = (k - 1) * n + i
                pltpu.make_async_remote_copy(src_ref=block(i, pid), dst_ref=dst[i].at[pid],
                                             send_sem=send_sems.at[idx], recv_sem=recv_sems.at[idx],
                                             device_id=peer, device_id_type=MESH).wait_recv()
        for cp in sends:
            cp.wait_send()
        for cp in own:
            cp.wait()

    n_sem = n * (N_DEV - 1)
    return pl.pallas_call(
        body, name=name,
        in_specs=[pl.BlockSpec(memory_space=pl.ANY)] * n,
        out_specs=[pl.BlockSpec(memory_space=pl.ANY)] * n,
        out_shape=[jax.ShapeDtypeStruct((N_DEV,) + a.shape[-2:], a.dtype) for a in arrays],
        scratch_shapes=[pltpu.SemaphoreType.DMA((n_sem,)), pltpu.SemaphoreType.DMA((n_sem,)),
                        pltpu.SemaphoreType.DMA((n,))],
    )(*arrays)


_HBM = pl.BlockSpec(memory_space=pltpu.HBM)
_SEM = pl.BlockSpec(memory_space=pltpu.SEMAPHORE)


def _exchange_start(name, arrays, gather, after=None):
    n = len(arrays)
    n_sem = n * (N_DEV - 1)
    me = 4 * lax.axis_index("x") + 2 * lax.axis_index("y") + lax.axis_index("c")
    lands = []
    for a in arrays:
        own = a[None] if gather else lax.dynamic_slice_in_dim(a, me, 1, 0)
        lands.append(lax.dynamic_update_slice(lax.empty((N_DEV,) + a.shape[-2:], a.dtype), own, (me, 0, 0)))
    n_after = 0 if after is None else 1

    def body(*refs):
        src, land = refs[:n], refs[n:2 * n]
        send_sems, recv_sems = refs[2 * n + n_after], refs[2 * n + n_after + 1]
        token = refs[-1]
        x, y, c = _mesh_pos()
        me_in = 4 * x + 2 * y + c
        for k in range(1, N_DEV):
            peer, pid = _peer(x, y, c, k)
            for i in range(n):
                idx = (k - 1) * n + i
                pltpu.make_async_remote_copy(src_ref=src[i] if gather else src[i].at[pid], dst_ref=land[i].at[me_in],
                                             send_sem=send_sems.at[idx], recv_sem=recv_sems.at[idx],
                                             device_id=peer, device_id_type=MESH).start()
        token[...] = jnp.zeros_like(token)

    operands = [pltpu.with_memory_space_constraint(a, pltpu.HBM) for a in list(arrays) + lands]
    outs = pl.pallas_call(
        body, name=name,
        out_shape=(pltpu.SemaphoreType.DMA((n_sem,)), pltpu.SemaphoreType.DMA((n_sem,)),
                   *[pltpu.HBM(a.shape, a.dtype) for a in list(arrays) + lands],
                   jax.ShapeDtypeStruct((8, 128), F32)),
        in_specs=[_HBM] * (2 * n) + [pl.BlockSpec(memory_space=pl.ANY)] * n_after,
        out_specs=(_SEM, _SEM, *[_HBM] * (2 * n), pl.BlockSpec(memory_space=pltpu.VMEM)),
        input_output_aliases={i: 2 + i for i in range(2 * n)},
        compiler_params=pltpu.CompilerParams(has_side_effects=pltpu.SideEffectType.DATAFLOW_SIDE_EFFECTING),
    )(*operands, *([after] if n_after else []))
    return (gather, outs[0], outs[1], outs[2:2 + n], outs[2 + n:2 + 2 * n]), outs[-1]


def _exchange_wait(name, handle, after):
    gather, send_sems, recv_sems, srcs, lands = handle
    n = len(srcs)

    def body(*refs):
        src, land = refs[:n], refs[n:2 * n]
        s_sems, r_sems = refs[2 * n], refs[2 * n + 1]
        x, y, c = _mesh_pos()
        for k in range(1, N_DEV):
            peer, pid = _peer(x, y, c, k)
            for i in range(n):
                idx = (k - 1) * n + i
                cp = pltpu.make_async_remote_copy(src_ref=src[i] if gather else src[i].at[pid], dst_ref=land[i].at[pid],
                                                  send_sem=s_sems.at[idx], recv_sem=r_sems.at[idx],
                                                  device_id=peer, device_id_type=MESH)
                cp.wait_send()
                cp.wait_recv()

    outs = pl.pallas_call(
        body, name=name,
        out_shape=tuple(pltpu.HBM(a.shape, a.dtype) for a in list(srcs) + list(lands)),
        in_specs=[_HBM] * (2 * n) + [_SEM, _SEM, pl.BlockSpec(memory_space=pl.ANY)],
        out_specs=tuple([_HBM] * (2 * n)),
        input_output_aliases={i: i for i in range(2 * n)},
        compiler_params=pltpu.CompilerParams(has_side_effects=pltpu.SideEffectType.DATAFLOW_SIDE_EFFECTING),
    )(*srcs, *lands, send_sems, recv_sems, after)
    return list(outs[n:])


def _adamw(name, parts, w, m, v):
    a_rows, b_cols = w.shape
    ta = min(a_rows, ADAM_TILE)
    assert a_rows % ta == 0
    c1 = 1.0 - ADAM_B1 ** ADAM_STEP
    c2 = 1.0 - ADAM_B2 ** ADAM_STEP

    def body(p_ref, w_ref, m_ref, v_ref, g_ref, d_ref, mo_ref, vo_ref):
        g = p_ref[0].astype(F32)
        for d in range(1, N_DEV):
            g = g + p_ref[d].astype(F32)
        g_ref[...] = g
        mn = ADAM_B1 * m_ref[...] + (1.0 - ADAM_B1) * g
        vn = ADAM_B2 * v_ref[...] + (1.0 - ADAM_B2) * (g * g)
        mo_ref[...] = mn
        vo_ref[...] = vn
        d_ref[...] = -ADAM_LR * ((mn / c1) / (jnp.sqrt(vn / c2) + ADAM_EPS) + ADAM_WD * w_ref[...])

    row = pl.BlockSpec((ta, b_cols), lambda i: (i, 0))
    return pl.pallas_call(
        body, name=name, grid=(a_rows // ta,),
        in_specs=[pl.BlockSpec((N_DEV, ta, b_cols), lambda i: (0, i, 0)), row, row, row],
        out_specs=[row] * 4,
        out_shape=[jax.ShapeDtypeStruct((a_rows, b_cols), F32)] * 4,
        compiler_params=pltpu.CompilerParams(dimension_semantics=("arbitrary",), vmem_limit_bytes=VMEM_LIMIT),
    )(parts, w, m, v)


def _pack_rows(arrays, rows):
    flat = jnp.concatenate([a.reshape(-1) for a in arrays])
    return jnp.pad(flat, (0, rows * LANES - flat.shape[0])).reshape(rows, LANES)


def _cols_from_blocks(g):
    return g.transpose(1, 0, 2).reshape(g.shape[1], N_DEV * g.shape[2])


def _blocks_from_cols(w):
    return w.reshape(w.shape[0], N_DEV, w.shape[1] // N_DEV).transpose(1, 0, 2)


def _unpack_flat(flat2d, shapes):
    flat = flat2d.reshape(-1)
    out, off = [], 0
    for shp in shapes:
        sz = math.prod(shp)
        out.append(flat[off:off + sz].reshape(shp))
        off += sz
    return out


def _s5_mats(lam_re, lam_im, log_dt, b_re, b_im, c_re, c_im):
    lr = jnp.minimum(lam_re, S5_MAX_RE)
    li = lam_im
    dt = jnp.exp(log_dt)[:, None]
    mag = jnp.exp(lr * dt)
    ang = li * dt
    ab_re = mag * jnp.cos(ang)
    ab_im = mag * jnp.sin(ang)
    den = lr * lr + li * li
    nr = ab_re - 1.0
    f_re = ((nr * lr + ab_im * li) / den)[..., None]
    f_im = ((ab_im * lr - nr * li) / den)[..., None]
    bb_re = f_re * b_re - f_im * b_im
    bb_im = f_re * b_im + f_im * b_re
    eye = jnp.eye(S5_GROUPS, dtype=F32)
    a = jnp.stack([ab_re.reshape(-1), ab_im.reshape(-1)])
    wb = jnp.concatenate([jnp.einsum("gph,gk->ghkp", bb_re, eye).reshape(S5_WIDTH, NS),
                          jnp.einsum("gph,gk->ghkp", bb_im, eye).reshape(S5_WIDTH, NS)], axis=1)
    wc = jnp.concatenate([jnp.einsum("ghp,gk->gpkh", c_re, eye).reshape(NS, S5_WIDTH),
                          -jnp.einsum("ghp,gk->gpkh", c_im, eye).reshape(NS, S5_WIDTH)], axis=0)
    return a, wb, wc


def _power_table(a):
    ar, ai = a[0], a[1]
    rows_r, rows_i = [ar], [ai]
    for _ in range(7):
        pr, pi = rows_r[-1], rows_i[-1]
        rows_r.append(pr * ar - pi * ai)
        rows_i.append(pr * ai + pi * ar)
    return jnp.concatenate([jnp.stack(rows_r), jnp.stack(rows_i)], axis=1)


def _rope_table(positions):
    inv = ROPE_THETA ** (-jnp.arange(0, MLA_ROPE, 2, dtype=F32) / MLA_ROPE)
    ang = positions.astype(F32)[:, None] * inv
    cos, sin = jnp.cos(ang), jnp.sin(ang)
    s = positions.shape[0]
    one, zero, pad = jnp.ones((s, MLA_NOPE), F32), jnp.zeros((s, MLA_NOPE), F32), jnp.zeros((s, 32), F32)
    return jnp.concatenate([one, cos, cos, pad, zero, -sin, sin, pad, zero, cos, cos, pad], axis=1)


def _derived_weights(full):
    w_in = full["w_in"]
    k1, k2 = w_in[:, 768:784], w_in[:, 784:800]
    z64, z32 = jnp.zeros((1024, 64), BF16), jnp.zeros((1024, 32), BF16)
    w_inx = jnp.concatenate([w_in[:, :768], z64, k1, k2, z32, z64, k2, k1, z32, w_in[:, 800:]], axis=1)
    uq = full["w_uq"].reshape(256, MLA_HEADS, MLA_QK)
    nope, r1, r2 = uq[:, :, :64], uq[:, :, 64:80], uq[:, :, 80:]
    zq64, zq32 = jnp.zeros((256, MLA_HEADS, 64), BF16), jnp.zeros((256, MLA_HEADS, 32), BF16)
    w_q2 = jnp.concatenate([jnp.concatenate([nope, r1, r2, zq32], axis=2).reshape(256, 1024),
                            jnp.concatenate([zq64, r2, r1, zq32], axis=2).reshape(256, 1024)], axis=1)
    ukv = full["w_ukv"].reshape(256, MLA_HEADS, 128)
    w_kv2 = jnp.concatenate([jnp.concatenate([ukv[:, :, :64], zq64], axis=2).reshape(256, 1024),
                             jnp.concatenate([ukv[:, :, 64:], zq64], axis=2).reshape(256, 1024)], axis=1)
    return w_inx, w_q2, w_kv2


def _oa_padded(w_oa):
    oa = w_oa.reshape(MLA_HEADS, MLA_V, 1024)
    return jnp.concatenate([oa, jnp.zeros_like(oa)], axis=1).reshape(1024, 1024)


def _fold_grads(d_inx, d_q2, d_kv2):
    d_k1 = d_inx[:, 832:848] + d_inx[:, 976:992]
    d_k2 = d_inx[:, 848:864] + d_inx[:, 960:976]
    d_w_in = jnp.concatenate([d_inx[:, :768], d_k1, d_k2, d_inx[:, 1024:]], axis=1)
    a = d_q2[:, :1024].reshape(256, MLA_HEADS, 128)
    b = d_q2[:, 1024:].reshape(256, MLA_HEADS, 128)
    d_w_uq = jnp.concatenate([a[:, :, :64], a[:, :, 64:80] + b[:, :, 80:96], a[:, :, 80:96] + b[:, :, 64:80]],
                             axis=2).reshape(256, MLA_HEADS * MLA_QK)
    kk = d_kv2[:, :1024].reshape(256, MLA_HEADS, 128)
    vv = d_kv2[:, 1024:].reshape(256, MLA_HEADS, 128)
    d_w_ukv = jnp.concatenate([kk[:, :, :64], vv[:, :, :64]], axis=2).reshape(256, 1024)
    return d_w_in, d_w_uq, d_w_ukv


def kernel(x, mem, positions, ln_in_g, ln_in_b, w_in, s5_lam_re, s5_lam_im, s5_log_dt, s5_b_re, s5_b_im, s5_c_re, s5_c_im, s5_d, w_glu, q_norm_g, w_uq, kv_norm_g, w_ukv, w_oa, w_o, ln1_g, ln1_b, w_xq, w_xk, w_xv, w_xo, ln2_g, ln2_b, w_up, w_down, ln3_g, ln3_b, loss_target, m_ln_in_g, m_ln_in_b, m_w_in, m_s5_lam_re, m_s5_lam_im, m_s5_log_dt, m_s5_b_re, m_s5_b_im, m_s5_c_re, m_s5_c_im, m_s5_d, m_w_glu, m_q_norm_g, m_w_uq, m_kv_norm_g, m_w_ukv, m_w_oa, m_w_o, m_ln1_g, m_ln1_b, m_w_xq, m_w_xk, m_w_xv, m_w_xo, m_ln2_g, m_ln2_b, m_w_up, m_w_down, m_ln3_g, m_ln3_b, v_ln_in_g, v_ln_in_b, v_w_in, v_s5_lam_re, v_s5_lam_im, v_s5_log_dt, v_s5_b_re, v_s5_b_im, v_s5_c_re, v_s5_c_im, v_s5_d, v_w_glu, v_q_norm_g, v_w_uq, v_kv_norm_g, v_w_ukv, v_w_oa, v_w_o, v_ln1_g, v_ln1_b, v_w_xq, v_w_xk, v_w_xv, v_w_xo, v_ln2_g, v_ln2_b, v_w_up, v_w_down, v_ln3_g, v_ln3_b):
    args = dict(locals())
    wts = {n: args[n] for n in WEIGHTS}
    mom = {n: args["m_" + n] for n in WEIGHTS}
    vel = {n: args["v_" + n] for n in WEIGHTS}
    xs, mems, tgt = x[0], mem[0], loss_target[0]
    s = xs.shape[0]
    names_big = [n for (n, _, _, _) in SHARDED]
    kind = {n: kd for (n, kd, _, _) in SHARDED}
    small_rows = _round_up(sum(math.prod(wts[n].shape) for n in SMALL), 8 * LANES) // LANES

    late = [n for n in names_big if n not in EARLY]
    shard = {n: wts[n][0].astype(BF16) for n in names_big}
    g_early = dict(zip(EARLY, _all_to_all("all_gather", [shard[n] for n in EARLY], True)))
    ag_handle, ag_token = _exchange_start("all_gather_late", [shard[n] for n in late], True, after=g_early["w_in"])
    full = {n: _cols_from_blocks(g_early[n]) for n in EARLY}
    w_inx, w_q2, w_kv2 = _derived_weights(full)
    s5_args = (s5_lam_re[0], s5_lam_im[0], s5_log_dt[0], s5_b_re[0], s5_b_im[0], s5_c_re[0], s5_c_im[0])
    (a_mat, w_b, w_c), s5_vjp = jax.vjp(_s5_mats, *s5_args)
    w_bb, w_cb = w_b.astype(BF16), w_c.astype(BF16)
    pw = _power_table(a_mat)
    pwb = _power_table(a_mat * jnp.array([[1.0], [-1.0]], F32))[::-1]
    tab = _rope_table(positions[0])
    row = lambda a: a.reshape(1, -1)
    ln_g, ln_b = row(ln_in_g) + ag_token[0:1, 0:1], row(ln_in_b)
    lns = [ln1_g, ln1_b, ln2_g, ln2_b]

    h0, z3, gate, bu, q, k, v, kt, vt = _in_fwd(xs, tab, ln_g, ln_b, q_norm_g, kv_norm_g, w_inx, w_bb, w_q2, w_kv2)
    h = _scan_fwd(pw, bu)
    yl, s_out = _s5out_fwd(h, z3, s5_d, w_cb, full["w_glu"])
    o, lse_t = _attn_fwd(q, k, vt)
    g_late = dict(zip(late, _exchange_wait("all_gather_late_wait", ag_handle, after=lse_t)))
    full.update({n: g_late[n].reshape(-1, g_late[n].shape[2]) if kind[n] == "row" else _cols_from_blocks(g_late[n])
                 for n in late})
    w_oap = _oa_padded(full["w_oa"])
    memk, memv = _mem_kv(mems, full["w_xk"], full["w_xv"])
    r2, mixin_b, h1_b, ox_b = _post_fwd(o, s_out, gate, h0, memk, memv, lns,
                                        w_oap, full["w_o"], full["w_xq"], full["w_xo"])
    dh2, h2_b, dup_b, act_b, dff_b, loss_acc, d_ln3_g, d_ln3_b = _mlp(
        r2, tgt, ln2_g, ln2_b, ln3_g, ln3_b, full["w_up"], full["w_down"])
    (do_b, delta, d_so, dgate, dh0p, daout_b, dmix_b, dqx_b, dxa_b,
     dmk, dmv, d_ln1_g, d_ln1_b, d_ln2_g, d_ln2_b) = _post_bwd(
        dh2, o, s_out, gate, h0, memk, memv, lns, w_oap, full["w_o"], full["w_xq"], full["w_xo"])
    rows8 = lambda g: g.reshape(N_DEV, g.shape[0] // N_DEV, g.shape[1])
    d_w_oa = _wgrad("wg_oa", o, daout_b).reshape(MLA_HEADS, 128, 1024)[:, :64].reshape(512, 1024)
    send = {
        "w_oa": _blocks_from_cols(d_w_oa).astype(BF16),
        "w_up": _wgrad("wg_up", h2_b, dup_b, BF16, N_DEV),
        "w_o": rows8(_wgrad("wg_o", mixin_b, dmix_b, BF16)),
        "w_xq": rows8(_wgrad("wg_xq", h1_b, dqx_b, BF16)),
        "w_xk": rows8(_wgrad("wg_xk", mems, dmk, BF16)),
        "w_xv": rows8(_wgrad("wg_xv", mems, dmv, BF16)),
        "w_xo": rows8(_wgrad("wg_xo", ox_b, dxa_b, BF16)),
        "w_down": rows8(_wgrad("wg_down", act_b, dff_b, BF16)),
    }
    rs_handle, rs_token = _exchange_start("grad_exchange_late", [send[n] for n in late], False)
    dl_t = delta.T[:MLA_HEADS].reshape(MLA_HEADS, 1, s) + rs_token[0, 0]
    dq, dk, dv = _attn_bwd(q, k, v, kt, do_b, lse_t, dl_t)
    gh, du_p, yg_b, dy12_b, dyl_b, d_s5_d = _s5out_bwd(d_so, yl, z3, s5_d, w_cb, full["w_glu"])
    lam, d_a = _scan_bwd(pwb, gh, h)
    (dx, h0_b, dz_b, u_b, cqn_b, ckvn_b, dq2_b, dkv2_b, d_ln_g, d_ln_b, d_qg, d_kvg) = _in_bwd(
        xs, tab, z3, dq, dk, dv, lam, du_p, dgate, dh0p, ln_g, ln_b, q_norm_g, kv_norm_g, w_inx, w_bb, w_q2, w_kv2)

    d_w_in, d_w_uq, d_w_ukv = _fold_grads(
        _wgrad("wg_in", h0_b, dz_b), _wgrad("wg_q", cqn_b, dq2_b), _wgrad("wg_kv", ckvn_b, dkv2_b))
    send.update({
        "w_in": _blocks_from_cols(d_w_in).astype(BF16), "w_uq": _blocks_from_cols(d_w_uq).astype(BF16),
        "w_ukv": _blocks_from_cols(d_w_ukv).astype(BF16),
        "w_glu": _wgrad("wg_glu", yg_b, dy12_b, BF16, N_DEV),
    })
    d_s5 = s5_vjp((d_a.reshape(2, NS), _wgrad("wg_s5b", u_b, lam), _wgrad("wg_s5c", h, dyl_b)))
    small_grads = {
        "ln_in_g": d_ln_g, "ln_in_b": d_ln_b, "s5_lam_re": d_s5[0], "s5_lam_im": d_s5[1], "s5_log_dt": d_s5[2],
        "s5_b_re": d_s5[3], "s5_b_im": d_s5[4], "s5_c_re": d_s5[5], "s5_c_im": d_s5[6], "s5_d": d_s5_d,
        "q_norm_g": d_qg, "kv_norm_g": d_kvg, "ln1_g": d_ln1_g, "ln1_b": d_ln1_b, "ln2_g": d_ln2_g,
        "ln2_b": d_ln2_b, "ln3_g": d_ln3_g, "ln3_b": d_ln3_b,
    }
    small_send = jnp.broadcast_to(_pack_rows([small_grads[n] for n in SMALL], small_rows)[None],
                                  (N_DEV, small_rows, LANES))

    recv_early = _all_to_all("grad_exchange", [send[n] for n in EARLY] + [small_send], False)
    recv_late = _exchange_wait("grad_exchange_late_wait", rs_handle, after=recv_early[-1])
    recv = dict(zip(list(EARLY) + late, list(recv_early[:-1]) + recv_late))
    results = [dict(), dict(), dict(), dict()]
    for n in names_big:
        parts = recv[n]
        outs = _adamw("adamw_" + n, parts, wts[n][0], mom[n][0], vel[n][0])
        for res, a in zip(results, outs):
            res[n] = a[None]
    small_out = _adamw("adamw_small", recv_early[-1],
                       *[_pack_rows([t[n] for n in SMALL], small_rows) for t in (wts, mom, vel)])
    for res, fs in zip(results, small_out):
        for n, a in zip(SMALL, _unpack_flat(fs, [wts[n].shape for n in SMALL])):
            res[n] = a

    loss = lax.psum(loss_acc[0, 0], AXES)
    return (loss, dx[None], *[res[n] for res in results for n in WEIGHTS])
```

```python
import math

import jax
import jax.numpy as jnp
from jax import lax
from jax.experimental import pallas as pl
from jax.experimental.pallas import tpu as pltpu

F32 = jnp.float32
BF16 = jnp.bfloat16

D_MODEL = 1024
S5_WIDTH = 256
S5_GROUP_CH = 16
S5_GROUPS = 16
S5_STATE = 64
NS = S5_GROUPS * S5_STATE
S5_MAX_RE = -1e-4
MLA_HEADS = 8
MLA_NOPE = 64
MLA_ROPE = 32
MLA_QK = 96
MLA_V = 64
HEAD_PAD = 128
ROPE_THETA = 10000.0
XATTN_HEADS = 4
XATTN_HD = 256
MLP_HIDDEN = 4096
LN_EPS = 1e-5
RMS_EPS = 1e-6
NEG_INF = -1e30
LOG2E = 1.4426950408889634
DN_ALPHA = 2.0 ** 0.25
ADAM_LR = 0.001
ADAM_B1 = 0.9
ADAM_B2 = 0.999
ADAM_EPS = 1e-08
ADAM_WD = 0.01
ADAM_STEP = 10

N_DEV = 8
AXES = ("x", "y", "c")
MESH = pl.DeviceIdType.MESH
LANES = 1024
VMEM_LIMIT = 60 * 1024 * 1024

ROW_TILE = 256
SCAN_TILE = 1024
ATT_TILE = 512
ADAM_TILE = 128

SHARDED = (
    ("w_in", "col", 1024, 2848), ("w_glu", "col", 256, 2048), ("w_uq", "col", 256, 768),
    ("w_ukv", "col", 256, 1024), ("w_oa", "col", 512, 1024), ("w_o", "row", 1024, 1024),
    ("w_xq", "row", 1024, 1024), ("w_xk", "row", 1024, 1024), ("w_xv", "row", 1024, 1024),
    ("w_xo", "row", 1024, 1024), ("w_up", "col", 1024, 4096), ("w_down", "row", 4096, 1024),
)
EARLY = ("w_in", "w_glu", "w_uq", "w_ukv")
SMALL = ("ln_in_g", "ln_in_b", "s5_lam_re", "s5_lam_im", "s5_log_dt", "s5_b_re", "s5_b_im", "s5_c_re",
         "s5_c_im", "s5_d", "q_norm_g", "kv_norm_g", "ln1_g", "ln1_b", "ln2_g", "ln2_b", "ln3_g", "ln3_b")
WEIGHTS = ("ln_in_g", "ln_in_b", "w_in", "s5_lam_re", "s5_lam_im", "s5_log_dt", "s5_b_re", "s5_b_im",
           "s5_c_re", "s5_c_im", "s5_d", "w_glu", "q_norm_g", "w_uq", "kv_norm_g", "w_ukv", "w_oa", "w_o",
           "ln1_g", "ln1_b", "w_xq", "w_xk", "w_xv", "w_xo", "ln2_g", "ln2_b", "w_up", "w_down", "ln3_g", "ln3_b")


def _round_up(n, m):
    return (n + m - 1) // m * m


def _bf(a):
    return a.astype(BF16)


def _mm(a, b):
    return jnp.dot(_bf(a), _bf(b), preferred_element_type=F32)


def _mm_nt(a, b):
    return lax.dot_general(_bf(a), _bf(b), (((1,), (1,)), ((), ())), preferred_element_type=F32)


def _mm_tn(a, b):
    return lax.dot_general(_bf(a), _bf(b), (((0,), (0,)), ((), ())), preferred_element_type=F32)


def _sigmoid(a):
    return 1.0 / (1.0 + jnp.exp(-a))


def _gelu(a):
    return 0.5 * a * (1.0 + lax.erf(a * (2.0 ** -0.5)))


def _gelu_grad(a):
    return 0.5 * (1.0 + lax.erf(a * (2.0 ** -0.5))) + a * jnp.exp(-0.5 * a * a) * (1.0 / math.sqrt(2.0 * math.pi))


def _ln_fwd(a, g, b):
    mu = jnp.mean(a, axis=-1, keepdims=True)
    ac = a - mu
    var = jnp.mean(ac * ac, axis=-1, keepdims=True)
    rstd = lax.rsqrt(var + LN_EPS)
    xhat = ac * rstd
    return xhat * g + b, xhat, rstd


def _ln_bwd(dy, xhat, rstd, g):
    dxh = dy * g
    m1 = jnp.mean(dxh, axis=-1, keepdims=True)
    m2 = jnp.mean(dxh * xhat, axis=-1, keepdims=True)
    dx = rstd * (dxh - m1 - xhat * m2)
    return dx, jnp.sum(dy * xhat, axis=0, keepdims=True), jnp.sum(dy, axis=0, keepdims=True)


def _rms_fwd(a, g):
    r = lax.rsqrt(jnp.mean(a * a, axis=-1, keepdims=True) + RMS_EPS)
    xn = a * r
    return xn * g, xn, r


def _rms_bwd(dy, xn, r, g):
    dxn = dy * g
    dx = r * (dxn - xn * jnp.mean(dxn * xn, axis=-1, keepdims=True))
    return dx, jnp.sum(dy * xn, axis=0, keepdims=True)


def _tile_heads(a):
    return jnp.concatenate([a] * MLA_HEADS, axis=1)


def _row_call(name, body, n_rows, ts, tiled_in, full_in, weights, tiled_out, acc_out, reverse=False, scratch=(),
              tiled_out_t=()):
    n = n_rows // ts
    assert n * ts == n_rows, (name, n_rows, ts)
    nt, nf, nw = len(tiled_in), len(full_in), len(weights)
    nto, nao = len(tiled_out) + len(tiled_out_t), len(acc_out)
    if reverse:
        imap = lambda i: (n - 1 - i, 0)
    else:
        imap = lambda i: (i, 0)
    const = lambda i: (0, 0)

    def kern(*refs):
        ins = refs[:nt + nf]
        w_hbm = refs[nt + nf:nt + nf + nw]
        outs = refs[nt + nf + nw:nt + nf + nw + nto + nao]
        scr = refs[nt + nf + nw + nto + nao:]
        w_vmem = scr[:nw]
        extra = scr[nw + 1:] if nw else scr
        if nw:
            sem = scr[nw]

            @pl.when(pl.program_id(0) == 0)
            def _():
                cps = [pltpu.make_async_copy(w_hbm[k], w_vmem[k], sem.at[k]) for k in range(nw)]
                for cp in cps:
                    cp.start()
                for cp in cps:
                    cp.wait()
        body(*ins, *w_vmem, *outs, *extra)

    in_specs = [pl.BlockSpec((ts, a.shape[1]), imap) for a in tiled_in]
    in_specs += [pl.BlockSpec(a.shape, const) for a in full_in]
    in_specs += [pl.BlockSpec(memory_space=pl.ANY) for _ in weights]
    assert not (reverse and tiled_out_t)
    out_shape = [jax.ShapeDtypeStruct((n_rows, c), dt) for (c, dt) in tiled_out]
    out_shape += [jax.ShapeDtypeStruct((c, n_rows), dt) for (c, dt) in tiled_out_t]
    out_shape += [jax.ShapeDtypeStruct(shp, F32) for shp in acc_out]
    out_specs = [pl.BlockSpec((ts, c), imap) for (c, dt) in tiled_out]
    out_specs += [pl.BlockSpec((c, ts), lambda i: (0, i)) for (c, dt) in tiled_out_t]
    out_specs += [pl.BlockSpec(shp, const) for shp in acc_out]
    scratch_shapes = [pltpu.VMEM(w.shape, w.dtype) for w in weights]
    if nw:
        scratch_shapes.append(pltpu.SemaphoreType.DMA((nw,)))
    scratch_shapes += list(scratch)
    return pl.pallas_call(
        kern, name=name, grid=(n,), in_specs=in_specs, out_specs=out_specs, out_shape=out_shape,
        scratch_shapes=scratch_shapes,
        compiler_params=pltpu.CompilerParams(dimension_semantics=("arbitrary",), vmem_limit_bytes=VMEM_LIMIT),
    )(*tiled_in, *full_in, *weights)


def _mem_kv(mem, w_xk, w_xv):
    m = mem.shape[0]

    def body(mem_ref, wk_ref, wv_ref, k_ref, v_ref):
        mb = mem_ref[...]
        k_ref[...] = _mm(mb, wk_ref[...]).astype(BF16)
        v_ref[...] = _mm(mb, wv_ref[...]).astype(BF16)

    return pl.pallas_call(
        body, name="mem_kv",
        out_shape=[jax.ShapeDtypeStruct((m, D_MODEL), BF16)] * 2,
        compiler_params=pltpu.CompilerParams(vmem_limit_bytes=VMEM_LIMIT),
    )(mem, w_xk, w_xv)


def _in_fwd(x, tab, ln_g, ln_b, qg, kvg, w_inx, w_b, w_q2, w_kv2):
    def body(x_ref, tab_ref, lng_ref, lnb_ref, qg_ref, kvg_ref, winx, wb, wq2, wkv2,
             h0_ref, z3_ref, gate_ref, bu_ref, q_ref, k_ref, v_ref, kt_ref, vt_ref):
        h0, _, _ = _ln_fwd(x_ref[...], lng_ref[...], lnb_ref[...])
        h0_ref[...] = h0
        z = _mm(h0, winx[...])
        z3_ref[...] = z[:, :768]
        gate_ref[...] = z[:, 1024:]
        bu_ref[...] = _mm(z[:, :256], wb[...])
        tab_v = tab_ref[...]
        cq1, cq2, ck1 = tab_v[:, :128], tab_v[:, 128:256], tab_v[:, 256:384]
        cqn, _, _ = _rms_fwd(z[:, 256:512], qg_ref[...])
        q2 = _mm(cqn, wq2[...])
        q_ref[...] = (q2[:, :1024] * _tile_heads(cq1) + q2[:, 1024:] * _tile_heads(cq2)).astype(BF16)
        ckvn, _, _ = _rms_fwd(z[:, 512:768], kvg_ref[...])
        kv2 = _mm(ckvn, wkv2[...])
        krp = z[:, 768:896] * ck1 + z[:, 896:1024] * cq2
        kf = kv2[:, :1024] + _tile_heads(krp)
        k_ref[...] = kf.astype(BF16)
        v_ref[...] = kv2[:, 1024:].astype(BF16)
        kt_ref[...] = kf.T.astype(BF16)
        vt_ref[...] = kv2[:, 1024:].T.astype(BF16)

    s = x.shape[0]
    return _row_call(
        "in_fwd", body, s, min(ROW_TILE, s), [x, tab], [ln_g, ln_b, qg, kvg], [w_inx, w_b, w_q2, w_kv2],
        [(1024, F32), (768, F32), (2048, F32), (2048, F32), (1024, BF16), (1024, BF16), (1024, BF16)], [],
        tiled_out_t=[(1024, BF16), (1024, BF16)])


def _scan_level(row, xr, xi, ar, ai, k, forward):
    if forward:
        keep = row >= k
        sr, si = pltpu.roll(xr, k, 0), pltpu.roll(xi, k, 0)
    else:
        keep = row < 8 - k
        sr, si = pltpu.roll(xr, 8 - k, 0), pltpu.roll(xi, 8 - k, 0)
    sr = jnp.where(keep, sr, 0.0)
    si = jnp.where(keep, si, 0.0)
    return xr + ar * sr - ai * si, xi + ar * si + ai * sr


def _scan_fwd(pw, bu):
    s = bu.shape[0]
    ts = min(SCAN_TILE, s)
    nblk = ts // 8

    def body(bu_ref, pw_ref, h_ref, carry):
        @pl.when(pl.program_id(0) == 0)
        def _():
            carry[...] = jnp.zeros_like(carry)

        row = lax.broadcasted_iota(jnp.int32, (8, NS), 0)
        a1 = (pw_ref[0:1, :NS], pw_ref[0:1, NS:])
        a2 = (pw_ref[1:2, :NS], pw_ref[1:2, NS:])
        a4 = (pw_ref[3:4, :NS], pw_ref[3:4, NS:])

        def blk(b, _):
            r0 = pl.multiple_of(b * 8, 8)
            xr = bu_ref[pl.ds(r0, 8), :NS]
            xi = bu_ref[pl.ds(r0, 8), NS:]
            xr, xi = _scan_level(row, xr, xi, a1[0], a1[1], 1, True)
            xr, xi = _scan_level(row, xr, xi, a2[0], a2[1], 2, True)
            xr, xi = _scan_level(row, xr, xi, a4[0], a4[1], 4, True)
            cr, ci = carry[7:8, :NS], carry[7:8, NS:]
            pr, pi = pw_ref[:, :NS], pw_ref[:, NS:]
            hr = xr + pr * cr - pi * ci
            hi = xi + pr * ci + pi * cr
            h_ref[pl.ds(r0, 8), :NS] = hr
            h_ref[pl.ds(r0, 8), NS:] = hi
            carry[:, :NS] = hr
            carry[:, NS:] = hi
            return 0

        lax.fori_loop(0, nblk, blk, 0)

    return _row_call("scan_fwd", body, s, ts, [bu], [pw], [], [(2 * NS, F32)], [],
                     scratch=[pltpu.VMEM((8, 2 * NS), F32)])[0]


def _scan_bwd(pwb, gh, h):
    s = gh.shape[0]
    ts = min(SCAN_TILE, s)
    nblk = ts // 8
    n_tiles = s // ts

    def body(g_ref, h_ref, pw_ref, lam_ref, da_ref, carry, acc):
        @pl.when(pl.program_id(0) == 0)
        def _():
            carry[...] = jnp.zeros_like(carry)
            acc[...] = jnp.zeros_like(acc)

        row = lax.broadcasted_iota(jnp.int32, (8, NS), 0)
        a1 = (pw_ref[7:8, :NS], pw_ref[7:8, NS:])
        a2 = (pw_ref[6:7, :NS], pw_ref[6:7, NS:])
        a4 = (pw_ref[4:5, :NS], pw_ref[4:5, NS:])

        def blk(bb, _):
            r0 = pl.multiple_of((nblk - 1 - bb) * 8, 8)
            xr = g_ref[pl.ds(r0, 8), :NS]
            xi = g_ref[pl.ds(r0, 8), NS:]
            xr, xi = _scan_level(row, xr, xi, a1[0], a1[1], 1, False)
            xr, xi = _scan_level(row, xr, xi, a2[0], a2[1], 2, False)
            xr, xi = _scan_level(row, xr, xi, a4[0], a4[1], 4, False)
            cr, ci = carry[0:1, :NS], carry[0:1, NS:]
            pr, pi = pw_ref[:, :NS], pw_ref[:, NS:]
            lr = xr + pr * cr - pi * ci
            li = xi + pr * ci + pi * cr
            lam_ref[pl.ds(r0, 8), :NS] = lr
            lam_ref[pl.ds(r0, 8), NS:] = li
            nr = jnp.where(row < 7, pltpu.roll(lr, 7, 0), cr)
            ni = jnp.where(row < 7, pltpu.roll(li, 7, 0), ci)
            hr = h_ref[pl.ds(r0, 8), :NS]
            hi = h_ref[pl.ds(r0, 8), NS:]
            acc[:, :NS] += nr * hr + ni * hi
            acc[:, NS:] += ni * hr - nr * hi
            carry[:, :NS] = lr
            carry[:, NS:] = li
            return 0

        lax.fori_loop(0, nblk, blk, 0)

        @pl.when(pl.program_id(0) == n_tiles - 1)
        def _():
            da_ref[...] = jnp.sum(acc[...], axis=0, keepdims=True)

    return _row_call("scan_bwd", body, s, ts, [gh, h], [pwb], [], [(2 * NS, F32)], [(1, 2 * NS)], reverse=True,
                     scratch=[pltpu.VMEM((8, 2 * NS), F32), pltpu.VMEM((8, 2 * NS), F32)])


def _s5out_fwd(h, z3, d_skip, w_c, w_glu):
    def body(h_ref, z3_ref, d_ref, wc, wglu, yl_ref, so_ref):
        yl = _mm(h_ref[...], wc[...]) + d_ref[...] * z3_ref[:, :256]
        yl_ref[...] = yl
        y12 = _mm(_gelu(yl), wglu[...])
        so_ref[...] = y12[:, :1024] * _sigmoid(y12[:, 1024:])

    s = h.shape[0]
    return _row_call("s5out_fwd", body, s, min(ROW_TILE, s), [h, z3], [d_skip], [w_c, w_glu],
                     [(256, F32), (1024, F32)], [])


def _causal_mask_t(t):
    row = lax.broadcasted_iota(jnp.int32, (t, t), 0)
    col = lax.broadcasted_iota(jnp.int32, (t, t), 1)
    return row <= col


def _attn_fwd(q, k, vt):
    s = q.shape[0]
    t = min(ATT_TILE, s)
    nq = s // t
    scale = MLA_QK ** -0.5
    c2 = scale * LOG2E

    def body(q_ref, k_ref, vt_ref, o_ref, lse_ref, s0, s1):
        i = pl.program_id(1)
        qb = q_ref[...]

        def scores(kb, dst):
            st = _mm_nt(k_ref[pl.ds(pl.multiple_of(kb * t, t), t), :], qb)
            dst[...] = st
            return jnp.max(st, axis=0, keepdims=True)

        def update(kb, src, mt, state, masked):
            m, l, acc = state
            if masked:
                keep = _causal_mask_t(t)
                mt = jnp.max(jnp.where(keep, src[...], NEG_INF), axis=0, keepdims=True)
            m_new = jnp.maximum(m, mt)
            if masked:
                p = jnp.exp2((jnp.where(keep, src[...], NEG_INF) - m_new) * c2)
            else:
                p = jnp.exp2((src[...] - m_new) * c2)
            a = jnp.exp2((m - m_new) * c2)
            l = a * l + jnp.sum(p, axis=0, keepdims=True)
            acc = a * acc + _mm(vt_ref[:, pl.ds(pl.multiple_of(kb * t, t), t)], p)
            return m_new, l, acc

        def pair(jj, carry):
            mt_a, state = carry
            mt_b = scores(2 * jj + 1, s1)
            state = update(2 * jj, s0, mt_a, state, False)
            mt_a = scores(2 * jj + 2, s0)
            return mt_a, update(2 * jj + 1, s1, mt_b, state, False)

        def odd_tail(carry):
            mt_a, state = carry
            mt_b = scores(i, s1)
            state = update(i - 1, s0, mt_a, state, False)
            return update(i, s1, mt_b, state, True)

        def even_tail(carry):
            mt_a, state = carry
            return update(i, s0, mt_a, state, True)

        init = (jnp.full((1, t), NEG_INF, F32), jnp.zeros((1, t), F32), jnp.zeros((HEAD_PAD, t), F32))
        carry = lax.fori_loop(0, i // 2, pair, (scores(0, s0), init))
        m, l, acc = lax.cond(i % 2 == 1, odd_tail, even_tail, carry)
        o_ref[...] = (acc / l).T
        lse_ref[...] = m * scale + jnp.log(l)

    return pl.pallas_call(
        body, name="attn_fwd", grid=(MLA_HEADS, nq),
        in_specs=[pl.BlockSpec((t, HEAD_PAD), lambda h, i: (i, h)),
                  pl.BlockSpec((s, HEAD_PAD), lambda h, i: (0, h)),
                  pl.BlockSpec((HEAD_PAD, s), lambda h, i: (h, 0))],
        out_specs=[pl.BlockSpec((t, HEAD_PAD), lambda h, i: (i, h)),
                   pl.BlockSpec((None, 1, t), lambda h, i: (h, 0, i))],
        out_shape=[jax.ShapeDtypeStruct((s, MLA_HEADS * HEAD_PAD), F32),
                   jax.ShapeDtypeStruct((MLA_HEADS, 1, s), F32)],
        scratch_shapes=[pltpu.VMEM((t, t), F32)] * 2,
        compiler_params=pltpu.CompilerParams(dimension_semantics=("arbitrary", "arbitrary"),
                                             vmem_limit_bytes=VMEM_LIMIT),
    )(q, k, vt)


def _attn_bwd(q, k, v, kt, do, lse_t, dl_t):
    s = q.shape[0]
    t = min(ATT_TILE, s)
    nq = s // t
    scale = MLA_QK ** -0.5
    c2 = scale * LOG2E

    def body(q_ref, k_ref, v_ref, kt_ref, do_ref, lse_ref, dl_ref, dq_ref, dk_ref, dv_ref, dqt, s0, p0, s1, p1):
        j = pl.program_id(1)
        n = nq - 1 - j

        @pl.when(j == 0)
        def _():
            dqt[...] = jnp.zeros_like(dqt)

        kk = k_ref[...]
        vv = v_ref[...]
        ktb = kt_ref[...]

        def rows(m):
            return pl.ds(pl.multiple_of(jnp.where(m < n, j + 1 + m, j) * t, t), t)

        def first(m, sbuf, pbuf):
            r = rows(m)
            sbuf[...] = _mm_nt(kk, q_ref[r, :])
            pbuf[...] = _mm_nt(vv, do_ref[r, :])

        def finish(m, sbuf, pbuf, acc, masked):
            dk, dv = acc
            r = rows(m)
            pt = jnp.exp2(sbuf[...] * c2 - lse_ref[:, r] * LOG2E)
            if masked:
                pt = jnp.where(_causal_mask_t(t), pt, 0.0)
            dv = dv + _mm(pt, do_ref[r, :])
            dst = (pt * (pbuf[...] - dl_ref[:, r])).astype(BF16)
            dk = dk + _mm(dst, q_ref[r, :])
            dqt[:, r] += _mm(ktb, dst)
            return dk, dv

        def pair(jj, acc):
            first(2 * jj + 1, s1, p1)
            acc = finish(2 * jj, s0, p0, acc, False)
            first(2 * jj + 2, s0, p0)
            return finish(2 * jj + 1, s1, p1, acc, False)

        def odd_tail(acc):
            first(n, s1, p1)
            acc = finish(n - 1, s0, p0, acc, False)
            return finish(n, s1, p1, acc, True)

        def even_tail(acc):
            return finish(n, s0, p0, acc, True)

        zero = jnp.zeros((t, HEAD_PAD), F32)
        first(0, s0, p0)
        acc = lax.fori_loop(0, n // 2, pair, (zero, zero))
        dk, dv = lax.cond(n % 2 == 1, odd_tail, even_tail, acc)
        dk_ref[...] = dk * scale
        dv_ref[...] = dv

        @pl.when(j == nq - 1)
        def _():
            for cc in range(nq):
                dq_ref[cc * t:(cc + 1) * t, :] = dqt[:, cc * t:(cc + 1) * t].T * scale

    full = pl.BlockSpec((s, HEAD_PAD), lambda h, j: (0, h))
    tile = pl.BlockSpec((t, HEAD_PAD), lambda h, j: (j, h))
    stat = pl.BlockSpec((None, 1, s), lambda h, j: (h, 0, 0))
    return pl.pallas_call(
        body, name="attn_bwd", grid=(MLA_HEADS, nq),
        in_specs=[full, tile, tile, pl.BlockSpec((HEAD_PAD, t), lambda h, j: (h, j)), full, stat, stat],
        out_specs=[full, tile, tile],
        out_shape=[jax.ShapeDtypeStruct((s, MLA_HEADS * HEAD_PAD), F32)] * 3,
        scratch_shapes=[pltpu.VMEM((HEAD_PAD, s), F32)] + [pltpu.VMEM((t, t), F32)] * 4,
        compiler_params=pltpu.CompilerParams(dimension_semantics=("arbitrary", "arbitrary"),
                                             vmem_limit_bytes=VMEM_LIMIT),
    )(q, k, v, kt, do, lse_t, dl_t)


def _post_forward(o, s_out, gate, h0, memk, memv, w_oa, w_o, w_xq, w_xo, g1, b1, g2, b2):
    a_out = _mm(o, w_oa)
    sg_s = _sigmoid(gate[:, :1024])
    sg_a = _sigmoid(gate[:, 1024:])
    mixin = sg_s * s_out + sg_a * a_out
    r1 = DN_ALPHA * h0 + _mm(mixin, w_o)
    h1, xh1, rs1 = _ln_fwd(r1, g1, b1)
    qxb = _mm(h1, w_xq).astype(BF16)
    ps, oxs = [], []
    for hh in range(XATTN_HEADS):
        sl = slice(hh * XATTN_HD, (hh + 1) * XATTN_HD)
        sc = _mm_nt(qxb[:, sl], memk[:, sl]) * (XATTN_HD ** -0.5)
        e = jnp.exp(sc - jnp.max(sc, axis=1, keepdims=True))
        p = e / jnp.sum(e, axis=1, keepdims=True)
        ps.append(p)
        oxs.append(_mm(p, memv[:, sl]))
    ox = jnp.concatenate(oxs, axis=1)
    r2 = DN_ALPHA * h1 + _mm(ox, w_xo)
    return dict(a_out=a_out, sg_s=sg_s, sg_a=sg_a, mixin=mixin, h1=h1, xh1=xh1, rs1=rs1, qxb=qxb, ps=ps, ox=ox, r2=r2)


def _post_fwd(o, s_out, gate, h0, memk, memv, lns, w_oa, w_o, w_xq, w_xo):
    def body(o_ref, so_ref, gate_ref, h0_ref, mk_ref, mv_ref, g1, b1, g2, b2, woa, wo, wxq, wxo,
             r2_ref, mixin_ref, h1_ref, ox_ref):
        f = _post_forward(o_ref[...], so_ref[...], gate_ref[...], h0_ref[...], mk_ref[...], mv_ref[...],
                          woa[...], wo[...], wxq[...], wxo[...], g1[...], b1[...], g2[...], b2[...])
        r2_ref[...] = f["r2"]
        mixin_ref[...] = f["mixin"].astype(BF16)
        h1_ref[...] = f["h1"].astype(BF16)
        ox_ref[...] = f["ox"].astype(BF16)

    s = o.shape[0]
    return _row_call("post_fwd", body, s, min(ROW_TILE, s), [o, s_out, gate, h0], [memk, memv, *lns],
                     [w_oa, w_o, w_xq, w_xo], [(1024, F32), (1024, BF16), (1024, BF16), (1024, BF16)], [])


def _mlp(r2, target, g2, b2, g3, b3, w_up, w_down):
    cw = 1024
    n_chunk = MLP_HIDDEN // cw

    def body(r2_ref, tgt_ref, g2_ref, b2_ref, g3_ref, b3_ref, wup, wdn,
             dh2_ref, h2b_ref, dup_ref, act_ref, dff_ref, loss_ref, dg3_ref, db3_ref, up_scr):
        @pl.when(pl.program_id(0) == 0)
        def _():
            loss_ref[...] = jnp.zeros_like(loss_ref)
            dg3_ref[...] = jnp.zeros_like(dg3_ref)
            db3_ref[...] = jnp.zeros_like(db3_ref)

        h2, _, _ = _ln_fwd(r2_ref[...], g2_ref[...], b2_ref[...])
        h2b = h2.astype(BF16)
        h2b_ref[...] = h2b
        ff = jnp.zeros(h2.shape, F32)
        for c in range(n_chunk):
            sl = slice(c * cw, (c + 1) * cw)
            a = jnp.maximum(_mm(h2b, wup[:, sl]), 0.0)
            up_scr[:, sl] = a
            actb = (a * a).astype(BF16)
            act_ref[:, sl] = actb
            ff = ff + _mm(actb, wdn[sl, :])
        h3, xh3, rs3 = _ln_fwd(DN_ALPHA * h2 + ff, g3_ref[...], b3_ref[...])
        err = h3 - tgt_ref[...]
        loss_ref[...] += 0.5 * jnp.sum(err * err) * (1.0 / D_MODEL)
        dr3, dg3, db3 = _ln_bwd(err * (1.0 / D_MODEL), xh3, rs3, g3_ref[...])
        dg3_ref[...] += dg3
        db3_ref[...] += db3
        dffb = dr3.astype(BF16)
        dff_ref[...] = dffb
        dh2 = DN_ALPHA * dr3
        for c in range(n_chunk):
            sl = slice(c * cw, (c + 1) * cw)
            dupb = (_mm_nt(dffb, wdn[sl, :]) * (2.0 * up_scr[:, sl])).astype(BF16)
            dup_ref[:, sl] = dupb
            dh2 = dh2 + _mm_nt(dupb, wup[:, sl])
        dh2_ref[...] = dh2

    s = r2.shape[0]
    ts = min(ROW_TILE, s)
    return _row_call("mlp", body, s, ts, [r2, target], [g2, b2, g3, b3], [w_up, w_down],
                     [(1024, F32), (1024, BF16), (MLP_HIDDEN, BF16), (MLP_HIDDEN, BF16), (1024, BF16)],
                     [(8, 128), (1, 1024), (1, 1024)], scratch=[pltpu.VMEM((ts, MLP_HIDDEN), F32)])


def _post_bwd(dh2, o, s_out, gate, h0, memk, memv, lns, w_oa, w_o, w_xq, w_xo):
    def body(dh2_ref, o_ref, so_ref, gate_ref, h0_ref, mk_ref, mv_ref, g1, b1, g2, b2, woa, wo, wxq, wxo,
             do_ref, dl_ref, dso_ref, dgate_ref, dh0_ref, daout_ref, dmix_ref, dqx_ref, dxa_ref,
             dmk_ref, dmv_ref, dg1_ref, db1_ref, dg2_ref, db2_ref):
        @pl.when(pl.program_id(0) == 0)
        def _():
            for r in (dmk_ref, dmv_ref, dg1_ref, db1_ref, dg2_ref, db2_ref):
                r[...] = jnp.zeros_like(r)

        o = o_ref[...]
        s_out = so_ref[...]
        memk, memv = mk_ref[...], mv_ref[...]
        f = _post_forward(o, s_out, gate_ref[...], h0_ref[...], memk, memv,
                          woa[...], wo[...], wxq[...], wxo[...], g1[...], b1[...], g2[...], b2[...])
        _, xh2, rs2 = _ln_fwd(f["r2"], g2[...], b2[...])
        dr2, dg2, db2 = _ln_bwd(dh2_ref[...], xh2, rs2, g2[...])
        dg2_ref[...] += dg2
        db2_ref[...] += db2
        dxab = dr2.astype(BF16)
        dxa_ref[...] = dxab
        dox = _mm_nt(dxab, wxo[...])
        dqs = []
        for hh in range(XATTN_HEADS):
            sl = slice(hh * XATTN_HD, (hh + 1) * XATTN_HD)
            p = f["ps"][hh]
            doxh = dox[:, sl].astype(BF16)
            dp = _mm_nt(doxh, memv[:, sl])
            ds = (p * (dp - jnp.sum(dp * p, axis=1, keepdims=True)) * (XATTN_HD ** -0.5)).astype(BF16)
            dqs.append(_mm(ds, memk[:, sl]))
            dmk_ref[:, sl] += _mm_tn(ds, f["qxb"][:, sl])
            dmv_ref[:, sl] += _mm_tn(p, doxh)
        dqxb = jnp.concatenate(dqs, axis=1).astype(BF16)
        dqx_ref[...] = dqxb
        dh1 = DN_ALPHA * dr2 + _mm_nt(dqxb, wxq[...])
        dr1, dg1, db1 = _ln_bwd(dh1, f["xh1"], f["rs1"], g1[...])
        dg1_ref[...] += dg1
        db1_ref[...] += db1
        dh0_ref[...] = DN_ALPHA * dr1
        dmixb = dr1.astype(BF16)
        dmix_ref[...] = dmixb
        dmixin = _mm_nt(dmixb, wo[...])
        sg_s, sg_a = f["sg_s"], f["sg_a"]
        dso_ref[...] = dmixin * sg_s
        daoutb = (dmixin * sg_a).astype(BF16)
        daout_ref[...] = daoutb
        dgate_ref[:, :1024] = dmixin * s_out * sg_s * (1.0 - sg_s)
        dgate_ref[:, 1024:] = dmixin * f["a_out"] * sg_a * (1.0 - sg_a)
        d_o = _mm_nt(daoutb, woa[...])
        do_ref[...] = d_o.astype(BF16)
        lane = lax.broadcasted_iota(jnp.int32, (o.shape[0], HEAD_PAD), 1)
        dlc = jnp.zeros((o.shape[0], HEAD_PAD), F32)
        for hh in range(MLA_HEADS):
            sl = slice(hh * HEAD_PAD, (hh + 1) * HEAD_PAD)
            dl = jnp.sum(d_o[:, sl] * o[:, sl], axis=1, keepdims=True)
            dlc = dlc + jnp.where(lane == hh, dl, 0.0)
        dl_ref[...] = dlc

    s = o.shape[0]
    m = memk.shape[0]
    return _row_call(
        "post_bwd", body, s, min(ROW_TILE, s), [dh2, o, s_out, gate, h0], [memk, memv, *lns],
        [w_oa, w_o, w_xq, w_xo],
        [(1024, BF16), (HEAD_PAD, F32), (1024, F32), (2048, F32), (1024, F32),
         (1024, BF16), (1024, BF16), (1024, BF16), (1024, BF16)],
        [(m, 1024), (m, 1024), (1, 1024), (1, 1024), (1, 1024), (1, 1024)])


def _s5out_bwd(d_so, yl, z3, d_skip, w_c, w_glu):
    def body(dso_ref, yl_ref, z3_ref, d_ref, wc, wglu, gh_ref, dup_ref, yg_ref, dy12_ref, dyl_ref, dd_ref):
        @pl.when(pl.program_id(0) == 0)
        def _():
            dd_ref[...] = jnp.zeros_like(dd_ref)

        yl = yl_ref[...]
        ygb = _gelu(yl).astype(BF16)
        yg_ref[...] = ygb
        y12 = _mm(ygb, wglu[...])
        sg = _sigmoid(y12[:, 1024:])
        dso = dso_ref[...]
        dy12b = jnp.concatenate([dso * sg, dso * y12[:, :1024] * sg * (1.0 - sg)], axis=1).astype(BF16)
        dy12_ref[...] = dy12b
        dyl = _mm_nt(dy12b, wglu[...]) * _gelu_grad(yl)
        dylb = dyl.astype(BF16)
        dyl_ref[...] = dylb
        gh_ref[...] = _mm_nt(dylb, wc[...])
        dup_ref[...] = dyl * d_ref[...]
        dd_ref[...] += jnp.sum(dyl * z3_ref[:, :256], axis=0, keepdims=True)

    s = d_so.shape[0]
    return _row_call("s5out_bwd", body, s, min(ROW_TILE, s), [d_so, yl, z3], [d_skip], [w_c, w_glu],
                     [(2 * NS, F32), (256, F32), (256, BF16), (2048, BF16), (256, BF16)], [(1, 256)])


def _in_bwd(x, tab, z3, dq, dk, dv, lam, du_p, dgate, dh0p, ln_g, ln_b, qg, kvg, w_inx, w_b, w_q2, w_kv2):
    def body(x_ref, tab_ref, z3_ref, dq_ref, dk_ref, dv_ref, lam_ref, dup_ref, dgate_ref, dh0p_ref,
             lng_ref, lnb_ref, qg_ref, kvg_ref, winx, wb, wq2, wkv2,
             dx_ref, h0b_ref, dz_ref, ub_ref, cqn_ref, ckvn_ref, dq2_ref, dkv2_ref,
             dlng_ref, dlnb_ref, dqg_ref, dkvg_ref):
        @pl.when(pl.program_id(0) == 0)
        def _():
            for r in (dlng_ref, dlnb_ref, dqg_ref, dkvg_ref):
                r[...] = jnp.zeros_like(r)

        h0, xh0, rs0 = _ln_fwd(x_ref[...], lng_ref[...], lnb_ref[...])
        h0b_ref[...] = h0.astype(BF16)
        z3 = z3_ref[...]
        ub_ref[...] = z3[:, :256].astype(BF16)
        tab_v = tab_ref[...]
        cq1, cq2, ck1 = tab_v[:, :128], tab_v[:, 128:256], tab_v[:, 256:384]
        du = _mm_nt(lam_ref[...], wb[...]) + dup_ref[...]
        dq = dq_ref[...]
        dq2b = jnp.concatenate([dq * _tile_heads(cq1), dq * _tile_heads(cq2)], axis=1).astype(BF16)
        dq2_ref[...] = dq2b
        cqg, cqn, rq = _rms_fwd(z3[:, 256:512], qg_ref[...])
        cqn_ref[...] = cqg.astype(BF16)
        dcq, dqg = _rms_bwd(_mm_nt(dq2b, wq2[...]), cqn, rq, qg_ref[...])
        dqg_ref[...] += dqg
        dk = dk_ref[...]
        dkv2b = jnp.concatenate([dk, dv_ref[...]], axis=1).astype(BF16)
        dkv2_ref[...] = dkv2b
        ckvg, ckvn, rkv = _rms_fwd(z3[:, 512:768], kvg_ref[...])
        ckvn_ref[...] = ckvg.astype(BF16)
        dckv, dkvg = _rms_bwd(_mm_nt(dkv2b, wkv2[...]), ckvn, rkv, kvg_ref[...])
        dkvg_ref[...] += dkvg
        dkrp = dk[:, :HEAD_PAD]
        for hh in range(1, MLA_HEADS):
            dkrp = dkrp + dk[:, hh * HEAD_PAD:(hh + 1) * HEAD_PAD]
        dzb = jnp.concatenate([du, dcq, dckv, dkrp * ck1, dkrp * cq2, dgate_ref[...]], axis=1).astype(BF16)
        dz_ref[...] = dzb
        dh0 = _mm_nt(dzb, winx[...]) + dh0p_ref[...]
        dx, dg, db = _ln_bwd(dh0, xh0, rs0, lng_ref[...])
        dx_ref[...] = dx
        dlng_ref[...] += dg
        dlnb_ref[...] += db

    s = x.shape[0]
    return _row_call(
        "in_bwd", body, s, min(ROW_TILE, s), [x, tab, z3, dq, dk, dv, lam, du_p, dgate, dh0p],
        [ln_g, ln_b, qg, kvg], [w_inx, w_b, w_q2, w_kv2],
        [(1024, F32), (1024, BF16), (3072, BF16), (256, BF16), (256, BF16), (256, BF16), (2048, BF16), (2048, BF16)],
        [(1, 1024), (1, 1024), (1, 256), (1, 256)])


def _wgrad(name, xs, dy, out_dtype=F32, blocks=1):
    s, k = xs.shape
    n = dy.shape[1]
    nb = n // blocks
    tk, tn = min(k, 1024), min(n, 1024)
    ts = min(s, 2048 if tk <= 256 else 1024)
    per_tile = tn // nb
    assert s % ts == 0 and k % tk == 0 and n % tn == 0 and (blocks == 1 or tn % nb == 0), (name, s, k, n)
    last = s // ts - 1

    def body(x_ref, dy_ref, o_ref, acc):
        @pl.when(pl.program_id(2) == 0)
        def _():
            acc[...] = jnp.zeros_like(acc)

        acc[...] += _mm_tn(x_ref[...], dy_ref[...])

        @pl.when(pl.program_id(2) == last)
        def _():
            if blocks == 1:
                o_ref[...] = acc[...].astype(out_dtype)
            else:
                for d in range(per_tile):
                    o_ref[d] = acc[:, d * nb:(d + 1) * nb].astype(out_dtype)

    if blocks == 1:
        out_shape = jax.ShapeDtypeStruct((k, n), out_dtype)
        out_spec = pl.BlockSpec((tk, tn), lambda a, b, c: (a, b))
    else:
        out_shape = jax.ShapeDtypeStruct((blocks, k, nb), out_dtype)
        out_spec = pl.BlockSpec((per_tile, tk, nb), lambda a, b, c: (b, a, 0))
    return pl.pallas_call(
        body, name=name, grid=(k // tk, n // tn, s // ts),
        in_specs=[pl.BlockSpec((ts, tk), lambda a, b, c: (c, a)), pl.BlockSpec((ts, tn), lambda a, b, c: (c, b))],
        out_specs=out_spec, out_shape=out_shape, scratch_shapes=[pltpu.VMEM((tk, tn), F32)],
        compiler_params=pltpu.CompilerParams(dimension_semantics=("arbitrary", "arbitrary", "arbitrary"),
                                             vmem_limit_bytes=VMEM_LIMIT),
    )(xs, dy)


def _mesh_pos():
    x, y, c = lax.axis_index("x"), lax.axis_index("y"), lax.axis_index("c")
    return x, y, c


def _peer(x, y, c, k):
    px = 1 - x if k & 4 else x
    py = 1 - y if k & 2 else y
    pc = 1 - c if k & 1 else c
    return (px, py, pc), 4 * px + 2 * py + pc


def _all_to_all(name, arrays, gather):
    n = len(arrays)

    def body(*refs):
        src, dst = refs[:n], refs[n:2 * n]
        send_sems, recv_sems, local_sems = refs[2 * n:]
        x, y, c = _mesh_pos()
        me = 4 * x + 2 * y + c

        def block(i, d):
            return src[i] if gather else src[i].at[d]

        own = [pltpu.make_async_copy(block(i, me), dst[i].at[me], local_sems.at[i]) for i in range(n)]
        for cp in own:
            cp.start()
        sends = []
        for k in range(1, N_DEV):
            peer, pid = _peer(x, y, c, k)
            for i in range(n):
                idx = (k - 1) * n + i
                cp = pltpu.make_async_remote_copy(src_ref=block(i, pid), dst_ref=dst[i].at[me],
                                                  send_sem=send_sems.at[idx], recv_sem=recv_sems.at[idx],
                                                  device_id=peer, device_id_type=MESH)
                cp.start()
                sends.append(cp)
        for k in range(1, N_DEV):
            peer, pid = _peer(x, y, c, k)
            for i in range(n):
                idx = (k - 1) * n + i
                pltpu.make_async_remote_copy(src_ref=block(i, pid), dst_ref=dst[i].at[pid],
                                             send_sem=send_sems.at[idx], recv_sem=recv_sems.at[idx],
                                             device_id=peer, device_id_type=MESH).wait_recv()
        for cp in sends:
            cp.wait_send()
        for cp in own:
            cp.wait()

    n_sem = n * (N_DEV - 1)
    return pl.pallas_call(
        body, name=name,
        in_specs=[pl.BlockSpec(memory_space=pl.ANY)] * n,
        out_specs=[pl.BlockSpec(memory_space=pl.ANY)] * n,
        out_shape=[jax.ShapeDtypeStruct((N_DEV,) + a.shape[-2:], a.dtype) for a in arrays],
        scratch_shapes=[pltpu.SemaphoreType.DMA((n_sem,)), pltpu.SemaphoreType.DMA((n_sem,)),
                        pltpu.SemaphoreType.DMA((n,))],
    )(*arrays)


_HBM = pl.BlockSpec(memory_space=pltpu.HBM)
_SEM = pl.BlockSpec(memory_space=pltpu.SEMAPHORE)


def _exchange_start(name, arrays, gather, after=None):
    n = len(arrays)
    n_sem = n * (N_DEV - 1)
    me = 4 * lax.axis_index("x") + 2 * lax.axis_index("y") + lax.axis_index("c")
    lands = []
    for a in arrays:
        own = a[None] if gather else lax.dynamic_slice_in_dim(a, me, 1, 0)
        lands.append(lax.dynamic_update_slice(lax.empty((N_DEV,) + a.shape[-2:], a.dtype), own, (me, 0, 0)))
    n_after = 0 if after is None else 1

    def body(*refs):
        src, land = refs[:n], refs[n:2 * n]
        send_sems, recv_sems = refs[2 * n + n_after], refs[2 * n + n_after + 1]
        token = refs[-1]
        x, y, c = _mesh_pos()
        me_in = 4 * x + 2 * y + c
        for k in range(1, N_DEV):
            peer, pid = _peer(x, y, c, k)
            for i in range(n):
                idx = (k - 1) * n + i
                pltpu.make_async_remote_copy(src_ref=src[i] if gather else src[i].at[pid], dst_ref=land[i].at[me_in],
                                             send_sem=send_sems.at[idx], recv_sem=recv_sems.at[idx],
                                             device_id=peer, device_id_type=MESH).start()
        token[...] = jnp.zeros_like(token)

    operands = [pltpu.with_memory_space_constraint(a, pltpu.HBM) for a in list(arrays) + lands]
    outs = pl.pallas_call(
        body, name=name,
        out_shape=(pltpu.SemaphoreType.DMA((n_sem,)), pltpu.SemaphoreType.DMA((n_sem,)),
                   *[pltpu.HBM(a.shape, a.dtype) for a in list(arrays) + lands],
                   jax.ShapeDtypeStruct((8, 128), F32)),
        in_specs=[_HBM] * (2 * n) + [pl.BlockSpec(memory_space=pl.ANY)] * n_after,
        out_specs=(_SEM, _SEM, *[_HBM] * (2 * n), pl.BlockSpec(memory_space=pltpu.VMEM)),
        input_output_aliases={i: 2 + i for i in range(2 * n)},
        compiler_params=pltpu.CompilerParams(has_side_effects=pltpu.SideEffectType.DATAFLOW_SIDE_EFFECTING),
    )(*operands, *([after] if n_after else []))
    return (gather, outs[0], outs[1], outs[2:2 + n], outs[2 + n:2 + 2 * n]), outs[-1]


def _exchange_wait(name, handle, after):
    gather, send_sems, recv_sems, srcs, lands = handle
    n = len(srcs)

    def body(*refs):
        src, land = refs[:n], refs[n:2 * n]
        s_sems, r_sems = refs[2 * n], refs[2 * n + 1]
        x, y, c = _mesh_pos()
        for k in range(1, N_DEV):
            peer, pid = _peer(x, y, c, k)
            for i in range(n):
                idx = (k - 1) * n + i
                cp = pltpu.make_async_remote_copy(src_ref=src[i] if gather else src[i].at[pid], dst_ref=land[i].at[pid],
                                                  send_sem=s_sems.at[idx], recv_sem=r_sems.at[idx],
                                                  device_id=peer, device_id_type=MESH)
                cp.wait_send()
                cp.wait_recv()

    outs = pl.pallas_call(
        body, name=name,
        out_shape=tuple(pltpu.HBM(a.shape, a.dtype) for a in list(srcs) + list(lands)),
        in_specs=[_HBM] * (2 * n) + [_SEM, _SEM, pl.BlockSpec(memory_space=pl.ANY)],
        out_specs=tuple([_HBM] * (2 * n)),
        input_output_aliases={i: i for i in range(2 * n)},
        compiler_params=pltpu.CompilerParams(has_side_effects=pltpu.SideEffectType.DATAFLOW_SIDE_EFFECTING),
    )(*srcs, *lands, send_sems, recv_sems, after)
    return list(outs[n:])


def _adamw(name, parts, w, m, v):
    a_rows, b_cols = w.shape
    ta = min(a_rows, ADAM_TILE)
    assert a_rows % ta == 0
    c1 = 1.0 - ADAM_B1 ** ADAM_STEP
    c2 = 1.0 - ADAM_B2 ** ADAM_STEP

    def body(p_ref, w_ref, m_ref, v_ref, g_ref, d_ref, mo_ref, vo_ref):
        g = p_ref[0].astype(F32)
        for d in range(1, N_DEV):
            g = g + p_ref[d].astype(F32)
        g_ref[...] = g
        mn = ADAM_B1 * m_ref[...] + (1.0 - ADAM_B1) * g
        vn = ADAM_B2 * v_ref[...] + (1.0 - ADAM_B2) * (g * g)
        mo_ref[...] = mn
        vo_ref[...] = vn
        d_ref[...] = -ADAM_LR * ((mn / c1) / (jnp.sqrt(vn / c2) + ADAM_EPS) + ADAM_WD * w_ref[...])

    row = pl.BlockSpec((ta, b_cols), lambda i: (i, 0))
    return pl.pallas_call(
        body, name=name, grid=(a_rows // ta,),
        in_specs=[pl.BlockSpec((N_DEV, ta, b_cols), lambda i: (0, i, 0)), row, row, row],
        out_specs=[row] * 4,
        out_shape=[jax.ShapeDtypeStruct((a_rows, b_cols), F32)] * 4,
        compiler_params=pltpu.CompilerParams(dimension_semantics=("arbitrary",), vmem_limit_bytes=VMEM_LIMIT),
    )(parts, w, m, v)


def _pack_rows(arrays, rows):
    flat = jnp.concatenate([a.reshape(-1) for a in arrays])
    return jnp.pad(flat, (0, rows * LANES - flat.shape[0])).reshape(rows, LANES)


def _cols_from_blocks(g):
    return g.transpose(1, 0, 2).reshape(g.shape[1], N_DEV * g.shape[2])


def _blocks_from_cols(w):
    return w.reshape(w.shape[0], N_DEV, w.shape[1] // N_DEV).transpose(1, 0, 2)


def _unpack_flat(flat2d, shapes):
    flat = flat2d.reshape(-1)
    out, off = [], 0
    for shp in shapes:
        sz = math.prod(shp)
        out.append(flat[off:off + sz].reshape(shp))
        off += sz
    return out


def _s5_mats(lam_re, lam_im, log_dt, b_re, b_im, c_re, c_im):
    lr = jnp.minimum(lam_re, S5_MAX_RE)
    li = lam_im
    dt = jnp.exp(log_dt)[:, None]
    mag = jnp.exp(lr * dt)
    ang = li * dt
    ab_re = mag * jnp.cos(ang)
    ab_im = mag * jnp.sin(ang)
    den = lr * lr + li * li
    nr = ab_re - 1.0
    f_re = ((nr * lr + ab_im * li) / den)[..., None]
    f_im = ((ab_im * lr - nr * li) / den)[..., None]
    bb_re = f_re * b_re - f_im * b_im
    bb_im = f_re * b_im + f_im * b_re
    eye = jnp.eye(S5_GROUPS, dtype=F32)
    a = jnp.stack([ab_re.reshape(-1), ab_im.reshape(-1)])
    wb = jnp.concatenate([jnp.einsum("gph,gk->ghkp", bb_re, eye).reshape(S5_WIDTH, NS),
                          jnp.einsum("gph,gk->ghkp", bb_im, eye).reshape(S5_WIDTH, NS)], axis=1)
    wc = jnp.concatenate([jnp.einsum("ghp,gk->gpkh", c_re, eye).reshape(NS, S5_WIDTH),
                          -jnp.einsum("ghp,gk->gpkh", c_im, eye).reshape(NS, S5_WIDTH)], axis=0)
    return a, wb, wc


def _power_table(a):
    ar, ai = a[0], a[1]
    rows_r, rows_i = [ar], [ai]
    for _ in range(7):
        pr, pi = rows_r[-1], rows_i[-1]
        rows_r.append(pr * ar - pi * ai)
        rows_i.append(pr * ai + pi * ar)
    return jnp.concatenate([jnp.stack(rows_r), jnp.stack(rows_i)], axis=1)


def _rope_table(positions):
    inv = ROPE_THETA ** (-jnp.arange(0, MLA_ROPE, 2, dtype=F32) / MLA_ROPE)
    ang = positions.astype(F32)[:, None] * inv
    cos, sin = jnp.cos(ang), jnp.sin(ang)
    s = positions.shape[0]
    one, zero, pad = jnp.ones((s, MLA_NOPE), F32), jnp.zeros((s, MLA_NOPE), F32), jnp.zeros((s, 32), F32)
    return jnp.concatenate([one, cos, cos, pad, zero, -sin, sin, pad, zero, cos, cos, pad], axis=1)


def _derived_weights(full):
    w_in = full["w_in"]
    k1, k2 = w_in[:, 768:784], w_in[:, 784:800]
    z64, z32 = jnp.zeros((1024, 64), BF16), jnp.zeros((1024, 32), BF16)
    w_inx = jnp.concatenate([w_in[:, :768], z64, k1, k2, z32, z64, k2, k1, z32, w_in[:, 800:]], axis=1)
    uq = full["w_uq"].reshape(256, MLA_HEADS, MLA_QK)
    nope, r1, r2 = uq[:, :, :64], uq[:, :, 64:80], uq[:, :, 80:]
    zq64, zq32 = jnp.zeros((256, MLA_HEADS, 64), BF16), jnp.zeros((256, MLA_HEADS, 32), BF16)
    w_q2 = jnp.concatenate([jnp.concatenate([nope, r1, r2, zq32], axis=2).reshape(256, 1024),
                            jnp.concatenate([zq64, r2, r1, zq32], axis=2).reshape(256, 1024)], axis=1)
    ukv = full["w_ukv"].reshape(256, MLA_HEADS, 128)
    w_kv2 = jnp.concatenate([jnp.concatenate([ukv[:, :, :64], zq64], axis=2).reshape(256, 1024),
                             jnp.concatenate([ukv[:, :, 64:], zq64], axis=2).reshape(256, 1024)], axis=1)
    return w_inx, w_q2, w_kv2


def _oa_padded(w_oa):
    oa = w_oa.reshape(MLA_HEADS, MLA_V, 1024)
    return jnp.concatenate([oa, jnp.zeros_like(oa)], axis=1).reshape(1024, 1024)


def _fold_grads(d_inx, d_q2, d_kv2):
    d_k1 = d_inx[:, 832:848] + d_inx[:, 976:992]
    d_k2 = d_inx[:, 848:864] + d_inx[:, 960:976]
    d_w_in = jnp.concatenate([d_inx[:, :768], d_k1, d_k2, d_inx[:, 1024:]], axis=1)
    a = d_q2[:, :1024].reshape(256, MLA_HEADS, 128)
    b = d_q2[:, 1024:].reshape(256, MLA_HEADS, 128)
    d_w_uq = jnp.concatenate([a[:, :, :64], a[:, :, 64:80] + b[:, :, 80:96], a[:, :, 80:96] + b[:, :, 64:80]],
                             axis=2).reshape(256, MLA_HEADS * MLA_QK)
    kk = d_kv2[:, :1024].reshape(256, MLA_HEADS, 128)
    vv = d_kv2[:, 1024:].reshape(256, MLA_HEADS, 128)
    d_w_ukv = jnp.concatenate([kk[:, :, :64], vv[:, :, :64]], axis=2).reshape(256, 1024)
    return d_w_in, d_w_uq, d_w_ukv


def kernel(x, mem, positions, ln_in_g, ln_in_b, w_in, s5_lam_re, s5_lam_im, s5_log_dt, s5_b_re, s5_b_im, s5_c_re, s5_c_im, s5_d, w_glu, q_norm_g, w_uq, kv_norm_g, w_ukv, w_oa, w_o, ln1_g, ln1_b, w_xq, w_xk, w_xv, w_xo, ln2_g, ln2_b, w_up, w_down, ln3_g, ln3_b, loss_target, m_ln_in_g, m_ln_in_b, m_w_in, m_s5_lam_re, m_s5_lam_im, m_s5_log_dt, m_s5_b_re, m_s5_b_im, m_s5_c_re, m_s5_c_im, m_s5_d, m_w_glu, m_q_norm_g, m_w_uq, m_kv_norm_g, m_w_ukv, m_w_oa, m_w_o, m_ln1_g, m_ln1_b, m_w_xq, m_w_xk, m_w_xv, m_w_xo, m_ln2_g, m_ln2_b, m_w_up, m_w_down, m_ln3_g, m_ln3_b, v_ln_in_g, v_ln_in_b, v_w_in, v_s5_lam_re, v_s5_lam_im, v_s5_log_dt, v_s5_b_re, v_s5_b_im, v_s5_c_re, v_s5_c_im, v_s5_d, v_w_glu, v_q_norm_g, v_w_uq, v_kv_norm_g, v_w_ukv, v_w_oa, v_w_o, v_ln1_g, v_ln1_b, v_w_xq, v_w_xk, v_w_xv, v_w_xo, v_ln2_g, v_ln2_b, v_w_up, v_w_down, v_ln3_g, v_ln3_b):
    args = dict(locals())
    wts = {n: args[n] for n in WEIGHTS}
    mom = {n: args["m_" + n] for n in WEIGHTS}
    vel = {n: args["v_" + n] for n in WEIGHTS}
    xs, mems, tgt = x[0], mem[0], loss_target[0]
    s = xs.shape[0]
    names_big = [n for (n, _, _, _) in SHARDED]
    kind = {n: kd for (n, kd, _, _) in SHARDED}
    small_rows = _round_up(sum(math.prod(wts[n].shape) for n in SMALL), 8 * LANES) // LANES

    late = [n for n in names_big if n not in EARLY]
    shard = {n: wts[n][0].astype(BF16) for n in names_big}
    g_early = dict(zip(EARLY, _all_to_all("all_gather", [shard[n] for n in EARLY], True)))
    ag_handle, ag_token = _exchange_start("all_gather_late", [shard[n] for n in late], True, after=g_early["w_in"])
    full = {n: _cols_from_blocks(g_early[n]) for n in EARLY}
    w_inx, w_q2, w_kv2 = _derived_weights(full)
    s5_args = (s5_lam_re[0], s5_lam_im[0], s5_log_dt[0], s5_b_re[0], s5_b_im[0], s5_c_re[0], s5_c_im[0])
    (a_mat, w_b, w_c), s5_vjp = jax.vjp(_s5_mats, *s5_args)
    w_bb, w_cb = w_b.astype(BF16), w_c.astype(BF16)
    pw = _power_table(a_mat)
    pwb = _power_table(a_mat * jnp.array([[1.0], [-1.0]], F32))[::-1]
    tab = _rope_table(positions[0])
    row = lambda a: a.reshape(1, -1)
    ln_g, ln_b = row(ln_in_g) + ag_token[0:1, 0:1], row(ln_in_b)
    lns = [ln1_g, ln1_b, ln2_g, ln2_b]

    h0, z3, gate, bu, q, k, v, kt, vt = _in_fwd(xs, tab, ln_g, ln_b, q_norm_g, kv_norm_g, w_inx, w_bb, w_q2, w_kv2)
    h = _scan_fwd(pw, bu)
    yl, s_out = _s5out_fwd(h, z3, s5_d, w_cb, full["w_glu"])
    o, lse_t = _attn_fwd(q, k, vt)
    g_late = dict(zip(late, _exchange_wait("all_gather_late_wait", ag_handle, after=lse_t)))
    full.update({n: g_late[n].reshape(-1, g_late[n].shape[2]) if kind[n] == "row" else _cols_from_blocks(g_late[n])
                 for n in late})
    w_oap = _oa_padded(full["w_oa"])
    memk, memv = _mem_kv(mems, full["w_xk"], full["w_xv"])
    r2, mixin_b, h1_b, ox_b = _post_fwd(o, s_out, gate, h0, memk, memv, lns,
                                        w_oap, full["w_o"], full["w_xq"], full["w_xo"])
    dh2, h2_b, dup_b, act_b, dff_b, loss_acc, d_ln3_g, d_ln3_b = _mlp(
        r2, tgt, ln2_g, ln2_b, ln3_g, ln3_b, full["w_up"], full["w_down"])
    (do_b, delta, d_so, dgate, dh0p, daout_b, dmix_b, dqx_b, dxa_b,
     dmk, dmv, d_ln1_g, d_ln1_b, d_ln2_g, d_ln2_b) = _post_bwd(
        dh2, o, s_out, gate, h0, memk, memv, lns, w_oap, full["w_o"], full["w_xq"], full["w_xo"])
    rows8 = lambda g: g.reshape(N_DEV, g.shape[0] // N_DEV, g.shape[1])
    d_w_oa = _wgrad("wg_oa", o, daout_b).reshape(MLA_HEADS, 128, 1024)[:, :64].reshape(512, 1024)
    send = {
        "w_oa": _blocks_from_cols(d_w_oa).astype(BF16),
        "w_up": _wgrad("wg_up", h2_b, dup_b, BF16, N_DEV),
        "w_o": rows8(_wgrad("wg_o", mixin_b, dmix_b, BF16)),
        "w_xq": rows8(_wgrad("wg_xq", h1_b, dqx_b, BF16)),
        "w_xk": rows8(_wgrad("wg_xk", mems, dmk, BF16)),
        "w_xv": rows8(_wgrad("wg_xv", mems, dmv, BF16)),
        "w_xo": rows8(_wgrad("wg_xo", ox_b, dxa_b, BF16)),
        "w_down": rows8(_wgrad("wg_down", act_b, dff_b, BF16)),
    }
    rs_handle, rs_token = _exchange_start("grad_exchange_late", [send[n] for n in late], False)
    dl_t = delta.T[:MLA_HEADS].reshape(MLA_HEADS, 1, s) + rs_token[0, 0]
    dq, dk, dv = _attn_bwd(q, k, v, kt, do_b, lse_t, dl_t)
    gh, du_p, yg_b, dy12_b, dyl_b, d_s5_d = _s5out_bwd(d_so, yl, z3, s5_d, w_cb, full["w_glu"])
    lam, d_a = _scan_bwd(pwb, gh, h)
    (dx, h0_b, dz_b, u_b, cqn_b, ckvn_b, dq2_b, dkv2_b, d_ln_g, d_ln_b, d_qg, d_kvg) = _in_bwd(
        xs, tab, z3, dq, dk, dv, lam, du_p, dgate, dh0p, ln_g, ln_b, q_norm_g, kv_norm_g, w_inx, w_bb, w_q2, w_kv2)

    d_w_in, d_w_uq, d_w_ukv = _fold_grads(
        _wgrad("wg_in", h0_b, dz_b), _wgrad("wg_q", cqn_b, dq2_b), _wgrad("wg_kv", ckvn_b, dkv2_b))
    send.update({
        "w_in": _blocks_from_cols(d_w_in).astype(BF16), "w_uq": _blocks_from_cols(d_w_uq).astype(BF16),
        "w_ukv": _blocks_from_cols(d_w_ukv).astype(BF16),
        "w_glu": _wgrad("wg_glu", yg_b, dy12_b, BF16, N_DEV),
    })
    d_s5 = s5_vjp((d_a.reshape(2, NS), _wgrad("wg_s5b", u_b, lam), _wgrad("wg_s5c", h, dyl_b)))
    small_grads = {
        "ln_in_g": d_ln_g, "ln_in_b": d_ln_b, "s5_lam_re": d_s5[0], "s5_lam_im": d_s5[1], "s5_log_dt": d_s5[2],
        "s5_b_re": d_s5[3], "s5_b_im": d_s5[4], "s5_c_re": d_s5[5], "s5_c_im": d_s5[6], "s5_d": d_s5_d,
        "q_norm_g": d_qg, "kv_norm_g": d_kvg, "ln1_g": d_ln1_g, "ln1_b": d_ln1_b, "ln2_g": d_ln2_g,
        "ln2_b": d_ln2_b, "ln3_g": d_ln3_g, "ln3_b": d_ln3_b,
    }
    small_send = jnp.broadcast_to(_pack_rows([small_grads[n] for n in SMALL], small_rows)[None],
                                  (N_DEV, small_rows, LANES))

    recv_early = _all_to_all("grad_exchange", [send[n] for n in EARLY] + [small_send], False)
    recv_late = _exchange_wait("grad_exchange_late_wait", rs_handle, after=recv_early[-1])
    recv = dict(zip(list(EARLY) + late, list(recv_early[:-1]) + recv_late))
    results = [dict(), dict(), dict(), dict()]
    for n in names_big:
        parts = recv[n]
        outs = _adamw("adamw_" + n, parts, wts[n][0], mom[n][0], vel[n][0])
        for res, a in zip(results, outs):
            res[n] = a[None]
    small_out = _adamw("adamw_small", recv_early[-1],
                       *[_pack_rows([t[n] for n in SMALL], small_rows) for t in (wts, mom, vel)])
    for res, fs in zip(results, small_out):
        for n, a in zip(SMALL, _unpack_flat(fs, [wts[n].shape for n in SMALL])):
            res[n] = a

    loss = lax.psum(loss_acc[0, 0], AXES)
    return (loss, dx[None], *[res[n] for res in results for n in WEIGHTS])
```

```python
import math

import jax
import jax.numpy as jnp
from jax import lax
from jax.experimental import pallas as pl
from jax.experimental.pallas import tpu as pltpu

F32 = jnp.float32
BF16 = jnp.bfloat16

D_MODEL = 1024
S5_WIDTH = 256
S5_GROUP_CH = 16
S5_GROUPS = 16
S5_STATE = 64
NS = S5_GROUPS * S5_STATE
S5_MAX_RE = -1e-4
MLA_HEADS = 8
MLA_NOPE = 64
MLA_ROPE = 32
MLA_QK = 96
MLA_V = 64
HEAD_PAD = 128
ROPE_THETA = 10000.0
XATTN_HEADS = 4
XATTN_HD = 256
MLP_HIDDEN = 4096
LN_EPS = 1e-5
RMS_EPS = 1e-6
NEG_INF = -1e30
LOG2E = 1.4426950408889634
DN_ALPHA = 2.0 ** 0.25
ADAM_LR = 0.001
ADAM_B1 = 0.9
ADAM_B2 = 0.999
ADAM_EPS = 1e-08
ADAM_WD = 0.01
ADAM_STEP = 10

N_DEV = 8
MESH = pl.DeviceIdType.MESH
LANES = 1024
VMEM_LIMIT = 60 * 1024 * 1024

ROW_TILE = 256
SCAN_TILE = 1024
ATT_TILE = 512
ADAM_TILE = 128

SHARDED = (
    ("w_in", "col", 1024, 2848), ("w_glu", "col", 256, 2048), ("w_uq", "col", 256, 768),
    ("w_ukv", "col", 256, 1024), ("w_oa", "col", 512, 1024), ("w_o", "row", 1024, 1024),
    ("w_xq", "row", 1024, 1024), ("w_xk", "row", 1024, 1024), ("w_xv", "row", 1024, 1024),
    ("w_xo", "row", 1024, 1024), ("w_up", "col", 1024, 4096), ("w_down", "row", 4096, 1024),
)
EARLY = ("w_in", "w_glu", "w_uq", "w_ukv")
SMALL = ("ln_in_g", "ln_in_b", "s5_lam_re", "s5_lam_im", "s5_log_dt", "s5_b_re", "s5_b_im", "s5_c_re",
         "s5_c_im", "s5_d", "q_norm_g", "kv_norm_g", "ln1_g", "ln1_b", "ln2_g", "ln2_b", "ln3_g", "ln3_b")
WEIGHTS = ("ln_in_g", "ln_in_b", "w_in", "s5_lam_re", "s5_lam_im", "s5_log_dt", "s5_b_re", "s5_b_im",
           "s5_c_re", "s5_c_im", "s5_d", "w_glu", "q_norm_g", "w_uq", "kv_norm_g", "w_ukv", "w_oa", "w_o",
           "ln1_g", "ln1_b", "w_xq", "w_xk", "w_xv", "w_xo", "ln2_g", "ln2_b", "w_up", "w_down", "ln3_g", "ln3_b")


def _round_up(n, m):
    return (n + m - 1) // m * m


def _bf(a):
    return a.astype(BF16)


def _mm(a, b):
    return jnp.dot(_bf(a), _bf(b), preferred_element_type=F32)


def _mm_nt(a, b):
    return lax.dot_general(_bf(a), _bf(b), (((1,), (1,)), ((), ())), preferred_element_type=F32)


def _mm_tn(a, b):
    return lax.dot_general(_bf(a), _bf(b), (((0,), (0,)), ((), ())), preferred_element_type=F32)


def _sigmoid(a):
    return 1.0 / (1.0 + jnp.exp(-a))


def _gelu(a):
    return 0.5 * a * (1.0 + lax.erf(a * (2.0 ** -0.5)))


def _gelu_grad(a):
    return 0.5 * (1.0 + lax.erf(a * (2.0 ** -0.5))) + a * jnp.exp(-0.5 * a * a) * (1.0 / math.sqrt(2.0 * math.pi))


def _ln_fwd(a, g, b):
    mu = jnp.mean(a, axis=-1, keepdims=True)
    ac = a - mu
    var = jnp.mean(ac * ac, axis=-1, keepdims=True)
    rstd = lax.rsqrt(var + LN_EPS)
    xhat = ac * rstd
    return xhat * g + b, xhat, rstd


def _ln_bwd(dy, xhat, rstd, g):
    dxh = dy * g
    m1 = jnp.mean(dxh, axis=-1, keepdims=True)
    m2 = jnp.mean(dxh * xhat, axis=-1, keepdims=True)
    dx = rstd * (dxh - m1 - xhat * m2)
    return dx, jnp.sum(dy * xhat, axis=0, keepdims=True), jnp.sum(dy, axis=0, keepdims=True)


def _rms_fwd(a, g):
    r = lax.rsqrt(jnp.mean(a * a, axis=-1, keepdims=True) + RMS_EPS)
    xn = a * r
    return xn * g, xn, r


def _rms_bwd(dy, xn, r, g):
    dxn = dy * g
    dx = r * (dxn - xn * jnp.mean(dxn * xn, axis=-1, keepdims=True))
    return dx, jnp.sum(dy * xn, axis=0, keepdims=True)


def _tile_heads(a):
    return jnp.concatenate([a] * MLA_HEADS, axis=1)


def _row_call(name, body, n_rows, ts, tiled_in, full_in, weights, tiled_out, acc_out, reverse=False, scratch=(),
              tiled_out_t=()):
    n = n_rows // ts
    assert n * ts == n_rows, (name, n_rows, ts)
    nt, nf, nw = len(tiled_in), len(full_in), len(weights)
    nto, nao = len(tiled_out) + len(tiled_out_t), len(acc_out)
    if reverse:
        imap = lambda i: (n - 1 - i, 0)
    else:
        imap = lambda i: (i, 0)
    const = lambda i: (0, 0)

    def kern(*refs):
        ins = refs[:nt + nf]
        w_hbm = refs[nt + nf:nt + nf + nw]
        outs = refs[nt + nf + nw:nt + nf + nw + nto + nao]
        scr = refs[nt + nf + nw + nto + nao:]
        w_vmem = scr[:nw]
        extra = scr[nw + 1:] if nw else scr
        if nw:
            sem = scr[nw]

            @pl.when(pl.program_id(0) == 0)
            def _():
                cps = [pltpu.make_async_copy(w_hbm[k], w_vmem[k], sem.at[k]) for k in range(nw)]
                for cp in cps:
                    cp.start()
                for cp in cps:
                    cp.wait()
        body(*ins, *w_vmem, *outs, *extra)

    in_specs = [pl.BlockSpec((ts, a.shape[1]), imap) for a in tiled_in]
    in_specs += [pl.BlockSpec(a.shape, const) for a in full_in]
    in_specs += [pl.BlockSpec(memory_space=pl.ANY) for _ in weights]
    assert not (reverse and tiled_out_t)
    out_shape = [jax.ShapeDtypeStruct((n_rows, c), dt) for (c, dt) in tiled_out]
    out_shape += [jax.ShapeDtypeStruct((c, n_rows), dt) for (c, dt) in tiled_out_t]
    out_shape += [jax.ShapeDtypeStruct(shp, F32) for shp in acc_out]
    out_specs = [pl.BlockSpec((ts, c), imap) for (c, dt) in tiled_out]
    out_specs += [pl.BlockSpec((c, ts), lambda i: (0, i)) for (c, dt) in tiled_out_t]
    out_specs += [pl.BlockSpec(shp, const) for shp in acc_out]
    scratch_shapes = [pltpu.VMEM(w.shape, w.dtype) for w in weights]
    if nw:
        scratch_shapes.append(pltpu.SemaphoreType.DMA((nw,)))
    scratch_shapes += list(scratch)
    return pl.pallas_call(
        kern, name=name, grid=(n,), in_specs=in_specs, out_specs=out_specs, out_shape=out_shape,
        scratch_shapes=scratch_shapes,
        compiler_params=pltpu.CompilerParams(dimension_semantics=("arbitrary",), vmem_limit_bytes=VMEM_LIMIT),
    )(*tiled_in, *full_in, *weights)


def _mem_kv(mem, w_xk, w_xv):
    m = mem.shape[0]

    def body(mem_ref, wk_ref, wv_ref, k_ref, v_ref):
        mb = mem_ref[...]
        k_ref[...] = _mm(mb, wk_ref[...]).astype(BF16)
        v_ref[...] = _mm(mb, wv_ref[...]).astype(BF16)

    return pl.pallas_call(
        body, name="mem_kv",
        out_shape=[jax.ShapeDtypeStruct((m, D_MODEL), BF16)] * 2,
        compiler_params=pltpu.CompilerParams(vmem_limit_bytes=VMEM_LIMIT),
    )(mem, w_xk, w_xv)


def _in_fwd(x, tab, ln_g, ln_b, qg, kvg, w_inx, w_b, w_q2, w_kv2):
    def body(x_ref, tab_ref, lng_ref, lnb_ref, qg_ref, kvg_ref, winx, wb, wq2, wkv2,
             h0_ref, z3_ref, gate_ref, bu_ref, q_ref, k_ref, v_ref, kt_ref, vt_ref):
        h0, _, _ = _ln_fwd(x_ref[...], lng_ref[...], lnb_ref[...])
        h0_ref[...] = h0
        z = _mm(h0, winx[...])
        z3_ref[...] = z[:, :768]
        gate_ref[...] = z[:, 1024:]
        bu_ref[...] = _mm(z[:, :256], wb[...])
        tab_v = tab_ref[...]
        cq1, cq2, ck1 = tab_v[:, :128], tab_v[:, 128:], tab_v[:, :128]
        cqn, _, _ = _rms_fwd(z[:, 256:512], qg_ref[...])
        q2 = _mm(cqn, wq2[...])
        q_ref[...] = (q2[:, :1024] * _tile_heads(cq1) + q2[:, 1024:] * _tile_heads(cq2)).astype(BF16)
        ckvn, _, _ = _rms_fwd(z[:, 512:768], kvg_ref[...])
        kv2 = _mm(ckvn, wkv2[...])
        krp = z[:, 768:896] * ck1 + z[:, 896:1024] * cq2
        kf = kv2[:, :1024] + _tile_heads(krp)
        k_ref[...] = kf.astype(BF16)
        v_ref[...] = kv2[:, 1024:].astype(BF16)
        kt_ref[...] = kf.T.astype(BF16)
        vt_ref[...] = kv2[:, 1024:].T.astype(BF16)

    s = x.shape[0]
    return _row_call(
        "in_fwd", body, s, min(ROW_TILE, s), [x, tab], [ln_g, ln_b, qg, kvg], [w_inx, w_b, w_q2, w_kv2],
        [(1024, F32), (768, F32), (2048, F32), (2048, F32), (1024, BF16), (1024, BF16), (1024, BF16)], [],
        tiled_out_t=[(1024, BF16), (1024, BF16)])


def _scan_level(row, xr, xi, ar, ai, k, forward):
    if forward:
        keep = row >= k
        sr, si = pltpu.roll(xr, k, 0), pltpu.roll(xi, k, 0)
    else:
        keep = row < 8 - k
        sr, si = pltpu.roll(xr, 8 - k, 0), pltpu.roll(xi, 8 - k, 0)
    sr = jnp.where(keep, sr, 0.0)
    si = jnp.where(keep, si, 0.0)
    return xr + ar * sr - ai * si, xi + ar * si + ai * sr


def _scan_fwd(pw, bu):
    s = bu.shape[0]
    ts = min(SCAN_TILE, s)
    nblk = ts // 8

    def body(bu_ref, pw_ref, h_ref, carry):
        @pl.when(pl.program_id(0) == 0)
        def _():
            carry[...] = jnp.zeros_like(carry)

        row = lax.broadcasted_iota(jnp.int32, (8, NS), 0)
        a1 = (pw_ref[0:1, :NS], pw_ref[0:1, NS:])
        a2 = (pw_ref[1:2, :NS], pw_ref[1:2, NS:])
        a4 = (pw_ref[3:4, :NS], pw_ref[3:4, NS:])

        def blk(b, _):
            r0 = pl.multiple_of(b * 8, 8)
            xr = bu_ref[pl.ds(r0, 8), :NS]
            xi = bu_ref[pl.ds(r0, 8), NS:]
            xr, xi = _scan_level(row, xr, xi, a1[0], a1[1], 1, True)
            xr, xi = _scan_level(row, xr, xi, a2[0], a2[1], 2, True)
            xr, xi = _scan_level(row, xr, xi, a4[0], a4[1], 4, True)
            cr, ci = carry[7:8, :NS], carry[7:8, NS:]
            pr, pi = pw_ref[:, :NS], pw_ref[:, NS:]
            hr = xr + pr * cr - pi * ci
            hi = xi + pr * ci + pi * cr
            h_ref[pl.ds(r0, 8), :NS] = hr
            h_ref[pl.ds(r0, 8), NS:] = hi
            carry[:, :NS] = hr
            carry[:, NS:] = hi
            return 0

        lax.fori_loop(0, nblk, blk, 0)

    return _row_call("scan_fwd", body, s, ts, [bu], [pw], [], [(2 * NS, F32)], [],
                     scratch=[pltpu.VMEM((8, 2 * NS), F32)])[0]


def _scan_bwd(pwb, gh, h):
    s = gh.shape[0]
    ts = min(SCAN_TILE, s)
    nblk = ts // 8
    n_tiles = s // ts

    def body(g_ref, h_ref, pw_ref, lam_ref, da_ref, carry, acc):
        @pl.when(pl.program_id(0) == 0)
        def _():
            carry[...] = jnp.zeros_like(carry)
            acc[...] = jnp.zeros_like(acc)

        row = lax.broadcasted_iota(jnp.int32, (8, NS), 0)
        a1 = (pw_ref[7:8, :NS], pw_ref[7:8, NS:])
        a2 = (pw_ref[6:7, :NS], pw_ref[6:7, NS:])
        a4 = (pw_ref[4:5, :NS], pw_ref[4:5, NS:])

        def blk(bb, _):
            r0 = pl.multiple_of((nblk - 1 - bb) * 8, 8)
            xr = g_ref[pl.ds(r0, 8), :NS]
            xi = g_ref[pl.ds(r0, 8), NS:]
            xr, xi = _scan_level(row, xr, xi, a1[0], a1[1], 1, False)
            xr, xi = _scan_level(row, xr, xi, a2[0], a2[1], 2, False)
            xr, xi = _scan_level(row, xr, xi, a4[0], a4[1], 4, False)
            cr, ci = carry[0:1, :NS], carry[0:1, NS:]
            pr, pi = pw_ref[:, :NS], pw_ref[:, NS:]
            lr = xr + pr * cr - pi * ci
            li = xi + pr * ci + pi * cr
            lam_ref[pl.ds(r0, 8), :NS] = lr
            lam_ref[pl.ds(r0, 8), NS:] = li
            nr = jnp.where(row < 7, pltpu.roll(lr, 7, 0), cr)
            ni = jnp.where(row < 7, pltpu.roll(li, 7, 0), ci)
            hr = h_ref[pl.ds(r0, 8), :NS]
            hi = h_ref[pl.ds(r0, 8), NS:]
            acc[:, :NS] += nr * hr + ni * hi
            acc[:, NS:] += ni * hr - nr * hi
            carry[:, :NS] = lr
            carry[:, NS:] = li
            return 0

        lax.fori_loop(0, nblk, blk, 0)

        @pl.when(pl.program_id(0) == n_tiles - 1)
        def _():
            da_ref[...] = jnp.sum(acc[...], axis=0, keepdims=True)

    return _row_call("scan_bwd", body, s, ts, [gh, h], [pwb], [], [(2 * NS, F32)], [(1, 2 * NS)], reverse=True,
                     scratch=[pltpu.VMEM((8, 2 * NS), F32), pltpu.VMEM((8, 2 * NS), F32)])


def _s5out_fwd(h, z3, d_skip, w_c, w_glu):
    def body(h_ref, z3_ref, d_ref, wc, wglu, yl_ref, so_ref):
        yl = _mm(h_ref[...], wc[...]) + d_ref[...] * z3_ref[:, :256]
        yl_ref[...] = yl
        y12 = _mm(_gelu(yl), wglu[...])
        so_ref[...] = y12[:, :1024] * _sigmoid(y12[:, 1024:])

    s = h.shape[0]
    return _row_call("s5out_fwd", body, s, min(ROW_TILE, s), [h, z3], [d_skip], [w_c, w_glu],
                     [(256, F32), (1024, F32)], [])


def _causal_mask_t(t):
    row = lax.broadcasted_iota(jnp.int32, (t, t), 0)
    col = lax.broadcasted_iota(jnp.int32, (t, t), 1)
    return row <= col


def _attn_fwd(q, k, vt):
    s = q.shape[0]
    t = min(ATT_TILE, s)
    nq = s // t
    scale = MLA_QK ** -0.5
    c2 = scale * LOG2E

    def body(q_ref, k_ref, vt_ref, o_ref, lse_ref, s0, s1):
        i = pl.program_id(1)
        qb = q_ref[...]

        def scores(kb, dst):
            st = _mm_nt(k_ref[pl.ds(pl.multiple_of(kb * t, t), t), :], qb)
            dst[...] = st
            return jnp.max(st, axis=0, keepdims=True)

        def update(kb, src, mt, state, masked):
            m, l, acc = state
            if masked:
                keep = _causal_mask_t(t)
                mt = jnp.max(jnp.where(keep, src[...], NEG_INF), axis=0, keepdims=True)
            m_new = jnp.maximum(m, mt)
            if masked:
                p = jnp.exp2((jnp.where(keep, src[...], NEG_INF) - m_new) * c2)
            else:
                p = jnp.exp2((src[...] - m_new) * c2)
            a = jnp.exp2((m - m_new) * c2)
            l = a * l + jnp.sum(p, axis=0, keepdims=True)
            acc = a * acc + _mm(vt_ref[:, pl.ds(pl.multiple_of(kb * t, t), t)], p)
            return m_new, l, acc

        def pair(jj, carry):
            mt_a, state = carry
            mt_b = scores(2 * jj + 1, s1)
            state = update(2 * jj, s0, mt_a, state, False)
            mt_a = scores(2 * jj + 2, s0)
            return mt_a, update(2 * jj + 1, s1, mt_b, state, False)

        def odd_tail(carry):
            mt_a, state = carry
            mt_b = scores(i, s1)
            state = update(i - 1, s0, mt_a, state, False)
            return update(i, s1, mt_b, state, True)

        def even_tail(carry):
            mt_a, state = carry
            return update(i, s0, mt_a, state, True)

        init = (jnp.full((1, t), NEG_INF, F32), jnp.zeros((1, t), F32), jnp.zeros((HEAD_PAD, t), F32))
        carry = lax.fori_loop(0, i // 2, pair, (scores(0, s0), init))
        m, l, acc = lax.cond(i % 2 == 1, odd_tail, even_tail, carry)
        o_ref[...] = (acc / l).T
        lse_ref[...] = m * scale + jnp.log(l)

    return pl.pallas_call(
        body, name="attn_fwd", grid=(MLA_HEADS, nq),
        in_specs=[pl.BlockSpec((t, HEAD_PAD), lambda h, i: (i, h)),
                  pl.BlockSpec((s, HEAD_PAD), lambda h, i: (0, h)),
                  pl.BlockSpec((HEAD_PAD, s), lambda h, i: (h, 0))],
        out_specs=[pl.BlockSpec((t, HEAD_PAD), lambda h, i: (i, h)),
                   pl.BlockSpec((None, 1, t), lambda h, i: (h, 0, i))],
        out_shape=[jax.ShapeDtypeStruct((s, MLA_HEADS * HEAD_PAD), F32),
                   jax.ShapeDtypeStruct((MLA_HEADS, 1, s), F32)],
        scratch_shapes=[pltpu.VMEM((t, t), F32)] * 2,
        compiler_params=pltpu.CompilerParams(dimension_semantics=("arbitrary", "arbitrary"),
                                             vmem_limit_bytes=VMEM_LIMIT),
    )(q, k, vt)


def _attn_bwd(q, k, v, kt, do, lse_t, dl_t):
    s = q.shape[0]
    t = min(ATT_TILE, s)
    nq = s // t
    scale = MLA_QK ** -0.5
    c2 = scale * LOG2E

    def body(q_ref, k_ref, v_ref, kt_ref, do_ref, lse_ref, dl_ref, dq_ref, dk_ref, dv_ref, dqt, s0, p0, s1, p1):
        j = pl.program_id(1)
        n = nq - 1 - j

        @pl.when(j == 0)
        def _():
            dqt[...] = jnp.zeros_like(dqt)

        kk = k_ref[...]
        vv = v_ref[...]
        ktb = kt_ref[...]

        def rows(m):
            return pl.ds(pl.multiple_of(jnp.where(m < n, j + 1 + m, j) * t, t), t)

        def first(m, sbuf, pbuf):
            r = rows(m)
            sbuf[...] = _mm_nt(kk, q_ref[r, :])
            pbuf[...] = _mm_nt(vv, do_ref[r, :])

        def finish(m, sbuf, pbuf, acc, masked):
            dk, dv = acc
            r = rows(m)
            pt = jnp.exp2(sbuf[...] * c2 - lse_ref[:, r] * LOG2E)
            if masked:
                pt = jnp.where(_causal_mask_t(t), pt, 0.0)
            dv = dv + _mm(pt, do_ref[r, :])
            dst = (pt * (pbuf[...] - dl_ref[:, r])).astype(BF16)
            dk = dk + _mm(dst, q_ref[r, :])
            dqt[:, r] += _mm(ktb, dst)
            return dk, dv

        def pair(jj, acc):
            first(2 * jj + 1, s1, p1)
            acc = finish(2 * jj, s0, p0, acc, False)
            first(2 * jj + 2, s0, p0)
            return finish(2 * jj + 1, s1, p1, acc, False)

        def odd_tail(acc):
            first(n, s1, p1)
            acc = finish(n - 1, s0, p0, acc, False)
            return finish(n, s1, p1, acc, True)

        def even_tail(acc):
            return finish(n, s0, p0, acc, True)

        zero = jnp.zeros((t, HEAD_PAD), F32)
        first(0, s0, p0)
        acc = lax.fori_loop(0, n // 2, pair, (zero, zero))
        dk, dv = lax.cond(n % 2 == 1, odd_tail, even_tail, acc)
        dk_ref[...] = (dk * scale).astype(BF16)
        dv_ref[...] = dv.astype(BF16)

        @pl.when(j == nq - 1)
        def _():
            for cc in range(nq):
                dq_ref[cc * t:(cc + 1) * t, :] = (dqt[:, cc * t:(cc + 1) * t].T * scale).astype(BF16)

    full = pl.BlockSpec((s, HEAD_PAD), lambda h, j: (0, h))
    tile = pl.BlockSpec((t, HEAD_PAD), lambda h, j: (j, h))
    stat = pl.BlockSpec((None, 1, s), lambda h, j: (h, 0, 0))
    return pl.pallas_call(
        body, name="attn_bwd", grid=(MLA_HEADS, nq),
        in_specs=[full, tile, tile, pl.BlockSpec((HEAD_PAD, t), lambda h, j: (h, j)), full, stat, stat],
        out_specs=[full, tile, tile],
        out_shape=[jax.ShapeDtypeStruct((s, MLA_HEADS * HEAD_PAD), BF16)] * 3,
        scratch_shapes=[pltpu.VMEM((HEAD_PAD, s), F32)] + [pltpu.VMEM((t, t), F32)] * 4,
        compiler_params=pltpu.CompilerParams(dimension_semantics=("arbitrary", "arbitrary"),
                                             vmem_limit_bytes=VMEM_LIMIT),
    )(q, k, v, kt, do, lse_t, dl_t)


def _post_forward(o, s_out, gate, h0, memk, memv, w_oa, w_o, w_xq, w_xo, g1, b1, g2, b2):
    a_out = _mm(o, w_oa)
    sg_s = _sigmoid(gate[:, :1024])
    sg_a = _sigmoid(gate[:, 1024:])
    mixin = sg_s * s_out + sg_a * a_out
    r1 = DN_ALPHA * h0 + _mm(mixin, w_o)
    h1, xh1, rs1 = _ln_fwd(r1, g1, b1)
    qxb = _mm(h1, w_xq).astype(BF16)
    ps, oxs = [], []
    for hh in range(XATTN_HEADS):
        sl = slice(hh * XATTN_HD, (hh + 1) * XATTN_HD)
        sc = _mm_nt(qxb[:, sl], memk[:, sl]) * (XATTN_HD ** -0.5)
        e = jnp.exp(sc - jnp.max(sc, axis=1, keepdims=True))
        p = e / jnp.sum(e, axis=1, keepdims=True)
        ps.append(p)
        oxs.append(_mm(p, memv[:, sl]))
    ox = jnp.concatenate(oxs, axis=1)
    r2 = DN_ALPHA * h1 + _mm(ox, w_xo)
    return dict(a_out=a_out, sg_s=sg_s, sg_a=sg_a, mixin=mixin, h1=h1, xh1=xh1, rs1=rs1, qxb=qxb, ps=ps, ox=ox, r2=r2)


def _post_fwd(o, s_out, gate, h0, memk, memv, lns, w_oa, w_o, w_xq, w_xo):
    def body(o_ref, so_ref, gate_ref, h0_ref, mk_ref, mv_ref, g1, b1, g2, b2, woa, wo, wxq, wxo,
             r2_ref, mixin_ref, h1_ref, ox_ref):
        f = _post_forward(o_ref[...], so_ref[...], gate_ref[...], h0_ref[...], mk_ref[...], mv_ref[...],
                          woa[...], wo[...], wxq[...], wxo[...], g1[...], b1[...], g2[...], b2[...])
        r2_ref[...] = f["r2"]
        mixin_ref[...] = f["mixin"].astype(BF16)
        h1_ref[...] = f["h1"].astype(BF16)
        ox_ref[...] = f["ox"].astype(BF16)

    s = o.shape[0]
    return _row_call("post_fwd", body, s, min(ROW_TILE, s), [o, s_out, gate, h0], [memk, memv, *lns],
                     [w_oa, w_o, w_xq, w_xo], [(1024, F32), (1024, BF16), (1024, BF16), (1024, BF16)], [])


def _mlp(r2, target, g2, b2, g3, b3, w_up, w_down):
    cw = 1024
    n_chunk = MLP_HIDDEN // cw

    def body(r2_ref, tgt_ref, g2_ref, b2_ref, g3_ref, b3_ref, wup, wdn,
             dh2_ref, h2b_ref, dup_ref, act_ref, dff_ref, loss_ref, dg3_ref, db3_ref, up_scr):
        @pl.when(pl.program_id(0) == 0)
        def _():
            loss_ref[...] = jnp.zeros_like(loss_ref)
            dg3_ref[...] = jnp.zeros_like(dg3_ref)
            db3_ref[...] = jnp.zeros_like(db3_ref)

        h2, _, _ = _ln_fwd(r2_ref[...], g2_ref[...], b2_ref[...])
        h2b = h2.astype(BF16)
        h2b_ref[...] = h2b
        ff = jnp.zeros(h2.shape, F32)
        for c in range(n_chunk):
            sl = slice(c * cw, (c + 1) * cw)
            a = jnp.maximum(_mm(h2b, wup[:, sl]), 0.0)
            up_scr[:, sl] = a
            actb = (a * a).astype(BF16)
            act_ref[:, sl] = actb
            ff = ff + _mm(actb, wdn[sl, :])
        h3, xh3, rs3 = _ln_fwd(DN_ALPHA * h2 + ff, g3_ref[...], b3_ref[...])
        err = h3 - tgt_ref[...]
        loss_ref[...] += 0.5 * jnp.sum(err * err) * (1.0 / D_MODEL)
        dr3, dg3, db3 = _ln_bwd(err * (1.0 / D_MODEL), xh3, rs3, g3_ref[...])
        dg3_ref[...] += dg3
        db3_ref[...] += db3
        dffb = dr3.astype(BF16)
        dff_ref[...] = dffb
        dh2 = DN_ALPHA * dr3
        for c in range(n_chunk):
            sl = slice(c * cw, (c + 1) * cw)
            dupb = (_mm_nt(dffb, wdn[sl, :]) * (2.0 * up_scr[:, sl])).astype(BF16)
            dup_ref[:, sl] = dupb
            dh2 = dh2 + _mm_nt(dupb, wup[:, sl])
        dh2_ref[...] = dh2

    s = r2.shape[0]
    ts = min(ROW_TILE, s)
    return _row_call("mlp", body, s, ts, [r2, target], [g2, b2, g3, b3], [w_up, w_down],
                     [(1024, F32), (1024, BF16), (MLP_HIDDEN, BF16), (MLP_HIDDEN, BF16), (1024, BF16)],
                     [(8, 128), (1, 1024), (1, 1024)], scratch=[pltpu.VMEM((ts, MLP_HIDDEN), F32)])


def _post_bwd(dh2, o, s_out, gate, h0, memk, memv, lns, w_oa, w_o, w_xq, w_xo):
    def body(dh2_ref, o_ref, so_ref, gate_ref, h0_ref, mk_ref, mv_ref, g1, b1, g2, b2, woa, wo, wxq, wxo,
             do_ref, dl_ref, dso_ref, dgate_ref, dh0_ref, daout_ref, dmix_ref, dqx_ref, dxa_ref,
             dmk_ref, dmv_ref, dg1_ref, db1_ref, dg2_ref, db2_ref):
        @pl.when(pl.program_id(0) == 0)
        def _():
            for r in (dmk_ref, dmv_ref, dg1_ref, db1_ref, dg2_ref, db2_ref):
                r[...] = jnp.zeros_like(r)

        o = o_ref[...]
        s_out = so_ref[...]
        memk, memv = mk_ref[...], mv_ref[...]
        f = _post_forward(o, s_out, gate_ref[...], h0_ref[...], memk, memv,
                          woa[...], wo[...], wxq[...], wxo[...], g1[...], b1[...], g2[...], b2[...])
        _, xh2, rs2 = _ln_fwd(f["r2"], g2[...], b2[...])
        dr2, dg2, db2 = _ln_bwd(dh2_ref[...], xh2, rs2, g2[...])
        dg2_ref[...] += dg2
        db2_ref[...] += db2
        dxab = dr2.astype(BF16)
        dxa_ref[...] = dxab
        dox = _mm_nt(dxab, wxo[...])
        dqs = []
        for hh in range(XATTN_HEADS):
            sl = slice(hh * XATTN_HD, (hh + 1) * XATTN_HD)
            p = f["ps"][hh]
            doxh = dox[:, sl].astype(BF16)
            dp = _mm_nt(doxh, memv[:, sl])
            ds = (p * (dp - jnp.sum(dp * p, axis=1, keepdims=True)) * (XATTN_HD ** -0.5)).astype(BF16)
            dqs.append(_mm(ds, memk[:, sl]))
            dmk_ref[:, sl] += _mm_tn(ds, f["qxb"][:, sl])
            dmv_ref[:, sl] += _mm_tn(p, doxh)
        dqxb = jnp.concatenate(dqs, axis=1).astype(BF16)
        dqx_ref[...] = dqxb
        dh1 = DN_ALPHA * dr2 + _mm_nt(dqxb, wxq[...])
        dr1, dg1, db1 = _ln_bwd(dh1, f["xh1"], f["rs1"], g1[...])
        dg1_ref[...] += dg1
        db1_ref[...] += db1
        dh0_ref[...] = DN_ALPHA * dr1
        dmixb = dr1.astype(BF16)
        dmix_ref[...] = dmixb
        dmixin = _mm_nt(dmixb, wo[...])
        sg_s, sg_a = f["sg_s"], f["sg_a"]
        dso_ref[...] = dmixin * sg_s
        daoutb = (dmixin * sg_a).astype(BF16)
        daout_ref[...] = daoutb
        dgate_ref[:, :1024] = (dmixin * s_out * sg_s * (1.0 - sg_s)).astype(BF16)
        dgate_ref[:, 1024:] = (dmixin * f["a_out"] * sg_a * (1.0 - sg_a)).astype(BF16)
        d_o = _mm_nt(daoutb, woa[...])
        do_ref[...] = d_o.astype(BF16)
        lane = lax.broadcasted_iota(jnp.int32, (o.shape[0], HEAD_PAD), 1)
        dlc = jnp.zeros((o.shape[0], HEAD_PAD), F32)
        for hh in range(MLA_HEADS):
            sl = slice(hh * HEAD_PAD, (hh + 1) * HEAD_PAD)
            dl = jnp.sum(d_o[:, sl] * o[:, sl], axis=1, keepdims=True)
            dlc = dlc + jnp.where(lane == hh, dl, 0.0)
        dl_ref[...] = dlc

    s = o.shape[0]
    m = memk.shape[0]
    return _row_call(
        "post_bwd", body, s, min(ROW_TILE, s), [dh2, o, s_out, gate, h0], [memk, memv, *lns],
        [w_oa, w_o, w_xq, w_xo],
        [(1024, BF16), (HEAD_PAD, F32), (1024, F32), (2048, BF16), (1024, F32),
         (1024, BF16), (1024, BF16), (1024, BF16), (1024, BF16)],
        [(m, 1024), (m, 1024), (1, 1024), (1, 1024), (1, 1024), (1, 1024)])


def _s5out_bwd(d_so, yl, z3, d_skip, w_c, w_glu):
    def body(dso_ref, yl_ref, z3_ref, d_ref, wc, wglu, gh_ref, dup_ref, yg_ref, dy12_ref, dyl_ref, dd_ref):
        @pl.when(pl.program_id(0) == 0)
        def _():
            dd_ref[...] = jnp.zeros_like(dd_ref)

        yl = yl_ref[...]
        ygb = _gelu(yl).astype(BF16)
        yg_ref[...] = ygb
        y12 = _mm(ygb, wglu[...])
        sg = _sigmoid(y12[:, 1024:])
        dso = dso_ref[...]
        dy12b = jnp.concatenate([dso * sg, dso * y12[:, :1024] * sg * (1.0 - sg)], axis=1).astype(BF16)
        dy12_ref[...] = dy12b
        dyl = _mm_nt(dy12b, wglu[...]) * _gelu_grad(yl)
        dylb = dyl.astype(BF16)
        dyl_ref[...] = dylb
        gh_ref[...] = _mm_nt(dylb, wc[...])
        dup_ref[...] = dyl * d_ref[...]
        dd_ref[...] += jnp.sum(dyl * z3_ref[:, :256], axis=0, keepdims=True)

    s = d_so.shape[0]
    return _row_call("s5out_bwd", body, s, min(ROW_TILE, s), [d_so, yl, z3], [d_skip], [w_c, w_glu],
                     [(2 * NS, F32), (256, F32), (256, BF16), (2048, BF16), (256, BF16)], [(1, 256)])


def _in_bwd(x, tab, z3, dq, dk, dv, lam, du_p, dgate, dh0p, ln_g, ln_b, qg, kvg, w_inx, w_b, w_q2, w_kv2):
    def body(x_ref, tab_ref, z3_ref, dq_ref, dk_ref, dv_ref, lam_ref, dup_ref, dgate_ref, dh0p_ref,
             lng_ref, lnb_ref, qg_ref, kvg_ref, winx, wb, wq2, wkv2,
             dx_ref, h0b_ref, dz_ref, ub_ref, cqn_ref, ckvn_ref, dq2_ref, dkv2_ref,
             dlng_ref, dlnb_ref, dqg_ref, dkvg_ref):
        @pl.when(pl.program_id(0) == 0)
        def _():
            for r in (dlng_ref, dlnb_ref, dqg_ref, dkvg_ref):
                r[...] = jnp.zeros_like(r)

        h0, xh0, rs0 = _ln_fwd(x_ref[...], lng_ref[...], lnb_ref[...])
        h0b_ref[...] = h0.astype(BF16)
        z3 = z3_ref[...]
        ub_ref[...] = z3[:, :256].astype(BF16)
        tab_v = tab_ref[...]
        cq1, cq2, ck1 = tab_v[:, :128], tab_v[:, 128:], tab_v[:, :128]
        du = _mm_nt(lam_ref[...], wb[...]) + dup_ref[...]
        dq = dq_ref[...].astype(F32)
        dq2b =jnp.concatenate([dq * _tile_heads(cq1), dq * _tile_heads(cq2)], axis=1).astype(BF16)
        dq2_ref[...] = dq2b
        cqg, cqn, rq = _rms_fwd(z3[:, 256:512], qg_ref[...])
        cqn_ref[...] = cqg.astype(BF16)
        dcq, dqg = _rms_bwd(_mm_nt(dq2b, wq2[...]), cqn, rq, qg_ref[...])
        dqg_ref[...] += dqg
        dkv2b = jnp.concatenate([dk_ref[...], dv_ref[...]], axis=1)
        dk = dk_ref[...].astype(F32)
        dkv2_ref[...] = dkv2b
        ckvg, ckvn, rkv = _rms_fwd(z3[:, 512:768], kvg_ref[...])
        ckvn_ref[...] = ckvg.astype(BF16)
        dckv, dkvg = _rms_bwd(_mm_nt(dkv2b, wkv2[...]), ckvn, rkv, kvg_ref[...])
        dkvg_ref[...] += dkvg
        dkrp = dk[:, :HEAD_PAD]
        for hh in range(1, MLA_HEADS):
            dkrp = dkrp + dk[:, hh * HEAD_PAD:(hh + 1) * HEAD_PAD]
        dzb = jnp.concatenate([a.astype(BF16) for a in (du, dcq, dckv, dkrp * ck1, dkrp * cq2)] + [dgate_ref[...]],
                              axis=1)
        dz_ref[...] = dzb
        dh0 = _mm_nt(dzb, winx[...]) + dh0p_ref[...]
        dx, dg, db = _ln_bwd(dh0, xh0, rs0, lng_ref[...])
        dx_ref[...] = dx
        dlng_ref[...] += dg
        dlnb_ref[...] += db

    s = x.shape[0]
    return _row_call(
        "in_bwd", body, s, min(ROW_TILE, s), [x, tab, z3, dq, dk, dv, lam, du_p, dgate, dh0p],
        [ln_g, ln_b, qg, kvg], [w_inx, w_b, w_q2, w_kv2],
        [(1024, F32), (1024, BF16), (3072, BF16), (256, BF16), (256, BF16), (256, BF16), (2048, BF16), (2048, BF16)],
        [(1, 1024), (1, 1024), (1, 256), (1, 256)])


def _wgrad(name, xs, dy, out_dtype=F32, blocks=1):
    s, k = xs.shape
    n = dy.shape[1]
    nb = n // blocks
    tk, tn = min(k, 1024), min(n, 1024)
    ts = min(s, 2048 if tk <= 256 else 1024)
    per_tile = tn // nb
    assert s % ts == 0 and k % tk == 0 and n % tn == 0 and (blocks == 1 or tn % nb == 0), (name, s, k, n)
    last = s // ts - 1

    def body(x_ref, dy_ref, o_ref, acc):
        @pl.when(pl.program_id(2) == 0)
        def _():
            acc[...] = jnp.zeros_like(acc)

        acc[...] += _mm_tn(x_ref[...], dy_ref[...])

        @pl.when(pl.program_id(2) == last)
        def _():
            if blocks == 1:
                o_ref[...] = acc[...].astype(out_dtype)
            else:
                for d in range(per_tile):
                    o_ref[d] = acc[:, d * nb:(d + 1) * nb].astype(out_dtype)

    if blocks == 1:
        out_shape = jax.ShapeDtypeStruct((k, n), out_dtype)
        out_spec = pl.BlockSpec((tk, tn), lambda a, b, c: (a, b))
    else:
        out_shape = jax.ShapeDtypeStruct((blocks, k, nb), out_dtype)
        out_spec = pl.BlockSpec((per_tile, tk, nb), lambda a, b, c: (b, a, 0))
    return pl.pallas_call(
        body, name=name, grid=(k // tk, n // tn, s // ts),
        in_specs=[pl.BlockSpec((ts, tk), lambda a, b, c: (c, a)), pl.BlockSpec((ts, tn), lambda a, b, c: (c, b))],
        out_specs=out_spec, out_shape=out_shape, scratch_shapes=[pltpu.VMEM((tk, tn), F32)],
        compiler_params=pltpu.CompilerParams(dimension_semantics=("arbitrary", "arbitrary", "arbitrary"),
                                             vmem_limit_bytes=VMEM_LIMIT),
    )(xs, dy)


def _mesh_pos():
    x, y, c = lax.axis_index("x"), lax.axis_index("y"), lax.axis_index("c")
    return x, y, c


def _peer(x, y, c, k):
    px = 1 - x if k & 4 else x
    py = 1 - y if k & 2 else y
    pc = 1 - c if k & 1 else c
    return (px, py, pc), 4 * px + 2 * py + pc


def _all_to_all(name, arrays, gather):
    n = len(arrays)

    def body(*refs):
        src, dst = refs[:n], refs[n:2 * n]
        send_sems, recv_sems, local_sems = refs[2 * n:]
        x, y, c = _mesh_pos()
        me = 4 * x + 2 * y + c

        def block(i, d):
            return src[i] if gather else src[i].at[d]

        own = [pltpu.make_async_copy(block(i, me), dst[i].at[me], local_sems.at[i]) for i in range(n)]
        for cp in own:
            cp.start()
        sends = []
        for k in range(1, N_DEV):
            peer, pid = _peer(x, y, c, k)
            for i in range(n):
                idx = (k - 1) * n + i
                cp = pltpu.make_async_remote_copy(src_ref=block(i, pid), dst_ref=dst[i].at[me],
                                                  send_sem=send_sems.at[idx], recv_sem=recv_sems.at[idx],
                                                  device_id=peer, device_id_type=MESH)
                cp.start()
                sends.append(cp)
        for k in range(1, N_DEV):
            peer, pid = _peer(x, y, c, k)
            for i in range(n):
                idx = (k - 1) * n + i
                pltpu.make_async_remote_copy(src_ref=block(i, pid), dst_ref=dst[i].at[pid],
                                             send_sem=send_sems.at[idx], recv_sem=recv_sems.at[idx],
                                             device_id=peer, device_id_type=MESH).wait_recv()
        for cp in sends:
            cp.wait_send()
        for cp in own:
            cp.wait()

    n_sem = n * (N_DEV - 1)
    return pl.pallas_call(
        body, name=name,
        in_specs=[pl.BlockSpec(memory_space=pl.ANY)] * n,
        out_specs=[pl.BlockSpec(memory_space=pl.ANY)] * n,
        out_shape=[jax.ShapeDtypeStruct((N_DEV,) + a.shape[-2:], a.dtype) for a in arrays],
        scratch_shapes=[pltpu.SemaphoreType.DMA((n_sem,)), pltpu.SemaphoreType.DMA((n_sem,)),
                        pltpu.SemaphoreType.DMA((n,))],
    )(*arrays)


_HBM = pl.BlockSpec(memory_space=pltpu.HBM)
_SEM = pl.BlockSpec(memory_space=pltpu.SEMAPHORE)


def _exchange_start(name, arrays, gather, after=None):
    n = len(arrays)
    n_sem = n * (N_DEV - 1)
    me = 4 * lax.axis_index("x") + 2 * lax.axis_index("y") + lax.axis_index("c")
    lands = []
    for a in arrays:
        own = a[None] if gather else lax.dynamic_slice_in_dim(a, me, 1, 0)
        lands.append(lax.dynamic_update_slice(lax.empty((N_DEV,) + a.shape[-2:], a.dtype), own, (me, 0, 0)))
    n_after = 0 if after is None else 1

    def body(*refs):
        src, land = refs[:n], refs[n:2 * n]
        send_sems, recv_sems = refs[2 * n + n_after], refs[2 * n + n_after + 1]
        token = refs[-1]
        x, y, c = _mesh_pos()
        me_in = 4 * x + 2 * y + c
        for k in range(1, N_DEV):
            peer, pid = _peer(x, y, c, k)
            for i in range(n):
                idx = (k - 1) * n + i
                pltpu.make_async_remote_copy(src_ref=src[i] if gather else src[i].at[pid], dst_ref=land[i].at[me_in],
                                             send_sem=send_sems.at[idx], recv_sem=recv_sems.at[idx],
                                             device_id=peer, device_id_type=MESH).start()
        token[...] = jnp.zeros_like(token)

    operands = [pltpu.with_memory_space_constraint(a, pltpu.HBM) for a in list(arrays) + lands]
    outs = pl.pallas_call(
        body, name=name,
        out_shape=(pltpu.SemaphoreType.DMA((n_sem,)), pltpu.SemaphoreType.DMA((n_sem,)),
                   *[pltpu.HBM(a.shape, a.dtype) for a in list(arrays) + lands],
                   jax.ShapeDtypeStruct((8, 128), F32)),
        in_specs=[_HBM] * (2 * n) + [pl.BlockSpec(memory_space=pl.ANY)] * n_after,
        out_specs=(_SEM, _SEM, *[_HBM] * (2 * n), pl.BlockSpec(memory_space=pltpu.VMEM)),
        input_output_aliases={i: 2 + i for i in range(2 * n)},
        compiler_params=pltpu.CompilerParams(has_side_effects=pltpu.SideEffectType.DATAFLOW_SIDE_EFFECTING),
    )(*operands, *([after] if n_after else []))
    return (gather, outs[0], outs[1], outs[2:2 + n], outs[2 + n:2 + 2 * n]), outs[-1]


def _exchange_wait(name, handle, after):
    gather, send_sems, recv_sems, srcs, lands = handle
    n = len(srcs)

    def body(*refs):
        src, land = refs[:n], refs[n:2 * n]
        s_sems, r_sems = refs[2 * n], refs[2 * n + 1]
        x, y, c = _mesh_pos()
        for k in range(1, N_DEV):
            peer, pid = _peer(x, y, c, k)
            for i in range(n):
                idx = (k - 1) * n + i
                cp = pltpu.make_async_remote_copy(src_ref=src[i] if gather else src[i].at[pid], dst_ref=land[i].at[pid],
                                                  send_sem=s_sems.at[idx], recv_sem=r_sems.at[idx],
                                                  device_id=peer, device_id_type=MESH)
                cp.wait_send()
                cp.wait_recv()

    outs = pl.pallas_call(
        body, name=name,
        out_shape=tuple(pltpu.HBM(a.shape, a.dtype) for a in list(srcs) + list(lands)),
        in_specs=[_HBM] * (2 * n) + [_SEM, _SEM, pl.BlockSpec(memory_space=pl.ANY)],
        out_specs=tuple([_HBM] * (2 * n)),
        input_output_aliases={i: i for i in range(2 * n)},
        compiler_params=pltpu.CompilerParams(has_side_effects=pltpu.SideEffectType.DATAFLOW_SIDE_EFFECTING),
    )(*srcs, *lands, send_sems, recv_sems, after)
    return list(outs[n:])


def _adamw(name, parts, w, m, v):
    a_rows, b_cols = w.shape
    ta = min(a_rows, ADAM_TILE)
    assert a_rows % ta == 0
    c1 = 1.0 - ADAM_B1 ** ADAM_STEP
    c2 = 1.0 - ADAM_B2 ** ADAM_STEP

    def body(p_ref, w_ref, m_ref, v_ref, g_ref, d_ref, mo_ref, vo_ref):
        g = p_ref[0].astype(F32)
        for d in range(1, N_DEV):
            g = g + p_ref[d].astype(F32)
        g_ref[...] = g
        mn = ADAM_B1 * m_ref[...] + (1.0 - ADAM_B1) * g
        vn = ADAM_B2 * v_ref[...] + (1.0 - ADAM_B2) * (g * g)
        mo_ref[...] = mn
        vo_ref[...] = vn
        d_ref[...] = -ADAM_LR * ((mn / c1) / (jnp.sqrt(vn / c2) + ADAM_EPS) + ADAM_WD * w_ref[...])

    row = pl.BlockSpec((ta, b_cols), lambda i: (i, 0))
    return pl.pallas_call(
        body, name=name, grid=(a_rows // ta,),
        in_specs=[pl.BlockSpec((N_DEV, ta, b_cols), lambda i: (0, i, 0)), row, row, row],
        out_specs=[row] * 4,
        out_shape=[jax.ShapeDtypeStruct((a_rows, b_cols), F32)] * 4,
        compiler_params=pltpu.CompilerParams(dimension_semantics=("arbitrary",), vmem_limit_bytes=VMEM_LIMIT),
    )(parts, w, m, v)


def _pack_rows(arrays, rows):
    flat = jnp.concatenate([a.reshape(-1) for a in arrays])
    return jnp.pad(flat, (0, rows * LANES - flat.shape[0])).reshape(rows, LANES)


def _cols_from_blocks(g):
    return g.transpose(1, 0, 2).reshape(g.shape[1], N_DEV * g.shape[2])


def _blocks_from_cols(w):
    return w.reshape(w.shape[0], N_DEV, w.shape[1] // N_DEV).transpose(1, 0, 2)


def _unpack_flat(flat2d, shapes):
    flat = flat2d.reshape(-1)
    out, off = [], 0
    for shp in shapes:
        sz = math.prod(shp)
        out.append(flat[off:off + sz].reshape(shp))
        off += sz
    return out


def _s5_mats(lam_re, lam_im, log_dt, b_re, b_im, c_re, c_im):
    lr = jnp.minimum(lam_re, S5_MAX_RE)
    li = lam_im
    dt = jnp.exp(log_dt)[:, None]
    mag = jnp.exp(lr * dt)
    ang = li * dt
    ab_re = mag * jnp.cos(ang)
    ab_im = mag * jnp.sin(ang)
    den = lr * lr + li * li
    nr = ab_re - 1.0
    f_re = ((nr * lr + ab_im * li) / den)[..., None]
    f_im = ((ab_im * lr - nr * li) / den)[..., None]
    bb_re = f_re * b_re - f_im * b_im
    bb_im = f_re * b_im + f_im * b_re
    eye = jnp.eye(S5_GROUPS, dtype=F32)
    a = jnp.stack([ab_re.reshape(-1), ab_im.reshape(-1)])
    wb = jnp.concatenate([jnp.einsum("gph,gk->ghkp", bb_re, eye).reshape(S5_WIDTH, NS),
                          jnp.einsum("gph,gk->ghkp", bb_im, eye).reshape(S5_WIDTH, NS)], axis=1)
    wc = jnp.concatenate([jnp.einsum("ghp,gk->gpkh", c_re, eye).reshape(NS, S5_WIDTH),
                          -jnp.einsum("ghp,gk->gpkh", c_im, eye).reshape(NS, S5_WIDTH)], axis=0)
    return a, wb, wc


def _power_table(a):
    ar, ai = a[0], a[1]
    rows_r, rows_i = [ar], [ai]
    for _ in range(7):
        pr, pi = rows_r[-1], rows_i[-1]
        rows_r.append(pr * ar - pi * ai)
        rows_i.append(pr * ai + pi * ar)
    return jnp.concatenate([jnp.stack(rows_r), jnp.stack(rows_i)], axis=1)


def _rope_table(positions):
    inv = ROPE_THETA ** (-jnp.arange(0, MLA_ROPE, 2, dtype=F32) / MLA_ROPE)
    inv128 = jnp.concatenate([jnp.zeros((MLA_NOPE,), F32), inv, inv, jnp.zeros((32,), F32)])
    sign = jnp.concatenate([jnp.zeros((MLA_NOPE,), F32), -jnp.ones((16,), F32), jnp.ones((16,), F32),
                            jnp.zeros((32,), F32)])
    ang = positions.astype(F32)[:, None] * inv128
    return jnp.concatenate([jnp.cos(ang), jnp.sin(ang) * sign], axis=1)


def _derived_weights(full):
    w_in = full["w_in"]
    k1, k2 = w_in[:, 768:784], w_in[:, 784:800]
    z64, z32 = jnp.zeros((1024, 64), BF16), jnp.zeros((1024, 32), BF16)
    w_inx = jnp.concatenate([w_in[:, :768], z64, k1, k2, z32, z64, k2, k1, z32, w_in[:, 800:]], axis=1)
    uq = full["w_uq"].reshape(256, MLA_HEADS, MLA_QK)
    nope, r1, r2 = uq[:, :, :64], uq[:, :, 64:80], uq[:, :, 80:]
    zq64, zq32 = jnp.zeros((256, MLA_HEADS, 64), BF16), jnp.zeros((256, MLA_HEADS, 32), BF16)
    w_q2 = jnp.concatenate([jnp.concatenate([nope, r1, r2, zq32], axis=2).reshape(256, 1024),
                            jnp.concatenate([zq64, r2, r1, zq32], axis=2).reshape(256, 1024)], axis=1)
    ukv = full["w_ukv"].reshape(256, MLA_HEADS, 128)
    w_kv2 = jnp.concatenate([jnp.concatenate([ukv[:, :, :64], zq64], axis=2).reshape(256, 1024),
                             jnp.concatenate([ukv[:, :, 64:], zq64], axis=2).reshape(256, 1024)], axis=1)
    return w_inx, w_q2, w_kv2


def _oa_padded(w_oa):
    oa = w_oa.reshape(MLA_HEADS, MLA_V, 1024)
    return jnp.concatenate([oa, jnp.zeros_like(oa)], axis=1).reshape(1024, 1024)


def _fold_grads(d_inx, d_q2, d_kv2):
    d_k1 = d_inx[:, 832:848] + d_inx[:, 976:992]
    d_k2 = d_inx[:, 848:864] + d_inx[:, 960:976]
    d_w_in = jnp.concatenate([d_inx[:, :768], d_k1, d_k2, d_inx[:, 1024:]], axis=1)
    a = d_q2[:, :1024].reshape(256, MLA_HEADS, 128)
    b = d_q2[:, 1024:].reshape(256, MLA_HEADS, 128)
    d_w_uq = jnp.concatenate([a[:, :, :64], a[:, :, 64:80] + b[:, :, 80:96], a[:, :, 80:96] + b[:, :, 64:80]],
                             axis=2).reshape(256, MLA_HEADS * MLA_QK)
    kk = d_kv2[:, :1024].reshape(256, MLA_HEADS, 128)
    vv = d_kv2[:, 1024:].reshape(256, MLA_HEADS, 128)
    d_w_ukv = jnp.concatenate([kk[:, :, :64], vv[:, :, :64]], axis=2).reshape(256, 1024)
    return d_w_in, d_w_uq, d_w_ukv


def kernel(x, mem, positions, ln_in_g, ln_in_b, w_in, s5_lam_re, s5_lam_im, s5_log_dt, s5_b_re, s5_b_im, s5_c_re, s5_c_im, s5_d, w_glu, q_norm_g, w_uq, kv_norm_g, w_ukv, w_oa, w_o, ln1_g, ln1_b, w_xq, w_xk, w_xv, w_xo, ln2_g, ln2_b, w_up, w_down, ln3_g, ln3_b, loss_target, m_ln_in_g, m_ln_in_b, m_w_in, m_s5_lam_re, m_s5_lam_im, m_s5_log_dt, m_s5_b_re, m_s5_b_im, m_s5_c_re, m_s5_c_im, m_s5_d, m_w_glu, m_q_norm_g, m_w_uq, m_kv_norm_g, m_w_ukv, m_w_oa, m_w_o, m_ln1_g, m_ln1_b, m_w_xq, m_w_xk, m_w_xv, m_w_xo, m_ln2_g, m_ln2_b, m_w_up, m_w_down, m_ln3_g, m_ln3_b, v_ln_in_g, v_ln_in_b, v_w_in, v_s5_lam_re, v_s5_lam_im, v_s5_log_dt, v_s5_b_re, v_s5_b_im, v_s5_c_re, v_s5_c_im, v_s5_d, v_w_glu, v_q_norm_g, v_w_uq, v_kv_norm_g, v_w_ukv, v_w_oa, v_w_o, v_ln1_g, v_ln1_b, v_w_xq, v_w_xk, v_w_xv, v_w_xo, v_ln2_g, v_ln2_b, v_w_up, v_w_down, v_ln3_g, v_ln3_b):
    args = dict(locals())
    wts = {n: args[n] for n in WEIGHTS}
    mom = {n: args["m_" + n] for n in WEIGHTS}
    vel = {n: args["v_" + n] for n in WEIGHTS}
    xs, mems, tgt = x[0], mem[0], loss_target[0]
    s = xs.shape[0]
    names_big = [n for (n, _, _, _) in SHARDED]
    kind = {n: kd for (n, kd, _, _) in SHARDED}
    n_small = sum(math.prod(wts[n].shape) for n in SMALL)
    small_rows = _round_up(n_small + 1, 8 * LANES) // LANES

    late = [n for n in names_big if n not in EARLY]
    shard = {n: wts[n][0].astype(BF16) for n in names_big}
    g_early = dict(zip(EARLY, _all_to_all("all_gather", [shard[n] for n in EARLY], True)))
    ag_handle, ag_token = _exchange_start("all_gather_late", [shard[n] for n in late], True, after=g_early["w_in"])
    full = {n: _cols_from_blocks(g_early[n]) for n in EARLY}
    w_inx, w_q2, w_kv2 = _derived_weights(full)
    s5_args = (s5_lam_re[0], s5_lam_im[0], s5_log_dt[0], s5_b_re[0], s5_b_im[0], s5_c_re[0], s5_c_im[0])
    (a_mat, w_b, w_c), s5_vjp = jax.vjp(_s5_mats, *s5_args)
    w_bb, w_cb = w_b.astype(BF16), w_c.astype(BF16)
    pw = _power_table(a_mat)
    pwb = _power_table(a_mat * jnp.array([[1.0], [-1.0]], F32))[::-1]
    tab = _rope_table(positions[0])
    row = lambda a: a.reshape(1, -1)
    ln_g, ln_b = row(ln_in_g) + ag_token[0:1, 0:1], row(ln_in_b)
    lns = [ln1_g, ln1_b, ln2_g, ln2_b]

    h0, z3, gate, bu, q, k, v, kt, vt = _in_fwd(xs, tab, ln_g, ln_b, q_norm_g, kv_norm_g, w_inx, w_bb, w_q2, w_kv2)
    h = _scan_fwd(pw, bu)
    yl, s_out = _s5out_fwd(h, z3, s5_d, w_cb, full["w_glu"])
    o, lse_t = _attn_fwd(q, k, vt)
    g_late = dict(zip(late, _exchange_wait("all_gather_late_wait", ag_handle, after=lse_t)))
    full.update({n: g_late[n].reshape(-1, g_late[n].shape[2]) if kind[n] == "row" else _cols_from_blocks(g_late[n])
                 for n in late})
    w_oap = _oa_padded(full["w_oa"])
    memk, memv = _mem_kv(mems, full["w_xk"], full["w_xv"])
    r2, mixin_b, h1_b, ox_b = _post_fwd(o, s_out, gate, h0, memk, memv, lns,
                                        w_oap, full["w_o"], full["w_xq"], full["w_xo"])
    dh2, h2_b, dup_b, act_b, dff_b, loss_acc, d_ln3_g, d_ln3_b = _mlp(
        r2, tgt, ln2_g, ln2_b, ln3_g, ln3_b, full["w_up"], full["w_down"])
    (do_b, delta, d_so, dgate, dh0p, daout_b, dmix_b, dqx_b, dxa_b,
     dmk, dmv, d_ln1_g, d_ln1_b, d_ln2_g, d_ln2_b) = _post_bwd(
        dh2, o, s_out, gate, h0, memk, memv, lns, w_oap, full["w_o"], full["w_xq"], full["w_xo"])
    rows8 = lambda g: g.reshape(N_DEV, g.shape[0] // N_DEV, g.shape[1])
    d_w_oa = _wgrad("wg_oa", o, daout_b).reshape(MLA_HEADS, 128, 1024)[:, :64].reshape(512, 1024)
    send = {
        "w_oa": _blocks_from_cols(d_w_oa).astype(BF16),
        "w_up": _wgrad("wg_up", h2_b, dup_b, BF16, N_DEV),
        "w_o": rows8(_wgrad("wg_o", mixin_b, dmix_b, BF16)),
        "w_xq": rows8(_wgrad("wg_xq", h1_b, dqx_b, BF16)),
        "w_xk": rows8(_wgrad("wg_xk", mems, dmk, BF16)),
        "w_xv": rows8(_wgrad("wg_xv", mems, dmv, BF16)),
        "w_xo": rows8(_wgrad("wg_xo", ox_b, dxa_b, BF16)),
        "w_down": rows8(_wgrad("wg_down", act_b, dff_b, BF16)),
    }
    rs_handle, rs_token = _exchange_start("grad_exchange_late", [send[n] for n in late], False)
    dl_t = delta.T[:MLA_HEADS].reshape(MLA_HEADS, 1, s) + rs_token[0, 0]
    dq, dk, dv = _attn_bwd(q, k, v, kt, do_b, lse_t, dl_t)
    gh, du_p, yg_b, dy12_b, dyl_b, d_s5_d = _s5out_bwd(d_so, yl, z3, s5_d, w_cb, full["w_glu"])
    lam, d_a = _scan_bwd(pwb, gh, h)
    (dx, h0_b, dz_b, u_b, cqn_b, ckvn_b, dq2_b, dkv2_b, d_ln_g, d_ln_b, d_qg, d_kvg) = _in_bwd(
        xs, tab, z3, dq, dk, dv, lam, du_p, dgate, dh0p, ln_g, ln_b, q_norm_g, kv_norm_g, w_inx, w_bb, w_q2, w_kv2)

    d_w_in, d_w_uq, d_w_ukv = _fold_grads(
        _wgrad("wg_in", h0_b, dz_b), _wgrad("wg_q", cqn_b, dq2_b), _wgrad("wg_kv", ckvn_b, dkv2_b))
    send.update({
        "w_in": _blocks_from_cols(d_w_in).astype(BF16), "w_uq": _blocks_from_cols(d_w_uq).astype(BF16),
        "w_ukv": _blocks_from_cols(d_w_ukv).astype(BF16),
        "w_glu": _wgrad("wg_glu", yg_b, dy12_b, BF16, N_DEV),
    })
    d_s5 = s5_vjp((d_a.reshape(2, NS), _wgrad("wg_s5b", u_b, lam), _wgrad("wg_s5c", h, dyl_b)))
    small_grads = {
        "ln_in_g": d_ln_g, "ln_in_b": d_ln_b, "s5_lam_re": d_s5[0], "s5_lam_im": d_s5[1], "s5_log_dt": d_s5[2],
        "s5_b_re": d_s5[3], "s5_b_im": d_s5[4], "s5_c_re": d_s5[5], "s5_c_im": d_s5[6], "s5_d": d_s5_d,
        "q_norm_g": d_qg, "kv_norm_g": d_kvg, "ln1_g": d_ln1_g, "ln1_b": d_ln1_b, "ln2_g": d_ln2_g,
        "ln2_b": d_ln2_b, "ln3_g": d_ln3_g, "ln3_b": d_ln3_b,
    }
    small_send = jnp.broadcast_to(_pack_rows([small_grads[n] for n in SMALL] + [loss_acc[0, :1]], small_rows)[None],
                                  (N_DEV, small_rows, LANES))

    recv_early = _all_to_all("grad_exchange", [send[n] for n in EARLY] + [small_send], False)
    recv_late = _exchange_wait("grad_exchange_late_wait", rs_handle, after=recv_early[-1])
    recv = dict(zip(list(EARLY) + late, list(recv_early[:-1]) + recv_late))
    results = [dict(), dict(), dict(), dict()]
    for n in names_big:
        parts = recv[n]
        outs = _adamw("adamw_" + n, parts, wts[n][0], mom[n][0], vel[n][0])
        for res, a in zip(results, outs):
            res[n] = a[None]
    small_out = _adamw("adamw_small", recv_early[-1],
                       *[_pack_rows([t[n] for n in SMALL], small_rows) for t in (wts, mom, vel)])
    for res, fs in zip(results, small_out):
        for n, a in zip(SMALL, _unpack_flat(fs, [wts[n].shape for n in SMALL])):
            res[n] = a

    loss = small_out[0].reshape(-1)[n_small]
    return (loss, dx[None], *[res[n] for res in results for n in WEIGHTS])
```

```python
import math

import jax
import jax.numpy as jnp
from jax import lax
from jax.experimental import pallas as pl
from jax.experimental.pallas import tpu as pltpu

F32 = jnp.float32
BF16 = jnp.bfloat16

D_MODEL = 1024
S5_WIDTH = 256
S5_GROUP_CH = 16
S5_GROUPS = 16
S5_STATE = 64
NS = S5_GROUPS * S5_STATE
S5_MAX_RE = -1e-4
MLA_HEADS = 8
MLA_NOPE = 64
MLA_ROPE = 32
MLA_QK = 96
MLA_V = 64
HEAD_PAD = 128
ROPE_THETA = 10000.0
XATTN_HEADS = 4
XATTN_HD = 256
MLP_HIDDEN = 4096
LN_EPS = 1e-5
RMS_EPS = 1e-6
NEG_INF = -1e30
LOG2E = 1.4426950408889634
DN_ALPHA = 2.0 ** 0.25
ADAM_LR = 0.001
ADAM_B1 = 0.9
ADAM_B2 = 0.999
ADAM_EPS = 1e-08
ADAM_WD = 0.01
ADAM_STEP = 10

N_DEV = 8
MESH = pl.DeviceIdType.MESH
LANES = 1024
VMEM_LIMIT = 60 * 1024 * 1024

ROW_TILE = 256
SCAN_TILE = 1024
ATT_TILE = 512
ADAM_TILE = 128

SHARDED = (
    ("w_in", "col", 1024, 2848), ("w_glu", "col", 256, 2048), ("w_uq", "col", 256, 768),
    ("w_ukv", "col", 256, 1024), ("w_oa", "col", 512, 1024), ("w_o", "row", 1024, 1024),
    ("w_xq", "row", 1024, 1024), ("w_xk", "row", 1024, 1024), ("w_xv", "row", 1024, 1024),
    ("w_xo", "row", 1024, 1024), ("w_up", "col", 1024, 4096), ("w_down", "row", 4096, 1024),
)
EARLY = ("w_in", "w_glu", "w_uq", "w_ukv")
SMALL = ("ln_in_g", "ln_in_b", "s5_lam_re", "s5_lam_im", "s5_log_dt", "s5_b_re", "s5_b_im", "s5_c_re",
         "s5_c_im", "s5_d", "q_norm_g", "kv_norm_g", "ln1_g", "ln1_b", "ln2_g", "ln2_b", "ln3_g", "ln3_b")
WEIGHTS = ("ln_in_g", "ln_in_b", "w_in", "s5_lam_re", "s5_lam_im", "s5_log_dt", "s5_b_re", "s5_b_im",
           "s5_c_re", "s5_c_im", "s5_d", "w_glu", "q_norm_g", "w_uq", "kv_norm_g", "w_ukv", "w_oa", "w_o",
           "ln1_g", "ln1_b", "w_xq", "w_xk", "w_xv", "w_xo", "ln2_g", "ln2_b", "w_up", "w_down", "ln3_g", "ln3_b")


def _round_up(n, m):
    return (n + m - 1) // m * m


def _bf(a):
    return a.astype(BF16)


def _mm(a, b):
    return jnp.dot(_bf(a), _bf(b), preferred_element_type=F32)


def _mm_nt(a, b):
    return lax.dot_general(_bf(a), _bf(b), (((1,), (1,)), ((), ())), preferred_element_type=F32)


def _mm_tn(a, b):
    return lax.dot_general(_bf(a), _bf(b), (((0,), (0,)), ((), ())), preferred_element_type=F32)


def _sigmoid(a):
    return 1.0 / (1.0 + jnp.exp(-a))


def _gelu(a):
    return 0.5 * a * (1.0 + lax.erf(a * (2.0 ** -0.5)))


def _gelu_grad(a):
    return 0.5 * (1.0 + lax.erf(a * (2.0 ** -0.5))) + a * jnp.exp(-0.5 * a * a) * (1.0 / math.sqrt(2.0 * math.pi))


def _ln_fwd(a, g, b):
    mu = jnp.mean(a, axis=-1, keepdims=True)
    ac = a - mu
    var = jnp.mean(ac * ac, axis=-1, keepdims=True)
    rstd = lax.rsqrt(var + LN_EPS)
    xhat = ac * rstd
    return xhat * g + b, xhat, rstd


def _ln_bwd(dy, xhat, rstd, g):
    dxh = dy * g
    m1 = jnp.mean(dxh, axis=-1, keepdims=True)
    m2 = jnp.mean(dxh * xhat, axis=-1, keepdims=True)
    dx = rstd * (dxh - m1 - xhat * m2)
    return dx, jnp.sum(dy * xhat, axis=0, keepdims=True), jnp.sum(dy, axis=0, keepdims=True)


def _rms_fwd(a, g):
    r = lax.rsqrt(jnp.mean(a * a, axis=-1, keepdims=True) + RMS_EPS)
    xn = a * r
    return xn * g, xn, r


def _rms_bwd(dy, xn, r, g):
    dxn = dy * g
    dx = r * (dxn - xn * jnp.mean(dxn * xn, axis=-1, keepdims=True))
    return dx, jnp.sum(dy * xn, axis=0, keepdims=True)


def _tile_heads(a):
    return jnp.concatenate([a] * MLA_HEADS, axis=1)


def _row_call(name, body, n_rows, ts, tiled_in, full_in, weights, tiled_out, acc_out, reverse=False, scratch=(),
              tiled_out_t=()):
    n = n_rows // ts
    assert n * ts == n_rows, (name, n_rows, ts)
    nt, nf, nw = len(tiled_in), len(full_in), len(weights)
    nto, nao = len(tiled_out) + len(tiled_out_t), len(acc_out)
    if reverse:
        imap = lambda i: (n - 1 - i, 0)
    else:
        imap = lambda i: (i, 0)
    const = lambda i: (0, 0)

    def kern(*refs):
        ins = refs[:nt + nf]
        w_hbm = refs[nt + nf:nt + nf + nw]
        outs = refs[nt + nf + nw:nt + nf + nw + nto + nao]
        scr = refs[nt + nf + nw + nto + nao:]
        w_vmem = scr[:nw]
        extra = scr[nw + 1:] if nw else scr
        if nw:
            sem = scr[nw]

            @pl.when(pl.program_id(0) == 0)
            def _():
                cps = [pltpu.make_async_copy(w_hbm[k], w_vmem[k], sem.at[k]) for k in range(nw)]
                for cp in cps:
                    cp.start()
                for cp in cps:
                    cp.wait()
        body(*ins, *w_vmem, *outs, *extra)

    in_specs = [pl.BlockSpec((ts, a.shape[1]), imap) for a in tiled_in]
    in_specs += [pl.BlockSpec(a.shape, const) for a in full_in]
    in_specs += [pl.BlockSpec(memory_space=pl.ANY) for _ in weights]
    assert not (reverse and tiled_out_t)
    out_shape = [jax.ShapeDtypeStruct((n_rows, c), dt) for (c, dt) in tiled_out]
    out_shape += [jax.ShapeDtypeStruct((c, n_rows), dt) for (c, dt) in tiled_out_t]
    out_shape += [jax.ShapeDtypeStruct(shp, F32) for shp in acc_out]
    out_specs = [pl.BlockSpec((ts, c), imap) for (c, dt) in tiled_out]
    out_specs += [pl.BlockSpec((c, ts), lambda i: (0, i)) for (c, dt) in tiled_out_t]
    out_specs += [pl.BlockSpec(shp, const) for shp in acc_out]
    scratch_shapes = [pltpu.VMEM(w.shape, w.dtype) for w in weights]
    if nw:
        scratch_shapes.append(pltpu.SemaphoreType.DMA((nw,)))
    scratch_shapes += list(scratch)
    return pl.pallas_call(
        kern, name=name, grid=(n,), in_specs=in_specs, out_specs=out_specs, out_shape=out_shape,
        scratch_shapes=scratch_shapes,
        compiler_params=pltpu.CompilerParams(dimension_semantics=("arbitrary",), vmem_limit_bytes=VMEM_LIMIT),
    )(*tiled_in, *full_in, *weights)


def _mem_kv(mem, w_xk, w_xv):
    m = mem.shape[0]

    def body(mem_ref, wk_ref, wv_ref, k_ref, v_ref):
        mb = mem_ref[...]
        k_ref[...] = _mm(mb, wk_ref[...]).astype(BF16)
        v_ref[...] = _mm(mb, wv_ref[...]).astype(BF16)

    return pl.pallas_call(
        body, name="mem_kv",
        out_shape=[jax.ShapeDtypeStruct((m, D_MODEL), BF16)] * 2,
        compiler_params=pltpu.CompilerParams(vmem_limit_bytes=VMEM_LIMIT),
    )(mem, w_xk, w_xv)


def _in_fwd(x, tab, ln_g, ln_b, qg, kvg, w_inx, w_b, w_q2, w_kv2):
    def body(x_ref, tab_ref, lng_ref, lnb_ref, qg_ref, kvg_ref, winx, wb, wq2, wkv2,
             h0_ref, z3_ref, gate_ref, bu_ref, q_ref, k_ref, v_ref, kt_ref, vt_ref):
        h0, _, _ = _ln_fwd(x_ref[...], lng_ref[...], lnb_ref[...])
        h0_ref[...] = h0
        z = _mm(h0, winx[...])
        z3_ref[...] = z[:, :768]
        gate_ref[...] = z[:, 1024:]
        bu_ref[...] = _mm(z[:, :256], wb[...])
        tab_v = tab_ref[...]
        cq1, cq2, ck1 = tab_v[:, :128], tab_v[:, 128:], tab_v[:, :128]
        cqn, _, _ = _rms_fwd(z[:, 256:512], qg_ref[...])
        q2 = _mm(cqn, wq2[...])
        q_ref[...] = (q2[:, :1024] * _tile_heads(cq1) + q2[:, 1024:] * _tile_heads(cq2)).astype(BF16)
        ckvn, _, _ = _rms_fwd(z[:, 512:768], kvg_ref[...])
        kv2 = _mm(ckvn, wkv2[...])
        krp = z[:, 768:896] * ck1 + z[:, 896:1024] * cq2
        kf = kv2[:, :1024] + _tile_heads(krp)
        k_ref[...] = kf.astype(BF16)
        v_ref[...] = kv2[:, 1024:].astype(BF16)
        kt_ref[...] = kf.T.astype(BF16)
        vt_ref[...] = kv2[:, 1024:].T.astype(BF16)

    s = x.shape[0]
    return _row_call(
        "in_fwd", body, s, min(ROW_TILE, s), [x, tab], [ln_g, ln_b, qg, kvg], [w_inx, w_b, w_q2, w_kv2],
        [(1024, F32), (768, F32), (2048, F32), (2048, F32), (1024, BF16), (1024, BF16), (1024, BF16)], [],
        tiled_out_t=[(1024, BF16), (1024, BF16)])


def _scan_coeffs(pw_ref, rows, forward):
    row = lax.broadcasted_iota(jnp.int32, (8, NS), 0)
    out = []
    for k, r in zip((1, 2, 4), rows):
        keep = (row >= k) if forward else (row < 8 - k)
        out.append((jnp.where(keep, pw_ref[r:r + 1, :NS], 0.0), jnp.where(keep, pw_ref[r:r + 1, NS:], 0.0), k))
    return out


def _scan_level(xr, xi, coeff, forward):
    ar, ai, k = coeff
    shift = k if forward else 8 - k
    sr, si = pltpu.roll(xr, shift, 0), pltpu.roll(xi, shift, 0)
    return xr + ar * sr - ai * si, xi + ar * si + ai * sr


def _scan_fwd(pw, bu):
    s = bu.shape[0]
    ts = min(SCAN_TILE, s)
    nblk = ts // 8

    def body(bu_ref, pw_ref, h_ref, carry):
        @pl.when(pl.program_id(0) == 0)
        def _():
            carry[...] = jnp.zeros_like(carry)

        coeffs = _scan_coeffs(pw_ref, (0, 1, 3), True)

        def blk(b, _):
            r0 = pl.multiple_of(b * 8, 8)
            xr = bu_ref[pl.ds(r0, 8), :NS]
            xi = bu_ref[pl.ds(r0, 8), NS:]
            for coeff in coeffs:
                xr, xi = _scan_level(xr, xi, coeff, True)
            cr, ci = carry[7:8, :NS], carry[7:8, NS:]
            pr, pi = pw_ref[:, :NS], pw_ref[:, NS:]
            hr = xr + pr * cr - pi * ci
            hi = xi + pr * ci + pi * cr
            h_ref[pl.ds(r0, 8), :NS] = hr
            h_ref[pl.ds(r0, 8), NS:] = hi
            carry[:, :NS] = hr
            carry[:, NS:] = hi
            return 0

        lax.fori_loop(0, nblk, blk, 0)

    return _row_call("scan_fwd", body, s, ts, [bu], [pw], [], [(2 * NS, F32)], [],
                     scratch=[pltpu.VMEM((8, 2 * NS), F32)])[0]


def _scan_bwd(pwb, gh, h):
    s = gh.shape[0]
    ts = min(SCAN_TILE, s)
    nblk = ts // 8
    n_tiles = s // ts

    def body(g_ref, h_ref, pw_ref, lam_ref, da_ref, carry, acc):
        @pl.when(pl.program_id(0) == 0)
        def _():
            carry[...] = jnp.zeros_like(carry)
            acc[...] = jnp.zeros_like(acc)

        row = lax.broadcasted_iota(jnp.int32, (8, NS), 0)
        coeffs = _scan_coeffs(pw_ref, (7, 6, 4), False)

        def blk(bb, _):
            r0 = pl.multiple_of((nblk - 1 - bb) * 8, 8)
            xr = g_ref[pl.ds(r0, 8), :NS]
            xi = g_ref[pl.ds(r0, 8), NS:]
            for coeff in coeffs:
                xr, xi = _scan_level(xr, xi, coeff, False)
            cr, ci = carry[0:1, :NS], carry[0:1, NS:]
            pr, pi = pw_ref[:, :NS], pw_ref[:, NS:]
            lr = xr + pr * cr - pi * ci
            li = xi + pr * ci + pi * cr
            lam_ref[pl.ds(r0, 8), :NS] = lr
            lam_ref[pl.ds(r0, 8), NS:] = li
            nr = jnp.where(row < 7, pltpu.roll(lr, 7, 0), cr)
            ni = jnp.where(row < 7, pltpu.roll(li, 7, 0), ci)
            hr = h_ref[pl.ds(r0, 8), :NS]
            hi = h_ref[pl.ds(r0, 8), NS:]
            acc[:, :NS] += nr * hr + ni * hi
            acc[:, NS:] += ni * hr - nr * hi
            carry[:, :NS] = lr
            carry[:, NS:] = li
            return 0

        lax.fori_loop(0, nblk, blk, 0)

        @pl.when(pl.program_id(0) == n_tiles - 1)
        def _():
            da_ref[...] = jnp.sum(acc[...], axis=0, keepdims=True)

    return _row_call("scan_bwd", body, s, ts, [gh, h], [pwb], [], [(2 * NS, F32)], [(1, 2 * NS)], reverse=True,
                     scratch=[pltpu.VMEM((8, 2 * NS), F32), pltpu.VMEM((8, 2 * NS), F32)])


def _s5out_fwd(h, z3, d_skip, w_c, w_glu):
    def body(h_ref, z3_ref, d_ref, wc, wglu, yl_ref, so_ref):
        yl = _mm(h_ref[...], wc[...]) + d_ref[...] * z3_ref[:, :256]
        yl_ref[...] = yl
        y12 = _mm(_gelu(yl), wglu[...])
        so_ref[...] = y12[:, :1024] * _sigmoid(y12[:, 1024:])

    s = h.shape[0]
    return _row_call("s5out_fwd", body, s, min(ROW_TILE, s), [h, z3], [d_skip], [w_c, w_glu],
                     [(256, F32), (1024, F32)], [])


def _causal_mask_t(t):
    row = lax.broadcasted_iota(jnp.int32, (t, t), 0)
    col = lax.broadcasted_iota(jnp.int32, (t, t), 1)
    return row <= col


def _attn_fwd(q, k, vt):
    s = q.shape[0]
    t = min(ATT_TILE, s)
    nq = s // t
    scale = MLA_QK ** -0.5
    c2 = scale * LOG2E

    def body(q_ref, k_ref, vt_ref, o_ref, lse_ref, s0, s1):
        i = pl.program_id(1)
        qb = q_ref[...]

        def scores(kb, dst):
            st = _mm_nt(k_ref[pl.ds(pl.multiple_of(kb * t, t), t), :], qb)
            dst[...] = st
            return jnp.max(st, axis=0, keepdims=True)

        def update(kb, src, mt, state, masked):
            m, l, acc = state
            if masked:
                keep = _causal_mask_t(t)
                mt = jnp.max(jnp.where(keep, src[...], NEG_INF), axis=0, keepdims=True)
            m_new = jnp.maximum(m, mt)
            if masked:
                p = jnp.exp2((jnp.where(keep, src[...], NEG_INF) - m_new) * c2)
            else:
                p = jnp.exp2((src[...] - m_new) * c2)
            a = jnp.exp2((m - m_new) * c2)
            l = a * l + jnp.sum(p, axis=0, keepdims=True)
            acc = a * acc + _mm(vt_ref[:, pl.ds(pl.multiple_of(kb * t, t), t)], p)
            return m_new, l, acc

        def pair(jj, carry):
            mt_a, state = carry
            mt_b = scores(2 * jj + 1, s1)
            state = update(2 * jj, s0, mt_a, state, False)
            mt_a = scores(2 * jj + 2, s0)
            return mt_a, update(2 * jj + 1, s1, mt_b, state, False)

        def odd_tail(carry):
            mt_a, state = carry
            mt_b = scores(i, s1)
            state = update(i - 1, s0, mt_a, state, False)
            return update(i, s1, mt_b, state, True)

        def even_tail(carry):
            mt_a, state = carry
            return update(i, s0, mt_a, state, True)

        init = (jnp.full((1, t), NEG_INF, F32), jnp.zeros((1, t), F32), jnp.zeros((HEAD_PAD, t), F32))
        carry = lax.fori_loop(0, i // 2, pair, (scores(0, s0), init))
        m, l, acc = lax.cond(i % 2 == 1, odd_tail, even_tail, carry)
        o_ref[...] = (acc / l).T
        lse_ref[...] = m * scale + jnp.log(l)

    return pl.pallas_call(
        body, name="attn_fwd", grid=(MLA_HEADS, nq),
        in_specs=[pl.BlockSpec((t, HEAD_PAD), lambda h, i: (i, h)),
                  pl.BlockSpec((s, HEAD_PAD), lambda h, i: (0, h)),
                  pl.BlockSpec((HEAD_PAD, s), lambda h, i: (h, 0))],
        out_specs=[pl.BlockSpec((t, HEAD_PAD), lambda h, i: (i, h)),
                   pl.BlockSpec((None, 1, t), lambda h, i: (h, 0, i))],
        out_shape=[jax.ShapeDtypeStruct((s, MLA_HEADS * HEAD_PAD), F32),
                   jax.ShapeDtypeStruct((MLA_HEADS, 1, s), F32)],
        scratch_shapes=[pltpu.VMEM((t, t), F32)] * 2,
        compiler_params=pltpu.CompilerParams(dimension_semantics=("arbitrary", "arbitrary"),
                                             vmem_limit_bytes=VMEM_LIMIT),
    )(q, k, vt)


def _attn_bwd(q, k, v, kt, do, lse_t, dl_t):
    s = q.shape[0]
    t = min(ATT_TILE, s)
    nq = s // t
    scale = MLA_QK ** -0.5
    c2 = scale * LOG2E

    def body(q_ref, k_ref, v_ref, kt_ref, do_ref, lse_ref, dl_ref, dq_ref, dk_ref, dv_ref, dqt, s0, p0, s1, p1):
        j = pl.program_id(1)
        n = nq - 1 - j

        @pl.when(j == 0)
        def _():
            dqt[...] = jnp.zeros_like(dqt)

        kk = k_ref[...]
        vv = v_ref[...]
        ktb = kt_ref[...]

        def rows(m):
            return pl.ds(pl.multiple_of(jnp.where(m < n, j + 1 + m, j) * t, t), t)

        def first(m, sbuf, pbuf):
            r = rows(m)
            sbuf[...] = _mm_nt(kk, q_ref[r, :])
            pbuf[...] = _mm_nt(vv, do_ref[r, :])

        def finish(m, sbuf, pbuf, acc, masked):
            dk, dv = acc
            r = rows(m)
            pt = jnp.exp2(sbuf[...] * c2 - lse_ref[:, r] * LOG2E)
            if masked:
                pt = jnp.where(_causal_mask_t(t), pt, 0.0)
            dv = dv + _mm(pt, do_ref[r, :])
            dst = (pt * (pbuf[...] - dl_ref[:, r])).astype(BF16)
            dk = dk + _mm(dst, q_ref[r, :])
            dqt[:, r] += _mm(ktb, dst)
            return dk, dv

        def pair(jj, acc):
            first(2 * jj + 1, s1, p1)
            acc = finish(2 * jj, s0, p0, acc, False)
            first(2 * jj + 2, s0, p0)
            return finish(2 * jj + 1, s1, p1, acc, False)

        def odd_tail(acc):
            first(n, s1, p1)
            acc = finish(n - 1, s0, p0, acc, False)
            return finish(n, s1, p1, acc, True)

        def even_tail(acc):
            return finish(n, s0, p0, acc, True)

        zero = jnp.zeros((t, HEAD_PAD), F32)
        first(0, s0, p0)
        acc = lax.fori_loop(0, n // 2, pair, (zero, zero))
        dk, dv = lax.cond(n % 2 == 1, odd_tail, even_tail, acc)
        dk_ref[...] = (dk * scale).astype(BF16)
        dv_ref[...] = dv.astype(BF16)

        @pl.when(j == nq - 1)
        def _():
            for cc in range(nq):
                dq_ref[cc * t:(cc + 1) * t, :] = (dqt[:, cc * t:(cc + 1) * t].T * scale).astype(BF16)

    full = pl.BlockSpec((s, HEAD_PAD), lambda h, j: (0, h))
    tile = pl.BlockSpec((t, HEAD_PAD), lambda h, j: (j, h))
    stat = pl.BlockSpec((None, 1, s), lambda h, j: (h, 0, 0))
    return pl.pallas_call(
        body, name="attn_bwd", grid=(MLA_HEADS, nq),
        in_specs=[full, tile, tile, pl.BlockSpec((HEAD_PAD, t), lambda h, j: (h, j)), full, stat, stat],
        out_specs=[full, tile, tile],
        out_shape=[jax.ShapeDtypeStruct((s, MLA_HEADS * HEAD_PAD), BF16)] * 3,
        scratch_shapes=[pltpu.VMEM((HEAD_PAD, s), F32)] + [pltpu.VMEM((t, t), F32)] * 4,
        compiler_params=pltpu.CompilerParams(dimension_semantics=("arbitrary", "arbitrary"),
                                             vmem_limit_bytes=VMEM_LIMIT),
    )(q, k, v, kt, do, lse_t, dl_t)


def _xattn_probs(qxb, memk, hh):
    sl = slice(hh * XATTN_HD, (hh + 1) * XATTN_HD)
    sc = _mm_nt(qxb[:, sl], memk[:, sl]) * (XATTN_HD ** -0.5)
    e = jnp.exp(sc - jnp.max(sc, axis=1, keepdims=True))
    return e / jnp.sum(e, axis=1, keepdims=True)


def _post_forward(o, s_out, gate, h0, memk, memv, w_oa, w_o, w_xq, w_xo, g1, b1, g2, b2):
    a_out = _mm(o, w_oa)
    sg_s = _sigmoid(gate[:, :1024])
    sg_a = _sigmoid(gate[:, 1024:])
    mixin = sg_s * s_out + sg_a * a_out
    r1 = DN_ALPHA * h0 + _mm(mixin, w_o)
    h1, xh1, rs1 = _ln_fwd(r1, g1, b1)
    qxb = _mm(h1, w_xq).astype(BF16)
    ox = jnp.concatenate([_mm(_xattn_probs(qxb, memk, hh), memv[:, hh * XATTN_HD:(hh + 1) * XATTN_HD])
                          for hh in range(XATTN_HEADS)], axis=1)
    r2 = DN_ALPHA * h1 + _mm(ox, w_xo)
    return dict(a_out=a_out, mixin=mixin, r1=r1, h1=h1, qxb=qxb, ox=ox, r2=r2)


def _post_fwd(o, s_out, gate, h0, memk, memv, lns, w_oa, w_o, w_xq, w_xo):
    def body(o_ref, so_ref, gate_ref, h0_ref, mk_ref, mv_ref, g1, b1, g2, b2, woa, wo, wxq, wxo,
             r2_ref, mixin_ref, h1_ref, ox_ref, aout_ref, r1_ref, qx_ref):
        f = _post_forward(o_ref[...], so_ref[...], gate_ref[...], h0_ref[...], mk_ref[...], mv_ref[...],
                          woa[...], wo[...], wxq[...], wxo[...], g1[...], b1[...], g2[...], b2[...])
        r2_ref[...] = f["r2"]
        mixin_ref[...] = f["mixin"].astype(BF16)
        h1_ref[...] = f["h1"].astype(BF16)
        ox_ref[...] = f["ox"].astype(BF16)
        aout_ref[...] = f["a_out"]
        r1_ref[...] = f["r1"]
        qx_ref[...] = f["qxb"]

    s = o.shape[0]
    return _row_call("post_fwd", body, s, min(ROW_TILE, s), [o, s_out, gate, h0], [memk, memv, *lns],
                     [w_oa, w_o, w_xq, w_xo],
                     [(1024, F32), (1024, BF16), (1024, BF16), (1024, BF16), (1024, F32), (1024, F32), (1024, BF16)], [])


def _mlp(r2, target, g2, b2, g3, b3, w_up, w_down):
    cw = 1024
    n_chunk = MLP_HIDDEN // cw

    def body(r2_ref, tgt_ref, g2_ref, b2_ref, g3_ref, b3_ref, wup, wdn,
             dh2_ref, h2b_ref, dup_ref, act_ref, dff_ref, loss_ref, dg3_ref, db3_ref, up_scr):
        @pl.when(pl.program_id(0) == 0)
        def _():
            loss_ref[...] = jnp.zeros_like(loss_ref)
            dg3_ref[...] = jnp.zeros_like(dg3_ref)
            db3_ref[...] = jnp.zeros_like(db3_ref)

        h2, _, _ = _ln_fwd(r2_ref[...], g2_ref[...], b2_ref[...])
        h2b = h2.astype(BF16)
        h2b_ref[...] = h2b
        ff = jnp.zeros(h2.shape, F32)
        for c in range(n_chunk):
            sl = slice(c * cw, (c + 1) * cw)
            a = jnp.maximum(_mm(h2b, wup[:, sl]), 0.0)
            up_scr[:, sl] = a
            actb = (a * a).astype(BF16)
            act_ref[:, sl] = actb
            ff = ff + _mm(actb, wdn[sl, :])
        h3, xh3, rs3 = _ln_fwd(DN_ALPHA * h2 + ff, g3_ref[...], b3_ref[...])
        err = h3 - tgt_ref[...]
        loss_ref[...] += 0.5 * jnp.sum(err * err) * (1.0 / D_MODEL)
        dr3, dg3, db3 = _ln_bwd(err * (1.0 / D_MODEL), xh3, rs3, g3_ref[...])
        dg3_ref[...] += dg3
        db3_ref[...] += db3
        dffb = dr3.astype(BF16)
        dff_ref[...] = dffb
        dh2 = DN_ALPHA * dr3
        for c in range(n_chunk):
            sl = slice(c * cw, (c + 1) * cw)
            dupb = (_mm_nt(dffb, wdn[sl, :]) * (2.0 * up_scr[:, sl])).astype(BF16)
            dup_ref[:, sl] = dupb
            dh2 = dh2 + _mm_nt(dupb, wup[:, sl])
        dh2_ref[...] = dh2

    s = r2.shape[0]
    ts = min(ROW_TILE, s)
    return _row_call("mlp", body, s, ts, [r2, target], [g2, b2, g3, b3], [w_up, w_down],
                     [(1024, F32), (1024, BF16), (MLP_HIDDEN, BF16), (MLP_HIDDEN, BF16), (1024, BF16)],
                     [(8, 128), (1, 1024), (1, 1024)], scratch=[pltpu.VMEM((ts, MLP_HIDDEN), F32)])


def _post_bwd(dh2, o, s_out, gate, a_out, r1, r2, qx, memk, memv, lns, w_oa, w_o, w_xq, w_xo):
    def body(dh2_ref, o_ref, so_ref, gate_ref, aout_ref, r1_ref, r2_ref, qx_ref, mk_ref, mv_ref, g1, b1, g2, b2,
             woa, wo, wxq, wxo,
             do_ref, dl_ref, dso_ref, dgate_ref, dh0_ref, daout_ref, dmix_ref, dqx_ref, dxa_ref,
             dmk_ref, dmv_ref, dg1_ref, db1_ref, dg2_ref, db2_ref):
        @pl.when(pl.program_id(0) == 0)
        def _():
            for r in (dmk_ref, dmv_ref, dg1_ref, db1_ref, dg2_ref, db2_ref):
                r[...] = jnp.zeros_like(r)

        o = o_ref[...]
        s_out = so_ref[...]
        memk, memv = mk_ref[...], mv_ref[...]
        qxb = qx_ref[...]
        _, xh1, rs1 = _ln_fwd(r1_ref[...], g1[...], b1[...])
        _, xh2, rs2 = _ln_fwd(r2_ref[...], g2[...], b2[...])
        dr2, dg2, db2 = _ln_bwd(dh2_ref[...], xh2, rs2, g2[...])
        dg2_ref[...] += dg2
        db2_ref[...] += db2
        dxab = dr2.astype(BF16)
        dxa_ref[...] = dxab
        dox = _mm_nt(dxab, wxo[...])
        dqs = []
        for hh in range(XATTN_HEADS):
            sl = slice(hh * XATTN_HD, (hh + 1) * XATTN_HD)
            p = _xattn_probs(qxb, memk, hh)
            doxh = dox[:, sl].astype(BF16)
            dp = _mm_nt(doxh, memv[:, sl])
            ds = (p * (dp - jnp.sum(dp * p, axis=1, keepdims=True)) * (XATTN_HD ** -0.5)).astype(BF16)
            dqs.append(_mm(ds, memk[:, sl]))
            dmk_ref[:, sl] += _mm_tn(ds, qxb[:, sl])
            dmv_ref[:, sl] += _mm_tn(p, doxh)
        dqxb = jnp.concatenate(dqs, axis=1).astype(BF16)
        dqx_ref[...] = dqxb
        dh1 = DN_ALPHA * dr2 + _mm_nt(dqxb, wxq[...])
        dr1, dg1, db1 = _ln_bwd(dh1, xh1, rs1, g1[...])
        dg1_ref[...] += dg1
        db1_ref[...] += db1
        dh0_ref[...] = DN_ALPHA * dr1
        dmixb = dr1.astype(BF16)
        dmix_ref[...] = dmixb
        dmixin = _mm_nt(dmixb, wo[...])
        sg_s = _sigmoid(gate_ref[:, :1024])
        sg_a = _sigmoid(gate_ref[:, 1024:])
        dso_ref[...] = dmixin * sg_s
        daoutb = (dmixin * sg_a).astype(BF16)
        daout_ref[...] = daoutb
        dgate_ref[:, :1024] = (dmixin * s_out * sg_s * (1.0 - sg_s)).astype(BF16)
        dgate_ref[:, 1024:] = (dmixin * aout_ref[...] * sg_a * (1.0 - sg_a)).astype(BF16)
        d_o = _mm_nt(daoutb, woa[...])
        do_ref[...] = d_o.astype(BF16)
        lane = lax.broadcasted_iota(jnp.int32, (o.shape[0], HEAD_PAD), 1)
        dlc = jnp.zeros((o.shape[0], HEAD_PAD), F32)
        for hh in range(MLA_HEADS):
            sl = slice(hh * HEAD_PAD, (hh + 1) * HEAD_PAD)
            dl = jnp.sum(d_o[:, sl] * o[:, sl], axis=1, keepdims=True)
            dlc = dlc + jnp.where(lane == hh, dl, 0.0)
        dl_ref[...] = dlc

    s = o.shape[0]
    m = memk.shape[0]
    return _row_call(
        "post_bwd", body, s, min(ROW_TILE, s), [dh2, o, s_out, gate, a_out, r1, r2, qx], [memk, memv, *lns],
        [w_oa, w_o, w_xq, w_xo],
        [(1024, BF16), (HEAD_PAD, F32), (1024, F32), (2048, BF16), (1024, F32),
         (1024, BF16), (1024, BF16), (1024, BF16), (1024, BF16)],
        [(m, 1024), (m, 1024), (1, 1024), (1, 1024), (1, 1024), (1, 1024)])


def _s5out_bwd(d_so, yl, z3, d_skip, w_c, w_glu):
    def body(dso_ref, yl_ref, z3_ref, d_ref, wc, wglu, gh_ref, dup_ref, yg_ref, dy12_ref, dyl_ref, dd_ref):
        @pl.when(pl.program_id(0) == 0)
        def _():
            dd_ref[...] = jnp.zeros_like(dd_ref)

        yl = yl_ref[...]
        ygb = _gelu(yl).astype(BF16)
        yg_ref[...] = ygb
        y12 = _mm(ygb, wglu[...])
        sg = _sigmoid(y12[:, 1024:])
        dso = dso_ref[...]
        dy12b = jnp.concatenate([dso * sg, dso * y12[:, :1024] * sg * (1.0 - sg)], axis=1).astype(BF16)
        dy12_ref[...] = dy12b
        dyl = _mm_nt(dy12b, wglu[...]) * _gelu_grad(yl)
        dylb = dyl.astype(BF16)
        dyl_ref[...] = dylb
        gh_ref[...] = _mm_nt(dylb, wc[...])
        dup_ref[...] = dyl * d_ref[...]
        dd_ref[...] += jnp.sum(dyl * z3_ref[:, :256], axis=0, keepdims=True)

    s = d_so.shape[0]
    return _row_call("s5out_bwd", body, s, min(ROW_TILE, s), [d_so, yl, z3], [d_skip], [w_c, w_glu],
                     [(2 * NS, F32), (256, F32), (256, BF16), (2048, BF16), (256, BF16)], [(1, 256)])


def _in_bwd(x, tab, z3, dq, dk, dv, lam, du_p, dgate, dh0p, ln_g, ln_b, qg, kvg, w_inx, w_b, w_q2, w_kv2):
    def body(x_ref, tab_ref, z3_ref, dq_ref, dk_ref, dv_ref, lam_ref, dup_ref, dgate_ref, dh0p_ref,
             lng_ref, lnb_ref, qg_ref, kvg_ref, winx, wb, wq2, wkv2,
             dx_ref, h0b_ref, dz_ref, ub_ref, cqn_ref, ckvn_ref, dq2_ref, dkv2_ref,
             dlng_ref, dlnb_ref, dqg_ref, dkvg_ref):
        @pl.when(pl.program_id(0) == 0)
        def _():
            for r in (dlng_ref, dlnb_ref, dqg_ref, dkvg_ref):
                r[...] = jnp.zeros_like(r)

        h0, xh0, rs0 = _ln_fwd(x_ref[...], lng_ref[...], lnb_ref[...])
        h0b_ref[...] = h0.astype(BF16)
        z3 = z3_ref[...]
        ub_ref[...] = z3[:, :256].astype(BF16)
        tab_v = tab_ref[...]
        cq1, cq2, ck1 = tab_v[:, :128], tab_v[:, 128:], tab_v[:, :128]
        du = _mm_nt(lam_ref[...], wb[...]) + dup_ref[...]
        dq = dq_ref[...].astype(F32)
        dq2b = jnp.concatenate([dq * _tile_heads(cq1), dq * _tile_heads(cq2)], axis=1).astype(BF16)
        dq2_ref[...] = dq2b
        cqg, cqn, rq = _rms_fwd(z3[:, 256:512], qg_ref[...])
        cqn_ref[...] = cqg.astype(BF16)
        dcq, dqg = _rms_bwd(_mm_nt(dq2b, wq2[...]), cqn, rq, qg_ref[...])
        dqg_ref[...] += dqg
        dkv2b = jnp.concatenate([dk_ref[...], dv_ref[...]], axis=1)
        dk = dk_ref[...].astype(F32)
        dkv2_ref[...] = dkv2b
        ckvg, ckvn, rkv = _rms_fwd(z3[:, 512:768], kvg_ref[...])
        ckvn_ref[...] = ckvg.astype(BF16)
        dckv, dkvg = _rms_bwd(_mm_nt(dkv2b, wkv2[...]), ckvn, rkv, kvg_ref[...])
        dkvg_ref[...] += dkvg
        dkrp = dk[:, :HEAD_PAD]
        for hh in range(1, MLA_HEADS):
            dkrp = dkrp + dk[:, hh * HEAD_PAD:(hh + 1) * HEAD_PAD]
        dzb = jnp.concatenate([a.astype(BF16) for a in (du, dcq, dckv, dkrp * ck1, dkrp * cq2)] + [dgate_ref[...]],
                              axis=1)
        dz_ref[...] = dzb
        dh0 = _mm_nt(dzb, winx[...]) + dh0p_ref[...]
        dx, dg, db = _ln_bwd(dh0, xh0, rs0, lng_ref[...])
        dx_ref[...] = dx
        dlng_ref[...] += dg
        dlnb_ref[...] += db

    s = x.shape[0]
    return _row_call(
        "in_bwd", body, s, min(ROW_TILE, s), [x, tab, z3, dq, dk, dv, lam, du_p, dgate, dh0p],
        [ln_g, ln_b, qg, kvg], [w_inx, w_b, w_q2, w_kv2],
        [(1024, F32), (1024, BF16), (3072, BF16), (256, BF16), (256, BF16), (256, BF16), (2048, BF16), (2048, BF16)],
        [(1, 1024), (1, 1024), (1, 256), (1, 256)])


def _wgrad(name, xs, dy, out_dtype=F32, blocks=1):
    s, k = xs.shape
    n = dy.shape[1]
    nb = n // blocks
    tk, tn = min(k, 1024), min(n, 1024)
    ts = min(s, 2048 if tk <= 256 else 1024)
    per_tile = tn // nb
    assert s % ts == 0 and k % tk == 0 and n % tn == 0 and (blocks == 1 or tn % nb == 0), (name, s, k, n)
    last = s // ts - 1

    def body(x_ref, dy_ref, o_ref, acc):
        @pl.when(pl.program_id(2) == 0)
        def _():
            acc[...] = jnp.zeros_like(acc)

        acc[...] += _mm_tn(x_ref[...], dy_ref[...])

        @pl.when(pl.program_id(2) == last)
        def _():
            if blocks == 1:
                o_ref[...] = acc[...].astype(out_dtype)
            else:
                for d in range(per_tile):
                    o_ref[d] = acc[:, d * nb:(d + 1) * nb].astype(out_dtype)

    if blocks == 1:
        out_shape = jax.ShapeDtypeStruct((k, n), out_dtype)
        out_spec = pl.BlockSpec((tk, tn), lambda a, b, c: (a, b))
    else:
        out_shape = jax.ShapeDtypeStruct((blocks, k, nb), out_dtype)
        out_spec = pl.BlockSpec((per_tile, tk, nb), lambda a, b, c: (b, a, 0))
    return pl.pallas_call(
        body, name=name, grid=(k // tk, n // tn, s // ts),
        in_specs=[pl.BlockSpec((ts, tk), lambda a, b, c: (c, a)), pl.BlockSpec((ts, tn), lambda a, b, c: (c, b))],
        out_specs=out_spec, out_shape=out_shape, scratch_shapes=[pltpu.VMEM((tk, tn), F32)],
        compiler_params=pltpu.CompilerParams(dimension_semantics=("arbitrary", "arbitrary", "arbitrary"),
                                             vmem_limit_bytes=VMEM_LIMIT),
    )(xs, dy)


def _mesh_pos():
    x, y, c = lax.axis_index("x"), lax.axis_index("y"), lax.axis_index("c")
    return x, y, c


def _peer(x, y, c, k):
    px = 1 - x if k & 4 else x
    py = 1 - y if k & 2 else y
    pc = 1 - c if k & 1 else c
    return (px, py, pc), 4 * px + 2 * py + pc


def _all_to_all(name, arrays, gather):
    n = len(arrays)

    def body(*refs):
        src, dst = refs[:n], refs[n:2 * n]
        send_sems, recv_sems, local_sems = refs[2 * n:]
        x, y, c = _mesh_pos()
        me = 4 * x + 2 * y + c

        def block(i, d):
            return src[i] if gather else src[i].at[d]

        own = [pltpu.make_async_copy(block(i, me), dst[i].at[me], local_sems.at[i]) for i in range(n)]
        for cp in own:
            cp.start()
        sends = []
        for k in range(1, N_DEV):
            peer, pid = _peer(x, y, c, k)
            for i in range(n):
                idx = (k - 1) * n + i
                cp = pltpu.make_async_remote_copy(src_ref=block(i, pid), dst_ref=dst[i].at[me],
                                                  send_sem=send_sems.at[idx], recv_sem=recv_sems.at[idx],
                                                  device_id=peer, device_id_type=MESH)
                cp.start()
                sends.append(cp)
        for k in range(1, N_DEV):
            peer, pid = _peer(x, y, c, k)
            for i in range(n):
                idx = (k - 1) * n + i
                pltpu.make_async_remote_copy(src_ref=block(i, pid), dst_ref=dst[i].at[pid],
                                             send_sem=send_sems.at[idx], recv_sem=recv_sems.at[idx],
                                             device_id=peer, device_id_type=MESH).wait_recv()
        for cp in sends:
            cp.wait_send()
        for cp in own:
            cp.wait()

    n_sem = n * (N_DEV - 1)
    return pl.pallas_call(
        body, name=name,
        in_specs=[pl.BlockSpec(memory_space=pl.ANY)] * n,
        out_specs=[pl.BlockSpec(memory_space=pl.ANY)] * n,
        out_shape=[jax.ShapeDtypeStruct((N_DEV,) + a.shape[-2:], a.dtype) for a in arrays],
        scratch_shapes=[pltpu.SemaphoreType.DMA((n_sem,)), pltpu.SemaphoreType.DMA((n_sem,)),
                        pltpu.SemaphoreType.DMA((n,))],
    )(*arrays)


_HBM = pl.BlockSpec(memory_space=pltpu.HBM)
_SEM = pl.BlockSpec(memory_space=pltpu.SEMAPHORE)


def _exchange_start(name, arrays, gather, after=None):
    n = len(arrays)
    n_sem = n * (N_DEV - 1)
    me = 4 * lax.axis_index("x") + 2 * lax.axis_index("y") + lax.axis_index("c")
    lands = []
    for a in arrays:
        own = a[None] if gather else lax.dynamic_slice_in_dim(a, me, 1, 0)
        lands.append(lax.dynamic_update_slice(lax.empty((N_DEV,) + a.shape[-2:], a.dtype), own, (me, 0, 0)))
    n_after = 0 if after is None else 1

    def body(*refs):
        src, land = refs[:n], refs[n:2 * n]
        send_sems, recv_sems = refs[2 * n + n_after], refs[2 * n + n_after + 1]
        token = refs[-1]
        x, y, c = _mesh_pos()
        me_in = 4 * x + 2 * y + c
        for k in range(1, N_DEV):
            peer, pid = _peer(x, y, c, k)
            for i in range(n):
                idx = (k - 1) * n + i
                pltpu.make_async_remote_copy(src_ref=src[i] if gather else src[i].at[pid], dst_ref=land[i].at[me_in],
                                             send_sem=send_sems.at[idx], recv_sem=recv_sems.at[idx],
                                             device_id=peer, device_id_type=MESH).start()
        token[...] = jnp.zeros_like(token)

    operands = [pltpu.with_memory_space_constraint(a, pltpu.HBM) for a in list(arrays) + lands]
    outs = pl.pallas_call(
        body, name=name,
        out_shape=(pltpu.SemaphoreType.DMA((n_sem,)), pltpu.SemaphoreType.DMA((n_sem,)),
                   *[pltpu.HBM(a.shape, a.dtype) for a in list(arrays) + lands],
                   jax.ShapeDtypeStruct((8, 128), F32)),
        in_specs=[_HBM] * (2 * n) + [pl.BlockSpec(memory_space=pl.ANY)] * n_after,
        out_specs=(_SEM, _SEM, *[_HBM] * (2 * n), pl.BlockSpec(memory_space=pltpu.VMEM)),
        input_output_aliases={i: 2 + i for i in range(2 * n)},
        compiler_params=pltpu.CompilerParams(has_side_effects=pltpu.SideEffectType.DATAFLOW_SIDE_EFFECTING),
    )(*operands, *([after] if n_after else []))
    return (gather, outs[0], outs[1], outs[2:2 + n], outs[2 + n:2 + 2 * n]), outs[-1]


def _exchange_wait(name, handle, after):
    gather, send_sems, recv_sems, srcs, lands = handle
    n = len(srcs)

    def body(*refs):
        src, land = refs[:n], refs[n:2 * n]
        s_sems, r_sems = refs[2 * n], refs[2 * n + 1]
        x, y, c = _mesh_pos()
        for k in range(1, N_DEV):
            peer, pid = _peer(x, y, c, k)
            for i in range(n):
                idx = (k - 1) * n + i
                cp = pltpu.make_async_remote_copy(src_ref=src[i] if gather else src[i].at[pid], dst_ref=land[i].at[pid],
                                                  send_sem=s_sems.at[idx], recv_sem=r_sems.at[idx],
                                                  device_id=peer, device_id_type=MESH)
                cp.wait_send()
                cp.wait_recv()

    outs = pl.pallas_call(
        body, name=name,
        out_shape=tuple(pltpu.HBM(a.shape, a.dtype) for a in list(srcs) + list(lands)),
        in_specs=[_HBM] * (2 * n) + [_SEM, _SEM, pl.BlockSpec(memory_space=pl.ANY)],
        out_specs=tuple([_HBM] * (2 * n)),
        input_output_aliases={i: i for i in range(2 * n)},
        compiler_params=pltpu.CompilerParams(has_side_effects=pltpu.SideEffectType.DATAFLOW_SIDE_EFFECTING),
    )(*srcs, *lands, send_sems, recv_sems, after)
    return list(outs[n:])


def _adamw(name, parts, w, m, v):
    a_rows, b_cols = w.shape
    ta = min(a_rows, ADAM_TILE)
    assert a_rows % ta == 0
    c1 = 1.0 - ADAM_B1 ** ADAM_STEP
    c2 = 1.0 - ADAM_B2 ** ADAM_STEP

    def body(p_ref, w_ref, m_ref, v_ref, g_ref, d_ref, mo_ref, vo_ref):
        g = p_ref[0].astype(F32)
        for d in range(1, N_DEV):
            g = g + p_ref[d].astype(F32)
        g_ref[...] = g
        mn = ADAM_B1 * m_ref[...] + (1.0 - ADAM_B1) * g
        vn = ADAM_B2 * v_ref[...] + (1.0 - ADAM_B2) * (g * g)
        mo_ref[...] = mn
        vo_ref[...] = vn
        d_ref[...] = -ADAM_LR * ((mn / c1) / (jnp.sqrt(vn / c2) + ADAM_EPS) + ADAM_WD * w_ref[...])

    row = pl.BlockSpec((ta, b_cols), lambda i: (i, 0))
    return pl.pallas_call(
        body, name=name, grid=(a_rows // ta,),
        in_specs=[pl.BlockSpec((N_DEV, ta, b_cols), lambda i: (0, i, 0)), row, row, row],
        out_specs=[row] * 4,
        out_shape=[jax.ShapeDtypeStruct((a_rows, b_cols), F32)] * 4,
        compiler_params=pltpu.CompilerParams(dimension_semantics=("arbitrary",), vmem_limit_bytes=VMEM_LIMIT),
    )(parts, w, m, v)


def _pack_rows(arrays, rows):
    flat = jnp.concatenate([a.reshape(-1) for a in arrays])
    return jnp.pad(flat, (0, rows * LANES - flat.shape[0])).reshape(rows, LANES)


def _cols_from_blocks(g):
    return g.transpose(1, 0, 2).reshape(g.shape[1], N_DEV * g.shape[2])


def _blocks_from_cols(w):
    return w.reshape(w.shape[0], N_DEV, w.shape[1] // N_DEV).transpose(1, 0, 2)


def _unpack_flat(flat2d, shapes):
    flat = flat2d.reshape(-1)
    out, off = [], 0
    for shp in shapes:
        sz = math.prod(shp)
        out.append(flat[off:off + sz].reshape(shp))
        off += sz
    return out


def _s5_mats(lam_re, lam_im, log_dt, b_re, b_im, c_re, c_im):
    lr = jnp.minimum(lam_re, S5_MAX_RE)
    li = lam_im
    dt = jnp.exp(log_dt)[:, None]
    mag = jnp.exp(lr * dt)
    ang = li * dt
    ab_re = mag * jnp.cos(ang)
    ab_im = mag * jnp.sin(ang)
    den = lr * lr + li * li
    nr = ab_re - 1.0
    f_re = ((nr * lr + ab_im * li) / den)[..., None]
    f_im = ((ab_im * lr - nr * li) / den)[..., None]
    bb_re = f_re * b_re - f_im * b_im
    bb_im = f_re * b_im + f_im * b_re
    eye = jnp.eye(S5_GROUPS, dtype=F32)
    a = jnp.stack([ab_re.reshape(-1), ab_im.reshape(-1)])
    wb = jnp.concatenate([jnp.einsum("gph,gk->ghkp", bb_re, eye).reshape(S5_WIDTH, NS),
                          jnp.einsum("gph,gk->ghkp", bb_im, eye).reshape(S5_WIDTH, NS)], axis=1)
    wc = jnp.concatenate([jnp.einsum("ghp,gk->gpkh", c_re, eye).reshape(NS, S5_WIDTH),
                          -jnp.einsum("ghp,gk->gpkh", c_im, eye).reshape(NS, S5_WIDTH)], axis=0)
    return a, wb, wc


def _power_table(a):
    ar, ai = a[0], a[1]
    rows_r, rows_i = [ar], [ai]
    for _ in range(7):
        pr, pi = rows_r[-1], rows_i[-1]
        rows_r.append(pr * ar - pi * ai)
        rows_i.append(pr * ai + pi * ar)
    return jnp.concatenate([jnp.stack(rows_r), jnp.stack(rows_i)], axis=1)


def _rope_table(positions):
    inv = ROPE_THETA ** (-jnp.arange(0, MLA_ROPE, 2, dtype=F32) / MLA_ROPE)
    inv128 = jnp.concatenate([jnp.zeros((MLA_NOPE,), F32), inv, inv, jnp.zeros((32,), F32)])
    sign = jnp.concatenate([jnp.zeros((MLA_NOPE,), F32), -jnp.ones((16,), F32), jnp.ones((16,), F32),
                            jnp.zeros((32,), F32)])
    ang = positions.astype(F32)[:, None] * inv128
    return jnp.concatenate([jnp.cos(ang), jnp.sin(ang) * sign], axis=1)


def _derived_weights(full):
    w_in = full["w_in"]
    k1, k2 = w_in[:, 768:784], w_in[:, 784:800]
    z64, z32 = jnp.zeros((1024, 64), BF16), jnp.zeros((1024, 32), BF16)
    w_inx = jnp.concatenate([w_in[:, :768], z64, k1, k2, z32, z64, k2, k1, z32, w_in[:, 800:]], axis=1)
    uq = full["w_uq"].reshape(256, MLA_HEADS, MLA_QK)
    nope, r1, r2 = uq[:, :, :64], uq[:, :, 64:80], uq[:, :, 80:]
    zq64, zq32 = jnp.zeros((256, MLA_HEADS, 64), BF16), jnp.zeros((256, MLA_HEADS, 32), BF16)
    w_q2 = jnp.concatenate([jnp.concatenate([nope, r1, r2, zq32], axis=2).reshape(256, 1024),
                            jnp.concatenate([zq64, r2, r1, zq32], axis=2).reshape(256, 1024)], axis=1)
    ukv = full["w_ukv"].reshape(256, MLA_HEADS, 128)
    w_kv2 = jnp.concatenate([jnp.concatenate([ukv[:, :, :64], zq64], axis=2).reshape(256, 1024),
                             jnp.concatenate([ukv[:, :, 64:], zq64], axis=2).reshape(256, 1024)], axis=1)
    return w_inx, w_q2, w_kv2


def _oa_padded(w_oa):
    oa = w_oa.reshape(MLA_HEADS, MLA_V, 1024)
    return jnp.concatenate([oa, jnp.zeros_like(oa)], axis=1).reshape(1024, 1024)


def _fold_grads(d_inx, d_q2, d_kv2):
    d_k1 = d_inx[:, 832:848] + d_inx[:, 976:992]
    d_k2 = d_inx[:, 848:864] + d_inx[:, 960:976]
    d_w_in = jnp.concatenate([d_inx[:, :768], d_k1, d_k2, d_inx[:, 1024:]], axis=1)
    a = d_q2[:, :1024].reshape(256, MLA_HEADS, 128)
    b = d_q2[:, 1024:].reshape(256, MLA_HEADS, 128)
    d_w_uq = jnp.concatenate([a[:, :, :64], a[:, :, 64:80] + b[:, :, 80:96], a[:, :, 80:96] + b[:, :, 64:80]],
                             axis=2).reshape(256, MLA_HEADS * MLA_QK)
    kk = d_kv2[:, :1024].reshape(256, MLA_HEADS, 128)
    vv = d_kv2[:, 1024:].reshape(256, MLA_HEADS, 128)
    d_w_ukv = jnp.concatenate([kk[:, :, :64], vv[:, :, :64]], axis=2).reshape(256, 1024)
    return d_w_in, d_w_uq, d_w_ukv


def kernel(x, mem, positions, ln_in_g, ln_in_b, w_in, s5_lam_re, s5_lam_im, s5_log_dt, s5_b_re, s5_b_im, s5_c_re, s5_c_im, s5_d, w_glu, q_norm_g, w_uq, kv_norm_g, w_ukv, w_oa, w_o, ln1_g, ln1_b, w_xq, w_xk, w_xv, w_xo, ln2_g, ln2_b, w_up, w_down, ln3_g, ln3_b, loss_target, m_ln_in_g, m_ln_in_b, m_w_in, m_s5_lam_re, m_s5_lam_im, m_s5_log_dt, m_s5_b_re, m_s5_b_im, m_s5_c_re, m_s5_c_im, m_s5_d, m_w_glu, m_q_norm_g, m_w_uq, m_kv_norm_g, m_w_ukv, m_w_oa, m_w_o, m_ln1_g, m_ln1_b, m_w_xq, m_w_xk, m_w_xv, m_w_xo, m_ln2_g, m_ln2_b, m_w_up, m_w_down, m_ln3_g, m_ln3_b, v_ln_in_g, v_ln_in_b, v_w_in, v_s5_lam_re, v_s5_lam_im, v_s5_log_dt, v_s5_b_re, v_s5_b_im, v_s5_c_re, v_s5_c_im, v_s5_d, v_w_glu, v_q_norm_g, v_w_uq, v_kv_norm_g, v_w_ukv, v_w_oa, v_w_o, v_ln1_g, v_ln1_b, v_w_xq, v_w_xk, v_w_xv, v_w_xo, v_ln2_g, v_ln2_b, v_w_up, v_w_down, v_ln3_g, v_ln3_b):
    args = dict(locals())
    wts = {n: args[n] for n in WEIGHTS}
    mom = {n: args["m_" + n] for n in WEIGHTS}
    vel = {n: args["v_" + n] for n in WEIGHTS}
    xs, mems, tgt = x[0], mem[0], loss_target[0]
    s = xs.shape[0]
    names_big = [n for (n, _, _, _) in SHARDED]
    kind = {n: kd for (n, kd, _, _) in SHARDED}
    n_small = sum(math.prod(wts[n].shape) for n in SMALL)
    small_rows = _round_up(n_small + 1, 8 * LANES) // LANES

    late = [n for n in names_big if n not in EARLY]
    shard = {n: wts[n][0].astype(BF16) for n in names_big}
    g_early = dict(zip(EARLY, _all_to_all("all_gather", [shard[n] for n in EARLY], True)))
    ag_handle, ag_token = _exchange_start("all_gather_late", [shard[n] for n in late], True, after=g_early["w_in"])
    full = {n: _cols_from_blocks(g_early[n]) for n in EARLY}
    w_inx, w_q2, w_kv2 = _derived_weights(full)
    s5_args = (s5_lam_re[0], s5_lam_im[0], s5_log_dt[0], s5_b_re[0], s5_b_im[0], s5_c_re[0], s5_c_im[0])
    (a_mat, w_b, w_c), s5_vjp = jax.vjp(_s5_mats, *s5_args)
    w_bb, w_cb = w_b.astype(BF16), w_c.astype(BF16)
    pw = _power_table(a_mat)
    pwb = _power_table(a_mat * jnp.array([[1.0], [-1.0]], F32))[::-1]
    tab = _rope_table(positions[0])
    row = lambda a: a.reshape(1, -1)
    ln_g, ln_b = row(ln_in_g) + ag_token[0:1, 0:1], row(ln_in_b)
    lns = [ln1_g, ln1_b, ln2_g, ln2_b]

    h0, z3, gate, bu, q, k, v, kt, vt = _in_fwd(xs, tab, ln_g, ln_b, q_norm_g, kv_norm_g, w_inx, w_bb, w_q2, w_kv2)
    h = _scan_fwd(pw, bu)
    yl, s_out = _s5out_fwd(h, z3, s5_d, w_cb, full["w_glu"])
    o, lse_t = _attn_fwd(q, k, vt)
    g_late = dict(zip(late, _exchange_wait("all_gather_late_wait", ag_handle, after=lse_t)))
    full.update({n: g_late[n].reshape(-1, g_late[n].shape[2]) if kind[n] == "row" else _cols_from_blocks(g_late[n])
                 for n in late})
    w_oap = _oa_padded(full["w_oa"])
    memk, memv = _mem_kv(mems, full["w_xk"], full["w_xv"])
    r2, mixin_b, h1_b, ox_b, a_out, r1, qx_b = _post_fwd(o, s_out, gate, h0, memk, memv, lns,
                                        w_oap, full["w_o"], full["w_xq"], full["w_xo"])
    dh2, h2_b, dup_b, act_b, dff_b, loss_acc, d_ln3_g, d_ln3_b = _mlp(
        r2, tgt, ln2_g, ln2_b, ln3_g, ln3_b, full["w_up"], full["w_down"])
    (do_b, delta, d_so, dgate, dh0p, daout_b, dmix_b, dqx_b, dxa_b,
     dmk, dmv, d_ln1_g, d_ln1_b, d_ln2_g, d_ln2_b) = _post_bwd(
        dh2, o, s_out, gate, a_out, r1, r2, qx_b, memk, memv, lns, w_oap, full["w_o"], full["w_xq"], full["w_xo"])
    rows8 = lambda g: g.reshape(N_DEV, g.shape[0] // N_DEV, g.shape[1])
    d_w_oa = _wgrad("wg_oa", o, daout_b).reshape(MLA_HEADS, 128, 1024)[:, :64].reshape(512, 1024)
    send = {
        "w_oa": _blocks_from_cols(d_w_oa).astype(BF16),
        "w_up": _wgrad("wg_up", h2_b, dup_b, BF16, N_DEV),
        "w_o": rows8(_wgrad("wg_o", mixin_b, dmix_b, BF16)),
        "w_xq": rows8(_wgrad("wg_xq", h1_b, dqx_b, BF16)),
        "w_xk": rows8(_wgrad("wg_xk", mems, dmk, BF16)),
        "w_xv": rows8(_wgrad("wg_xv", mems, dmv, BF16)),
        "w_xo": rows8(_wgrad("wg_xo", ox_b, dxa_b, BF16)),
        "w_down": rows8(_wgrad("wg_down", act_b, dff_b, BF16)),
    }
    rs_handle, rs_token = _exchange_start("grad_exchange_late", [send[n] for n in late], False)
    dl_t = delta.T[:MLA_HEADS].reshape(MLA_HEADS, 1, s) + rs_token[0, 0]
    dq, dk, dv = _attn_bwd(q, k, v, kt, do_b, lse_t, dl_t)
    gh, du_p, yg_b, dy12_b, dyl_b, d_s5_d = _s5out_bwd(d_so, yl, z3, s5_d, w_cb, full["w_glu"])
    lam, d_a = _scan_bwd(pwb, gh, h)
    (dx, h0_b, dz_b, u_b, cqn_b, ckvn_b, dq2_b, dkv2_b, d_ln_g, d_ln_b, d_qg, d_kvg) = _in_bwd(
        xs, tab, z3, dq, dk, dv, lam, du_p, dgate, dh0p, ln_g, ln_b, q_norm_g, kv_norm_g, w_inx, w_bb, w_q2, w_kv2)

    d_w_in, d_w_uq, d_w_ukv = _fold_grads(
        _wgrad("wg_in", h0_b, dz_b), _wgrad("wg_q", cqn_b, dq2_b), _wgrad("wg_kv", ckvn_b, dkv2_b))
    send.update({
        "w_in": _blocks_from_cols(d_w_in).astype(BF16), "w_uq": _blocks_from_cols(d_w_uq).astype(BF16),
        "w_ukv": _blocks_from_cols(d_w_ukv).astype(BF16),
        "w_glu": _wgrad("wg_glu", yg_b, dy12_b, BF16, N_DEV),
    })
    d_s5 = s5_vjp((d_a.reshape(2, NS), _wgrad("wg_s5b", u_b, lam), _wgrad("wg_s5c", h, dyl_b)))
    small_grads = {
        "ln_in_g": d_ln_g, "ln_in_b": d_ln_b, "s5_lam_re": d_s5[0], "s5_lam_im": d_s5[1], "s5_log_dt": d_s5[2],
        "s5_b_re": d_s5[3], "s5_b_im": d_s5[4], "s5_c_re": d_s5[5], "s5_c_im": d_s5[6], "s5_d": d_s5_d,
        "q_norm_g": d_qg, "kv_norm_g": d_kvg, "ln1_g": d_ln1_g, "ln1_b": d_ln1_b, "ln2_g": d_ln2_g,
        "ln2_b": d_ln2_b, "ln3_g": d_ln3_g, "ln3_b": d_ln3_b,
    }
    small_send = jnp.broadcast_to(_pack_rows([small_grads[n] for n in SMALL] + [loss_acc[0, :1]], small_rows)[None],
                                  (N_DEV, small_rows, LANES))

    recv_early = _all_to_all("grad_exchange", [send[n] for n in EARLY] + [small_send], False)
    recv_late = _exchange_wait("grad_exchange_late_wait", rs_handle, after=recv_early[-1])
    recv = dict(zip(list(EARLY) + late, list(recv_early[:-1]) + recv_late))
    results = [dict(), dict(), dict(), dict()]
    for n in names_big:
        parts = recv[n]
        outs = _adamw("adamw_" + n, parts, wts[n][0], mom[n][0], vel[n][0])
        for res, a in zip(results, outs):
            res[n] = a[None]
    small_out = _adamw("adamw_small", recv_early[-1],
                       *[_pack_rows([t[n] for n in SMALL], small_rows) for t in (wts, mom, vel)])
    for res, fs in zip(results, small_out):
        for n, a in zip(SMALL, _unpack_flat(fs, [wts[n].shape for n in SMALL])):
            res[n] = a

    loss = small_out[0].reshape(-1)[n_small]
    return (loss, dx[None], *[res[n] for res in results for n in WEIGHTS])
```

```python
import math

import jax
import jax.numpy as jnp
from jax import lax
from jax.experimental import pallas as pl
from jax.experimental.pallas import tpu as pltpu

F32 = jnp.float32
BF16 = jnp.bfloat16

D_MODEL = 1024
S5_WIDTH = 256
S5_GROUP_CH = 16
S5_GROUPS = 16
S5_STATE = 64
NS = S5_GROUPS * S5_STATE
S5_MAX_RE = -1e-4
MLA_HEADS = 8
MLA_NOPE = 64
MLA_ROPE = 32
MLA_QK = 96
MLA_V = 64
HEAD_PAD = 128
ROPE_THETA = 10000.0
XATTN_HEADS = 4
XATTN_HD = 256
MLP_HIDDEN = 4096
LN_EPS = 1e-5
RMS_EPS = 1e-6
NEG_INF = -1e30
LOG2E = 1.4426950408889634
DN_ALPHA = 2.0 ** 0.25
ADAM_LR = 0.001
ADAM_B1 = 0.9
ADAM_B2 = 0.999
ADAM_EPS = 1e-08
ADAM_WD = 0.01
ADAM_STEP = 10

N_DEV = 8
MESH = pl.DeviceIdType.MESH
LANES = 1024
VMEM_LIMIT = 60 * 1024 * 1024

ROW_TILE = 256
SCAN_TILE = 1024
ATT_TILE = 512
ADAM_TILE = 128

SHARDED = (
    ("w_in", "col", 1024, 2848), ("w_glu", "col", 256, 2048), ("w_uq", "col", 256, 768),
    ("w_ukv", "col", 256, 1024), ("w_oa", "col", 512, 1024), ("w_o", "row", 1024, 1024),
    ("w_xq", "row", 1024, 1024), ("w_xk", "row", 1024, 1024), ("w_xv", "row", 1024, 1024),
    ("w_xo", "row", 1024, 1024), ("w_up", "col", 1024, 4096), ("w_down", "row", 4096, 1024),
)
EARLY = ("w_in", "w_glu", "w_uq", "w_ukv")
SMALL = ("ln_in_g", "ln_in_b", "s5_lam_re", "s5_lam_im", "s5_log_dt", "s5_b_re", "s5_b_im", "s5_c_re",
         "s5_c_im", "s5_d", "q_norm_g", "kv_norm_g", "ln1_g", "ln1_b", "ln2_g", "ln2_b", "ln3_g", "ln3_b")
WEIGHTS = ("ln_in_g", "ln_in_b", "w_in", "s5_lam_re", "s5_lam_im", "s5_log_dt", "s5_b_re", "s5_b_im",
           "s5_c_re", "s5_c_im", "s5_d", "w_glu", "q_norm_g", "w_uq", "kv_norm_g", "w_ukv", "w_oa", "w_o",
           "ln1_g", "ln1_b", "w_xq", "w_xk", "w_xv", "w_xo", "ln2_g", "ln2_b", "w_up", "w_down", "ln3_g", "ln3_b")


def _round_up(n, m):
    return (n + m - 1) // m * m


def _bf(a):
    return a.astype(BF16)


def _mm(a, b):
    return jnp.dot(_bf(a), _bf(b), preferred_element_type=F32)


def _mm_nt(a, b):
    return lax.dot_general(_bf(a), _bf(b), (((1,), (1,)), ((), ())), preferred_element_type=F32)


def _mm_tn(a, b):
    return lax.dot_general(_bf(a), _bf(b), (((0,), (0,)), ((), ())), preferred_element_type=F32)


def _sigmoid(a):
    return 1.0 / (1.0 + jnp.exp(-a))


def _gelu(a):
    return 0.5 * a * (1.0 + lax.erf(a * (2.0 ** -0.5)))


def _gelu_grad(a):
    return 0.5 * (1.0 + lax.erf(a * (2.0 ** -0.5))) + a * jnp.exp(-0.5 * a * a) * (1.0 / math.sqrt(2.0 * math.pi))


def _ln_fwd(a, g, b):
    mu = jnp.mean(a, axis=-1, keepdims=True)
    ac = a - mu
    var = jnp.mean(ac * ac, axis=-1, keepdims=True)
    rstd = lax.rsqrt(var + LN_EPS)
    xhat = ac * rstd
    return xhat * g + b, xhat, rstd


def _ln_bwd(dy, xhat, rstd, g):
    dxh = dy * g
    m1 = jnp.mean(dxh, axis=-1, keepdims=True)
    m2 = jnp.mean(dxh * xhat, axis=-1, keepdims=True)
    dx = rstd * (dxh - m1 - xhat * m2)
    return dx, jnp.sum(dy * xhat, axis=0, keepdims=True), jnp.sum(dy, axis=0, keepdims=True)


def _rms_fwd(a, g):
    r = lax.rsqrt(jnp.mean(a * a, axis=-1, keepdims=True) + RMS_EPS)
    xn = a * r
    return xn * g, xn, r


def _rms_bwd(dy, xn, r, g):
    dxn = dy * g
    dx = r * (dxn - xn * jnp.mean(dxn * xn, axis=-1, keepdims=True))
    return dx, jnp.sum(dy * xn, axis=0, keepdims=True)


def _tile_heads(a):
    return jnp.concatenate([a] * MLA_HEADS, axis=1)


def _row_call(name, body, n_rows, ts, tiled_in, full_in, weights, tiled_out, acc_out, reverse=False, scratch=(),
              tiled_out_t=()):
    n = n_rows // ts
    assert n * ts == n_rows, (name, n_rows, ts)
    nt, nf, nw = len(tiled_in), len(full_in), len(weights)
    nto, nao = len(tiled_out) + len(tiled_out_t), len(acc_out)
    if reverse:
        imap = lambda i: (n - 1 - i, 0)
    else:
        imap = lambda i: (i, 0)
    const = lambda i: (0, 0)

    def kern(*refs):
        ins = refs[:nt + nf]
        w_hbm = refs[nt + nf:nt + nf + nw]
        outs = refs[nt + nf + nw:nt + nf + nw + nto + nao]
        scr = refs[nt + nf + nw + nto + nao:]
        w_vmem = scr[:nw]
        extra = scr[nw + 1:] if nw else scr
        if nw:
            sem = scr[nw]

            @pl.when(pl.program_id(0) == 0)
            def _():
                cps = [pltpu.make_async_copy(w_hbm[k], w_vmem[k], sem.at[k]) for k in range(nw)]
                for cp in cps:
                    cp.start()
                for cp in cps:
                    cp.wait()
        body(*ins, *w_vmem, *outs, *extra)

    in_specs = [pl.BlockSpec((ts, a.shape[1]), imap) for a in tiled_in]
    in_specs += [pl.BlockSpec(a.shape, const) for a in full_in]
    in_specs += [pl.BlockSpec(memory_space=pl.ANY) for _ in weights]
    assert not (reverse and tiled_out_t)
    out_shape = [jax.ShapeDtypeStruct((n_rows, c), dt) for (c, dt) in tiled_out]
    out_shape += [jax.ShapeDtypeStruct((c, n_rows), dt) for (c, dt) in tiled_out_t]
    out_shape += [jax.ShapeDtypeStruct(shp, F32) for shp in acc_out]
    out_specs = [pl.BlockSpec((ts, c), imap) for (c, dt) in tiled_out]
    out_specs += [pl.BlockSpec((c, ts), lambda i: (0, i)) for (c, dt) in tiled_out_t]
    out_specs += [pl.BlockSpec(shp, const) for shp in acc_out]
    scratch_shapes = [pltpu.VMEM(w.shape, w.dtype) for w in weights]
    if nw:
        scratch_shapes.append(pltpu.SemaphoreType.DMA((nw,)))
    scratch_shapes += list(scratch)
    return pl.pallas_call(
        kern, name=name, grid=(n,), in_specs=in_specs, out_specs=out_specs, out_shape=out_shape,
        scratch_shapes=scratch_shapes,
        compiler_params=pltpu.CompilerParams(dimension_semantics=("arbitrary",), vmem_limit_bytes=VMEM_LIMIT),
    )(*tiled_in, *full_in, *weights)


def _mem_kv(mem, w_xk, w_xv):
    m = mem.shape[0]

    def body(mem_ref, wk_ref, wv_ref, k_ref, v_ref):
        mb = mem_ref[...]
        k_ref[...] = _mm(mb, wk_ref[...]).astype(BF16)
        v_ref[...] = _mm(mb, wv_ref[...]).astype(BF16)

    return pl.pallas_call(
        body, name="mem_kv",
        out_shape=[jax.ShapeDtypeStruct((m, D_MODEL), BF16)] * 2,
        compiler_params=pltpu.CompilerParams(vmem_limit_bytes=VMEM_LIMIT),
    )(mem, w_xk, w_xv)


def _in_fwd(x, tab, ln_g, ln_b, qg, kvg, w_inx, w_b, w_q2, w_kv2):
    def body(x_ref, tab_ref, lng_ref, lnb_ref, qg_ref, kvg_ref, winx, wb, wq2, wkv2,
             h0_ref, z3_ref, gate_ref, bu_ref, q_ref, k_ref, v_ref, kt_ref, vt_ref):
        h0, _, _ = _ln_fwd(x_ref[...], lng_ref[...], lnb_ref[...])
        h0_ref[...] = h0
        z = _mm(h0, winx[...])
        z3_ref[...] = z[:, :768]
        gate_ref[...] = z[:, 1024:].astype(BF16)
        bu_ref[...] = _mm(z[:, :256], wb[...])
        tab_v = tab_ref[...]
        cq1, cq2, ck1 = tab_v[:, :128], tab_v[:, 128:], tab_v[:, :128]
        cqn, _, _ = _rms_fwd(z[:, 256:512], qg_ref[...])
        q2 = _mm(cqn, wq2[...])
        q_ref[...] = (q2[:, :1024] * _tile_heads(cq1) + q2[:, 1024:] * _tile_heads(cq2)).astype(BF16)
        ckvn, _, _ = _rms_fwd(z[:, 512:768], kvg_ref[...])
        kv2 = _mm(ckvn, wkv2[...])
        krp = z[:, 768:896] * ck1 + z[:, 896:1024] * cq2
        kf = kv2[:, :1024] + _tile_heads(krp)
        k_ref[...] = kf.astype(BF16)
        v_ref[...] = kv2[:, 1024:].astype(BF16)
        kt_ref[...] = kf.T.astype(BF16)
        vt_ref[...] = kv2[:, 1024:].T.astype(BF16)

    s = x.shape[0]
    return _row_call(
        "in_fwd", body, s, min(ROW_TILE, s), [x, tab], [ln_g, ln_b, qg, kvg], [w_inx, w_b, w_q2, w_kv2],
        [(1024, F32), (768, F32), (2048, BF16), (2048, F32), (1024, BF16), (1024, BF16), (1024, BF16)], [],
        tiled_out_t=[(1024, BF16), (1024, BF16)])


def _scan_coeffs(pw_ref, rows, forward):
    row = lax.broadcasted_iota(jnp.int32, (8, NS), 0)
    out = []
    for k, r in zip((1, 2, 4), rows):
        keep = (row >= k) if forward else (row < 8 - k)
        out.append((jnp.where(keep, pw_ref[r:r + 1, :NS], 0.0), jnp.where(keep, pw_ref[r:r + 1, NS:], 0.0), k))
    return out


def _scan_level(xr, xi, coeff, forward):
    ar, ai, k = coeff
    shift = k if forward else 8 - k
    sr, si = pltpu.roll(xr, shift, 0), pltpu.roll(xi, shift, 0)
    return xr + ar * sr - ai * si, xi + ar * si + ai * sr


def _scan_fwd(pw, bu):
    s = bu.shape[0]
    ts = min(SCAN_TILE, s)
    nblk = ts // 8

    def body(bu_ref, pw_ref, h_ref, carry):
        @pl.when(pl.program_id(0) == 0)
        def _():
            carry[...] = jnp.zeros_like(carry)

        coeffs = _scan_coeffs(pw_ref, (0, 1, 3), True)

        def blk(b, _):
            r0 = pl.multiple_of(b * 8, 8)
            xr = bu_ref[pl.ds(r0, 8), :NS]
            xi = bu_ref[pl.ds(r0, 8), NS:]
            for coeff in coeffs:
                xr, xi = _scan_level(xr, xi, coeff, True)
            cr, ci = carry[7:8, :NS], carry[7:8, NS:]
            pr, pi = pw_ref[:, :NS], pw_ref[:, NS:]
            hr = xr + pr * cr - pi * ci
            hi = xi + pr * ci + pi * cr
            h_ref[pl.ds(r0, 8), :NS] = hr
            h_ref[pl.ds(r0, 8), NS:] = hi
            carry[:, :NS] = hr
            carry[:, NS:] = hi
            return 0

        lax.fori_loop(0, nblk, blk, 0)

    return _row_call("scan_fwd", body, s, ts, [bu], [pw], [], [(2 * NS, F32)], [],
                     scratch=[pltpu.VMEM((8, 2 * NS), F32)])[0]


def _scan_bwd(pwb, gh, h):
    s = gh.shape[0]
    ts = min(SCAN_TILE, s)
    nblk = ts // 8
    n_tiles = s // ts

    def body(g_ref, h_ref, pw_ref, lam_ref, da_ref, carry, acc):
        @pl.when(pl.program_id(0) == 0)
        def _():
            carry[...] = jnp.zeros_like(carry)
            acc[...] = jnp.zeros_like(acc)

        row = lax.broadcasted_iota(jnp.int32, (8, NS), 0)
        coeffs = _scan_coeffs(pw_ref, (7, 6, 4), False)

        def blk(bb, _):
            r0 = pl.multiple_of((nblk - 1 - bb) * 8, 8)
            xr = g_ref[pl.ds(r0, 8), :NS]
            xi = g_ref[pl.ds(r0, 8), NS:]
            for coeff in coeffs:
                xr, xi = _scan_level(xr, xi, coeff, False)
            cr, ci = carry[0:1, :NS], carry[0:1, NS:]
            pr, pi = pw_ref[:, :NS], pw_ref[:, NS:]
            lr = xr + pr * cr - pi * ci
            li = xi + pr * ci + pi * cr
            lam_ref[pl.ds(r0, 8), :NS] = lr
            lam_ref[pl.ds(r0, 8), NS:] = li
            nr = jnp.where(row < 7, pltpu.roll(lr, 7, 0), cr)
            ni = jnp.where(row < 7, pltpu.roll(li, 7, 0), ci)
            hr = h_ref[pl.ds(r0, 8), :NS]
            hi = h_ref[pl.ds(r0, 8), NS:]
            acc[:, :NS] += nr * hr + ni * hi
            acc[:, NS:] += ni * hr - nr * hi
            carry[:, :NS] = lr
            carry[:, NS:] = li
            return 0

        lax.fori_loop(0, nblk, blk, 0)

        @pl.when(pl.program_id(0) == n_tiles - 1)
        def _():
            da_ref[...] = jnp.sum(acc[...], axis=0, keepdims=True)

    return _row_call("scan_bwd", body, s, ts, [gh, h], [pwb], [], [(2 * NS, F32)], [(1, 2 * NS)], reverse=True,
                     scratch=[pltpu.VMEM((8, 2 * NS), F32), pltpu.VMEM((8, 2 * NS), F32)])


def _s5out_fwd(h, z3, d_skip, w_c, w_glu):
    def body(h_ref, z3_ref, d_ref, wc, wglu, yl_ref, so_ref):
        yl = _mm(h_ref[...], wc[...]) + d_ref[...] * z3_ref[:, :256]
        yl_ref[...] = yl
        y12 = _mm(_gelu(yl), wglu[...])
        so_ref[...] = y12[:, :1024] * _sigmoid(y12[:, 1024:])

    s = h.shape[0]
    return _row_call("s5out_fwd", body, s, min(ROW_TILE, s), [h, z3], [d_skip], [w_c, w_glu],
                     [(256, F32), (1024, F32)], [])


def _causal_mask_t(t):
    row = lax.broadcasted_iota(jnp.int32, (t, t), 0)
    col = lax.broadcasted_iota(jnp.int32, (t, t), 1)
    return row <= col


def _attn_fwd(q, k, vt):
    s = q.shape[0]
    t = min(ATT_TILE, s)
    nq = s // t
    scale = MLA_QK ** -0.5
    c2 = scale * LOG2E

    def body(q_ref, k_ref, vt_ref, o_ref, lse_ref, s0, s1):
        i = pl.program_id(1)
        qb = q_ref[...]

        def scores(kb, dst):
            st = _mm_nt(k_ref[pl.ds(pl.multiple_of(kb * t, t), t), :], qb)
            dst[...] = st
            return jnp.max(st, axis=0, keepdims=True)

        def update(kb, src, mt, state, masked):
            m, l, acc = state
            if masked:
                keep = _causal_mask_t(t)
                mt = jnp.max(jnp.where(keep, src[...], NEG_INF), axis=0, keepdims=True)
            m_new = jnp.maximum(m, mt)
            if masked:
                p = jnp.exp2((jnp.where(keep, src[...], NEG_INF) - m_new) * c2)
            else:
                p = jnp.exp2((src[...] - m_new) * c2)
            a = jnp.exp2((m - m_new) * c2)
            l = a * l + jnp.sum(p, axis=0, keepdims=True)
            acc = a * acc + _mm(vt_ref[:, pl.ds(pl.multiple_of(kb * t, t), t)], p)
            return m_new, l, acc

        def pair(jj, carry):
            mt_a, state = carry
            mt_b = scores(2 * jj + 1, s1)
            state = update(2 * jj, s0, mt_a, state, False)
            mt_a = scores(2 * jj + 2, s0)
            return mt_a, update(2 * jj + 1, s1, mt_b, state, False)

        def odd_tail(carry):
            mt_a, state = carry
            mt_b = scores(i, s1)
            state = update(i - 1, s0, mt_a, state, False)
            return update(i, s1, mt_b, state, True)

        def even_tail(carry):
            mt_a, state = carry
            return update(i, s0, mt_a, state, True)

        init = (jnp.full((1, t), NEG_INF, F32), jnp.zeros((1, t), F32), jnp.zeros((HEAD_PAD, t), F32))
        carry = lax.fori_loop(0, i // 2, pair, (scores(0, s0), init))
        m, l, acc = lax.cond(i % 2 == 1, odd_tail, even_tail, carry)
        o_ref[...] = (acc / l).T
        lse_ref[...] = m * scale + jnp.log(l)

    return pl.pallas_call(
        body, name="attn_fwd", grid=(MLA_HEADS, nq),
        in_specs=[pl.BlockSpec((t, HEAD_PAD), lambda h, i: (i, h)),
                  pl.BlockSpec((s, HEAD_PAD), lambda h, i: (0, h)),
                  pl.BlockSpec((HEAD_PAD, s), lambda h, i: (h, 0))],
        out_specs=[pl.BlockSpec((t, HEAD_PAD), lambda h, i: (i, h)),
                   pl.BlockSpec((None, 1, t), lambda h, i: (h, 0, i))],
        out_shape=[jax.ShapeDtypeStruct((s, MLA_HEADS * HEAD_PAD), F32),
                   jax.ShapeDtypeStruct((MLA_HEADS, 1, s), F32)],
        scratch_shapes=[pltpu.VMEM((t, t), F32)] * 2,
        compiler_params=pltpu.CompilerParams(dimension_semantics=("arbitrary", "arbitrary"),
                                             vmem_limit_bytes=VMEM_LIMIT),
    )(q, k, vt)


def _attn_bwd(q, k, v, kt, do, lse_t, dl_t):
    s = q.shape[0]
    t = min(ATT_TILE, s)
    nq = s // t
    scale = MLA_QK ** -0.5
    c2 = scale * LOG2E

    def body(q_ref, k_ref, v_ref, kt_ref, do_ref, lse_ref, dl_ref, dq_ref, dk_ref, dv_ref, dqt, s0, p0, s1, p1):
        j = pl.program_id(1)
        n = nq - 1 - j

        @pl.when(j == 0)
        def _():
            dqt[...] = jnp.zeros_like(dqt)

        kk = k_ref[...]
        vv = v_ref[...]
        ktb = kt_ref[:MLA_QK, :]

        def rows(m):
            return pl.ds(pl.multiple_of(jnp.where(m < n, j + 1 + m, j) * t, t), t)

        def first(m, sbuf, pbuf):
            r = rows(m)
            sbuf[...] = _mm_nt(kk, q_ref[r, :])
            pbuf[...] = _mm_nt(vv, do_ref[r, :])

        def finish(m, sbuf, pbuf, acc, masked):
            dk, dv = acc
            r = rows(m)
            pt = jnp.exp2(sbuf[...] * c2 - lse_ref[:, r] * LOG2E)
            if masked:
                pt = jnp.where(_causal_mask_t(t), pt, 0.0)
            dv = dv + _mm(pt, do_ref[r, :])
            dst = (pt * (pbuf[...] - dl_ref[:, r])).astype(BF16)
            dk = dk + _mm(dst, q_ref[r, :])
            dqt[:MLA_QK, r] += _mm(ktb, dst)
            return dk, dv

        def pair(jj, acc):
            first(2 * jj + 1, s1, p1)
            acc = finish(2 * jj, s0, p0, acc, False)
            first(2 * jj + 2, s0, p0)
            return finish(2 * jj + 1, s1, p1, acc, False)

        def odd_tail(acc):
            first(n, s1, p1)
            acc = finish(n - 1, s0, p0, acc, False)
            return finish(n, s1, p1, acc, True)

        def even_tail(acc):
            return finish(n, s0, p0, acc, True)

        zero = jnp.zeros((t, HEAD_PAD), F32)
        first(0, s0, p0)
        acc = lax.fori_loop(0, n // 2, pair, (zero, zero))
        dk, dv = lax.cond(n % 2 == 1, odd_tail, even_tail, acc)
        dk_ref[...] = (dk * scale).astype(BF16)
        dv_ref[...] = dv.astype(BF16)

        @pl.when(j == nq - 1)
        def _():
            for cc in range(nq):
                dq_ref[cc * t:(cc + 1) * t, :] = (dqt[:, cc * t:(cc + 1) * t].T * scale).astype(BF16)

    full = pl.BlockSpec((s, HEAD_PAD), lambda h, j: (0, h))
    tile = pl.BlockSpec((t, HEAD_PAD), lambda h, j: (j, h))
    stat = pl.BlockSpec((None, 1, s), lambda h, j: (h, 0, 0))
    return pl.pallas_call(
        body, name="attn_bwd", grid=(MLA_HEADS, nq),
        in_specs=[full, tile, tile, pl.BlockSpec((HEAD_PAD, t), lambda h, j: (h, j)), full, stat, stat],
        out_specs=[full, tile, tile],
        out_shape=[jax.ShapeDtypeStruct((s, MLA_HEADS * HEAD_PAD), BF16)] * 3,
        scratch_shapes=[pltpu.VMEM((HEAD_PAD, s), F32)] + [pltpu.VMEM((t, t), F32)] * 4,
        compiler_params=pltpu.CompilerParams(dimension_semantics=("arbitrary", "arbitrary"),
                                             vmem_limit_bytes=VMEM_LIMIT),
    )(q, k, v, kt, do, lse_t, dl_t)


def _xattn_probs(qxb, memk, hh):
    sl = slice(hh * XATTN_HD, (hh + 1) * XATTN_HD)
    sc = _mm_nt(qxb[:, sl], memk[:, sl]) * (XATTN_HD ** -0.5)
    e = jnp.exp(sc - jnp.max(sc, axis=1, keepdims=True))
    return e / jnp.sum(e, axis=1, keepdims=True)


def _post_forward(o, s_out, gate, h0, memk, memv, w_oa, w_o, w_xq, w_xo, g1, b1, g2, b2):
    a_out = _mm(o, w_oa)
    sg_s = _sigmoid(gate[:, :1024])
    sg_a = _sigmoid(gate[:, 1024:])
    mixin = sg_s * s_out + sg_a * a_out
    r1 = DN_ALPHA * h0 + _mm(mixin, w_o)
    h1, xh1, rs1 = _ln_fwd(r1, g1, b1)
    qxb = _mm(h1, w_xq).astype(BF16)
    ox = jnp.concatenate([_mm(_xattn_probs(qxb, memk, hh), memv[:, hh * XATTN_HD:(hh + 1) * XATTN_HD])
                          for hh in range(XATTN_HEADS)], axis=1)
    r2 = DN_ALPHA * h1 + _mm(ox, w_xo)
    return dict(a_out=a_out, mixin=mixin, r1=r1, h1=h1, qxb=qxb, ox=ox, r2=r2)


def _post_fwd(o, s_out, gate, h0, memk, memv, lns, w_oa, w_o, w_xq, w_xo):
    def body(o_ref, so_ref, gate_ref, h0_ref, mk_ref, mv_ref, g1, b1, g2, b2, woa, wo, wxq, wxo,
             r2_ref, mixin_ref, h1_ref, ox_ref, aout_ref, r1_ref, qx_ref):
        f = _post_forward(o_ref[...], so_ref[...], gate_ref[...].astype(F32), h0_ref[...], mk_ref[...], mv_ref[...],
                          woa[...], wo[...], wxq[...], wxo[...], g1[...], b1[...], g2[...], b2[...])
        r2_ref[...] = f["r2"]
        mixin_ref[...] = f["mixin"].astype(BF16)
        h1_ref[...] = f["h1"].astype(BF16)
        ox_ref[...] = f["ox"].astype(BF16)
        aout_ref[...] = f["a_out"]
        r1_ref[...] = f["r1"]
        qx_ref[...] = f["qxb"]

    s = o.shape[0]
    return _row_call("post_fwd", body, s, min(ROW_TILE, s), [o, s_out, gate, h0], [memk, memv, *lns],
                     [w_oa, w_o, w_xq, w_xo],
                     [(1024, F32), (1024, BF16), (1024, BF16), (1024, BF16), (1024, F32), (1024, F32), (1024, BF16)], [])


def _mlp(r2, target, g2, b2, g3, b3, w_up, w_down):
    cw = 1024
    n_chunk = MLP_HIDDEN // cw

    def body(r2_ref, tgt_ref, g2_ref, b2_ref, g3_ref, b3_ref, wup, wdn,
             dh2_ref, h2b_ref, dup_ref, act_ref, dff_ref, loss_ref, dg3_ref, db3_ref, up_scr):
        @pl.when(pl.program_id(0) == 0)
        def _():
            loss_ref[...] = jnp.zeros_like(loss_ref)
            dg3_ref[...] = jnp.zeros_like(dg3_ref)
            db3_ref[...] = jnp.zeros_like(db3_ref)

        h2, _, _ = _ln_fwd(r2_ref[...], g2_ref[...], b2_ref[...])
        h2b = h2.astype(BF16)
        h2b_ref[...] = h2b
        ff = jnp.zeros(h2.shape, F32)
        for c in range(n_chunk):
            sl = slice(c * cw, (c + 1) * cw)
            a = jnp.maximum(_mm(h2b, wup[:, sl]), 0.0)
            up_scr[:, sl] = a
            actb = (a * a).astype(BF16)
            act_ref[:, sl] = actb
            ff = ff + _mm(actb, wdn[sl, :])
        h3, xh3, rs3 = _ln_fwd(DN_ALPHA * h2 + ff, g3_ref[...], b3_ref[...])
        err = h3 - tgt_ref[...]
        loss_ref[...] += 0.5 * jnp.sum(err * err) * (1.0 / D_MODEL)
        dr3, dg3, db3 = _ln_bwd(err * (1.0 / D_MODEL), xh3, rs3, g3_ref[...])
        dg3_ref[...] += dg3
        db3_ref[...] += db3
        dffb = dr3.astype(BF16)
        dff_ref[...] = dffb
        dh2 = DN_ALPHA * dr3
        for c in range(n_chunk):
            sl = slice(c * cw, (c + 1) * cw)
            dupb = (_mm_nt(dffb, wdn[sl, :]) * (2.0 * up_scr[:, sl])).astype(BF16)
            dup_ref[:, sl] = dupb
            dh2 = dh2 + _mm_nt(dupb, wup[:, sl])
        dh2_ref[...] = dh2

    s = r2.shape[0]
    ts = min(ROW_TILE, s)
    return _row_call("mlp", body, s, ts, [r2, target], [g2, b2, g3, b3], [w_up, w_down],
                     [(1024, F32), (1024, BF16), (MLP_HIDDEN, BF16), (MLP_HIDDEN, BF16), (1024, BF16)],
                     [(8, 128), (1, 1024), (1, 1024)], scratch=[pltpu.VMEM((ts, MLP_HIDDEN), F32)])


def _post_bwd(dh2, o, s_out, gate, a_out, r1, r2, qx, memk, memv, lns, w_oa, w_o, w_xq, w_xo):
    def body(dh2_ref, o_ref, so_ref, gate_ref, aout_ref, r1_ref, r2_ref, qx_ref, mk_ref, mv_ref, g1, b1, g2, b2,
             woa, wo, wxq, wxo,
             do_ref, dl_ref, dso_ref, dgate_ref, dh0_ref, daout_ref, dmix_ref, dqx_ref, dxa_ref,
             dmk_ref, dmv_ref, dg1_ref, db1_ref, dg2_ref, db2_ref):
        @pl.when(pl.program_id(0) == 0)
        def _():
            for r in (dmk_ref, dmv_ref, dg1_ref, db1_ref, dg2_ref, db2_ref):
                r[...] = jnp.zeros_like(r)

        o = o_ref[...]
        s_out = so_ref[...]
        memk, memv = mk_ref[...], mv_ref[...]
        qxb = qx_ref[...]
        _, xh1, rs1 = _ln_fwd(r1_ref[...], g1[...], b1[...])
        _, xh2, rs2 = _ln_fwd(r2_ref[...], g2[...], b2[...])
        dr2, dg2, db2 = _ln_bwd(dh2_ref[...], xh2, rs2, g2[...])
        dg2_ref[...] += dg2
        db2_ref[...] += db2
        dxab = dr2.astype(BF16)
        dxa_ref[...] = dxab
        dox = _mm_nt(dxab, wxo[...])
        dqs = []
        for hh in range(XATTN_HEADS):
            sl = slice(hh * XATTN_HD, (hh + 1) * XATTN_HD)
            p = _xattn_probs(qxb, memk, hh)
            doxh = dox[:, sl].astype(BF16)
            dp = _mm_nt(doxh, memv[:, sl])
            ds = (p * (dp - jnp.sum(dp * p, axis=1, keepdims=True)) * (XATTN_HD ** -0.5)).astype(BF16)
            dqs.append(_mm(ds, memk[:, sl]))
            dmk_ref[:, sl] += _mm_tn(ds, qxb[:, sl])
            dmv_ref[:, sl] += _mm_tn(p, doxh)
        dqxb = jnp.concatenate(dqs, axis=1).astype(BF16)
        dqx_ref[...] = dqxb
        dh1 = DN_ALPHA * dr2 + _mm_nt(dqxb, wxq[...])
        dr1, dg1, db1 = _ln_bwd(dh1, xh1, rs1, g1[...])
        dg1_ref[...] += dg1
        db1_ref[...] += db1
        dh0_ref[...] = DN_ALPHA * dr1
        dmixb = dr1.astype(BF16)
        dmix_ref[...] = dmixb
        dmixin = _mm_nt(dmixb, wo[...])
        sg_s = _sigmoid(gate_ref[:, :1024].astype(F32))
        sg_a = _sigmoid(gate_ref[:, 1024:].astype(F32))
        dso_ref[...] = dmixin * sg_s
        daoutb = (dmixin * sg_a).astype(BF16)
        daout_ref[...] = daoutb
        dgate_ref[:, :1024] = (dmixin * s_out * sg_s * (1.0 - sg_s)).astype(BF16)
        dgate_ref[:, 1024:] = (dmixin * aout_ref[...] * sg_a * (1.0 - sg_a)).astype(BF16)
        d_o = _mm_nt(daoutb, woa[...])
        do_ref[...] = d_o.astype(BF16)
        lane = lax.broadcasted_iota(jnp.int32, (o.shape[0], HEAD_PAD), 1)
        dlc = jnp.zeros((o.shape[0], HEAD_PAD), F32)
        for hh in range(MLA_HEADS):
            sl = slice(hh * HEAD_PAD, (hh + 1) * HEAD_PAD)
            dl = jnp.sum(d_o[:, sl] * o[:, sl], axis=1, keepdims=True)
            dlc = dlc + jnp.where(lane == hh, dl, 0.0)
        dl_ref[...] = dlc

    s = o.shape[0]
    m = memk.shape[0]
    return _row_call(
        "post_bwd", body, s, min(ROW_TILE, s), [dh2, o, s_out, gate, a_out, r1, r2, qx], [memk, memv, *lns],
        [w_oa, w_o, w_xq, w_xo],
        [(1024, BF16), (HEAD_PAD, F32), (1024, F32), (2048, BF16), (1024, F32),
         (1024, BF16), (1024, BF16), (1024, BF16), (1024, BF16)],
        [(m, 1024), (m, 1024), (1, 1024), (1, 1024), (1, 1024), (1, 1024)])


def _s5out_bwd(d_so, yl, z3, d_skip, w_c, w_glu):
    def body(dso_ref, yl_ref, z3_ref, d_ref, wc, wglu, gh_ref, dup_ref, yg_ref, dy12_ref, dyl_ref, dd_ref):
        @pl.when(pl.program_id(0) == 0)
        def _():
            dd_ref[...] = jnp.zeros_like(dd_ref)

        yl = yl_ref[...]
        ygb = _gelu(yl).astype(BF16)
        yg_ref[...] = ygb
        y12 = _mm(ygb, wglu[...])
        sg = _sigmoid(y12[:, 1024:])
        dso = dso_ref[...]
        dy12b = jnp.concatenate([dso * sg, dso * y12[:, :1024] * sg * (1.0 - sg)], axis=1).astype(BF16)
        dy12_ref[...] = dy12b
        dyl = _mm_nt(dy12b, wglu[...]) * _gelu_grad(yl)
        dylb = dyl.astype(BF16)
        dyl_ref[...] = dylb
        gh_ref[...] = _mm_nt(dylb, wc[...])
        dup_ref[...] = dyl * d_ref[...]
        dd_ref[...] += jnp.sum(dyl * z3_ref[:, :256], axis=0, keepdims=True)

    s = d_so.shape[0]
    return _row_call("s5out_bwd", body, s, min(ROW_TILE, s), [d_so, yl, z3], [d_skip], [w_c, w_glu],
                     [(2 * NS, F32), (256, F32), (256, BF16), (2048, BF16), (256, BF16)], [(1, 256)])


def _in_bwd(x, tab, z3, dq, dk, dv, lam, du_p, dgate, dh0p, ln_g, ln_b, qg, kvg, w_inx, w_b, w_q2, w_kv2):
    def body(x_ref, tab_ref, z3_ref, dq_ref, dk_ref, dv_ref, lam_ref, dup_ref, dgate_ref, dh0p_ref,
             lng_ref, lnb_ref, qg_ref, kvg_ref, winx, wb, wq2, wkv2,
             dx_ref, h0b_ref, dz_ref, ub_ref, cqn_ref, ckvn_ref, dq2_ref, dkv2_ref,
             dlng_ref, dlnb_ref, dqg_ref, dkvg_ref):
        @pl.when(pl.program_id(0) == 0)
        def _():
            for r in (dlng_ref, dlnb_ref, dqg_ref, dkvg_ref):
                r[...] = jnp.zeros_like(r)

        h0, xh0, rs0 = _ln_fwd(x_ref[...], lng_ref[...], lnb_ref[...])
        h0b_ref[...] = h0.astype(BF16)
        z3 = z3_ref[...]
        ub_ref[...] = z3[:, :256].astype(BF16)
        tab_v = tab_ref[...]
        cq1, cq2, ck1 = tab_v[:, :128], tab_v[:, 128:], tab_v[:, :128]
        du = _mm_nt(lam_ref[...], wb[...]) + dup_ref[...]
        dq = dq_ref[...].astype(F32)
        dq2b = jnp.concatenate([dq * _tile_heads(cq1), dq * _tile_heads(cq2)], axis=1).astype(BF16)
        dq2_ref[...] = dq2b
        cqg, cqn, rq = _rms_fwd(z3[:, 256:512], qg_ref[...])
        cqn_ref[...] = cqg.astype(BF16)
        dcq, dqg = _rms_bwd(_mm_nt(dq2b, wq2[...]), cqn, rq, qg_ref[...])
        dqg_ref[...] += dqg
        dkv2b = jnp.concatenate([dk_ref[...], dv_ref[...]], axis=1)
        dk = dk_ref[...].astype(F32)
        dkv2_ref[...] = dkv2b
        ckvg, ckvn, rkv = _rms_fwd(z3[:, 512:768], kvg_ref[...])
        ckvn_ref[...] = ckvg.astype(BF16)
        dckv, dkvg = _rms_bwd(_mm_nt(dkv2b, wkv2[...]), ckvn, rkv, kvg_ref[...])
        dkvg_ref[...] += dkvg
        dkrp = dk[:, :HEAD_PAD]
        for hh in range(1, MLA_HEADS):
            dkrp = dkrp + dk[:, hh * HEAD_PAD:(hh + 1) * HEAD_PAD]
        dzb = jnp.concatenate([a.astype(BF16) for a in (du, dcq, dckv, dkrp * ck1, dkrp * cq2)] + [dgate_ref[...]],
                              axis=1)
        dz_ref[...] = dzb
        dh0 = _mm_nt(dzb, winx[...]) + dh0p_ref[...]
        dx, dg, db = _ln_bwd(dh0, xh0, rs0, lng_ref[...])
        dx_ref[...] = dx
        dlng_ref[...] += dg
        dlnb_ref[...] += db

    s = x.shape[0]
    return _row_call(
        "in_bwd", body, s, min(ROW_TILE, s), [x, tab, z3, dq, dk, dv, lam, du_p, dgate, dh0p],
        [ln_g, ln_b, qg, kvg], [w_inx, w_b, w_q2, w_kv2],
        [(1024, F32), (1024, BF16), (3072, BF16), (256, BF16), (256, BF16), (256, BF16), (2048, BF16), (2048, BF16)],
        [(1, 1024), (1, 1024), (1, 256), (1, 256)])


def _wgrad(name, xs, dy, out_dtype=F32, blocks=1, after=None):
    s, k = xs.shape
    n = dy.shape[1]
    nb = n // blocks
    tk, tn = min(k, 1024), min(n, 1024)
    ts = min(s, 2048 if tk <= 256 else 1024)
    per_tile = tn // nb
    assert s % ts == 0 and k % tk == 0 and n % tn == 0 and (blocks == 1 or tn % nb == 0), (name, s, k, n)
    last = s // ts - 1

    def body(x_ref, dy_ref, *rest):
        o_ref, acc = rest[-2:]

        @pl.when(pl.program_id(2) == 0)
        def _():
            acc[...] = jnp.zeros_like(acc)

        acc[...] += _mm_tn(x_ref[...], dy_ref[...])

        @pl.when(pl.program_id(2) == last)
        def _():
            if blocks == 1:
                o_ref[...] = acc[...].astype(out_dtype)
            else:
                for d in range(per_tile):
                    o_ref[d] = acc[:, d * nb:(d + 1) * nb].astype(out_dtype)

    if blocks == 1:
        out_shape = jax.ShapeDtypeStruct((k, n), out_dtype)
        out_spec = pl.BlockSpec((tk, tn), lambda a, b, c: (a, b))
    else:
        out_shape = jax.ShapeDtypeStruct((blocks, k, nb), out_dtype)
        out_spec = pl.BlockSpec((per_tile, tk, nb), lambda a, b, c: (b, a, 0))
    return pl.pallas_call(
        body, name=name, grid=(k // tk, n // tn, s // ts),
        in_specs=[pl.BlockSpec((ts, tk), lambda a, b, c: (c, a)), pl.BlockSpec((ts, tn), lambda a, b, c: (c, b))]
        + ([] if after is None else [pl.BlockSpec(memory_space=pl.ANY)]),
        out_specs=out_spec, out_shape=out_shape, scratch_shapes=[pltpu.VMEM((tk, tn), F32)],
        compiler_params=pltpu.CompilerParams(dimension_semantics=("arbitrary", "arbitrary", "arbitrary"),
                                             vmem_limit_bytes=VMEM_LIMIT),
    )(xs, dy, *([] if after is None else [after]))


def _mesh_pos():
    x, y, c = lax.axis_index("x"), lax.axis_index("y"), lax.axis_index("c")
    return x, y, c


def _peer(x, y, c, k):
    px = 1 - x if k & 4 else x
    py = 1 - y if k & 2 else y
    pc = 1 - c if k & 1 else c
    return (px, py, pc), 4 * px + 2 * py + pc


def _all_to_all(name, arrays, gather):
    n = len(arrays)

    def body(*refs):
        src, dst = refs[:n], refs[n:2 * n]
        send_sems, recv_sems, local_sems = refs[2 * n:]
        x, y, c = _mesh_pos()
        me = 4 * x + 2 * y + c

        def block(i, d):
            return src[i] if gather else src[i].at[d]

        own = [pltpu.make_async_copy(block(i, me), dst[i].at[me], local_sems.at[i]) for i in range(n)]
        for cp in own:
            cp.start()
        sends = []
        for k in range(1, N_DEV):
            peer, pid = _peer(x, y, c, k)
            for i in range(n):
                idx = (k - 1) * n + i
                cp = pltpu.make_async_remote_copy(src_ref=block(i, pid), dst_ref=dst[i].at[me],
                                                  send_sem=send_sems.at[idx], recv_sem=recv_sems.at[idx],
                                                  device_id=peer, device_id_type=MESH)
                cp.start()
                sends.append(cp)
        for k in range(1, N_DEV):
            peer, pid = _peer(x, y, c, k)
            for i in range(n):
                idx = (k - 1) * n + i
                pltpu.make_async_remote_copy(src_ref=block(i, pid), dst_ref=dst[i].at[pid],
                                             send_sem=send_sems.at[idx], recv_sem=recv_sems.at[idx],
                                             device_id=peer, device_id_type=MESH).wait_recv()
        for cp in sends:
            cp.wait_send()
        for cp in own:
            cp.wait()

    n_sem = n * (N_DEV - 1)
    return pl.pallas_call(
        body, name=name,
        in_specs=[pl.BlockSpec(memory_space=pl.ANY)] * n,
        out_specs=[pl.BlockSpec(memory_space=pl.ANY)] * n,
        out_shape=[jax.ShapeDtypeStruct((N_DEV,) + a.shape[-2:], a.dtype) for a in arrays],
        scratch_shapes=[pltpu.SemaphoreType.DMA((n_sem,)), pltpu.SemaphoreType.DMA((n_sem,)),
                        pltpu.SemaphoreType.DMA((n,))],
    )(*arrays)


_HBM = pl.BlockSpec(memory_space=pltpu.HBM)
_SEM = pl.BlockSpec(memory_space=pltpu.SEMAPHORE)


def _exchange_start(name, arrays, gather, after=None):
    n = len(arrays)
    n_sem = n * (N_DEV - 1)
    me = 4 * lax.axis_index("x") + 2 * lax.axis_index("y") + lax.axis_index("c")
    lands = []
    for a in arrays:
        own = a[None] if gather else lax.dynamic_slice_in_dim(a, me, 1, 0)
        lands.append(lax.dynamic_update_slice(lax.empty((N_DEV,) + a.shape[-2:], a.dtype), own, (me, 0, 0)))
    n_after = 0 if after is None else 1

    def body(*refs):
        src, land = refs[:n], refs[n:2 * n]
        send_sems, recv_sems = refs[2 * n + n_after], refs[2 * n + n_after + 1]
        token = refs[-1]
        x, y, c = _mesh_pos()
        me_in = 4 * x + 2 * y + c
        for k in range(1, N_DEV):
            peer, pid = _peer(x, y, c, k)
            for i in range(n):
                idx = (k - 1) * n + i
                pltpu.make_async_remote_copy(src_ref=src[i] if gather else src[i].at[pid], dst_ref=land[i].at[me_in],
                                             send_sem=send_sems.at[idx], recv_sem=recv_sems.at[idx],
                                             device_id=peer, device_id_type=MESH).start()
        token[...] = jnp.zeros_like(token)

    operands = [pltpu.with_memory_space_constraint(a, pltpu.HBM) for a in list(arrays) + lands]
    outs = pl.pallas_call(
        body, name=name,
        out_shape=(pltpu.SemaphoreType.DMA((n_sem,)), pltpu.SemaphoreType.DMA((n_sem,)),
                   *[pltpu.HBM(a.shape, a.dtype) for a in list(arrays) + lands],
                   jax.ShapeDtypeStruct((8, 128), F32)),
        in_specs=[_HBM] * (2 * n) + [pl.BlockSpec(memory_space=pl.ANY)] * n_after,
        out_specs=(_SEM, _SEM, *[_HBM] * (2 * n), pl.BlockSpec(memory_space=pltpu.VMEM)),
        input_output_aliases={i: 2 + i for i in range(2 * n)},
        compiler_params=pltpu.CompilerParams(has_side_effects=pltpu.SideEffectType.DATAFLOW_SIDE_EFFECTING),
    )(*operands, *([after] if n_after else []))
    return (gather, outs[0], outs[1], outs[2:2 + n], outs[2 + n:2 + 2 * n]), outs[-1]


def _exchange_wait(name, handle, after):
    gather, send_sems, recv_sems, srcs, lands = handle
    n = len(srcs)

    def body(*refs):
        src, land = refs[:n], refs[n:2 * n]
        s_sems, r_sems = refs[2 * n], refs[2 * n + 1]
        x, y, c = _mesh_pos()
        for k in range(1, N_DEV):
            peer, pid = _peer(x, y, c, k)
            for i in range(n):
                idx = (k - 1) * n + i
                cp = pltpu.make_async_remote_copy(src_ref=src[i] if gather else src[i].at[pid], dst_ref=land[i].at[pid],
                                                  send_sem=s_sems.at[idx], recv_sem=r_sems.at[idx],
                                                  device_id=peer, device_id_type=MESH)
                cp.wait_send()
                cp.wait_recv()

    outs = pl.pallas_call(
        body, name=name,
        out_shape=tuple(pltpu.HBM(a.shape, a.dtype) for a in list(srcs) + list(lands)),
        in_specs=[_HBM] * (2 * n) + [_SEM, _SEM, pl.BlockSpec(memory_space=pl.ANY)],
        out_specs=tuple([_HBM] * (2 * n)),
        input_output_aliases={i: i for i in range(2 * n)},
        compiler_params=pltpu.CompilerParams(has_side_effects=pltpu.SideEffectType.DATAFLOW_SIDE_EFFECTING),
    )(*srcs, *lands, send_sems, recv_sems, after)
    return list(outs[n:])


def _adamw(name, parts, w, m, v):
    a_rows, b_cols = w.shape
    ta = min(a_rows, ADAM_TILE)
    assert a_rows % ta == 0
    c1 = 1.0 - ADAM_B1 ** ADAM_STEP
    c2 = 1.0 - ADAM_B2 ** ADAM_STEP

    def body(p_ref, w_ref, m_ref, v_ref, g_ref, d_ref, mo_ref, vo_ref):
        g = p_ref[0].astype(F32)
        for d in range(1, N_DEV):
            g = g + p_ref[d].astype(F32)
        g_ref[...] = g
        mn = ADAM_B1 * m_ref[...] + (1.0 - ADAM_B1) * g
        vn = ADAM_B2 * v_ref[...] + (1.0 - ADAM_B2) * (g * g)
        mo_ref[...] = mn
        vo_ref[...] = vn
        d_ref[...] = -ADAM_LR * ((mn / c1) / (jnp.sqrt(vn / c2) + ADAM_EPS) + ADAM_WD * w_ref[...])

    row = pl.BlockSpec((ta, b_cols), lambda i: (i, 0))
    return pl.pallas_call(
        body, name=name, grid=(a_rows // ta,),
        in_specs=[pl.BlockSpec((N_DEV, ta, b_cols), lambda i: (0, i, 0)), row, row, row],
        out_specs=[row] * 4,
        out_shape=[jax.ShapeDtypeStruct((a_rows, b_cols), F32)] * 4,
        compiler_params=pltpu.CompilerParams(dimension_semantics=("arbitrary",), vmem_limit_bytes=VMEM_LIMIT),
    )(parts, w, m, v)


def _pack_rows(arrays, rows):
    flat = jnp.concatenate([a.reshape(-1) for a in arrays])
    return jnp.pad(flat, (0, rows * LANES - flat.shape[0])).reshape(rows, LANES)


def _cols_from_blocks(g):
    return g.transpose(1, 0, 2).reshape(g.shape[1], N_DEV * g.shape[2])


def _blocks_from_cols(w):
    return w.reshape(w.shape[0], N_DEV, w.shape[1] // N_DEV).transpose(1, 0, 2)


def _unpack_flat(flat2d, shapes):
    flat = flat2d.reshape(-1)
    out, off = [], 0
    for shp in shapes:
        sz = math.prod(shp)
        out.append(flat[off:off + sz].reshape(shp))
        off += sz
    return out


def _s5_mats(lam_re, lam_im, log_dt, b_re, b_im, c_re, c_im):
    lr = jnp.minimum(lam_re, S5_MAX_RE)
    li = lam_im
    dt = jnp.exp(log_dt)[:, None]
    mag = jnp.exp(lr * dt)
    ang = li * dt
    ab_re = mag * jnp.cos(ang)
    ab_im = mag * jnp.sin(ang)
    den = lr * lr + li * li
    nr = ab_re - 1.0
    f_re = ((nr * lr + ab_im * li) / den)[..., None]
    f_im = ((ab_im * lr - nr * li) / den)[..., None]
    bb_re = f_re * b_re - f_im * b_im
    bb_im = f_re * b_im + f_im * b_re
    eye = jnp.eye(S5_GROUPS, dtype=F32)
    a = jnp.stack([ab_re.reshape(-1), ab_im.reshape(-1)])
    wb = jnp.concatenate([jnp.einsum("gph,gk->ghkp", bb_re, eye).reshape(S5_WIDTH, NS),
                          jnp.einsum("gph,gk->ghkp", bb_im, eye).reshape(S5_WIDTH, NS)], axis=1)
    wc = jnp.concatenate([jnp.einsum("ghp,gk->gpkh", c_re, eye).reshape(NS, S5_WIDTH),
                          -jnp.einsum("ghp,gk->gpkh", c_im, eye).reshape(NS, S5_WIDTH)], axis=0)
    return a, wb, wc


def _power_table(a):
    ar, ai = a[0], a[1]
    rows_r, rows_i = [ar], [ai]
    for _ in range(7):
        pr, pi = rows_r[-1], rows_i[-1]
        rows_r.append(pr * ar - pi * ai)
        rows_i.append(pr * ai + pi * ar)
    return jnp.concatenate([jnp.stack(rows_r), jnp.stack(rows_i)], axis=1)


def _rope_table(positions):
    inv = ROPE_THETA ** (-jnp.arange(0, MLA_ROPE, 2, dtype=F32) / MLA_ROPE)
    inv128 = jnp.concatenate([jnp.zeros((MLA_NOPE,), F32), inv, inv, jnp.zeros((32,), F32)])
    sign = jnp.concatenate([jnp.zeros((MLA_NOPE,), F32), -jnp.ones((16,), F32), jnp.ones((16,), F32),
                            jnp.zeros((32,), F32)])
    ang = positions.astype(F32)[:, None] * inv128
    return jnp.concatenate([jnp.cos(ang), jnp.sin(ang) * sign], axis=1)


def _derived_weights(full):
    w_in = full["w_in"]
    k1, k2 = w_in[:, 768:784], w_in[:, 784:800]
    z64, z32 = jnp.zeros((1024, 64), BF16), jnp.zeros((1024, 32), BF16)
    w_inx = jnp.concatenate([w_in[:, :768], z64, k1, k2, z32, z64, k2, k1, z32, w_in[:, 800:]], axis=1)
    uq = full["w_uq"].reshape(256, MLA_HEADS, MLA_QK)
    nope, r1, r2 = uq[:, :, :64], uq[:, :, 64:80], uq[:, :, 80:]
    zq64, zq32 = jnp.zeros((256, MLA_HEADS, 64), BF16), jnp.zeros((256, MLA_HEADS, 32), BF16)
    w_q2 = jnp.concatenate([jnp.concatenate([nope, r1, r2, zq32], axis=2).reshape(256, 1024),
                            jnp.concatenate([zq64, r2, r1, zq32], axis=2).reshape(256, 1024)], axis=1)
    ukv = full["w_ukv"].reshape(256, MLA_HEADS, 128)
    w_kv2 = jnp.concatenate([jnp.concatenate([ukv[:, :, :64], zq64], axis=2).reshape(256, 1024),
                             jnp.concatenate([ukv[:, :, 64:], zq64], axis=2).reshape(256, 1024)], axis=1)
    return w_inx, w_q2, w_kv2


def _oa_padded(w_oa):
    oa = w_oa.reshape(MLA_HEADS, MLA_V, 1024)
    return jnp.concatenate([oa, jnp.zeros_like(oa)], axis=1).reshape(1024, 1024)


def _fold_w_in(d_inx):
    d_k1 = d_inx[:, 832:848] + d_inx[:, 976:992]
    d_k2 = d_inx[:, 848:864] + d_inx[:, 960:976]
    return jnp.concatenate([d_inx[:, :768], d_k1, d_k2, d_inx[:, 1024:]], axis=1)


def _fold_qkv(d_q2, d_kv2):
    a = d_q2[:, :1024].reshape(256, MLA_HEADS, 128)
    b = d_q2[:, 1024:].reshape(256, MLA_HEADS, 128)
    d_w_uq = jnp.concatenate([a[:, :, :64], a[:, :, 64:80] + b[:, :, 80:96], a[:, :, 80:96] + b[:, :, 64:80]],
                             axis=2).reshape(256, MLA_HEADS * MLA_QK)
    kk = d_kv2[:, :1024].reshape(256, MLA_HEADS, 128)
    vv = d_kv2[:, 1024:].reshape(256, MLA_HEADS, 128)
    d_w_ukv = jnp.concatenate([kk[:, :, :64], vv[:, :, :64]], axis=2).reshape(256, 1024)
    return d_w_uq, d_w_ukv


def kernel(x, mem, positions, ln_in_g, ln_in_b, w_in, s5_lam_re, s5_lam_im, s5_log_dt, s5_b_re, s5_b_im, s5_c_re, s5_c_im, s5_d, w_glu, q_norm_g, w_uq, kv_norm_g, w_ukv, w_oa, w_o, ln1_g, ln1_b, w_xq, w_xk, w_xv, w_xo, ln2_g, ln2_b, w_up, w_down, ln3_g, ln3_b, loss_target, m_ln_in_g, m_ln_in_b, m_w_in, m_s5_lam_re, m_s5_lam_im, m_s5_log_dt, m_s5_b_re, m_s5_b_im, m_s5_c_re, m_s5_c_im, m_s5_d, m_w_glu, m_q_norm_g, m_w_uq, m_kv_norm_g, m_w_ukv, m_w_oa, m_w_o, m_ln1_g, m_ln1_b, m_w_xq, m_w_xk, m_w_xv, m_w_xo, m_ln2_g, m_ln2_b, m_w_up, m_w_down, m_ln3_g, m_ln3_b, v_ln_in_g, v_ln_in_b, v_w_in, v_s5_lam_re, v_s5_lam_im, v_s5_log_dt, v_s5_b_re, v_s5_b_im, v_s5_c_re, v_s5_c_im, v_s5_d, v_w_glu, v_q_norm_g, v_w_uq, v_kv_norm_g, v_w_ukv, v_w_oa, v_w_o, v_ln1_g, v_ln1_b, v_w_xq, v_w_xk, v_w_xv, v_w_xo, v_ln2_g, v_ln2_b, v_w_up, v_w_down, v_ln3_g, v_ln3_b):
    args = dict(locals())
    wts = {n: args[n] for n in WEIGHTS}
    mom = {n: args["m_" + n] for n in WEIGHTS}
    vel = {n: args["v_" + n] for n in WEIGHTS}
    xs, mems, tgt = x[0], mem[0], loss_target[0]
    s = xs.shape[0]
    names_big = [n for (n, _, _, _) in SHARDED]
    kind = {n: kd for (n, kd, _, _) in SHARDED}
    n_small = sum(math.prod(wts[n].shape) for n in SMALL)
    small_rows = _round_up(n_small + 1, 8 * LANES) // LANES

    late = [n for n in names_big if n not in EARLY]
    shard = {n: wts[n][0].astype(BF16) for n in names_big}
    g_early = dict(zip(EARLY, _all_to_all("all_gather", [shard[n] for n in EARLY], True)))
    ag_handle, ag_token = _exchange_start("all_gather_late", [shard[n] for n in late], True, after=g_early["w_in"])
    full = {n: _cols_from_blocks(g_early[n]) for n in EARLY}
    w_inx, w_q2, w_kv2 = _derived_weights(full)
    s5_args = (s5_lam_re[0], s5_lam_im[0], s5_log_dt[0], s5_b_re[0], s5_b_im[0], s5_c_re[0], s5_c_im[0])
    (a_mat, w_b, w_c), s5_vjp = jax.vjp(_s5_mats, *s5_args)
    w_bb, w_cb = w_b.astype(BF16), w_c.astype(BF16)
    pw = _power_table(a_mat)
    pwb = _power_table(a_mat * jnp.array([[1.0], [-1.0]], F32))[::-1]
    tab = _rope_table(positions[0])
    row = lambda a: a.reshape(1, -1)
    ln_g, ln_b = row(ln_in_g) + ag_token[0:1, 0:1], row(ln_in_b)
    lns = [ln1_g, ln1_b, ln2_g, ln2_b]

    h0, z3, gate, bu, q, k, v, kt, vt = _in_fwd(xs, tab, ln_g, ln_b, q_norm_g, kv_norm_g, w_inx, w_bb, w_q2, w_kv2)
    h = _scan_fwd(pw, bu)
    yl, s_out = _s5out_fwd(h, z3, s5_d, w_cb, full["w_glu"])
    o, lse_t = _attn_fwd(q, k, vt)
    g_late = dict(zip(late, _exchange_wait("all_gather_late_wait", ag_handle, after=lse_t)))
    full.update({n: g_late[n].reshape(-1, g_late[n].shape[2]) if kind[n] == "row" else _cols_from_blocks(g_late[n])
                 for n in late})
    w_oap = _oa_padded(full["w_oa"])
    memk, memv = _mem_kv(mems, full["w_xk"], full["w_xv"])
    r2, mixin_b, h1_b, ox_b, a_out, r1, qx_b = _post_fwd(o, s_out, gate, h0, memk, memv, lns,
                                        w_oap, full["w_o"], full["w_xq"], full["w_xo"])
    dh2, h2_b, dup_b, act_b, dff_b, loss_acc, d_ln3_g, d_ln3_b = _mlp(
        r2, tgt, ln2_g, ln2_b, ln3_g, ln3_b, full["w_up"], full["w_down"])
    (do_b, delta, d_so, dgate, dh0p, daout_b, dmix_b, dqx_b, dxa_b,
     dmk, dmv, d_ln1_g, d_ln1_b, d_ln2_g, d_ln2_b) = _post_bwd(
        dh2, o, s_out, gate, a_out, r1, r2, qx_b, memk, memv, lns, w_oap, full["w_o"], full["w_xq"], full["w_xo"])
    rows8 = lambda g: g.reshape(N_DEV, g.shape[0] // N_DEV, g.shape[1])
    d_w_oa = _wgrad("wg_oa", o, daout_b).reshape(MLA_HEADS, 128, 1024)[:, :64].reshape(512, 1024)
    send = {
        "w_oa": _blocks_from_cols(d_w_oa).astype(BF16),
        "w_up": _wgrad("wg_up", h2_b, dup_b, BF16, N_DEV),
        "w_o": rows8(_wgrad("wg_o", mixin_b, dmix_b, BF16)),
        "w_xq": rows8(_wgrad("wg_xq", h1_b, dqx_b, BF16)),
        "w_xk": rows8(_wgrad("wg_xk", mems, dmk, BF16)),
        "w_xv": rows8(_wgrad("wg_xv", mems, dmv, BF16)),
        "w_xo": rows8(_wgrad("wg_xo", ox_b, dxa_b, BF16)),
        "w_down": rows8(_wgrad("wg_down", act_b, dff_b, BF16)),
    }
    rs_handle, rs_token = _exchange_start("grad_exchange_late", [send[n] for n in late], False)
    dl_t = delta.T[:MLA_HEADS].reshape(MLA_HEADS, 1, s) + rs_token[0, 0]
    dq, dk, dv = _attn_bwd(q, k, v, kt, do_b, lse_t, dl_t)
    gh, du_p, yg_b, dy12_b, dyl_b, d_s5_d = _s5out_bwd(d_so, yl, z3, s5_d, w_cb, full["w_glu"])
    lam, d_a = _scan_bwd(pwb, gh, h)
    (dx, h0_b, dz_b, u_b, cqn_b, ckvn_b, dq2_b, dkv2_b, d_ln_g, d_ln_b, d_qg, d_kvg) = _in_bwd(
        xs, tab, z3, dq, dk, dv, lam, du_p, dgate, dh0p, ln_g, ln_b, q_norm_g, kv_norm_g, w_inx, w_bb, w_q2, w_kv2)

    d_w_in = _fold_w_in(_wgrad("wg_in", h0_b, dz_b))
    win_handle, win_token = _exchange_start("grad_exchange_w_in", [_blocks_from_cols(d_w_in).astype(BF16)], False)
    d_w_uq, d_w_ukv = _fold_qkv(_wgrad("wg_q", cqn_b, dq2_b, after=win_token), _wgrad("wg_kv", ckvn_b, dkv2_b))
    send.update({
        "w_uq": _blocks_from_cols(d_w_uq).astype(BF16), "w_ukv": _blocks_from_cols(d_w_ukv).astype(BF16),
        "w_glu": _wgrad("wg_glu", yg_b, dy12_b, BF16, N_DEV, after=win_token),
    })
    d_s5 = s5_vjp((d_a.reshape(2, NS), _wgrad("wg_s5b", u_b, lam, after=win_token),
                   _wgrad("wg_s5c", h, dyl_b, after=win_token)))
    small_grads = {
        "ln_in_g": d_ln_g, "ln_in_b": d_ln_b, "s5_lam_re": d_s5[0], "s5_lam_im": d_s5[1], "s5_log_dt": d_s5[2],
        "s5_b_re": d_s5[3], "s5_b_im": d_s5[4], "s5_c_re": d_s5[5], "s5_c_im": d_s5[6], "s5_d": d_s5_d,
        "q_norm_g": d_qg, "kv_norm_g": d_kvg, "ln1_g": d_ln1_g, "ln1_b": d_ln1_b, "ln2_g": d_ln2_g,
        "ln2_b": d_ln2_b, "ln3_g": d_ln3_g, "ln3_b": d_ln3_b,
    }
    small_send = jnp.broadcast_to(_pack_rows([small_grads[n] for n in SMALL] + [loss_acc[0, :1]], small_rows)[None],
                                  (N_DEV, small_rows, LANES))

    early_rest = [n for n in EARLY if n != "w_in"]
    recv_early = _all_to_all("grad_exchange", [send[n] for n in early_rest] + [small_send], False)
    recv_late = _exchange_wait("grad_exchange_late_wait", rs_handle, after=recv_early[-1])
    recv = dict(zip(early_rest + late, list(recv_early[:-1]) + recv_late))
    results = [dict(), dict(), dict(), dict()]
    for n in late + early_rest + ["w_in"]:
        if n == "w_in":
            recv[n] = _exchange_wait("grad_exchange_w_in_wait", win_handle, after=results[0][early_rest[-1]])[0]
        parts = recv[n]
        outs = _adamw("adamw_" + n, parts, wts[n][0], mom[n][0], vel[n][0])
        for res, a in zip(results, outs):
            res[n] = a[None]
    small_out = _adamw("adamw_small", recv_early[-1],
                       *[_pack_rows([t[n] for n in SMALL], small_rows) for t in (wts, mom, vel)])
    for res, fs in zip(results, small_out):
        for n, a in zip(SMALL, _unpack_flat(fs, [wts[n].shape for n in SMALL])):
            res[n] = a

    loss = small_out[0].reshape(-1)[n_small]
    return (loss, dx[None], *[res[n] for res in results for n in WEIGHTS])
```

```python
import math

import jax
import jax.numpy as jnp
from jax import lax
from jax.experimental import pallas as pl
from jax.experimental.pallas import tpu as pltpu

F32 = jnp.float32
BF16 = jnp.bfloat16

D_MODEL = 1024
S5_WIDTH = 256
S5_GROUP_CH = 16
S5_GROUPS = 16
S5_STATE = 64
NS = S5_GROUPS * S5_STATE
S5_MAX_RE = -1e-4
MLA_HEADS = 8
MLA_NOPE = 64
MLA_ROPE = 32
MLA_QK = 96
MLA_V = 64
HEAD_PAD = 128
ROPE_THETA = 10000.0
XATTN_HEADS = 4
XATTN_HD = 256
MLP_HIDDEN = 4096
LN_EPS = 1e-5
RMS_EPS = 1e-6
NEG_INF = -1e30
LOG2E = 1.4426950408889634
DN_ALPHA = 2.0 ** 0.25
ADAM_LR = 0.001
ADAM_B1 = 0.9
ADAM_B2 = 0.999
ADAM_EPS = 1e-08
ADAM_WD = 0.01
ADAM_STEP = 10

N_DEV = 8
MESH = pl.DeviceIdType.MESH
LANES = 1024
VMEM_LIMIT = 60 * 1024 * 1024

ROW_TILE = 256
SCAN_TILE = 1024
ATT_TILE = 512
ADAM_TILE = 128

SHARDED = (
    ("w_in", "col", 1024, 2848), ("w_glu", "col", 256, 2048), ("w_uq", "col", 256, 768),
    ("w_ukv", "col", 256, 1024), ("w_oa", "col", 512, 1024), ("w_o", "row", 1024, 1024),
    ("w_xq", "row", 1024, 1024), ("w_xk", "row", 1024, 1024), ("w_xv", "row", 1024, 1024),
    ("w_xo", "row", 1024, 1024), ("w_up", "col", 1024, 4096), ("w_down", "row", 4096, 1024),
)
EARLY = ("w_in", "w_glu", "w_uq", "w_ukv")
SMALL = ("ln_in_g", "ln_in_b", "s5_lam_re", "s5_lam_im", "s5_log_dt", "s5_b_re", "s5_b_im", "s5_c_re",
         "s5_c_im", "s5_d", "q_norm_g", "kv_norm_g", "ln1_g", "ln1_b", "ln2_g", "ln2_b", "ln3_g", "ln3_b")
WEIGHTS = ("ln_in_g", "ln_in_b", "w_in", "s5_lam_re", "s5_lam_im", "s5_log_dt", "s5_b_re", "s5_b_im",
           "s5_c_re", "s5_c_im", "s5_d", "w_glu", "q_norm_g", "w_uq", "kv_norm_g", "w_ukv", "w_oa", "w_o",
           "ln1_g", "ln1_b", "w_xq", "w_xk", "w_xv", "w_xo", "ln2_g", "ln2_b", "w_up", "w_down", "ln3_g", "ln3_b")


def _round_up(n, m):
    return (n + m - 1) // m * m


def _bf(a):
    return a.astype(BF16)


def _mm(a, b):
    return jnp.dot(_bf(a), _bf(b), preferred_element_type=F32)


def _mm_nt(a, b):
    return lax.dot_general(_bf(a), _bf(b), (((1,), (1,)), ((), ())), preferred_element_type=F32)


def _mm_tn(a, b):
    return lax.dot_general(_bf(a), _bf(b), (((0,), (0,)), ((), ())), preferred_element_type=F32)


def _sigmoid(a):
    return 1.0 / (1.0 + jnp.exp(-a))


def _gelu(a):
    return 0.5 * a * (1.0 + lax.erf(a * (2.0 ** -0.5)))


def _gelu_grad(a):
    return 0.5 * (1.0 + lax.erf(a * (2.0 ** -0.5))) + a * jnp.exp(-0.5 * a * a) * (1.0 / math.sqrt(2.0 * math.pi))


def _ln_fwd(a, g, b):
    mu = jnp.mean(a, axis=-1, keepdims=True)
    ac = a - mu
    var = jnp.mean(ac * ac, axis=-1, keepdims=True)
    rstd = lax.rsqrt(var + LN_EPS)
    xhat = ac * rstd
    return xhat * g + b, xhat, rstd


def _ln_bwd(dy, xhat, rstd, g):
    dxh = dy * g
    m1 = jnp.mean(dxh, axis=-1, keepdims=True)
    m2 = jnp.mean(dxh * xhat, axis=-1, keepdims=True)
    dx = rstd * (dxh - m1 - xhat * m2)
    return dx, jnp.sum(dy * xhat, axis=0, keepdims=True), jnp.sum(dy, axis=0, keepdims=True)


def _rms_fwd(a, g):
    r = lax.rsqrt(jnp.mean(a * a, axis=-1, keepdims=True) + RMS_EPS)
    xn = a * r
    return xn * g, xn, r


def _rms_bwd(dy, xn, r, g):
    dxn = dy * g
    dx = r * (dxn - xn * jnp.mean(dxn * xn, axis=-1, keepdims=True))
    return dx, jnp.sum(dy * xn, axis=0, keepdims=True)


def _tile_heads(a):
    return jnp.concatenate([a] * MLA_HEADS, axis=1)


def _row_call(name, body, n_rows, ts, tiled_in, full_in, weights, tiled_out, acc_out, reverse=False, scratch=(),
              tiled_out_t=()):
    n = n_rows // ts
    assert n * ts == n_rows, (name, n_rows, ts)
    nt, nf, nw = len(tiled_in), len(full_in), len(weights)
    nto, nao = len(tiled_out) + len(tiled_out_t), len(acc_out)
    if reverse:
        imap = lambda i: (n - 1 - i, 0)
    else:
        imap = lambda i: (i, 0)
    const = lambda i: (0, 0)

    def kern(*refs):
        ins = refs[:nt + nf]
        w_hbm = refs[nt + nf:nt + nf + nw]
        outs = refs[nt + nf + nw:nt + nf + nw + nto + nao]
        scr = refs[nt + nf + nw + nto + nao:]
        w_vmem = scr[:nw]
        extra = scr[nw + 1:] if nw else scr
        if nw:
            sem = scr[nw]

            @pl.when(pl.program_id(0) == 0)
            def _():
                cps = [pltpu.make_async_copy(w_hbm[k], w_vmem[k], sem.at[k]) for k in range(nw)]
                for cp in cps:
                    cp.start()
                for cp in cps:
                    cp.wait()
        body(*ins, *w_vmem, *outs, *extra)

    in_specs = [pl.BlockSpec((ts, a.shape[1]), imap) for a in tiled_in]
    in_specs += [pl.BlockSpec(a.shape, const) for a in full_in]
    in_specs += [pl.BlockSpec(memory_space=pl.ANY) for _ in weights]
    assert not (reverse and tiled_out_t)
    out_shape = [jax.ShapeDtypeStruct((n_rows, c), dt) for (c, dt) in tiled_out]
    out_shape += [jax.ShapeDtypeStruct((c, n_rows), dt) for (c, dt) in tiled_out_t]
    out_shape += [jax.ShapeDtypeStruct(shp, F32) for shp in acc_out]
    out_specs = [pl.BlockSpec((ts, c), imap) for (c, dt) in tiled_out]
    out_specs += [pl.BlockSpec((c, ts), lambda i: (0, i)) for (c, dt) in tiled_out_t]
    out_specs += [pl.BlockSpec(shp, const) for shp in acc_out]
    scratch_shapes = [pltpu.VMEM(w.shape, w.dtype) for w in weights]
    if nw:
        scratch_shapes.append(pltpu.SemaphoreType.DMA((nw,)))
    scratch_shapes += list(scratch)
    return pl.pallas_call(
        kern, name=name, grid=(n,), in_specs=in_specs, out_specs=out_specs, out_shape=out_shape,
        scratch_shapes=scratch_shapes,
        compiler_params=pltpu.CompilerParams(dimension_semantics=("arbitrary",), vmem_limit_bytes=VMEM_LIMIT),
    )(*tiled_in, *full_in, *weights)


def _mem_kv(mem, w_xk, w_xv):
    m = mem.shape[0]

    def body(mem_ref, wk_ref, wv_ref, k_ref, v_ref):
        mb = mem_ref[...]
        k_ref[...] = _mm(mb, wk_ref[...]).astype(BF16)
        v_ref[...] = _mm(mb, wv_ref[...]).astype(BF16)

    return pl.pallas_call(
        body, name="mem_kv",
        out_shape=[jax.ShapeDtypeStruct((m, D_MODEL), BF16)] * 2,
        compiler_params=pltpu.CompilerParams(vmem_limit_bytes=VMEM_LIMIT),
    )(mem, w_xk, w_xv)


def _in_fwd(x, tab, ln_g, ln_b, qg, kvg, w_inx, w_b, w_q2, w_kv2):
    def body(x_ref, tab_ref, lng_ref, lnb_ref, qg_ref, kvg_ref, winx, wb, wq2, wkv2,
             h0_ref, z3_ref, gate_ref, bu_ref, q_ref, k_ref, v_ref, kt_ref, vt_ref):
        h0, _, _ = _ln_fwd(x_ref[...], lng_ref[...], lnb_ref[...])
        h0_ref[...] = h0
        z = _mm(h0, winx[...])
        z3_ref[...] = z[:, :768]
        gate_ref[...] = z[:, 1024:].astype(BF16)
        bu_ref[...] = _mm(z[:, :256], wb[...])
        tab_v = tab_ref[...]
        cq1, cq2, ck1 = tab_v[:, :128], tab_v[:, 128:], tab_v[:, :128]
        cqn, _, _ = _rms_fwd(z[:, 256:512], qg_ref[...])
        q2 = _mm(cqn, wq2[...])
        q_ref[...] = (q2[:, :1024] * _tile_heads(cq1) + q2[:, 1024:] * _tile_heads(cq2)).astype(BF16)
        ckvn, _, _ = _rms_fwd(z[:, 512:768], kvg_ref[...])
        kv2 = _mm(ckvn, wkv2[...])
        krp = z[:, 768:896] * ck1 + z[:, 896:1024] * cq2
        kf = kv2[:, :1024] + _tile_heads(krp)
        k_ref[...] = kf.astype(BF16)
        v_ref[...] = kv2[:, 1024:].astype(BF16)
        kt_ref[...] = kf.T.astype(BF16)
        vt_ref[...] = kv2[:, 1024:].T.astype(BF16)

    s = x.shape[0]
    return _row_call(
        "in_fwd", body, s, min(ROW_TILE, s), [x, tab], [ln_g, ln_b, qg, kvg], [w_inx, w_b, w_q2, w_kv2],
        [(1024, F32), (768, F32), (2048, BF16), (2048, F32), (1024, BF16), (1024, BF16), (1024, BF16)], [],
        tiled_out_t=[(1024, BF16), (1024, BF16)])


def _scan_coeffs(pw_ref, rows, forward):
    row = lax.broadcasted_iota(jnp.int32, (8, NS), 0)
    out = []
    for k, r in zip((1, 2, 4), rows):
        keep = (row >= k) if forward else (row < 8 - k)
        out.append((jnp.where(keep, pw_ref[r:r + 1, :NS], 0.0), jnp.where(keep, pw_ref[r:r + 1, NS:], 0.0), k))
    return out


def _scan_level(xr, xi, coeff, forward):
    ar, ai, k = coeff
    shift = k if forward else 8 - k
    sr, si = pltpu.roll(xr, shift, 0), pltpu.roll(xi, shift, 0)
    return xr + ar * sr - ai * si, xi + ar * si + ai * sr


def _scan_fwd(pw, bu):
    s = bu.shape[0]
    ts = min(SCAN_TILE, s)
    nblk = ts // 8

    def body(bu_ref, pw_ref, h_ref, carry):
        @pl.when(pl.program_id(0) == 0)
        def _():
            carry[...] = jnp.zeros_like(carry)

        coeffs = _scan_coeffs(pw_ref, (0, 1, 3), True)

        def blk(b, _):
            r0 = pl.multiple_of(b * 8, 8)
            xr = bu_ref[pl.ds(r0, 8), :NS]
            xi = bu_ref[pl.ds(r0, 8), NS:]
            for coeff in coeffs:
                xr, xi = _scan_level(xr, xi, coeff, True)
            cr, ci = carry[7:8, :NS], carry[7:8, NS:]
            pr, pi = pw_ref[:, :NS], pw_ref[:, NS:]
            hr = xr + pr * cr - pi * ci
            hi = xi + pr * ci + pi * cr
            h_ref[pl.ds(r0, 8), :NS] = hr
            h_ref[pl.ds(r0, 8), NS:] = hi
            carry[:, :NS] = hr
            carry[:, NS:] = hi
            return 0

        lax.fori_loop(0, nblk, blk, 0)

    return _row_call("scan_fwd", body, s, ts, [bu], [pw], [], [(2 * NS, F32)], [],
                     scratch=[pltpu.VMEM((8, 2 * NS), F32)])[0]


def _scan_bwd(pwb, gh, h):
    s = gh.shape[0]
    ts = min(SCAN_TILE, s)
    nblk = ts // 8
    n_tiles = s // ts

    def body(g_ref, h_ref, pw_ref, lam_ref, da_ref, carry, acc):
        @pl.when(pl.program_id(0) == 0)
        def _():
            carry[...] = jnp.zeros_like(carry)
            acc[...] = jnp.zeros_like(acc)

        row = lax.broadcasted_iota(jnp.int32, (8, NS), 0)
        coeffs = _scan_coeffs(pw_ref, (7, 6, 4), False)

        def blk(bb, _):
            r0 = pl.multiple_of((nblk - 1 - bb) * 8, 8)
            xr = g_ref[pl.ds(r0, 8), :NS]
            xi = g_ref[pl.ds(r0, 8), NS:]
            for coeff in coeffs:
                xr, xi = _scan_level(xr, xi, coeff, False)
            cr, ci = carry[0:1, :NS], carry[0:1, NS:]
            pr, pi = pw_ref[:, :NS], pw_ref[:, NS:]
            lr = xr + pr * cr - pi * ci
            li = xi + pr * ci + pi * cr
            lam_ref[pl.ds(r0, 8), :NS] = lr
            lam_ref[pl.ds(r0, 8), NS:] = li
            nr = jnp.where(row < 7, pltpu.roll(lr, 7, 0), cr)
            ni = jnp.where(row < 7, pltpu.roll(li, 7, 0), ci)
            hr = h_ref[pl.ds(r0, 8), :NS]
            hi = h_ref[pl.ds(r0, 8), NS:]
            acc[:, :NS] += nr * hr + ni * hi
            acc[:, NS:] += ni * hr - nr * hi
            carry[:, :NS] = lr
            carry[:, NS:] = li
            return 0

        lax.fori_loop(0, nblk, blk, 0)

        @pl.when(pl.program_id(0) == n_tiles - 1)
        def _():
            da_ref[...] = jnp.sum(acc[...], axis=0, keepdims=True)

    return _row_call("scan_bwd", body, s, ts, [gh, h], [pwb], [], [(2 * NS, F32)], [(1, 2 * NS)], reverse=True,
                     scratch=[pltpu.VMEM((8, 2 * NS), F32), pltpu.VMEM((8, 2 * NS), F32)])


def _s5out_fwd(h, z3, d_skip, w_c, w_glu):
    def body(h_ref, z3_ref, d_ref, wc, wglu, yl_ref, so_ref):
        yl = _mm(h_ref[...], wc[...]) + d_ref[...] * z3_ref[:, :256]
        yl_ref[...] = yl
        y12 = _mm(_gelu(yl), wglu[...])
        so_ref[...] = y12[:, :1024] * _sigmoid(y12[:, 1024:])

    s = h.shape[0]
    return _row_call("s5out_fwd", body, s, min(ROW_TILE, s), [h, z3], [d_skip], [w_c, w_glu],
                     [(256, F32), (1024, F32)], [])


def _causal_mask_t(t):
    row = lax.broadcasted_iota(jnp.int32, (t, t), 0)
    col = lax.broadcasted_iota(jnp.int32, (t, t), 1)
    return row <= col


def _attn_fwd(q, k, vt):
    s = q.shape[0]
    t = min(ATT_TILE, s)
    nq = s // t
    scale = MLA_QK ** -0.5
    c2 = scale * LOG2E

    def body(q_ref, k_ref, vt_ref, o_ref, lse_ref, s0, s1):
        i = pl.program_id(1)
        qb = q_ref[...]

        def scores(kb, dst):
            st = _mm_nt(k_ref[pl.ds(pl.multiple_of(kb * t, t), t), :], qb)
            dst[...] = st
            return jnp.max(st, axis=0, keepdims=True)

        def update(kb, src, mt, state, masked):
            m, l, acc = state
            if masked:
                keep = _causal_mask_t(t)
                mt = jnp.max(jnp.where(keep, src[...], NEG_INF), axis=0, keepdims=True)
            m_new = jnp.maximum(m, mt)
            if masked:
                p = jnp.exp2((jnp.where(keep, src[...], NEG_INF) - m_new) * c2)
            else:
                p = jnp.exp2((src[...] - m_new) * c2)
            a = jnp.exp2((m - m_new) * c2)
            l = a * l + jnp.sum(p, axis=0, keepdims=True)
            acc = a * acc + _mm(vt_ref[:, pl.ds(pl.multiple_of(kb * t, t), t)], p)
            return m_new, l, acc

        def pair(jj, carry):
            mt_a, state = carry
            mt_b = scores(2 * jj + 1, s1)
            state = update(2 * jj, s0, mt_a, state, False)
            mt_a = scores(2 * jj + 2, s0)
            return mt_a, update(2 * jj + 1, s1, mt_b, state, False)

        def odd_tail(carry):
            mt_a, state = carry
            mt_b = scores(i, s1)
            state = update(i - 1, s0, mt_a, state, False)
            return update(i, s1, mt_b, state, True)

        def even_tail(carry):
            mt_a, state = carry
            return update(i, s0, mt_a, state, True)

        init = (jnp.full((1, t), NEG_INF, F32), jnp.zeros((1, t), F32), jnp.zeros((HEAD_PAD, t), F32))
        carry = lax.fori_loop(0, i // 2, pair, (scores(0, s0), init))
        m, l, acc = lax.cond(i % 2 == 1, odd_tail, even_tail, carry)
        o_ref[...] = (acc / l).T
        lse_ref[...] = m * scale + jnp.log(l)

    return pl.pallas_call(
        body, name="attn_fwd", grid=(MLA_HEADS, nq),
        in_specs=[pl.BlockSpec((t, HEAD_PAD), lambda h, i: (i, h)),
                  pl.BlockSpec((s, HEAD_PAD), lambda h, i: (0, h)),
                  pl.BlockSpec((HEAD_PAD, s), lambda h, i: (h, 0))],
        out_specs=[pl.BlockSpec((t, HEAD_PAD), lambda h, i: (i, h)),
                   pl.BlockSpec((None, 1, t), lambda h, i: (h, 0, i))],
        out_shape=[jax.ShapeDtypeStruct((s, MLA_HEADS * HEAD_PAD), F32),
                   jax.ShapeDtypeStruct((MLA_HEADS, 1, s), F32)],
        scratch_shapes=[pltpu.VMEM((t, t), F32)] * 2,
        compiler_params=pltpu.CompilerParams(dimension_semantics=("arbitrary", "arbitrary"),
                                             vmem_limit_bytes=VMEM_LIMIT),
    )(q, k, vt)


def _attn_bwd(q, k, v, kt, do, lse_t, dl_t):
    s = q.shape[0]
    t = min(ATT_TILE, s)
    nq = s // t
    scale = MLA_QK ** -0.5
    c2 = scale * LOG2E

    def body(q_ref, k_ref, v_ref, kt_ref, do_ref, lse_ref, dl_ref, dq_ref, dk_ref, dv_ref, dqt, s0, p0, s1, p1):
        j = pl.program_id(1)
        n = nq - 1 - j

        @pl.when(j == 0)
        def _():
            dqt[...] = jnp.zeros_like(dqt)

        kk = k_ref[...]
        vv = v_ref[...]
        ktb = kt_ref[:MLA_QK, :]

        def rows(m):
            return pl.ds(pl.multiple_of(jnp.where(m < n, j + 1 + m, j) * t, t), t)

        def first(m, sbuf, pbuf):
            r = rows(m)
            sbuf[...] = _mm_nt(kk, q_ref[r, :])
            pbuf[...] = _mm_nt(vv, do_ref[r, :])

        def finish(m, sbuf, pbuf, acc, masked):
            dk, dv = acc
            r = rows(m)
            pt = jnp.exp2(sbuf[...] * c2 - lse_ref[:, r] * LOG2E)
            if masked:
                pt = jnp.where(_causal_mask_t(t), pt, 0.0)
            dv = dv + _mm(pt, do_ref[r, :])
            dst = (pt * (pbuf[...] - dl_ref[:, r])).astype(BF16)
            dk = dk + _mm(dst, q_ref[r, :])
            dqt[:MLA_QK, r] += _mm(ktb, dst)
            return dk, dv

        def pair(jj, acc):
            first(2 * jj + 1, s1, p1)
            acc = finish(2 * jj, s0, p0, acc, False)
            first(2 * jj + 2, s0, p0)
            return finish(2 * jj + 1, s1, p1, acc, False)

        def odd_tail(acc):
            first(n, s1, p1)
            acc = finish(n - 1, s0, p0, acc, False)
            return finish(n, s1, p1, acc, True)

        def even_tail(acc):
            return finish(n, s0, p0, acc, True)

        zero = jnp.zeros((t, HEAD_PAD), F32)
        first(0, s0, p0)
        acc = lax.fori_loop(0, n // 2, pair, (zero, zero))
        dk, dv = lax.cond(n % 2 == 1, odd_tail, even_tail, acc)
        dk_ref[...] = (dk * scale).astype(BF16)
        dv_ref[...] = dv.astype(BF16)

        @pl.when(j == nq - 1)
        def _():
            for cc in range(nq):
                dq_ref[cc * t:(cc + 1) * t, :] = (dqt[:, cc * t:(cc + 1) * t].T * scale).astype(BF16)

    full = pl.BlockSpec((s, HEAD_PAD), lambda h, j: (0, h))
    tile = pl.BlockSpec((t, HEAD_PAD), lambda h, j: (j, h))
    stat = pl.BlockSpec((None, 1, s), lambda h, j: (h, 0, 0))
    return pl.pallas_call(
        body, name="attn_bwd", grid=(MLA_HEADS, nq),
        in_specs=[full, tile, tile, pl.BlockSpec((HEAD_PAD, t), lambda h, j: (h, j)), full, stat, stat],
        out_specs=[full, tile, tile],
        out_shape=[jax.ShapeDtypeStruct((s, MLA_HEADS * HEAD_PAD), BF16)] * 3,
        scratch_shapes=[pltpu.VMEM((HEAD_PAD, s), F32)] + [pltpu.VMEM((t, t), F32)] * 4,
        compiler_params=pltpu.CompilerParams(dimension_semantics=("arbitrary", "arbitrary"),
                                             vmem_limit_bytes=VMEM_LIMIT),
    )(q, k, v, kt, do, lse_t, dl_t)


def _xattn_probs(qxb, memk, hh):
    sl = slice(hh * XATTN_HD, (hh + 1) * XATTN_HD)
    sc = _mm_nt(qxb[:, sl], memk[:, sl]) * (XATTN_HD ** -0.5)
    e = jnp.exp(sc - jnp.max(sc, axis=1, keepdims=True))
    return e / jnp.sum(e, axis=1, keepdims=True)


def _post_forward(o, s_out, gate, h0, memk, memv, w_oa, w_o, w_xq, w_xo, g1, b1, g2, b2):
    a_out = _mm(o, w_oa)
    sg_s = _sigmoid(gate[:, :1024])
    sg_a = _sigmoid(gate[:, 1024:])
    mixin = sg_s * s_out + sg_a * a_out
    r1 = DN_ALPHA * h0 + _mm(mixin, w_o)
    h1, xh1, rs1 = _ln_fwd(r1, g1, b1)
    qxb = _mm(h1, w_xq).astype(BF16)
    ox = jnp.concatenate([_mm(_xattn_probs(qxb, memk, hh), memv[:, hh * XATTN_HD:(hh + 1) * XATTN_HD])
                          for hh in range(XATTN_HEADS)], axis=1)
    r2 = DN_ALPHA * h1 + _mm(ox, w_xo)
    return dict(a_out=a_out, mixin=mixin, r1=r1, h1=h1, qxb=qxb, ox=ox, r2=r2)


def _post_fwd(o, s_out, gate, h0, memk, memv, lns, w_oa, w_o, w_xq, w_xo):
    def body(o_ref, so_ref, gate_ref, h0_ref, mk_ref, mv_ref, g1, b1, g2, b2, woa, wo, wxq, wxo,
             r2_ref, mixin_ref, h1_ref, ox_ref, aout_ref, r1_ref, qx_ref):
        f = _post_forward(o_ref[...], so_ref[...], gate_ref[...].astype(F32), h0_ref[...], mk_ref[...], mv_ref[...],
                          woa[...], wo[...], wxq[...], wxo[...], g1[...], b1[...], g2[...], b2[...])
        r2_ref[...] = f["r2"]
        mixin_ref[...] = f["mixin"].astype(BF16)
        h1_ref[...] = f["h1"].astype(BF16)
        ox_ref[...] = f["ox"].astype(BF16)
        aout_ref[...] = f["a_out"]
        r1_ref[...] = f["r1"]
        qx_ref[...] = f["qxb"]

    s = o.shape[0]
    return _row_call("post_fwd", body, s, min(ROW_TILE, s), [o, s_out, gate, h0], [memk, memv, *lns],
                     [w_oa, w_o, w_xq, w_xo],
                     [(1024, F32), (1024, BF16), (1024, BF16), (1024, BF16), (1024, F32), (1024, F32), (1024, BF16)], [])


def _mlp(r2, target, g2, b2, g3, b3, w_up, w_down):
    cw = 1024
    n_chunk = MLP_HIDDEN // cw

    def body(r2_ref, tgt_ref, g2_ref, b2_ref, g3_ref, b3_ref, wup, wdn,
             dh2_ref, h2b_ref, dup_ref, act_ref, dff_ref, loss_ref, dg3_ref, db3_ref, up_scr):
        @pl.when(pl.program_id(0) == 0)
        def _():
            loss_ref[...] = jnp.zeros_like(loss_ref)
            dg3_ref[...] = jnp.zeros_like(dg3_ref)
            db3_ref[...] = jnp.zeros_like(db3_ref)

        h2, _, _ = _ln_fwd(r2_ref[...], g2_ref[...], b2_ref[...])
        h2b = h2.astype(BF16)
        h2b_ref[...] = h2b
        ff = jnp.zeros(h2.shape, F32)
        for c in range(n_chunk):
            sl = slice(c * cw, (c + 1) * cw)
            a = jnp.maximum(_mm(h2b, wup[:, sl]), 0.0)
            up_scr[:, sl] = a
            actb = (a * a).astype(BF16)
            act_ref[:, sl] = actb
            ff = ff + _mm(actb, wdn[sl, :])
        h3, xh3, rs3 = _ln_fwd(DN_ALPHA * h2 + ff, g3_ref[...], b3_ref[...])
        err = h3 - tgt_ref[...]
        loss_ref[...] += 0.5 * jnp.sum(err * err) * (1.0 / D_MODEL)
        dr3, dg3, db3 = _ln_bwd(err * (1.0 / D_MODEL), xh3, rs3, g3_ref[...])
        dg3_ref[...] += dg3
        db3_ref[...] += db3
        dffb = dr3.astype(BF16)
        dff_ref[...] = dffb
        dh2 = DN_ALPHA * dr3
        for c in range(n_chunk):
            sl = slice(c * cw, (c + 1) * cw)
            dupb = (_mm_nt(dffb, wdn[sl, :]) * (2.0 * up_scr[:, sl])).astype(BF16)
            dup_ref[:, sl] = dupb
            dh2 = dh2 + _mm_nt(dupb, wup[:, sl])
        dh2_ref[...] = dh2

    s = r2.shape[0]
    ts = min(ROW_TILE, s)
    return _row_call("mlp", body, s, ts, [r2, target], [g2, b2, g3, b3], [w_up, w_down],
                     [(1024, F32), (1024, BF16), (MLP_HIDDEN, BF16), (MLP_HIDDEN, BF16), (1024, BF16)],
                     [(8, 128), (1, 1024), (1, 1024)], scratch=[pltpu.VMEM((ts, MLP_HIDDEN), F32)])


def _post_bwd(dh2, o, s_out, gate, a_out, r1, r2, qx, memk, memv, lns, w_oa, w_o, w_xq, w_xo):
    def body(dh2_ref, o_ref, so_ref, gate_ref, aout_ref, r1_ref, r2_ref, qx_ref, mk_ref, mv_ref, g1, b1, g2, b2,
             woa, wo, wxq, wxo,
             do_ref, dl_ref, dso_ref, dgate_ref, dh0_ref, daout_ref, dmix_ref, dqx_ref, dxa_ref,
             dmk_ref, dmv_ref, dg1_ref, db1_ref, dg2_ref, db2_ref):
        @pl.when(pl.program_id(0) == 0)
        def _():
            for r in (dmk_ref, dmv_ref, dg1_ref, db1_ref, dg2_ref, db2_ref):
                r[...] = jnp.zeros_like(r)

        o = o_ref[...]
        s_out = so_ref[...]
        memk, memv = mk_ref[...], mv_ref[...]
        qxb = qx_ref[...]
        _, xh1, rs1 = _ln_fwd(r1_ref[...], g1[...], b1[...])
        _, xh2, rs2 = _ln_fwd(r2_ref[...], g2[...], b2[...])
        dr2, dg2, db2 = _ln_bwd(dh2_ref[...], xh2, rs2, g2[...])
        dg2_ref[...] += dg2
        db2_ref[...] += db2
        dxab = dr2.astype(BF16)
        dxa_ref[...] = dxab
        dox = _mm_nt(dxab, wxo[...])
        dqs = []
        for hh in range(XATTN_HEADS):
            sl = slice(hh * XATTN_HD, (hh + 1) * XATTN_HD)
            p = _xattn_probs(qxb, memk, hh)
            doxh = dox[:, sl].astype(BF16)
            dp = _mm_nt(doxh, memv[:, sl])
            ds = (p * (dp - jnp.sum(dp * p, axis=1, keepdims=True)) * (XATTN_HD ** -0.5)).astype(BF16)
            dqs.append(_mm(ds, memk[:, sl]))
            dmk_ref[:, sl] += _mm_tn(ds, qxb[:, sl])
            dmv_ref[:, sl] += _mm_tn(p, doxh)
        dqxb = jnp.concatenate(dqs, axis=1).astype(BF16)
        dqx_ref[...] = dqxb
        dh1 = DN_ALPHA * dr2 + _mm_nt(dqxb, wxq[...])
        dr1, dg1, db1 = _ln_bwd(dh1, xh1, rs1, g1[...])
        dg1_ref[...] += dg1
        db1_ref[...] += db1
        dh0_ref[...] = DN_ALPHA * dr1
        dmixb = dr1.astype(BF16)
        dmix_ref[...] = dmixb
        dmixin = _mm_nt(dmixb, wo[...])
        sg_s = _sigmoid(gate_ref[:, :1024].astype(F32))
        sg_a = _sigmoid(gate_ref[:, 1024:].astype(F32))
        dso_ref[...] = dmixin * sg_s
        daoutb = (dmixin * sg_a).astype(BF16)
        daout_ref[...] = daoutb
        dgate_ref[:, :1024] = (dmixin * s_out * sg_s * (1.0 - sg_s)).astype(BF16)
        dgate_ref[:, 1024:] = (dmixin * aout_ref[...] * sg_a * (1.0 - sg_a)).astype(BF16)
        d_o = _mm_nt(daoutb, woa[...])
        do_ref[...] = d_o.astype(BF16)
        lane = lax.broadcasted_iota(jnp.int32, (o.shape[0], HEAD_PAD), 1)
        dlc = jnp.zeros((o.shape[0], HEAD_PAD), F32)
        for hh in range(MLA_HEADS):
            sl = slice(hh * HEAD_PAD, (hh + 1) * HEAD_PAD)
            dl = jnp.sum(d_o[:, sl] * o[:, sl], axis=1, keepdims=True)
            dlc = dlc + jnp.where(lane == hh, dl, 0.0)
        dl_ref[...] = dlc

    s = o.shape[0]
    m = memk.shape[0]
    return _row_call(
        "post_bwd", body, s, min(ROW_TILE, s), [dh2, o, s_out, gate, a_out, r1, r2, qx], [memk, memv, *lns],
        [w_oa, w_o, w_xq, w_xo],
        [(1024, BF16), (HEAD_PAD, F32), (1024, F32), (2048, BF16), (1024, F32),
         (1024, BF16), (1024, BF16), (1024, BF16), (1024, BF16)],
        [(m, 1024), (m, 1024), (1, 1024), (1, 1024), (1, 1024), (1, 1024)])


def _s5out_bwd(d_so, yl, z3, d_skip, w_c, w_glu):
    def body(dso_ref, yl_ref, z3_ref, d_ref, wc, wglu, gh_ref, dup_ref, yg_ref, dy12_ref, dyl_ref, dd_ref):
        @pl.when(pl.program_id(0) == 0)
        def _():
            dd_ref[...] = jnp.zeros_like(dd_ref)

        yl = yl_ref[...]
        ygb = _gelu(yl).astype(BF16)
        yg_ref[...] = ygb
        y12 = _mm(ygb, wglu[...])
        sg = _sigmoid(y12[:, 1024:])
        dso = dso_ref[...]
        dy12b = jnp.concatenate([dso * sg, dso * y12[:, :1024] * sg * (1.0 - sg)], axis=1).astype(BF16)
        dy12_ref[...] = dy12b
        dyl = _mm_nt(dy12b, wglu[...]) * _gelu_grad(yl)
        dylb = dyl.astype(BF16)
        dyl_ref[...] = dylb
        gh_ref[...] = _mm_nt(dylb, wc[...])
        dup_ref[...] = dyl * d_ref[...]
        dd_ref[...] += jnp.sum(dyl * z3_ref[:, :256], axis=0, keepdims=True)

    s = d_so.shape[0]
    return _row_call("s5out_bwd", body, s, min(ROW_TILE, s), [d_so, yl, z3], [d_skip], [w_c, w_glu],
                     [(2 * NS, F32), (256, F32), (256, BF16), (2048, BF16), (256, BF16)], [(1, 256)])


def _in_bwd(x, tab, z3, dq, dk, dv, lam, du_p, dgate, dh0p, ln_g, ln_b, qg, kvg, w_inx, w_b, w_q2, w_kv2):
    def body(x_ref, tab_ref, z3_ref, dq_ref, dk_ref, dv_ref, lam_ref, dup_ref, dgate_ref, dh0p_ref,
             lng_ref, lnb_ref, qg_ref, kvg_ref, winx, wb, wq2, wkv2,
             dx_ref, h0b_ref, dz_ref, ub_ref, cqn_ref, ckvn_ref, dq2_ref, dkv2_ref,
             dlng_ref, dlnb_ref, dqg_ref, dkvg_ref):
        @pl.when(pl.program_id(0) == 0)
        def _():
            for r in (dlng_ref, dlnb_ref, dqg_ref, dkvg_ref):
                r[...] = jnp.zeros_like(r)

        h0, xh0, rs0 = _ln_fwd(x_ref[...], lng_ref[...], lnb_ref[...])
        h0b_ref[...] = h0.astype(BF16)
        z3 = z3_ref[...]
        ub_ref[...] = z3[:, :256].astype(BF16)
        tab_v = tab_ref[...]
        cq1, cq2, ck1 = tab_v[:, :128], tab_v[:, 128:], tab_v[:, :128]
        du = _mm_nt(lam_ref[...], wb[...]) + dup_ref[...]
        dq = dq_ref[...].astype(F32)
        dq2b = jnp.concatenate([dq * _tile_heads(cq1), dq * _tile_heads(cq2)], axis=1).astype(BF16)
        dq2_ref[...] = dq2b
        cqg, cqn, rq = _rms_fwd(z3[:, 256:512], qg_ref[...])
        cqn_ref[...] = cqg.astype(BF16)
        dcq, dqg = _rms_bwd(_mm_nt(dq2b, wq2[...]), cqn, rq, qg_ref[...])
        dqg_ref[...] += dqg
        dkv2b = jnp.concatenate([dk_ref[...], dv_ref[...]], axis=1)
        dk = dk_ref[...].astype(F32)
        dkv2_ref[...] = dkv2b
        ckvg, ckvn, rkv = _rms_fwd(z3[:, 512:768], kvg_ref[...])
        ckvn_ref[...] = ckvg.astype(BF16)
        dckv, dkvg = _rms_bwd(_mm_nt(dkv2b, wkv2[...]), ckvn, rkv, kvg_ref[...])
        dkvg_ref[...] += dkvg
        dkrp = dk[:, :HEAD_PAD]
        for hh in range(1, MLA_HEADS):
            dkrp = dkrp + dk[:, hh * HEAD_PAD:(hh + 1) * HEAD_PAD]
        dzb = jnp.concatenate([a.astype(BF16) for a in (du, dcq, dckv, dkrp * ck1, dkrp * cq2)] + [dgate_ref[...]],
                              axis=1)
        dz_ref[...] = dzb
        dh0 = _mm_nt(dzb, winx[...]) + dh0p_ref[...]
        dx, dg, db = _ln_bwd(dh0, xh0, rs0, lng_ref[...])
        dx_ref[...] = dx
        dlng_ref[...] += dg
        dlnb_ref[...] += db

    s = x.shape[0]
    return _row_call(
        "in_bwd", body, s, min(ROW_TILE, s), [x, tab, z3, dq, dk, dv, lam, du_p, dgate, dh0p],
        [ln_g, ln_b, qg, kvg], [w_inx, w_b, w_q2, w_kv2],
        [(1024, F32), (1024, BF16), (3072, BF16), (256, BF16), (256, BF16), (256, BF16), (2048, BF16), (2048, BF16)],
        [(1, 1024), (1, 1024), (1, 256), (1, 256)])


def _wgrad(name, xs, dy, out_dtype=F32, blocks=1, after=None):
    s, k = xs.shape
    n = dy.shape[1]
    nb = n // blocks
    tk, tn = min(k, 1024), min(n, 1024)
    ts = min(s, 2048 if tk <= 256 else 1024)
    per_tile = tn // nb
    assert s % ts == 0 and k % tk == 0 and n % tn == 0 and (blocks == 1 or tn % nb == 0), (name, s, k, n)
    last = s // ts - 1

    def body(x_ref, dy_ref, *rest):
        o_ref, acc = rest[-2:]

        @pl.when(pl.program_id(2) == 0)
        def _():
            acc[...] = jnp.zeros_like(acc)

        acc[...] += _mm_tn(x_ref[...], dy_ref[...])

        @pl.when(pl.program_id(2) == last)
        def _():
            if blocks == 1:
                o_ref[...] = acc[...].astype(out_dtype)
            else:
                for d in range(per_tile):
                    o_ref[d] = acc[:, d * nb:(d + 1) * nb].astype(out_dtype)

    if blocks == 1:
        out_shape = jax.ShapeDtypeStruct((k, n), out_dtype)
        out_spec = pl.BlockSpec((tk, tn), lambda a, b, c: (a, b))
    else:
        out_shape = jax.ShapeDtypeStruct((blocks, k, nb), out_dtype)
        out_spec = pl.BlockSpec((per_tile, tk, nb), lambda a, b, c: (b, a, 0))
    return pl.pallas_call(
        body, name=name, grid=(k // tk, n // tn, s // ts),
        in_specs=[pl.BlockSpec((ts, tk), lambda a, b, c: (c, a)), pl.BlockSpec((ts, tn), lambda a, b, c: (c, b))]
        + ([] if after is None else [pl.BlockSpec(memory_space=pl.ANY)]),
        out_specs=out_spec, out_shape=out_shape, scratch_shapes=[pltpu.VMEM((tk, tn), F32)],
        compiler_params=pltpu.CompilerParams(dimension_semantics=("arbitrary", "arbitrary", "arbitrary"),
                                             vmem_limit_bytes=VMEM_LIMIT),
    )(xs, dy, *([] if after is None else [after]))


def _mesh_pos():
    x, y, c = lax.axis_index("x"), lax.axis_index("y"), lax.axis_index("c")
    return x, y, c


def _peer(x, y, c, k):
    px = 1 - x if k & 4 else x
    py = 1 - y if k & 2 else y
    pc = 1 - c if k & 1 else c
    return (px, py, pc), 4 * px + 2 * py + pc


def _all_to_all(name, arrays, gather):
    n = len(arrays)

    def body(*refs):
        src, dst = refs[:n], refs[n:2 * n]
        send_sems, recv_sems, local_sems = refs[2 * n:]
        x, y, c = _mesh_pos()
        me = 4 * x + 2 * y + c

        def block(i, d):
            return src[i] if gather else src[i].at[d]

        own = [pltpu.make_async_copy(block(i, me), dst[i].at[me], local_sems.at[i]) for i in range(n)]
        for cp in own:
            cp.start()
        sends = []
        for k in range(1, N_DEV):
            peer, pid = _peer(x, y, c, k)
            for i in range(n):
                idx = (k - 1) * n + i
                cp = pltpu.make_async_remote_copy(src_ref=block(i, pid), dst_ref=dst[i].at[me],
                                                  send_sem=send_sems.at[idx], recv_sem=recv_sems.at[idx],
                                                  device_id=peer, device_id_type=MESH)
                cp.start()
                sends.append(cp)
        for k in range(1, N_DEV):
            peer, pid = _peer(x, y, c, k)
            for i in range(n):
                idx = (k - 1) * n + i
                pltpu.make_async_remote_copy(src_ref=block(i, pid), dst_ref=dst[i].at[pid],
                                             send_sem=send_sems.at[idx], recv_sem=recv_sems.at[idx],
                                             device_id=peer, device_id_type=MESH).wait_recv()
        for cp in sends:
            cp.wait_send()
        for cp in own:
            cp.wait()

    n_sem = n * (N_DEV - 1)
    return pl.pallas_call(
        body, name=name,
        in_specs=[pl.BlockSpec(memory_space=pl.ANY)] * n,
        out_specs=[pl.BlockSpec(memory_space=pl.ANY)] * n,
        out_shape=[jax.ShapeDtypeStruct((N_DEV,) + a.shape[-2:], a.dtype) for a in arrays],
        scratch_shapes=[pltpu.SemaphoreType.DMA((n_sem,)), pltpu.SemaphoreType.DMA((n_sem,)),
                        pltpu.SemaphoreType.DMA((n,))],
    )(*arrays)


_HBM = pl.BlockSpec(memory_space=pltpu.HBM)
_SEM = pl.BlockSpec(memory_space=pltpu.SEMAPHORE)


def _exchange_start(name, arrays, gather, after=None):
    n = len(arrays)
    n_sem = n * (N_DEV - 1)
    me = 4 * lax.axis_index("x") + 2 * lax.axis_index("y") + lax.axis_index("c")
    lands = []
    for a in arrays:
        own = a[None] if gather else lax.dynamic_slice_in_dim(a, me, 1, 0)
        lands.append(lax.dynamic_update_slice(lax.empty((N_DEV,) + a.shape[-2:], a.dtype), own, (me, 0, 0)))
    n_after = 0 if after is None else 1

    def body(*refs):
        src, land = refs[:n], refs[n:2 * n]
        send_sems, recv_sems = refs[2 * n + n_after], refs[2 * n + n_after + 1]
        token = refs[-1]
        x, y, c = _mesh_pos()
        me_in = 4 * x + 2 * y + c
        for k in range(1, N_DEV):
            peer, pid = _peer(x, y, c, k)
            for i in range(n):
                idx = (k - 1) * n + i
                pltpu.make_async_remote_copy(src_ref=src[i] if gather else src[i].at[pid], dst_ref=land[i].at[me_in],
                                             send_sem=send_sems.at[idx], recv_sem=recv_sems.at[idx],
                                             device_id=peer, device_id_type=MESH).start()
        token[...] = jnp.zeros_like(token)

    operands = [pltpu.with_memory_space_constraint(a, pltpu.HBM) for a in list(arrays) + lands]
    outs = pl.pallas_call(
        body, name=name,
        out_shape=(pltpu.SemaphoreType.DMA((n_sem,)), pltpu.SemaphoreType.DMA((n_sem,)),
                   *[pltpu.HBM(a.shape, a.dtype) for a in list(arrays) + lands],
                   jax.ShapeDtypeStruct((8, 128), F32)),
        in_specs=[_HBM] * (2 * n) + [pl.BlockSpec(memory_space=pl.ANY)] * n_after,
        out_specs=(_SEM, _SEM, *[_HBM] * (2 * n), pl.BlockSpec(memory_space=pltpu.VMEM)),
        input_output_aliases={i: 2 + i for i in range(2 * n)},
        compiler_params=pltpu.CompilerParams(has_side_effects=pltpu.SideEffectType.DATAFLOW_SIDE_EFFECTING),
    )(*operands, *([after] if n_after else []))
    return (gather, outs[0], outs[1], outs[2:2 + n], outs[2 + n:2 + 2 * n]), outs[-1]


def _exchange_wait(name, handle, after):
    gather, send_sems, recv_sems, srcs, lands = handle
    n = len(srcs)

    def body(*refs):
        src, land = refs[:n], refs[n:2 * n]
        s_sems, r_sems = refs[2 * n], refs[2 * n + 1]
        x, y, c = _mesh_pos()
        for k in range(1, N_DEV):
            peer, pid = _peer(x, y, c, k)
            for i in range(n):
                idx = (k - 1) * n + i
                cp = pltpu.make_async_remote_copy(src_ref=src[i] if gather else src[i].at[pid], dst_ref=land[i].at[pid],
                                                  send_sem=s_sems.at[idx], recv_sem=r_sems.at[idx],
                                                  device_id=peer, device_id_type=MESH)
                cp.wait_send()
                cp.wait_recv()

    outs = pl.pallas_call(
        body, name=name,
        out_shape=tuple(pltpu.HBM(a.shape, a.dtype) for a in list(srcs) + list(lands)),
        in_specs=[_HBM] * (2 * n) + [_SEM, _SEM, pl.BlockSpec(memory_space=pl.ANY)],
        out_specs=tuple([_HBM] * (2 * n)),
        input_output_aliases={i: i for i in range(2 * n)},
        compiler_params=pltpu.CompilerParams(has_side_effects=pltpu.SideEffectType.DATAFLOW_SIDE_EFFECTING),
    )(*srcs, *lands, send_sems, recv_sems, after)
    return list(outs[n:])


def _adamw(name, parts, w, m, v):
    a_rows, b_cols = w.shape
    ta = min(a_rows, ADAM_TILE)
    assert a_rows % ta == 0
    c1 = 1.0 - ADAM_B1 ** ADAM_STEP
    c2 = 1.0 - ADAM_B2 ** ADAM_STEP

    def body(p_ref, w_ref, m_ref, v_ref, g_ref, d_ref, mo_ref, vo_ref):
        g = p_ref[0].astype(F32)
        for d in range(1, N_DEV):
            g = g + p_ref[d].astype(F32)
        g_ref[...] = g
        mn = ADAM_B1 * m_ref[...] + (1.0 - ADAM_B1) * g
        vn = ADAM_B2 * v_ref[...] + (1.0 - ADAM_B2) * (g * g)
        mo_ref[...] = mn
        vo_ref[...] = vn
        d_ref[...] = -ADAM_LR * ((mn / c1) / (jnp.sqrt(vn / c2) + ADAM_EPS) + ADAM_WD * w_ref[...])

    row = pl.BlockSpec((ta, b_cols), lambda i: (i, 0))
    return pl.pallas_call(
        body, name=name, grid=(a_rows // ta,),
        in_specs=[pl.BlockSpec((N_DEV, ta, b_cols), lambda i: (0, i, 0)), row, row, row],
        out_specs=[row] * 4,
        out_shape=[jax.ShapeDtypeStruct((a_rows, b_cols), F32)] * 4,
        compiler_params=pltpu.CompilerParams(dimension_semantics=("arbitrary",), vmem_limit_bytes=VMEM_LIMIT),
    )(parts, w, m, v)


def _pack_rows(arrays, rows):
    flat = jnp.concatenate([a.reshape(-1) for a in arrays])
    return jnp.pad(flat, (0, rows * LANES - flat.shape[0])).reshape(rows, LANES)


def _cols_from_blocks(g):
    return g.transpose(1, 0, 2).reshape(g.shape[1], N_DEV * g.shape[2])


def _blocks_from_cols(w):
    return w.reshape(w.shape[0], N_DEV, w.shape[1] // N_DEV).transpose(1, 0, 2)


def _unpack_flat(flat2d, shapes):
    flat = flat2d.reshape(-1)
    out, off = [], 0
    for shp in shapes:
        sz = math.prod(shp)
        out.append(flat[off:off + sz].reshape(shp))
        off += sz
    return out


def _s5_mats(lam_re, lam_im, log_dt, b_re, b_im, c_re, c_im):
    lr = jnp.minimum(lam_re, S5_MAX_RE)
    li = lam_im
    dt = jnp.exp(log_dt)[:, None]
    mag = jnp.exp(lr * dt)
    ang = li * dt
    ab_re = mag * jnp.cos(ang)
    ab_im = mag * jnp.sin(ang)
    den = lr * lr + li * li
    nr = ab_re - 1.0
    f_re = ((nr * lr + ab_im * li) / den)[..., None]
    f_im = ((ab_im * lr - nr * li) / den)[..., None]
    bb_re = f_re * b_re - f_im * b_im
    bb_im = f_re * b_im + f_im * b_re
    eye = jnp.eye(S5_GROUPS, dtype=F32)
    a = jnp.stack([ab_re.reshape(-1), ab_im.reshape(-1)])
    wb = jnp.concatenate([jnp.einsum("gph,gk->ghkp", bb_re, eye).reshape(S5_WIDTH, NS),
                          jnp.einsum("gph,gk->ghkp", bb_im, eye).reshape(S5_WIDTH, NS)], axis=1)
    wc = jnp.concatenate([jnp.einsum("ghp,gk->gpkh", c_re, eye).reshape(NS, S5_WIDTH),
                          -jnp.einsum("ghp,gk->gpkh", c_im, eye).reshape(NS, S5_WIDTH)], axis=0)
    return a, wb, wc


def _power_table(a):
    ar, ai = a[0], a[1]
    rows_r, rows_i = [ar], [ai]
    for _ in range(7):
        pr, pi = rows_r[-1], rows_i[-1]
        rows_r.append(pr * ar - pi * ai)
        rows_i.append(pr * ai + pi * ar)
    return jnp.concatenate([jnp.stack(rows_r), jnp.stack(rows_i)], axis=1)


def _rope_table(positions, zero):
    inv = ROPE_THETA ** (-jnp.arange(0, MLA_ROPE, 2, dtype=F32) / MLA_ROPE)
    inv128 = jnp.concatenate([jnp.zeros((MLA_NOPE,), F32), inv, inv, jnp.zeros((32,), F32)])
    sign = jnp.concatenate([jnp.zeros((MLA_NOPE,), F32), -jnp.ones((16,), F32), jnp.ones((16,), F32),
                            jnp.zeros((32,), F32)])
    ang = positions.astype(F32)[:, None] * inv128 + zero
    return jnp.concatenate([jnp.cos(ang), jnp.sin(ang) * sign], axis=1)


def _derived_weights(full):
    w_in = full["w_in"]
    k1, k2 = w_in[:, 768:784], w_in[:, 784:800]
    z64, z32 = jnp.zeros((1024, 64), BF16), jnp.zeros((1024, 32), BF16)
    w_inx = jnp.concatenate([w_in[:, :768], z64, k1, k2, z32, z64, k2, k1, z32, w_in[:, 800:]], axis=1)
    uq = full["w_uq"].reshape(256, MLA_HEADS, MLA_QK)
    nope, r1, r2 = uq[:, :, :64], uq[:, :, 64:80], uq[:, :, 80:]
    zq64, zq32 = jnp.zeros((256, MLA_HEADS, 64), BF16), jnp.zeros((256, MLA_HEADS, 32), BF16)
    w_q2 = jnp.concatenate([jnp.concatenate([nope, r1, r2, zq32], axis=2).reshape(256, 1024),
                            jnp.concatenate([zq64, r2, r1, zq32], axis=2).reshape(256, 1024)], axis=1)
    ukv = full["w_ukv"].reshape(256, MLA_HEADS, 128)
    w_kv2 = jnp.concatenate([jnp.concatenate([ukv[:, :, :64], zq64], axis=2).reshape(256, 1024),
                             jnp.concatenate([ukv[:, :, 64:], zq64], axis=2).reshape(256, 1024)], axis=1)
    return w_inx, w_q2, w_kv2


def _oa_padded(w_oa):
    oa = w_oa.reshape(MLA_HEADS, MLA_V, 1024)
    return jnp.concatenate([oa, jnp.zeros_like(oa)], axis=1).reshape(1024, 1024)


def _fold_w_in(d_inx):
    d_k1 = d_inx[:, 832:848] + d_inx[:, 976:992]
    d_k2 = d_inx[:, 848:864] + d_inx[:, 960:976]
    return jnp.concatenate([d_inx[:, :768], d_k1, d_k2, d_inx[:, 1024:]], axis=1)


def _fold_qkv(d_q2, d_kv2):
    a = d_q2[:, :1024].reshape(256, MLA_HEADS, 128)
    b = d_q2[:, 1024:].reshape(256, MLA_HEADS, 128)
    d_w_uq = jnp.concatenate([a[:, :, :64], a[:, :, 64:80] + b[:, :, 80:96], a[:, :, 80:96] + b[:, :, 64:80]],
                             axis=2).reshape(256, MLA_HEADS * MLA_QK)
    kk = d_kv2[:, :1024].reshape(256, MLA_HEADS, 128)
    vv = d_kv2[:, 1024:].reshape(256, MLA_HEADS, 128)
    d_w_ukv = jnp.concatenate([kk[:, :, :64], vv[:, :, :64]], axis=2).reshape(256, 1024)
    return d_w_uq, d_w_ukv


def kernel(x, mem, positions, ln_in_g, ln_in_b, w_in, s5_lam_re, s5_lam_im, s5_log_dt, s5_b_re, s5_b_im, s5_c_re, s5_c_im, s5_d, w_glu, q_norm_g, w_uq, kv_norm_g, w_ukv, w_oa, w_o, ln1_g, ln1_b, w_xq, w_xk, w_xv, w_xo, ln2_g, ln2_b, w_up, w_down, ln3_g, ln3_b, loss_target, m_ln_in_g, m_ln_in_b, m_w_in, m_s5_lam_re, m_s5_lam_im, m_s5_log_dt, m_s5_b_re, m_s5_b_im, m_s5_c_re, m_s5_c_im, m_s5_d, m_w_glu, m_q_norm_g, m_w_uq, m_kv_norm_g, m_w_ukv, m_w_oa, m_w_o, m_ln1_g, m_ln1_b, m_w_xq, m_w_xk, m_w_xv, m_w_xo, m_ln2_g, m_ln2_b, m_w_up, m_w_down, m_ln3_g, m_ln3_b, v_ln_in_g, v_ln_in_b, v_w_in, v_s5_lam_re, v_s5_lam_im, v_s5_log_dt, v_s5_b_re, v_s5_b_im, v_s5_c_re, v_s5_c_im, v_s5_d, v_w_glu, v_q_norm_g, v_w_uq, v_kv_norm_g, v_w_ukv, v_w_oa, v_w_o, v_ln1_g, v_ln1_b, v_w_xq, v_w_xk, v_w_xv, v_w_xo, v_ln2_g, v_ln2_b, v_w_up, v_w_down, v_ln3_g, v_ln3_b):
    args = dict(locals())
    wts = {n: args[n] for n in WEIGHTS}
    mom = {n: args["m_" + n] for n in WEIGHTS}
    vel = {n: args["v_" + n] for n in WEIGHTS}
    xs, mems, tgt = x[0], mem[0], loss_target[0]
    s = xs.shape[0]
    names_big = [n for (n, _, _, _) in SHARDED]
    kind = {n: kd for (n, kd, _, _) in SHARDED}
    n_small = sum(math.prod(wts[n].shape) for n in SMALL)
    small_rows = _round_up(n_small + 1, 8 * LANES) // LANES

    late = [n for n in names_big if n not in EARLY]
    shard = {n: wts[n][0].astype(BF16) for n in names_big}
    age_handle, age_token = _exchange_start("all_gather_early", [shard[n] for n in EARLY], True)
    t0 = age_token[0, 0]
    tab = _rope_table(positions[0], t0)
    small_packs = [_pack_rows([t[n] for n in SMALL], small_rows) + t0 for t in (wts, mom, vel)]
    s5_args = (s5_lam_re[0] + t0, s5_lam_im[0], s5_log_dt[0], s5_b_re[0], s5_b_im[0], s5_c_re[0], s5_c_im[0])
    (a_mat, w_b, w_c), s5_vjp = jax.vjp(_s5_mats, *s5_args)
    w_bb, w_cb = w_b.astype(BF16), w_c.astype(BF16)
    pw = _power_table(a_mat)
    pwb = _power_table(a_mat * jnp.array([[1.0], [-1.0]], F32))[::-1]
    g_early = dict(zip(EARLY, _exchange_wait("all_gather_early_wait", age_handle, after=tab)))
    ag_handle, ag_token = _exchange_start("all_gather_late", [shard[n] for n in late], True, after=g_early["w_in"])
    full = {n: _cols_from_blocks(g_early[n]) for n in EARLY}
    w_inx, w_q2, w_kv2 = _derived_weights(full)
    row = lambda a: a.reshape(1, -1)
    ln_g, ln_b = row(ln_in_g) + ag_token[0:1, 0:1], row(ln_in_b)
    lns = [ln1_g, ln1_b, ln2_g, ln2_b]

    h0, z3, gate, bu, q, k, v, kt, vt = _in_fwd(xs, tab, ln_g, ln_b, q_norm_g, kv_norm_g, w_inx, w_bb, w_q2, w_kv2)
    h = _scan_fwd(pw, bu)
    yl, s_out = _s5out_fwd(h, z3, s5_d, w_cb, full["w_glu"])
    o, lse_t = _attn_fwd(q, k, vt)
    g_late = dict(zip(late, _exchange_wait("all_gather_late_wait", ag_handle, after=lse_t)))
    full.update({n: g_late[n].reshape(-1, g_late[n].shape[2]) if kind[n] == "row" else _cols_from_blocks(g_late[n])
                 for n in late})
    w_oap = _oa_padded(full["w_oa"])
    memk, memv = _mem_kv(mems, full["w_xk"], full["w_xv"])
    r2, mixin_b, h1_b, ox_b, a_out, r1, qx_b = _post_fwd(o, s_out, gate, h0, memk, memv, lns,
                                        w_oap, full["w_o"], full["w_xq"], full["w_xo"])
    dh2, h2_b, dup_b, act_b, dff_b, loss_acc, d_ln3_g, d_ln3_b = _mlp(
        r2, tgt, ln2_g, ln2_b, ln3_g, ln3_b, full["w_up"], full["w_down"])
    (do_b, delta, d_so, dgate, dh0p, daout_b, dmix_b, dqx_b, dxa_b,
     dmk, dmv, d_ln1_g, d_ln1_b, d_ln2_g, d_ln2_b) = _post_bwd(
        dh2, o, s_out, gate, a_out, r1, r2, qx_b, memk, memv, lns, w_oap, full["w_o"], full["w_xq"], full["w_xo"])
    rows8 = lambda g: g.reshape(N_DEV, g.shape[0] // N_DEV, g.shape[1])
    d_w_oa = _wgrad("wg_oa", o, daout_b).reshape(MLA_HEADS, 128, 1024)[:, :64].reshape(512, 1024)
    send = {
        "w_oa": _blocks_from_cols(d_w_oa).astype(BF16),
        "w_up": _wgrad("wg_up", h2_b, dup_b, BF16, N_DEV),
        "w_o": rows8(_wgrad("wg_o", mixin_b, dmix_b, BF16)),
        "w_xq": rows8(_wgrad("wg_xq", h1_b, dqx_b, BF16)),
        "w_xk": rows8(_wgrad("wg_xk", mems, dmk, BF16)),
        "w_xv": rows8(_wgrad("wg_xv", mems, dmv, BF16)),
        "w_xo": rows8(_wgrad("wg_xo", ox_b, dxa_b, BF16)),
        "w_down": rows8(_wgrad("wg_down", act_b, dff_b, BF16)),
    }
    rs_handle, rs_token = _exchange_start("grad_exchange_late", [send[n] for n in late], False)
    dl_t = delta.T[:MLA_HEADS].reshape(MLA_HEADS, 1, s) + rs_token[0, 0]
    dq, dk, dv = _attn_bwd(q, k, v, kt, do_b, lse_t, dl_t)
    gh, du_p, yg_b, dy12_b, dyl_b, d_s5_d = _s5out_bwd(d_so, yl, z3, s5_d, w_cb, full["w_glu"])
    lam, d_a = _scan_bwd(pwb, gh, h)
    (dx, h0_b, dz_b, u_b, cqn_b, ckvn_b, dq2_b, dkv2_b, d_ln_g, d_ln_b, d_qg, d_kvg) = _in_bwd(
        xs, tab, z3, dq, dk, dv, lam, du_p, dgate, dh0p, ln_g, ln_b, q_norm_g, kv_norm_g, w_inx, w_bb, w_q2, w_kv2)

    d_w_in = _fold_w_in(_wgrad("wg_in", h0_b, dz_b))
    win_handle, win_token = _exchange_start("grad_exchange_w_in", [_blocks_from_cols(d_w_in).astype(BF16)], False)
    d_w_uq, d_w_ukv = _fold_qkv(_wgrad("wg_q", cqn_b, dq2_b, after=win_token), _wgrad("wg_kv", ckvn_b, dkv2_b))
    send.update({
        "w_uq": _blocks_from_cols(d_w_uq).astype(BF16), "w_ukv": _blocks_from_cols(d_w_ukv).astype(BF16),
        "w_glu": _wgrad("wg_glu", yg_b, dy12_b, BF16, N_DEV, after=win_token),
    })
    d_s5 = s5_vjp((d_a.reshape(2, NS), _wgrad("wg_s5b", u_b, lam, after=win_token),
                   _wgrad("wg_s5c", h, dyl_b, after=win_token)))
    small_grads = {
        "ln_in_g": d_ln_g, "ln_in_b": d_ln_b, "s5_lam_re": d_s5[0], "s5_lam_im": d_s5[1], "s5_log_dt": d_s5[2],
        "s5_b_re": d_s5[3], "s5_b_im": d_s5[4], "s5_c_re": d_s5[5], "s5_c_im": d_s5[6], "s5_d": d_s5_d,
        "q_norm_g": d_qg, "kv_norm_g": d_kvg, "ln1_g": d_ln1_g, "ln1_b": d_ln1_b, "ln2_g": d_ln2_g,
        "ln2_b": d_ln2_b, "ln3_g": d_ln3_g, "ln3_b": d_ln3_b,
    }
    small_send = jnp.broadcast_to(_pack_rows([small_grads[n] for n in SMALL] + [loss_acc[0, :1]], small_rows)[None],
                                  (N_DEV, small_rows, LANES))

    early_rest = [n for n in EARLY if n != "w_in"]
    recv_early = _all_to_all("grad_exchange", [send[n] for n in early_rest] + [small_send], False)
    recv_late = _exchange_wait("grad_exchange_late_wait", rs_handle, after=recv_early[-1])
    recv = dict(zip(early_rest + late, list(recv_early[:-1]) + recv_late))
    results = [dict(), dict(), dict(), dict()]
    for n in late + early_rest + ["w_in"]:
        if n == "w_in":
            recv[n] = _exchange_wait("grad_exchange_w_in_wait", win_handle, after=results[0][early_rest[-1]])[0]
        parts = recv[n]
        outs = _adamw("adamw_" + n, parts, wts[n][0], mom[n][0], vel[n][0])
        for res, a in zip(results, outs):
            res[n] = a[None]
    small_out = _adamw("adamw_small", recv_early[-1], *small_packs)
    for res, fs in zip(results, small_out):
        for n, a in zip(SMALL, _unpack_flat(fs, [wts[n].shape for n in SMALL])):
            res[n] = a

    loss = small_out[0].reshape(-1)[n_small]
    return (loss, dx[None], *[res[n] for res in results for n in WEIGHTS])
```

```python
import math

import jax
import jax.numpy as jnp
from jax import lax
from jax.experimental import pallas as pl
from jax.experimental.pallas import tpu as pltpu

F32 = jnp.float32
BF16 = jnp.bfloat16

D_MODEL = 1024
S5_WIDTH = 256
S5_GROUP_CH = 16
S5_GROUPS = 16
S5_STATE = 64
NS = S5_GROUPS * S5_STATE
S5_MAX_RE = -1e-4
MLA_HEADS = 8
MLA_NOPE = 64
MLA_ROPE = 32
MLA_QK = 96
MLA_V = 64
HEAD_PAD = 128
ROPE_THETA = 10000.0
XATTN_HEADS = 4
XATTN_HD = 256
MLP_HIDDEN = 4096
LN_EPS = 1e-5
RMS_EPS = 1e-6
NEG_INF = -1e30
LOG2E = 1.4426950408889634
DN_ALPHA = 2.0 ** 0.25
ADAM_LR = 0.001
ADAM_B1 = 0.9
ADAM_B2 = 0.999
ADAM_EPS = 1e-08
ADAM_WD = 0.01
ADAM_STEP = 10

N_DEV = 8
MESH = pl.DeviceIdType.MESH
LANES = 1024
VMEM_LIMIT = 60 * 1024 * 1024

ROW_TILE = 256
SCAN_TILE = 1024
ATT_TILE = 512
ADAM_TILE = 128

SHARDED = (
    ("w_in", "col", 1024, 2848), ("w_glu", "col", 256, 2048), ("w_uq", "col", 256, 768),
    ("w_ukv", "col", 256, 1024), ("w_oa", "col", 512, 1024), ("w_o", "row", 1024, 1024),
    ("w_xq", "row", 1024, 1024), ("w_xk", "row", 1024, 1024), ("w_xv", "row", 1024, 1024),
    ("w_xo", "row", 1024, 1024), ("w_up", "col", 1024, 4096), ("w_down", "row", 4096, 1024),
)
EARLY = ("w_in", "w_glu", "w_uq", "w_ukv")
SMALL = ("ln_in_g", "ln_in_b", "s5_lam_re", "s5_lam_im", "s5_log_dt", "s5_b_re", "s5_b_im", "s5_c_re",
         "s5_c_im", "s5_d", "q_norm_g", "kv_norm_g", "ln1_g", "ln1_b", "ln2_g", "ln2_b", "ln3_g", "ln3_b")
WEIGHTS = ("ln_in_g", "ln_in_b", "w_in", "s5_lam_re", "s5_lam_im", "s5_log_dt", "s5_b_re", "s5_b_im",
           "s5_c_re", "s5_c_im", "s5_d", "w_glu", "q_norm_g", "w_uq", "kv_norm_g", "w_ukv", "w_oa", "w_o",
           "ln1_g", "ln1_b", "w_xq", "w_xk", "w_xv", "w_xo", "ln2_g", "ln2_b", "w_up", "w_down", "ln3_g", "ln3_b")


def _round_up(n, m):
    return (n + m - 1) // m * m


def _bf(a):
    return a.astype(BF16)


def _mm(a, b):
    return jnp.dot(_bf(a), _bf(b), preferred_element_type=F32)


def _mm_nt(a, b):
    return lax.dot_general(_bf(a), _bf(b), (((1,), (1,)), ((), ())), preferred_element_type=F32)


def _mm_tn(a, b):
    return lax.dot_general(_bf(a), _bf(b), (((0,), (0,)), ((), ())), preferred_element_type=F32)


def _sigmoid(a):
    return 1.0 / (1.0 + jnp.exp(-a))


def _gelu(a):
    return 0.5 * a * (1.0 + lax.erf(a * (2.0 ** -0.5)))


def _gelu_grad(a):
    return 0.5 * (1.0 + lax.erf(a * (2.0 ** -0.5))) + a * jnp.exp(-0.5 * a * a) * (1.0 / math.sqrt(2.0 * math.pi))


def _ln_fwd(a, g, b):
    mu = jnp.mean(a, axis=-1, keepdims=True)
    ac = a - mu
    var = jnp.mean(ac * ac, axis=-1, keepdims=True)
    rstd = lax.rsqrt(var + LN_EPS)
    xhat = ac * rstd
    return xhat * g + b, xhat, rstd


def _ln_bwd(dy, xhat, rstd, g):
    dxh = dy * g
    m1 = jnp.mean(dxh, axis=-1, keepdims=True)
    m2 = jnp.mean(dxh * xhat, axis=-1, keepdims=True)
    dx = rstd * (dxh - m1 - xhat * m2)
    return dx, jnp.sum(dy * xhat, axis=0, keepdims=True), jnp.sum(dy, axis=0, keepdims=True)


def _rms_fwd(a, g):
    r = lax.rsqrt(jnp.mean(a * a, axis=-1, keepdims=True) + RMS_EPS)
    xn = a * r
    return xn * g, xn, r


def _rms_bwd(dy, xn, r, g):
    dxn = dy * g
    dx = r * (dxn - xn * jnp.mean(dxn * xn, axis=-1, keepdims=True))
    return dx, jnp.sum(dy * xn, axis=0, keepdims=True)


def _tile_heads(a):
    return jnp.concatenate([a] * MLA_HEADS, axis=1)


def _row_call(name, body, n_rows, ts, tiled_in, full_in, weights, tiled_out, acc_out, reverse=False, scratch=(),
              tiled_out_t=()):
    n = n_rows // ts
    assert n * ts == n_rows, (name, n_rows, ts)
    nt, nf, nw = len(tiled_in), len(full_in), len(weights)
    nto, nao = len(tiled_out) + len(tiled_out_t), len(acc_out)
    if reverse:
        imap = lambda i: (n - 1 - i, 0)
    else:
        imap = lambda i: (i, 0)
    const = lambda i: (0, 0)

    def kern(*refs):
        ins = refs[:nt + nf]
        w_hbm = refs[nt + nf:nt + nf + nw]
        outs = refs[nt + nf + nw:nt + nf + nw + nto + nao]
        scr = refs[nt + nf + nw + nto + nao:]
        w_vmem = scr[:nw]
        extra = scr[nw + 1:] if nw else scr
        if nw:
            sem = scr[nw]

            @pl.when(pl.program_id(0) == 0)
            def _():
                cps = [pltpu.make_async_copy(w_hbm[k], w_vmem[k], sem.at[k]) for k in range(nw)]
                for cp in cps:
                    cp.start()
                for cp in cps:
                    cp.wait()
        body(*ins, *w_vmem, *outs, *extra)

    in_specs = [pl.BlockSpec((ts, a.shape[1]), imap) for a in tiled_in]
    in_specs += [pl.BlockSpec(a.shape, const) for a in full_in]
    in_specs += [pl.BlockSpec(memory_space=pl.ANY) for _ in weights]
    assert not (reverse and tiled_out_t)
    out_shape = [jax.ShapeDtypeStruct((n_rows, c), dt) for (c, dt) in tiled_out]
    out_shape += [jax.ShapeDtypeStruct((c, n_rows), dt) for (c, dt) in tiled_out_t]
    out_shape += [jax.ShapeDtypeStruct(shp, F32) for shp in acc_out]
    out_specs = [pl.BlockSpec((ts, c), imap) for (c, dt) in tiled_out]
    out_specs += [pl.BlockSpec((c, ts), lambda i: (0, i)) for (c, dt) in tiled_out_t]
    out_specs += [pl.BlockSpec(shp, const) for shp in acc_out]
    scratch_shapes = [pltpu.VMEM(w.shape, w.dtype) for w in weights]
    if nw:
        scratch_shapes.append(pltpu.SemaphoreType.DMA((nw,)))
    scratch_shapes += list(scratch)
    return pl.pallas_call(
        kern, name=name, grid=(n,), in_specs=in_specs, out_specs=out_specs, out_shape=out_shape,
        scratch_shapes=scratch_shapes,
        compiler_params=pltpu.CompilerParams(dimension_semantics=("arbitrary",), vmem_limit_bytes=VMEM_LIMIT),
    )(*tiled_in, *full_in, *weights)


def _mem_kv(mem, w_xk, w_xv):
    m = mem.shape[0]

    def body(mem_ref, wk_ref, wv_ref, k_ref, v_ref):
        mb = mem_ref[...]
        k_ref[...] = _mm(mb, wk_ref[...]).astype(BF16)
        v_ref[...] = _mm(mb, wv_ref[...]).astype(BF16)

    return pl.pallas_call(
        body, name="mem_kv",
        out_shape=[jax.ShapeDtypeStruct((m, D_MODEL), BF16)] * 2,
        compiler_params=pltpu.CompilerParams(vmem_limit_bytes=VMEM_LIMIT),
    )(mem, w_xk, w_xv)


def _in_fwd(x, tab, ln_g, ln_b, qg, kvg, w_inx, w_b, w_q2, w_kv2):
    def body(x_ref, tab_ref, lng_ref, lnb_ref, qg_ref, kvg_ref, winx, wb, wq2, wkv2,
             h0_ref, z3_ref, gate_ref, bu_ref, q_ref, k_ref, v_ref, kt_ref, vt_ref):
        h0, _, _ = _ln_fwd(x_ref[...], lng_ref[...], lnb_ref[...])
        h0_ref[...] = h0
        z = _mm(h0, winx[...])
        z3_ref[...] = z[:, :768]
        gate_ref[...] = z[:, 1024:].astype(BF16)
        bu_ref[...] = _mm(z[:, :256], wb[...])
        tab_v = tab_ref[...]
        cq1, cq2, ck1 = tab_v[:, :128], tab_v[:, 128:], tab_v[:, :128]
        cqn, _, _ = _rms_fwd(z[:, 256:512], qg_ref[...])
        q2 = _mm(cqn, wq2[...])
        q_ref[...] = (q2[:, :1024] * _tile_heads(cq1) + q2[:, 1024:] * _tile_heads(cq2)).astype(BF16)
        ckvn, _, _ = _rms_fwd(z[:, 512:768], kvg_ref[...])
        kv2 = _mm(ckvn, wkv2[...])
        krp = z[:, 768:896] * ck1 + z[:, 896:1024] * cq2
        kf = kv2[:, :1024] + _tile_heads(krp)
        k_ref[...] = kf.astype(BF16)
        v_ref[...] = kv2[:, 1024:].astype(BF16)
        kt_ref[...] = kf.T.astype(BF16)
        vt_ref[...] = kv2[:, 1024:].T.astype(BF16)

    s = x.shape[0]
    return _row_call(
        "in_fwd", body, s, min(ROW_TILE, s), [x, tab], [ln_g, ln_b, qg, kvg], [w_inx, w_b, w_q2, w_kv2],
        [(1024, F32), (768, F32), (2048, BF16), (2048, F32), (1024, BF16), (1024, BF16), (1024, BF16)], [],
        tiled_out_t=[(1024, BF16), (1024, BF16)])


def _scan_coeffs(pw_ref, rows, forward):
    row = lax.broadcasted_iota(jnp.int32, (8, NS), 0)
    out = []
    for k, r in zip((1, 2, 4), rows):
        keep = (row >= k) if forward else (row < 8 - k)
        out.append((jnp.where(keep, pw_ref[r:r + 1, :NS], 0.0), jnp.where(keep, pw_ref[r:r + 1, NS:], 0.0), k))
    return out


def _scan_level(xr, xi, coeff, forward):
    ar, ai, k = coeff
    shift = k if forward else 8 - k
    sr, si = pltpu.roll(xr, shift, 0), pltpu.roll(xi, shift, 0)
    return xr + ar * sr - ai * si, xi + ar * si + ai * sr


def _scan_fwd(pw, bu):
    s = bu.shape[0]
    ts = min(SCAN_TILE, s)
    nblk = ts // 8

    def body(bu_ref, pw_ref, h_ref, carry):
        @pl.when(pl.program_id(0) == 0)
        def _():
            carry[...] = jnp.zeros_like(carry)

        coeffs = _scan_coeffs(pw_ref, (0, 1, 3), True)

        def blk(b, _):
            r0 = pl.multiple_of(b * 8, 8)
            xr = bu_ref[pl.ds(r0, 8), :NS]
            xi = bu_ref[pl.ds(r0, 8), NS:]
            for coeff in coeffs:
                xr, xi = _scan_level(xr, xi, coeff, True)
            cr, ci = carry[7:8, :NS], carry[7:8, NS:]
            pr, pi = pw_ref[:, :NS], pw_ref[:, NS:]
            hr = xr + pr * cr - pi * ci
            hi = xi + pr * ci + pi * cr
            h_ref[pl.ds(r0, 8), :NS] = hr
            h_ref[pl.ds(r0, 8), NS:] = hi
            carry[:, :NS] = hr
            carry[:, NS:] = hi
            return 0

        lax.fori_loop(0, nblk, blk, 0)

    return _row_call("scan_fwd", body, s, ts, [bu], [pw], [], [(2 * NS, F32)], [],
                     scratch=[pltpu.VMEM((8, 2 * NS), F32)])[0]


def _scan_bwd(pwb, gh, h):
    s = gh.shape[0]
    ts = min(SCAN_TILE, s)
    nblk = ts // 8
    n_tiles = s // ts

    def body(g_ref, h_ref, pw_ref, lam_ref, da_ref, carry, acc):
        @pl.when(pl.program_id(0) == 0)
        def _():
            carry[...] = jnp.zeros_like(carry)
            acc[...] = jnp.zeros_like(acc)

        row = lax.broadcasted_iota(jnp.int32, (8, NS), 0)
        coeffs = _scan_coeffs(pw_ref, (7, 6, 4), False)

        def blk(bb, _):
            r0 = pl.multiple_of((nblk - 1 - bb) * 8, 8)
            xr = g_ref[pl.ds(r0, 8), :NS]
            xi = g_ref[pl.ds(r0, 8), NS:]
            for coeff in coeffs:
                xr, xi = _scan_level(xr, xi, coeff, False)
            cr, ci = carry[0:1, :NS], carry[0:1, NS:]
            pr, pi = pw_ref[:, :NS], pw_ref[:, NS:]
            lr = xr + pr * cr - pi * ci
            li = xi + pr * ci + pi * cr
            lam_ref[pl.ds(r0, 8), :NS] = lr
            lam_ref[pl.ds(r0, 8), NS:] = li
            nr = jnp.where(row < 7, pltpu.roll(lr, 7, 0), cr)
            ni = jnp.where(row < 7, pltpu.roll(li, 7, 0), ci)
            hr = h_ref[pl.ds(r0, 8), :NS]
            hi = h_ref[pl.ds(r0, 8), NS:]
            acc[:, :NS] += nr * hr + ni * hi
            acc[:, NS:] += ni * hr - nr * hi
            carry[:, :NS] = lr
            carry[:, NS:] = li
            return 0

        lax.fori_loop(0, nblk, blk, 0)

        @pl.when(pl.program_id(0) == n_tiles - 1)
        def _():
            da_ref[...] = jnp.sum(acc[...], axis=0, keepdims=True)

    return _row_call("scan_bwd", body, s, ts, [gh, h], [pwb], [], [(2 * NS, F32)], [(1, 2 * NS)], reverse=True,
                     scratch=[pltpu.VMEM((8, 2 * NS), F32), pltpu.VMEM((8, 2 * NS), F32)])


def _s5out_fwd(h, z3, d_skip, w_c, w_glu):
    def body(h_ref, z3_ref, d_ref, wc, wglu, yl_ref, so_ref):
        yl = _mm(h_ref[...], wc[...]) + d_ref[...] * z3_ref[:, :256]
        yl_ref[...] = yl
        y12 = _mm(_gelu(yl), wglu[...])
        so_ref[...] = y12[:, :1024] * _sigmoid(y12[:, 1024:])

    s = h.shape[0]
    return _row_call("s5out_fwd", body, s, min(ROW_TILE, s), [h, z3], [d_skip], [w_c, w_glu],
                     [(256, F32), (1024, F32)], [])


def _causal_mask_t(t):
    row = lax.broadcasted_iota(jnp.int32, (t, t), 0)
    col = lax.broadcasted_iota(jnp.int32, (t, t), 1)
    return row <= col


def _attn_fwd(q, k, vt):
    s = q.shape[0]
    t = min(ATT_TILE, s)
    nq = s // t
    scale = MLA_QK ** -0.5
    c2 = scale * LOG2E

    def body(q_ref, k_ref, vt_ref, o_ref, lse_ref, s0, s1):
        i = pl.program_id(1)
        qb = q_ref[...]

        def scores(kb, dst):
            st = _mm_nt(k_ref[pl.ds(pl.multiple_of(kb * t, t), t), :], qb)
            dst[...] = st
            return jnp.max(st, axis=0, keepdims=True)

        def update(kb, src, mt, state, masked):
            m, l, acc = state
            if masked:
                keep = _causal_mask_t(t)
                mt = jnp.max(jnp.where(keep, src[...], NEG_INF), axis=0, keepdims=True)
            m_new = jnp.maximum(m, mt)
            if masked:
                p = jnp.exp2((jnp.where(keep, src[...], NEG_INF) - m_new) * c2)
            else:
                p = jnp.exp2((src[...] - m_new) * c2)
            a = jnp.exp2((m - m_new) * c2)
            l = a * l + jnp.sum(p, axis=0, keepdims=True)
            acc = a * acc + _mm(vt_ref[:, pl.ds(pl.multiple_of(kb * t, t), t)], p)
            return m_new, l, acc

        def pair(jj, carry):
            mt_a, state = carry
            mt_b = scores(2 * jj + 1, s1)
            state = update(2 * jj, s0, mt_a, state, False)
            mt_a = scores(2 * jj + 2, s0)
            return mt_a, update(2 * jj + 1, s1, mt_b, state, False)

        def odd_tail(carry):
            mt_a, state = carry
            mt_b = scores(i, s1)
            state = update(i - 1, s0, mt_a, state, False)
            return update(i, s1, mt_b, state, True)

        def even_tail(carry):
            mt_a, state = carry
            return update(i, s0, mt_a, state, True)

        init = (jnp.full((1, t), NEG_INF, F32), jnp.zeros((1, t), F32), jnp.zeros((HEAD_PAD, t), F32))
        carry = lax.fori_loop(0, i // 2, pair, (scores(0, s0), init))
        m, l, acc = lax.cond(i % 2 == 1, odd_tail, even_tail, carry)
        o_ref[...] = (acc / l).T
        lse_ref[...] = m * scale + jnp.log(l)

    return pl.pallas_call(
        body, name="attn_fwd", grid=(MLA_HEADS, nq),
        in_specs=[pl.BlockSpec((t, HEAD_PAD), lambda h, i: (i, h)),
                  pl.BlockSpec((s, HEAD_PAD), lambda h, i: (0, h)),
                  pl.BlockSpec((HEAD_PAD, s), lambda h, i: (h, 0))],
        out_specs=[pl.BlockSpec((t, HEAD_PAD), lambda h, i: (i, h)),
                   pl.BlockSpec((None, 1, t), lambda h, i: (h, 0, i))],
        out_shape=[jax.ShapeDtypeStruct((s, MLA_HEADS * HEAD_PAD), F32),
                   jax.ShapeDtypeStruct((MLA_HEADS, 1, s), F32)],
        scratch_shapes=[pltpu.VMEM((t, t), F32)] * 2,
        compiler_params=pltpu.CompilerParams(dimension_semantics=("arbitrary", "arbitrary"),
                                             vmem_limit_bytes=VMEM_LIMIT),
    )(q, k, vt)


def _attn_bwd(q, k, v, kt, do, lse_t, dl_t):
    s = q.shape[0]
    t = min(ATT_TILE, s)
    nq = s // t
    scale = MLA_QK ** -0.5
    c2 = scale * LOG2E

    def body(q_ref, k_ref, v_ref, kt_ref, do_ref, lse_ref, dl_ref, dq_ref, dk_ref, dv_ref, dqt, s0, p0, s1, p1):
        j = pl.program_id(1)
        n = nq - 1 - j

        @pl.when(j == 0)
        def _():
            dqt[...] = jnp.zeros_like(dqt)

        kk = k_ref[...]
        vv = v_ref[...]
        ktb = kt_ref[:MLA_QK, :]

        def rows(m):
            return pl.ds(pl.multiple_of(jnp.where(m < n, j + 1 + m, j) * t, t), t)

        def first(m, sbuf, pbuf):
            r = rows(m)
            sbuf[...] = _mm_nt(kk, q_ref[r, :])
            pbuf[...] = _mm_nt(vv, do_ref[r, :])

        def finish(m, sbuf, pbuf, acc, masked):
            dk, dv = acc
            r = rows(m)
            pt = jnp.exp2(sbuf[...] * c2 - lse_ref[:, r] * LOG2E)
            if masked:
                pt = jnp.where(_causal_mask_t(t), pt, 0.0)
            dv = dv + _mm(pt, do_ref[r, :])
            dst = (pt * (pbuf[...] - dl_ref[:, r])).astype(BF16)
            dk = dk + _mm(dst, q_ref[r, :])
            dqt[:MLA_QK, r] += _mm(ktb, dst)
            return dk, dv

        def pair(jj, acc):
            first(2 * jj + 1, s1, p1)
            acc = finish(2 * jj, s0, p0, acc, False)
            first(2 * jj + 2, s0, p0)
            return finish(2 * jj + 1, s1, p1, acc, False)

        def odd_tail(acc):
            first(n, s1, p1)
            acc = finish(n - 1, s0, p0, acc, False)
            return finish(n, s1, p1, acc, True)

        def even_tail(acc):
            return finish(n, s0, p0, acc, True)

        zero = jnp.zeros((t, HEAD_PAD), F32)
        first(0, s0, p0)
        acc = lax.fori_loop(0, n // 2, pair, (zero, zero))
        dk, dv = lax.cond(n % 2 == 1, odd_tail, even_tail, acc)
        dk_ref[...] = (dk * scale).astype(BF16)
        dv_ref[...] = dv.astype(BF16)

        @pl.when(j == nq - 1)
        def _():
            for cc in range(nq):
                dq_ref[cc * t:(cc + 1) * t, :] = (dqt[:, cc * t:(cc + 1) * t].T * scale).astype(BF16)

    full = pl.BlockSpec((s, HEAD_PAD), lambda h, j: (0, h))
    tile = pl.BlockSpec((t, HEAD_PAD), lambda h, j: (j, h))
    stat = pl.BlockSpec((None, 1, s), lambda h, j: (h, 0, 0))
    return pl.pallas_call(
        body, name="attn_bwd", grid=(MLA_HEADS, nq),
        in_specs=[full, tile, tile, pl.BlockSpec((HEAD_PAD, t), lambda h, j: (h, j)), full, stat, stat],
        out_specs=[full, tile, tile],
        out_shape=[jax.ShapeDtypeStruct((s, MLA_HEADS * HEAD_PAD), BF16)] * 3,
        scratch_shapes=[pltpu.VMEM((HEAD_PAD, s), F32)] + [pltpu.VMEM((t, t), F32)] * 4,
        compiler_params=pltpu.CompilerParams(dimension_semantics=("arbitrary", "arbitrary"),
                                             vmem_limit_bytes=VMEM_LIMIT),
    )(q, k, v, kt, do, lse_t, dl_t)


def _xattn_probs(qxb, memk, hh):
    sl = slice(hh * XATTN_HD, (hh + 1) * XATTN_HD)
    sc = _mm_nt(qxb[:, sl], memk[:, sl]) * (XATTN_HD ** -0.5)
    e = jnp.exp(sc - jnp.max(sc, axis=1, keepdims=True))
    return e / jnp.sum(e, axis=1, keepdims=True)


def _post_forward(o, s_out, gate, h0, memk, memv, w_oa, w_o, w_xq, w_xo, g1, b1, g2, b2):
    a_out = _mm(o, w_oa)
    sg_s = _sigmoid(gate[:, :1024])
    sg_a = _sigmoid(gate[:, 1024:])
    mixin = sg_s * s_out + sg_a * a_out
    r1 = DN_ALPHA * h0 + _mm(mixin, w_o)
    h1, xh1, rs1 = _ln_fwd(r1, g1, b1)
    qxb = _mm(h1, w_xq).astype(BF16)
    ox = jnp.concatenate([_mm(_xattn_probs(qxb, memk, hh), memv[:, hh * XATTN_HD:(hh + 1) * XATTN_HD])
                          for hh in range(XATTN_HEADS)], axis=1)
    r2 = DN_ALPHA * h1 + _mm(ox, w_xo)
    return dict(a_out=a_out, mixin=mixin, r1=r1, h1=h1, qxb=qxb, ox=ox, r2=r2)


def _post_fwd(o, s_out, gate, h0, memk, memv, lns, w_oa, w_o, w_xq, w_xo):
    def body(o_ref, so_ref, gate_ref, h0_ref, mk_ref, mv_ref, g1, b1, g2, b2, woa, wo, wxq, wxo,
             r2_ref, mixin_ref, h1_ref, ox_ref, aout_ref, r1_ref, qx_ref):
        f = _post_forward(o_ref[...], so_ref[...], gate_ref[...].astype(F32), h0_ref[...], mk_ref[...], mv_ref[...],
                          woa[...], wo[...], wxq[...], wxo[...], g1[...], b1[...], g2[...], b2[...])
        r2_ref[...] = f["r2"]
        mixin_ref[...] = f["mixin"].astype(BF16)
        h1_ref[...] = f["h1"].astype(BF16)
        ox_ref[...] = f["ox"].astype(BF16)
        aout_ref[...] = f["a_out"]
        r1_ref[...] = f["r1"]
        qx_ref[...] = f["qxb"]

    s = o.shape[0]
    return _row_call("post_fwd", body, s, min(ROW_TILE, s), [o, s_out, gate, h0], [memk, memv, *lns],
                     [w_oa, w_o, w_xq, w_xo],
                     [(1024, F32), (1024, BF16), (1024, BF16), (1024, BF16), (1024, F32), (1024, F32), (1024, BF16)], [])


def _mlp(r2, target, g2, b2, g3, b3, w_up, w_down):
    cw = 1024
    n_chunk = MLP_HIDDEN // cw

    def body(r2_ref, tgt_ref, g2_ref, b2_ref, g3_ref, b3_ref, wup, wdn,
             dh2_ref, h2b_ref, dup_ref, act_ref, dff_ref, loss_ref, dg3_ref, db3_ref, up_scr):
        @pl.when(pl.program_id(0) == 0)
        def _():
            loss_ref[...] = jnp.zeros_like(loss_ref)
            dg3_ref[...] = jnp.zeros_like(dg3_ref)
            db3_ref[...] = jnp.zeros_like(db3_ref)

        h2, _, _ = _ln_fwd(r2_ref[...], g2_ref[...], b2_ref[...])
        h2b = h2.astype(BF16)
        h2b_ref[...] = h2b
        ff = jnp.zeros(h2.shape, F32)
        for c in range(n_chunk):
            sl = slice(c * cw, (c + 1) * cw)
            a = jnp.maximum(_mm(h2b, wup[:, sl]), 0.0)
            up_scr[:, sl] = a
            actb = (a * a).astype(BF16)
            act_ref[:, sl] = actb
            ff = ff + _mm(actb, wdn[sl, :])
        h3, xh3, rs3 = _ln_fwd(DN_ALPHA * h2 + ff, g3_ref[...], b3_ref[...])
        err = h3 - tgt_ref[...]
        loss_ref[...] += 0.5 * jnp.sum(err * err) * (1.0 / D_MODEL)
        dr3, dg3, db3 = _ln_bwd(err * (1.0 / D_MODEL), xh3, rs3, g3_ref[...])
        dg3_ref[...] += dg3
        db3_ref[...] += db3
        dffb = dr3.astype(BF16)
        dff_ref[...] = dffb
        dh2 = DN_ALPHA * dr3
        for c in range(n_chunk):
            sl = slice(c * cw, (c + 1) * cw)
            dupb = (_mm_nt(dffb, wdn[sl, :]) * (2.0 * up_scr[:, sl])).astype(BF16)
            dup_ref[:, sl] = dupb
            dh2 = dh2 + _mm_nt(dupb, wup[:, sl])
        dh2_ref[...] = dh2

    s = r2.shape[0]
    ts = min(ROW_TILE, s)
    return _row_call("mlp", body, s, ts, [r2, target], [g2, b2, g3, b3], [w_up, w_down],
                     [(1024, F32), (1024, BF16), (MLP_HIDDEN, BF16), (MLP_HIDDEN, BF16), (1024, BF16)],
                     [(8, 128), (1, 1024), (1, 1024)], scratch=[pltpu.VMEM((ts, MLP_HIDDEN), F32)])


def _post_bwd(dh2, o, s_out, gate, a_out, r1, r2, qx, memk, memv, lns, w_oa, w_o, w_xq, w_xo):
    def body(dh2_ref, o_ref, so_ref, gate_ref, aout_ref, r1_ref, r2_ref, qx_ref, mk_ref, mv_ref, g1, b1, g2, b2,
             woa, wo, wxq, wxo,
             do_ref, dl_ref, dso_ref, dgate_ref, dh0_ref, daout_ref, dmix_ref, dqx_ref, dxa_ref,
             dmk_ref, dmv_ref, dg1_ref, db1_ref, dg2_ref, db2_ref):
        @pl.when(pl.program_id(0) == 0)
        def _():
            for r in (dmk_ref, dmv_ref, dg1_ref, db1_ref, dg2_ref, db2_ref):
                r[...] = jnp.zeros_like(r)

        o = o_ref[...]
        s_out = so_ref[...]
        memk, memv = mk_ref[...], mv_ref[...]
        qxb = qx_ref[...]
        _, xh1, rs1 = _ln_fwd(r1_ref[...], g1[...], b1[...])
        _, xh2, rs2 = _ln_fwd(r2_ref[...], g2[...], b2[...])
        dr2, dg2, db2 = _ln_bwd(dh2_ref[...], xh2, rs2, g2[...])
        dg2_ref[...] += dg2
        db2_ref[...] += db2
        dxab = dr2.astype(BF16)
        dxa_ref[...] = dxab
        dox = _mm_nt(dxab, wxo[...])
        dqs = []
        for hh in range(XATTN_HEADS):
            sl = slice(hh * XATTN_HD, (hh + 1) * XATTN_HD)
            p = _xattn_probs(qxb, memk, hh)
            doxh = dox[:, sl].astype(BF16)
            dp = _mm_nt(doxh, memv[:, sl])
            ds = (p * (dp - jnp.sum(dp * p, axis=1, keepdims=True)) * (XATTN_HD ** -0.5)).astype(BF16)
            dqs.append(_mm(ds, memk[:, sl]))
            dmk_ref[:, sl] += _mm_tn(ds, qxb[:, sl])
            dmv_ref[:, sl] += _mm_tn(p, doxh)
        dqxb = jnp.concatenate(dqs, axis=1).astype(BF16)
        dqx_ref[...] = dqxb
        dh1 = DN_ALPHA * dr2 + _mm_nt(dqxb, wxq[...])
        dr1, dg1, db1 = _ln_bwd(dh1, xh1, rs1, g1[...])
        dg1_ref[...] += dg1
        db1_ref[...] += db1
        dh0_ref[...] = DN_ALPHA * dr1
        dmixb = dr1.astype(BF16)
        dmix_ref[...] = dmixb
        dmixin = _mm_nt(dmixb, wo[...])
        sg_s = _sigmoid(gate_ref[:, :1024].astype(F32))
        sg_a = _sigmoid(gate_ref[:, 1024:].astype(F32))
        dso_ref[...] = dmixin * sg_s
        daoutb = (dmixin * sg_a).astype(BF16)
        daout_ref[...] = daoutb
        dgate_ref[:, :1024] = (dmixin * s_out * sg_s * (1.0 - sg_s)).astype(BF16)
        dgate_ref[:, 1024:] = (dmixin * aout_ref[...] * sg_a * (1.0 - sg_a)).astype(BF16)
        d_o = _mm_nt(daoutb, woa[...])
        do_ref[...] = d_o.astype(BF16)
        lane = lax.broadcasted_iota(jnp.int32, (o.shape[0], HEAD_PAD), 1)
        dlc = jnp.zeros((o.shape[0], HEAD_PAD), F32)
        for hh in range(MLA_HEADS):
            sl = slice(hh * HEAD_PAD, (hh + 1) * HEAD_PAD)
            dl = jnp.sum(d_o[:, sl] * o[:, sl], axis=1, keepdims=True)
            dlc = dlc + jnp.where(lane == hh, dl, 0.0)
        dl_ref[...] = dlc

    s = o.shape[0]
    m = memk.shape[0]
    return _row_call(
        "post_bwd", body, s, min(ROW_TILE, s), [dh2, o, s_out, gate, a_out, r1, r2, qx], [memk, memv, *lns],
        [w_oa, w_o, w_xq, w_xo],
        [(1024, BF16), (HEAD_PAD, F32), (1024, F32), (2048, BF16), (1024, F32),
         (1024, BF16), (1024, BF16), (1024, BF16), (1024, BF16)],
        [(m, 1024), (m, 1024), (1, 1024), (1, 1024), (1, 1024), (1, 1024)])


def _s5out_bwd(d_so, yl, z3, d_skip, w_c, w_glu):
    def body(dso_ref, yl_ref, z3_ref, d_ref, wc, wglu, gh_ref, dup_ref, yg_ref, dy12_ref, dyl_ref, dd_ref):
        @pl.when(pl.program_id(0) == 0)
        def _():
            dd_ref[...] = jnp.zeros_like(dd_ref)

        yl = yl_ref[...]
        ygb = _gelu(yl).astype(BF16)
        yg_ref[...] = ygb
        y12 = _mm(ygb, wglu[...])
        sg = _sigmoid(y12[:, 1024:])
        dso = dso_ref[...]
        dy12b = jnp.concatenate([dso * sg, dso * y12[:, :1024] * sg * (1.0 - sg)], axis=1).astype(BF16)
        dy12_ref[...] = dy12b
        dyl = _mm_nt(dy12b, wglu[...]) * _gelu_grad(yl)
        dylb = dyl.astype(BF16)
        dyl_ref[...] = dylb
        gh_ref[...] = _mm_nt(dylb, wc[...])
        dup_ref[...] = dyl * d_ref[...]
        dd_ref[...] += jnp.sum(dyl * z3_ref[:, :256], axis=0, keepdims=True)

    s = d_so.shape[0]
    return _row_call("s5out_bwd", body, s, min(ROW_TILE, s), [d_so, yl, z3], [d_skip], [w_c, w_glu],
                     [(2 * NS, F32), (256, F32), (256, BF16), (2048, BF16), (256, BF16)], [(1, 256)])


def _in_bwd(x, tab, z3, dq, dk, dv, lam, du_p, dgate, dh0p, ln_g, ln_b, qg, kvg, w_inx, w_b, w_q2, w_kv2):
    def body(x_ref, tab_ref, z3_ref, dq_ref, dk_ref, dv_ref, lam_ref, dup_ref, dgate_ref, dh0p_ref,
             lng_ref, lnb_ref, qg_ref, kvg_ref, winx, wb, wq2, wkv2,
             dx_ref, h0b_ref, dz_ref, ub_ref, cqn_ref, ckvn_ref, dq2_ref, dkv2_ref,
             dlng_ref, dlnb_ref, dqg_ref, dkvg_ref):
        @pl.when(pl.program_id(0) == 0)
        def _():
            for r in (dlng_ref, dlnb_ref, dqg_ref, dkvg_ref):
                r[...] = jnp.zeros_like(r)

        h0, xh0, rs0 = _ln_fwd(x_ref[...], lng_ref[...], lnb_ref[...])
        h0b_ref[...] = h0.astype(BF16)
        z3 = z3_ref[...]
        ub_ref[...] = z3[:, :256].astype(BF16)
        tab_v = tab_ref[...]
        cq1, cq2, ck1 = tab_v[:, :128], tab_v[:, 128:], tab_v[:, :128]
        du = _mm_nt(lam_ref[...], wb[...]) + dup_ref[...]
        dq = dq_ref[...].astype(F32)
        dq2b = jnp.concatenate([dq * _tile_heads(cq1), dq * _tile_heads(cq2)], axis=1).astype(BF16)
        dq2_ref[...] = dq2b
        cqg, cqn, rq = _rms_fwd(z3[:, 256:512], qg_ref[...])
        cqn_ref[...] = cqg.astype(BF16)
        dcq, dqg = _rms_bwd(_mm_nt(dq2b, wq2[...]), cqn, rq, qg_ref[...])
        dqg_ref[...] += dqg
        dkv2b = jnp.concatenate([dk_ref[...], dv_ref[...]], axis=1)
        dk = dk_ref[...].astype(F32)
        dkv2_ref[...] = dkv2b
        ckvg, ckvn, rkv = _rms_fwd(z3[:, 512:768], kvg_ref[...])
        ckvn_ref[...] = ckvg.astype(BF16)
        dckv, dkvg = _rms_bwd(_mm_nt(dkv2b, wkv2[...]), ckvn, rkv, kvg_ref[...])
        dkvg_ref[...] += dkvg
        dkrp = dk[:, :HEAD_PAD]
        for hh in range(1, MLA_HEADS):
            dkrp = dkrp + dk[:, hh * HEAD_PAD:(hh + 1) * HEAD_PAD]
        dzb = jnp.concatenate([a.astype(BF16) for a in (du, dcq, dckv, dkrp * ck1, dkrp * cq2)] + [dgate_ref[...]],
                              axis=1)
        dz_ref[...] = dzb
        dh0 = _mm_nt(dzb, winx[...]) + dh0p_ref[...]
        dx, dg, db = _ln_bwd(dh0, xh0, rs0, lng_ref[...])
        dx_ref[...] = dx
        dlng_ref[...] += dg
        dlnb_ref[...] += db

    s = x.shape[0]
    return _row_call(
        "in_bwd", body, s, min(ROW_TILE, s), [x, tab, z3, dq, dk, dv, lam, du_p, dgate, dh0p],
        [ln_g, ln_b, qg, kvg], [w_inx, w_b, w_q2, w_kv2],
        [(1024, F32), (1024, BF16), (3072, BF16), (256, BF16), (256, BF16), (256, BF16), (2048, BF16), (2048, BF16)],
        [(1, 1024), (1, 1024), (1, 256), (1, 256)])


def _wgrad(name, xs, dy, out_dtype=F32, blocks=1, after=None):
    s, k = xs.shape
    n = dy.shape[1]
    nb = n // blocks
    tk = min(k, 1024)
    tn = max(d for d in range(128, min(n, 2048) + 1, 128) if n % d == 0)
    ts = min(s, 2048 if tk <= 256 else 1024)
    per_tile = tn // nb
    assert s % ts == 0 and k % tk == 0 and n % tn == 0 and (blocks == 1 or tn % nb == 0), (name, s, k, n)
    last = s // ts - 1

    def body(x_ref, dy_ref, *rest):
        o_ref, acc = rest[-2:]

        @pl.when(pl.program_id(2) == 0)
        def _():
            acc[...] = jnp.zeros_like(acc)

        acc[...] += _mm_tn(x_ref[...], dy_ref[...])

        @pl.when(pl.program_id(2) == last)
        def _():
            if blocks == 1:
                o_ref[...] = acc[...].astype(out_dtype)
            else:
                for d in range(per_tile):
                    o_ref[d] = acc[:, d * nb:(d + 1) * nb].astype(out_dtype)

    if blocks == 1:
        out_shape = jax.ShapeDtypeStruct((k, n), out_dtype)
        out_spec = pl.BlockSpec((tk, tn), lambda a, b, c: (a, b))
    else:
        out_shape = jax.ShapeDtypeStruct((blocks, k, nb), out_dtype)
        out_spec = pl.BlockSpec((per_tile, tk, nb), lambda a, b, c: (b, a, 0))
    return pl.pallas_call(
        body, name=name, grid=(k // tk, n // tn, s // ts),
        in_specs=[pl.BlockSpec((ts, tk), lambda a, b, c: (c, a)), pl.BlockSpec((ts, tn), lambda a, b, c: (c, b))]
        + ([] if after is None else [pl.BlockSpec(memory_space=pl.ANY)]),
        out_specs=out_spec, out_shape=out_shape, scratch_shapes=[pltpu.VMEM((tk, tn), F32)],
        compiler_params=pltpu.CompilerParams(dimension_semantics=("arbitrary", "arbitrary", "arbitrary"),
                                             vmem_limit_bytes=VMEM_LIMIT),
    )(xs, dy, *([] if after is None else [after]))


def _mesh_pos():
    x, y, c = lax.axis_index("x"), lax.axis_index("y"), lax.axis_index("c")
    return x, y, c


def _peer(x, y, c, k):
    px = 1 - x if k & 4 else x
    py = 1 - y if k & 2 else y
    pc = 1 - c if k & 1 else c
    return (px, py, pc), 4 * px + 2 * py + pc


def _all_to_all(name, arrays, gather):
    n = len(arrays)

    def body(*refs):
        src, dst = refs[:n], refs[n:2 * n]
        send_sems, recv_sems, local_sems = refs[2 * n:]
        x, y, c = _mesh_pos()
        me = 4 * x + 2 * y + c

        def block(i, d):
            return src[i] if gather else src[i].at[d]

        own = [pltpu.make_async_copy(block(i, me), dst[i].at[me], local_sems.at[i]) for i in range(n)]
        for cp in own:
            cp.start()
        sends = []
        for k in range(1, N_DEV):
            peer, pid = _peer(x, y, c, k)
            for i in range(n):
                idx = (k - 1) * n + i
                cp = pltpu.make_async_remote_copy(src_ref=block(i, pid), dst_ref=dst[i].at[me],
                                                  send_sem=send_sems.at[idx], recv_sem=recv_sems.at[idx],
                                                  device_id=peer, device_id_type=MESH)
                cp.start()
                sends.append(cp)
        for k in range(1, N_DEV):
            peer, pid = _peer(x, y, c, k)
            for i in range(n):
                idx = (k - 1) * n + i
                pltpu.make_async_remote_copy(src_ref=block(i, pid), dst_ref=dst[i].at[pid],
                                             send_sem=send_sems.at[idx], recv_sem=recv_sems.at[idx],
                                             device_id=peer, device_id_type=MESH).wait_recv()
        for cp in sends:
            cp.wait_send()
        for cp in own:
            cp.wait()

    n_sem = n * (N_DEV - 1)
    return pl.pallas_call(
        body, name=name,
        in_specs=[pl.BlockSpec(memory_space=pl.ANY)] * n,
        out_specs=[pl.BlockSpec(memory_space=pl.ANY)] * n,
        out_shape=[jax.ShapeDtypeStruct((N_DEV,) + a.shape[-2:], a.dtype) for a in arrays],
        scratch_shapes=[pltpu.SemaphoreType.DMA((n_sem,)), pltpu.SemaphoreType.DMA((n_sem,)),
                        pltpu.SemaphoreType.DMA((n,))],
    )(*arrays)


def _all_gather_two_level(name, arrays):
    n = len(arrays)

    def body(*refs):
        src, dst = refs[:n], refs[n:2 * n]
        send_sems, recv_sems, local_sems = refs[2 * n:]
        x, y, c = _mesh_pos()
        me = 4 * x + 2 * y + c
        sibling = (x, y, 1 - c)

        def copy(i, k, source, slot, to):
            return pltpu.make_async_remote_copy(src_ref=source, dst_ref=dst[i].at[slot],
                                                send_sem=send_sems.at[(k - 1) * n + i],
                                                recv_sem=recv_sems.at[(k - 1) * n + i],
                                                device_id=to, device_id_type=MESH)

        own = [pltpu.make_async_copy(src[i], dst[i].at[me], local_sems.at[i]) for i in range(n)]
        for cp in own:
            cp.start()
        sends = []
        for k in (1, 2, 4, 6):
            peer, _ = _peer(x, y, c, k)
            for i in range(n):
                sends.append(copy(i, k, src[i], me, peer))
                sends[-1].start()
        for k in (2, 4, 6):
            peer, pid = _peer(x, y, c, k)
            for i in range(n):
                copy(i, k, src[i], pid, peer).wait_recv()
                sends.append(copy(i, k + 1, dst[i].at[pid], pid, sibling))
                sends[-1].start()
        for k in (1, 3, 5, 7):
            peer, pid = _peer(x, y, c, k)
            for i in range(n):
                copy(i, k, src[i], pid, peer).wait_recv()
        for cp in sends:
            cp.wait_send()
        for cp in own:
            cp.wait()

    n_sem = n * (N_DEV - 1)
    return pl.pallas_call(
        body, name=name,
        in_specs=[pl.BlockSpec(memory_space=pl.ANY)] * n,
        out_specs=[pl.BlockSpec(memory_space=pl.ANY)] * n,
        out_shape=[jax.ShapeDtypeStruct((N_DEV,) + a.shape, a.dtype) for a in arrays],
        scratch_shapes=[pltpu.SemaphoreType.DMA((n_sem,)), pltpu.SemaphoreType.DMA((n_sem,)),
                        pltpu.SemaphoreType.DMA((n,))],
    )(*arrays)


_HBM = pl.BlockSpec(memory_space=pltpu.HBM)
_SEM = pl.BlockSpec(memory_space=pltpu.SEMAPHORE)


def _exchange_start(name, arrays, gather, after=None):
    n = len(arrays)
    n_sem = n * (N_DEV - 1)
    me = 4 * lax.axis_index("x") + 2 * lax.axis_index("y") + lax.axis_index("c")
    lands = []
    for a in arrays:
        own = a[None] if gather else lax.dynamic_slice_in_dim(a, me, 1, 0)
        lands.append(lax.dynamic_update_slice(lax.empty((N_DEV,) + a.shape[-2:], a.dtype), own, (me, 0, 0)))
    n_after = 0 if after is None else 1

    def body(*refs):
        src, land = refs[:n], refs[n:2 * n]
        send_sems, recv_sems = refs[2 * n + n_after], refs[2 * n + n_after + 1]
        token = refs[-1]
        x, y, c = _mesh_pos()
        me_in = 4 * x + 2 * y + c
        for k in range(1, N_DEV):
            peer, pid = _peer(x, y, c, k)
            for i in range(n):
                idx = (k - 1) * n + i
                pltpu.make_async_remote_copy(src_ref=src[i] if gather else src[i].at[pid], dst_ref=land[i].at[me_in],
                                             send_sem=send_sems.at[idx], recv_sem=recv_sems.at[idx],
                                             device_id=peer, device_id_type=MESH).start()
        token[...] = jnp.zeros_like(token)

    operands = [pltpu.with_memory_space_constraint(a, pltpu.HBM) for a in list(arrays) + lands]
    outs = pl.pallas_call(
        body, name=name,
        out_shape=(pltpu.SemaphoreType.DMA((n_sem,)), pltpu.SemaphoreType.DMA((n_sem,)),
                   *[pltpu.HBM(a.shape, a.dtype) for a in list(arrays) + lands],
                   jax.ShapeDtypeStruct((8, 128), F32)),
        in_specs=[_HBM] * (2 * n) + [pl.BlockSpec(memory_space=pl.ANY)] * n_after,
        out_specs=(_SEM, _SEM, *[_HBM] * (2 * n), pl.BlockSpec(memory_space=pltpu.VMEM)),
        input_output_aliases={i: 2 + i for i in range(2 * n)},
        compiler_params=pltpu.CompilerParams(has_side_effects=pltpu.SideEffectType.DATAFLOW_SIDE_EFFECTING),
    )(*operands, *([after] if n_after else []))
    return (gather, outs[0], outs[1], outs[2:2 + n], outs[2 + n:2 + 2 * n]), outs[-1]


def _exchange_wait(name, handle, after):
    gather, send_sems, recv_sems, srcs, lands = handle
    n = len(srcs)

    def body(*refs):
        src, land = refs[:n], refs[n:2 * n]
        s_sems, r_sems = refs[2 * n], refs[2 * n + 1]
        x, y, c = _mesh_pos()
        for k in range(1, N_DEV):
            peer, pid = _peer(x, y, c, k)
            for i in range(n):
                idx = (k - 1) * n + i
                cp = pltpu.make_async_remote_copy(src_ref=src[i] if gather else src[i].at[pid], dst_ref=land[i].at[pid],
                                                  send_sem=s_sems.at[idx], recv_sem=r_sems.at[idx],
                                                  device_id=peer, device_id_type=MESH)
                cp.wait_send()
                cp.wait_recv()

    outs = pl.pallas_call(
        body, name=name,
        out_shape=tuple(pltpu.HBM(a.shape, a.dtype) for a in list(srcs) + list(lands)),
        in_specs=[_HBM] * (2 * n) + [_SEM, _SEM, pl.BlockSpec(memory_space=pl.ANY)],
        out_specs=tuple([_HBM] * (2 * n)),
        input_output_aliases={i: i for i in range(2 * n)},
        compiler_params=pltpu.CompilerParams(has_side_effects=pltpu.SideEffectType.DATAFLOW_SIDE_EFFECTING),
    )(*srcs, *lands, send_sems, recv_sems, after)
    return list(outs[n:])


def _adamw(name, parts, w, m, v):
    a_rows, b_cols = w.shape
    ta = min(a_rows, ADAM_TILE)
    assert a_rows % ta == 0
    c1 = 1.0 - ADAM_B1 ** ADAM_STEP
    c2 = 1.0 - ADAM_B2 ** ADAM_STEP

    def body(p_ref, w_ref, m_ref, v_ref, g_ref, d_ref, mo_ref, vo_ref):
        g = p_ref[0].astype(F32)
        for d in range(1, N_DEV):
            g = g + p_ref[d].astype(F32)
        g_ref[...] = g
        mn = ADAM_B1 * m_ref[...] + (1.0 - ADAM_B1) * g
        vn = ADAM_B2 * v_ref[...] + (1.0 - ADAM_B2) * (g * g)
        mo_ref[...] = mn
        vo_ref[...] = vn
        d_ref[...] = -ADAM_LR * ((mn / c1) / (jnp.sqrt(vn / c2) + ADAM_EPS) + ADAM_WD * w_ref[...])

    row = pl.BlockSpec((ta, b_cols), lambda i: (i, 0))
    return pl.pallas_call(
        body, name=name, grid=(a_rows // ta,),
        in_specs=[pl.BlockSpec((N_DEV, ta, b_cols), lambda i: (0, i, 0)), row, row, row],
        out_specs=[row] * 4,
        out_shape=[jax.ShapeDtypeStruct((a_rows, b_cols), F32)] * 4,
        compiler_params=pltpu.CompilerParams(dimension_semantics=("arbitrary",), vmem_limit_bytes=VMEM_LIMIT),
    )(parts, w, m, v)


def _pack_rows(arrays, rows):
    flat = jnp.concatenate([a.reshape(-1) for a in arrays])
    return jnp.pad(flat, (0, rows * LANES - flat.shape[0])).reshape(rows, LANES)


def _cols_from_blocks(g):
    return g.transpose(1, 0, 2).reshape(g.shape[1], N_DEV * g.shape[2])


def _blocks_from_cols(w):
    return w.reshape(w.shape[0], N_DEV, w.shape[1] // N_DEV).transpose(1, 0, 2)


def _unpack_flat(flat2d, shapes):
    flat = flat2d.reshape(-1)
    out, off = [], 0
    for shp in shapes:
        sz = math.prod(shp)
        out.append(flat[off:off + sz].reshape(shp))
        off += sz
    return out


def _s5_mats(lam_re, lam_im, log_dt, b_re, b_im, c_re, c_im):
    lr = jnp.minimum(lam_re, S5_MAX_RE)
    li = lam_im
    dt = jnp.exp(log_dt)[:, None]
    mag = jnp.exp(lr * dt)
    ang = li * dt
    ab_re = mag * jnp.cos(ang)
    ab_im = mag * jnp.sin(ang)
    den = lr * lr + li * li
    nr = ab_re - 1.0
    f_re = ((nr * lr + ab_im * li) / den)[..., None]
    f_im = ((ab_im * lr - nr * li) / den)[..., None]
    bb_re = f_re * b_re - f_im * b_im
    bb_im = f_re * b_im + f_im * b_re
    eye = jnp.eye(S5_GROUPS, dtype=F32)
    a = jnp.stack([ab_re.reshape(-1), ab_im.reshape(-1)])
    wb = jnp.concatenate([jnp.einsum("gph,gk->ghkp", bb_re, eye).reshape(S5_WIDTH, NS),
                          jnp.einsum("gph,gk->ghkp", bb_im, eye).reshape(S5_WIDTH, NS)], axis=1)
    wc = jnp.concatenate([jnp.einsum("ghp,gk->gpkh", c_re, eye).reshape(NS, S5_WIDTH),
                          -jnp.einsum("ghp,gk->gpkh", c_im, eye).reshape(NS, S5_WIDTH)], axis=0)
    return a, wb, wc


def _power_table(a):
    ar, ai = a[0], a[1]
    rows_r, rows_i = [ar], [ai]
    for _ in range(7):
        pr, pi = rows_r[-1], rows_i[-1]
        rows_r.append(pr * ar - pi * ai)
        rows_i.append(pr * ai + pi * ar)
    return jnp.concatenate([jnp.stack(rows_r), jnp.stack(rows_i)], axis=1)


def _rope_table(positions):
    inv = ROPE_THETA ** (-jnp.arange(0, MLA_ROPE, 2, dtype=F32) / MLA_ROPE)
    inv128 = jnp.concatenate([jnp.zeros((MLA_NOPE,), F32), inv, inv, jnp.zeros((32,), F32)])
    sign = jnp.concatenate([jnp.zeros((MLA_NOPE,), F32), -jnp.ones((16,), F32), jnp.ones((16,), F32),
                            jnp.zeros((32,), F32)])
    ang = positions.astype(F32)[:, None] * inv128
    return jnp.concatenate([jnp.cos(ang), jnp.sin(ang) * sign], axis=1)


def _derived_weights(full):
    w_in = full["w_in"]
    k1, k2 = w_in[:, 768:784], w_in[:, 784:800]
    z64, z32 = jnp.zeros((1024, 64), BF16), jnp.zeros((1024, 32), BF16)
    w_inx = jnp.concatenate([w_in[:, :768], z64, k1, k2, z32, z64, k2, k1, z32, w_in[:, 800:]], axis=1)
    uq = full["w_uq"].reshape(256, MLA_HEADS, MLA_QK)
    nope, r1, r2 = uq[:, :, :64], uq[:, :, 64:80], uq[:, :, 80:]
    zq64, zq32 = jnp.zeros((256, MLA_HEADS, 64), BF16), jnp.zeros((256, MLA_HEADS, 32), BF16)
    w_q2 = jnp.concatenate([jnp.concatenate([nope, r1, r2, zq32], axis=2).reshape(256, 1024),
                            jnp.concatenate([zq64, r2, r1, zq32], axis=2).reshape(256, 1024)], axis=1)
    ukv = full["w_ukv"].reshape(256, MLA_HEADS, 128)
    w_kv2 = jnp.concatenate([jnp.concatenate([ukv[:, :, :64], zq64], axis=2).reshape(256, 1024),
                             jnp.concatenate([ukv[:, :, 64:], zq64], axis=2).reshape(256, 1024)], axis=1)
    return w_inx, w_q2, w_kv2


def _oa_padded(w_oa):
    oa = w_oa.reshape(MLA_HEADS, MLA_V, 1024)
    return jnp.concatenate([oa, jnp.zeros_like(oa)], axis=1).reshape(1024, 1024)


def _fold_w_in(d_inx):
    d_k1 = d_inx[:, 832:848] + d_inx[:, 976:992]
    d_k2 = d_inx[:, 848:864] + d_inx[:, 960:976]
    return jnp.concatenate([d_inx[:, :768], d_k1, d_k2, d_inx[:, 1024:]], axis=1)


def _fold_qkv(d_q2, d_kv2):
    a = d_q2[:, :1024].reshape(256, MLA_HEADS, 128)
    b = d_q2[:, 1024:].reshape(256, MLA_HEADS, 128)
    d_w_uq = jnp.concatenate([a[:, :, :64], a[:, :, 64:80] + b[:, :, 80:96], a[:, :, 80:96] + b[:, :, 64:80]],
                             axis=2).reshape(256, MLA_HEADS * MLA_QK)
    kk = d_kv2[:, :1024].reshape(256, MLA_HEADS, 128)
    vv = d_kv2[:, 1024:].reshape(256, MLA_HEADS, 128)
    d_w_ukv = jnp.concatenate([kk[:, :, :64], vv[:, :, :64]], axis=2).reshape(256, 1024)
    return d_w_uq, d_w_ukv


def kernel(x, mem, positions, ln_in_g, ln_in_b, w_in, s5_lam_re, s5_lam_im, s5_log_dt, s5_b_re, s5_b_im, s5_c_re, s5_c_im, s5_d, w_glu, q_norm_g, w_uq, kv_norm_g, w_ukv, w_oa, w_o, ln1_g, ln1_b, w_xq, w_xk, w_xv, w_xo, ln2_g, ln2_b, w_up, w_down, ln3_g, ln3_b, loss_target, m_ln_in_g, m_ln_in_b, m_w_in, m_s5_lam_re, m_s5_lam_im, m_s5_log_dt, m_s5_b_re, m_s5_b_im, m_s5_c_re, m_s5_c_im, m_s5_d, m_w_glu, m_q_norm_g, m_w_uq, m_kv_norm_g, m_w_ukv, m_w_oa, m_w_o, m_ln1_g, m_ln1_b, m_w_xq, m_w_xk, m_w_xv, m_w_xo, m_ln2_g, m_ln2_b, m_w_up, m_w_down, m_ln3_g, m_ln3_b, v_ln_in_g, v_ln_in_b, v_w_in, v_s5_lam_re, v_s5_lam_im, v_s5_log_dt, v_s5_b_re, v_s5_b_im, v_s5_c_re, v_s5_c_im, v_s5_d, v_w_glu, v_q_norm_g, v_w_uq, v_kv_norm_g, v_w_ukv, v_w_oa, v_w_o, v_ln1_g, v_ln1_b, v_w_xq, v_w_xk, v_w_xv, v_w_xo, v_ln2_g, v_ln2_b, v_w_up, v_w_down, v_ln3_g, v_ln3_b):
    args = dict(locals())
    wts = {n: args[n] for n in WEIGHTS}
    mom = {n: args["m_" + n] for n in WEIGHTS}
    vel = {n: args["v_" + n] for n in WEIGHTS}
    xs, mems, tgt = x[0], mem[0], loss_target[0]
    s = xs.shape[0]
    names_big = [n for (n, _, _, _) in SHARDED]
    kind = {n: kd for (n, kd, _, _) in SHARDED}
    n_small = sum(math.prod(wts[n].shape) for n in SMALL)
    small_rows = _round_up(n_small + 1, 8 * LANES) // LANES

    late = [n for n in names_big if n not in EARLY]
    shard = {n: wts[n][0].astype(BF16) for n in names_big}
    g_early = dict(zip(EARLY, _all_gather_two_level("all_gather", [shard[n] for n in EARLY])))
    ag_handle, ag_token = _exchange_start("all_gather_late", [shard[n] for n in late], True, after=g_early["w_in"])
    full = {n: _cols_from_blocks(g_early[n]) for n in EARLY}
    w_inx, w_q2, w_kv2 = _derived_weights(full)
    s5_args = (s5_lam_re[0], s5_lam_im[0], s5_log_dt[0], s5_b_re[0], s5_b_im[0], s5_c_re[0], s5_c_im[0])
    (a_mat, w_b, w_c), s5_vjp = jax.vjp(_s5_mats, *s5_args)
    w_bb, w_cb = w_b.astype(BF16), w_c.astype(BF16)
    pw = _power_table(a_mat)
    pwb = _power_table(a_mat * jnp.array([[1.0], [-1.0]], F32))[::-1]
    tab = _rope_table(positions[0])
    row = lambda a: a.reshape(1, -1)
    ln_g, ln_b = row(ln_in_g) + ag_token[0:1, 0:1], row(ln_in_b)
    lns = [ln1_g, ln1_b, ln2_g, ln2_b]

    h0, z3, gate, bu, q, k, v, kt, vt = _in_fwd(xs, tab, ln_g, ln_b, q_norm_g, kv_norm_g, w_inx, w_bb, w_q2, w_kv2)
    h = _scan_fwd(pw, bu)
    yl, s_out = _s5out_fwd(h, z3, s5_d, w_cb, full["w_glu"])
    o, lse_t = _attn_fwd(q, k, vt)
    g_late = dict(zip(late, _exchange_wait("all_gather_late_wait", ag_handle, after=lse_t)))
    full.update({n: g_late[n].reshape(-1, g_late[n].shape[2]) if kind[n] == "row" else _cols_from_blocks(g_late[n])
                 for n in late})
    w_oap = _oa_padded(full["w_oa"])
    memk, memv = _mem_kv(mems, full["w_xk"], full["w_xv"])
    r2, mixin_b, h1_b, ox_b, a_out, r1, qx_b = _post_fwd(o, s_out, gate, h0, memk, memv, lns,
                                        w_oap, full["w_o"], full["w_xq"], full["w_xo"])
    dh2, h2_b, dup_b, act_b, dff_b, loss_acc, d_ln3_g, d_ln3_b = _mlp(
        r2, tgt, ln2_g, ln2_b, ln3_g, ln3_b, full["w_up"], full["w_down"])
    (do_b, delta, d_so, dgate, dh0p, daout_b, dmix_b, dqx_b, dxa_b,
     dmk, dmv, d_ln1_g, d_ln1_b, d_ln2_g, d_ln2_b) = _post_bwd(
        dh2, o, s_out, gate, a_out, r1, r2, qx_b, memk, memv, lns, w_oap, full["w_o"], full["w_xq"], full["w_xo"])
    rows8 = lambda g: g.reshape(N_DEV, g.shape[0] // N_DEV, g.shape[1])
    d_w_oa = _wgrad("wg_oa", o, daout_b).reshape(MLA_HEADS, 128, 1024)[:, :64].reshape(512, 1024)
    send = {
        "w_oa": _blocks_from_cols(d_w_oa).astype(BF16),
        "w_up": _wgrad("wg_up", h2_b, dup_b, BF16, N_DEV),
        "w_o": rows8(_wgrad("wg_o", mixin_b, dmix_b, BF16)),
        "w_xq": rows8(_wgrad("wg_xq", h1_b, dqx_b, BF16)),
        "w_xk": rows8(_wgrad("wg_xk", mems, dmk, BF16)),
        "w_xv": rows8(_wgrad("wg_xv", mems, dmv, BF16)),
        "w_xo": rows8(_wgrad("wg_xo", ox_b, dxa_b, BF16)),
        "w_down": rows8(_wgrad("wg_down", act_b, dff_b, BF16)),
    }
    rs_handle, rs_token = _exchange_start("grad_exchange_late", [send[n] for n in late], False)
    dl_t = delta.T[:MLA_HEADS].reshape(MLA_HEADS, 1, s) + rs_token[0, 0]
    dq, dk, dv = _attn_bwd(q, k, v, kt, do_b, lse_t, dl_t)
    gh, du_p, yg_b, dy12_b, dyl_b, d_s5_d = _s5out_bwd(d_so, yl, z3, s5_d, w_cb, full["w_glu"])
    lam, d_a = _scan_bwd(pwb, gh, h)
    (dx, h0_b, dz_b, u_b, cqn_b, ckvn_b, dq2_b, dkv2_b, d_ln_g, d_ln_b, d_qg, d_kvg) = _in_bwd(
        xs, tab, z3, dq, dk, dv, lam, du_p, dgate, dh0p, ln_g, ln_b, q_norm_g, kv_norm_g, w_inx, w_bb, w_q2, w_kv2)

    d_w_in = _fold_w_in(_wgrad("wg_in", h0_b, dz_b))
    win_handle, win_token = _exchange_start("grad_exchange_w_in", [_blocks_from_cols(d_w_in).astype(BF16)], False)
    d_w_uq, d_w_ukv = _fold_qkv(_wgrad("wg_q", cqn_b, dq2_b, after=win_token), _wgrad("wg_kv", ckvn_b, dkv2_b))
    send.update({
        "w_uq": _blocks_from_cols(d_w_uq).astype(BF16), "w_ukv": _blocks_from_cols(d_w_ukv).astype(BF16),
        "w_glu": _wgrad("wg_glu", yg_b, dy12_b, BF16, N_DEV, after=win_token),
    })
    d_s5 = s5_vjp((d_a.reshape(2, NS), _wgrad("wg_s5b", u_b, lam, after=win_token),
                   _wgrad("wg_s5c", h, dyl_b, after=win_token)))
    small_grads = {
        "ln_in_g": d_ln_g, "ln_in_b": d_ln_b, "s5_lam_re": d_s5[0], "s5_lam_im": d_s5[1], "s5_log_dt": d_s5[2],
        "s5_b_re": d_s5[3], "s5_b_im": d_s5[4], "s5_c_re": d_s5[5], "s5_c_im": d_s5[6], "s5_d": d_s5_d,
        "q_norm_g": d_qg, "kv_norm_g": d_kvg, "ln1_g": d_ln1_g, "ln1_b": d_ln1_b, "ln2_g": d_ln2_g,
        "ln2_b": d_ln2_b, "ln3_g": d_ln3_g, "ln3_b": d_ln3_b,
    }
    small_send = jnp.broadcast_to(_pack_rows([small_grads[n] for n in SMALL] + [loss_acc[0, :1]], small_rows)[None],
                                  (N_DEV, small_rows, LANES))

    early_rest = [n for n in EARLY if n != "w_in"]
    recv_early = _all_to_all("grad_exchange", [send[n] for n in early_rest] + [small_send], False)
    recv_late = _exchange_wait("grad_exchange_late_wait", rs_handle, after=recv_early[-1])
    recv = dict(zip(early_rest + late, list(recv_early[:-1]) + recv_late))
    results = [dict(), dict(), dict(), dict()]
    for n in late + early_rest + ["w_in"]:
        if n == "w_in":
            recv[n] = _exchange_wait("grad_exchange_w_in_wait", win_handle, after=results[0][early_rest[-1]])[0]
        parts = recv[n]
        outs = _adamw("adamw_" + n, parts, wts[n][0], mom[n][0], vel[n][0])
        for res, a in zip(results, outs):
            res[n] = a[None]
    small_out = _adamw("adamw_small", recv_early[-1],
                       *[_pack_rows([t[n] for n in SMALL], small_rows) for t in (wts, mom, vel)])
    for res, fs in zip(results, small_out):
        for n, a in zip(SMALL, _unpack_flat(fs, [wts[n].shape for n in SMALL])):
            res[n] = a

    loss = small_out[0].reshape(-1)[n_small]
    return (loss, dx[None], *[res[n] for res in results for n in WEIGHTS])
```

```python
import math

import jax
import jax.numpy as jnp
from jax import lax
from jax.experimental import pallas as pl
from jax.experimental.pallas import tpu as pltpu

F32 = jnp.float32
BF16 = jnp.bfloat16

D_MODEL = 1024
S5_WIDTH = 256
S5_GROUP_CH = 16
S5_GROUPS = 16
S5_STATE = 64
NS = S5_GROUPS * S5_STATE
S5_MAX_RE = -1e-4
MLA_HEADS = 8
MLA_NOPE = 64
MLA_ROPE = 32
MLA_QK = 96
MLA_V = 64
HEAD_PAD = 128
ROPE_THETA = 10000.0
XATTN_HEADS = 4
XATTN_HD = 256
MLP_HIDDEN = 4096
LN_EPS = 1e-5
RMS_EPS = 1e-6
NEG_INF = -1e30
LOG2E = 1.4426950408889634
DN_ALPHA = 2.0 ** 0.25
ADAM_LR = 0.001
ADAM_B1 = 0.9
ADAM_B2 = 0.999
ADAM_EPS = 1e-08
ADAM_WD = 0.01
ADAM_STEP = 10

N_DEV = 8
MESH = pl.DeviceIdType.MESH
LANES = 1024
VMEM_LIMIT = 60 * 1024 * 1024

ROW_TILE = 256
SCAN_TILE = 1024
ATT_TILE = 1024
ADAM_TILE = 128

SHARDED = (
    ("w_in", "col", 1024, 2848), ("w_glu", "col", 256, 2048), ("w_uq", "col", 256, 768),
    ("w_ukv", "col", 256, 1024), ("w_oa", "col", 512, 1024), ("w_o", "row", 1024, 1024),
    ("w_xq", "row", 1024, 1024), ("w_xk", "row", 1024, 1024), ("w_xv", "row", 1024, 1024),
    ("w_xo", "row", 1024, 1024), ("w_up", "col", 1024, 4096), ("w_down", "row", 4096, 1024),
)
EARLY = ("w_in", "w_glu", "w_uq", "w_ukv")
SMALL = ("ln_in_g", "ln_in_b", "s5_lam_re", "s5_lam_im", "s5_log_dt", "s5_b_re", "s5_b_im", "s5_c_re",
         "s5_c_im", "s5_d", "q_norm_g", "kv_norm_g", "ln1_g", "ln1_b", "ln2_g", "ln2_b", "ln3_g", "ln3_b")
WEIGHTS = ("ln_in_g", "ln_in_b", "w_in", "s5_lam_re", "s5_lam_im", "s5_log_dt", "s5_b_re", "s5_b_im",
           "s5_c_re", "s5_c_im", "s5_d", "w_glu", "q_norm_g", "w_uq", "kv_norm_g", "w_ukv", "w_oa", "w_o",
           "ln1_g", "ln1_b", "w_xq", "w_xk", "w_xv", "w_xo", "ln2_g", "ln2_b", "w_up", "w_down", "ln3_g", "ln3_b")


def _round_up(n, m):
    return (n + m - 1) // m * m


def _bf(a):
    return a.astype(BF16)


def _mm(a, b):
    return jnp.dot(_bf(a), _bf(b), preferred_element_type=F32)


def _mm_nt(a, b):
    return lax.dot_general(_bf(a), _bf(b), (((1,), (1,)), ((), ())), preferred_element_type=F32)


def _mm_tn(a, b):
    return lax.dot_general(_bf(a), _bf(b), (((0,), (0,)), ((), ())), preferred_element_type=F32)


def _sigmoid(a):
    return 1.0 / (1.0 + jnp.exp(-a))


def _gelu(a):
    return 0.5 * a * (1.0 + lax.erf(a * (2.0 ** -0.5)))


def _gelu_grad(a):
    return 0.5 * (1.0 + lax.erf(a * (2.0 ** -0.5))) + a * jnp.exp(-0.5 * a * a) * (1.0 / math.sqrt(2.0 * math.pi))


def _ln_fwd(a, g, b):
    mu = jnp.mean(a, axis=-1, keepdims=True)
    ac = a - mu
    var = jnp.mean(ac * ac, axis=-1, keepdims=True)
    rstd = lax.rsqrt(var + LN_EPS)
    xhat = ac * rstd
    return xhat * g + b, xhat, rstd


def _ln_bwd(dy, xhat, rstd, g):
    dxh = dy * g
    m1 = jnp.mean(dxh, axis=-1, keepdims=True)
    m2 = jnp.mean(dxh * xhat, axis=-1, keepdims=True)
    dx = rstd * (dxh - m1 - xhat * m2)
    return dx, jnp.sum(dy * xhat, axis=0, keepdims=True), jnp.sum(dy, axis=0, keepdims=True)


def _rms_fwd(a, g):
    r = lax.rsqrt(jnp.mean(a * a, axis=-1, keepdims=True) + RMS_EPS)
    xn = a * r
    return xn * g, xn, r


def _rms_bwd(dy, xn, r, g):
    dxn = dy * g
    dx = r * (dxn - xn * jnp.mean(dxn * xn, axis=-1, keepdims=True))
    return dx, jnp.sum(dy * xn, axis=0, keepdims=True)


def _tile_heads(a):
    return jnp.concatenate([a] * MLA_HEADS, axis=1)


def _row_call(name, body, n_rows, ts, tiled_in, full_in, weights, tiled_out, acc_out, reverse=False, scratch=(),
              tiled_out_t=()):
    n = n_rows // ts
    assert n * ts == n_rows, (name, n_rows, ts)
    nt, nf, nw = len(tiled_in), len(full_in), len(weights)
    nto, nao = len(tiled_out) + len(tiled_out_t), len(acc_out)
    if reverse:
        imap = lambda i: (n - 1 - i, 0)
    else:
        imap = lambda i: (i, 0)
    const = lambda i: (0, 0)

    def kern(*refs):
        ins = refs[:nt + nf]
        w_hbm = refs[nt + nf:nt + nf + nw]
        outs = refs[nt + nf + nw:nt + nf + nw + nto + nao]
        scr = refs[nt + nf + nw + nto + nao:]
        w_vmem = scr[:nw]
        extra = scr[nw + 1:] if nw else scr
        if nw:
            sem = scr[nw]

            @pl.when(pl.program_id(0) == 0)
            def _():
                cps = [pltpu.make_async_copy(w_hbm[k], w_vmem[k], sem.at[k]) for k in range(nw)]
                for cp in cps:
                    cp.start()
                for cp in cps:
                    cp.wait()
        body(*ins, *w_vmem, *outs, *extra)

    in_specs = [pl.BlockSpec((ts, a.shape[1]), imap) for a in tiled_in]
    in_specs += [pl.BlockSpec(a.shape, const) for a in full_in]
    in_specs += [pl.BlockSpec(memory_space=pl.ANY) for _ in weights]
    assert not (reverse and tiled_out_t)
    out_shape = [jax.ShapeDtypeStruct((n_rows, c), dt) for (c, dt) in tiled_out]
    out_shape += [jax.ShapeDtypeStruct((c, n_rows), dt) for (c, dt) in tiled_out_t]
    out_shape += [jax.ShapeDtypeStruct(shp, F32) for shp in acc_out]
    out_specs = [pl.BlockSpec((ts, c), imap) for (c, dt) in tiled_out]
    out_specs += [pl.BlockSpec((c, ts), lambda i: (0, i)) for (c, dt) in tiled_out_t]
    out_specs += [pl.BlockSpec(shp, const) for shp in acc_out]
    scratch_shapes = [pltpu.VMEM(w.shape, w.dtype) for w in weights]
    if nw:
        scratch_shapes.append(pltpu.SemaphoreType.DMA((nw,)))
    scratch_shapes += list(scratch)
    return pl.pallas_call(
        kern, name=name, grid=(n,), in_specs=in_specs, out_specs=out_specs, out_shape=out_shape,
        scratch_shapes=scratch_shapes,
        compiler_params=pltpu.CompilerParams(dimension_semantics=("arbitrary",), vmem_limit_bytes=VMEM_LIMIT),
    )(*tiled_in, *full_in, *weights)


def _mem_kv(mem, w_xk, w_xv):
    m = mem.shape[0]

    def body(mem_ref, wk_ref, wv_ref, k_ref, v_ref):
        mb = mem_ref[...]
        k_ref[...] = _mm(mb, wk_ref[...]).astype(BF16)
        v_ref[...] = _mm(mb, wv_ref[...]).astype(BF16)

    return pl.pallas_call(
        body, name="mem_kv",
        out_shape=[jax.ShapeDtypeStruct((m, D_MODEL), BF16)] * 2,
        compiler_params=pltpu.CompilerParams(vmem_limit_bytes=VMEM_LIMIT),
    )(mem, w_xk, w_xv)


def _in_fwd(x, tab, ln_g, ln_b, qg, kvg, w_inx, w_b, w_q2, w_kv2):
    def body(x_ref, tab_ref, lng_ref, lnb_ref, qg_ref, kvg_ref, winx, wb, wq2, wkv2,
             h0_ref, z3_ref, gate_ref, bu_ref, q_ref, k_ref, v_ref, kt_ref, vt_ref):
        h0, _, _ = _ln_fwd(x_ref[...], lng_ref[...], lnb_ref[...])
        h0_ref[...] = h0
        z = _mm(h0, winx[...])
        z3_ref[...] = z[:, :768]
        gate_ref[...] = z[:, 1024:].astype(BF16)
        bu_ref[...] = _mm(z[:, :256], wb[...])
        tab_v = tab_ref[...]
        cq1, cq2, ck1 = tab_v[:, :128], tab_v[:, 128:], tab_v[:, :128]
        cqn, _, _ = _rms_fwd(z[:, 256:512], qg_ref[...])
        q2 = _mm(cqn, wq2[...])
        q_ref[...] = (q2[:, :1024] * _tile_heads(cq1) + q2[:, 1024:] * _tile_heads(cq2)).astype(BF16)
        ckvn, _, _ = _rms_fwd(z[:, 512:768], kvg_ref[...])
        kv2 = _mm(ckvn, wkv2[...])
        krp = z[:, 768:896] * ck1 + z[:, 896:1024] * cq2
        kf = kv2[:, :1024] + _tile_heads(krp)
        k_ref[...] = kf.astype(BF16)
        v_ref[...] = kv2[:, 1024:].astype(BF16)
        kt_ref[...] = kf.T.astype(BF16)
        vt_ref[...] = kv2[:, 1024:].T.astype(BF16)

    s = x.shape[0]
    return _row_call(
        "in_fwd", body, s, min(ROW_TILE, s), [x, tab], [ln_g, ln_b, qg, kvg], [w_inx, w_b, w_q2, w_kv2],
        [(1024, F32), (768, F32), (2048, BF16), (2048, F32), (1024, BF16), (1024, BF16), (1024, BF16)], [],
        tiled_out_t=[(1024, BF16), (1024, BF16)])


def _scan_coeffs(pw_ref, rows, forward):
    row = lax.broadcasted_iota(jnp.int32, (8, NS), 0)
    out = []
    for k, r in zip((1, 2, 4), rows):
        keep = (row >= k) if forward else (row < 8 - k)
        out.append((jnp.where(keep, pw_ref[r:r + 1, :NS], 0.0), jnp.where(keep, pw_ref[r:r + 1, NS:], 0.0), k))
    return out


def _scan_level(xr, xi, coeff, forward):
    ar, ai, k = coeff
    shift = k if forward else 8 - k
    sr, si = pltpu.roll(xr, shift, 0), pltpu.roll(xi, shift, 0)
    return xr + ar * sr - ai * si, xi + ar * si + ai * sr


def _scan_fwd(pw, bu):
    s = bu.shape[0]
    ts = min(SCAN_TILE, s)
    nblk = ts // 8

    def body(bu_ref, pw_ref, h_ref, carry):
        @pl.when(pl.program_id(0) == 0)
        def _():
            carry[...] = jnp.zeros_like(carry)

        coeffs = _scan_coeffs(pw_ref, (0, 1, 3), True)

        def blk(b, _):
            r0 = pl.multiple_of(b * 8, 8)
            xr = bu_ref[pl.ds(r0, 8), :NS]
            xi = bu_ref[pl.ds(r0, 8), NS:]
            for coeff in coeffs:
                xr, xi = _scan_level(xr, xi, coeff, True)
            cr, ci = carry[7:8, :NS], carry[7:8, NS:]
            pr, pi = pw_ref[:, :NS], pw_ref[:, NS:]
            hr = xr + pr * cr - pi * ci
            hi = xi + pr * ci + pi * cr
            h_ref[pl.ds(r0, 8), :NS] = hr
            h_ref[pl.ds(r0, 8), NS:] = hi
            carry[:, :NS] = hr
            carry[:, NS:] = hi
            return 0

        lax.fori_loop(0, nblk, blk, 0)

    return _row_call("scan_fwd", body, s, ts, [bu], [pw], [], [(2 * NS, F32)], [],
                     scratch=[pltpu.VMEM((8, 2 * NS), F32)])[0]


def _scan_bwd(pwb, gh, h):
    s = gh.shape[0]
    ts = min(SCAN_TILE, s)
    nblk = ts // 8
    n_tiles = s // ts

    def body(g_ref, h_ref, pw_ref, lam_ref, da_ref, carry, acc):
        @pl.when(pl.program_id(0) == 0)
        def _():
            carry[...] = jnp.zeros_like(carry)
            acc[...] = jnp.zeros_like(acc)

        row = lax.broadcasted_iota(jnp.int32, (8, NS), 0)
        coeffs = _scan_coeffs(pw_ref, (7, 6, 4), False)

        def blk(bb, _):
            r0 = pl.multiple_of((nblk - 1 - bb) * 8, 8)
            xr = g_ref[pl.ds(r0, 8), :NS]
            xi = g_ref[pl.ds(r0, 8), NS:]
            for coeff in coeffs:
                xr, xi = _scan_level(xr, xi, coeff, False)
            cr, ci = carry[0:1, :NS], carry[0:1, NS:]
            pr, pi = pw_ref[:, :NS], pw_ref[:, NS:]
            lr = xr + pr * cr - pi * ci
            li = xi + pr * ci + pi * cr
            lam_ref[pl.ds(r0, 8), :NS] = lr
            lam_ref[pl.ds(r0, 8), NS:] = li
            nr = jnp.where(row < 7, pltpu.roll(lr, 7, 0), cr)
            ni = jnp.where(row < 7, pltpu.roll(li, 7, 0), ci)
            hr = h_ref[pl.ds(r0, 8), :NS]
            hi = h_ref[pl.ds(r0, 8), NS:]
            acc[:, :NS] += nr * hr + ni * hi
            acc[:, NS:] += ni * hr - nr * hi
            carry[:, :NS] = lr
            carry[:, NS:] = li
            return 0

        lax.fori_loop(0, nblk, blk, 0)

        @pl.when(pl.program_id(0) == n_tiles - 1)
        def _():
            da_ref[...] = jnp.sum(acc[...], axis=0, keepdims=True)

    return _row_call("scan_bwd", body, s, ts, [gh, h], [pwb], [], [(2 * NS, F32)], [(1, 2 * NS)], reverse=True,
                     scratch=[pltpu.VMEM((8, 2 * NS), F32), pltpu.VMEM((8, 2 * NS), F32)])


def _s5out_fwd(h, z3, d_skip, w_c, w_glu):
    def body(h_ref, z3_ref, d_ref, wc, wglu, yl_ref, so_ref):
        yl = _mm(h_ref[...], wc[...]) + d_ref[...] * z3_ref[:, :256]
        yl_ref[...] = yl
        y12 = _mm(_gelu(yl), wglu[...])
        so_ref[...] = y12[:, :1024] * _sigmoid(y12[:, 1024:])

    s = h.shape[0]
    return _row_call("s5out_fwd", body, s, min(ROW_TILE, s), [h, z3], [d_skip], [w_c, w_glu],
                     [(256, F32), (1024, F32)], [])


def _causal_mask_t(t):
    row = lax.broadcasted_iota(jnp.int32, (t, t), 0)
    col = lax.broadcasted_iota(jnp.int32, (t, t), 1)
    return row <= col


def _attn_fwd(q, k, vt):
    s = q.shape[0]
    t = min(ATT_TILE, s)
    nq = s // t
    scale = MLA_QK ** -0.5
    c2 = scale * LOG2E

    def body(q_ref, k_ref, vt_ref, o_ref, lse_ref, s0, s1):
        i = pl.program_id(1)
        qb = q_ref[...]

        def scores(kb, dst):
            st = _mm_nt(k_ref[pl.ds(pl.multiple_of(kb * t, t), t), :], qb)
            dst[...] = st
            return jnp.max(st, axis=0, keepdims=True)

        def update(kb, src, mt, state, masked):
            m, l, acc = state
            if masked:
                keep = _causal_mask_t(t)
                mt = jnp.max(jnp.where(keep, src[...], NEG_INF), axis=0, keepdims=True)
            m_new = jnp.maximum(m, mt)
            if masked:
                p = jnp.exp2((jnp.where(keep, src[...], NEG_INF) - m_new) * c2)
            else:
                p = jnp.exp2((src[...] - m_new) * c2)
            a = jnp.exp2((m - m_new) * c2)
            l = a * l + jnp.sum(p, axis=0, keepdims=True)
            acc = a * acc + _mm(vt_ref[:, pl.ds(pl.multiple_of(kb * t, t), t)], p)
            return m_new, l, acc

        def pair(jj, carry):
            mt_a, state = carry
            mt_b = scores(2 * jj + 1, s1)
            state = update(2 * jj, s0, mt_a, state, False)
            mt_a = scores(2 * jj + 2, s0)
            return mt_a, update(2 * jj + 1, s1, mt_b, state, False)

        def odd_tail(carry):
            mt_a, state = carry
            mt_b = scores(i, s1)
            state = update(i - 1, s0, mt_a, state, False)
            return update(i, s1, mt_b, state, True)

        def even_tail(carry):
            mt_a, state = carry
            return update(i, s0, mt_a, state, True)

        init = (jnp.full((1, t), NEG_INF, F32), jnp.zeros((1, t), F32), jnp.zeros((HEAD_PAD, t), F32))
        carry = lax.fori_loop(0, i // 2, pair, (scores(0, s0), init))
        m, l, acc = lax.cond(i % 2 == 1, odd_tail, even_tail, carry)
        o_ref[...] = (acc / l).T
        lse_ref[...] = m * scale + jnp.log(l)

    return pl.pallas_call(
        body, name="attn_fwd", grid=(MLA_HEADS, nq),
        in_specs=[pl.BlockSpec((t, HEAD_PAD), lambda h, i: (i, h)),
                  pl.BlockSpec((s, HEAD_PAD), lambda h, i: (0, h)),
                  pl.BlockSpec((HEAD_PAD, s), lambda h, i: (h, 0))],
        out_specs=[pl.BlockSpec((t, HEAD_PAD), lambda h, i: (i, h)),
                   pl.BlockSpec((None, 1, t), lambda h, i: (h, 0, i))],
        out_shape=[jax.ShapeDtypeStruct((s, MLA_HEADS * HEAD_PAD), F32),
                   jax.ShapeDtypeStruct((MLA_HEADS, 1, s), F32)],
        scratch_shapes=[pltpu.VMEM((t, t), F32)] * 2,
        compiler_params=pltpu.CompilerParams(dimension_semantics=("arbitrary", "arbitrary"),
                                             vmem_limit_bytes=VMEM_LIMIT),
    )(q, k, vt)


def _attn_bwd(q, k, v, kt, do, lse_t, dl_t):
    s = q.shape[0]
    t = min(ATT_TILE, s)
    nq = s // t
    scale = MLA_QK ** -0.5
    c2 = scale * LOG2E

    def body(q_ref, k_ref, v_ref, kt_ref, do_ref, lse_ref, dl_ref, dq_ref, dk_ref, dv_ref, dqt, s0, p0, s1, p1):
        j = pl.program_id(1)
        n = nq - 1 - j

        @pl.when(j == 0)
        def _():
            dqt[...] = jnp.zeros_like(dqt)

        kk = k_ref[...]
        vv = v_ref[...]
        ktb = kt_ref[:MLA_QK, :]

        def rows(m):
            return pl.ds(pl.multiple_of(jnp.where(m < n, j + 1 + m, j) * t, t), t)

        def first(m, sbuf, pbuf):
            r = rows(m)
            sbuf[...] = _mm_nt(kk, q_ref[r, :])
            pbuf[...] = _mm_nt(vv, do_ref[r, :])

        def finish(m, sbuf, pbuf, acc, masked):
            dk, dv = acc
            r = rows(m)
            pt = jnp.exp2(sbuf[...] * c2 - lse_ref[:, r] * LOG2E)
            if masked:
                pt = jnp.where(_causal_mask_t(t), pt, 0.0)
            dv = dv + _mm(pt, do_ref[r, :])
            dst = (pt * (pbuf[...] - dl_ref[:, r])).astype(BF16)
            dk = dk + _mm(dst, q_ref[r, :])
            dqt[:MLA_QK, r] += _mm(ktb, dst)
            return dk, dv

        def pair(jj, acc):
            first(2 * jj + 1, s1, p1)
            acc = finish(2 * jj, s0, p0, acc, False)
            first(2 * jj + 2, s0, p0)
            return finish(2 * jj + 1, s1, p1, acc, False)

        def odd_tail(acc):
            first(n, s1, p1)
            acc = finish(n - 1, s0, p0, acc, False)
            return finish(n, s1, p1, acc, True)

        def even_tail(acc):
            return finish(n, s0, p0, acc, True)

        zero = jnp.zeros((t, HEAD_PAD), F32)
        first(0, s0, p0)
        acc = lax.fori_loop(0, n // 2, pair, (zero, zero))
        dk, dv = lax.cond(n % 2 == 1, odd_tail, even_tail, acc)
        dk_ref[...] = (dk * scale).astype(BF16)
        dv_ref[...] = dv.astype(BF16)

        @pl.when(j == nq - 1)
        def _():
            for cc in range(nq):
                dq_ref[cc * t:(cc + 1) * t, :] = (dqt[:, cc * t:(cc + 1) * t].T * scale).astype(BF16)

    full = pl.BlockSpec((s, HEAD_PAD), lambda h, j: (0, h))
    tile = pl.BlockSpec((t, HEAD_PAD), lambda h, j: (j, h))
    stat = pl.BlockSpec((None, 1, s), lambda h, j: (h, 0, 0))
    return pl.pallas_call(
        body, name="attn_bwd", grid=(MLA_HEADS, nq),
        in_specs=[full, tile, tile, pl.BlockSpec((HEAD_PAD, t), lambda h, j: (h, j)), full, stat, stat],
        out_specs=[full, tile, tile],
        out_shape=[jax.ShapeDtypeStruct((s, MLA_HEADS * HEAD_PAD), BF16)] * 3,
        scratch_shapes=[pltpu.VMEM((HEAD_PAD, s), F32)] + [pltpu.VMEM((t, t), F32)] * 4,
        compiler_params=pltpu.CompilerParams(dimension_semantics=("arbitrary", "arbitrary"),
                                             vmem_limit_bytes=VMEM_LIMIT),
    )(q, k, v, kt, do, lse_t, dl_t)


def _xattn_probs(qxb, memk, hh):
    sl = slice(hh * XATTN_HD, (hh + 1) * XATTN_HD)
    sc = _mm_nt(qxb[:, sl], memk[:, sl]) * (XATTN_HD ** -0.5)
    e = jnp.exp(sc - jnp.max(sc, axis=1, keepdims=True))
    return e / jnp.sum(e, axis=1, keepdims=True)


def _post_forward(o, s_out, gate, h0, memk, memv, w_oa, w_o, w_xq, w_xo, g1, b1, g2, b2):
    a_out = _mm(o, w_oa)
    sg_s = _sigmoid(gate[:, :1024])
    sg_a = _sigmoid(gate[:, 1024:])
    mixin = sg_s * s_out + sg_a * a_out
    r1 = DN_ALPHA * h0 + _mm(mixin, w_o)
    h1, xh1, rs1 = _ln_fwd(r1, g1, b1)
    qxb = _mm(h1, w_xq).astype(BF16)
    ox = jnp.concatenate([_mm(_xattn_probs(qxb, memk, hh), memv[:, hh * XATTN_HD:(hh + 1) * XATTN_HD])
                          for hh in range(XATTN_HEADS)], axis=1)
    r2 = DN_ALPHA * h1 + _mm(ox, w_xo)
    return dict(a_out=a_out, mixin=mixin, r1=r1, h1=h1, qxb=qxb, ox=ox, r2=r2)


def _post_fwd(o, s_out, gate, h0, memk, memv, lns, w_oa, w_o, w_xq, w_xo):
    def body(o_ref, so_ref, gate_ref, h0_ref, mk_ref, mv_ref, g1, b1, g2, b2, woa, wo, wxq, wxo,
             r2_ref, mixin_ref, h1_ref, ox_ref, aout_ref, r1_ref, qx_ref):
        f = _post_forward(o_ref[...], so_ref[...], gate_ref[...].astype(F32), h0_ref[...], mk_ref[...], mv_ref[...],
                          woa[...], wo[...], wxq[...], wxo[...], g1[...], b1[...], g2[...], b2[...])
        r2_ref[...] = f["r2"]
        mixin_ref[...] = f["mixin"].astype(BF16)
        h1_ref[...] = f["h1"].astype(BF16)
        ox_ref[...] = f["ox"].astype(BF16)
        aout_ref[...] = f["a_out"]
        r1_ref[...] = f["r1"]
        qx_ref[...] = f["qxb"]

    s = o.shape[0]
    return _row_call("post_fwd", body, s, min(ROW_TILE, s), [o, s_out, gate, h0], [memk, memv, *lns],
                     [w_oa, w_o, w_xq, w_xo],
                     [(1024, F32), (1024, BF16), (1024, BF16), (1024, BF16), (1024, F32), (1024, F32), (1024, BF16)], [])


def _mlp(r2, target, g2, b2, g3, b3, w_up, w_down):
    cw = 1024
    n_chunk = MLP_HIDDEN // cw

    def body(r2_ref, tgt_ref, g2_ref, b2_ref, g3_ref, b3_ref, wup, wdn,
             dh2_ref, h2b_ref, dup_ref, act_ref, dff_ref, loss_ref, dg3_ref, db3_ref, up_scr):
        @pl.when(pl.program_id(0) == 0)
        def _():
            loss_ref[...] = jnp.zeros_like(loss_ref)
            dg3_ref[...] = jnp.zeros_like(dg3_ref)
            db3_ref[...] = jnp.zeros_like(db3_ref)

        h2, _, _ = _ln_fwd(r2_ref[...], g2_ref[...], b2_ref[...])
        h2b = h2.astype(BF16)
        h2b_ref[...] = h2b
        ff = jnp.zeros(h2.shape, F32)
        for c in range(n_chunk):
            sl = slice(c * cw, (c + 1) * cw)
            a = jnp.maximum(_mm(h2b, wup[:, sl]), 0.0)
            up_scr[:, sl] = a
            actb = (a * a).astype(BF16)
            act_ref[:, sl] = actb
            ff = ff + _mm(actb, wdn[sl, :])
        h3, xh3, rs3 = _ln_fwd(DN_ALPHA * h2 + ff, g3_ref[...], b3_ref[...])
        err = h3 - tgt_ref[...]
        loss_ref[...] += 0.5 * jnp.sum(err * err) * (1.0 / D_MODEL)
        dr3, dg3, db3 = _ln_bwd(err * (1.0 / D_MODEL), xh3, rs3, g3_ref[...])
        dg3_ref[...] += dg3
        db3_ref[...] += db3
        dffb = dr3.astype(BF16)
        dff_ref[...] = dffb
        dh2 = DN_ALPHA * dr3
        for c in range(n_chunk):
            sl = slice(c * cw, (c + 1) * cw)
            dupb = (_mm_nt(dffb, wdn[sl, :]) * (2.0 * up_scr[:, sl])).astype(BF16)
            dup_ref[:, sl] = dupb
            dh2 = dh2 + _mm_nt(dupb, wup[:, sl])
        dh2_ref[...] = dh2

    s = r2.shape[0]
    ts = min(ROW_TILE, s)
    return _row_call("mlp", body, s, ts, [r2, target], [g2, b2, g3, b3], [w_up, w_down],
                     [(1024, F32), (1024, BF16), (MLP_HIDDEN, BF16), (MLP_HIDDEN, BF16), (1024, BF16)],
                     [(8, 128), (1, 1024), (1, 1024)], scratch=[pltpu.VMEM((ts, MLP_HIDDEN), F32)])


def _post_bwd(dh2, o, s_out, gate, a_out, r1, r2, qx, memk, memv, lns, w_oa, w_o, w_xq, w_xo):
    def body(dh2_ref, o_ref, so_ref, gate_ref, aout_ref, r1_ref, r2_ref, qx_ref, mk_ref, mv_ref, g1, b1, g2, b2,
             woa, wo, wxq, wxo,
             do_ref, dl_ref, dso_ref, dgate_ref, dh0_ref, daout_ref, dmix_ref, dqx_ref, dxa_ref,
             dmk_ref, dmv_ref, dg1_ref, db1_ref, dg2_ref, db2_ref):
        @pl.when(pl.program_id(0) == 0)
        def _():
            for r in (dmk_ref, dmv_ref, dg1_ref, db1_ref, dg2_ref, db2_ref):
                r[...] = jnp.zeros_like(r)

        o = o_ref[...]
        s_out = so_ref[...]
        memk, memv = mk_ref[...], mv_ref[...]
        qxb = qx_ref[...]
        _, xh1, rs1 = _ln_fwd(r1_ref[...], g1[...], b1[...])
        _, xh2, rs2 = _ln_fwd(r2_ref[...], g2[...], b2[...])
        dr2, dg2, db2 = _ln_bwd(dh2_ref[...], xh2, rs2, g2[...])
        dg2_ref[...] += dg2
        db2_ref[...] += db2
        dxab = dr2.astype(BF16)
        dxa_ref[...] = dxab
        dox = _mm_nt(dxab, wxo[...])
        dqs = []
        for hh in range(XATTN_HEADS):
            sl = slice(hh * XATTN_HD, (hh + 1) * XATTN_HD)
            p = _xattn_probs(qxb, memk, hh)
            doxh = dox[:, sl].astype(BF16)
            dp = _mm_nt(doxh, memv[:, sl])
            ds = (p * (dp - jnp.sum(dp * p, axis=1, keepdims=True)) * (XATTN_HD ** -0.5)).astype(BF16)
            dqs.append(_mm(ds, memk[:, sl]))
            dmk_ref[:, sl] += _mm_tn(ds, qxb[:, sl])
            dmv_ref[:, sl] += _mm_tn(p, doxh)
        dqxb = jnp.concatenate(dqs, axis=1).astype(BF16)
        dqx_ref[...] = dqxb
        dh1 = DN_ALPHA * dr2 + _mm_nt(dqxb, wxq[...])
        dr1, dg1, db1 = _ln_bwd(dh1, xh1, rs1, g1[...])
        dg1_ref[...] += dg1
        db1_ref[...] += db1
        dh0_ref[...] = DN_ALPHA * dr1
        dmixb = dr1.astype(BF16)
        dmix_ref[...] = dmixb
        dmixin = _mm_nt(dmixb, wo[...])
        sg_s = _sigmoid(gate_ref[:, :1024].astype(F32))
        sg_a = _sigmoid(gate_ref[:, 1024:].astype(F32))
        dso_ref[...] = dmixin * sg_s
        daoutb = (dmixin * sg_a).astype(BF16)
        daout_ref[...] = daoutb
        dgate_ref[:, :1024] = (dmixin * s_out * sg_s * (1.0 - sg_s)).astype(BF16)
        dgate_ref[:, 1024:] = (dmixin * aout_ref[...] * sg_a * (1.0 - sg_a)).astype(BF16)
        d_o = _mm_nt(daoutb, woa[...])
        do_ref[...] = d_o.astype(BF16)
        lane = lax.broadcasted_iota(jnp.int32, (o.shape[0], HEAD_PAD), 1)
        dlc = jnp.zeros((o.shape[0], HEAD_PAD), F32)
        for hh in range(MLA_HEADS):
            sl = slice(hh * HEAD_PAD, (hh + 1) * HEAD_PAD)
            dl = jnp.sum(d_o[:, sl] * o[:, sl], axis=1, keepdims=True)
            dlc = dlc + jnp.where(lane == hh, dl, 0.0)
        dl_ref[...] = dlc

    s = o.shape[0]
    m = memk.shape[0]
    return _row_call(
        "post_bwd", body, s, min(ROW_TILE, s), [dh2, o, s_out, gate, a_out, r1, r2, qx], [memk, memv, *lns],
        [w_oa, w_o, w_xq, w_xo],
        [(1024, BF16), (HEAD_PAD, F32), (1024, F32), (2048, BF16), (1024, F32),
         (1024, BF16), (1024, BF16), (1024, BF16), (1024, BF16)],
        [(m, 1024), (m, 1024), (1, 1024), (1, 1024), (1, 1024), (1, 1024)])


def _s5out_bwd(d_so, yl, z3, d_skip, w_c, w_glu):
    def body(dso_ref, yl_ref, z3_ref, d_ref, wc, wglu, gh_ref, dup_ref, yg_ref, dy12_ref, dyl_ref, dd_ref):
        @pl.when(pl.program_id(0) == 0)
        def _():
            dd_ref[...] = jnp.zeros_like(dd_ref)

        yl = yl_ref[...]
        ygb = _gelu(yl).astype(BF16)
        yg_ref[...] = ygb
        y12 = _mm(ygb, wglu[...])
        sg = _sigmoid(y12[:, 1024:])
        dso = dso_ref[...]
        dy12b = jnp.concatenate([dso * sg, dso * y12[:, :1024] * sg * (1.0 - sg)], axis=1).astype(BF16)
        dy12_ref[...] = dy12b
        dyl = _mm_nt(dy12b, wglu[...]) * _gelu_grad(yl)
        dylb = dyl.astype(BF16)
        dyl_ref[...] = dylb
        gh_ref[...] = _mm_nt(dylb, wc[...])
        dup_ref[...] = dyl * d_ref[...]
        dd_ref[...] += jnp.sum(dyl * z3_ref[:, :256], axis=0, keepdims=True)

    s = d_so.shape[0]
    return _row_call("s5out_bwd", body, s, min(ROW_TILE, s), [d_so, yl, z3], [d_skip], [w_c, w_glu],
                     [(2 * NS, F32), (256, F32), (256, BF16), (2048, BF16), (256, BF16)], [(1, 256)])


def _in_bwd(x, tab, z3, dq, dk, dv, lam, du_p, dgate, dh0p, ln_g, ln_b, qg, kvg, w_inx, w_b, w_q2, w_kv2):
    def body(x_ref, tab_ref, z3_ref, dq_ref, dk_ref, dv_ref, lam_ref, dup_ref, dgate_ref, dh0p_ref,
             lng_ref, lnb_ref, qg_ref, kvg_ref, winx, wb, wq2, wkv2,
             dx_ref, h0b_ref, dz_ref, ub_ref, cqn_ref, ckvn_ref, dq2_ref, dkv2_ref,
             dlng_ref, dlnb_ref, dqg_ref, dkvg_ref):
        @pl.when(pl.program_id(0) == 0)
        def _():
            for r in (dlng_ref, dlnb_ref, dqg_ref, dkvg_ref):
                r[...] = jnp.zeros_like(r)

        h0, xh0, rs0 = _ln_fwd(x_ref[...], lng_ref[...], lnb_ref[...])
        h0b_ref[...] = h0.astype(BF16)
        z3 = z3_ref[...]
        ub_ref[...] = z3[:, :256].astype(BF16)
        tab_v = tab_ref[...]
        cq1, cq2, ck1 = tab_v[:, :128], tab_v[:, 128:], tab_v[:, :128]
        du = _mm_nt(lam_ref[...], wb[...]) + dup_ref[...]
        dq = dq_ref[...].astype(F32)
        dq2b = jnp.concatenate([dq * _tile_heads(cq1), dq * _tile_heads(cq2)], axis=1).astype(BF16)
        dq2_ref[...] = dq2b
        cqg, cqn, rq = _rms_fwd(z3[:, 256:512], qg_ref[...])
        cqn_ref[...] = cqg.astype(BF16)
        dcq, dqg = _rms_bwd(_mm_nt(dq2b, wq2[...]), cqn, rq, qg_ref[...])
        dqg_ref[...] += dqg
        dkv2b = jnp.concatenate([dk_ref[...], dv_ref[...]], axis=1)
        dk = dk_ref[...].astype(F32)
        dkv2_ref[...] = dkv2b
        ckvg, ckvn, rkv = _rms_fwd(z3[:, 512:768], kvg_ref[...])
        ckvn_ref[...] = ckvg.astype(BF16)
        dckv, dkvg = _rms_bwd(_mm_nt(dkv2b, wkv2[...]), ckvn, rkv, kvg_ref[...])
        dkvg_ref[...] += dkvg
        dkrp = dk[:, :HEAD_PAD]
        for hh in range(1, MLA_HEADS):
            dkrp = dkrp + dk[:, hh * HEAD_PAD:(hh + 1) * HEAD_PAD]
        dzb = jnp.concatenate([a.astype(BF16) for a in (du, dcq, dckv, dkrp * ck1, dkrp * cq2)] + [dgate_ref[...]],
                              axis=1)
        dz_ref[...] = dzb
        dh0 = _mm_nt(dzb, winx[...]) + dh0p_ref[...]
        dx, dg, db = _ln_bwd(dh0, xh0, rs0, lng_ref[...])
        dx_ref[...] = dx
        dlng_ref[...] += dg
        dlnb_ref[...] += db

    s = x.shape[0]
    return _row_call(
        "in_bwd", body, s, min(ROW_TILE, s), [x, tab, z3, dq, dk, dv, lam, du_p, dgate, dh0p],
        [ln_g, ln_b, qg, kvg], [w_inx, w_b, w_q2, w_kv2],
        [(1024, F32), (1024, BF16), (3072, BF16), (256, BF16), (256, BF16), (256, BF16), (2048, BF16), (2048, BF16)],
        [(1, 1024), (1, 1024), (1, 256), (1, 256)])


def _wgrad(name, xs, dy, out_dtype=F32, blocks=1, after=None):
    s, k = xs.shape
    n = dy.shape[1]
    nb = n // blocks
    tk = min(k, 1024)
    tn = max(d for d in range(128, min(n, 2048) + 1, 128) if n % d == 0)
    ts = min(s, 2048)
    per_tile = tn // nb
    assert s % ts == 0 and k % tk == 0 and n % tn == 0 and (blocks == 1 or tn % nb == 0), (name, s, k, n)
    last = s // ts - 1

    def body(x_ref, dy_ref, *rest):
        o_ref, acc = rest[-2:]

        @pl.when(pl.program_id(2) == 0)
        def _():
            acc[...] = jnp.zeros_like(acc)

        acc[...] += _mm_tn(x_ref[...], dy_ref[...])

        @pl.when(pl.program_id(2) == last)
        def _():
            if blocks == 1:
                o_ref[...] = acc[...].astype(out_dtype)
            else:
                for d in range(per_tile):
                    o_ref[d] = acc[:, d * nb:(d + 1) * nb].astype(out_dtype)

    if blocks == 1:
        out_shape = jax.ShapeDtypeStruct((k, n), out_dtype)
        out_spec = pl.BlockSpec((tk, tn), lambda a, b, c: (a, b))
    else:
        out_shape = jax.ShapeDtypeStruct((blocks, k, nb), out_dtype)
        out_spec = pl.BlockSpec((per_tile, tk, nb), lambda a, b, c: (b, a, 0))
    return pl.pallas_call(
        body, name=name, grid=(k // tk, n // tn, s // ts),
        in_specs=[pl.BlockSpec((ts, tk), lambda a, b, c: (c, a)), pl.BlockSpec((ts, tn), lambda a, b, c: (c, b))]
        + ([] if after is None else [pl.BlockSpec(memory_space=pl.ANY)]),
        out_specs=out_spec, out_shape=out_shape, scratch_shapes=[pltpu.VMEM((tk, tn), F32)],
        compiler_params=pltpu.CompilerParams(dimension_semantics=("arbitrary", "arbitrary", "arbitrary"),
                                             vmem_limit_bytes=VMEM_LIMIT),
    )(xs, dy, *([] if after is None else [after]))


def _mesh_pos():
    x, y, c = lax.axis_index("x"), lax.axis_index("y"), lax.axis_index("c")
    return x, y, c


def _peer(x, y, c, k):
    px = 1 - x if k & 4 else x
    py = 1 - y if k & 2 else y
    pc = 1 - c if k & 1 else c
    return (px, py, pc), 4 * px + 2 * py + pc


def _all_to_all(name, arrays, gather):
    n = len(arrays)

    def body(*refs):
        src, dst = refs[:n], refs[n:2 * n]
        send_sems, recv_sems, local_sems = refs[2 * n:]
        x, y, c = _mesh_pos()
        me = 4 * x + 2 * y + c

        def block(i, d):
            return src[i] if gather else src[i].at[d]

        own = [pltpu.make_async_copy(block(i, me), dst[i].at[me], local_sems.at[i]) for i in range(n)]
        for cp in own:
            cp.start()
        sends = []
        for k in range(1, N_DEV):
            peer, pid = _peer(x, y, c, k)
            for i in range(n):
                idx = (k - 1) * n + i
                cp = pltpu.make_async_remote_copy(src_ref=block(i, pid), dst_ref=dst[i].at[me],
                                                  send_sem=send_sems.at[idx], recv_sem=recv_sems.at[idx],
                                                  device_id=peer, device_id_type=MESH)
                cp.start()
                sends.append(cp)
        for k in range(1, N_DEV):
            peer, pid = _peer(x, y, c, k)
            for i in range(n):
                idx = (k - 1) * n + i
                pltpu.make_async_remote_copy(src_ref=block(i, pid), dst_ref=dst[i].at[pid],
                                             send_sem=send_sems.at[idx], recv_sem=recv_sems.at[idx],
                                             device_id=peer, device_id_type=MESH).wait_recv()
        for cp in sends:
            cp.wait_send()
        for cp in own:
            cp.wait()

    n_sem = n * (N_DEV - 1)
    return pl.pallas_call(
        body, name=name,
        in_specs=[pl.BlockSpec(memory_space=pl.ANY)] * n,
        out_specs=[pl.BlockSpec(memory_space=pl.ANY)] * n,
        out_shape=[jax.ShapeDtypeStruct((N_DEV,) + a.shape[-2:], a.dtype) for a in arrays],
        scratch_shapes=[pltpu.SemaphoreType.DMA((n_sem,)), pltpu.SemaphoreType.DMA((n_sem,)),
                        pltpu.SemaphoreType.DMA((n,))],
    )(*arrays)


def _all_gather_two_level(name, arrays):
    n = len(arrays)

    def body(*refs):
        src, dst = refs[:n], refs[n:2 * n]
        send_sems, recv_sems, local_sems = refs[2 * n:]
        x, y, c = _mesh_pos()
        me = 4 * x + 2 * y + c
        sibling = (x, y, 1 - c)

        def copy(i, k, source, slot, to):
            return pltpu.make_async_remote_copy(src_ref=source, dst_ref=dst[i].at[slot],
                                                send_sem=send_sems.at[(k - 1) * n + i],
                                                recv_sem=recv_sems.at[(k - 1) * n + i],
                                                device_id=to, device_id_type=MESH)

        own = [pltpu.make_async_copy(src[i], dst[i].at[me], local_sems.at[i]) for i in range(n)]
        for cp in own:
            cp.start()
        sends = []
        for k in (1, 2, 4, 6):
            peer, _ = _peer(x, y, c, k)
            for i in range(n):
                sends.append(copy(i, k, src[i], me, peer))
                sends[-1].start()
        for k in (2, 4, 6):
            peer, pid = _peer(x, y, c, k)
            for i in range(n):
                copy(i, k, src[i], pid, peer).wait_recv()
                sends.append(copy(i, k + 1, dst[i].at[pid], pid, sibling))
                sends[-1].start()
        for k in (1, 3, 5, 7):
            peer, pid = _peer(x, y, c, k)
            for i in range(n):
                copy(i, k, src[i], pid, peer).wait_recv()
        for cp in sends:
            cp.wait_send()
        for cp in own:
            cp.wait()

    n_sem = n * (N_DEV - 1)
    return pl.pallas_call(
        body, name=name,
        in_specs=[pl.BlockSpec(memory_space=pl.ANY)] * n,
        out_specs=[pl.BlockSpec(memory_space=pl.ANY)] * n,
        out_shape=[jax.ShapeDtypeStruct((N_DEV,) + a.shape, a.dtype) for a in arrays],
        scratch_shapes=[pltpu.SemaphoreType.DMA((n_sem,)), pltpu.SemaphoreType.DMA((n_sem,)),
                        pltpu.SemaphoreType.DMA((n,))],
    )(*arrays)


_HBM = pl.BlockSpec(memory_space=pltpu.HBM)
_SEM = pl.BlockSpec(memory_space=pltpu.SEMAPHORE)


def _exchange_start(name, arrays, gather, after=None):
    n = len(arrays)
    n_sem = n * (N_DEV - 1)
    me = 4 * lax.axis_index("x") + 2 * lax.axis_index("y") + lax.axis_index("c")
    lands = []
    for a in arrays:
        own = a[None] if gather else lax.dynamic_slice_in_dim(a, me, 1, 0)
        lands.append(lax.dynamic_update_slice(lax.empty((N_DEV,) + a.shape[-2:], a.dtype), own, (me, 0, 0)))
    n_after = 0 if after is None else 1

    def body(*refs):
        src, land = refs[:n], refs[n:2 * n]
        send_sems, recv_sems = refs[2 * n + n_after], refs[2 * n + n_after + 1]
        token = refs[-1]
        x, y, c = _mesh_pos()
        me_in = 4 * x + 2 * y + c
        for k in range(1, N_DEV):
            peer, pid = _peer(x, y, c, k)
            for i in range(n):
                idx = (k - 1) * n + i
                pltpu.make_async_remote_copy(src_ref=src[i] if gather else src[i].at[pid], dst_ref=land[i].at[me_in],
                                             send_sem=send_sems.at[idx], recv_sem=recv_sems.at[idx],
                                             device_id=peer, device_id_type=MESH).start()
        token[...] = jnp.zeros_like(token)

    operands = [pltpu.with_memory_space_constraint(a, pltpu.HBM) for a in list(arrays) + lands]
    outs = pl.pallas_call(
        body, name=name,
        out_shape=(pltpu.SemaphoreType.DMA((n_sem,)), pltpu.SemaphoreType.DMA((n_sem,)),
                   *[pltpu.HBM(a.shape, a.dtype) for a in list(arrays) + lands],
                   jax.ShapeDtypeStruct((8, 128), F32)),
        in_specs=[_HBM] * (2 * n) + [pl.BlockSpec(memory_space=pl.ANY)] * n_after,
        out_specs=(_SEM, _SEM, *[_HBM] * (2 * n), pl.BlockSpec(memory_space=pltpu.VMEM)),
        input_output_aliases={i: 2 + i for i in range(2 * n)},
        compiler_params=pltpu.CompilerParams(has_side_effects=pltpu.SideEffectType.DATAFLOW_SIDE_EFFECTING),
    )(*operands, *([after] if n_after else []))
    return (gather, outs[0], outs[1], outs[2:2 + n], outs[2 + n:2 + 2 * n]), outs[-1]


def _exchange_wait(name, handle, after):
    gather, send_sems, recv_sems, srcs, lands = handle
    n = len(srcs)

    def body(*refs):
        src, land = refs[:n], refs[n:2 * n]
        s_sems, r_sems = refs[2 * n], refs[2 * n + 1]
        x, y, c = _mesh_pos()
        for k in range(1, N_DEV):
            peer, pid = _peer(x, y, c, k)
            for i in range(n):
                idx = (k - 1) * n + i
                cp = pltpu.make_async_remote_copy(src_ref=src[i] if gather else src[i].at[pid], dst_ref=land[i].at[pid],
                                                  send_sem=s_sems.at[idx], recv_sem=r_sems.at[idx],
                                                  device_id=peer, device_id_type=MESH)
                cp.wait_send()
                cp.wait_recv()

    outs = pl.pallas_call(
        body, name=name,
        out_shape=tuple(pltpu.HBM(a.shape, a.dtype) for a in list(srcs) + list(lands)),
        in_specs=[_HBM] * (2 * n) + [_SEM, _SEM, pl.BlockSpec(memory_space=pl.ANY)],
        out_specs=tuple([_HBM] * (2 * n)),
        input_output_aliases={i: i for i in range(2 * n)},
        compiler_params=pltpu.CompilerParams(has_side_effects=pltpu.SideEffectType.DATAFLOW_SIDE_EFFECTING),
    )(*srcs, *lands, send_sems, recv_sems, after)
    return list(outs[n:])


def _adamw(name, parts, w, m, v):
    a_rows, b_cols = w.shape
    ta = min(a_rows, ADAM_TILE)
    assert a_rows % ta == 0
    c1 = 1.0 - ADAM_B1 ** ADAM_STEP
    c2 = 1.0 - ADAM_B2 ** ADAM_STEP

    def body(p_ref, w_ref, m_ref, v_ref, g_ref, d_ref, mo_ref, vo_ref):
        g = p_ref[0].astype(F32)
        for d in range(1, N_DEV):
            g = g + p_ref[d].astype(F32)
        g_ref[...] = g
        mn = ADAM_B1 * m_ref[...] + (1.0 - ADAM_B1) * g
        vn = ADAM_B2 * v_ref[...] + (1.0 - ADAM_B2) * (g * g)
        mo_ref[...] = mn
        vo_ref[...] = vn
        d_ref[...] = -ADAM_LR * ((mn / c1) / (jnp.sqrt(vn / c2) + ADAM_EPS) + ADAM_WD * w_ref[...])

    row = pl.BlockSpec((ta, b_cols), lambda i: (i, 0))
    return pl.pallas_call(
        body, name=name, grid=(a_rows // ta,),
        in_specs=[pl.BlockSpec((N_DEV, ta, b_cols), lambda i: (0, i, 0)), row, row, row],
        out_specs=[row] * 4,
        out_shape=[jax.ShapeDtypeStruct((a_rows, b_cols), F32)] * 4,
        compiler_params=pltpu.CompilerParams(dimension_semantics=("arbitrary",), vmem_limit_bytes=VMEM_LIMIT),
    )(parts, w, m, v)


def _pack_rows(arrays, rows):
    flat = jnp.concatenate([a.reshape(-1) for a in arrays])
    return jnp.pad(flat, (0, rows * LANES - flat.shape[0])).reshape(rows, LANES)


def _cols_from_blocks(g):
    return g.transpose(1, 0, 2).reshape(g.shape[1], N_DEV * g.shape[2])


def _blocks_from_cols(w):
    return w.reshape(w.shape[0], N_DEV, w.shape[1] // N_DEV).transpose(1, 0, 2)


def _unpack_flat(flat2d, shapes):
    flat = flat2d.reshape(-1)
    out, off = [], 0
    for shp in shapes:
        sz = math.prod(shp)
        out.append(flat[off:off + sz].reshape(shp))
        off += sz
    return out


def _s5_mats(lam_re, lam_im, log_dt, b_re, b_im, c_re, c_im):
    lr = jnp.minimum(lam_re, S5_MAX_RE)
    li = lam_im
    dt = jnp.exp(log_dt)[:, None]
    mag = jnp.exp(lr * dt)
    ang = li * dt
    ab_re = mag * jnp.cos(ang)
    ab_im = mag * jnp.sin(ang)
    den = lr * lr + li * li
    nr = ab_re - 1.0
    f_re = ((nr * lr + ab_im * li) / den)[..., None]
    f_im = ((ab_im * lr - nr * li) / den)[..., None]
    bb_re = f_re * b_re - f_im * b_im
    bb_im = f_re * b_im + f_im * b_re
    eye = jnp.eye(S5_GROUPS, dtype=F32)
    a = jnp.stack([ab_re.reshape(-1), ab_im.reshape(-1)])
    wb = jnp.concatenate([jnp.einsum("gph,gk->ghkp", bb_re, eye).reshape(S5_WIDTH, NS),
                          jnp.einsum("gph,gk->ghkp", bb_im, eye).reshape(S5_WIDTH, NS)], axis=1)
    wc = jnp.concatenate([jnp.einsum("ghp,gk->gpkh", c_re, eye).reshape(NS, S5_WIDTH),
                          -jnp.einsum("ghp,gk->gpkh", c_im, eye).reshape(NS, S5_WIDTH)], axis=0)
    return a, wb, wc


def _power_table(a):
    ar, ai = a[0], a[1]
    rows_r, rows_i = [ar], [ai]
    for _ in range(7):
        pr, pi = rows_r[-1], rows_i[-1]
        rows_r.append(pr * ar - pi * ai)
        rows_i.append(pr * ai + pi * ar)
    return jnp.concatenate([jnp.stack(rows_r), jnp.stack(rows_i)], axis=1)


def _rope_table(positions):
    inv = ROPE_THETA ** (-jnp.arange(0, MLA_ROPE, 2, dtype=F32) / MLA_ROPE)
    inv128 = jnp.concatenate([jnp.zeros((MLA_NOPE,), F32), inv, inv, jnp.zeros((32,), F32)])
    sign = jnp.concatenate([jnp.zeros((MLA_NOPE,), F32), -jnp.ones((16,), F32), jnp.ones((16,), F32),
                            jnp.zeros((32,), F32)])
    ang = positions.astype(F32)[:, None] * inv128
    return jnp.concatenate([jnp.cos(ang), jnp.sin(ang) * sign], axis=1)


def _derived_weights(full):
    w_in = full["w_in"]
    k1, k2 = w_in[:, 768:784], w_in[:, 784:800]
    z64, z32 = jnp.zeros((1024, 64), BF16), jnp.zeros((1024, 32), BF16)
    w_inx = jnp.concatenate([w_in[:, :768], z64, k1, k2, z32, z64, k2, k1, z32, w_in[:, 800:]], axis=1)
    uq = full["w_uq"].reshape(256, MLA_HEADS, MLA_QK)
    nope, r1, r2 = uq[:, :, :64], uq[:, :, 64:80], uq[:, :, 80:]
    zq64, zq32 = jnp.zeros((256, MLA_HEADS, 64), BF16), jnp.zeros((256, MLA_HEADS, 32), BF16)
    w_q2 = jnp.concatenate([jnp.concatenate([nope, r1, r2, zq32], axis=2).reshape(256, 1024),
                            jnp.concatenate([zq64, r2, r1, zq32], axis=2).reshape(256, 1024)], axis=1)
    ukv = full["w_ukv"].reshape(256, MLA_HEADS, 128)
    w_kv2 = jnp.concatenate([jnp.concatenate([ukv[:, :, :64], zq64], axis=2).reshape(256, 1024),
                             jnp.concatenate([ukv[:, :, 64:], zq64], axis=2).reshape(256, 1024)], axis=1)
    return w_inx, w_q2, w_kv2


def _oa_padded(w_oa):
    oa = w_oa.reshape(MLA_HEADS, MLA_V, 1024)
    return jnp.concatenate([oa, jnp.zeros_like(oa)], axis=1).reshape(1024, 1024)


def _fold_w_in(d_inx):
    d_k1 = d_inx[:, 832:848] + d_inx[:, 976:992]
    d_k2 = d_inx[:, 848:864] + d_inx[:, 960:976]
    return jnp.concatenate([d_inx[:, :768], d_k1, d_k2, d_inx[:, 1024:]], axis=1)


def _fold_qkv(d_q2, d_kv2):
    a = d_q2[:, :1024].reshape(256, MLA_HEADS, 128)
    b = d_q2[:, 1024:].reshape(256, MLA_HEADS, 128)
    d_w_uq = jnp.concatenate([a[:, :, :64], a[:, :, 64:80] + b[:, :, 80:96], a[:, :, 80:96] + b[:, :, 64:80]],
                             axis=2).reshape(256, MLA_HEADS * MLA_QK)
    kk = d_kv2[:, :1024].reshape(256, MLA_HEADS, 128)
    vv = d_kv2[:, 1024:].reshape(256, MLA_HEADS, 128)
    d_w_ukv = jnp.concatenate([kk[:, :, :64], vv[:, :, :64]], axis=2).reshape(256, 1024)
    return d_w_uq, d_w_ukv


def kernel(x, mem, positions, ln_in_g, ln_in_b, w_in, s5_lam_re, s5_lam_im, s5_log_dt, s5_b_re, s5_b_im, s5_c_re, s5_c_im, s5_d, w_glu, q_norm_g, w_uq, kv_norm_g, w_ukv, w_oa, w_o, ln1_g, ln1_b, w_xq, w_xk, w_xv, w_xo, ln2_g, ln2_b, w_up, w_down, ln3_g, ln3_b, loss_target, m_ln_in_g, m_ln_in_b, m_w_in, m_s5_lam_re, m_s5_lam_im, m_s5_log_dt, m_s5_b_re, m_s5_b_im, m_s5_c_re, m_s5_c_im, m_s5_d, m_w_glu, m_q_norm_g, m_w_uq, m_kv_norm_g, m_w_ukv, m_w_oa, m_w_o, m_ln1_g, m_ln1_b, m_w_xq, m_w_xk, m_w_xv, m_w_xo, m_ln2_g, m_ln2_b, m_w_up, m_w_down, m_ln3_g, m_ln3_b, v_ln_in_g, v_ln_in_b, v_w_in, v_s5_lam_re, v_s5_lam_im, v_s5_log_dt, v_s5_b_re, v_s5_b_im, v_s5_c_re, v_s5_c_im, v_s5_d, v_w_glu, v_q_norm_g, v_w_uq, v_kv_norm_g, v_w_ukv, v_w_oa, v_w_o, v_ln1_g, v_ln1_b, v_w_xq, v_w_xk, v_w_xv, v_w_xo, v_ln2_g, v_ln2_b, v_w_up, v_w_down, v_ln3_g, v_ln3_b):
    args = dict(locals())
    wts = {n: args[n] for n in WEIGHTS}
    mom = {n: args["m_" + n] for n in WEIGHTS}
    vel = {n: args["v_" + n] for n in WEIGHTS}
    xs, mems, tgt = x[0], mem[0], loss_target[0]
    s = xs.shape[0]
    names_big = [n for (n, _, _, _) in SHARDED]
    kind = {n: kd for (n, kd, _, _) in SHARDED}
    n_small = sum(math.prod(wts[n].shape) for n in SMALL)
    small_rows = _round_up(n_small + 1, 8 * LANES) // LANES

    late = [n for n in names_big if n not in EARLY]
    shard = {n: wts[n][0].astype(BF16) for n in names_big}
    g_early = dict(zip(EARLY, _all_gather_two_level("all_gather", [shard[n] for n in EARLY])))
    ag_handle, ag_token = _exchange_start("all_gather_late", [shard[n] for n in late], True, after=g_early["w_in"])
    full = {n: _cols_from_blocks(g_early[n]) for n in EARLY}
    w_inx, w_q2, w_kv2 = _derived_weights(full)
    s5_args = (s5_lam_re[0], s5_lam_im[0], s5_log_dt[0], s5_b_re[0], s5_b_im[0], s5_c_re[0], s5_c_im[0])
    (a_mat, w_b, w_c), s5_vjp = jax.vjp(_s5_mats, *s5_args)
    w_bb, w_cb = w_b.astype(BF16), w_c.astype(BF16)
    pw = _power_table(a_mat)
    pwb = _power_table(a_mat * jnp.array([[1.0], [-1.0]], F32))[::-1]
    tab = _rope_table(positions[0])
    row = lambda a: a.reshape(1, -1)
    ln_g, ln_b = row(ln_in_g) + ag_token[0:1, 0:1], row(ln_in_b)
    lns = [ln1_g, ln1_b, ln2_g, ln2_b]

    h0, z3, gate, bu, q, k, v, kt, vt = _in_fwd(xs, tab, ln_g, ln_b, q_norm_g, kv_norm_g, w_inx, w_bb, w_q2, w_kv2)
    h = _scan_fwd(pw, bu)
    yl, s_out = _s5out_fwd(h, z3, s5_d, w_cb, full["w_glu"])
    o, lse_t = _attn_fwd(q, k, vt)
    g_late = dict(zip(late, _exchange_wait("all_gather_late_wait", ag_handle, after=lse_t)))
    full.update({n: g_late[n].reshape(-1, g_late[n].shape[2]) if kind[n] == "row" else _cols_from_blocks(g_late[n])
                 for n in late})
    w_oap = _oa_padded(full["w_oa"])
    memk, memv = _mem_kv(mems, full["w_xk"], full["w_xv"])
    r2, mixin_b, h1_b, ox_b, a_out, r1, qx_b = _post_fwd(o, s_out, gate, h0, memk, memv, lns,
                                        w_oap, full["w_o"], full["w_xq"], full["w_xo"])
    dh2, h2_b, dup_b, act_b, dff_b, loss_acc, d_ln3_g, d_ln3_b = _mlp(
        r2, tgt, ln2_g, ln2_b, ln3_g, ln3_b, full["w_up"], full["w_down"])
    (do_b, delta, d_so, dgate, dh0p, daout_b, dmix_b, dqx_b, dxa_b,
     dmk, dmv, d_ln1_g, d_ln1_b, d_ln2_g, d_ln2_b) = _post_bwd(
        dh2, o, s_out, gate, a_out, r1, r2, qx_b, memk, memv, lns, w_oap, full["w_o"], full["w_xq"], full["w_xo"])
    rows8 = lambda g: g.reshape(N_DEV, g.shape[0] // N_DEV, g.shape[1])
    d_w_oa = _wgrad("wg_oa", o, daout_b).reshape(MLA_HEADS, 128, 1024)[:, :64].reshape(512, 1024)
    send = {
        "w_oa": _blocks_from_cols(d_w_oa).astype(BF16),
        "w_up": _wgrad("wg_up", h2_b, dup_b, BF16, N_DEV),
        "w_o": rows8(_wgrad("wg_o", mixin_b, dmix_b, BF16)),
        "w_xq": rows8(_wgrad("wg_xq", h1_b, dqx_b, BF16)),
        "w_xk": rows8(_wgrad("wg_xk", mems, dmk, BF16)),
        "w_xv": rows8(_wgrad("wg_xv", mems, dmv, BF16)),
        "w_xo": rows8(_wgrad("wg_xo", ox_b, dxa_b, BF16)),
        "w_down": rows8(_wgrad("wg_down", act_b, dff_b, BF16)),
    }
    rs_handle, rs_token = _exchange_start("grad_exchange_late", [send[n] for n in late], False)
    dl_t = delta.T[:MLA_HEADS].reshape(MLA_HEADS, 1, s) + rs_token[0, 0]
    dq, dk, dv = _attn_bwd(q, k, v, kt, do_b, lse_t, dl_t)
    gh, du_p, yg_b, dy12_b, dyl_b, d_s5_d = _s5out_bwd(d_so, yl, z3, s5_d, w_cb, full["w_glu"])
    lam, d_a = _scan_bwd(pwb, gh, h)
    (dx, h0_b, dz_b, u_b, cqn_b, ckvn_b, dq2_b, dkv2_b, d_ln_g, d_ln_b, d_qg, d_kvg) = _in_bwd(
        xs, tab, z3, dq, dk, dv, lam, du_p, dgate, dh0p, ln_g, ln_b, q_norm_g, kv_norm_g, w_inx, w_bb, w_q2, w_kv2)

    d_w_in = _fold_w_in(_wgrad("wg_in", h0_b, dz_b))
    win_handle, win_token = _exchange_start("grad_exchange_w_in", [_blocks_from_cols(d_w_in).astype(BF16)], False)
    d_w_uq, d_w_ukv = _fold_qkv(_wgrad("wg_q", cqn_b, dq2_b, after=win_token), _wgrad("wg_kv", ckvn_b, dkv2_b))
    send.update({
        "w_uq": _blocks_from_cols(d_w_uq).astype(BF16), "w_ukv": _blocks_from_cols(d_w_ukv).astype(BF16),
        "w_glu": _wgrad("wg_glu", yg_b, dy12_b, BF16, N_DEV, after=win_token),
    })
    d_s5 = s5_vjp((d_a.reshape(2, NS), _wgrad("wg_s5b", u_b, lam, after=win_token),
                   _wgrad("wg_s5c", h, dyl_b, after=win_token)))
    small_grads = {
        "ln_in_g": d_ln_g, "ln_in_b": d_ln_b, "s5_lam_re": d_s5[0], "s5_lam_im": d_s5[1], "s5_log_dt": d_s5[2],
        "s5_b_re": d_s5[3], "s5_b_im": d_s5[4], "s5_c_re": d_s5[5], "s5_c_im": d_s5[6], "s5_d": d_s5_d,
        "q_norm_g": d_qg, "kv_norm_g": d_kvg, "ln1_g": d_ln1_g, "ln1_b": d_ln1_b, "ln2_g": d_ln2_g,
        "ln2_b": d_ln2_b, "ln3_g": d_ln3_g, "ln3_b": d_ln3_b,
    }
    small_send = jnp.broadcast_to(_pack_rows([small_grads[n] for n in SMALL] + [loss_acc[0, :1]], small_rows)[None],
                                  (N_DEV, small_rows, LANES))

    early_rest = [n for n in EARLY if n != "w_in"]
    recv_early = _all_to_all("grad_exchange", [send[n] for n in early_rest] + [small_send], False)
    recv_late = _exchange_wait("grad_exchange_late_wait", rs_handle, after=recv_early[-1])
    recv = dict(zip(early_rest + late, list(recv_early[:-1]) + recv_late))
    results = [dict(), dict(), dict(), dict()]
    for n in late + early_rest + ["w_in"]:
        if n == "w_in":
            recv[n] = _exchange_wait("grad_exchange_w_in_wait", win_handle, after=results[0][early_rest[-1]])[0]
        parts = recv[n]
        outs = _adamw("adamw_" + n, parts, wts[n][0], mom[n][0], vel[n][0])
        for res, a in zip(results, outs):
            res[n] = a[None]
    small_out = _adamw("adamw_small", recv_early[-1],
                       *[_pack_rows([t[n] for n in SMALL], small_rows) for t in (wts, mom, vel)])
    for res, fs in zip(results, small_out):
        for n, a in zip(SMALL, _unpack_flat(fs, [wts[n].shape for n in SMALL])):
            res[n] = a

    loss = small_out[0].reshape(-1)[n_small]
    return (loss, dx[None], *[res[n] for res in results for n in WEIGHTS])
```

```python
import math

import jax
import jax.numpy as jnp
from jax import lax
from jax.experimental import pallas as pl
from jax.experimental.pallas import tpu as pltpu

F32 = jnp.float32
BF16 = jnp.bfloat16

D_MODEL = 1024
S5_WIDTH = 256
S5_GROUP_CH = 16
S5_GROUPS = 16
S5_STATE = 64
NS = S5_GROUPS * S5_STATE
S5_MAX_RE = -1e-4
MLA_HEADS = 8
MLA_NOPE = 64
MLA_ROPE = 32
MLA_QK = 96
MLA_V = 64
HEAD_PAD = 128
ROPE_THETA = 10000.0
XATTN_HEADS = 4
XATTN_HD = 256
MLP_HIDDEN = 4096
LN_EPS = 1e-5
RMS_EPS = 1e-6
NEG_INF = -1e30
LOG2E = 1.4426950408889634
DN_ALPHA = 2.0 ** 0.25
ADAM_LR = 0.001
ADAM_B1 = 0.9
ADAM_B2 = 0.999
ADAM_EPS = 1e-08
ADAM_WD = 0.01
ADAM_STEP = 10

N_DEV = 8
MESH = pl.DeviceIdType.MESH
LANES = 1024
VMEM_LIMIT = 60 * 1024 * 1024

ROW_TILE = 256
ATT_TILE = 1024
ADAM_TILE = 128

SHARDED = (
    ("w_in", "col", 1024, 2848), ("w_glu", "col", 256, 2048), ("w_uq", "col", 256, 768),
    ("w_ukv", "col", 256, 1024), ("w_oa", "col", 512, 1024), ("w_o", "row", 1024, 1024),
    ("w_xq", "row", 1024, 1024), ("w_xk", "row", 1024, 1024), ("w_xv", "row", 1024, 1024),
    ("w_xo", "row", 1024, 1024), ("w_up", "col", 1024, 4096), ("w_down", "row", 4096, 1024),
)
EARLY = ("w_in", "w_glu", "w_uq", "w_ukv")
SMALL = ("ln_in_g", "ln_in_b", "s5_lam_re", "s5_lam_im", "s5_log_dt", "s5_b_re", "s5_b_im", "s5_c_re",
         "s5_c_im", "s5_d", "q_norm_g", "kv_norm_g", "ln1_g", "ln1_b", "ln2_g", "ln2_b", "ln3_g", "ln3_b")
WEIGHTS = ("ln_in_g", "ln_in_b", "w_in", "s5_lam_re", "s5_lam_im", "s5_log_dt", "s5_b_re", "s5_b_im",
           "s5_c_re", "s5_c_im", "s5_d", "w_glu", "q_norm_g", "w_uq", "kv_norm_g", "w_ukv", "w_oa", "w_o",
           "ln1_g", "ln1_b", "w_xq", "w_xk", "w_xv", "w_xo", "ln2_g", "ln2_b", "w_up", "w_down", "ln3_g", "ln3_b")


def _round_up(n, m):
    return (n + m - 1) // m * m


def _bf(a):
    return a.astype(BF16)


def _mm(a, b):
    return jnp.dot(_bf(a), _bf(b), preferred_element_type=F32)


def _mm_nt(a, b):
    return lax.dot_general(_bf(a), _bf(b), (((1,), (1,)), ((), ())), preferred_element_type=F32)


def _mm_tn(a, b):
    return lax.dot_general(_bf(a), _bf(b), (((0,), (0,)), ((), ())), preferred_element_type=F32)


def _sigmoid(a):
    return 1.0 / (1.0 + jnp.exp(-a))


def _gelu(a):
    return 0.5 * a * (1.0 + lax.erf(a * (2.0 ** -0.5)))


def _gelu_grad(a):
    return 0.5 * (1.0 + lax.erf(a * (2.0 ** -0.5))) + a * jnp.exp(-0.5 * a * a) * (1.0 / math.sqrt(2.0 * math.pi))


def _ln_fwd(a, g, b):
    mu = jnp.mean(a, axis=-1, keepdims=True)
    ac = a - mu
    var = jnp.mean(ac * ac, axis=-1, keepdims=True)
    rstd = lax.rsqrt(var + LN_EPS)
    xhat = ac * rstd
    return xhat * g + b, xhat, rstd


def _ln_bwd(dy, xhat, rstd, g):
    dxh = dy * g
    m1 = jnp.mean(dxh, axis=-1, keepdims=True)
    m2 = jnp.mean(dxh * xhat, axis=-1, keepdims=True)
    dx = rstd * (dxh - m1 - xhat * m2)
    return dx, jnp.sum(dy * xhat, axis=0, keepdims=True), jnp.sum(dy, axis=0, keepdims=True)


def _rms_fwd(a, g):
    r = lax.rsqrt(jnp.mean(a * a, axis=-1, keepdims=True) + RMS_EPS)
    xn = a * r
    return xn * g, xn, r


def _rms_bwd(dy, xn, r, g):
    dxn = dy * g
    dx = r * (dxn - xn * jnp.mean(dxn * xn, axis=-1, keepdims=True))
    return dx, jnp.sum(dy * xn, axis=0, keepdims=True)


def _tile_heads(a):
    return jnp.concatenate([a] * MLA_HEADS, axis=1)


def _row_call(name, body, n_rows, ts, tiled_in, full_in, weights, tiled_out, acc_out, reverse=False, scratch=(),
              tiled_out_t=()):
    n = n_rows // ts
    assert n * ts == n_rows, (name, n_rows, ts)
    nt, nf, nw = len(tiled_in), len(full_in), len(weights)
    nto, nao = len(tiled_out) + len(tiled_out_t), len(acc_out)
    if reverse:
        imap = lambda i: (n - 1 - i, 0)
    else:
        imap = lambda i: (i, 0)
    const = lambda i: (0, 0)

    def kern(*refs):
        ins = refs[:nt + nf]
        w_hbm = refs[nt + nf:nt + nf + nw]
        outs = refs[nt + nf + nw:nt + nf + nw + nto + nao]
        scr = refs[nt + nf + nw + nto + nao:]
        w_vmem = scr[:nw]
        extra = scr[nw + 1:] if nw else scr
        if nw:
            sem = scr[nw]

            @pl.when(pl.program_id(0) == 0)
            def _():
                cps = [pltpu.make_async_copy(w_hbm[k], w_vmem[k], sem.at[k]) for k in range(nw)]
                for cp in cps:
                    cp.start()
                for cp in cps:
                    cp.wait()
        body(*ins, *w_vmem, *outs, *extra)

    in_specs = [pl.BlockSpec((ts, a.shape[1]), imap) for a in tiled_in]
    in_specs += [pl.BlockSpec(a.shape, const) for a in full_in]
    in_specs += [pl.BlockSpec(memory_space=pl.ANY) for _ in weights]
    assert not (reverse and tiled_out_t)
    out_shape = [jax.ShapeDtypeStruct((n_rows, c), dt) for (c, dt) in tiled_out]
    out_shape += [jax.ShapeDtypeStruct((c, n_rows), dt) for (c, dt) in tiled_out_t]
    out_shape += [jax.ShapeDtypeStruct(shp, F32) for shp in acc_out]
    out_specs = [pl.BlockSpec((ts, c), imap) for (c, dt) in tiled_out]
    out_specs += [pl.BlockSpec((c, ts), lambda i: (0, i)) for (c, dt) in tiled_out_t]
    out_specs += [pl.BlockSpec(shp, const) for shp in acc_out]
    scratch_shapes = [pltpu.VMEM(w.shape, w.dtype) for w in weights]
    if nw:
        scratch_shapes.append(pltpu.SemaphoreType.DMA((nw,)))
    scratch_shapes += list(scratch)
    return pl.pallas_call(
        kern, name=name, grid=(n,), in_specs=in_specs, out_specs=out_specs, out_shape=out_shape,
        scratch_shapes=scratch_shapes,
        compiler_params=pltpu.CompilerParams(dimension_semantics=("arbitrary",), vmem_limit_bytes=VMEM_LIMIT),
    )(*tiled_in, *full_in, *weights)


def _mem_kv(mem, w_xk, w_xv):
    m = mem.shape[0]

    def body(mem_ref, wk_ref, wv_ref, k_ref, v_ref):
        mb = mem_ref[...]
        k_ref[...] = _mm(mb, wk_ref[...]).astype(BF16)
        v_ref[...] = _mm(mb, wv_ref[...]).astype(BF16)

    return pl.pallas_call(
        body, name="mem_kv",
        out_shape=[jax.ShapeDtypeStruct((m, D_MODEL), BF16)] * 2,
        compiler_params=pltpu.CompilerParams(vmem_limit_bytes=VMEM_LIMIT),
    )(mem, w_xk, w_xv)


def _in_fwd(x, tab, ln_g, ln_b, qg, kvg, pw, w_inx, w_b, w_q2, w_kv2):
    ts = min(ROW_TILE, x.shape[0])

    def body(x_ref, tab_ref, lng_ref, lnb_ref, qg_ref, kvg_ref, pw_ref, winx, wb, wq2, wkv2,
             h0_ref, z3_ref, gate_ref, h_ref, q_ref, k_ref, v_ref, kt_ref, vt_ref, bu_scr, carry):
        @pl.when(pl.program_id(0) == 0)
        def _():
            carry[...] = jnp.zeros_like(carry)

        h0, _, _ = _ln_fwd(x_ref[...], lng_ref[...], lnb_ref[...])
        h0_ref[...] = h0
        z = _mm(h0, winx[...])
        z3_ref[...] = z[:, :768]
        gate_ref[...] = z[:, 1024:].astype(BF16)
        bu_scr[...] = _mm(z[:, :256], wb[...])
        _scan_fwd_rows(pw_ref, bu_scr, h_ref, carry, ts // 8)
        tab_v = tab_ref[...]
        cq1, cq2, ck1 = tab_v[:, :128], tab_v[:, 128:], tab_v[:, :128]
        cqn, _, _ = _rms_fwd(z[:, 256:512], qg_ref[...])
        q2 = _mm(cqn, wq2[...])
        q_ref[...] = (q2[:, :1024] * _tile_heads(cq1) + q2[:, 1024:] * _tile_heads(cq2)).astype(BF16)
        ckvn, _, _ = _rms_fwd(z[:, 512:768], kvg_ref[...])
        kv2 = _mm(ckvn, wkv2[...])
        krp = z[:, 768:896] * ck1 + z[:, 896:1024] * cq2
        kf = kv2[:, :1024] + _tile_heads(krp)
        k_ref[...] = kf.astype(BF16)
        v_ref[...] = kv2[:, 1024:].astype(BF16)
        kt_ref[...] = kf.T.astype(BF16)
        vt_ref[...] = kv2[:, 1024:].T.astype(BF16)

    s = x.shape[0]
    return _row_call(
        "in_fwd", body, s, ts, [x, tab], [ln_g, ln_b, qg, kvg, pw], [w_inx, w_b, w_q2, w_kv2],
        [(1024, F32), (768, F32), (2048, BF16), (2048, F32), (1024, BF16), (1024, BF16), (1024, BF16)], [],
        tiled_out_t=[(1024, BF16), (1024, BF16)], scratch=[pltpu.VMEM((ts, 2 * NS), F32), pltpu.VMEM((8, 2 * NS), F32)])


def _scan_coeffs(pw_ref, rows, forward):
    row = lax.broadcasted_iota(jnp.int32, (8, NS), 0)
    out = []
    for k, r in zip((1, 2, 4), rows):
        keep = (row >= k) if forward else (row < 8 - k)
        out.append((jnp.where(keep, pw_ref[r:r + 1, :NS], 0.0), jnp.where(keep, pw_ref[r:r + 1, NS:], 0.0), k))
    return out


def _scan_level(xr, xi, coeff, forward):
    ar, ai, k = coeff
    shift = k if forward else 8 - k
    sr, si = pltpu.roll(xr, shift, 0), pltpu.roll(xi, shift, 0)
    return xr + ar * sr - ai * si, xi + ar * si + ai * sr


def _scan_fwd_rows(pw_ref, bu_ref, h_ref, carry, nblk):
    coeffs = _scan_coeffs(pw_ref, (0, 1, 3), True)

    def blk(b, _):
        r0 = pl.multiple_of(b * 8, 8)
        xr = bu_ref[pl.ds(r0, 8), :NS]
        xi = bu_ref[pl.ds(r0, 8), NS:]
        for coeff in coeffs:
            xr, xi = _scan_level(xr, xi, coeff, True)
        cr, ci = carry[7:8, :NS], carry[7:8, NS:]
        pr, pi = pw_ref[:, :NS], pw_ref[:, NS:]
        hr = xr + pr * cr - pi * ci
        hi = xi + pr * ci + pi * cr
        h_ref[pl.ds(r0, 8), :NS] = hr
        h_ref[pl.ds(r0, 8), NS:] = hi
        carry[:, :NS] = hr
        carry[:, NS:] = hi
        return 0

    lax.fori_loop(0, nblk, blk, 0)


def _scan_bwd_rows(pw_ref, g_ref, h_ref, lam_ref, carry, acc, nblk):
    row = lax.broadcasted_iota(jnp.int32, (8, NS), 0)
    coeffs = _scan_coeffs(pw_ref, (7, 6, 4), False)

    def blk(bb, _):
        r0 = pl.multiple_of((nblk - 1 - bb) * 8, 8)
        xr = g_ref[pl.ds(r0, 8), :NS]
        xi = g_ref[pl.ds(r0, 8), NS:]
        for coeff in coeffs:
            xr, xi = _scan_level(xr, xi, coeff, False)
        cr, ci = carry[0:1, :NS], carry[0:1, NS:]
        pr, pi = pw_ref[:, :NS], pw_ref[:, NS:]
        lr = xr + pr * cr - pi * ci
        li = xi + pr * ci + pi * cr
        lam_ref[pl.ds(r0, 8), :NS] = lr
        lam_ref[pl.ds(r0, 8), NS:] = li
        nr = jnp.where(row < 7, pltpu.roll(lr, 7, 0), cr)
        ni = jnp.where(row < 7, pltpu.roll(li, 7, 0), ci)
        hr = h_ref[pl.ds(r0, 8), :NS]
        hi = h_ref[pl.ds(r0, 8), NS:]
        acc[:, :NS] += nr * hr + ni * hi
        acc[:, NS:] += ni * hr - nr * hi
        carry[:, :NS] = lr
        carry[:, NS:] = li
        return 0

    lax.fori_loop(0, nblk, blk, 0)


def _s5out_fwd(h, z3, d_skip, w_c, w_glu):
    def body(h_ref, z3_ref, d_ref, wc, wglu, yl_ref, so_ref):
        yl = _mm(h_ref[...], wc[...]) + d_ref[...] * z3_ref[:, :256]
        yl_ref[...] = yl
        y12 = _mm(_gelu(yl), wglu[...])
        so_ref[...] = y12[:, :1024] * _sigmoid(y12[:, 1024:])

    s = h.shape[0]
    return _row_call("s5out_fwd", body, s, min(ROW_TILE, s), [h, z3], [d_skip], [w_c, w_glu],
                     [(256, F32), (1024, F32)], [])


def _causal_mask_t(t):
    row = lax.broadcasted_iota(jnp.int32, (t, t), 0)
    col = lax.broadcasted_iota(jnp.int32, (t, t), 1)
    return row <= col


def _attn_fwd(q, k, vt):
    s = q.shape[0]
    t = min(ATT_TILE, s)
    nq = s // t
    scale = MLA_QK ** -0.5
    c2 = scale * LOG2E

    def body(q_ref, k_ref, vt_ref, o_ref, lse_ref, s0, s1):
        i = pl.program_id(1)
        qb = q_ref[...]

        def scores(kb, dst):
            st = _mm_nt(k_ref[pl.ds(pl.multiple_of(kb * t, t), t), :], qb)
            dst[...] = st
            return jnp.max(st, axis=0, keepdims=True)

        def update(kb, src, mt, state, masked):
            m, l, acc = state
            if masked:
                keep = _causal_mask_t(t)
                mt = jnp.max(jnp.where(keep, src[...], NEG_INF), axis=0, keepdims=True)
            m_new = jnp.maximum(m, mt)
            if masked:
                p = jnp.exp2((jnp.where(keep, src[...], NEG_INF) - m_new) * c2)
            else:
                p = jnp.exp2((src[...] - m_new) * c2)
            a = jnp.exp2((m - m_new) * c2)
            l = a * l + jnp.sum(p, axis=0, keepdims=True)
            acc = a * acc + _mm(vt_ref[:, pl.ds(pl.multiple_of(kb * t, t), t)], p)
            return m_new, l, acc

        def pair(jj, carry):
            mt_a, state = carry
            mt_b = scores(2 * jj + 1, s1)
            state = update(2 * jj, s0, mt_a, state, False)
            mt_a = scores(2 * jj + 2, s0)
            return mt_a, update(2 * jj + 1, s1, mt_b, state, False)

        def odd_tail(carry):
            mt_a, state = carry
            mt_b = scores(i, s1)
            state = update(i - 1, s0, mt_a, state, False)
            return update(i, s1, mt_b, state, True)

        def even_tail(carry):
            mt_a, state = carry
            return update(i, s0, mt_a, state, True)

        init = (jnp.full((1, t), NEG_INF, F32), jnp.zeros((1, t), F32), jnp.zeros((HEAD_PAD, t), F32))
        carry = lax.fori_loop(0, i // 2, pair, (scores(0, s0), init))
        m, l, acc = lax.cond(i % 2 == 1, odd_tail, even_tail, carry)
        o_ref[...] = (acc / l).T
        lse_ref[...] = m * scale + jnp.log(l)

    return pl.pallas_call(
        body, name="attn_fwd", grid=(MLA_HEADS, nq),
        in_specs=[pl.BlockSpec((t, HEAD_PAD), lambda h, i: (i, h)),
                  pl.BlockSpec((s, HEAD_PAD), lambda h, i: (0, h)),
                  pl.BlockSpec((HEAD_PAD, s), lambda h, i: (h, 0))],
        out_specs=[pl.BlockSpec((t, HEAD_PAD), lambda h, i: (i, h)),
                   pl.BlockSpec((None, 1, t), lambda h, i: (h, 0, i))],
        out_shape=[jax.ShapeDtypeStruct((s, MLA_HEADS * HEAD_PAD), F32),
                   jax.ShapeDtypeStruct((MLA_HEADS, 1, s), F32)],
        scratch_shapes=[pltpu.VMEM((t, t), F32)] * 2,
        compiler_params=pltpu.CompilerParams(dimension_semantics=("arbitrary", "arbitrary"),
                                             vmem_limit_bytes=VMEM_LIMIT),
    )(q, k, vt)


def _attn_bwd(q, k, v, kt, do, lse_t, dl_t):
    s = q.shape[0]
    t = min(ATT_TILE, s)
    nq = s // t
    scale = MLA_QK ** -0.5
    c2 = scale * LOG2E

    def body(q_ref, k_ref, v_ref, kt_ref, do_ref, lse_ref, dl_ref, dq_ref, dk_ref, dv_ref, dqt, s0, p0, s1, p1):
        j = pl.program_id(1)
        n = nq - 1 - j

        @pl.when(j == 0)
        def _():
            dqt[...] = jnp.zeros_like(dqt)

        kk = k_ref[...]
        vv = v_ref[...]
        ktb = kt_ref[:MLA_QK, :]

        def rows(m):
            return pl.ds(pl.multiple_of(jnp.where(m < n, j + 1 + m, j) * t, t), t)

        def first(m, sbuf, pbuf):
            r = rows(m)
            sbuf[...] = _mm_nt(kk, q_ref[r, :])
            pbuf[...] = _mm_nt(vv, do_ref[r, :])

        def finish(m, sbuf, pbuf, acc, masked):
            dk, dv = acc
            r = rows(m)
            pt = jnp.exp2(sbuf[...] * c2 - lse_ref[:, r] * LOG2E)
            if masked:
                pt = jnp.where(_causal_mask_t(t), pt, 0.0)
            dv = dv + _mm(pt, do_ref[r, :])
            dst = (pt * (pbuf[...] - dl_ref[:, r])).astype(BF16)
            dk = dk + _mm(dst, q_ref[r, :])
            dqt[:MLA_QK, r] += _mm(ktb, dst)
            return dk, dv

        def pair(jj, acc):
            first(2 * jj + 1, s1, p1)
            acc = finish(2 * jj, s0, p0, acc, False)
            first(2 * jj + 2, s0, p0)
            return finish(2 * jj + 1, s1, p1, acc, False)

        def odd_tail(acc):
            first(n, s1, p1)
            acc = finish(n - 1, s0, p0, acc, False)
            return finish(n, s1, p1, acc, True)

        def even_tail(acc):
            return finish(n, s0, p0, acc, True)

        zero = jnp.zeros((t, HEAD_PAD), F32)
        first(0, s0, p0)
        acc = lax.fori_loop(0, n // 2, pair, (zero, zero))
        dk, dv = lax.cond(n % 2 == 1, odd_tail, even_tail, acc)
        dk_ref[...] = (dk * scale).astype(BF16)
        dv_ref[...] = dv.astype(BF16)

        @pl.when(j == nq - 1)
        def _():
            for cc in range(nq):
                dq_ref[cc * t:(cc + 1) * t, :] = (dqt[:, cc * t:(cc + 1) * t].T * scale).astype(BF16)

    full = pl.BlockSpec((s, HEAD_PAD), lambda h, j: (0, h))
    tile = pl.BlockSpec((t, HEAD_PAD), lambda h, j: (j, h))
    stat = pl.BlockSpec((None, 1, s), lambda h, j: (h, 0, 0))
    return pl.pallas_call(
        body, name="attn_bwd", grid=(MLA_HEADS, nq),
        in_specs=[full, tile, tile, pl.BlockSpec((HEAD_PAD, t), lambda h, j: (h, j)), full, stat, stat],
        out_specs=[full, tile, tile],
        out_shape=[jax.ShapeDtypeStruct((s, MLA_HEADS * HEAD_PAD), BF16)] * 3,
        scratch_shapes=[pltpu.VMEM((HEAD_PAD, s), F32)] + [pltpu.VMEM((t, t), F32)] * 4,
        compiler_params=pltpu.CompilerParams(dimension_semantics=("arbitrary", "arbitrary"),
                                             vmem_limit_bytes=VMEM_LIMIT),
    )(q, k, v, kt, do, lse_t, dl_t)


def _xattn_probs(qxb, memk, hh):
    sl = slice(hh * XATTN_HD, (hh + 1) * XATTN_HD)
    sc = _mm_nt(qxb[:, sl], memk[:, sl]) * (XATTN_HD ** -0.5)
    e = jnp.exp(sc - jnp.max(sc, axis=1, keepdims=True))
    return e / jnp.sum(e, axis=1, keepdims=True)


def _post_forward(o, s_out, gate, h0, memk, memv, w_oa, w_o, w_xq, w_xo, g1, b1, g2, b2):
    a_out = _mm(o, w_oa)
    sg_s = _sigmoid(gate[:, :1024])
    sg_a = _sigmoid(gate[:, 1024:])
    mixin = sg_s * s_out + sg_a * a_out
    r1 = DN_ALPHA * h0 + _mm(mixin, w_o)
    h1, xh1, rs1 = _ln_fwd(r1, g1, b1)
    qxb = _mm(h1, w_xq).astype(BF16)
    ox = jnp.concatenate([_mm(_xattn_probs(qxb, memk, hh), memv[:, hh * XATTN_HD:(hh + 1) * XATTN_HD])
                          for hh in range(XATTN_HEADS)], axis=1)
    r2 = DN_ALPHA * h1 + _mm(ox, w_xo)
    return dict(a_out=a_out, mixin=mixin, r1=r1, h1=h1, qxb=qxb, ox=ox, r2=r2)


def _post_fwd(o, s_out, gate, h0, memk, memv, lns, w_oa, w_o, w_xq, w_xo):
    def body(o_ref, so_ref, gate_ref, h0_ref, mk_ref, mv_ref, g1, b1, g2, b2, woa, wo, wxq, wxo,
             r2_ref, mixin_ref, h1_ref, ox_ref, aout_ref, r1_ref, qx_ref):
        f = _post_forward(o_ref[...], so_ref[...], gate_ref[...].astype(F32), h0_ref[...], mk_ref[...], mv_ref[...],
                          woa[...], wo[...], wxq[...], wxo[...], g1[...], b1[...], g2[...], b2[...])
        r2_ref[...] = f["r2"]
        mixin_ref[...] = f["mixin"].astype(BF16)
        h1_ref[...] = f["h1"].astype(BF16)
        ox_ref[...] = f["ox"].astype(BF16)
        aout_ref[...] = f["a_out"]
        r1_ref[...] = f["r1"]
        qx_ref[...] = f["qxb"]

    s = o.shape[0]
    return _row_call("post_fwd", body, s, min(ROW_TILE, s), [o, s_out, gate, h0], [memk, memv, *lns],
                     [w_oa, w_o, w_xq, w_xo],
                     [(1024, F32), (1024, BF16), (1024, BF16), (1024, BF16), (1024, F32), (1024, F32), (1024, BF16)], [])


def _mlp(r2, target, g2, b2, g3, b3, w_up, w_down):
    cw = 1024
    n_chunk = MLP_HIDDEN // cw

    def body(r2_ref, tgt_ref, g2_ref, b2_ref, g3_ref, b3_ref, wup, wdn,
             dh2_ref, h2b_ref, dup_ref, act_ref, dff_ref, loss_ref, dg3_ref, db3_ref, up_scr):
        @pl.when(pl.program_id(0) == 0)
        def _():
            loss_ref[...] = jnp.zeros_like(loss_ref)
            dg3_ref[...] = jnp.zeros_like(dg3_ref)
            db3_ref[...] = jnp.zeros_like(db3_ref)

        h2, _, _ = _ln_fwd(r2_ref[...], g2_ref[...], b2_ref[...])
        h2b = h2.astype(BF16)
        h2b_ref[...] = h2b
        ff = jnp.zeros(h2.shape, F32)
        for c in range(n_chunk):
            sl = slice(c * cw, (c + 1) * cw)
            a = jnp.maximum(_mm(h2b, wup[:, sl]), 0.0)
            up_scr[:, sl] = a
            actb = (a * a).astype(BF16)
            act_ref[:, sl] = actb
            ff = ff + _mm(actb, wdn[sl, :])
        h3, xh3, rs3 = _ln_fwd(DN_ALPHA * h2 + ff, g3_ref[...], b3_ref[...])
        err = h3 - tgt_ref[...]
        loss_ref[...] += 0.5 * jnp.sum(err * err) * (1.0 / D_MODEL)
        dr3, dg3, db3 = _ln_bwd(err * (1.0 / D_MODEL), xh3, rs3, g3_ref[...])
        dg3_ref[...] += dg3
        db3_ref[...] += db3
        dffb = dr3.astype(BF16)
        dff_ref[...] = dffb
        dh2 = DN_ALPHA * dr3
        for c in range(n_chunk):
            sl = slice(c * cw, (c + 1) * cw)
            dupb = (_mm_nt(dffb, wdn[sl, :]) * (2.0 * up_scr[:, sl])).astype(BF16)
            dup_ref[:, sl] = dupb
            dh2 = dh2 + _mm_nt(dupb, wup[:, sl])
        dh2_ref[...] = dh2

    s = r2.shape[0]
    ts = min(ROW_TILE, s)
    return _row_call("mlp", body, s, ts, [r2, target], [g2, b2, g3, b3], [w_up, w_down],
                     [(1024, F32), (1024, BF16), (MLP_HIDDEN, BF16), (MLP_HIDDEN, BF16), (1024, BF16)],
                     [(8, 128), (1, 1024), (1, 1024)], scratch=[pltpu.VMEM((ts, MLP_HIDDEN), F32)])


def _post_bwd(dh2, o, s_out, gate, a_out, r1, r2, qx, memk, memv, lns, w_oa, w_o, w_xq, w_xo):
    def body(dh2_ref, o_ref, so_ref, gate_ref, aout_ref, r1_ref, r2_ref, qx_ref, mk_ref, mv_ref, g1, b1, g2, b2,
             woa, wo, wxq, wxo,
             do_ref, dl_ref, dso_ref, dgate_ref, dh0_ref, daout_ref, dmix_ref, dqx_ref, dxa_ref,
             dmk_ref, dmv_ref, dg1_ref, db1_ref, dg2_ref, db2_ref):
        @pl.when(pl.program_id(0) == 0)
        def _():
            for r in (dmk_ref, dmv_ref, dg1_ref, db1_ref, dg2_ref, db2_ref):
                r[...] = jnp.zeros_like(r)

        o = o_ref[...]
        s_out = so_ref[...]
        memk, memv = mk_ref[...], mv_ref[...]
        qxb = qx_ref[...]
        _, xh1, rs1 = _ln_fwd(r1_ref[...], g1[...], b1[...])
        _, xh2, rs2 = _ln_fwd(r2_ref[...], g2[...], b2[...])
        dr2, dg2, db2 = _ln_bwd(dh2_ref[...], xh2, rs2, g2[...])
        dg2_ref[...] += dg2
        db2_ref[...] += db2
        dxab = dr2.astype(BF16)
        dxa_ref[...] = dxab
        dox = _mm_nt(dxab, wxo[...])
        dqs = []
        for hh in range(XATTN_HEADS):
            sl = slice(hh * XATTN_HD, (hh + 1) * XATTN_HD)
            p = _xattn_probs(qxb, memk, hh)
            doxh = dox[:, sl].astype(BF16)
            dp = _mm_nt(doxh, memv[:, sl])
            ds = (p * (dp - jnp.sum(dp * p, axis=1, keepdims=True)) * (XATTN_HD ** -0.5)).astype(BF16)
            dqs.append(_mm(ds, memk[:, sl]))
            dmk_ref[:, sl] += _mm_tn(ds, qxb[:, sl])
            dmv_ref[:, sl] += _mm_tn(p, doxh)
        dqxb = jnp.concatenate(dqs, axis=1).astype(BF16)
        dqx_ref[...] = dqxb
        dh1 = DN_ALPHA * dr2 + _mm_nt(dqxb, wxq[...])
        dr1, dg1, db1 = _ln_bwd(dh1, xh1, rs1, g1[...])
        dg1_ref[...] += dg1
        db1_ref[...] += db1
        dh0_ref[...] = DN_ALPHA * dr1
        dmixb = dr1.astype(BF16)
        dmix_ref[...] = dmixb
        dmixin = _mm_nt(dmixb, wo[...])
        sg_s = _sigmoid(gate_ref[:, :1024].astype(F32))
        sg_a = _sigmoid(gate_ref[:, 1024:].astype(F32))
        dso_ref[...] = dmixin * sg_s
        daoutb = (dmixin * sg_a).astype(BF16)
        daout_ref[...] = daoutb
        dgate_ref[:, :1024] = (dmixin * s_out * sg_s * (1.0 - sg_s)).astype(BF16)
        dgate_ref[:, 1024:] = (dmixin * aout_ref[...] * sg_a * (1.0 - sg_a)).astype(BF16)
        d_o = _mm_nt(daoutb, woa[...])
        do_ref[...] = d_o.astype(BF16)
        lane = lax.broadcasted_iota(jnp.int32, (o.shape[0], HEAD_PAD), 1)
        dlc = jnp.zeros((o.shape[0], HEAD_PAD), F32)
        for hh in range(MLA_HEADS):
            sl = slice(hh * HEAD_PAD, (hh + 1) * HEAD_PAD)
            dl = jnp.sum(d_o[:, sl] * o[:, sl], axis=1, keepdims=True)
            dlc = dlc + jnp.where(lane == hh, dl, 0.0)
        dl_ref[...] = dlc

    s = o.shape[0]
    m = memk.shape[0]
    return _row_call(
        "post_bwd", body, s, min(ROW_TILE, s), [dh2, o, s_out, gate, a_out, r1, r2, qx], [memk, memv, *lns],
        [w_oa, w_o, w_xq, w_xo],
        [(1024, BF16), (HEAD_PAD, F32), (1024, F32), (2048, BF16), (1024, F32),
         (1024, BF16), (1024, BF16), (1024, BF16), (1024, BF16)],
        [(m, 1024), (m, 1024), (1, 1024), (1, 1024), (1, 1024), (1, 1024)])


def _s5out_bwd(d_so, yl, z3, h, d_skip, pwb, w_c, w_glu):
    s = d_so.shape[0]
    ts = min(ROW_TILE, s)
    n_tiles = s // ts

    def body(dso_ref, yl_ref, z3_ref, h_ref, d_ref, pw_ref, wc, wglu,
             lam_ref, dup_ref, yg_ref, dy12_ref, dyl_ref, dd_ref, da_ref, gh_scr, carry, acc):
        @pl.when(pl.program_id(0) == 0)
        def _():
            dd_ref[...] = jnp.zeros_like(dd_ref)
            carry[...] = jnp.zeros_like(carry)
            acc[...] = jnp.zeros_like(acc)

        yl = yl_ref[...]
        ygb = _gelu(yl).astype(BF16)
        yg_ref[...] = ygb
        y12 = _mm(ygb, wglu[...])
        sg = _sigmoid(y12[:, 1024:])
        dso = dso_ref[...]
        dy12b = jnp.concatenate([dso * sg, dso * y12[:, :1024] * sg * (1.0 - sg)], axis=1).astype(BF16)
        dy12_ref[...] = dy12b
        dyl = _mm_nt(dy12b, wglu[...]) * _gelu_grad(yl)
        dylb = dyl.astype(BF16)
        dyl_ref[...] = dylb
        gh_scr[...] = _mm_nt(dylb, wc[...])
        _scan_bwd_rows(pw_ref, gh_scr, h_ref, lam_ref, carry, acc, ts // 8)
        dup_ref[...] = dyl * d_ref[...]
        dd_ref[...] += jnp.sum(dyl * z3_ref[:, :256], axis=0, keepdims=True)

        @pl.when(pl.program_id(0) == n_tiles - 1)
        def _():
            da_ref[...] = jnp.sum(acc[...], axis=0, keepdims=True)

    return _row_call("s5out_bwd", body, s, ts, [d_so, yl, z3, h], [d_skip, pwb], [w_c, w_glu],
                     [(2 * NS, F32), (256, F32), (256, BF16), (2048, BF16), (256, BF16)], [(1, 256), (1, 2 * NS)],
                     reverse=True, scratch=[pltpu.VMEM((ts, 2 * NS), F32), pltpu.VMEM((8, 2 * NS), F32),
                                            pltpu.VMEM((8, 2 * NS), F32)])


def _in_bwd(x, tab, z3, dq, dk, dv, lam, du_p, dgate, dh0p, ln_g, ln_b, qg, kvg, w_inx, w_b, w_q2, w_kv2):
    def body(x_ref, tab_ref, z3_ref, dq_ref, dk_ref, dv_ref, lam_ref, dup_ref, dgate_ref, dh0p_ref,
             lng_ref, lnb_ref, qg_ref, kvg_ref, winx, wb, wq2, wkv2,
             dx_ref, h0b_ref, dz_ref, ub_ref, cqn_ref, ckvn_ref, dq2_ref, dkv2_ref,
             dlng_ref, dlnb_ref, dqg_ref, dkvg_ref):
        @pl.when(pl.program_id(0) == 0)
        def _():
            for r in (dlng_ref, dlnb_ref, dqg_ref, dkvg_ref):
                r[...] = jnp.zeros_like(r)

        h0, xh0, rs0 = _ln_fwd(x_ref[...], lng_ref[...], lnb_ref[...])
        h0b_ref[...] = h0.astype(BF16)
        z3 = z3_ref[...]
        ub_ref[...] = z3[:, :256].astype(BF16)
        tab_v = tab_ref[...]
        cq1, cq2, ck1 = tab_v[:, :128], tab_v[:, 128:], tab_v[:, :128]
        du = _mm_nt(lam_ref[...], wb[...]) + dup_ref[...]
        dq = dq_ref[...].astype(F32)
        dq2b = jnp.concatenate([dq * _tile_heads(cq1), dq * _tile_heads(cq2)], axis=1).astype(BF16)
        dq2_ref[...] = dq2b
        cqg, cqn, rq = _rms_fwd(z3[:, 256:512], qg_ref[...])
        cqn_ref[...] = cqg.astype(BF16)
        dcq, dqg = _rms_bwd(_mm_nt(dq2b, wq2[...]), cqn, rq, qg_ref[...])
        dqg_ref[...] += dqg
        dkv2b = jnp.concatenate([dk_ref[...], dv_ref[...]], axis=1)
        dk = dk_ref[...].astype(F32)
        dkv2_ref[...] = dkv2b
        ckvg, ckvn, rkv = _rms_fwd(z3[:, 512:768], kvg_ref[...])
        ckvn_ref[...] = ckvg.astype(BF16)
        dckv, dkvg = _rms_bwd(_mm_nt(dkv2b, wkv2[...]), ckvn, rkv, kvg_ref[...])
        dkvg_ref[...] += dkvg
        dkrp = dk[:, :HEAD_PAD]
        for hh in range(1, MLA_HEADS):
            dkrp = dkrp + dk[:, hh * HEAD_PAD:(hh + 1) * HEAD_PAD]
        dzb = jnp.concatenate([a.astype(BF16) for a in (du, dcq, dckv, dkrp * ck1, dkrp * cq2)] + [dgate_ref[...]],
                              axis=1)
        dz_ref[...] = dzb
        dh0 = _mm_nt(dzb, winx[...]) + dh0p_ref[...]
        dx, dg, db = _ln_bwd(dh0, xh0, rs0, lng_ref[...])
        dx_ref[...] = dx
        dlng_ref[...] += dg
        dlnb_ref[...] += db

    s = x.shape[0]
    return _row_call(
        "in_bwd", body, s, min(ROW_TILE, s), [x, tab, z3, dq, dk, dv, lam, du_p, dgate, dh0p],
        [ln_g, ln_b, qg, kvg], [w_inx, w_b, w_q2, w_kv2],
        [(1024, F32), (1024, BF16), (3072, BF16), (256, BF16), (256, BF16), (256, BF16), (2048, BF16), (2048, BF16)],
        [(1, 1024), (1, 1024), (1, 256), (1, 256)])


def _wgrad(name, xs, dy, out_dtype=F32, blocks=1, after=None):
    s, k = xs.shape
    n = dy.shape[1]
    nb = n // blocks
    tk = min(k, 1024)
    tn = max(d for d in range(128, min(n, 2048) + 1, 128) if n % d == 0)
    ts = min(s, 2048)
    per_tile = tn // nb
    assert s % ts == 0 and k % tk == 0 and n % tn == 0 and (blocks == 1 or tn % nb == 0), (name, s, k, n)
    last = s // ts - 1

    def body(x_ref, dy_ref, *rest):
        o_ref, acc = rest[-2:]

        @pl.when(pl.program_id(2) == 0)
        def _():
            acc[...] = jnp.zeros_like(acc)

        acc[...] += _mm_tn(x_ref[...], dy_ref[...])

        @pl.when(pl.program_id(2) == last)
        def _():
            if blocks == 1:
                o_ref[...] = acc[...].astype(out_dtype)
            else:
                for d in range(per_tile):
                    o_ref[d] = acc[:, d * nb:(d + 1) * nb].astype(out_dtype)

    if blocks == 1:
        out_shape = jax.ShapeDtypeStruct((k, n), out_dtype)
        out_spec = pl.BlockSpec((tk, tn), lambda a, b, c: (a, b))
    else:
        out_shape = jax.ShapeDtypeStruct((blocks, k, nb), out_dtype)
        out_spec = pl.BlockSpec((per_tile, tk, nb), lambda a, b, c: (b, a, 0))
    return pl.pallas_call(
        body, name=name, grid=(k // tk, n // tn, s // ts),
        in_specs=[pl.BlockSpec((ts, tk), lambda a, b, c: (c, a)), pl.BlockSpec((ts, tn), lambda a, b, c: (c, b))]
        + ([] if after is None else [pl.BlockSpec(memory_space=pl.ANY)]),
        out_specs=out_spec, out_shape=out_shape, scratch_shapes=[pltpu.VMEM((tk, tn), F32)],
        compiler_params=pltpu.CompilerParams(dimension_semantics=("arbitrary", "arbitrary", "arbitrary"),
                                             vmem_limit_bytes=VMEM_LIMIT),
    )(xs, dy, *([] if after is None else [after]))


def _mesh_pos():
    x, y, c = lax.axis_index("x"), lax.axis_index("y"), lax.axis_index("c")
    return x, y, c


def _peer(x, y, c, k):
    px = 1 - x if k & 4 else x
    py = 1 - y if k & 2 else y
    pc = 1 - c if k & 1 else c
    return (px, py, pc), 4 * px + 2 * py + pc


def _all_to_all(name, arrays, gather):
    n = len(arrays)

    def body(*refs):
        src, dst = refs[:n], refs[n:2 * n]
        send_sems, recv_sems, local_sems = refs[2 * n:]
        x, y, c = _mesh_pos()
        me = 4 * x + 2 * y + c

        def block(i, d):
            return src[i] if gather else src[i].at[d]

        own = [pltpu.make_async_copy(block(i, me), dst[i].at[me], local_sems.at[i]) for i in range(n)]
        for cp in own:
            cp.start()
        sends = []
        for k in range(1, N_DEV):
            peer, pid = _peer(x, y, c, k)
            for i in range(n):
                idx = (k - 1) * n + i
                cp = pltpu.make_async_remote_copy(src_ref=block(i, pid), dst_ref=dst[i].at[me],
                                                  send_sem=send_sems.at[idx], recv_sem=recv_sems.at[idx],
                                                  device_id=peer, device_id_type=MESH)
                cp.start()
                sends.append(cp)
        for k in range(1, N_DEV):
            peer, pid = _peer(x, y, c, k)
            for i in range(n):
                idx = (k - 1) * n + i
                pltpu.make_async_remote_copy(src_ref=block(i, pid), dst_ref=dst[i].at[pid],
                                             send_sem=send_sems.at[idx], recv_sem=recv_sems.at[idx],
                                             device_id=peer, device_id_type=MESH).wait_recv()
        for cp in sends:
            cp.wait_send()
        for cp in own:
            cp.wait()

    n_sem = n * (N_DEV - 1)
    return pl.pallas_call(
        body, name=name,
        in_specs=[pl.BlockSpec(memory_space=pl.ANY)] * n,
        out_specs=[pl.BlockSpec(memory_space=pl.ANY)] * n,
        out_shape=[jax.ShapeDtypeStruct((N_DEV,) + a.shape[-2:], a.dtype) for a in arrays],
        scratch_shapes=[pltpu.SemaphoreType.DMA((n_sem,)), pltpu.SemaphoreType.DMA((n_sem,)),
                        pltpu.SemaphoreType.DMA((n,))],
    )(*arrays)


def _all_gather_two_level(name, arrays):
    n = len(arrays)

    def body(*refs):
        src, dst = refs[:n], refs[n:2 * n]
        send_sems, recv_sems, local_sems = refs[2 * n:]
        x, y, c = _mesh_pos()
        me = 4 * x + 2 * y + c
        sibling = (x, y, 1 - c)

        def copy(i, k, source, slot, to):
            return pltpu.make_async_remote_copy(src_ref=source, dst_ref=dst[i].at[slot],
                                                send_sem=send_sems.at[(k - 1) * n + i],
                                                recv_sem=recv_sems.at[(k - 1) * n + i],
                                                device_id=to, device_id_type=MESH)

        own = [pltpu.make_async_copy(src[i], dst[i].at[me], local_sems.at[i]) for i in range(n)]
        for cp in own:
            cp.start()
        sends = []
        for k in (1, 2, 4, 6):
            peer, _ = _peer(x, y, c, k)
            for i in range(n):
                sends.append(copy(i, k, src[i], me, peer))
                sends[-1].start()
        for k in (2, 4, 6):
            peer, pid = _peer(x, y, c, k)
            for i in range(n):
                copy(i, k, src[i], pid, peer).wait_recv()
                sends.append(copy(i, k + 1, dst[i].at[pid], pid, sibling))
                sends[-1].start()
        for k in (1, 3, 5, 7):
            peer, pid = _peer(x, y, c, k)
            for i in range(n):
                copy(i, k, src[i], pid, peer).wait_recv()
        for cp in sends:
            cp.wait_send()
        for cp in own:
            cp.wait()

    n_sem = n * (N_DEV - 1)
    return pl.pallas_call(
        body, name=name,
        in_specs=[pl.BlockSpec(memory_space=pl.ANY)] * n,
        out_specs=[pl.BlockSpec(memory_space=pl.ANY)] * n,
        out_shape=[jax.ShapeDtypeStruct((N_DEV,) + a.shape, a.dtype) for a in arrays],
        scratch_shapes=[pltpu.SemaphoreType.DMA((n_sem,)), pltpu.SemaphoreType.DMA((n_sem,)),
                        pltpu.SemaphoreType.DMA((n,))],
    )(*arrays)


_HBM = pl.BlockSpec(memory_space=pltpu.HBM)
_SEM = pl.BlockSpec(memory_space=pltpu.SEMAPHORE)


def _exchange_start(name, arrays, gather, after=None):
    n = len(arrays)
    n_sem = n * (N_DEV - 1)
    me = 4 * lax.axis_index("x") + 2 * lax.axis_index("y") + lax.axis_index("c")
    lands = []
    for a in arrays:
        own = a[None] if gather else lax.dynamic_slice_in_dim(a, me, 1, 0)
        lands.append(lax.dynamic_update_slice(lax.empty((N_DEV,) + a.shape[-2:], a.dtype), own, (me, 0, 0)))
    n_after = 0 if after is None else 1

    def body(*refs):
        src, land = refs[:n], refs[n:2 * n]
        send_sems, recv_sems = refs[2 * n + n_after], refs[2 * n + n_after + 1]
        token = refs[-1]
        x, y, c = _mesh_pos()
        me_in = 4 * x + 2 * y + c
        for k in range(1, N_DEV):
            peer, pid = _peer(x, y, c, k)
            for i in range(n):
                idx = (k - 1) * n + i
                pltpu.make_async_remote_copy(src_ref=src[i] if gather else src[i].at[pid], dst_ref=land[i].at[me_in],
                                             send_sem=send_sems.at[idx], recv_sem=recv_sems.at[idx],
                                             device_id=peer, device_id_type=MESH).start()
        token[...] = jnp.zeros_like(token)

    operands = [pltpu.with_memory_space_constraint(a, pltpu.HBM) for a in list(arrays) + lands]
    outs = pl.pallas_call(
        body, name=name,
        out_shape=(pltpu.SemaphoreType.DMA((n_sem,)), pltpu.SemaphoreType.DMA((n_sem,)),
                   *[pltpu.HBM(a.shape, a.dtype) for a in list(arrays) + lands],
                   jax.ShapeDtypeStruct((8, 128), F32)),
        in_specs=[_HBM] * (2 * n) + [pl.BlockSpec(memory_space=pl.ANY)] * n_after,
        out_specs=(_SEM, _SEM, *[_HBM] * (2 * n), pl.BlockSpec(memory_space=pltpu.VMEM)),
        input_output_aliases={i: 2 + i for i in range(2 * n)},
        compiler_params=pltpu.CompilerParams(has_side_effects=pltpu.SideEffectType.DATAFLOW_SIDE_EFFECTING),
    )(*operands, *([after] if n_after else []))
    return (gather, outs[0], outs[1], outs[2:2 + n], outs[2 + n:2 + 2 * n]), outs[-1]


def _exchange_wait(name, handle, after):
    gather, send_sems, recv_sems, srcs, lands = handle
    n = len(srcs)

    def body(*refs):
        src, land = refs[:n], refs[n:2 * n]
        s_sems, r_sems = refs[2 * n], refs[2 * n + 1]
        x, y, c = _mesh_pos()
        for k in range(1, N_DEV):
            peer, pid = _peer(x, y, c, k)
            for i in range(n):
                idx = (k - 1) * n + i
                cp = pltpu.make_async_remote_copy(src_ref=src[i] if gather else src[i].at[pid], dst_ref=land[i].at[pid],
                                                  send_sem=s_sems.at[idx], recv_sem=r_sems.at[idx],
                                                  device_id=peer, device_id_type=MESH)
                cp.wait_send()
                cp.wait_recv()

    outs = pl.pallas_call(
        body, name=name,
        out_shape=tuple(pltpu.HBM(a.shape, a.dtype) for a in list(srcs) + list(lands)),
        in_specs=[_HBM] * (2 * n) + [_SEM, _SEM, pl.BlockSpec(memory_space=pl.ANY)],
        out_specs=tuple([_HBM] * (2 * n)),
        input_output_aliases={i: i for i in range(2 * n)},
        compiler_params=pltpu.CompilerParams(has_side_effects=pltpu.SideEffectType.DATAFLOW_SIDE_EFFECTING),
    )(*srcs, *lands, send_sems, recv_sems, after)
    return list(outs[n:])


def _adamw(name, parts, w, m, v):
    a_rows, b_cols = w.shape
    ta = min(a_rows, ADAM_TILE)
    assert a_rows % ta == 0
    c1 = 1.0 - ADAM_B1 ** ADAM_STEP
    c2 = 1.0 - ADAM_B2 ** ADAM_STEP

    def body(p_ref, w_ref, m_ref, v_ref, g_ref, d_ref, mo_ref, vo_ref):
        g = p_ref[0].astype(F32)
        for d in range(1, N_DEV):
            g = g + p_ref[d].astype(F32)
        g_ref[...] = g
        mn = ADAM_B1 * m_ref[...] + (1.0 - ADAM_B1) * g
        vn = ADAM_B2 * v_ref[...] + (1.0 - ADAM_B2) * (g * g)
        mo_ref[...] = mn
        vo_ref[...] = vn
        d_ref[...] = -ADAM_LR * ((mn / c1) / (jnp.sqrt(vn / c2) + ADAM_EPS) + ADAM_WD * w_ref[...])

    row = pl.BlockSpec((ta, b_cols), lambda i: (i, 0))
    return pl.pallas_call(
        body, name=name, grid=(a_rows // ta,),
        in_specs=[pl.BlockSpec((N_DEV, ta, b_cols), lambda i: (0, i, 0)), row, row, row],
        out_specs=[row] * 4,
        out_shape=[jax.ShapeDtypeStruct((a_rows, b_cols), F32)] * 4,
        compiler_params=pltpu.CompilerParams(dimension_semantics=("arbitrary",), vmem_limit_bytes=VMEM_LIMIT),
    )(parts, w, m, v)


def _pack_rows(arrays, rows):
    flat = jnp.concatenate([a.reshape(-1) for a in arrays])
    return jnp.pad(flat, (0, rows * LANES - flat.shape[0])).reshape(rows, LANES)


def _cols_from_blocks(g):
    return g.transpose(1, 0, 2).reshape(g.shape[1], N_DEV * g.shape[2])


def _blocks_from_cols(w):
    return w.reshape(w.shape[0], N_DEV, w.shape[1] // N_DEV).transpose(1, 0, 2)


def _unpack_flat(flat2d, shapes):
    flat = flat2d.reshape(-1)
    out, off = [], 0
    for shp in shapes:
        sz = math.prod(shp)
        out.append(flat[off:off + sz].reshape(shp))
        off += sz
    return out


def _s5_mats(lam_re, lam_im, log_dt, b_re, b_im, c_re, c_im):
    lr = jnp.minimum(lam_re, S5_MAX_RE)
    li = lam_im
    dt = jnp.exp(log_dt)[:, None]
    mag = jnp.exp(lr * dt)
    ang = li * dt
    ab_re = mag * jnp.cos(ang)
    ab_im = mag * jnp.sin(ang)
    den = lr * lr + li * li
    nr = ab_re - 1.0
    f_re = ((nr * lr + ab_im * li) / den)[..., None]
    f_im = ((ab_im * lr - nr * li) / den)[..., None]
    bb_re = f_re * b_re - f_im * b_im
    bb_im = f_re * b_im + f_im * b_re
    eye = jnp.eye(S5_GROUPS, dtype=F32)
    a = jnp.stack([ab_re.reshape(-1), ab_im.reshape(-1)])
    wb = jnp.concatenate([jnp.einsum("gph,gk->ghkp", bb_re, eye).reshape(S5_WIDTH, NS),
                          jnp.einsum("gph,gk->ghkp", bb_im, eye).reshape(S5_WIDTH, NS)], axis=1)
    wc = jnp.concatenate([jnp.einsum("ghp,gk->gpkh", c_re, eye).reshape(NS, S5_WIDTH),
                          -jnp.einsum("ghp,gk->gpkh", c_im, eye).reshape(NS, S5_WIDTH)], axis=0)
    return a, wb, wc


def _power_table(a):
    ar, ai = a[0], a[1]
    rows_r, rows_i = [ar], [ai]
    for _ in range(7):
        pr, pi = rows_r[-1], rows_i[-1]
        rows_r.append(pr * ar - pi * ai)
        rows_i.append(pr * ai + pi * ar)
    return jnp.concatenate([jnp.stack(rows_r), jnp.stack(rows_i)], axis=1)


def _rope_table(positions):
    inv = ROPE_THETA ** (-jnp.arange(0, MLA_ROPE, 2, dtype=F32) / MLA_ROPE)
    inv128 = jnp.concatenate([jnp.zeros((MLA_NOPE,), F32), inv, inv, jnp.zeros((32,), F32)])
    sign = jnp.concatenate([jnp.zeros((MLA_NOPE,), F32), -jnp.ones((16,), F32), jnp.ones((16,), F32),
                            jnp.zeros((32,), F32)])
    ang = positions.astype(F32)[:, None] * inv128
    return jnp.concatenate([jnp.cos(ang), jnp.sin(ang) * sign], axis=1)


def _derived_weights(full):
    w_in = full["w_in"]
    k1, k2 = w_in[:, 768:784], w_in[:, 784:800]
    z64, z32 = jnp.zeros((1024, 64), BF16), jnp.zeros((1024, 32), BF16)
    w_inx = jnp.concatenate([w_in[:, :768], z64, k1, k2, z32, z64, k2, k1, z32, w_in[:, 800:]], axis=1)
    uq = full["w_uq"].reshape(256, MLA_HEADS, MLA_QK)
    nope, r1, r2 = uq[:, :, :64], uq[:, :, 64:80], uq[:, :, 80:]
    zq64, zq32 = jnp.zeros((256, MLA_HEADS, 64), BF16), jnp.zeros((256, MLA_HEADS, 32), BF16)
    w_q2 = jnp.concatenate([jnp.concatenate([nope, r1, r2, zq32], axis=2).reshape(256, 1024),
                            jnp.concatenate([zq64, r2, r1, zq32], axis=2).reshape(256, 1024)], axis=1)
    ukv = full["w_ukv"].reshape(256, MLA_HEADS, 128)
    w_kv2 = jnp.concatenate([jnp.concatenate([ukv[:, :, :64], zq64], axis=2).reshape(256, 1024),
                             jnp.concatenate([ukv[:, :, 64:], zq64], axis=2).reshape(256, 1024)], axis=1)
    return w_inx, w_q2, w_kv2


def _oa_padded(w_oa):
    oa = w_oa.reshape(MLA_HEADS, MLA_V, 1024)
    return jnp.concatenate([oa, jnp.zeros_like(oa)], axis=1).reshape(1024, 1024)


def _fold_w_in(d_inx):
    d_k1 = d_inx[:, 832:848] + d_inx[:, 976:992]
    d_k2 = d_inx[:, 848:864] + d_inx[:, 960:976]
    return jnp.concatenate([d_inx[:, :768], d_k1, d_k2, d_inx[:, 1024:]], axis=1)


def _fold_qkv(d_q2, d_kv2):
    a = d_q2[:, :1024].reshape(256, MLA_HEADS, 128)
    b = d_q2[:, 1024:].reshape(256, MLA_HEADS, 128)
    d_w_uq = jnp.concatenate([a[:, :, :64], a[:, :, 64:80] + b[:, :, 80:96], a[:, :, 80:96] + b[:, :, 64:80]],
                             axis=2).reshape(256, MLA_HEADS * MLA_QK)
    kk = d_kv2[:, :1024].reshape(256, MLA_HEADS, 128)
    vv = d_kv2[:, 1024:].reshape(256, MLA_HEADS, 128)
    d_w_ukv = jnp.concatenate([kk[:, :, :64], vv[:, :, :64]], axis=2).reshape(256, 1024)
    return d_w_uq, d_w_ukv


def kernel(x, mem, positions, ln_in_g, ln_in_b, w_in, s5_lam_re, s5_lam_im, s5_log_dt, s5_b_re, s5_b_im, s5_c_re, s5_c_im, s5_d, w_glu, q_norm_g, w_uq, kv_norm_g, w_ukv, w_oa, w_o, ln1_g, ln1_b, w_xq, w_xk, w_xv, w_xo, ln2_g, ln2_b, w_up, w_down, ln3_g, ln3_b, loss_target, m_ln_in_g, m_ln_in_b, m_w_in, m_s5_lam_re, m_s5_lam_im, m_s5_log_dt, m_s5_b_re, m_s5_b_im, m_s5_c_re, m_s5_c_im, m_s5_d, m_w_glu, m_q_norm_g, m_w_uq, m_kv_norm_g, m_w_ukv, m_w_oa, m_w_o, m_ln1_g, m_ln1_b, m_w_xq, m_w_xk, m_w_xv, m_w_xo, m_ln2_g, m_ln2_b, m_w_up, m_w_down, m_ln3_g, m_ln3_b, v_ln_in_g, v_ln_in_b, v_w_in, v_s5_lam_re, v_s5_lam_im, v_s5_log_dt, v_s5_b_re, v_s5_b_im, v_s5_c_re, v_s5_c_im, v_s5_d, v_w_glu, v_q_norm_g, v_w_uq, v_kv_norm_g, v_w_ukv, v_w_oa, v_w_o, v_ln1_g, v_ln1_b, v_w_xq, v_w_xk, v_w_xv, v_w_xo, v_ln2_g, v_ln2_b, v_w_up, v_w_down, v_ln3_g, v_ln3_b):
    args = dict(locals())
    wts = {n: args[n] for n in WEIGHTS}
    mom = {n: args["m_" + n] for n in WEIGHTS}
    vel = {n: args["v_" + n] for n in WEIGHTS}
    xs, mems, tgt = x[0], mem[0], loss_target[0]
    s = xs.shape[0]
    names_big = [n for (n, _, _, _) in SHARDED]
    kind = {n: kd for (n, kd, _, _) in SHARDED}
    n_small = sum(math.prod(wts[n].shape) for n in SMALL)
    small_rows = _round_up(n_small + 1, 8 * LANES) // LANES

    late = [n for n in names_big if n not in EARLY]
    shard = {n: wts[n][0].astype(BF16) for n in names_big}
    g_early = dict(zip(EARLY, _all_gather_two_level("all_gather", [shard[n] for n in EARLY])))
    ag_handle, ag_token = _exchange_start("all_gather_late", [shard[n] for n in late], True, after=g_early["w_in"])
    full = {n: _cols_from_blocks(g_early[n]) for n in EARLY}
    w_inx, w_q2, w_kv2 = _derived_weights(full)
    s5_args = (s5_lam_re[0], s5_lam_im[0], s5_log_dt[0], s5_b_re[0], s5_b_im[0], s5_c_re[0], s5_c_im[0])
    (a_mat, w_b, w_c), s5_vjp = jax.vjp(_s5_mats, *s5_args)
    w_bb, w_cb = w_b.astype(BF16), w_c.astype(BF16)
    pw = _power_table(a_mat)
    pwb = _power_table(a_mat * jnp.array([[1.0], [-1.0]], F32))[::-1]
    tab = _rope_table(positions[0])
    row = lambda a: a.reshape(1, -1)
    ln_g, ln_b = row(ln_in_g) + ag_token[0:1, 0:1], row(ln_in_b)
    lns = [ln1_g, ln1_b, ln2_g, ln2_b]

    h0, z3, gate, h, q, k, v, kt, vt = _in_fwd(xs, tab, ln_g, ln_b, q_norm_g, kv_norm_g, pw, w_inx, w_bb, w_q2, w_kv2)
    yl, s_out = _s5out_fwd(h, z3, s5_d, w_cb, full["w_glu"])
    o, lse_t = _attn_fwd(q, k, vt)
    g_late = dict(zip(late, _exchange_wait("all_gather_late_wait", ag_handle, after=lse_t)))
    full.update({n: g_late[n].reshape(-1, g_late[n].shape[2]) if kind[n] == "row" else _cols_from_blocks(g_late[n])
                 for n in late})
    w_oap = _oa_padded(full["w_oa"])
    memk, memv = _mem_kv(mems, full["w_xk"], full["w_xv"])
    r2, mixin_b, h1_b, ox_b, a_out, r1, qx_b = _post_fwd(o, s_out, gate, h0, memk, memv, lns,
                                        w_oap, full["w_o"], full["w_xq"], full["w_xo"])
    dh2, h2_b, dup_b, act_b, dff_b, loss_acc, d_ln3_g, d_ln3_b = _mlp(
        r2, tgt, ln2_g, ln2_b, ln3_g, ln3_b, full["w_up"], full["w_down"])
    (do_b, delta, d_so, dgate, dh0p, daout_b, dmix_b, dqx_b, dxa_b,
     dmk, dmv, d_ln1_g, d_ln1_b, d_ln2_g, d_ln2_b) = _post_bwd(
        dh2, o, s_out, gate, a_out, r1, r2, qx_b, memk, memv, lns, w_oap, full["w_o"], full["w_xq"], full["w_xo"])
    rows8 = lambda g: g.reshape(N_DEV, g.shape[0] // N_DEV, g.shape[1])
    d_w_oa = _wgrad("wg_oa", o, daout_b).reshape(MLA_HEADS, 128, 1024)[:, :64].reshape(512, 1024)
    send = {
        "w_oa": _blocks_from_cols(d_w_oa).astype(BF16),
        "w_up": _wgrad("wg_up", h2_b, dup_b, BF16, N_DEV),
        "w_o": rows8(_wgrad("wg_o", mixin_b, dmix_b, BF16)),
        "w_xq": rows8(_wgrad("wg_xq", h1_b, dqx_b, BF16)),
        "w_xk": rows8(_wgrad("wg_xk", mems, dmk, BF16)),
        "w_xv": rows8(_wgrad("wg_xv", mems, dmv, BF16)),
        "w_xo": rows8(_wgrad("wg_xo", ox_b, dxa_b, BF16)),
        "w_down": rows8(_wgrad("wg_down", act_b, dff_b, BF16)),
    }
    rs_handle, rs_token = _exchange_start("grad_exchange_late", [send[n] for n in late], False)
    dl_t = delta.T[:MLA_HEADS].reshape(MLA_HEADS, 1, s) + rs_token[0, 0]
    dq, dk, dv = _attn_bwd(q, k, v, kt, do_b, lse_t, dl_t)
    lam, du_p, yg_b, dy12_b, dyl_b, d_s5_d, d_a = _s5out_bwd(d_so, yl, z3, h, s5_d, pwb, w_cb, full["w_glu"])
    (dx, h0_b, dz_b, u_b, cqn_b, ckvn_b, dq2_b, dkv2_b, d_ln_g, d_ln_b, d_qg, d_kvg) = _in_bwd(
        xs, tab, z3, dq, dk, dv, lam, du_p, dgate, dh0p, ln_g, ln_b, q_norm_g, kv_norm_g, w_inx, w_bb, w_q2, w_kv2)

    d_w_in = _fold_w_in(_wgrad("wg_in", h0_b, dz_b))
    win_handle, win_token = _exchange_start("grad_exchange_w_in", [_blocks_from_cols(d_w_in).astype(BF16)], False)
    d_w_uq, d_w_ukv = _fold_qkv(_wgrad("wg_q", cqn_b, dq2_b, after=win_token), _wgrad("wg_kv", ckvn_b, dkv2_b))
    send.update({
        "w_uq": _blocks_from_cols(d_w_uq).astype(BF16), "w_ukv": _blocks_from_cols(d_w_ukv).astype(BF16),
        "w_glu": _wgrad("wg_glu", yg_b, dy12_b, BF16, N_DEV, after=win_token),
    })
    d_s5 = s5_vjp((d_a.reshape(2, NS), _wgrad("wg_s5b", u_b, lam, after=win_token),
                   _wgrad("wg_s5c", h, dyl_b, after=win_token)))
    small_grads = {
        "ln_in_g": d_ln_g, "ln_in_b": d_ln_b, "s5_lam_re": d_s5[0], "s5_lam_im": d_s5[1], "s5_log_dt": d_s5[2],
        "s5_b_re": d_s5[3], "s5_b_im": d_s5[4], "s5_c_re": d_s5[5], "s5_c_im": d_s5[6], "s5_d": d_s5_d,
        "q_norm_g": d_qg, "kv_norm_g": d_kvg, "ln1_g": d_ln1_g, "ln1_b": d_ln1_b, "ln2_g": d_ln2_g,
        "ln2_b": d_ln2_b, "ln3_g": d_ln3_g, "ln3_b": d_ln3_b,
    }
    small_send = jnp.broadcast_to(_pack_rows([small_grads[n] for n in SMALL] + [loss_acc[0, :1]], small_rows)[None],
                                  (N_DEV, small_rows, LANES))

    early_rest = [n for n in EARLY if n != "w_in"]
    recv_early = _all_to_all("grad_exchange", [send[n] for n in early_rest] + [small_send], False)
    recv_late = _exchange_wait("grad_exchange_late_wait", rs_handle, after=recv_early[-1])
    recv = dict(zip(early_rest + late, list(recv_early[:-1]) + recv_late))
    results = [dict(), dict(), dict(), dict()]
    for n in late + early_rest + ["w_in"]:
        if n == "w_in":
            recv[n] = _exchange_wait("grad_exchange_w_in_wait", win_handle, after=results[0][early_rest[-1]])[0]
        parts = recv[n]
        outs = _adamw("adamw_" + n, parts, wts[n][0], mom[n][0], vel[n][0])
        for res, a in zip(results, outs):
            res[n] = a[None]
    small_out = _adamw("adamw_small", recv_early[-1],
                       *[_pack_rows([t[n] for n in SMALL], small_rows) for t in (wts, mom, vel)])
    for res, fs in zip(results, small_out):
        for n, a in zip(SMALL, _unpack_flat(fs, [wts[n].shape for n in SMALL])):
            res[n] = a

    loss = small_out[0].reshape(-1)[n_small]
    return (loss, dx[None], *[res[n] for res in results for n in WEIGHTS])
```

```python
import math

import jax
import jax.numpy as jnp
from jax import lax
from jax.experimental import pallas as pl
from jax.experimental.pallas import tpu as pltpu

F32 = jnp.float32
BF16 = jnp.bfloat16

D_MODEL = 1024
S5_WIDTH = 256
S5_GROUP_CH = 16
S5_GROUPS = 16
S5_STATE = 64
NS = S5_GROUPS * S5_STATE
S5_MAX_RE = -1e-4
MLA_HEADS = 8
MLA_NOPE = 64
MLA_ROPE = 32
MLA_QK = 96
MLA_V = 64
HEAD_PAD = 128
ROPE_THETA = 10000.0
XATTN_HEADS = 4
XATTN_HD = 256
MLP_HIDDEN = 4096
LN_EPS = 1e-5
RMS_EPS = 1e-6
NEG_INF = -1e30
LOG2E = 1.4426950408889634
DN_ALPHA = 2.0 ** 0.25
ADAM_LR = 0.001
ADAM_B1 = 0.9
ADAM_B2 = 0.999
ADAM_EPS = 1e-08
ADAM_WD = 0.01
ADAM_STEP = 10

N_DEV = 8
MESH = pl.DeviceIdType.MESH
LANES = 1024
VMEM_LIMIT = 60 * 1024 * 1024

ROW_TILE = 256
ATT_TILE = 1024
ADAM_TILE = 128

SHARDED = (
    ("w_in", "col", 1024, 2848), ("w_glu", "col", 256, 2048), ("w_uq", "col", 256, 768),
    ("w_ukv", "col", 256, 1024), ("w_oa", "col", 512, 1024), ("w_o", "row", 1024, 1024),
    ("w_xq", "row", 1024, 1024), ("w_xk", "row", 1024, 1024), ("w_xv", "row", 1024, 1024),
    ("w_xo", "row", 1024, 1024), ("w_up", "col", 1024, 4096), ("w_down", "row", 4096, 1024),
)
EARLY = ("w_in", "w_glu", "w_uq", "w_ukv")
SMALL = ("ln_in_g", "ln_in_b", "s5_lam_re", "s5_lam_im", "s5_log_dt", "s5_b_re", "s5_b_im", "s5_c_re",
         "s5_c_im", "s5_d", "q_norm_g", "kv_norm_g", "ln1_g", "ln1_b", "ln2_g", "ln2_b", "ln3_g", "ln3_b")
WEIGHTS = ("ln_in_g", "ln_in_b", "w_in", "s5_lam_re", "s5_lam_im", "s5_log_dt", "s5_b_re", "s5_b_im",
           "s5_c_re", "s5_c_im", "s5_d", "w_glu", "q_norm_g", "w_uq", "kv_norm_g", "w_ukv", "w_oa", "w_o",
           "ln1_g", "ln1_b", "w_xq", "w_xk", "w_xv", "w_xo", "ln2_g", "ln2_b", "w_up", "w_down", "ln3_g", "ln3_b")


def _round_up(n, m):
    return (n + m - 1) // m * m


def _bf(a):
    return a.astype(BF16)


def _mm(a, b):
    return jnp.dot(_bf(a), _bf(b), preferred_element_type=F32)


def _mm_nt(a, b):
    return lax.dot_general(_bf(a), _bf(b), (((1,), (1,)), ((), ())), preferred_element_type=F32)


def _mm_tn(a, b):
    return lax.dot_general(_bf(a), _bf(b), (((0,), (0,)), ((), ())), preferred_element_type=F32)


def _sigmoid(a):
    return 1.0 / (1.0 + jnp.exp(-a))


def _gelu(a):
    return 0.5 * a * (1.0 + lax.erf(a * (2.0 ** -0.5)))


def _gelu_grad(a):
    return 0.5 * (1.0 + lax.erf(a * (2.0 ** -0.5))) + a * jnp.exp(-0.5 * a * a) * (1.0 / math.sqrt(2.0 * math.pi))


def _ln_fwd(a, g, b):
    mu = jnp.mean(a, axis=-1, keepdims=True)
    ac = a - mu
    var = jnp.mean(ac * ac, axis=-1, keepdims=True)
    rstd = lax.rsqrt(var + LN_EPS)
    xhat = ac * rstd
    return xhat * g + b, xhat, rstd


def _ln_bwd(dy, xhat, rstd, g):
    dxh = dy * g
    m1 = jnp.mean(dxh, axis=-1, keepdims=True)
    m2 = jnp.mean(dxh * xhat, axis=-1, keepdims=True)
    dx = rstd * (dxh - m1 - xhat * m2)
    return dx, jnp.sum(dy * xhat, axis=0, keepdims=True), jnp.sum(dy, axis=0, keepdims=True)


def _rms_fwd(a, g):
    r = lax.rsqrt(jnp.mean(a * a, axis=-1, keepdims=True) + RMS_EPS)
    xn = a * r
    return xn * g, xn, r


def _rms_bwd(dy, xn, r, g):
    dxn = dy * g
    dx = r * (dxn - xn * jnp.mean(dxn * xn, axis=-1, keepdims=True))
    return dx, jnp.sum(dy * xn, axis=0, keepdims=True)


def _tile_heads(a):
    return jnp.concatenate([a] * MLA_HEADS, axis=1)


def _row_call(name, body, n_rows, ts, tiled_in, full_in, weights, tiled_out, acc_out, reverse=False, scratch=(),
              tiled_out_t=()):
    n = n_rows // ts
    assert n * ts == n_rows, (name, n_rows, ts)
    nt, nf, nw = len(tiled_in), len(full_in), len(weights)
    nto, nao = len(tiled_out) + len(tiled_out_t), len(acc_out)
    if reverse:
        imap = lambda i: (n - 1 - i, 0)
    else:
        imap = lambda i: (i, 0)
    const = lambda i: (0, 0)

    def kern(*refs):
        ins = refs[:nt + nf]
        w_hbm = refs[nt + nf:nt + nf + nw]
        outs = refs[nt + nf + nw:nt + nf + nw + nto + nao]
        scr = refs[nt + nf + nw + nto + nao:]
        w_vmem = scr[:nw]
        extra = scr[nw + 1:] if nw else scr
        if nw:
            sem = scr[nw]

            @pl.when(pl.program_id(0) == 0)
            def _():
                cps = [pltpu.make_async_copy(w_hbm[k], w_vmem[k], sem.at[k]) for k in range(nw)]
                for cp in cps:
                    cp.start()
                for cp in cps:
                    cp.wait()
        body(*ins, *w_vmem, *outs, *extra)

    in_specs = [pl.BlockSpec((ts, a.shape[1]), imap) for a in tiled_in]
    in_specs += [pl.BlockSpec(a.shape, const) for a in full_in]
    in_specs += [pl.BlockSpec(memory_space=pl.ANY) for _ in weights]
    assert not (reverse and tiled_out_t)
    out_shape = [jax.ShapeDtypeStruct((n_rows, c), dt) for (c, dt) in tiled_out]
    out_shape += [jax.ShapeDtypeStruct((c, n_rows), dt) for (c, dt) in tiled_out_t]
    out_shape += [jax.ShapeDtypeStruct(shp, F32) for shp in acc_out]
    out_specs = [pl.BlockSpec((ts, c), imap) for (c, dt) in tiled_out]
    out_specs += [pl.BlockSpec((c, ts), lambda i: (0, i)) for (c, dt) in tiled_out_t]
    out_specs += [pl.BlockSpec(shp, const) for shp in acc_out]
    scratch_shapes = [pltpu.VMEM(w.shape, w.dtype) for w in weights]
    if nw:
        scratch_shapes.append(pltpu.SemaphoreType.DMA((nw,)))
    scratch_shapes += list(scratch)
    return pl.pallas_call(
        kern, name=name, grid=(n,), in_specs=in_specs, out_specs=out_specs, out_shape=out_shape,
        scratch_shapes=scratch_shapes,
        compiler_params=pltpu.CompilerParams(dimension_semantics=("arbitrary",), vmem_limit_bytes=VMEM_LIMIT),
    )(*tiled_in, *full_in, *weights)


def _mem_kv(mem, w_xk, w_xv):
    m = mem.shape[0]

    def body(mem_ref, wk_ref, wv_ref, k_ref, v_ref):
        mb = mem_ref[...]
        k_ref[...] = _mm(mb, wk_ref[...]).astype(BF16)
        v_ref[...] = _mm(mb, wv_ref[...]).astype(BF16)

    return pl.pallas_call(
        body, name="mem_kv",
        out_shape=[jax.ShapeDtypeStruct((m, D_MODEL), BF16)] * 2,
        compiler_params=pltpu.CompilerParams(vmem_limit_bytes=VMEM_LIMIT),
    )(mem, w_xk, w_xv)


def _in_fwd(x, tab, ln_g, ln_b, qg, kvg, pw, w_inx, w_b, w_q2, w_kv2):
    ts = min(ROW_TILE, x.shape[0])

    def body(x_ref, tab_ref, lng_ref, lnb_ref, qg_ref, kvg_ref, pw_ref, winx, wb, wq2, wkv2,
             h0_ref, z3_ref, gate_ref, h_ref, q_ref, k_ref, v_ref, kt_ref, vt_ref, bu_scr, carry):
        @pl.when(pl.program_id(0) == 0)
        def _():
            carry[...] = jnp.zeros_like(carry)

        h0, _, _ = _ln_fwd(x_ref[...], lng_ref[...], lnb_ref[...])
        h0_ref[...] = h0
        z = _mm(h0, winx[...])
        z3_ref[...] = z[:, :768]
        gate_ref[...] = z[:, 1024:].astype(BF16)
        bu_scr[...] = _mm(z[:, :256], wb[...])
        _scan_fwd_rows(pw_ref, bu_scr, h_ref, carry, ts // 8)
        tab_v = tab_ref[...]
        cq1, cq2, ck1 = tab_v[:, :128], tab_v[:, 128:], tab_v[:, :128]
        cqn, _, _ = _rms_fwd(z[:, 256:512], qg_ref[...])
        q2 = _mm(cqn, wq2[...])
        q_ref[...] = (q2[:, :1024] * _tile_heads(cq1) + q2[:, 1024:] * _tile_heads(cq2)).astype(BF16)
        ckvn, _, _ = _rms_fwd(z[:, 512:768], kvg_ref[...])
        kv2 = _mm(ckvn, wkv2[...])
        krp = z[:, 768:896] * ck1 + z[:, 896:1024] * cq2
        kf = kv2[:, :1024] + _tile_heads(krp)
        k_ref[...] = kf.astype(BF16)
        v_ref[...] = kv2[:, 1024:].astype(BF16)
        kt_ref[...] = kf.T.astype(BF16)
        vt_ref[...] = kv2[:, 1024:].T.astype(BF16)

    s = x.shape[0]
    return _row_call(
        "in_fwd", body, s, ts, [x, tab], [ln_g, ln_b, qg, kvg, pw], [w_inx, w_b, w_q2, w_kv2],
        [(1024, F32), (768, F32), (2048, BF16), (2048, F32), (1024, BF16), (1024, BF16), (1024, BF16)], [],
        tiled_out_t=[(1024, BF16), (1024, BF16)], scratch=[pltpu.VMEM((ts, 2 * NS), F32), pltpu.VMEM((8, 2 * NS), F32)])


def _scan_coeffs(pw_ref, rows, forward):
    row = lax.broadcasted_iota(jnp.int32, (8, NS), 0)
    out = []
    for k, r in zip((1, 2, 4), rows):
        keep = (row >= k) if forward else (row < 8 - k)
        out.append((jnp.where(keep, pw_ref[r:r + 1, :NS], 0.0), jnp.where(keep, pw_ref[r:r + 1, NS:], 0.0), k))
    return out


def _scan_level(xr, xi, coeff, forward):
    ar, ai, k = coeff
    shift = k if forward else 8 - k
    sr, si = pltpu.roll(xr, shift, 0), pltpu.roll(xi, shift, 0)
    return xr + ar * sr - ai * si, xi + ar * si + ai * sr


def _scan_fwd_rows(pw_ref, bu_ref, h_ref, carry, nblk):
    coeffs = _scan_coeffs(pw_ref, (0, 1, 3), True)

    def blk(b, _):
        r0 = pl.multiple_of(b * 8, 8)
        xr = bu_ref[pl.ds(r0, 8), :NS]
        xi = bu_ref[pl.ds(r0, 8), NS:]
        for coeff in coeffs:
            xr, xi = _scan_level(xr, xi, coeff, True)
        cr, ci = carry[7:8, :NS], carry[7:8, NS:]
        pr, pi = pw_ref[:, :NS], pw_ref[:, NS:]
        hr = xr + pr * cr - pi * ci
        hi = xi + pr * ci + pi * cr
        h_ref[pl.ds(r0, 8), :NS] = hr
        h_ref[pl.ds(r0, 8), NS:] = hi
        carry[:, :NS] = hr
        carry[:, NS:] = hi
        return 0

    lax.fori_loop(0, nblk, blk, 0)


def _scan_bwd_rows(pw_ref, g_ref, h_ref, lam_ref, carry, acc, nblk):
    row = lax.broadcasted_iota(jnp.int32, (8, NS), 0)
    coeffs = _scan_coeffs(pw_ref, (7, 6, 4), False)

    def blk(bb, _):
        r0 = pl.multiple_of((nblk - 1 - bb) * 8, 8)
        xr = g_ref[pl.ds(r0, 8), :NS]
        xi = g_ref[pl.ds(r0, 8), NS:]
        for coeff in coeffs:
            xr, xi = _scan_level(xr, xi, coeff, False)
        cr, ci = carry[0:1, :NS], carry[0:1, NS:]
        pr, pi = pw_ref[:, :NS], pw_ref[:, NS:]
        lr = xr + pr * cr - pi * ci
        li = xi + pr * ci + pi * cr
        lam_ref[pl.ds(r0, 8), :NS] = lr
        lam_ref[pl.ds(r0, 8), NS:] = li
        nr = jnp.where(row < 7, pltpu.roll(lr, 7, 0), cr)
        ni = jnp.where(row < 7, pltpu.roll(li, 7, 0), ci)
        hr = h_ref[pl.ds(r0, 8), :NS]
        hi = h_ref[pl.ds(r0, 8), NS:]
        acc[:, :NS] += nr * hr + ni * hi
        acc[:, NS:] += ni * hr - nr * hi
        carry[:, :NS] = lr
        carry[:, NS:] = li
        return 0

    lax.fori_loop(0, nblk, blk, 0)


def _s5out_fwd(h, z3, d_skip, w_c, w_glu):
    def body(h_ref, z3_ref, d_ref, wc, wglu, yl_ref, so_ref):
        yl = _mm(h_ref[...], wc[...]) + d_ref[...] * z3_ref[:, :256]
        yl_ref[...] = yl
        y12 = _mm(_gelu(yl), wglu[...])
        so_ref[...] = y12[:, :1024] * _sigmoid(y12[:, 1024:])

    s = h.shape[0]
    return _row_call("s5out_fwd", body, s, min(ROW_TILE, s), [h, z3], [d_skip], [w_c, w_glu],
                     [(256, F32), (1024, F32)], [])


def _causal_mask_t(t):
    row = lax.broadcasted_iota(jnp.int32, (t, t), 0)
    col = lax.broadcasted_iota(jnp.int32, (t, t), 1)
    return row <= col


def _attn_fwd(q, k, vt):
    s = q.shape[0]
    t = min(ATT_TILE, s)
    nq = s // t
    scale = MLA_QK ** -0.5
    c2 = scale * LOG2E

    def body(q_ref, k_ref, vt_ref, o_ref, lse_ref, s0, s1):
        i = pl.program_id(1)
        qb = q_ref[...]

        def scores(kb, dst):
            st = _mm_nt(k_ref[pl.ds(pl.multiple_of(kb * t, t), t), :], qb)
            dst[...] = st
            return jnp.max(st, axis=0, keepdims=True)

        def update(kb, src, mt, state, masked):
            m, l, acc = state
            if masked:
                keep = _causal_mask_t(t)
                mt = jnp.max(jnp.where(keep, src[...], NEG_INF), axis=0, keepdims=True)
            m_new = jnp.maximum(m, mt)
            if masked:
                p = jnp.exp2((jnp.where(keep, src[...], NEG_INF) - m_new) * c2)
            else:
                p = jnp.exp2((src[...] - m_new) * c2)
            a = jnp.exp2((m - m_new) * c2)
            l = a * l + jnp.sum(p, axis=0, keepdims=True)
            acc = a * acc + _mm(vt_ref[:, pl.ds(pl.multiple_of(kb * t, t), t)], p)
            return m_new, l, acc

        def pair(jj, carry):
            mt_a, state = carry
            mt_b = scores(2 * jj + 1, s1)
            state = update(2 * jj, s0, mt_a, state, False)
            mt_a = scores(2 * jj + 2, s0)
            return mt_a, update(2 * jj + 1, s1, mt_b, state, False)

        def odd_tail(carry):
            mt_a, state = carry
            mt_b = scores(i, s1)
            state = update(i - 1, s0, mt_a, state, False)
            return update(i, s1, mt_b, state, True)

        def even_tail(carry):
            mt_a, state = carry
            return update(i, s0, mt_a, state, True)

        init = (jnp.full((1, t), NEG_INF, F32), jnp.zeros((1, t), F32), jnp.zeros((HEAD_PAD, t), F32))
        carry = lax.fori_loop(0, i // 2, pair, (scores(0, s0), init))
        m, l, acc = lax.cond(i % 2 == 1, odd_tail, even_tail, carry)
        o_ref[...] = (acc / l).T
        lse_ref[...] = m * scale + jnp.log(l)

    return pl.pallas_call(
        body, name="attn_fwd", grid=(MLA_HEADS, nq),
        in_specs=[pl.BlockSpec((t, HEAD_PAD), lambda h, i: (i, h)),
                  pl.BlockSpec((s, HEAD_PAD), lambda h, i: (0, h)),
                  pl.BlockSpec((HEAD_PAD, s), lambda h, i: (h, 0))],
        out_specs=[pl.BlockSpec((t, HEAD_PAD), lambda h, i: (i, h)),
                   pl.BlockSpec((None, 1, t), lambda h, i: (h, 0, i))],
        out_shape=[jax.ShapeDtypeStruct((s, MLA_HEADS * HEAD_PAD), F32),
                   jax.ShapeDtypeStruct((MLA_HEADS, 1, s), F32)],
        scratch_shapes=[pltpu.VMEM((t, t), F32)] * 2,
        compiler_params=pltpu.CompilerParams(dimension_semantics=("arbitrary", "arbitrary"),
                                             vmem_limit_bytes=VMEM_LIMIT),
    )(q, k, vt)


def _attn_bwd(q, k, v, kt, do, lse_t, dl_t):
    s = q.shape[0]
    t = min(ATT_TILE, s)
    nq = s // t
    scale = MLA_QK ** -0.5
    c2 = scale * LOG2E

    def body(q_ref, k_ref, v_ref, kt_ref, do_ref, lse_ref, dl_ref, dq_ref, dk_ref, dv_ref, dqt, s0, p0, s1, p1):
        j = pl.program_id(1)
        n = nq - 1 - j

        @pl.when(j == 0)
        def _():
            dqt[...] = jnp.zeros_like(dqt)

        kk = k_ref[...]
        vv = v_ref[...]
        ktb = kt_ref[:MLA_QK, :]

        def rows(m):
            return pl.ds(pl.multiple_of(jnp.where(m < n, j + 1 + m, j) * t, t), t)

        def first(m, sbuf, pbuf):
            r = rows(m)
            sbuf[...] = _mm_nt(kk, q_ref[r, :])
            pbuf[...] = _mm_nt(vv, do_ref[r, :])

        def finish(m, sbuf, pbuf, acc, masked):
            dk, dv = acc
            r = rows(m)
            pt = jnp.exp2(sbuf[...] * c2 - lse_ref[:, r] * LOG2E)
            if masked:
                pt = jnp.where(_causal_mask_t(t), pt, 0.0)
            dv = dv + _mm(pt, do_ref[r, :])
            dst = (pt * (pbuf[...] - dl_ref[:, r])).astype(BF16)
            dk = dk + _mm(dst, q_ref[r, :])
            dqt[:MLA_QK, r] += _mm(ktb, dst)
            return dk, dv

        def pair(jj, acc):
            first(2 * jj + 1, s1, p1)
            acc = finish(2 * jj, s0, p0, acc, False)
            first(2 * jj + 2, s0, p0)
            return finish(2 * jj + 1, s1, p1, acc, False)

        def odd_tail(acc):
            first(n, s1, p1)
            acc = finish(n - 1, s0, p0, acc, False)
            return finish(n, s1, p1, acc, True)

        def even_tail(acc):
            return finish(n, s0, p0, acc, True)

        zero = jnp.zeros((t, HEAD_PAD), F32)
        first(0, s0, p0)
        acc = lax.fori_loop(0, n // 2, pair, (zero, zero))
        dk, dv = lax.cond(n % 2 == 1, odd_tail, even_tail, acc)
        dk_ref[...] = (dk * scale).astype(BF16)
        dv_ref[...] = dv.astype(BF16)

        @pl.when(j == nq - 1)
        def _():
            for cc in range(nq):
                dq_ref[cc * t:(cc + 1) * t, :] = (dqt[:, cc * t:(cc + 1) * t].T * scale).astype(BF16)

    full = pl.BlockSpec((s, HEAD_PAD), lambda h, j: (0, h))
    tile = pl.BlockSpec((t, HEAD_PAD), lambda h, j: (j, h))
    stat = pl.BlockSpec((None, 1, s), lambda h, j: (h, 0, 0))
    return pl.pallas_call(
        body, name="attn_bwd", grid=(MLA_HEADS, nq),
        in_specs=[full, tile, tile, pl.BlockSpec((HEAD_PAD, t), lambda h, j: (h, j)), full, stat, stat],
        out_specs=[full, tile, tile],
        out_shape=[jax.ShapeDtypeStruct((s, MLA_HEADS * HEAD_PAD), BF16)] * 3,
        scratch_shapes=[pltpu.VMEM((HEAD_PAD, s), F32)] + [pltpu.VMEM((t, t), F32)] * 4,
        compiler_params=pltpu.CompilerParams(dimension_semantics=("arbitrary", "arbitrary"),
                                             vmem_limit_bytes=VMEM_LIMIT),
    )(q, k, v, kt, do, lse_t, dl_t)


def _xattn_probs(qxb, memk, hh):
    sl = slice(hh * XATTN_HD, (hh + 1) * XATTN_HD)
    sc = _mm_nt(qxb[:, sl], memk[:, sl]) * (XATTN_HD ** -0.5)
    e = jnp.exp(sc - jnp.max(sc, axis=1, keepdims=True))
    return e / jnp.sum(e, axis=1, keepdims=True)


def _post_forward(o, s_out, gate, h0, memk, memv, w_oa, w_o, w_xq, w_xo, g1, b1, g2, b2):
    a_out = _mm(o, w_oa)
    sg_s = _sigmoid(gate[:, :1024])
    sg_a = _sigmoid(gate[:, 1024:])
    mixin = sg_s * s_out + sg_a * a_out
    r1 = DN_ALPHA * h0 + _mm(mixin, w_o)
    h1, xh1, rs1 = _ln_fwd(r1, g1, b1)
    qxb = _mm(h1, w_xq).astype(BF16)
    ox = jnp.concatenate([_mm(_xattn_probs(qxb, memk, hh), memv[:, hh * XATTN_HD:(hh + 1) * XATTN_HD])
                          for hh in range(XATTN_HEADS)], axis=1)
    r2 = DN_ALPHA * h1 + _mm(ox, w_xo)
    return dict(a_out=a_out, mixin=mixin, r1=r1, h1=h1, qxb=qxb, ox=ox, r2=r2)


def _post_fwd(o, s_out, gate, h0, memk, memv, lns, w_oa, w_o, w_xq, w_xo):
    def body(o_ref, so_ref, gate_ref, h0_ref, mk_ref, mv_ref, g1, b1, g2, b2, woa, wo, wxq, wxo,
             r2_ref, mixin_ref, h1_ref, ox_ref, aout_ref, r1_ref, qx_ref):
        f = _post_forward(o_ref[...], so_ref[...], gate_ref[...].astype(F32), h0_ref[...], mk_ref[...], mv_ref[...],
                          woa[...], wo[...], wxq[...], wxo[...], g1[...], b1[...], g2[...], b2[...])
        r2_ref[...] = f["r2"]
        mixin_ref[...] = f["mixin"].astype(BF16)
        h1_ref[...] = f["h1"].astype(BF16)
        ox_ref[...] = f["ox"].astype(BF16)
        aout_ref[...] = f["a_out"]
        r1_ref[...] = f["r1"]
        qx_ref[...] = f["qxb"]

    s = o.shape[0]
    return _row_call("post_fwd", body, s, min(ROW_TILE, s), [o, s_out, gate, h0], [memk, memv, *lns],
                     [w_oa, w_o, w_xq, w_xo],
                     [(1024, F32), (1024, BF16), (1024, BF16), (1024, BF16), (1024, F32), (1024, F32), (1024, BF16)], [])


def _mlp(r2, target, g2, b2, g3, b3, w_up, w_down):
    cw = 1024
    n_chunk = MLP_HIDDEN // cw

    def body(r2_ref, tgt_ref, g2_ref, b2_ref, g3_ref, b3_ref, wup, wdn,
             dh2_ref, h2b_ref, dup_ref, act_ref, dff_ref, loss_ref, dg3_ref, db3_ref, up_scr):
        @pl.when(pl.program_id(0) == 0)
        def _():
            loss_ref[...] = jnp.zeros_like(loss_ref)
            dg3_ref[...] = jnp.zeros_like(dg3_ref)
            db3_ref[...] = jnp.zeros_like(db3_ref)

        h2, _, _ = _ln_fwd(r2_ref[...], g2_ref[...], b2_ref[...])
        h2b = h2.astype(BF16)
        h2b_ref[...] = h2b
        ff = jnp.zeros(h2.shape, F32)
        for c in range(n_chunk):
            sl = slice(c * cw, (c + 1) * cw)
            a = jnp.maximum(_mm(h2b, wup[:, sl]), 0.0)
            up_scr[:, sl] = a
            actb = (a * a).astype(BF16)
            act_ref[:, sl] = actb
            ff = ff + _mm(actb, wdn[sl, :])
        h3, xh3, rs3 = _ln_fwd(DN_ALPHA * h2 + ff, g3_ref[...], b3_ref[...])
        err = h3 - tgt_ref[...]
        loss_ref[...] += 0.5 * jnp.sum(err * err) * (1.0 / D_MODEL)
        dr3, dg3, db3 = _ln_bwd(err * (1.0 / D_MODEL), xh3, rs3, g3_ref[...])
        dg3_ref[...] += dg3
        db3_ref[...] += db3
        dffb = dr3.astype(BF16)
        dff_ref[...] = dffb
        dh2 = DN_ALPHA * dr3
        for c in range(n_chunk):
            sl = slice(c * cw, (c + 1) * cw)
            dupb = (_mm_nt(dffb, wdn[sl, :]) * (2.0 * up_scr[:, sl])).astype(BF16)
            dup_ref[:, sl] = dupb
            dh2 = dh2 + _mm_nt(dupb, wup[:, sl])
        dh2_ref[...] = dh2

    s = r2.shape[0]
    ts = min(ROW_TILE, s)
    return _row_call("mlp", body, s, ts, [r2, target], [g2, b2, g3, b3], [w_up, w_down],
                     [(1024, F32), (1024, BF16), (MLP_HIDDEN, BF16), (MLP_HIDDEN, BF16), (1024, BF16)],
                     [(8, 128), (1, 1024), (1, 1024)], scratch=[pltpu.VMEM((ts, MLP_HIDDEN), F32)])


def _post_bwd(dh2, o, s_out, gate, a_out, r1, r2, qx, memk, memv, lns, w_oa, w_o, w_xq, w_xo):
    def body(dh2_ref, o_ref, so_ref, gate_ref, aout_ref, r1_ref, r2_ref, qx_ref, mk_ref, mv_ref, g1, b1, g2, b2,
             woa, wo, wxq, wxo,
             do_ref, dl_ref, dso_ref, dgate_ref, dh0_ref, daout_ref, dmix_ref, dqx_ref, dxa_ref,
             dmk_ref, dmv_ref, dg1_ref, db1_ref, dg2_ref, db2_ref):
        @pl.when(pl.program_id(0) == 0)
        def _():
            for r in (dmk_ref, dmv_ref, dg1_ref, db1_ref, dg2_ref, db2_ref):
                r[...] = jnp.zeros_like(r)

        o = o_ref[...]
        s_out = so_ref[...]
        memk, memv = mk_ref[...], mv_ref[...]
        qxb = qx_ref[...]
        _, xh1, rs1 = _ln_fwd(r1_ref[...], g1[...], b1[...])
        _, xh2, rs2 = _ln_fwd(r2_ref[...], g2[...], b2[...])
        dr2, dg2, db2 = _ln_bwd(dh2_ref[...], xh2, rs2, g2[...])
        dg2_ref[...] += dg2
        db2_ref[...] += db2
        dxab = dr2.astype(BF16)
        dxa_ref[...] = dxab
        dox = _mm_nt(dxab, wxo[...])
        dqs = []
        for hh in range(XATTN_HEADS):
            sl = slice(hh * XATTN_HD, (hh + 1) * XATTN_HD)
            p = _xattn_probs(qxb, memk, hh)
            doxh = dox[:, sl].astype(BF16)
            dp = _mm_nt(doxh, memv[:, sl])
            ds = (p * (dp - jnp.sum(dp * p, axis=1, keepdims=True)) * (XATTN_HD ** -0.5)).astype(BF16)
            dqs.append(_mm(ds, memk[:, sl]))
            dmk_ref[:, sl] += _mm_tn(ds, qxb[:, sl])
            dmv_ref[:, sl] += _mm_tn(p, doxh)
        dqxb = jnp.concatenate(dqs, axis=1).astype(BF16)
        dqx_ref[...] = dqxb
        dh1 = DN_ALPHA * dr2 + _mm_nt(dqxb, wxq[...])
        dr1, dg1, db1 = _ln_bwd(dh1, xh1, rs1, g1[...])
        dg1_ref[...] += dg1
        db1_ref[...] += db1
        dh0_ref[...] = DN_ALPHA * dr1
        dmixb = dr1.astype(BF16)
        dmix_ref[...] = dmixb
        dmixin = _mm_nt(dmixb, wo[...])
        sg_s = _sigmoid(gate_ref[:, :1024].astype(F32))
        sg_a = _sigmoid(gate_ref[:, 1024:].astype(F32))
        dso_ref[...] = dmixin * sg_s
        daoutb = (dmixin * sg_a).astype(BF16)
        daout_ref[...] = daoutb
        dgate_ref[:, :1024] = (dmixin * s_out * sg_s * (1.0 - sg_s)).astype(BF16)
        dgate_ref[:, 1024:] = (dmixin * aout_ref[...] * sg_a * (1.0 - sg_a)).astype(BF16)
        d_o = _mm_nt(daoutb, woa[...])
        do_ref[...] = d_o.astype(BF16)
        lane = lax.broadcasted_iota(jnp.int32, (o.shape[0], HEAD_PAD), 1)
        dlc = jnp.zeros((o.shape[0], HEAD_PAD), F32)
        for hh in range(MLA_HEADS):
            sl = slice(hh * HEAD_PAD, (hh + 1) * HEAD_PAD)
            dl = jnp.sum(d_o[:, sl] * o[:, sl], axis=1, keepdims=True)
            dlc = dlc + jnp.where(lane == hh, dl, 0.0)
        dl_ref[...] = dlc

    s = o.shape[0]
    m = memk.shape[0]
    return _row_call(
        "post_bwd", body, s, min(ROW_TILE, s), [dh2, o, s_out, gate, a_out, r1, r2, qx], [memk, memv, *lns],
        [w_oa, w_o, w_xq, w_xo],
        [(1024, BF16), (HEAD_PAD, F32), (1024, F32), (2048, BF16), (1024, F32),
         (1024, BF16), (1024, BF16), (1024, BF16), (1024, BF16)],
        [(m, 1024), (m, 1024), (1, 1024), (1, 1024), (1, 1024), (1, 1024)])


def _s5out_bwd(d_so, yl, z3, h, d_skip, pwb, w_c, w_glu):
    s = d_so.shape[0]
    ts = min(ROW_TILE, s)
    n_tiles = s // ts

    def body(dso_ref, yl_ref, z3_ref, h_ref, d_ref, pw_ref, wc, wglu,
             lam_ref, dup_ref, yg_ref, dy12_ref, dyl_ref, dd_ref, da_ref, gh_scr, carry, acc):
        @pl.when(pl.program_id(0) == 0)
        def _():
            dd_ref[...] = jnp.zeros_like(dd_ref)
            carry[...] = jnp.zeros_like(carry)
            acc[...] = jnp.zeros_like(acc)

        yl = yl_ref[...]
        ygb = _gelu(yl).astype(BF16)
        yg_ref[...] = ygb
        y12 = _mm(ygb, wglu[...])
        sg = _sigmoid(y12[:, 1024:])
        dso = dso_ref[...]
        dy12b = jnp.concatenate([dso * sg, dso * y12[:, :1024] * sg * (1.0 - sg)], axis=1).astype(BF16)
        dy12_ref[...] = dy12b
        dyl = _mm_nt(dy12b, wglu[...]) * _gelu_grad(yl)
        dylb = dyl.astype(BF16)
        dyl_ref[...] = dylb
        gh_scr[...] = _mm_nt(dylb, wc[...])
        _scan_bwd_rows(pw_ref, gh_scr, h_ref, lam_ref, carry, acc, ts // 8)
        dup_ref[...] = dyl * d_ref[...]
        dd_ref[...] += jnp.sum(dyl * z3_ref[:, :256], axis=0, keepdims=True)

        @pl.when(pl.program_id(0) == n_tiles - 1)
        def _():
            da_ref[...] = jnp.sum(acc[...], axis=0, keepdims=True)

    return _row_call("s5out_bwd", body, s, ts, [d_so, yl, z3, h], [d_skip, pwb], [w_c, w_glu],
                     [(2 * NS, F32), (256, F32), (256, BF16), (2048, BF16), (256, BF16)], [(1, 256), (1, 2 * NS)],
                     reverse=True, scratch=[pltpu.VMEM((ts, 2 * NS), F32), pltpu.VMEM((8, 2 * NS), F32),
                                            pltpu.VMEM((8, 2 * NS), F32)])


def _in_bwd(x, tab, z3, dq, dk, dv, lam, du_p, dgate, dh0p, ln_g, ln_b, qg, kvg, w_inx, w_b, w_q2, w_kv2):
    def body(x_ref, tab_ref, z3_ref, dq_ref, dk_ref, dv_ref, lam_ref, dup_ref, dgate_ref, dh0p_ref,
             lng_ref, lnb_ref, qg_ref, kvg_ref, winx, wb, wq2, wkv2,
             dx_ref, h0b_ref, dz_ref, ub_ref, cqn_ref, ckvn_ref, dq2_ref, dkv2_ref,
             dlng_ref, dlnb_ref, dqg_ref, dkvg_ref):
        @pl.when(pl.program_id(0) == 0)
        def _():
            for r in (dlng_ref, dlnb_ref, dqg_ref, dkvg_ref):
                r[...] = jnp.zeros_like(r)

        h0, xh0, rs0 = _ln_fwd(x_ref[...], lng_ref[...], lnb_ref[...])
        h0b_ref[...] = h0.astype(BF16)
        z3 = z3_ref[...]
        ub_ref[...] = z3[:, :256].astype(BF16)
        tab_v = tab_ref[...]
        cq1, cq2, ck1 = tab_v[:, :128], tab_v[:, 128:], tab_v[:, :128]
        du = _mm_nt(lam_ref[...], wb[...]) + dup_ref[...]
        dq = dq_ref[...].astype(F32)
        dq2b = jnp.concatenate([dq * _tile_heads(cq1), dq * _tile_heads(cq2)], axis=1).astype(BF16)
        dq2_ref[...] = dq2b
        cqg, cqn, rq = _rms_fwd(z3[:, 256:512], qg_ref[...])
        cqn_ref[...] = cqg.astype(BF16)
        dcq, dqg = _rms_bwd(_mm_nt(dq2b, wq2[...]), cqn, rq, qg_ref[...])
        dqg_ref[...] += dqg
        dkv2b = jnp.concatenate([dk_ref[...], dv_ref[...]], axis=1)
        dk = dk_ref[...].astype(F32)
        dkv2_ref[...] = dkv2b
        ckvg, ckvn, rkv = _rms_fwd(z3[:, 512:768], kvg_ref[...])
        ckvn_ref[...] = ckvg.astype(BF16)
        dckv, dkvg = _rms_bwd(_mm_nt(dkv2b, wkv2[...]), ckvn, rkv, kvg_ref[...])
        dkvg_ref[...] += dkvg
        dkrp = dk[:, :HEAD_PAD]
        for hh in range(1, MLA_HEADS):
            dkrp = dkrp + dk[:, hh * HEAD_PAD:(hh + 1) * HEAD_PAD]
        dzb = jnp.concatenate([a.astype(BF16) for a in (du, dcq, dckv, dkrp * ck1, dkrp * cq2)] + [dgate_ref[...]],
                              axis=1)
        dz_ref[...] = dzb
        dh0 = _mm_nt(dzb, winx[...]) + dh0p_ref[...]
        dx, dg, db = _ln_bwd(dh0, xh0, rs0, lng_ref[...])
        dx_ref[...] = dx
        dlng_ref[...] += dg
        dlnb_ref[...] += db

    s = x.shape[0]
    return _row_call(
        "in_bwd", body, s, min(ROW_TILE, s), [x, tab, z3, dq, dk, dv, lam, du_p, dgate, dh0p],
        [ln_g, ln_b, qg, kvg], [w_inx, w_b, w_q2, w_kv2],
        [(1024, F32), (1024, BF16), (3072, BF16), (256, BF16), (256, BF16), (256, BF16), (2048, BF16), (2048, BF16)],
        [(1, 1024), (1, 1024), (1, 256), (1, 256)])


def _wgrad(name, xs, dy, out_dtype=F32, blocks=1, after=None):
    s, k = xs.shape
    n = dy.shape[1]
    nb = n // blocks
    tk = min(k, 1024)
    tn = max(d for d in range(128, min(n, 2048) + 1, 128) if n % d == 0)
    ts = min(s, 2048)
    per_tile = tn // nb
    assert s % ts == 0 and k % tk == 0 and n % tn == 0 and (blocks == 1 or tn % nb == 0), (name, s, k, n)
    last = s // ts - 1

    def body(x_ref, dy_ref, *rest):
        o_ref, acc = rest[-2:]

        @pl.when(pl.program_id(2) == 0)
        def _():
            acc[...] = jnp.zeros_like(acc)

        acc[...] += _mm_tn(x_ref[...], dy_ref[...])

        @pl.when(pl.program_id(2) == last)
        def _():
            if blocks == 1:
                o_ref[...] = acc[...].astype(out_dtype)
            else:
                for d in range(per_tile):
                    o_ref[d] = acc[:, d * nb:(d + 1) * nb].astype(out_dtype)

    if blocks == 1:
        out_shape = jax.ShapeDtypeStruct((k, n), out_dtype)
        out_spec = pl.BlockSpec((tk, tn), lambda a, b, c: (a, b))
    else:
        out_shape = jax.ShapeDtypeStruct((blocks, k, nb), out_dtype)
        out_spec = pl.BlockSpec((per_tile, tk, nb), lambda a, b, c: (b, a, 0))
    return pl.pallas_call(
        body, name=name, grid=(k // tk, n // tn, s // ts),
        in_specs=[pl.BlockSpec((ts, tk), lambda a, b, c: (c, a)), pl.BlockSpec((ts, tn), lambda a, b, c: (c, b))]
        + ([] if after is None else [pl.BlockSpec(memory_space=pl.ANY)]),
        out_specs=out_spec, out_shape=out_shape, scratch_shapes=[pltpu.VMEM((tk, tn), F32)],
        compiler_params=pltpu.CompilerParams(dimension_semantics=("arbitrary", "arbitrary", "arbitrary"),
                                             vmem_limit_bytes=VMEM_LIMIT),
    )(xs, dy, *([] if after is None else [after]))


def _mesh_pos():
    x, y, c = lax.axis_index("x"), lax.axis_index("y"), lax.axis_index("c")
    return x, y, c


def _peer(x, y, c, k):
    px = 1 - x if k & 4 else x
    py = 1 - y if k & 2 else y
    pc = 1 - c if k & 1 else c
    return (px, py, pc), 4 * px + 2 * py + pc


def _all_gather_two_level(name, arrays):
    n = len(arrays)

    def body(*refs):
        src, dst = refs[:n], refs[n:2 * n]
        send_sems, recv_sems, local_sems = refs[2 * n:]
        x, y, c = _mesh_pos()
        me = 4 * x + 2 * y + c
        sibling = (x, y, 1 - c)

        def copy(i, k, source, slot, to):
            return pltpu.make_async_remote_copy(src_ref=source, dst_ref=dst[i].at[slot],
                                                send_sem=send_sems.at[(k - 1) * n + i],
                                                recv_sem=recv_sems.at[(k - 1) * n + i],
                                                device_id=to, device_id_type=MESH)

        own = [pltpu.make_async_copy(src[i], dst[i].at[me], local_sems.at[i]) for i in range(n)]
        for cp in own:
            cp.start()
        sends = []
        for k in (1, 2, 4, 6):
            peer, _ = _peer(x, y, c, k)
            for i in range(n):
                sends.append(copy(i, k, src[i], me, peer))
                sends[-1].start()
        for k in (2, 4, 6):
            peer, pid = _peer(x, y, c, k)
            for i in range(n):
                copy(i, k, src[i], pid, peer).wait_recv()
                sends.append(copy(i, k + 1, dst[i].at[pid], pid, sibling))
                sends[-1].start()
        for k in (1, 3, 5, 7):
            peer, pid = _peer(x, y, c, k)
            for i in range(n):
                copy(i, k, src[i], pid, peer).wait_recv()
        for cp in sends:
            cp.wait_send()
        for cp in own:
            cp.wait()

    n_sem = n * (N_DEV - 1)
    return pl.pallas_call(
        body, name=name,
        in_specs=[pl.BlockSpec(memory_space=pl.ANY)] * n,
        out_specs=[pl.BlockSpec(memory_space=pl.ANY)] * n,
        out_shape=[jax.ShapeDtypeStruct((N_DEV,) + a.shape, a.dtype) for a in arrays],
        scratch_shapes=[pltpu.SemaphoreType.DMA((n_sem,)), pltpu.SemaphoreType.DMA((n_sem,)),
                        pltpu.SemaphoreType.DMA((n,))],
    )(*arrays)


_HBM = pl.BlockSpec(memory_space=pltpu.HBM)
_SEM = pl.BlockSpec(memory_space=pltpu.SEMAPHORE)


def _exchange_start(name, arrays, gather, after=None):
    n = len(arrays)
    n_sem = n * (N_DEV - 1)
    me = 4 * lax.axis_index("x") + 2 * lax.axis_index("y") + lax.axis_index("c")
    lands = []
    for a in arrays:
        own = a[None] if gather else lax.dynamic_slice_in_dim(a, me, 1, 0)
        lands.append(lax.dynamic_update_slice(lax.empty((N_DEV,) + a.shape[-2:], a.dtype), own, (me, 0, 0)))
    n_after = 0 if after is None else 1

    def body(*refs):
        src, land = refs[:n], refs[n:2 * n]
        send_sems, recv_sems = refs[2 * n + n_after], refs[2 * n + n_after + 1]
        token = refs[-1]
        x, y, c = _mesh_pos()
        me_in = 4 * x + 2 * y + c
        for k in range(1, N_DEV):
            peer, pid = _peer(x, y, c, k)
            for i in range(n):
                idx = (k - 1) * n + i
                pltpu.make_async_remote_copy(src_ref=src[i] if gather else src[i].at[pid], dst_ref=land[i].at[me_in],
                                             send_sem=send_sems.at[idx], recv_sem=recv_sems.at[idx],
                                             device_id=peer, device_id_type=MESH).start()
        token[...] = jnp.zeros_like(token)

    operands = [pltpu.with_memory_space_constraint(a, pltpu.HBM) for a in list(arrays) + lands]
    outs = pl.pallas_call(
        body, name=name,
        out_shape=(pltpu.SemaphoreType.DMA((n_sem,)), pltpu.SemaphoreType.DMA((n_sem,)),
                   *[pltpu.HBM(a.shape, a.dtype) for a in list(arrays) + lands],
                   jax.ShapeDtypeStruct((8, 128), F32)),
        in_specs=[_HBM] * (2 * n) + [pl.BlockSpec(memory_space=pl.ANY)] * n_after,
        out_specs=(_SEM, _SEM, *[_HBM] * (2 * n), pl.BlockSpec(memory_space=pltpu.VMEM)),
        input_output_aliases={i: 2 + i for i in range(2 * n)},
        compiler_params=pltpu.CompilerParams(has_side_effects=pltpu.SideEffectType.DATAFLOW_SIDE_EFFECTING),
    )(*operands, *([after] if n_after else []))
    return (gather, outs[0], outs[1], outs[2:2 + n], outs[2 + n:2 + 2 * n]), outs[-1]


def _exchange_wait(name, handle, after):
    gather, send_sems, recv_sems, srcs, lands = handle
    n = len(srcs)

    def body(*refs):
        src, land = refs[:n], refs[n:2 * n]
        s_sems, r_sems = refs[2 * n], refs[2 * n + 1]
        x, y, c = _mesh_pos()
        for k in range(1, N_DEV):
            peer, pid = _peer(x, y, c, k)
            for i in range(n):
                idx = (k - 1) * n + i
                cp = pltpu.make_async_remote_copy(src_ref=src[i] if gather else src[i].at[pid], dst_ref=land[i].at[pid],
                                                  send_sem=s_sems.at[idx], recv_sem=r_sems.at[idx],
                                                  device_id=peer, device_id_type=MESH)
                cp.wait_send()
                cp.wait_recv()

    outs = pl.pallas_call(
        body, name=name,
        out_shape=tuple(pltpu.HBM(a.shape, a.dtype) for a in list(srcs) + list(lands)),
        in_specs=[_HBM] * (2 * n) + [_SEM, _SEM, pl.BlockSpec(memory_space=pl.ANY)],
        out_specs=tuple([_HBM] * (2 * n)),
        input_output_aliases={i: i for i in range(2 * n)},
        compiler_params=pltpu.CompilerParams(has_side_effects=pltpu.SideEffectType.DATAFLOW_SIDE_EFFECTING),
    )(*srcs, *lands, send_sems, recv_sems, after)
    return list(outs[n:])


def _adamw(name, parts, w, m, v):
    a_rows, b_cols = w.shape
    ta = min(a_rows, ADAM_TILE)
    assert a_rows % ta == 0
    c1 = 1.0 - ADAM_B1 ** ADAM_STEP
    c2 = 1.0 - ADAM_B2 ** ADAM_STEP

    def body(p_ref, w_ref, m_ref, v_ref, g_ref, d_ref, mo_ref, vo_ref):
        g = p_ref[0].astype(F32)
        for d in range(1, N_DEV):
            g = g + p_ref[d].astype(F32)
        g_ref[...] = g
        mn = ADAM_B1 * m_ref[...] + (1.0 - ADAM_B1) * g
        vn = ADAM_B2 * v_ref[...] + (1.0 - ADAM_B2) * (g * g)
        mo_ref[...] = mn
        vo_ref[...] = vn
        d_ref[...] = -ADAM_LR * ((mn / c1) / (jnp.sqrt(vn / c2) + ADAM_EPS) + ADAM_WD * w_ref[...])

    row = pl.BlockSpec((ta, b_cols), lambda i: (i, 0))
    return pl.pallas_call(
        body, name=name, grid=(a_rows // ta,),
        in_specs=[pl.BlockSpec((N_DEV, ta, b_cols), lambda i: (0, i, 0)), row, row, row],
        out_specs=[row] * 4,
        out_shape=[jax.ShapeDtypeStruct((a_rows, b_cols), F32)] * 4,
        compiler_params=pltpu.CompilerParams(dimension_semantics=("arbitrary",), vmem_limit_bytes=VMEM_LIMIT),
    )(parts, w, m, v)


def _pack_rows(arrays, rows):
    flat = jnp.concatenate([a.reshape(-1) for a in arrays])
    return jnp.pad(flat, (0, rows * LANES - flat.shape[0])).reshape(rows, LANES)


def _cols_from_blocks(g):
    return g.transpose(1, 0, 2).reshape(g.shape[1], N_DEV * g.shape[2])


def _blocks_from_cols(w):
    return w.reshape(w.shape[0], N_DEV, w.shape[1] // N_DEV).transpose(1, 0, 2)


def _unpack_flat(flat2d, shapes):
    flat = flat2d.reshape(-1)
    out, off = [], 0
    for shp in shapes:
        sz = math.prod(shp)
        out.append(flat[off:off + sz].reshape(shp))
        off += sz
    return out


def _s5_mats(lam_re, lam_im, log_dt, b_re, b_im, c_re, c_im):
    lr = jnp.minimum(lam_re, S5_MAX_RE)
    li = lam_im
    dt = jnp.exp(log_dt)[:, None]
    mag = jnp.exp(lr * dt)
    ang = li * dt
    ab_re = mag * jnp.cos(ang)
    ab_im = mag * jnp.sin(ang)
    den = lr * lr + li * li
    nr = ab_re - 1.0
    f_re = ((nr * lr + ab_im * li) / den)[..., None]
    f_im = ((ab_im * lr - nr * li) / den)[..., None]
    bb_re = f_re * b_re - f_im * b_im
    bb_im = f_re * b_im + f_im * b_re
    eye = jnp.eye(S5_GROUPS, dtype=F32)
    a = jnp.stack([ab_re.reshape(-1), ab_im.reshape(-1)])
    wb = jnp.concatenate([jnp.einsum("gph,gk->ghkp", bb_re, eye).reshape(S5_WIDTH, NS),
                          jnp.einsum("gph,gk->ghkp", bb_im, eye).reshape(S5_WIDTH, NS)], axis=1)
    wc = jnp.concatenate([jnp.einsum("ghp,gk->gpkh", c_re, eye).reshape(NS, S5_WIDTH),
                          -jnp.einsum("ghp,gk->gpkh", c_im, eye).reshape(NS, S5_WIDTH)], axis=0)
    return a, wb, wc


def _power_table(a):
    ar, ai = a[0], a[1]
    rows_r, rows_i = [ar], [ai]
    for _ in range(7):
        pr, pi = rows_r[-1], rows_i[-1]
        rows_r.append(pr * ar - pi * ai)
        rows_i.append(pr * ai + pi * ar)
    return jnp.concatenate([jnp.stack(rows_r), jnp.stack(rows_i)], axis=1)


def _rope_table(positions):
    inv = ROPE_THETA ** (-jnp.arange(0, MLA_ROPE, 2, dtype=F32) / MLA_ROPE)
    inv128 = jnp.concatenate([jnp.zeros((MLA_NOPE,), F32), inv, inv, jnp.zeros((32,), F32)])
    sign = jnp.concatenate([jnp.zeros((MLA_NOPE,), F32), -jnp.ones((16,), F32), jnp.ones((16,), F32),
                            jnp.zeros((32,), F32)])
    ang = positions.astype(F32)[:, None] * inv128
    return jnp.concatenate([jnp.cos(ang), jnp.sin(ang) * sign], axis=1)


def _derived_weights(full):
    w_in = full["w_in"]
    k1, k2 = w_in[:, 768:784], w_in[:, 784:800]
    z64, z32 = jnp.zeros((1024, 64), BF16), jnp.zeros((1024, 32), BF16)
    w_inx = jnp.concatenate([w_in[:, :768], z64, k1, k2, z32, z64, k2, k1, z32, w_in[:, 800:]], axis=1)
    uq = full["w_uq"].reshape(256, MLA_HEADS, MLA_QK)
    nope, r1, r2 = uq[:, :, :64], uq[:, :, 64:80], uq[:, :, 80:]
    zq64, zq32 = jnp.zeros((256, MLA_HEADS, 64), BF16), jnp.zeros((256, MLA_HEADS, 32), BF16)
    w_q2 = jnp.concatenate([jnp.concatenate([nope, r1, r2, zq32], axis=2).reshape(256, 1024),
                            jnp.concatenate([zq64, r2, r1, zq32], axis=2).reshape(256, 1024)], axis=1)
    ukv = full["w_ukv"].reshape(256, MLA_HEADS, 128)
    w_kv2 = jnp.concatenate([jnp.concatenate([ukv[:, :, :64], zq64], axis=2).reshape(256, 1024),
                             jnp.concatenate([ukv[:, :, 64:], zq64], axis=2).reshape(256, 1024)], axis=1)
    return w_inx, w_q2, w_kv2


def _oa_padded(w_oa):
    oa = w_oa.reshape(MLA_HEADS, MLA_V, 1024)
    return jnp.concatenate([oa, jnp.zeros_like(oa)], axis=1).reshape(1024, 1024)


def _fold_w_in(d_inx):
    d_k1 = d_inx[:, 832:848] + d_inx[:, 976:992]
    d_k2 = d_inx[:, 848:864] + d_inx[:, 960:976]
    return jnp.concatenate([d_inx[:, :768], d_k1, d_k2, d_inx[:, 1024:]], axis=1)


def _fold_qkv(d_q2, d_kv2):
    a = d_q2[:, :1024].reshape(256, MLA_HEADS, 128)
    b = d_q2[:, 1024:].reshape(256, MLA_HEADS, 128)
    d_w_uq = jnp.concatenate([a[:, :, :64], a[:, :, 64:80] + b[:, :, 80:96], a[:, :, 80:96] + b[:, :, 64:80]],
                             axis=2).reshape(256, MLA_HEADS * MLA_QK)
    kk = d_kv2[:, :1024].reshape(256, MLA_HEADS, 128)
    vv = d_kv2[:, 1024:].reshape(256, MLA_HEADS, 128)
    d_w_ukv = jnp.concatenate([kk[:, :, :64], vv[:, :, :64]], axis=2).reshape(256, 1024)
    return d_w_uq, d_w_ukv


def kernel(x, mem, positions, ln_in_g, ln_in_b, w_in, s5_lam_re, s5_lam_im, s5_log_dt, s5_b_re, s5_b_im, s5_c_re, s5_c_im, s5_d, w_glu, q_norm_g, w_uq, kv_norm_g, w_ukv, w_oa, w_o, ln1_g, ln1_b, w_xq, w_xk, w_xv, w_xo, ln2_g, ln2_b, w_up, w_down, ln3_g, ln3_b, loss_target, m_ln_in_g, m_ln_in_b, m_w_in, m_s5_lam_re, m_s5_lam_im, m_s5_log_dt, m_s5_b_re, m_s5_b_im, m_s5_c_re, m_s5_c_im, m_s5_d, m_w_glu, m_q_norm_g, m_w_uq, m_kv_norm_g, m_w_ukv, m_w_oa, m_w_o, m_ln1_g, m_ln1_b, m_w_xq, m_w_xk, m_w_xv, m_w_xo, m_ln2_g, m_ln2_b, m_w_up, m_w_down, m_ln3_g, m_ln3_b, v_ln_in_g, v_ln_in_b, v_w_in, v_s5_lam_re, v_s5_lam_im, v_s5_log_dt, v_s5_b_re, v_s5_b_im, v_s5_c_re, v_s5_c_im, v_s5_d, v_w_glu, v_q_norm_g, v_w_uq, v_kv_norm_g, v_w_ukv, v_w_oa, v_w_o, v_ln1_g, v_ln1_b, v_w_xq, v_w_xk, v_w_xv, v_w_xo, v_ln2_g, v_ln2_b, v_w_up, v_w_down, v_ln3_g, v_ln3_b):
    args = dict(locals())
    wts = {n: args[n] for n in WEIGHTS}
    mom = {n: args["m_" + n] for n in WEIGHTS}
    vel = {n: args["v_" + n] for n in WEIGHTS}
    xs, mems, tgt = x[0], mem[0], loss_target[0]
    s = xs.shape[0]
    names_big = [n for (n, _, _, _) in SHARDED]
    kind = {n: kd for (n, kd, _, _) in SHARDED}
    n_small = sum(math.prod(wts[n].shape) for n in SMALL)
    small_rows = _round_up(n_small + 1, 8 * LANES) // LANES

    late = [n for n in names_big if n not in EARLY]
    shard = {n: wts[n][0].astype(BF16) for n in names_big}
    g_early = dict(zip(EARLY, _all_gather_two_level("all_gather", [shard[n] for n in EARLY])))
    ag_handle, ag_token = _exchange_start("all_gather_late", [shard[n] for n in late], True, after=g_early["w_in"])
    full = {n: _cols_from_blocks(g_early[n]) for n in EARLY}
    w_inx, w_q2, w_kv2 = _derived_weights(full)
    s5_args = (s5_lam_re[0], s5_lam_im[0], s5_log_dt[0], s5_b_re[0], s5_b_im[0], s5_c_re[0], s5_c_im[0])
    (a_mat, w_b, w_c), s5_vjp = jax.vjp(_s5_mats, *s5_args)
    w_bb, w_cb = w_b.astype(BF16), w_c.astype(BF16)
    pw = _power_table(a_mat)
    pwb = _power_table(a_mat * jnp.array([[1.0], [-1.0]], F32))[::-1]
    tab = _rope_table(positions[0])
    row = lambda a: a.reshape(1, -1)
    ln_g, ln_b = row(ln_in_g) + ag_token[0:1, 0:1], row(ln_in_b)
    lns = [ln1_g, ln1_b, ln2_g, ln2_b]

    h0, z3, gate, h, q, k, v, kt, vt = _in_fwd(xs, tab, ln_g, ln_b, q_norm_g, kv_norm_g, pw, w_inx, w_bb, w_q2, w_kv2)
    yl, s_out = _s5out_fwd(h, z3, s5_d, w_cb, full["w_glu"])
    o, lse_t = _attn_fwd(q, k, vt)
    g_late = dict(zip(late, _exchange_wait("all_gather_late_wait", ag_handle, after=lse_t)))
    full.update({n: g_late[n].reshape(-1, g_late[n].shape[2]) if kind[n] == "row" else _cols_from_blocks(g_late[n])
                 for n in late})
    w_oap = _oa_padded(full["w_oa"])
    memk, memv = _mem_kv(mems, full["w_xk"], full["w_xv"])
    r2, mixin_b, h1_b, ox_b, a_out, r1, qx_b = _post_fwd(o, s_out, gate, h0, memk, memv, lns,
                                        w_oap, full["w_o"], full["w_xq"], full["w_xo"])
    dh2, h2_b, dup_b, act_b, dff_b, loss_acc, d_ln3_g, d_ln3_b = _mlp(
        r2, tgt, ln2_g, ln2_b, ln3_g, ln3_b, full["w_up"], full["w_down"])
    (do_b, delta, d_so, dgate, dh0p, daout_b, dmix_b, dqx_b, dxa_b,
     dmk, dmv, d_ln1_g, d_ln1_b, d_ln2_g, d_ln2_b) = _post_bwd(
        dh2, o, s_out, gate, a_out, r1, r2, qx_b, memk, memv, lns, w_oap, full["w_o"], full["w_xq"], full["w_xo"])
    rows8 = lambda g: g.reshape(N_DEV, g.shape[0] // N_DEV, g.shape[1])
    d_w_oa = _wgrad("wg_oa", o, daout_b).reshape(MLA_HEADS, 128, 1024)[:, :64].reshape(512, 1024)
    send = {
        "w_oa": _blocks_from_cols(d_w_oa).astype(BF16),
        "w_up": _wgrad("wg_up", h2_b, dup_b, BF16, N_DEV),
        "w_o": rows8(_wgrad("wg_o", mixin_b, dmix_b, BF16)),
        "w_xq": rows8(_wgrad("wg_xq", h1_b, dqx_b, BF16)),
        "w_xk": rows8(_wgrad("wg_xk", mems, dmk, BF16)),
        "w_xv": rows8(_wgrad("wg_xv", mems, dmv, BF16)),
        "w_xo": rows8(_wgrad("wg_xo", ox_b, dxa_b, BF16)),
        "w_down": rows8(_wgrad("wg_down", act_b, dff_b, BF16)),
    }
    rs_handle, rs_token = _exchange_start("grad_exchange_late", [send[n] for n in late], False)
    dl_t = delta.T[:MLA_HEADS].reshape(MLA_HEADS, 1, s) + rs_token[0, 0]
    dq, dk, dv = _attn_bwd(q, k, v, kt, do_b, lse_t, dl_t)
    lam, du_p, yg_b, dy12_b, dyl_b, d_s5_d, d_a = _s5out_bwd(d_so, yl, z3, h, s5_d, pwb, w_cb, full["w_glu"])
    (dx, h0_b, dz_b, u_b, cqn_b, ckvn_b, dq2_b, dkv2_b, d_ln_g, d_ln_b, d_qg, d_kvg) = _in_bwd(
        xs, tab, z3, dq, dk, dv, lam, du_p, dgate, dh0p, ln_g, ln_b, q_norm_g, kv_norm_g, w_inx, w_bb, w_q2, w_kv2)

    d_w_in = _fold_w_in(_wgrad("wg_in", h0_b, dz_b))
    d_w_uq, d_w_ukv = _fold_qkv(_wgrad("wg_q", cqn_b, dq2_b), _wgrad("wg_kv", ckvn_b, dkv2_b))
    send.update({
        "w_in": _blocks_from_cols(d_w_in).astype(BF16), "w_uq": _blocks_from_cols(d_w_uq).astype(BF16),
        "w_ukv": _blocks_from_cols(d_w_ukv).astype(BF16), "w_glu": _wgrad("wg_glu", yg_b, dy12_b, BF16, N_DEV),
    })
    re_handle, re_token = _exchange_start("grad_exchange_early", [send[n] for n in EARLY], False)
    d_s5 = s5_vjp((d_a.reshape(2, NS), _wgrad("wg_s5b", u_b, lam, after=re_token),
                   _wgrad("wg_s5c", h, dyl_b, after=re_token)))
    small_grads = {
        "ln_in_g": d_ln_g, "ln_in_b": d_ln_b, "s5_lam_re": d_s5[0], "s5_lam_im": d_s5[1], "s5_log_dt": d_s5[2],
        "s5_b_re": d_s5[3], "s5_b_im": d_s5[4], "s5_c_re": d_s5[5], "s5_c_im": d_s5[6], "s5_d": d_s5_d,
        "q_norm_g": d_qg, "kv_norm_g": d_kvg, "ln1_g": d_ln1_g, "ln1_b": d_ln1_b, "ln2_g": d_ln2_g,
        "ln2_b": d_ln2_b, "ln3_g": d_ln3_g, "ln3_b": d_ln3_b,
    }
    recv_small = _all_gather_two_level(
        "grad_exchange", [_pack_rows([small_grads[n] for n in SMALL] + [loss_acc[0, :1]], small_rows)])[0]

    recv = dict(zip(late, _exchange_wait("grad_exchange_late_wait", rs_handle, after=recv_small)))
    results = [dict(), dict(), dict(), dict()]
    small_out = _adamw("adamw_small", recv_small,
                       *[_pack_rows([t[n] for n in SMALL], small_rows) for t in (wts, mom, vel)])
    for res, fs in zip(results, small_out):
        for n, a in zip(SMALL, _unpack_flat(fs, [wts[n].shape for n in SMALL])):
            res[n] = a
    for n in late + list(EARLY):
        if n == EARLY[0]:
            recv.update(zip(EARLY, _exchange_wait("grad_exchange_early_wait", re_handle, after=results[0][late[-1]])))
        outs = _adamw("adamw_" + n, recv[n], wts[n][0], mom[n][0], vel[n][0])
        for res, a in zip(results, outs):
            res[n] = a[None]

    loss = small_out[0].reshape(-1)[n_small]
    return (loss, dx[None], *[res[n] for res in results for n in WEIGHTS])
```

```python
import math

import jax
import jax.numpy as jnp
from jax import lax
from jax.experimental import pallas as pl
from jax.experimental.pallas import tpu as pltpu

F32 = jnp.float32
BF16 = jnp.bfloat16

D_MODEL = 1024
S5_WIDTH = 256
S5_GROUP_CH = 16
S5_GROUPS = 16
S5_STATE = 64
NS = S5_GROUPS * S5_STATE
S5_MAX_RE = -1e-4
MLA_HEADS = 8
MLA_NOPE = 64
MLA_ROPE = 32
MLA_QK = 96
MLA_V = 64
HEAD_PAD = 128
ROPE_THETA = 10000.0
XATTN_HEADS = 4
XATTN_HD = 256
MLP_HIDDEN = 4096
LN_EPS = 1e-5
RMS_EPS = 1e-6
NEG_INF = -1e30
LOG2E = 1.4426950408889634
DN_ALPHA = 2.0 ** 0.25
ADAM_LR = 0.001
ADAM_B1 = 0.9
ADAM_B2 = 0.999
ADAM_EPS = 1e-08
ADAM_WD = 0.01
ADAM_STEP = 10

N_DEV = 8
MESH = pl.DeviceIdType.MESH
LANES = 1024
VMEM_LIMIT = 60 * 1024 * 1024

ROW_TILE = 256
ATT_TILE = 1024
ADAM_TILE = 256

SHARDED = (
    ("w_in", "col", 1024, 2848), ("w_glu", "col", 256, 2048), ("w_uq", "col", 256, 768),
    ("w_ukv", "col", 256, 1024), ("w_oa", "col", 512, 1024), ("w_o", "row", 1024, 1024),
    ("w_xq", "row", 1024, 1024), ("w_xk", "row", 1024, 1024), ("w_xv", "row", 1024, 1024),
    ("w_xo", "row", 1024, 1024), ("w_up", "col", 1024, 4096), ("w_down", "row", 4096, 1024),
)
EARLY = ("w_in", "w_glu", "w_uq", "w_ukv")
SMALL = ("ln_in_g", "ln_in_b", "s5_lam_re", "s5_lam_im", "s5_log_dt", "s5_b_re", "s5_b_im", "s5_c_re",
         "s5_c_im", "s5_d", "q_norm_g", "kv_norm_g", "ln1_g", "ln1_b", "ln2_g", "ln2_b", "ln3_g", "ln3_b")
WEIGHTS = ("ln_in_g", "ln_in_b", "w_in", "s5_lam_re", "s5_lam_im", "s5_log_dt", "s5_b_re", "s5_b_im",
           "s5_c_re", "s5_c_im", "s5_d", "w_glu", "q_norm_g", "w_uq", "kv_norm_g", "w_ukv", "w_oa", "w_o",
           "ln1_g", "ln1_b", "w_xq", "w_xk", "w_xv", "w_xo", "ln2_g", "ln2_b", "w_up", "w_down", "ln3_g", "ln3_b")


def _round_up(n, m):
    return (n + m - 1) // m * m


def _bf(a):
    return a.astype(BF16)


def _mm(a, b):
    return jnp.dot(_bf(a), _bf(b), preferred_element_type=F32)


def _mm_nt(a, b):
    return lax.dot_general(_bf(a), _bf(b), (((1,), (1,)), ((), ())), preferred_element_type=F32)


def _mm_tn(a, b):
    return lax.dot_general(_bf(a), _bf(b), (((0,), (0,)), ((), ())), preferred_element_type=F32)


def _sigmoid(a):
    return 1.0 / (1.0 + jnp.exp(-a))


def _gelu(a):
    return 0.5 * a * (1.0 + lax.erf(a * (2.0 ** -0.5)))


def _gelu_grad(a):
    return 0.5 * (1.0 + lax.erf(a * (2.0 ** -0.5))) + a * jnp.exp(-0.5 * a * a) * (1.0 / math.sqrt(2.0 * math.pi))


def _ln_fwd(a, g, b):
    mu = jnp.mean(a, axis=-1, keepdims=True)
    ac = a - mu
    var = jnp.mean(ac * ac, axis=-1, keepdims=True)
    rstd = lax.rsqrt(var + LN_EPS)
    xhat = ac * rstd
    return xhat * g + b, xhat, rstd


def _ln_bwd(dy, xhat, rstd, g):
    dxh = dy * g
    m1 = jnp.mean(dxh, axis=-1, keepdims=True)
    m2 = jnp.mean(dxh * xhat, axis=-1, keepdims=True)
    dx = rstd * (dxh - m1 - xhat * m2)
    return dx, jnp.sum(dy * xhat, axis=0, keepdims=True), jnp.sum(dy, axis=0, keepdims=True)


def _rms_fwd(a, g):
    r = lax.rsqrt(jnp.mean(a * a, axis=-1, keepdims=True) + RMS_EPS)
    xn = a * r
    return xn * g, xn, r


def _rms_bwd(dy, xn, r, g):
    dxn = dy * g
    dx = r * (dxn - xn * jnp.mean(dxn * xn, axis=-1, keepdims=True))
    return dx, jnp.sum(dy * xn, axis=0, keepdims=True)


def _tile_heads(a):
    return jnp.concatenate([a] * MLA_HEADS, axis=1)


def _row_call(name, body, n_rows, ts, tiled_in, full_in, weights, tiled_out, acc_out, reverse=False, scratch=(),
              tiled_out_t=()):
    n = n_rows // ts
    assert n * ts == n_rows, (name, n_rows, ts)
    nt, nf, nw = len(tiled_in), len(full_in), len(weights)
    nto, nao = len(tiled_out) + len(tiled_out_t), len(acc_out)
    if reverse:
        imap = lambda i: (n - 1 - i, 0)
    else:
        imap = lambda i: (i, 0)
    const = lambda i: (0, 0)

    def kern(*refs):
        ins = refs[:nt + nf]
        w_hbm = refs[nt + nf:nt + nf + nw]
        outs = refs[nt + nf + nw:nt + nf + nw + nto + nao]
        scr = refs[nt + nf + nw + nto + nao:]
        w_vmem = scr[:nw]
        extra = scr[nw + 1:] if nw else scr
        if nw:
            sem = scr[nw]

            @pl.when(pl.program_id(0) == 0)
            def _():
                cps = [pltpu.make_async_copy(w_hbm[k], w_vmem[k], sem.at[k]) for k in range(nw)]
                for cp in cps:
                    cp.start()
                for cp in cps:
                    cp.wait()
        body(*ins, *w_vmem, *outs, *extra)

    in_specs = [pl.BlockSpec((ts, a.shape[1]), imap) for a in tiled_in]
    in_specs += [pl.BlockSpec(a.shape, const) for a in full_in]
    in_specs += [pl.BlockSpec(memory_space=pl.ANY) for _ in weights]
    assert not (reverse and tiled_out_t)
    out_shape = [jax.ShapeDtypeStruct((n_rows, c), dt) for (c, dt) in tiled_out]
    out_shape += [jax.ShapeDtypeStruct((c, n_rows), dt) for (c, dt) in tiled_out_t]
    out_shape += [jax.ShapeDtypeStruct(shp, F32) for shp in acc_out]
    out_specs = [pl.BlockSpec((ts, c), imap) for (c, dt) in tiled_out]
    out_specs += [pl.BlockSpec((c, ts), lambda i: (0, i)) for (c, dt) in tiled_out_t]
    out_specs += [pl.BlockSpec(shp, const) for shp in acc_out]
    scratch_shapes = [pltpu.VMEM(w.shape, w.dtype) for w in weights]
    if nw:
        scratch_shapes.append(pltpu.SemaphoreType.DMA((nw,)))
    scratch_shapes += list(scratch)
    return pl.pallas_call(
        kern, name=name, grid=(n,), in_specs=in_specs, out_specs=out_specs, out_shape=out_shape,
        scratch_shapes=scratch_shapes,
        compiler_params=pltpu.CompilerParams(dimension_semantics=("arbitrary",), vmem_limit_bytes=VMEM_LIMIT),
    )(*tiled_in, *full_in, *weights)


def _mem_kv(mem, w_xk, w_xv):
    m = mem.shape[0]

    def body(mem_ref, wk_ref, wv_ref, k_ref, v_ref):
        mb = mem_ref[...]
        k_ref[...] = _mm(mb, wk_ref[...]).astype(BF16)
        v_ref[...] = _mm(mb, wv_ref[...]).astype(BF16)

    return pl.pallas_call(
        body, name="mem_kv",
        out_shape=[jax.ShapeDtypeStruct((m, D_MODEL), BF16)] * 2,
        compiler_params=pltpu.CompilerParams(vmem_limit_bytes=VMEM_LIMIT),
    )(mem, w_xk, w_xv)


def _in_fwd(x, tab, ln_g, ln_b, qg, kvg, pw, w_inx, w_b, w_q2, w_kv2):
    ts = min(ROW_TILE, x.shape[0])

    def body(x_ref, tab_ref, lng_ref, lnb_ref, qg_ref, kvg_ref, pw_ref, winx, wb, wq2, wkv2,
             h0_ref, z3_ref, gate_ref, h_ref, q_ref, k_ref, v_ref, kt_ref, vt_ref, bu_scr, carry):
        @pl.when(pl.program_id(0) == 0)
        def _():
            carry[...] = jnp.zeros_like(carry)

        h0, _, _ = _ln_fwd(x_ref[...], lng_ref[...], lnb_ref[...])
        h0_ref[...] = h0
        z = _mm(h0, winx[...])
        z3_ref[...] = z[:, :768]
        gate_ref[...] = z[:, 1024:].astype(BF16)
        bu_scr[...] = _mm(z[:, :256], wb[...])
        _scan_fwd_rows(pw_ref, bu_scr, h_ref, carry, ts // 8)
        tab_v = tab_ref[...]
        cq1, cq2, ck1 = tab_v[:, :128], tab_v[:, 128:], tab_v[:, :128]
        cqn, _, _ = _rms_fwd(z[:, 256:512], qg_ref[...])
        q2 = _mm(cqn, wq2[...])
        q_ref[...] = (q2[:, :1024] * _tile_heads(cq1) + q2[:, 1024:] * _tile_heads(cq2)).astype(BF16)
        ckvn, _, _ = _rms_fwd(z[:, 512:768], kvg_ref[...])
        kv2 = _mm(ckvn, wkv2[...])
        krp = z[:, 768:896] * ck1 + z[:, 896:1024] * cq2
        kf = kv2[:, :1024] + _tile_heads(krp)
        k_ref[...] = kf.astype(BF16)
        v_ref[...] = kv2[:, 1024:].astype(BF16)
        kt_ref[...] = kf.T.astype(BF16)
        vt_ref[...] = kv2[:, 1024:].T.astype(BF16)

    s = x.shape[0]
    return _row_call(
        "in_fwd", body, s, ts, [x, tab], [ln_g, ln_b, qg, kvg, pw], [w_inx, w_b, w_q2, w_kv2],
        [(1024, F32), (768, F32), (2048, BF16), (2048, F32), (1024, BF16), (1024, BF16), (1024, BF16)], [],
        tiled_out_t=[(1024, BF16), (1024, BF16)], scratch=[pltpu.VMEM((ts, 2 * NS), F32), pltpu.VMEM((8, 2 * NS), F32)])


def _scan_coeffs(pw_ref, rows, forward):
    row = lax.broadcasted_iota(jnp.int32, (8, NS), 0)
    out = []
    for k, r in zip((1, 2, 4), rows):
        keep = (row >= k) if forward else (row < 8 - k)
        out.append((jnp.where(keep, pw_ref[r:r + 1, :NS], 0.0), jnp.where(keep, pw_ref[r:r + 1, NS:], 0.0), k))
    return out


def _scan_level(xr, xi, coeff, forward):
    ar, ai, k = coeff
    shift = k if forward else 8 - k
    sr, si = pltpu.roll(xr, shift, 0), pltpu.roll(xi, shift, 0)
    return xr + ar * sr - ai * si, xi + ar * si + ai * sr


def _scan_fwd_rows(pw_ref, bu_ref, h_ref, carry, nblk):
    coeffs = _scan_coeffs(pw_ref, (0, 1, 3), True)

    def blk(b, _):
        r0 = pl.multiple_of(b * 8, 8)
        xr = bu_ref[pl.ds(r0, 8), :NS]
        xi = bu_ref[pl.ds(r0, 8), NS:]
        for coeff in coeffs:
            xr, xi = _scan_level(xr, xi, coeff, True)
        cr, ci = carry[7:8, :NS], carry[7:8, NS:]
        pr, pi = pw_ref[:, :NS], pw_ref[:, NS:]
        hr = xr + pr * cr - pi * ci
        hi = xi + pr * ci + pi * cr
        h_ref[pl.ds(r0, 8), :NS] = hr
        h_ref[pl.ds(r0, 8), NS:] = hi
        carry[:, :NS] = hr
        carry[:, NS:] = hi
        return 0

    lax.fori_loop(0, nblk, blk, 0)


def _scan_bwd_rows(pw_ref, g_ref, h_ref, lam_ref, carry, acc, nblk):
    row = lax.broadcasted_iota(jnp.int32, (8, NS), 0)
    coeffs = _scan_coeffs(pw_ref, (7, 6, 4), False)

    def blk(bb, _):
        r0 = pl.multiple_of((nblk - 1 - bb) * 8, 8)
        xr = g_ref[pl.ds(r0, 8), :NS]
        xi = g_ref[pl.ds(r0, 8), NS:]
        for coeff in coeffs:
            xr, xi = _scan_level(xr, xi, coeff, False)
        cr, ci = carry[0:1, :NS], carry[0:1, NS:]
        pr, pi = pw_ref[:, :NS], pw_ref[:, NS:]
        lr = xr + pr * cr - pi * ci
        li = xi + pr * ci + pi * cr
        lam_ref[pl.ds(r0, 8), :NS] = lr
        lam_ref[pl.ds(r0, 8), NS:] = li
        nr = jnp.where(row < 7, pltpu.roll(lr, 7, 0), cr)
        ni = jnp.where(row < 7, pltpu.roll(li, 7, 0), ci)
        hr = h_ref[pl.ds(r0, 8), :NS]
        hi = h_ref[pl.ds(r0, 8), NS:]
        acc[:, :NS] += nr * hr + ni * hi
        acc[:, NS:] += ni * hr - nr * hi
        carry[:, :NS] = lr
        carry[:, NS:] = li
        return 0

    lax.fori_loop(0, nblk, blk, 0)


def _s5out_fwd(h, z3, d_skip, w_c, w_glu):
    def body(h_ref, z3_ref, d_ref, wc, wglu, yl_ref, so_ref):
        yl = _mm(h_ref[...], wc[...]) + d_ref[...] * z3_ref[:, :256]
        yl_ref[...] = yl
        y12 = _mm(_gelu(yl), wglu[...])
        so_ref[...] = y12[:, :1024] * _sigmoid(y12[:, 1024:])

    s = h.shape[0]
    return _row_call("s5out_fwd", body, s, min(ROW_TILE, s), [h, z3], [d_skip], [w_c, w_glu],
                     [(256, F32), (1024, F32)], [])


def _causal_mask_t(t):
    row = lax.broadcasted_iota(jnp.int32, (t, t), 0)
    col = lax.broadcasted_iota(jnp.int32, (t, t), 1)
    return row <= col


def _attn_fwd(q, k, vt):
    s = q.shape[0]
    t = min(ATT_TILE, s)
    nq = s // t
    scale = MLA_QK ** -0.5
    c2 = scale * LOG2E

    def body(q_ref, k_ref, vt_ref, o_ref, lse_ref, s0, s1):
        i = pl.program_id(1)
        qb = q_ref[...]

        def scores(kb, dst):
            st = _mm_nt(k_ref[pl.ds(pl.multiple_of(kb * t, t), t), :], qb)
            dst[...] = st
            return jnp.max(st, axis=0, keepdims=True)

        def update(kb, src, mt, state, masked):
            m, l, acc = state
            if masked:
                keep = _causal_mask_t(t)
                mt = jnp.max(jnp.where(keep, src[...], NEG_INF), axis=0, keepdims=True)
            m_new = jnp.maximum(m, mt)
            if masked:
                p = jnp.exp2((jnp.where(keep, src[...], NEG_INF) - m_new) * c2)
            else:
                p = jnp.exp2((src[...] - m_new) * c2)
            a = jnp.exp2((m - m_new) * c2)
            l = a * l + jnp.sum(p, axis=0, keepdims=True)
            acc = a * acc + _mm(vt_ref[:, pl.ds(pl.multiple_of(kb * t, t), t)], p)
            return m_new, l, acc

        def pair(jj, carry):
            mt_a, state = carry
            mt_b = scores(2 * jj + 1, s1)
            state = update(2 * jj, s0, mt_a, state, False)
            mt_a = scores(2 * jj + 2, s0)
            return mt_a, update(2 * jj + 1, s1, mt_b, state, False)

        def odd_tail(carry):
            mt_a, state = carry
            mt_b = scores(i, s1)
            state = update(i - 1, s0, mt_a, state, False)
            return update(i, s1, mt_b, state, True)

        def even_tail(carry):
            mt_a, state = carry
            return update(i, s0, mt_a, state, True)

        init = (jnp.full((1, t), NEG_INF, F32), jnp.zeros((1, t), F32), jnp.zeros((HEAD_PAD, t), F32))
        carry = lax.fori_loop(0, i // 2, pair, (scores(0, s0), init))
        m, l, acc = lax.cond(i % 2 == 1, odd_tail, even_tail, carry)
        o_ref[...] = (acc / l).T
        lse_ref[...] = m * scale + jnp.log(l)

    return pl.pallas_call(
        body, name="attn_fwd", grid=(MLA_HEADS, nq),
        in_specs=[pl.BlockSpec((t, HEAD_PAD), lambda h, i: (i, h)),
                  pl.BlockSpec((s, HEAD_PAD), lambda h, i: (0, h)),
                  pl.BlockSpec((HEAD_PAD, s), lambda h, i: (h, 0))],
        out_specs=[pl.BlockSpec((t, HEAD_PAD), lambda h, i: (i, h)),
                   pl.BlockSpec((None, 1, t), lambda h, i: (h, 0, i))],
        out_shape=[jax.ShapeDtypeStruct((s, MLA_HEADS * HEAD_PAD), F32),
                   jax.ShapeDtypeStruct((MLA_HEADS, 1, s), F32)],
        scratch_shapes=[pltpu.VMEM((t, t), F32)] * 2,
        compiler_params=pltpu.CompilerParams(dimension_semantics=("arbitrary", "arbitrary"),
                                             vmem_limit_bytes=VMEM_LIMIT),
    )(q, k, vt)


def _attn_bwd(q, k, v, kt, do, lse_t, dl_t):
    s = q.shape[0]
    t = min(ATT_TILE, s)
    nq = s // t
    scale = MLA_QK ** -0.5
    c2 = scale * LOG2E

    def body(q_ref, k_ref, v_ref, kt_ref, do_ref, lse_ref, dl_ref, dq_ref, dk_ref, dv_ref, dqt, s0, p0, s1, p1):
        j = pl.program_id(1)
        n = nq - 1 - j

        @pl.when(j == 0)
        def _():
            dqt[...] = jnp.zeros_like(dqt)

        kk = k_ref[...]
        vv = v_ref[...]
        ktb = kt_ref[:MLA_QK, :]

        def rows(m):
            return pl.ds(pl.multiple_of(jnp.where(m < n, j + 1 + m, j) * t, t), t)

        def first(m, sbuf, pbuf):
            r = rows(m)
            sbuf[...] = _mm_nt(kk, q_ref[r, :])
            pbuf[...] = _mm_nt(vv, do_ref[r, :])

        def finish(m, sbuf, pbuf, acc, masked):
            dk, dv = acc
            r = rows(m)
            pt = jnp.exp2(sbuf[...] * c2 - lse_ref[:, r] * LOG2E)
            if masked:
                pt = jnp.where(_causal_mask_t(t), pt, 0.0)
            dv = dv + _mm(pt, do_ref[r, :])
            dst = (pt * (pbuf[...] - dl_ref[:, r])).astype(BF16)
            dk = dk + _mm(dst, q_ref[r, :])
            dqt[:MLA_QK, r] += _mm(ktb, dst)
            return dk, dv

        def pair(jj, acc):
            first(2 * jj + 1, s1, p1)
            acc = finish(2 * jj, s0, p0, acc, False)
            first(2 * jj + 2, s0, p0)
            return finish(2 * jj + 1, s1, p1, acc, False)

        def odd_tail(acc):
            first(n, s1, p1)
            acc = finish(n - 1, s0, p0, acc, False)
            return finish(n, s1, p1, acc, True)

        def even_tail(acc):
            return finish(n, s0, p0, acc, True)

        zero = jnp.zeros((t, HEAD_PAD), F32)
        first(0, s0, p0)
        acc = lax.fori_loop(0, n // 2, pair, (zero, zero))
        dk, dv = lax.cond(n % 2 == 1, odd_tail, even_tail, acc)
        dk_ref[...] = (dk * scale).astype(BF16)
        dv_ref[...] = dv.astype(BF16)

        @pl.when(j == nq - 1)
        def _():
            for cc in range(nq):
                dq_ref[cc * t:(cc + 1) * t, :] = (dqt[:, cc * t:(cc + 1) * t].T * scale).astype(BF16)

    full = pl.BlockSpec((s, HEAD_PAD), lambda h, j: (0, h))
    tile = pl.BlockSpec((t, HEAD_PAD), lambda h, j: (j, h))
    stat = pl.BlockSpec((None, 1, s), lambda h, j: (h, 0, 0))
    return pl.pallas_call(
        body, name="attn_bwd", grid=(MLA_HEADS, nq),
        in_specs=[full, tile, tile, pl.BlockSpec((HEAD_PAD, t), lambda h, j: (h, j)), full, stat, stat],
        out_specs=[full, tile, tile],
        out_shape=[jax.ShapeDtypeStruct((s, MLA_HEADS * HEAD_PAD), BF16)] * 3,
        scratch_shapes=[pltpu.VMEM((HEAD_PAD, s), F32)] + [pltpu.VMEM((t, t), F32)] * 4,
        compiler_params=pltpu.CompilerParams(dimension_semantics=("arbitrary", "arbitrary"),
                                             vmem_limit_bytes=VMEM_LIMIT),
    )(q, k, v, kt, do, lse_t, dl_t)


def _xattn_probs(qxb, memk, hh):
    sl = slice(hh * XATTN_HD, (hh + 1) * XATTN_HD)
    sc = _mm_nt(qxb[:, sl], memk[:, sl]) * (XATTN_HD ** -0.5)
    e = jnp.exp(sc - jnp.max(sc, axis=1, keepdims=True))
    return e / jnp.sum(e, axis=1, keepdims=True)


def _post_forward(o, s_out, gate, h0, memk, memv, w_oa, w_o, w_xq, w_xo, g1, b1, g2, b2):
    a_out = _mm(o, w_oa)
    sg_s = _sigmoid(gate[:, :1024])
    sg_a = _sigmoid(gate[:, 1024:])
    mixin = sg_s * s_out + sg_a * a_out
    r1 = DN_ALPHA * h0 + _mm(mixin, w_o)
    h1, xh1, rs1 = _ln_fwd(r1, g1, b1)
    qxb = _mm(h1, w_xq).astype(BF16)
    ox = jnp.concatenate([_mm(_xattn_probs(qxb, memk, hh), memv[:, hh * XATTN_HD:(hh + 1) * XATTN_HD])
                          for hh in range(XATTN_HEADS)], axis=1)
    r2 = DN_ALPHA * h1 + _mm(ox, w_xo)
    return dict(a_out=a_out, mixin=mixin, r1=r1, h1=h1, qxb=qxb, ox=ox, r2=r2)


def _post_fwd(o, s_out, gate, h0, memk, memv, lns, w_oa, w_o, w_xq, w_xo):
    def body(o_ref, so_ref, gate_ref, h0_ref, mk_ref, mv_ref, g1, b1, g2, b2, woa, wo, wxq, wxo,
             r2_ref, mixin_ref, h1_ref, ox_ref, aout_ref, r1_ref, qx_ref):
        f = _post_forward(o_ref[...], so_ref[...], gate_ref[...].astype(F32), h0_ref[...], mk_ref[...], mv_ref[...],
                          woa[...], wo[...], wxq[...], wxo[...], g1[...], b1[...], g2[...], b2[...])
        r2_ref[...] = f["r2"]
        mixin_ref[...] = f["mixin"].astype(BF16)
        h1_ref[...] = f["h1"].astype(BF16)
        ox_ref[...] = f["ox"].astype(BF16)
        aout_ref[...] = f["a_out"]
        r1_ref[...] = f["r1"]
        qx_ref[...] = f["qxb"]

    s = o.shape[0]
    return _row_call("post_fwd", body, s, min(ROW_TILE, s), [o, s_out, gate, h0], [memk, memv, *lns],
                     [w_oa, w_o, w_xq, w_xo],
                     [(1024, F32), (1024, BF16), (1024, BF16), (1024, BF16), (1024, F32), (1024, F32), (1024, BF16)], [])


def _mlp(r2, target, g2, b2, g3, b3, w_up, w_down):
    cw = 1024
    n_chunk = MLP_HIDDEN // cw

    def body(r2_ref, tgt_ref, g2_ref, b2_ref, g3_ref, b3_ref, wup, wdn,
             dh2_ref, h2b_ref, dup_ref, act_ref, dff_ref, loss_ref, dg3_ref, db3_ref, up_scr):
        @pl.when(pl.program_id(0) == 0)
        def _():
            loss_ref[...] = jnp.zeros_like(loss_ref)
            dg3_ref[...] = jnp.zeros_like(dg3_ref)
            db3_ref[...] = jnp.zeros_like(db3_ref)

        h2, _, _ = _ln_fwd(r2_ref[...], g2_ref[...], b2_ref[...])
        h2b = h2.astype(BF16)
        h2b_ref[...] = h2b
        ff = jnp.zeros(h2.shape, F32)
        for c in range(n_chunk):
            sl = slice(c * cw, (c + 1) * cw)
            a = jnp.maximum(_mm(h2b, wup[:, sl]), 0.0)
            up_scr[:, sl] = a
            actb = (a * a).astype(BF16)
            act_ref[:, sl] = actb
            ff = ff + _mm(actb, wdn[sl, :])
        h3, xh3, rs3 = _ln_fwd(DN_ALPHA * h2 + ff, g3_ref[...], b3_ref[...])
        err = h3 - tgt_ref[...]
        loss_ref[...] += 0.5 * jnp.sum(err * err) * (1.0 / D_MODEL)
        dr3, dg3, db3 = _ln_bwd(err * (1.0 / D_MODEL), xh3, rs3, g3_ref[...])
        dg3_ref[...] += dg3
        db3_ref[...] += db3
        dffb = dr3.astype(BF16)
        dff_ref[...] = dffb
        dh2 = DN_ALPHA * dr3
        for c in range(n_chunk):
            sl = slice(c * cw, (c + 1) * cw)
            dupb = (_mm_nt(dffb, wdn[sl, :]) * (2.0 * up_scr[:, sl])).astype(BF16)
            dup_ref[:, sl] = dupb
            dh2 = dh2 + _mm_nt(dupb, wup[:, sl])
        dh2_ref[...] = dh2

    s = r2.shape[0]
    ts = min(ROW_TILE, s)
    return _row_call("mlp", body, s, ts, [r2, target], [g2, b2, g3, b3], [w_up, w_down],
                     [(1024, F32), (1024, BF16), (MLP_HIDDEN, BF16), (MLP_HIDDEN, BF16), (1024, BF16)],
                     [(8, 128), (1, 1024), (1, 1024)], scratch=[pltpu.VMEM((ts, MLP_HIDDEN), F32)])


def _post_bwd(dh2, o, s_out, gate, a_out, r1, r2, qx, memk, memv, lns, w_oa, w_o, w_xq, w_xo):
    def body(dh2_ref, o_ref, so_ref, gate_ref, aout_ref, r1_ref, r2_ref, qx_ref, mk_ref, mv_ref, g1, b1, g2, b2,
             woa, wo, wxq, wxo,
             do_ref, dl_ref, dso_ref, dgate_ref, dh0_ref, daout_ref, dmix_ref, dqx_ref, dxa_ref,
             dmk_ref, dmv_ref, dg1_ref, db1_ref, dg2_ref, db2_ref):
        @pl.when(pl.program_id(0) == 0)
        def _():
            for r in (dmk_ref, dmv_ref, dg1_ref, db1_ref, dg2_ref, db2_ref):
                r[...] = jnp.zeros_like(r)

        o = o_ref[...]
        s_out = so_ref[...]
        memk, memv = mk_ref[...], mv_ref[...]
        qxb = qx_ref[...]
        _, xh1, rs1 = _ln_fwd(r1_ref[...], g1[...], b1[...])
        _, xh2, rs2 = _ln_fwd(r2_ref[...], g2[...], b2[...])
        dr2, dg2, db2 = _ln_bwd(dh2_ref[...], xh2, rs2, g2[...])
        dg2_ref[...] += dg2
        db2_ref[...] += db2
        dxab = dr2.astype(BF16)
        dxa_ref[...] = dxab
        dox = _mm_nt(dxab, wxo[...])
        dqs = []
        for hh in range(XATTN_HEADS):
            sl = slice(hh * XATTN_HD, (hh + 1) * XATTN_HD)
            p = _xattn_probs(qxb, memk, hh)
            doxh = dox[:, sl].astype(BF16)
            dp = _mm_nt(doxh, memv[:, sl])
            ds = (p * (dp - jnp.sum(dp * p, axis=1, keepdims=True)) * (XATTN_HD ** -0.5)).astype(BF16)
            dqs.append(_mm(ds, memk[:, sl]))
            dmk_ref[:, sl] += _mm_tn(ds, qxb[:, sl])
            dmv_ref[:, sl] += _mm_tn(p, doxh)
        dqxb = jnp.concatenate(dqs, axis=1).astype(BF16)
        dqx_ref[...] = dqxb
        dh1 = DN_ALPHA * dr2 + _mm_nt(dqxb, wxq[...])
        dr1, dg1, db1 = _ln_bwd(dh1, xh1, rs1, g1[...])
        dg1_ref[...] += dg1
        db1_ref[...] += db1
        dh0_ref[...] = DN_ALPHA * dr1
        dmixb = dr1.astype(BF16)
        dmix_ref[...] = dmixb
        dmixin = _mm_nt(dmixb, wo[...])
        sg_s = _sigmoid(gate_ref[:, :1024].astype(F32))
        sg_a = _sigmoid(gate_ref[:, 1024:].astype(F32))
        dso_ref[...] = dmixin * sg_s
        daoutb = (dmixin * sg_a).astype(BF16)
        daout_ref[...] = daoutb
        dgate_ref[:, :1024] = (dmixin * s_out * sg_s * (1.0 - sg_s)).astype(BF16)
        dgate_ref[:, 1024:] = (dmixin * aout_ref[...] * sg_a * (1.0 - sg_a)).astype(BF16)
        d_o = _mm_nt(daoutb, woa[...])
        do_ref[...] = d_o.astype(BF16)
        lane = lax.broadcasted_iota(jnp.int32, (o.shape[0], HEAD_PAD), 1)
        dlc = jnp.zeros((o.shape[0], HEAD_PAD), F32)
        for hh in range(MLA_HEADS):
            sl = slice(hh * HEAD_PAD, (hh + 1) * HEAD_PAD)
            dl = jnp.sum(d_o[:, sl] * o[:, sl], axis=1, keepdims=True)
            dlc = dlc + jnp.where(lane == hh, dl, 0.0)
        dl_ref[...] = dlc

    s = o.shape[0]
    m = memk.shape[0]
    return _row_call(
        "post_bwd", body, s, min(ROW_TILE, s), [dh2, o, s_out, gate, a_out, r1, r2, qx], [memk, memv, *lns],
        [w_oa, w_o, w_xq, w_xo],
        [(1024, BF16), (HEAD_PAD, F32), (1024, F32), (2048, BF16), (1024, F32),
         (1024, BF16), (1024, BF16), (1024, BF16), (1024, BF16)],
        [(m, 1024), (m, 1024), (1, 1024), (1, 1024), (1, 1024), (1, 1024)])


def _s5out_bwd(d_so, yl, z3, h, d_skip, pwb, w_c, w_glu):
    s = d_so.shape[0]
    ts = min(ROW_TILE, s)
    n_tiles = s // ts

    def body(dso_ref, yl_ref, z3_ref, h_ref, d_ref, pw_ref, wc, wglu,
             lam_ref, dup_ref, yg_ref, dy12_ref, dyl_ref, dd_ref, da_ref, gh_scr, carry, acc):
        @pl.when(pl.program_id(0) == 0)
        def _():
            dd_ref[...] = jnp.zeros_like(dd_ref)
            carry[...] = jnp.zeros_like(carry)
            acc[...] = jnp.zeros_like(acc)

        yl = yl_ref[...]
        ygb = _gelu(yl).astype(BF16)
        yg_ref[...] = ygb
        y12 = _mm(ygb, wglu[...])
        sg = _sigmoid(y12[:, 1024:])
        dso = dso_ref[...]
        dy12b = jnp.concatenate([dso * sg, dso * y12[:, :1024] * sg * (1.0 - sg)], axis=1).astype(BF16)
        dy12_ref[...] = dy12b
        dyl = _mm_nt(dy12b, wglu[...]) * _gelu_grad(yl)
        dylb = dyl.astype(BF16)
        dyl_ref[...] = dylb
        gh_scr[...] = _mm_nt(dylb, wc[...])
        _scan_bwd_rows(pw_ref, gh_scr, h_ref, lam_ref, carry, acc, ts // 8)
        dup_ref[...] = dyl * d_ref[...]
        dd_ref[...] += jnp.sum(dyl * z3_ref[:, :256], axis=0, keepdims=True)

        @pl.when(pl.program_id(0) == n_tiles - 1)
        def _():
            da_ref[...] = jnp.sum(acc[...], axis=0, keepdims=True)

    return _row_call("s5out_bwd", body, s, ts, [d_so, yl, z3, h], [d_skip, pwb], [w_c, w_glu],
                     [(2 * NS, F32), (256, F32), (256, BF16), (2048, BF16), (256, BF16)], [(1, 256), (1, 2 * NS)],
                     reverse=True, scratch=[pltpu.VMEM((ts, 2 * NS), F32), pltpu.VMEM((8, 2 * NS), F32),
                                            pltpu.VMEM((8, 2 * NS), F32)])


def _in_bwd(x, tab, z3, dq, dk, dv, lam, du_p, dgate, dh0p, ln_g, ln_b, qg, kvg, w_inx, w_b, w_q2, w_kv2):
    def body(x_ref, tab_ref, z3_ref, dq_ref, dk_ref, dv_ref, lam_ref, dup_ref, dgate_ref, dh0p_ref,
             lng_ref, lnb_ref, qg_ref, kvg_ref, winx, wb, wq2, wkv2,
             dx_ref, h0b_ref, dz_ref, ub_ref, cqn_ref, ckvn_ref, dq2_ref, dkv2_ref,
             dlng_ref, dlnb_ref, dqg_ref, dkvg_ref):
        @pl.when(pl.program_id(0) == 0)
        def _():
            for r in (dlng_ref, dlnb_ref, dqg_ref, dkvg_ref):
                r[...] = jnp.zeros_like(r)

        h0, xh0, rs0 = _ln_fwd(x_ref[...], lng_ref[...], lnb_ref[...])
        h0b_ref[...] = h0.astype(BF16)
        z3 = z3_ref[...]
        ub_ref[...] = z3[:, :256].astype(BF16)
        tab_v = tab_ref[...]
        cq1, cq2, ck1 = tab_v[:, :128], tab_v[:, 128:], tab_v[:, :128]
        du = _mm_nt(lam_ref[...], wb[...]) + dup_ref[...]
        dq = dq_ref[...].astype(F32)
        dq2b = jnp.concatenate([dq * _tile_heads(cq1), dq * _tile_heads(cq2)], axis=1).astype(BF16)
        dq2_ref[...] = dq2b
        cqg, cqn, rq = _rms_fwd(z3[:, 256:512], qg_ref[...])
        cqn_ref[...] = cqg.astype(BF16)
        dcq, dqg = _rms_bwd(_mm_nt(dq2b, wq2[...]), cqn, rq, qg_ref[...])
        dqg_ref[...] += dqg
        dkv2b = jnp.concatenate([dk_ref[...], dv_ref[...]], axis=1)
        dk = dk_ref[...].astype(F32)
        dkv2_ref[...] = dkv2b
        ckvg, ckvn, rkv = _rms_fwd(z3[:, 512:768], kvg_ref[...])
        ckvn_ref[...] = ckvg.astype(BF16)
        dckv, dkvg = _rms_bwd(_mm_nt(dkv2b, wkv2[...]), ckvn, rkv, kvg_ref[...])
        dkvg_ref[...] += dkvg
        dkrp = dk[:, :HEAD_PAD]
        for hh in range(1, MLA_HEADS):
            dkrp = dkrp + dk[:, hh * HEAD_PAD:(hh + 1) * HEAD_PAD]
        dzb = jnp.concatenate([a.astype(BF16) for a in (du, dcq, dckv, dkrp * ck1, dkrp * cq2)] + [dgate_ref[...]],
                              axis=1)
        dz_ref[...] = dzb
        dh0 = _mm_nt(dzb, winx[...]) + dh0p_ref[...]
        dx, dg, db = _ln_bwd(dh0, xh0, rs0, lng_ref[...])
        dx_ref[...] = dx
        dlng_ref[...] += dg
        dlnb_ref[...] += db

    s = x.shape[0]
    return _row_call(
        "in_bwd", body, s, min(ROW_TILE, s), [x, tab, z3, dq, dk, dv, lam, du_p, dgate, dh0p],
        [ln_g, ln_b, qg, kvg], [w_inx, w_b, w_q2, w_kv2],
        [(1024, F32), (1024, BF16), (3072, BF16), (256, BF16), (256, BF16), (256, BF16), (2048, BF16), (2048, BF16)],
        [(1, 1024), (1, 1024), (1, 256), (1, 256)])


def _wgrad(name, xs, dy, out_dtype=F32, blocks=1, after=None):
    s, k = xs.shape
    n = dy.shape[1]
    nb = n // blocks
    tk = min(k, 1024)
    tn = max(d for d in range(128, min(n, 2048) + 1, 128) if n % d == 0)
    ts = min(s, 2048)
    per_tile = tn // nb
    assert s % ts == 0 and k % tk == 0 and n % tn == 0 and (blocks == 1 or tn % nb == 0), (name, s, k, n)
    last = s // ts - 1

    def body(x_ref, dy_ref, *rest):
        o_ref, acc = rest[-2:]

        @pl.when(pl.program_id(2) == 0)
        def _():
            acc[...] = jnp.zeros_like(acc)

        acc[...] += _mm_tn(x_ref[...], dy_ref[...])

        @pl.when(pl.program_id(2) == last)
        def _():
            if blocks == 1:
                o_ref[...] = acc[...].astype(out_dtype)
            else:
                for d in range(per_tile):
                    o_ref[d] = acc[:, d * nb:(d + 1) * nb].astype(out_dtype)

    if blocks == 1:
        out_shape = jax.ShapeDtypeStruct((k, n), out_dtype)
        out_spec = pl.BlockSpec((tk, tn), lambda a, b, c: (a, b))
    else:
        out_shape = jax.ShapeDtypeStruct((blocks, k, nb), out_dtype)
        out_spec = pl.BlockSpec((per_tile, tk, nb), lambda a, b, c: (b, a, 0))
    return pl.pallas_call(
        body, name=name, grid=(k // tk, n // tn, s // ts),
        in_specs=[pl.BlockSpec((ts, tk), lambda a, b, c: (c, a)), pl.BlockSpec((ts, tn), lambda a, b, c: (c, b))]
        + ([] if after is None else [pl.BlockSpec(memory_space=pl.ANY)]),
        out_specs=out_spec, out_shape=out_shape, scratch_shapes=[pltpu.VMEM((tk, tn), F32)],
        compiler_params=pltpu.CompilerParams(dimension_semantics=("arbitrary", "arbitrary", "arbitrary"),
                                             vmem_limit_bytes=VMEM_LIMIT),
    )(xs, dy, *([] if after is None else [after]))


def _mesh_pos():
    x, y, c = lax.axis_index("x"), lax.axis_index("y"), lax.axis_index("c")
    return x, y, c


def _peer(x, y, c, k):
    px = 1 - x if k & 4 else x
    py = 1 - y if k & 2 else y
    pc = 1 - c if k & 1 else c
    return (px, py, pc), 4 * px + 2 * py + pc


def _all_gather_two_level(name, arrays):
    n = len(arrays)

    def body(*refs):
        src, dst = refs[:n], refs[n:2 * n]
        send_sems, recv_sems, local_sems = refs[2 * n:]
        x, y, c = _mesh_pos()
        me = 4 * x + 2 * y + c
        sibling = (x, y, 1 - c)

        def copy(i, k, source, slot, to):
            return pltpu.make_async_remote_copy(src_ref=source, dst_ref=dst[i].at[slot],
                                                send_sem=send_sems.at[(k - 1) * n + i],
                                                recv_sem=recv_sems.at[(k - 1) * n + i],
                                                device_id=to, device_id_type=MESH)

        own = [pltpu.make_async_copy(src[i], dst[i].at[me], local_sems.at[i]) for i in range(n)]
        for cp in own:
            cp.start()
        sends = []
        for k in (1, 2, 4, 6):
            peer, _ = _peer(x, y, c, k)
            for i in range(n):
                sends.append(copy(i, k, src[i], me, peer))
                sends[-1].start()
        for k in (2, 4, 6):
            peer, pid = _peer(x, y, c, k)
            for i in range(n):
                copy(i, k, src[i], pid, peer).wait_recv()
                sends.append(copy(i, k + 1, dst[i].at[pid], pid, sibling))
                sends[-1].start()
        for k in (1, 3, 5, 7):
            peer, pid = _peer(x, y, c, k)
            for i in range(n):
                copy(i, k, src[i], pid, peer).wait_recv()
        for cp in sends:
            cp.wait_send()
        for cp in own:
            cp.wait()

    n_sem = n * (N_DEV - 1)
    return pl.pallas_call(
        body, name=name,
        in_specs=[pl.BlockSpec(memory_space=pl.ANY)] * n,
        out_specs=[pl.BlockSpec(memory_space=pl.ANY)] * n,
        out_shape=[jax.ShapeDtypeStruct((N_DEV,) + a.shape, a.dtype) for a in arrays],
        scratch_shapes=[pltpu.SemaphoreType.DMA((n_sem,)), pltpu.SemaphoreType.DMA((n_sem,)),
                        pltpu.SemaphoreType.DMA((n,))],
    )(*arrays)


_HBM = pl.BlockSpec(memory_space=pltpu.HBM)
_SEM = pl.BlockSpec(memory_space=pltpu.SEMAPHORE)


def _exchange_start(name, arrays, gather, after=None):
    n = len(arrays)
    n_sem = n * (N_DEV - 1)
    me = 4 * lax.axis_index("x") + 2 * lax.axis_index("y") + lax.axis_index("c")
    lands = []
    for a in arrays:
        own = a[None] if gather else lax.dynamic_slice_in_dim(a, me, 1, 0)
        lands.append(lax.dynamic_update_slice(lax.empty((N_DEV,) + a.shape[-2:], a.dtype), own, (me, 0, 0)))
    n_after = 0 if after is None else 1

    def body(*refs):
        src, land = refs[:n], refs[n:2 * n]
        send_sems, recv_sems = refs[2 * n + n_after], refs[2 * n + n_after + 1]
        token = refs[-1]
        x, y, c = _mesh_pos()
        me_in = 4 * x + 2 * y + c
        for k in range(1, N_DEV):
            peer, pid = _peer(x, y, c, k)
            for i in range(n):
                idx = (k - 1) * n + i
                pltpu.make_async_remote_copy(src_ref=src[i] if gather else src[i].at[pid], dst_ref=land[i].at[me_in],
                                             send_sem=send_sems.at[idx], recv_sem=recv_sems.at[idx],
                                             device_id=peer, device_id_type=MESH).start()
        token[...] = jnp.zeros_like(token)

    operands = [pltpu.with_memory_space_constraint(a, pltpu.HBM) for a in list(arrays) + lands]
    outs = pl.pallas_call(
        body, name=name,
        out_shape=(pltpu.SemaphoreType.DMA((n_sem,)), pltpu.SemaphoreType.DMA((n_sem,)),
                   *[pltpu.HBM(a.shape, a.dtype) for a in list(arrays) + lands],
                   jax.ShapeDtypeStruct((8, 128), F32)),
        in_specs=[_HBM] * (2 * n) + [pl.BlockSpec(memory_space=pl.ANY)] * n_after,
        out_specs=(_SEM, _SEM, *[_HBM] * (2 * n), pl.BlockSpec(memory_space=pltpu.VMEM)),
        input_output_aliases={i: 2 + i for i in range(2 * n)},
        compiler_params=pltpu.CompilerParams(has_side_effects=pltpu.SideEffectType.DATAFLOW_SIDE_EFFECTING),
    )(*operands, *([after] if n_after else []))
    return (gather, outs[0], outs[1], outs[2:2 + n], outs[2 + n:2 + 2 * n]), outs[-1]


def _exchange_wait(name, handle, after):
    gather, send_sems, recv_sems, srcs, lands = handle
    n = len(srcs)

    def body(*refs):
        src, land = refs[:n], refs[n:2 * n]
        s_sems, r_sems = refs[2 * n], refs[2 * n + 1]
        x, y, c = _mesh_pos()
        for k in range(1, N_DEV):
            peer, pid = _peer(x, y, c, k)
            for i in range(n):
                idx = (k - 1) * n + i
                cp = pltpu.make_async_remote_copy(src_ref=src[i] if gather else src[i].at[pid], dst_ref=land[i].at[pid],
                                                  send_sem=s_sems.at[idx], recv_sem=r_sems.at[idx],
                                                  device_id=peer, device_id_type=MESH)
                cp.wait_send()
                cp.wait_recv()

    outs = pl.pallas_call(
        body, name=name,
        out_shape=tuple(pltpu.HBM(a.shape, a.dtype) for a in list(srcs) + list(lands)),
        in_specs=[_HBM] * (2 * n) + [_SEM, _SEM, pl.BlockSpec(memory_space=pl.ANY)],
        out_specs=tuple([_HBM] * (2 * n)),
        input_output_aliases={i: i for i in range(2 * n)},
        compiler_params=pltpu.CompilerParams(has_side_effects=pltpu.SideEffectType.DATAFLOW_SIDE_EFFECTING),
    )(*srcs, *lands, send_sems, recv_sems, after)
    return list(outs[n:])


def _adamw(name, parts, w, m, v):
    a_rows, b_cols = w.shape
    ta = min(a_rows, ADAM_TILE)
    assert a_rows % ta == 0
    c1 = 1.0 - ADAM_B1 ** ADAM_STEP
    c2 = 1.0 - ADAM_B2 ** ADAM_STEP

    def body(p_ref, w_ref, m_ref, v_ref, g_ref, d_ref, mo_ref, vo_ref):
        g = p_ref[0].astype(F32)
        for d in range(1, N_DEV):
            g = g + p_ref[d].astype(F32)
        g_ref[...] = g
        mn = ADAM_B1 * m_ref[...] + (1.0 - ADAM_B1) * g
        vn = ADAM_B2 * v_ref[...] + (1.0 - ADAM_B2) * (g * g)
        mo_ref[...] = mn
        vo_ref[...] = vn
        d_ref[...] = -ADAM_LR * ((mn / c1) / (jnp.sqrt(vn / c2) + ADAM_EPS) + ADAM_WD * w_ref[...])

    row = pl.BlockSpec((ta, b_cols), lambda i: (i, 0))
    return pl.pallas_call(
        body, name=name, grid=(a_rows // ta,),
        in_specs=[pl.BlockSpec((N_DEV, ta, b_cols), lambda i: (0, i, 0)), row, row, row],
        out_specs=[row] * 4,
        out_shape=[jax.ShapeDtypeStruct((a_rows, b_cols), F32)] * 4,
        compiler_params=pltpu.CompilerParams(dimension_semantics=("arbitrary",), vmem_limit_bytes=VMEM_LIMIT),
    )(parts, w, m, v)


def _pack_rows(arrays, rows):
    flat = jnp.concatenate([a.reshape(-1) for a in arrays])
    return jnp.pad(flat, (0, rows * LANES - flat.shape[0])).reshape(rows, LANES)


def _cols_from_blocks(g):
    return g.transpose(1, 0, 2).reshape(g.shape[1], N_DEV * g.shape[2])


def _blocks_from_cols(w):
    return w.reshape(w.shape[0], N_DEV, w.shape[1] // N_DEV).transpose(1, 0, 2)


def _unpack_flat(flat2d, shapes):
    flat = flat2d.reshape(-1)
    out, off = [], 0
    for shp in shapes:
        sz = math.prod(shp)
        out.append(flat[off:off + sz].reshape(shp))
        off += sz
    return out


def _s5_mats(lam_re, lam_im, log_dt, b_re, b_im, c_re, c_im):
    lr = jnp.minimum(lam_re, S5_MAX_RE)
    li = lam_im
    dt = jnp.exp(log_dt)[:, None]
    mag = jnp.exp(lr * dt)
    ang = li * dt
    ab_re = mag * jnp.cos(ang)
    ab_im = mag * jnp.sin(ang)
    den = lr * lr + li * li
    nr = ab_re - 1.0
    f_re = ((nr * lr + ab_im * li) / den)[..., None]
    f_im = ((ab_im * lr - nr * li) / den)[..., None]
    bb_re = f_re * b_re - f_im * b_im
    bb_im = f_re * b_im + f_im * b_re
    eye = jnp.eye(S5_GROUPS, dtype=F32)
    a = jnp.stack([ab_re.reshape(-1), ab_im.reshape(-1)])
    wb = jnp.concatenate([jnp.einsum("gph,gk->ghkp", bb_re, eye).reshape(S5_WIDTH, NS),
                          jnp.einsum("gph,gk->ghkp", bb_im, eye).reshape(S5_WIDTH, NS)], axis=1)
    wc = jnp.concatenate([jnp.einsum("ghp,gk->gpkh", c_re, eye).reshape(NS, S5_WIDTH),
                          -jnp.einsum("ghp,gk->gpkh", c_im, eye).reshape(NS, S5_WIDTH)], axis=0)
    return a, wb, wc


def _power_table(a):
    ar, ai = a[0], a[1]
    rows_r, rows_i = [ar], [ai]
    for _ in range(7):
        pr, pi = rows_r[-1], rows_i[-1]
        rows_r.append(pr * ar - pi * ai)
        rows_i.append(pr * ai + pi * ar)
    return jnp.concatenate([jnp.stack(rows_r), jnp.stack(rows_i)], axis=1)


def _rope_table(positions):
    inv = ROPE_THETA ** (-jnp.arange(0, MLA_ROPE, 2, dtype=F32) / MLA_ROPE)
    inv128 = jnp.concatenate([jnp.zeros((MLA_NOPE,), F32), inv, inv, jnp.zeros((32,), F32)])
    sign = jnp.concatenate([jnp.zeros((MLA_NOPE,), F32), -jnp.ones((16,), F32), jnp.ones((16,), F32),
                            jnp.zeros((32,), F32)])
    ang = positions.astype(F32)[:, None] * inv128
    return jnp.concatenate([jnp.cos(ang), jnp.sin(ang) * sign], axis=1)


def _derived_weights(full):
    w_in = full["w_in"]
    k1, k2 = w_in[:, 768:784], w_in[:, 784:800]
    z64, z32 = jnp.zeros((1024, 64), BF16), jnp.zeros((1024, 32), BF16)
    w_inx = jnp.concatenate([w_in[:, :768], z64, k1, k2, z32, z64, k2, k1, z32, w_in[:, 800:]], axis=1)
    uq = full["w_uq"].reshape(256, MLA_HEADS, MLA_QK)
    nope, r1, r2 = uq[:, :, :64], uq[:, :, 64:80], uq[:, :, 80:]
    zq64, zq32 = jnp.zeros((256, MLA_HEADS, 64), BF16), jnp.zeros((256, MLA_HEADS, 32), BF16)
    w_q2 = jnp.concatenate([jnp.concatenate([nope, r1, r2, zq32], axis=2).reshape(256, 1024),
                            jnp.concatenate([zq64, r2, r1, zq32], axis=2).reshape(256, 1024)], axis=1)
    ukv = full["w_ukv"].reshape(256, MLA_HEADS, 128)
    w_kv2 = jnp.concatenate([jnp.concatenate([ukv[:, :, :64], zq64], axis=2).reshape(256, 1024),
                             jnp.concatenate([ukv[:, :, 64:], zq64], axis=2).reshape(256, 1024)], axis=1)
    return w_inx, w_q2, w_kv2


def _oa_padded(w_oa):
    oa = w_oa.reshape(MLA_HEADS, MLA_V, 1024)
    return jnp.concatenate([oa, jnp.zeros_like(oa)], axis=1).reshape(1024, 1024)


def _fold_w_in(d_inx):
    d_k1 = d_inx[:, 832:848] + d_inx[:, 976:992]
    d_k2 = d_inx[:, 848:864] + d_inx[:, 960:976]
    return jnp.concatenate([d_inx[:, :768], d_k1, d_k2, d_inx[:, 1024:]], axis=1)


def _fold_qkv(d_q2, d_kv2):
    a = d_q2[:, :1024].reshape(256, MLA_HEADS, 128)
    b = d_q2[:, 1024:].reshape(256, MLA_HEADS, 128)
    d_w_uq = jnp.concatenate([a[:, :, :64], a[:, :, 64:80] + b[:, :, 80:96], a[:, :, 80:96] + b[:, :, 64:80]],
                             axis=2).reshape(256, MLA_HEADS * MLA_QK)
    kk = d_kv2[:, :1024].reshape(256, MLA_HEADS, 128)
    vv = d_kv2[:, 1024:].reshape(256, MLA_HEADS, 128)
    d_w_ukv = jnp.concatenate([kk[:, :, :64], vv[:, :, :64]], axis=2).reshape(256, 1024)
    return d_w_uq, d_w_ukv


def kernel(x, mem, positions, ln_in_g, ln_in_b, w_in, s5_lam_re, s5_lam_im, s5_log_dt, s5_b_re, s5_b_im, s5_c_re, s5_c_im, s5_d, w_glu, q_norm_g, w_uq, kv_norm_g, w_ukv, w_oa, w_o, ln1_g, ln1_b, w_xq, w_xk, w_xv, w_xo, ln2_g, ln2_b, w_up, w_down, ln3_g, ln3_b, loss_target, m_ln_in_g, m_ln_in_b, m_w_in, m_s5_lam_re, m_s5_lam_im, m_s5_log_dt, m_s5_b_re, m_s5_b_im, m_s5_c_re, m_s5_c_im, m_s5_d, m_w_glu, m_q_norm_g, m_w_uq, m_kv_norm_g, m_w_ukv, m_w_oa, m_w_o, m_ln1_g, m_ln1_b, m_w_xq, m_w_xk, m_w_xv, m_w_xo, m_ln2_g, m_ln2_b, m_w_up, m_w_down, m_ln3_g, m_ln3_b, v_ln_in_g, v_ln_in_b, v_w_in, v_s5_lam_re, v_s5_lam_im, v_s5_log_dt, v_s5_b_re, v_s5_b_im, v_s5_c_re, v_s5_c_im, v_s5_d, v_w_glu, v_q_norm_g, v_w_uq, v_kv_norm_g, v_w_ukv, v_w_oa, v_w_o, v_ln1_g, v_ln1_b, v_w_xq, v_w_xk, v_w_xv, v_w_xo, v_ln2_g, v_ln2_b, v_w_up, v_w_down, v_ln3_g, v_ln3_b):
    args = dict(locals())
    wts = {n: args[n] for n in WEIGHTS}
    mom = {n: args["m_" + n] for n in WEIGHTS}
    vel = {n: args["v_" + n] for n in WEIGHTS}
    xs, mems, tgt = x[0], mem[0], loss_target[0]
    s = xs.shape[0]
    names_big = [n for (n, _, _, _) in SHARDED]
    kind = {n: kd for (n, kd, _, _) in SHARDED}
    n_small = sum(math.prod(wts[n].shape) for n in SMALL)
    small_rows = _round_up(n_small + 1, 8 * LANES) // LANES

    late = [n for n in names_big if n not in EARLY]
    shard = {n: wts[n][0].astype(BF16) for n in names_big}
    g_early = dict(zip(EARLY, _all_gather_two_level("all_gather", [shard[n] for n in EARLY])))
    ag_handle, ag_token = _exchange_start("all_gather_late", [shard[n] for n in late], True, after=g_early["w_in"])
    full = {n: _cols_from_blocks(g_early[n]) for n in EARLY}
    w_inx, w_q2, w_kv2 = _derived_weights(full)
    s5_args = (s5_lam_re[0], s5_lam_im[0], s5_log_dt[0], s5_b_re[0], s5_b_im[0], s5_c_re[0], s5_c_im[0])
    (a_mat, w_b, w_c), s5_vjp = jax.vjp(_s5_mats, *s5_args)
    w_bb, w_cb = w_b.astype(BF16), w_c.astype(BF16)
    pw = _power_table(a_mat)
    pwb = _power_table(a_mat * jnp.array([[1.0], [-1.0]], F32))[::-1]
    tab = _rope_table(positions[0])
    row = lambda a: a.reshape(1, -1)
    ln_g, ln_b = row(ln_in_g) + ag_token[0:1, 0:1], row(ln_in_b)
    lns = [ln1_g, ln1_b, ln2_g, ln2_b]

    h0, z3, gate, h, q, k, v, kt, vt = _in_fwd(xs, tab, ln_g, ln_b, q_norm_g, kv_norm_g, pw, w_inx, w_bb, w_q2, w_kv2)
    yl, s_out = _s5out_fwd(h, z3, s5_d, w_cb, full["w_glu"])
    o, lse_t = _attn_fwd(q, k, vt)
    g_late = dict(zip(late, _exchange_wait("all_gather_late_wait", ag_handle, after=lse_t)))
    full.update({n: g_late[n].reshape(-1, g_late[n].shape[2]) if kind[n] == "row" else _cols_from_blocks(g_late[n])
                 for n in late})
    w_oap = _oa_padded(full["w_oa"])
    memk, memv = _mem_kv(mems, full["w_xk"], full["w_xv"])
    r2, mixin_b, h1_b, ox_b, a_out, r1, qx_b = _post_fwd(o, s_out, gate, h0, memk, memv, lns,
                                        w_oap, full["w_o"], full["w_xq"], full["w_xo"])
    dh2, h2_b, dup_b, act_b, dff_b, loss_acc, d_ln3_g, d_ln3_b = _mlp(
        r2, tgt, ln2_g, ln2_b, ln3_g, ln3_b, full["w_up"], full["w_down"])
    (do_b, delta, d_so, dgate, dh0p, daout_b, dmix_b, dqx_b, dxa_b,
     dmk, dmv, d_ln1_g, d_ln1_b, d_ln2_g, d_ln2_b) = _post_bwd(
        dh2, o, s_out, gate, a_out, r1, r2, qx_b, memk, memv, lns, w_oap, full["w_o"], full["w_xq"], full["w_xo"])
    rows8 = lambda g: g.reshape(N_DEV, g.shape[0] // N_DEV, g.shape[1])
    d_w_oa = _wgrad("wg_oa", o, daout_b).reshape(MLA_HEADS, 128, 1024)[:, :64].reshape(512, 1024)
    send = {
        "w_oa": _blocks_from_cols(d_w_oa).astype(BF16),
        "w_up": _wgrad("wg_up", h2_b, dup_b, BF16, N_DEV),
        "w_o": rows8(_wgrad("wg_o", mixin_b, dmix_b, BF16)),
        "w_xq": rows8(_wgrad("wg_xq", h1_b, dqx_b, BF16)),
        "w_xk": rows8(_wgrad("wg_xk", mems, dmk, BF16)),
        "w_xv": rows8(_wgrad("wg_xv", mems, dmv, BF16)),
        "w_xo": rows8(_wgrad("wg_xo", ox_b, dxa_b, BF16)),
        "w_down": rows8(_wgrad("wg_down", act_b, dff_b, BF16)),
    }
    rs_handle, rs_token = _exchange_start("grad_exchange_late", [send[n] for n in late], False)
    dl_t = delta.T[:MLA_HEADS].reshape(MLA_HEADS, 1, s) + rs_token[0, 0]
    dq, dk, dv = _attn_bwd(q, k, v, kt, do_b, lse_t, dl_t)
    lam, du_p, yg_b, dy12_b, dyl_b, d_s5_d, d_a = _s5out_bwd(d_so, yl, z3, h, s5_d, pwb, w_cb, full["w_glu"])
    (dx, h0_b, dz_b, u_b, cqn_b, ckvn_b, dq2_b, dkv2_b, d_ln_g, d_ln_b, d_qg, d_kvg) = _in_bwd(
        xs, tab, z3, dq, dk, dv, lam, du_p, dgate, dh0p, ln_g, ln_b, q_norm_g, kv_norm_g, w_inx, w_bb, w_q2, w_kv2)

    d_w_in = _fold_w_in(_wgrad("wg_in", h0_b, dz_b))
    d_w_uq, d_w_ukv = _fold_qkv(_wgrad("wg_q", cqn_b, dq2_b), _wgrad("wg_kv", ckvn_b, dkv2_b))
    send.update({
        "w_in": _blocks_from_cols(d_w_in).astype(BF16), "w_uq": _blocks_from_cols(d_w_uq).astype(BF16),
        "w_ukv": _blocks_from_cols(d_w_ukv).astype(BF16), "w_glu": _wgrad("wg_glu", yg_b, dy12_b, BF16, N_DEV),
    })
    re_handle, re_token = _exchange_start("grad_exchange_early", [send[n] for n in EARLY], False)
    d_s5 = s5_vjp((d_a.reshape(2, NS), _wgrad("wg_s5b", u_b, lam, after=re_token),
                   _wgrad("wg_s5c", h, dyl_b, after=re_token)))
    small_grads = {
        "ln_in_g": d_ln_g, "ln_in_b": d_ln_b, "s5_lam_re": d_s5[0], "s5_lam_im": d_s5[1], "s5_log_dt": d_s5[2],
        "s5_b_re": d_s5[3], "s5_b_im": d_s5[4], "s5_c_re": d_s5[5], "s5_c_im": d_s5[6], "s5_d": d_s5_d,
        "q_norm_g": d_qg, "kv_norm_g": d_kvg, "ln1_g": d_ln1_g, "ln1_b": d_ln1_b, "ln2_g": d_ln2_g,
        "ln2_b": d_ln2_b, "ln3_g": d_ln3_g, "ln3_b": d_ln3_b,
    }
    recv_small = _all_gather_two_level(
        "grad_exchange", [_pack_rows([small_grads[n] for n in SMALL] + [loss_acc[0, :1]], small_rows)])[0]

    recv = dict(zip(late, _exchange_wait("grad_exchange_late_wait", rs_handle, after=recv_small)))
    results = [dict(), dict(), dict(), dict()]
    small_out = _adamw("adamw_small", recv_small,
                       *[_pack_rows([t[n] for n in SMALL], small_rows) for t in (wts, mom, vel)])
    for res, fs in zip(results, small_out):
        for n, a in zip(SMALL, _unpack_flat(fs, [wts[n].shape for n in SMALL])):
            res[n] = a
    for n in late + list(EARLY):
        if n == EARLY[0]:
            recv.update(zip(EARLY, _exchange_wait("grad_exchange_early_wait", re_handle, after=results[0][late[-1]])))
        outs = _adamw("adamw_" + n, recv[n], wts[n][0], mom[n][0], vel[n][0])
        for res, a in zip(results, outs):
            res[n] = a[None]

    loss = small_out[0].reshape(-1)[n_small]
    return (loss, dx[None], *[res[n] for res in results for n in WEIGHTS])
```

```python
import math

import jax
import jax.numpy as jnp
from jax import lax
from jax.experimental import pallas as pl
from jax.experimental.pallas import tpu as pltpu

F32 = jnp.float32
BF16 = jnp.bfloat16

D_MODEL = 1024
S5_WIDTH = 256
S5_GROUP_CH = 16
S5_GROUPS = 16
S5_STATE = 64
NS = S5_GROUPS * S5_STATE
S5_MAX_RE = -1e-4
MLA_HEADS = 8
MLA_NOPE = 64
MLA_ROPE = 32
MLA_QK = 96
MLA_V = 64
HEAD_PAD = 128
ROPE_THETA = 10000.0
XATTN_HEADS = 4
XATTN_HD = 256
MLP_HIDDEN = 4096
LN_EPS = 1e-5
RMS_EPS = 1e-6
NEG_INF = -1e30
LOG2E = 1.4426950408889634
DN_ALPHA = 2.0 ** 0.25
ADAM_LR = 0.001
ADAM_B1 = 0.9
ADAM_B2 = 0.999
ADAM_EPS = 1e-08
ADAM_WD = 0.01
ADAM_STEP = 10

N_DEV = 8
MESH = pl.DeviceIdType.MESH
LANES = 1024
VMEM_LIMIT = 60 * 1024 * 1024

ROW_TILE = 256
ATT_TILE = 1024
ADAM_TILE = 256

SHARDED = (
    ("w_in", "col", 1024, 2848), ("w_glu", "col", 256, 2048), ("w_uq", "col", 256, 768),
    ("w_ukv", "col", 256, 1024), ("w_oa", "col", 512, 1024), ("w_o", "row", 1024, 1024),
    ("w_xq", "row", 1024, 1024), ("w_xk", "row", 1024, 1024), ("w_xv", "row", 1024, 1024),
    ("w_xo", "row", 1024, 1024), ("w_up", "col", 1024, 4096), ("w_down", "row", 4096, 1024),
)
EARLY = ("w_in", "w_glu", "w_uq", "w_ukv")
GATHER_EARLY = ("w_in", "w_uq", "w_ukv")
SMALL = ("ln_in_g", "ln_in_b", "s5_lam_re", "s5_lam_im", "s5_log_dt", "s5_b_re", "s5_b_im", "s5_c_re",
         "s5_c_im", "s5_d", "q_norm_g", "kv_norm_g", "ln1_g", "ln1_b", "ln2_g", "ln2_b", "ln3_g", "ln3_b")
WEIGHTS = ("ln_in_g", "ln_in_b", "w_in", "s5_lam_re", "s5_lam_im", "s5_log_dt", "s5_b_re", "s5_b_im",
           "s5_c_re", "s5_c_im", "s5_d", "w_glu", "q_norm_g", "w_uq", "kv_norm_g", "w_ukv", "w_oa", "w_o",
           "ln1_g", "ln1_b", "w_xq", "w_xk", "w_xv", "w_xo", "ln2_g", "ln2_b", "w_up", "w_down", "ln3_g", "ln3_b")


def _round_up(n, m):
    return (n + m - 1) // m * m


def _bf(a):
    return a.astype(BF16)


def _mm(a, b):
    return jnp.dot(_bf(a), _bf(b), preferred_element_type=F32)


def _mm_nt(a, b):
    return lax.dot_general(_bf(a), _bf(b), (((1,), (1,)), ((), ())), preferred_element_type=F32)


def _mm_tn(a, b):
    return lax.dot_general(_bf(a), _bf(b), (((0,), (0,)), ((), ())), preferred_element_type=F32)


def _sigmoid(a):
    return 1.0 / (1.0 + jnp.exp(-a))


def _gelu(a):
    return 0.5 * a * (1.0 + lax.erf(a * (2.0 ** -0.5)))


def _gelu_grad(a):
    return 0.5 * (1.0 + lax.erf(a * (2.0 ** -0.5))) + a * jnp.exp(-0.5 * a * a) * (1.0 / math.sqrt(2.0 * math.pi))


def _ln_fwd(a, g, b):
    mu = jnp.mean(a, axis=-1, keepdims=True)
    ac = a - mu
    var = jnp.mean(ac * ac, axis=-1, keepdims=True)
    rstd = lax.rsqrt(var + LN_EPS)
    xhat = ac * rstd
    return xhat * g + b, xhat, rstd


def _ln_bwd(dy, xhat, rstd, g):
    dxh = dy * g
    m1 = jnp.mean(dxh, axis=-1, keepdims=True)
    m2 = jnp.mean(dxh * xhat, axis=-1, keepdims=True)
    dx = rstd * (dxh - m1 - xhat * m2)
    return dx, jnp.sum(dy * xhat, axis=0, keepdims=True), jnp.sum(dy, axis=0, keepdims=True)


def _rms_fwd(a, g):
    r = lax.rsqrt(jnp.mean(a * a, axis=-1, keepdims=True) + RMS_EPS)
    xn = a * r
    return xn * g, xn, r


def _rms_bwd(dy, xn, r, g):
    dxn = dy * g
    dx = r * (dxn - xn * jnp.mean(dxn * xn, axis=-1, keepdims=True))
    return dx, jnp.sum(dy * xn, axis=0, keepdims=True)


def _tile_heads(a):
    return jnp.concatenate([a] * MLA_HEADS, axis=1)


def _row_call(name, body, n_rows, ts, tiled_in, full_in, weights, tiled_out, acc_out, reverse=False, scratch=(),
              tiled_out_t=()):
    n = n_rows // ts
    assert n * ts == n_rows, (name, n_rows, ts)
    nt, nf, nw = len(tiled_in), len(full_in), len(weights)
    nto, nao = len(tiled_out) + len(tiled_out_t), len(acc_out)
    if reverse:
        imap = lambda i: (n - 1 - i, 0)
    else:
        imap = lambda i: (i, 0)
    const = lambda i: (0, 0)

    def kern(*refs):
        ins = refs[:nt + nf]
        w_hbm = refs[nt + nf:nt + nf + nw]
        outs = refs[nt + nf + nw:nt + nf + nw + nto + nao]
        scr = refs[nt + nf + nw + nto + nao:]
        w_vmem = scr[:nw]
        extra = scr[nw + 1:] if nw else scr
        if nw:
            sem = scr[nw]

            @pl.when(pl.program_id(0) == 0)
            def _():
                cps = [pltpu.make_async_copy(w_hbm[k], w_vmem[k], sem.at[k]) for k in range(nw)]
                for cp in cps:
                    cp.start()
                for cp in cps:
                    cp.wait()
        body(*ins, *w_vmem, *outs, *extra)

    in_specs = [pl.BlockSpec((ts, a.shape[1]), imap) for a in tiled_in]
    in_specs += [pl.BlockSpec(a.shape, const) for a in full_in]
    in_specs += [pl.BlockSpec(memory_space=pl.ANY) for _ in weights]
    assert not (reverse and tiled_out_t)
    out_shape = [jax.ShapeDtypeStruct((n_rows, c), dt) for (c, dt) in tiled_out]
    out_shape += [jax.ShapeDtypeStruct((c, n_rows), dt) for (c, dt) in tiled_out_t]
    out_shape += [jax.ShapeDtypeStruct(shp, F32) for shp in acc_out]
    out_specs = [pl.BlockSpec((ts, c), imap) for (c, dt) in tiled_out]
    out_specs += [pl.BlockSpec((c, ts), lambda i: (0, i)) for (c, dt) in tiled_out_t]
    out_specs += [pl.BlockSpec(shp, const) for shp in acc_out]
    scratch_shapes = [pltpu.VMEM(w.shape, w.dtype) for w in weights]
    if nw:
        scratch_shapes.append(pltpu.SemaphoreType.DMA((nw,)))
    scratch_shapes += list(scratch)
    return pl.pallas_call(
        kern, name=name, grid=(n,), in_specs=in_specs, out_specs=out_specs, out_shape=out_shape,
        scratch_shapes=scratch_shapes,
        compiler_params=pltpu.CompilerParams(dimension_semantics=("arbitrary",), vmem_limit_bytes=VMEM_LIMIT),
    )(*tiled_in, *full_in, *weights)


def _mem_kv(mem, w_xk, w_xv):
    m = mem.shape[0]

    def body(mem_ref, wk_ref, wv_ref, k_ref, v_ref):
        mb = mem_ref[...]
        k_ref[...] = _mm(mb, wk_ref[...]).astype(BF16)
        v_ref[...] = _mm(mb, wv_ref[...]).astype(BF16)

    return pl.pallas_call(
        body, name="mem_kv",
        out_shape=[jax.ShapeDtypeStruct((m, D_MODEL), BF16)] * 2,
        compiler_params=pltpu.CompilerParams(vmem_limit_bytes=VMEM_LIMIT),
    )(mem, w_xk, w_xv)


def _in_fwd(x, tab, ln_g, ln_b, qg, kvg, pw, w_inx, w_b, w_q2, w_kv2):
    ts = min(ROW_TILE, x.shape[0])

    def body(x_ref, tab_ref, lng_ref, lnb_ref, qg_ref, kvg_ref, pw_ref, winx, wb, wq2, wkv2,
             h0_ref, z3_ref, gate_ref, h_ref, q_ref, k_ref, v_ref, kt_ref, vt_ref, bu_scr, carry):
        @pl.when(pl.program_id(0) == 0)
        def _():
            carry[...] = jnp.zeros_like(carry)

        h0, _, _ = _ln_fwd(x_ref[...], lng_ref[...], lnb_ref[...])
        h0_ref[...] = h0
        z = _mm(h0, winx[...])
        z3_ref[...] = z[:, :768]
        gate_ref[...] = z[:, 1024:].astype(BF16)
        bu_scr[...] = _mm(z[:, :256], wb[...])
        _scan_fwd_rows(pw_ref, bu_scr, h_ref, carry, ts // 8)
        tab_v = tab_ref[...]
        cq1, cq2, ck1 = tab_v[:, :128], tab_v[:, 128:], tab_v[:, :128]
        cqn, _, _ = _rms_fwd(z[:, 256:512], qg_ref[...])
        q2 = _mm(cqn, wq2[...])
        q_ref[...] = (q2[:, :1024] * _tile_heads(cq1) + q2[:, 1024:] * _tile_heads(cq2)).astype(BF16)
        ckvn, _, _ = _rms_fwd(z[:, 512:768], kvg_ref[...])
        kv2 = _mm(ckvn, wkv2[...])
        krp = z[:, 768:896] * ck1 + z[:, 896:1024] * cq2
        kf = kv2[:, :1024] + _tile_heads(krp)
        k_ref[...] = kf.astype(BF16)
        v_ref[...] = kv2[:, 1024:].astype(BF16)
        kt_ref[...] = kf.T.astype(BF16)
        vt_ref[...] = kv2[:, 1024:].T.astype(BF16)

    s = x.shape[0]
    return _row_call(
        "in_fwd", body, s, ts, [x, tab], [ln_g, ln_b, qg, kvg, pw], [w_inx, w_b, w_q2, w_kv2],
        [(1024, F32), (768, F32), (2048, BF16), (2048, F32), (1024, BF16), (1024, BF16), (1024, BF16)], [],
        tiled_out_t=[(1024, BF16), (1024, BF16)], scratch=[pltpu.VMEM((ts, 2 * NS), F32), pltpu.VMEM((8, 2 * NS), F32)])


def _scan_coeffs(pw_ref, rows, forward):
    row = lax.broadcasted_iota(jnp.int32, (8, NS), 0)
    out = []
    for k, r in zip((1, 2, 4), rows):
        keep = (row >= k) if forward else (row < 8 - k)
        out.append((jnp.where(keep, pw_ref[r:r + 1, :NS], 0.0), jnp.where(keep, pw_ref[r:r + 1, NS:], 0.0), k))
    return out


def _scan_level(xr, xi, coeff, forward):
    ar, ai, k = coeff
    shift = k if forward else 8 - k
    sr, si = pltpu.roll(xr, shift, 0), pltpu.roll(xi, shift, 0)
    return xr + ar * sr - ai * si, xi + ar * si + ai * sr


def _scan_fwd_rows(pw_ref, bu_ref, h_ref, carry, nblk):
    coeffs = _scan_coeffs(pw_ref, (0, 1, 3), True)

    def blk(b, _):
        r0 = pl.multiple_of(b * 8, 8)
        xr = bu_ref[pl.ds(r0, 8), :NS]
        xi = bu_ref[pl.ds(r0, 8), NS:]
        for coeff in coeffs:
            xr, xi = _scan_level(xr, xi, coeff, True)
        cr, ci = carry[7:8, :NS], carry[7:8, NS:]
        pr, pi = pw_ref[:, :NS], pw_ref[:, NS:]
        hr = xr + pr * cr - pi * ci
        hi = xi + pr * ci + pi * cr
        h_ref[pl.ds(r0, 8), :NS] = hr
        h_ref[pl.ds(r0, 8), NS:] = hi
        carry[:, :NS] = hr
        carry[:, NS:] = hi
        return 0

    lax.fori_loop(0, nblk, blk, 0)


def _scan_bwd_rows(pw_ref, g_ref, h_ref, lam_ref, carry, acc, nblk):
    row = lax.broadcasted_iota(jnp.int32, (8, NS), 0)
    coeffs = _scan_coeffs(pw_ref, (7, 6, 4), False)

    def blk(bb, _):
        r0 = pl.multiple_of((nblk - 1 - bb) * 8, 8)
        xr = g_ref[pl.ds(r0, 8), :NS]
        xi = g_ref[pl.ds(r0, 8), NS:]
        for coeff in coeffs:
            xr, xi = _scan_level(xr, xi, coeff, False)
        cr, ci = carry[0:1, :NS], carry[0:1, NS:]
        pr, pi = pw_ref[:, :NS], pw_ref[:, NS:]
        lr = xr + pr * cr - pi * ci
        li = xi + pr * ci + pi * cr
        lam_ref[pl.ds(r0, 8), :NS] = lr
        lam_ref[pl.ds(r0, 8), NS:] = li
        nr = jnp.where(row < 7, pltpu.roll(lr, 7, 0), cr)
        ni = jnp.where(row < 7, pltpu.roll(li, 7, 0), ci)
        hr = h_ref[pl.ds(r0, 8), :NS]
        hi = h_ref[pl.ds(r0, 8), NS:]
        acc[:, :NS] += nr * hr + ni * hi
        acc[:, NS:] += ni * hr - nr * hi
        carry[:, :NS] = lr
        carry[:, NS:] = li
        return 0

    lax.fori_loop(0, nblk, blk, 0)


def _s5out_fwd(h, z3, d_skip, w_c, w_glu):
    def body(h_ref, z3_ref, d_ref, wc, wglu, yl_ref, so_ref):
        yl = _mm(h_ref[...], wc[...]) + d_ref[...] * z3_ref[:, :256]
        yl_ref[...] = yl
        y12 = _mm(_gelu(yl), wglu[...])
        so_ref[...] = y12[:, :1024] * _sigmoid(y12[:, 1024:])

    s = h.shape[0]
    return _row_call("s5out_fwd", body, s, min(ROW_TILE, s), [h, z3], [d_skip], [w_c, w_glu],
                     [(256, F32), (1024, F32)], [])


def _causal_mask_t(t):
    row = lax.broadcasted_iota(jnp.int32, (t, t), 0)
    col = lax.broadcasted_iota(jnp.int32, (t, t), 1)
    return row <= col


def _attn_fwd(q, k, vt):
    s = q.shape[0]
    t = min(ATT_TILE, s)
    nq = s // t
    scale = MLA_QK ** -0.5
    c2 = scale * LOG2E

    def body(q_ref, k_ref, vt_ref, o_ref, lse_ref, s0, s1):
        i = pl.program_id(1)
        qb = q_ref[...]

        def scores(kb, dst):
            st = _mm_nt(k_ref[pl.ds(pl.multiple_of(kb * t, t), t), :], qb)
            dst[...] = st
            return jnp.max(st, axis=0, keepdims=True)

        def update(kb, src, mt, state, masked):
            m, l, acc = state
            if masked:
                keep = _causal_mask_t(t)
                mt = jnp.max(jnp.where(keep, src[...], NEG_INF), axis=0, keepdims=True)
            m_new = jnp.maximum(m, mt)
            if masked:
                p = jnp.exp2((jnp.where(keep, src[...], NEG_INF) - m_new) * c2)
            else:
                p = jnp.exp2((src[...] - m_new) * c2)
            a = jnp.exp2((m - m_new) * c2)
            l = a * l + jnp.sum(p, axis=0, keepdims=True)
            acc = a * acc + _mm(vt_ref[:, pl.ds(pl.multiple_of(kb * t, t), t)], p)
            return m_new, l, acc

        def pair(jj, carry):
            mt_a, state = carry
            mt_b = scores(2 * jj + 1, s1)
            state = update(2 * jj, s0, mt_a, state, False)
            mt_a = scores(2 * jj + 2, s0)
            return mt_a, update(2 * jj + 1, s1, mt_b, state, False)

        def odd_tail(carry):
            mt_a, state = carry
            mt_b = scores(i, s1)
            state = update(i - 1, s0, mt_a, state, False)
            return update(i, s1, mt_b, state, True)

        def even_tail(carry):
            mt_a, state = carry
            return update(i, s0, mt_a, state, True)

        init = (jnp.full((1, t), NEG_INF, F32), jnp.zeros((1, t), F32), jnp.zeros((HEAD_PAD, t), F32))
        carry = lax.fori_loop(0, i // 2, pair, (scores(0, s0), init))
        m, l, acc = lax.cond(i % 2 == 1, odd_tail, even_tail, carry)
        o_ref[...] = (acc / l).T
        lse_ref[...] = m * scale + jnp.log(l)

    return pl.pallas_call(
        body, name="attn_fwd", grid=(MLA_HEADS, nq),
        in_specs=[pl.BlockSpec((t, HEAD_PAD), lambda h, i: (i, h)),
                  pl.BlockSpec((s, HEAD_PAD), lambda h, i: (0, h)),
                  pl.BlockSpec((HEAD_PAD, s), lambda h, i: (h, 0))],
        out_specs=[pl.BlockSpec((t, HEAD_PAD), lambda h, i: (i, h)),
                   pl.BlockSpec((None, 1, t), lambda h, i: (h, 0, i))],
        out_shape=[jax.ShapeDtypeStruct((s, MLA_HEADS * HEAD_PAD), F32),
                   jax.ShapeDtypeStruct((MLA_HEADS, 1, s), F32)],
        scratch_shapes=[pltpu.VMEM((t, t), F32)] * 2,
        compiler_params=pltpu.CompilerParams(dimension_semantics=("arbitrary", "arbitrary"),
                                             vmem_limit_bytes=VMEM_LIMIT),
    )(q, k, vt)


def _attn_bwd(q, k, v, kt, do, lse_t, dl_t):
    s = q.shape[0]
    t = min(ATT_TILE, s)
    nq = s // t
    scale = MLA_QK ** -0.5
    c2 = scale * LOG2E

    def body(q_ref, k_ref, v_ref, kt_ref, do_ref, lse_ref, dl_ref, dq_ref, dk_ref, dv_ref, dqt, s0, p0, s1, p1):
        j = pl.program_id(1)
        n = nq - 1 - j

        @pl.when(j == 0)
        def _():
            dqt[...] = jnp.zeros_like(dqt)

        kk = k_ref[...]
        vv = v_ref[...]
        ktb = kt_ref[:MLA_QK, :]

        def rows(m):
            return pl.ds(pl.multiple_of(jnp.where(m < n, j + 1 + m, j) * t, t), t)

        def first(m, sbuf, pbuf):
            r = rows(m)
            sbuf[...] = _mm_nt(kk, q_ref[r, :])
            pbuf[...] = _mm_nt(vv, do_ref[r, :])

        def finish(m, sbuf, pbuf, acc, masked):
            dk, dv = acc
            r = rows(m)
            pt = jnp.exp2(sbuf[...] * c2 - lse_ref[:, r] * LOG2E)
            if masked:
                pt = jnp.where(_causal_mask_t(t), pt, 0.0)
            dv = dv + _mm(pt, do_ref[r, :])
            dst = (pt * (pbuf[...] - dl_ref[:, r])).astype(BF16)
            dk = dk + _mm(dst, q_ref[r, :])
            dqt[:MLA_QK, r] += _mm(ktb, dst)
            return dk, dv

        def pair(jj, acc):
            first(2 * jj + 1, s1, p1)
            acc = finish(2 * jj, s0, p0, acc, False)
            first(2 * jj + 2, s0, p0)
            return finish(2 * jj + 1, s1, p1, acc, False)

        def odd_tail(acc):
            first(n, s1, p1)
            acc = finish(n - 1, s0, p0, acc, False)
            return finish(n, s1, p1, acc, True)

        def even_tail(acc):
            return finish(n, s0, p0, acc, True)

        zero = jnp.zeros((t, HEAD_PAD), F32)
        first(0, s0, p0)
        acc = lax.fori_loop(0, n // 2, pair, (zero, zero))
        dk, dv = lax.cond(n % 2 == 1, odd_tail, even_tail, acc)
        dk_ref[...] = (dk * scale).astype(BF16)
        dv_ref[...] = dv.astype(BF16)

        @pl.when(j == nq - 1)
        def _():
            for cc in range(nq):
                dq_ref[cc * t:(cc + 1) * t, :] = (dqt[:, cc * t:(cc + 1) * t].T * scale).astype(BF16)

    full = pl.BlockSpec((s, HEAD_PAD), lambda h, j: (0, h))
    tile = pl.BlockSpec((t, HEAD_PAD), lambda h, j: (j, h))
    stat = pl.BlockSpec((None, 1, s), lambda h, j: (h, 0, 0))
    return pl.pallas_call(
        body, name="attn_bwd", grid=(MLA_HEADS, nq),
        in_specs=[full, tile, tile, pl.BlockSpec((HEAD_PAD, t), lambda h, j: (h, j)), full, stat, stat],
        out_specs=[full, tile, tile],
        out_shape=[jax.ShapeDtypeStruct((s, MLA_HEADS * HEAD_PAD), BF16)] * 3,
        scratch_shapes=[pltpu.VMEM((HEAD_PAD, s), F32)] + [pltpu.VMEM((t, t), F32)] * 4,
        compiler_params=pltpu.CompilerParams(dimension_semantics=("arbitrary", "arbitrary"),
                                             vmem_limit_bytes=VMEM_LIMIT),
    )(q, k, v, kt, do, lse_t, dl_t)


def _xattn_probs(qxb, memk, hh):
    sl = slice(hh * XATTN_HD, (hh + 1) * XATTN_HD)
    sc = _mm_nt(qxb[:, sl], memk[:, sl]) * (XATTN_HD ** -0.5)
    e = jnp.exp(sc - jnp.max(sc, axis=1, keepdims=True))
    return e / jnp.sum(e, axis=1, keepdims=True)


def _post_forward(o, s_out, gate, h0, memk, memv, w_oa, w_o, w_xq, w_xo, g1, b1, g2, b2):
    a_out = _mm(o, w_oa)
    sg_s = _sigmoid(gate[:, :1024])
    sg_a = _sigmoid(gate[:, 1024:])
    mixin = sg_s * s_out + sg_a * a_out
    r1 = DN_ALPHA * h0 + _mm(mixin, w_o)
    h1, xh1, rs1 = _ln_fwd(r1, g1, b1)
    qxb = _mm(h1, w_xq).astype(BF16)
    ox = jnp.concatenate([_mm(_xattn_probs(qxb, memk, hh), memv[:, hh * XATTN_HD:(hh + 1) * XATTN_HD])
                          for hh in range(XATTN_HEADS)], axis=1)
    r2 = DN_ALPHA * h1 + _mm(ox, w_xo)
    return dict(a_out=a_out, mixin=mixin, r1=r1, h1=h1, qxb=qxb, ox=ox, r2=r2)


def _post_fwd(o, s_out, gate, h0, memk, memv, lns, w_oa, w_o, w_xq, w_xo):
    def body(o_ref, so_ref, gate_ref, h0_ref, mk_ref, mv_ref, g1, b1, g2, b2, woa, wo, wxq, wxo,
             r2_ref, mixin_ref, h1_ref, ox_ref, aout_ref, r1_ref, qx_ref):
        f = _post_forward(o_ref[...], so_ref[...], gate_ref[...].astype(F32), h0_ref[...], mk_ref[...], mv_ref[...],
                          woa[...], wo[...], wxq[...], wxo[...], g1[...], b1[...], g2[...], b2[...])
        r2_ref[...] = f["r2"]
        mixin_ref[...] = f["mixin"].astype(BF16)
        h1_ref[...] = f["h1"].astype(BF16)
        ox_ref[...] = f["ox"].astype(BF16)
        aout_ref[...] = f["a_out"].astype(BF16)
        r1_ref[...] = f["r1"]
        qx_ref[...] = f["qxb"]

    s = o.shape[0]
    return _row_call("post_fwd", body, s, min(ROW_TILE, s), [o, s_out, gate, h0], [memk, memv, *lns],
                     [w_oa, w_o, w_xq, w_xo],
                     [(1024, F32), (1024, BF16), (1024, BF16), (1024, BF16), (1024, BF16), (1024, F32), (1024, BF16)], [])


def _mlp(r2, target, g2, b2, g3, b3, w_up, w_down):
    cw = 1024
    n_chunk = MLP_HIDDEN // cw

    def body(r2_ref, tgt_ref, g2_ref, b2_ref, g3_ref, b3_ref, wup, wdn,
             dh2_ref, h2b_ref, dup_ref, act_ref, dff_ref, loss_ref, dg3_ref, db3_ref, up_scr):
        @pl.when(pl.program_id(0) == 0)
        def _():
            loss_ref[...] = jnp.zeros_like(loss_ref)
            dg3_ref[...] = jnp.zeros_like(dg3_ref)
            db3_ref[...] = jnp.zeros_like(db3_ref)

        h2, _, _ = _ln_fwd(r2_ref[...], g2_ref[...], b2_ref[...])
        h2b = h2.astype(BF16)
        h2b_ref[...] = h2b
        ff = jnp.zeros(h2.shape, F32)
        for c in range(n_chunk):
            sl = slice(c * cw, (c + 1) * cw)
            a = jnp.maximum(_mm(h2b, wup[:, sl]), 0.0)
            up_scr[:, sl] = a
            actb = (a * a).astype(BF16)
            act_ref[:, sl] = actb
            ff = ff + _mm(actb, wdn[sl, :])
        h3, xh3, rs3 = _ln_fwd(DN_ALPHA * h2 + ff, g3_ref[...], b3_ref[...])
        err = h3 - tgt_ref[...]
        loss_ref[...] += 0.5 * jnp.sum(err * err) * (1.0 / D_MODEL)
        dr3, dg3, db3 = _ln_bwd(err * (1.0 / D_MODEL), xh3, rs3, g3_ref[...])
        dg3_ref[...] += dg3
        db3_ref[...] += db3
        dffb = dr3.astype(BF16)
        dff_ref[...] = dffb
        dh2 = DN_ALPHA * dr3
        for c in range(n_chunk):
            sl = slice(c * cw, (c + 1) * cw)
            dupb = (_mm_nt(dffb, wdn[sl, :]) * (2.0 * up_scr[:, sl])).astype(BF16)
            dup_ref[:, sl] = dupb
            dh2 = dh2 + _mm_nt(dupb, wup[:, sl])
        dh2_ref[...] = dh2

    s = r2.shape[0]
    ts = min(ROW_TILE, s)
    return _row_call("mlp", body, s, ts, [r2, target], [g2, b2, g3, b3], [w_up, w_down],
                     [(1024, F32), (1024, BF16), (MLP_HIDDEN, BF16), (MLP_HIDDEN, BF16), (1024, BF16)],
                     [(8, 128), (1, 1024), (1, 1024)], scratch=[pltpu.VMEM((ts, MLP_HIDDEN), F32)])


def _post_bwd(dh2, o, s_out, gate, a_out, r1, r2, qx, memk, memv, lns, w_oa, w_o, w_xq, w_xo):
    def body(dh2_ref, o_ref, so_ref, gate_ref, aout_ref, r1_ref, r2_ref, qx_ref, mk_ref, mv_ref, g1, b1, g2, b2,
             woa, wo, wxq, wxo,
             do_ref, dl_ref, dso_ref, dgate_ref, dh0_ref, daout_ref, dmix_ref, dqx_ref, dxa_ref,
             dmk_ref, dmv_ref, dg1_ref, db1_ref, dg2_ref, db2_ref):
        @pl.when(pl.program_id(0) == 0)
        def _():
            for r in (dmk_ref, dmv_ref, dg1_ref, db1_ref, dg2_ref, db2_ref):
                r[...] = jnp.zeros_like(r)

        o = o_ref[...]
        s_out = so_ref[...]
        memk, memv = mk_ref[...], mv_ref[...]
        qxb = qx_ref[...]
        _, xh1, rs1 = _ln_fwd(r1_ref[...], g1[...], b1[...])
        _, xh2, rs2 = _ln_fwd(r2_ref[...], g2[...], b2[...])
        dr2, dg2, db2 = _ln_bwd(dh2_ref[...], xh2, rs2, g2[...])
        dg2_ref[...] += dg2
        db2_ref[...] += db2
        dxab = dr2.astype(BF16)
        dxa_ref[...] = dxab
        dox = _mm_nt(dxab, wxo[...])
        dqs = []
        for hh in range(XATTN_HEADS):
            sl = slice(hh * XATTN_HD, (hh + 1) * XATTN_HD)
            p = _xattn_probs(qxb, memk, hh)
            doxh = dox[:, sl].astype(BF16)
            dp = _mm_nt(doxh, memv[:, sl])
            ds = (p * (dp - jnp.sum(dp * p, axis=1, keepdims=True)) * (XATTN_HD ** -0.5)).astype(BF16)
            dqs.append(_mm(ds, memk[:, sl]))
            dmk_ref[:, sl] += _mm_tn(ds, qxb[:, sl])
            dmv_ref[:, sl] += _mm_tn(p, doxh)
        dqxb = jnp.concatenate(dqs, axis=1).astype(BF16)
        dqx_ref[...] = dqxb
        dh1 = DN_ALPHA * dr2 + _mm_nt(dqxb, wxq[...])
        dr1, dg1, db1 = _ln_bwd(dh1, xh1, rs1, g1[...])
        dg1_ref[...] += dg1
        db1_ref[...] += db1
        dh0_ref[...] = DN_ALPHA * dr1
        dmixb = dr1.astype(BF16)
        dmix_ref[...] = dmixb
        dmixin = _mm_nt(dmixb, wo[...])
        sg_s = _sigmoid(gate_ref[:, :1024].astype(F32))
        sg_a = _sigmoid(gate_ref[:, 1024:].astype(F32))
        dso_ref[...] = dmixin * sg_s
        daoutb = (dmixin * sg_a).astype(BF16)
        daout_ref[...] = daoutb
        dgate_ref[:, :1024] = (dmixin * s_out * sg_s * (1.0 - sg_s)).astype(BF16)
        dgate_ref[:, 1024:] = (dmixin * aout_ref[...].astype(F32) * sg_a * (1.0 - sg_a)).astype(BF16)
        d_o = _mm_nt(daoutb, woa[...])
        do_ref[...] = d_o.astype(BF16)
        lane = lax.broadcasted_iota(jnp.int32, (o.shape[0], HEAD_PAD), 1)
        dlc = jnp.zeros((o.shape[0], HEAD_PAD), F32)
        for hh in range(MLA_HEADS):
            sl = slice(hh * HEAD_PAD, (hh + 1) * HEAD_PAD)
            dl = jnp.sum(d_o[:, sl] * o[:, sl], axis=1, keepdims=True)
            dlc = dlc + jnp.where(lane == hh, dl, 0.0)
        dl_ref[...] = dlc

    s = o.shape[0]
    m = memk.shape[0]
    return _row_call(
        "post_bwd", body, s, min(ROW_TILE, s), [dh2, o, s_out, gate, a_out, r1, r2, qx], [memk, memv, *lns],
        [w_oa, w_o, w_xq, w_xo],
        [(1024, BF16), (HEAD_PAD, F32), (1024, F32), (2048, BF16), (1024, F32),
         (1024, BF16), (1024, BF16), (1024, BF16), (1024, BF16)],
        [(m, 1024), (m, 1024), (1, 1024), (1, 1024), (1, 1024), (1, 1024)])


def _s5out_bwd(d_so, yl, z3, h, d_skip, pwb, w_c, w_glu):
    s = d_so.shape[0]
    ts = min(ROW_TILE, s)
    n_tiles = s // ts

    def body(dso_ref, yl_ref, z3_ref, h_ref, d_ref, pw_ref, wc, wglu,
             lam_ref, dup_ref, yg_ref, dy12_ref, dyl_ref, dd_ref, da_ref, gh_scr, carry, acc):
        @pl.when(pl.program_id(0) == 0)
        def _():
            dd_ref[...] = jnp.zeros_like(dd_ref)
            carry[...] = jnp.zeros_like(carry)
            acc[...] = jnp.zeros_like(acc)

        yl = yl_ref[...]
        ygb = _gelu(yl).astype(BF16)
        yg_ref[...] = ygb
        y12 = _mm(ygb, wglu[...])
        sg = _sigmoid(y12[:, 1024:])
        dso = dso_ref[...]
        dy12b = jnp.concatenate([dso * sg, dso * y12[:, :1024] * sg * (1.0 - sg)], axis=1).astype(BF16)
        dy12_ref[...] = dy12b
        dyl = _mm_nt(dy12b, wglu[...]) * _gelu_grad(yl)
        dylb = dyl.astype(BF16)
        dyl_ref[...] = dylb
        gh_scr[...] = _mm_nt(dylb, wc[...])
        _scan_bwd_rows(pw_ref, gh_scr, h_ref, lam_ref, carry, acc, ts // 8)
        dup_ref[...] = dyl * d_ref[...]
        dd_ref[...] += jnp.sum(dyl * z3_ref[:, :256], axis=0, keepdims=True)

        @pl.when(pl.program_id(0) == n_tiles - 1)
        def _():
            da_ref[...] = jnp.sum(acc[...], axis=0, keepdims=True)

    return _row_call("s5out_bwd", body, s, ts, [d_so, yl, z3, h], [d_skip, pwb], [w_c, w_glu],
                     [(2 * NS, F32), (256, F32), (256, BF16), (2048, BF16), (256, BF16)], [(1, 256), (1, 2 * NS)],
                     reverse=True, scratch=[pltpu.VMEM((ts, 2 * NS), F32), pltpu.VMEM((8, 2 * NS), F32),
                                            pltpu.VMEM((8, 2 * NS), F32)])


def _in_bwd(x, tab, z3, dq, dk, dv, lam, du_p, dgate, dh0p, ln_g, ln_b, qg, kvg, w_inx, w_b, w_q2, w_kv2):
    def body(x_ref, tab_ref, z3_ref, dq_ref, dk_ref, dv_ref, lam_ref, dup_ref, dgate_ref, dh0p_ref,
             lng_ref, lnb_ref, qg_ref, kvg_ref, winx, wb, wq2, wkv2,
             dx_ref, h0b_ref, dz_ref, ub_ref, cqn_ref, ckvn_ref, dq2_ref, dkv2_ref,
             dlng_ref, dlnb_ref, dqg_ref, dkvg_ref):
        @pl.when(pl.program_id(0) == 0)
        def _():
            for r in (dlng_ref, dlnb_ref, dqg_ref, dkvg_ref):
                r[...] = jnp.zeros_like(r)

        h0, xh0, rs0 = _ln_fwd(x_ref[...], lng_ref[...], lnb_ref[...])
        h0b_ref[...] = h0.astype(BF16)
        z3 = z3_ref[...]
        ub_ref[...] = z3[:, :256].astype(BF16)
        tab_v = tab_ref[...]
        cq1, cq2, ck1 = tab_v[:, :128], tab_v[:, 128:], tab_v[:, :128]
        du = _mm_nt(lam_ref[...], wb[...]) + dup_ref[...]
        dq = dq_ref[...].astype(F32)
        dq2b = jnp.concatenate([dq * _tile_heads(cq1), dq * _tile_heads(cq2)], axis=1).astype(BF16)
        dq2_ref[...] = dq2b
        cqg, cqn, rq = _rms_fwd(z3[:, 256:512], qg_ref[...])
        cqn_ref[...] = cqg.astype(BF16)
        dcq, dqg = _rms_bwd(_mm_nt(dq2b, wq2[...]), cqn, rq, qg_ref[...])
        dqg_ref[...] += dqg
        dkv2b = jnp.concatenate([dk_ref[...], dv_ref[...]], axis=1)
        dk = dk_ref[...].astype(F32)
        dkv2_ref[...] = dkv2b
        ckvg, ckvn, rkv = _rms_fwd(z3[:, 512:768], kvg_ref[...])
        ckvn_ref[...] = ckvg.astype(BF16)
        dckv, dkvg = _rms_bwd(_mm_nt(dkv2b, wkv2[...]), ckvn, rkv, kvg_ref[...])
        dkvg_ref[...] += dkvg
        dkrp = dk[:, :HEAD_PAD]
        for hh in range(1, MLA_HEADS):
            dkrp = dkrp + dk[:, hh * HEAD_PAD:(hh + 1) * HEAD_PAD]
        dzb = jnp.concatenate([a.astype(BF16) for a in (du, dcq, dckv, dkrp * ck1, dkrp * cq2)] + [dgate_ref[...]],
                              axis=1)
        dz_ref[...] = dzb
        dh0 = _mm_nt(dzb, winx[...]) + dh0p_ref[...]
        dx, dg, db = _ln_bwd(dh0, xh0, rs0, lng_ref[...])
        dx_ref[...] = dx
        dlng_ref[...] += dg
        dlnb_ref[...] += db

    s = x.shape[0]
    return _row_call(
        "in_bwd", body, s, min(ROW_TILE, s), [x, tab, z3, dq, dk, dv, lam, du_p, dgate, dh0p],
        [ln_g, ln_b, qg, kvg], [w_inx, w_b, w_q2, w_kv2],
        [(1024, F32), (1024, BF16), (3072, BF16), (256, BF16), (256, BF16), (256, BF16), (2048, BF16), (2048, BF16)],
        [(1, 1024), (1, 1024), (1, 256), (1, 256)])


def _wgrad(name, xs, dy, out_dtype=F32, blocks=1, after=None):
    s, k = xs.shape
    n = dy.shape[1]
    nb = n // blocks
    tk = min(k, 1024)
    tn = max(d for d in range(128, min(n, 2048) + 1, 128) if n % d == 0)
    ts = min(s, 2048)
    per_tile = tn // nb
    assert s % ts == 0 and k % tk == 0 and n % tn == 0 and (blocks == 1 or tn % nb == 0), (name, s, k, n)
    last = s // ts - 1

    def body(x_ref, dy_ref, *rest):
        o_ref, acc = rest[-2:]

        @pl.when(pl.program_id(2) == 0)
        def _():
            acc[...] = jnp.zeros_like(acc)

        acc[...] += _mm_tn(x_ref[...], dy_ref[...])

        @pl.when(pl.program_id(2) == last)
        def _():
            if blocks == 1:
                o_ref[...] = acc[...].astype(out_dtype)
            else:
                for d in range(per_tile):
                    o_ref[d] = acc[:, d * nb:(d + 1) * nb].astype(out_dtype)

    if blocks == 1:
        out_shape = jax.ShapeDtypeStruct((k, n), out_dtype)
        out_spec = pl.BlockSpec((tk, tn), lambda a, b, c: (a, b))
    else:
        out_shape = jax.ShapeDtypeStruct((blocks, k, nb), out_dtype)
        out_spec = pl.BlockSpec((per_tile, tk, nb), lambda a, b, c: (b, a, 0))
    return pl.pallas_call(
        body, name=name, grid=(k // tk, n // tn, s // ts),
        in_specs=[pl.BlockSpec((ts, tk), lambda a, b, c: (c, a)), pl.BlockSpec((ts, tn), lambda a, b, c: (c, b))]
        + ([] if after is None else [pl.BlockSpec(memory_space=pl.ANY)]),
        out_specs=out_spec, out_shape=out_shape, scratch_shapes=[pltpu.VMEM((tk, tn), F32)],
        compiler_params=pltpu.CompilerParams(dimension_semantics=("arbitrary", "arbitrary", "arbitrary"),
                                             vmem_limit_bytes=VMEM_LIMIT),
    )(xs, dy, *([] if after is None else [after]))


def _mesh_pos():
    x, y, c = lax.axis_index("x"), lax.axis_index("y"), lax.axis_index("c")
    return x, y, c


def _peer(x, y, c, k):
    px = 1 - x if k & 4 else x
    py = 1 - y if k & 2 else y
    pc = 1 - c if k & 1 else c
    return (px, py, pc), 4 * px + 2 * py + pc


def _all_gather_two_level(name, arrays):
    n = len(arrays)

    def body(*refs):
        src, dst = refs[:n], refs[n:2 * n]
        send_sems, recv_sems, local_sems = refs[2 * n:]
        x, y, c = _mesh_pos()
        me = 4 * x + 2 * y + c
        sibling = (x, y, 1 - c)

        def copy(i, k, source, slot, to):
            return pltpu.make_async_remote_copy(src_ref=source, dst_ref=dst[i].at[slot],
                                                send_sem=send_sems.at[(k - 1) * n + i],
                                                recv_sem=recv_sems.at[(k - 1) * n + i],
                                                device_id=to, device_id_type=MESH)

        own = [pltpu.make_async_copy(src[i], dst[i].at[me], local_sems.at[i]) for i in range(n)]
        for cp in own:
            cp.start()
        sends = []
        for k in (1, 2, 4, 6):
            peer, _ = _peer(x, y, c, k)
            for i in range(n):
                sends.append(copy(i, k, src[i], me, peer))
                sends[-1].start()
        for k in (2, 4, 6):
            peer, pid = _peer(x, y, c, k)
            for i in range(n):
                copy(i, k, src[i], pid, peer).wait_recv()
                sends.append(copy(i, k + 1, dst[i].at[pid], pid, sibling))
                sends[-1].start()
        for k in (1, 3, 5, 7):
            peer, pid = _peer(x, y, c, k)
            for i in range(n):
                copy(i, k, src[i], pid, peer).wait_recv()
        for cp in sends:
            cp.wait_send()
        for cp in own:
            cp.wait()

    n_sem = n * (N_DEV - 1)
    return pl.pallas_call(
        body, name=name,
        in_specs=[pl.BlockSpec(memory_space=pl.ANY)] * n,
        out_specs=[pl.BlockSpec(memory_space=pl.ANY)] * n,
        out_shape=[jax.ShapeDtypeStruct((N_DEV,) + a.shape, a.dtype) for a in arrays],
        scratch_shapes=[pltpu.SemaphoreType.DMA((n_sem,)), pltpu.SemaphoreType.DMA((n_sem,)),
                        pltpu.SemaphoreType.DMA((n,))],
    )(*arrays)


_HBM = pl.BlockSpec(memory_space=pltpu.HBM)
_SEM = pl.BlockSpec(memory_space=pltpu.SEMAPHORE)


def _exchange_start(name, arrays, gather, after=None):
    n = len(arrays)
    n_sem = n * (N_DEV - 1)
    me = 4 * lax.axis_index("x") + 2 * lax.axis_index("y") + lax.axis_index("c")
    lands = []
    for a in arrays:
        own = a[None] if gather else lax.dynamic_slice_in_dim(a, me, 1, 0)
        lands.append(lax.dynamic_update_slice(lax.empty((N_DEV,) + a.shape[-2:], a.dtype), own, (me, 0, 0)))
    n_after = 0 if after is None else 1

    def body(*refs):
        src, land = refs[:n], refs[n:2 * n]
        send_sems, recv_sems = refs[2 * n + n_after], refs[2 * n + n_after + 1]
        token = refs[-1]
        x, y, c = _mesh_pos()
        me_in = 4 * x + 2 * y + c
        for k in range(1, N_DEV):
            peer, pid = _peer(x, y, c, k)
            for i in range(n):
                idx = (k - 1) * n + i
                pltpu.make_async_remote_copy(src_ref=src[i] if gather else src[i].at[pid], dst_ref=land[i].at[me_in],
                                             send_sem=send_sems.at[idx], recv_sem=recv_sems.at[idx],
                                             device_id=peer, device_id_type=MESH).start()
        token[...] = jnp.zeros_like(token)

    operands = [pltpu.with_memory_space_constraint(a, pltpu.HBM) for a in list(arrays) + lands]
    outs = pl.pallas_call(
        body, name=name,
        out_shape=(pltpu.SemaphoreType.DMA((n_sem,)), pltpu.SemaphoreType.DMA((n_sem,)),
                   *[pltpu.HBM(a.shape, a.dtype) for a in list(arrays) + lands],
                   jax.ShapeDtypeStruct((8, 128), F32)),
        in_specs=[_HBM] * (2 * n) + [pl.BlockSpec(memory_space=pl.ANY)] * n_after,
        out_specs=(_SEM, _SEM, *[_HBM] * (2 * n), pl.BlockSpec(memory_space=pltpu.VMEM)),
        input_output_aliases={i: 2 + i for i in range(2 * n)},
        compiler_params=pltpu.CompilerParams(has_side_effects=pltpu.SideEffectType.DATAFLOW_SIDE_EFFECTING),
    )(*operands, *([after] if n_after else []))
    return (gather, outs[0], outs[1], outs[2:2 + n], outs[2 + n:2 + 2 * n]), outs[-1]


def _exchange_wait(name, handle, after):
    gather, send_sems, recv_sems, srcs, lands = handle
    n = len(srcs)

    def body(*refs):
        src, land = refs[:n], refs[n:2 * n]
        s_sems, r_sems = refs[2 * n], refs[2 * n + 1]
        x, y, c = _mesh_pos()
        for k in range(1, N_DEV):
            peer, pid = _peer(x, y, c, k)
            for i in range(n):
                idx = (k - 1) * n + i
                cp = pltpu.make_async_remote_copy(src_ref=src[i] if gather else src[i].at[pid], dst_ref=land[i].at[pid],
                                                  send_sem=s_sems.at[idx], recv_sem=r_sems.at[idx],
                                                  device_id=peer, device_id_type=MESH)
                cp.wait_send()
                cp.wait_recv()

    outs = pl.pallas_call(
        body, name=name,
        out_shape=tuple(pltpu.HBM(a.shape, a.dtype) for a in list(srcs) + list(lands)),
        in_specs=[_HBM] * (2 * n) + [_SEM, _SEM, pl.BlockSpec(memory_space=pl.ANY)],
        out_specs=tuple([_HBM] * (2 * n)),
        input_output_aliases={i: i for i in range(2 * n)},
        compiler_params=pltpu.CompilerParams(has_side_effects=pltpu.SideEffectType.DATAFLOW_SIDE_EFFECTING),
    )(*srcs, *lands, send_sems, recv_sems, after)
    return list(outs[n:])


def _adamw(name, parts, w, m, v):
    a_rows, b_cols = w.shape
    ta = min(a_rows, ADAM_TILE)
    assert a_rows % ta == 0
    c1 = 1.0 - ADAM_B1 ** ADAM_STEP
    c2 = 1.0 - ADAM_B2 ** ADAM_STEP

    def body(p_ref, w_ref, m_ref, v_ref, g_ref, d_ref, mo_ref, vo_ref):
        g = p_ref[0].astype(F32)
        for d in range(1, N_DEV):
            g = g + p_ref[d].astype(F32)
        g_ref[...] = g
        mn = ADAM_B1 * m_ref[...] + (1.0 - ADAM_B1) * g
        vn = ADAM_B2 * v_ref[...] + (1.0 - ADAM_B2) * (g * g)
        mo_ref[...] = mn
        vo_ref[...] = vn
        d_ref[...] = -ADAM_LR * ((mn / c1) / (jnp.sqrt(vn / c2) + ADAM_EPS) + ADAM_WD * w_ref[...])

    row = pl.BlockSpec((ta, b_cols), lambda i: (i, 0))
    return pl.pallas_call(
        body, name=name, grid=(a_rows // ta,),
        in_specs=[pl.BlockSpec((N_DEV, ta, b_cols), lambda i: (0, i, 0)), row, row, row],
        out_specs=[row] * 4,
        out_shape=[jax.ShapeDtypeStruct((a_rows, b_cols), F32)] * 4,
        compiler_params=pltpu.CompilerParams(dimension_semantics=("arbitrary",), vmem_limit_bytes=VMEM_LIMIT),
    )(parts, w, m, v)


def _pack_rows(arrays, rows):
    flat = jnp.concatenate([a.reshape(-1) for a in arrays])
    return jnp.pad(flat, (0, rows * LANES - flat.shape[0])).reshape(rows, LANES)


def _cols_from_blocks(g):
    return g.transpose(1, 0, 2).reshape(g.shape[1], N_DEV * g.shape[2])


def _blocks_from_cols(w):
    return w.reshape(w.shape[0], N_DEV, w.shape[1] // N_DEV).transpose(1, 0, 2)


def _unpack_flat(flat2d, shapes):
    flat = flat2d.reshape(-1)
    out, off = [], 0
    for shp in shapes:
        sz = math.prod(shp)
        out.append(flat[off:off + sz].reshape(shp))
        off += sz
    return out


def _s5_mats(lam_re, lam_im, log_dt, b_re, b_im, c_re, c_im):
    lr = jnp.minimum(lam_re, S5_MAX_RE)
    li = lam_im
    dt = jnp.exp(log_dt)[:, None]
    mag = jnp.exp(lr * dt)
    ang = li * dt
    ab_re = mag * jnp.cos(ang)
    ab_im = mag * jnp.sin(ang)
    den = lr * lr + li * li
    nr = ab_re - 1.0
    f_re = ((nr * lr + ab_im * li) / den)[..., None]
    f_im = ((ab_im * lr - nr * li) / den)[..., None]
    bb_re = f_re * b_re - f_im * b_im
    bb_im = f_re * b_im + f_im * b_re
    eye = jnp.eye(S5_GROUPS, dtype=F32)
    a = jnp.stack([ab_re.reshape(-1), ab_im.reshape(-1)])
    wb = jnp.concatenate([jnp.einsum("gph,gk->ghkp", bb_re, eye).reshape(S5_WIDTH, NS),
                          jnp.einsum("gph,gk->ghkp", bb_im, eye).reshape(S5_WIDTH, NS)], axis=1)
    wc = jnp.concatenate([jnp.einsum("ghp,gk->gpkh", c_re, eye).reshape(NS, S5_WIDTH),
                          -jnp.einsum("ghp,gk->gpkh", c_im, eye).reshape(NS, S5_WIDTH)], axis=0)
    return a, wb, wc


def _power_table(a):
    ar, ai = a[0], a[1]
    rows_r, rows_i = [ar], [ai]
    for _ in range(7):
        pr, pi = rows_r[-1], rows_i[-1]
        rows_r.append(pr * ar - pi * ai)
        rows_i.append(pr * ai + pi * ar)
    return jnp.concatenate([jnp.stack(rows_r), jnp.stack(rows_i)], axis=1)


def _rope_table(positions):
    inv = ROPE_THETA ** (-jnp.arange(0, MLA_ROPE, 2, dtype=F32) / MLA_ROPE)
    inv128 = jnp.concatenate([jnp.zeros((MLA_NOPE,), F32), inv, inv, jnp.zeros((32,), F32)])
    sign = jnp.concatenate([jnp.zeros((MLA_NOPE,), F32), -jnp.ones((16,), F32), jnp.ones((16,), F32),
                            jnp.zeros((32,), F32)])
    ang = positions.astype(F32)[:, None] * inv128
    return jnp.concatenate([jnp.cos(ang), jnp.sin(ang) * sign], axis=1)


def _derived_weights(full):
    w_in = full["w_in"]
    k1, k2 = w_in[:, 768:784], w_in[:, 784:800]
    z64, z32 = jnp.zeros((1024, 64), BF16), jnp.zeros((1024, 32), BF16)
    w_inx = jnp.concatenate([w_in[:, :768], z64, k1, k2, z32, z64, k2, k1, z32, w_in[:, 800:]], axis=1)
    uq = full["w_uq"].reshape(256, MLA_HEADS, MLA_QK)
    nope, r1, r2 = uq[:, :, :64], uq[:, :, 64:80], uq[:, :, 80:]
    zq64, zq32 = jnp.zeros((256, MLA_HEADS, 64), BF16), jnp.zeros((256, MLA_HEADS, 32), BF16)
    w_q2 = jnp.concatenate([jnp.concatenate([nope, r1, r2, zq32], axis=2).reshape(256, 1024),
                            jnp.concatenate([zq64, r2, r1, zq32], axis=2).reshape(256, 1024)], axis=1)
    ukv = full["w_ukv"].reshape(256, MLA_HEADS, 128)
    w_kv2 = jnp.concatenate([jnp.concatenate([ukv[:, :, :64], zq64], axis=2).reshape(256, 1024),
                             jnp.concatenate([ukv[:, :, 64:], zq64], axis=2).reshape(256, 1024)], axis=1)
    return w_inx, w_q2, w_kv2


def _oa_padded(w_oa):
    oa = w_oa.reshape(MLA_HEADS, MLA_V, 1024)
    return jnp.concatenate([oa, jnp.zeros_like(oa)], axis=1).reshape(1024, 1024)


def _fold_w_in(d_inx):
    d_k1 = d_inx[:, 832:848] + d_inx[:, 976:992]
    d_k2 = d_inx[:, 848:864] + d_inx[:, 960:976]
    return jnp.concatenate([d_inx[:, :768], d_k1, d_k2, d_inx[:, 1024:]], axis=1)


def _fold_qkv(d_q2, d_kv2):
    a = d_q2[:, :1024].reshape(256, MLA_HEADS, 128)
    b = d_q2[:, 1024:].reshape(256, MLA_HEADS, 128)
    d_w_uq = jnp.concatenate([a[:, :, :64], a[:, :, 64:80] + b[:, :, 80:96], a[:, :, 80:96] + b[:, :, 64:80]],
                             axis=2).reshape(256, MLA_HEADS * MLA_QK)
    kk = d_kv2[:, :1024].reshape(256, MLA_HEADS, 128)
    vv = d_kv2[:, 1024:].reshape(256, MLA_HEADS, 128)
    d_w_ukv = jnp.concatenate([kk[:, :, :64], vv[:, :, :64]], axis=2).reshape(256, 1024)
    return d_w_uq, d_w_ukv


def kernel(x, mem, positions, ln_in_g, ln_in_b, w_in, s5_lam_re, s5_lam_im, s5_log_dt, s5_b_re, s5_b_im, s5_c_re, s5_c_im, s5_d, w_glu, q_norm_g, w_uq, kv_norm_g, w_ukv, w_oa, w_o, ln1_g, ln1_b, w_xq, w_xk, w_xv, w_xo, ln2_g, ln2_b, w_up, w_down, ln3_g, ln3_b, loss_target, m_ln_in_g, m_ln_in_b, m_w_in, m_s5_lam_re, m_s5_lam_im, m_s5_log_dt, m_s5_b_re, m_s5_b_im, m_s5_c_re, m_s5_c_im, m_s5_d, m_w_glu, m_q_norm_g, m_w_uq, m_kv_norm_g, m_w_ukv, m_w_oa, m_w_o, m_ln1_g, m_ln1_b, m_w_xq, m_w_xk, m_w_xv, m_w_xo, m_ln2_g, m_ln2_b, m_w_up, m_w_down, m_ln3_g, m_ln3_b, v_ln_in_g, v_ln_in_b, v_w_in, v_s5_lam_re, v_s5_lam_im, v_s5_log_dt, v_s5_b_re, v_s5_b_im, v_s5_c_re, v_s5_c_im, v_s5_d, v_w_glu, v_q_norm_g, v_w_uq, v_kv_norm_g, v_w_ukv, v_w_oa, v_w_o, v_ln1_g, v_ln1_b, v_w_xq, v_w_xk, v_w_xv, v_w_xo, v_ln2_g, v_ln2_b, v_w_up, v_w_down, v_ln3_g, v_ln3_b):
    args = dict(locals())
    wts = {n: args[n] for n in WEIGHTS}
    mom = {n: args["m_" + n] for n in WEIGHTS}
    vel = {n: args["v_" + n] for n in WEIGHTS}
    xs, mems, tgt = x[0], mem[0], loss_target[0]
    s = xs.shape[0]
    names_big = [n for (n, _, _, _) in SHARDED]
    kind = {n: kd for (n, kd, _, _) in SHARDED}
    n_small = sum(math.prod(wts[n].shape) for n in SMALL)
    small_rows = _round_up(n_small + 1, 8 * LANES) // LANES

    late = [n for n in names_big if n not in EARLY]
    shard = {n: wts[n][0].astype(BF16) for n in names_big}
    late_g = [n for n in names_big if n not in GATHER_EARLY]
    g_early = dict(zip(GATHER_EARLY, _all_gather_two_level("all_gather", [shard[n] for n in GATHER_EARLY])))
    ag_handle, ag_token = _exchange_start("all_gather_late", [shard[n] for n in late_g], True, after=g_early["w_in"])
    full = {n: _cols_from_blocks(g_early[n]) for n in GATHER_EARLY}
    w_inx, w_q2, w_kv2 = _derived_weights(full)
    s5_args = (s5_lam_re[0], s5_lam_im[0], s5_log_dt[0], s5_b_re[0], s5_b_im[0], s5_c_re[0], s5_c_im[0])
    (a_mat, w_b, w_c), s5_vjp = jax.vjp(_s5_mats, *s5_args)
    w_bb, w_cb = w_b.astype(BF16), w_c.astype(BF16)
    pw = _power_table(a_mat)
    pwb = _power_table(a_mat * jnp.array([[1.0], [-1.0]], F32))[::-1]
    tab = _rope_table(positions[0])
    row = lambda a: a.reshape(1, -1)
    ln_g, ln_b = row(ln_in_g) + ag_token[0:1, 0:1], row(ln_in_b)
    lns = [ln1_g, ln1_b, ln2_g, ln2_b]

    h0, z3, gate, h, q, k, v, kt, vt = _in_fwd(xs, tab, ln_g, ln_b, q_norm_g, kv_norm_g, pw, w_inx, w_bb, w_q2, w_kv2)
    o, lse_t = _attn_fwd(q, k, vt)
    g_late = dict(zip(late_g, _exchange_wait("all_gather_late_wait", ag_handle, after=lse_t)))
    full.update({n: g_late[n].reshape(-1, g_late[n].shape[2]) if kind[n] == "row" else _cols_from_blocks(g_late[n])
                 for n in late_g})
    yl, s_out = _s5out_fwd(h, z3, s5_d, w_cb, full["w_glu"])
    w_oap = _oa_padded(full["w_oa"])
    memk, memv = _mem_kv(mems, full["w_xk"], full["w_xv"])
    r2, mixin_b, h1_b, ox_b, a_out, r1, qx_b = _post_fwd(o, s_out, gate, h0, memk, memv, lns,
                                        w_oap, full["w_o"], full["w_xq"], full["w_xo"])
    dh2, h2_b, dup_b, act_b, dff_b, loss_acc, d_ln3_g, d_ln3_b = _mlp(
        r2, tgt, ln2_g, ln2_b, ln3_g, ln3_b, full["w_up"], full["w_down"])
    (do_b, delta, d_so, dgate, dh0p, daout_b, dmix_b, dqx_b, dxa_b,
     dmk, dmv, d_ln1_g, d_ln1_b, d_ln2_g, d_ln2_b) = _post_bwd(
        dh2, o, s_out, gate, a_out, r1, r2, qx_b, memk, memv, lns, w_oap, full["w_o"], full["w_xq"], full["w_xo"])
    rows8 = lambda g: g.reshape(N_DEV, g.shape[0] // N_DEV, g.shape[1])
    d_w_oa = _wgrad("wg_oa", o, daout_b).reshape(MLA_HEADS, 128, 1024)[:, :64].reshape(512, 1024)
    send = {
        "w_oa": _blocks_from_cols(d_w_oa).astype(BF16),
        "w_up": _wgrad("wg_up", h2_b, dup_b, BF16, N_DEV),
        "w_o": rows8(_wgrad("wg_o", mixin_b, dmix_b, BF16)),
        "w_xq": rows8(_wgrad("wg_xq", h1_b, dqx_b, BF16)),
        "w_xk": rows8(_wgrad("wg_xk", mems, dmk, BF16)),
        "w_xv": rows8(_wgrad("wg_xv", mems, dmv, BF16)),
        "w_xo": rows8(_wgrad("wg_xo", ox_b, dxa_b, BF16)),
        "w_down": rows8(_wgrad("wg_down", act_b, dff_b, BF16)),
    }
    rs_handle, rs_token = _exchange_start("grad_exchange_late", [send[n] for n in late], False)
    dl_t = delta.T[:MLA_HEADS].reshape(MLA_HEADS, 1, s) + rs_token[0, 0]
    dq, dk, dv = _attn_bwd(q, k, v, kt, do_b, lse_t, dl_t)
    lam, du_p, yg_b, dy12_b, dyl_b, d_s5_d, d_a = _s5out_bwd(d_so, yl, z3, h, s5_d, pwb, w_cb, full["w_glu"])
    (dx, h0_b, dz_b, u_b, cqn_b, ckvn_b, dq2_b, dkv2_b, d_ln_g, d_ln_b, d_qg, d_kvg) = _in_bwd(
        xs, tab, z3, dq, dk, dv, lam, du_p, dgate, dh0p, ln_g, ln_b, q_norm_g, kv_norm_g, w_inx, w_bb, w_q2, w_kv2)

    d_w_in = _fold_w_in(_wgrad("wg_in", h0_b, dz_b))
    d_w_uq, d_w_ukv = _fold_qkv(_wgrad("wg_q", cqn_b, dq2_b), _wgrad("wg_kv", ckvn_b, dkv2_b))
    send.update({
        "w_in": _blocks_from_cols(d_w_in).astype(BF16), "w_uq": _blocks_from_cols(d_w_uq).astype(BF16),
        "w_ukv": _blocks_from_cols(d_w_ukv).astype(BF16), "w_glu": _wgrad("wg_glu", yg_b, dy12_b, BF16, N_DEV),
    })
    re_handle, re_token = _exchange_start("grad_exchange_early", [send[n] for n in EARLY], False)
    d_s5 = s5_vjp((d_a.reshape(2, NS), _wgrad("wg_s5b", u_b, lam, after=re_token),
                   _wgrad("wg_s5c", h, dyl_b, after=re_token)))
    small_grads = {
        "ln_in_g": d_ln_g, "ln_in_b": d_ln_b, "s5_lam_re": d_s5[0], "s5_lam_im": d_s5[1], "s5_log_dt": d_s5[2],
        "s5_b_re": d_s5[3], "s5_b_im": d_s5[4], "s5_c_re": d_s5[5], "s5_c_im": d_s5[6], "s5_d": d_s5_d,
        "q_norm_g": d_qg, "kv_norm_g": d_kvg, "ln1_g": d_ln1_g, "ln1_b": d_ln1_b, "ln2_g": d_ln2_g,
        "ln2_b": d_ln2_b, "ln3_g": d_ln3_g, "ln3_b": d_ln3_b,
    }
    recv_small = _all_gather_two_level(
        "grad_exchange", [_pack_rows([small_grads[n] for n in SMALL] + [loss_acc[0, :1]], small_rows)])[0]

    recv = dict(zip(late, _exchange_wait("grad_exchange_late_wait", rs_handle, after=recv_small)))
    results = [dict(), dict(), dict(), dict()]
    small_out = _adamw("adamw_small", recv_small,
                       *[_pack_rows([t[n] for n in SMALL], small_rows) for t in (wts, mom, vel)])
    for res, fs in zip(results, small_out):
        for n, a in zip(SMALL, _unpack_flat(fs, [wts[n].shape for n in SMALL])):
            res[n] = a
    for n in late + list(EARLY):
        if n == EARLY[0]:
            recv.update(zip(EARLY, _exchange_wait("grad_exchange_early_wait", re_handle, after=results[0][late[-1]])))
        outs = _adamw("adamw_" + n, recv[n], wts[n][0], mom[n][0], vel[n][0])
        for res, a in zip(results, outs):
            res[n] = a[None]

    loss = small_out[0].reshape(-1)[n_small]
    return (loss, dx[None], *[res[n] for res in results for n in WEIGHTS])
```

```python
import math

import jax
import jax.numpy as jnp
from jax import lax
from jax.experimental import pallas as pl
from jax.experimental.pallas import tpu as pltpu

F32 = jnp.float32
BF16 = jnp.bfloat16

D_MODEL = 1024
S5_WIDTH = 256
S5_GROUP_CH = 16
S5_GROUPS = 16
S5_STATE = 64
NS = S5_GROUPS * S5_STATE
S5_MAX_RE = -1e-4
MLA_HEADS = 8
MLA_NOPE = 64
MLA_ROPE = 32
MLA_QK = 96
MLA_V = 64
HEAD_PAD = 128
ROPE_THETA = 10000.0
XATTN_HEADS = 4
XATTN_HD = 256
MLP_HIDDEN = 4096
LN_EPS = 1e-5
RMS_EPS = 1e-6
NEG_INF = -1e30
LOG2E = 1.4426950408889634
DN_ALPHA = 2.0 ** 0.25
ADAM_LR = 0.001
ADAM_B1 = 0.9
ADAM_B2 = 0.999
ADAM_EPS = 1e-08
ADAM_WD = 0.01
ADAM_STEP = 10

N_DEV = 8
MESH = pl.DeviceIdType.MESH
LANES = 1024
VMEM_LIMIT = 60 * 1024 * 1024

ROW_TILE = 256
ATT_TILE = 1024
ADAM_TILE = 256

SHARDED = (
    ("w_in", "col", 1024, 2848), ("w_glu", "col", 256, 2048), ("w_uq", "col", 256, 768),
    ("w_ukv", "col", 256, 1024), ("w_oa", "col", 512, 1024), ("w_o", "row", 1024, 1024),
    ("w_xq", "row", 1024, 1024), ("w_xk", "row", 1024, 1024), ("w_xv", "row", 1024, 1024),
    ("w_xo", "row", 1024, 1024), ("w_up", "col", 1024, 4096), ("w_down", "row", 4096, 1024),
)
EARLY = ("w_in", "w_glu", "w_uq", "w_ukv")
SMALL = ("ln_in_g", "ln_in_b", "s5_lam_re", "s5_lam_im", "s5_log_dt", "s5_b_re", "s5_b_im", "s5_c_re",
         "s5_c_im", "s5_d", "q_norm_g", "kv_norm_g", "ln1_g", "ln1_b", "ln2_g", "ln2_b", "ln3_g", "ln3_b")
WEIGHTS = ("ln_in_g", "ln_in_b", "w_in", "s5_lam_re", "s5_lam_im", "s5_log_dt", "s5_b_re", "s5_b_im",
           "s5_c_re", "s5_c_im", "s5_d", "w_glu", "q_norm_g", "w_uq", "kv_norm_g", "w_ukv", "w_oa", "w_o",
           "ln1_g", "ln1_b", "w_xq", "w_xk", "w_xv", "w_xo", "ln2_g", "ln2_b", "w_up", "w_down", "ln3_g", "ln3_b")


def _round_up(n, m):
    return (n + m - 1) // m * m


def _bf(a):
    return a.astype(BF16)


def _mm(a, b):
    return jnp.dot(_bf(a), _bf(b), preferred_element_type=F32)


def _mm_nt(a, b):
    return lax.dot_general(_bf(a), _bf(b), (((1,), (1,)), ((), ())), preferred_element_type=F32)


def _mm_tn(a, b):
    return lax.dot_general(_bf(a), _bf(b), (((0,), (0,)), ((), ())), preferred_element_type=F32)


def _sigmoid(a):
    return 1.0 / (1.0 + jnp.exp(-a))


def _gelu(a):
    return 0.5 * a * (1.0 + lax.erf(a * (2.0 ** -0.5)))


def _gelu_grad(a):
    return 0.5 * (1.0 + lax.erf(a * (2.0 ** -0.5))) + a * jnp.exp(-0.5 * a * a) * (1.0 / math.sqrt(2.0 * math.pi))


def _ln_fwd(a, g, b):
    mu = jnp.mean(a, axis=-1, keepdims=True)
    ac = a - mu
    var = jnp.mean(ac * ac, axis=-1, keepdims=True)
    rstd = lax.rsqrt(var + LN_EPS)
    xhat = ac * rstd
    return xhat * g + b, xhat, rstd


def _ln_bwd(dy, xhat, rstd, g):
    dxh = dy * g
    m1 = jnp.mean(dxh, axis=-1, keepdims=True)
    m2 = jnp.mean(dxh * xhat, axis=-1, keepdims=True)
    dx = rstd * (dxh - m1 - xhat * m2)
    return dx, jnp.sum(dy * xhat, axis=0, keepdims=True), jnp.sum(dy, axis=0, keepdims=True)


def _rms_fwd(a, g):
    r = lax.rsqrt(jnp.mean(a * a, axis=-1, keepdims=True) + RMS_EPS)
    xn = a * r
    return xn * g, xn, r


def _rms_bwd(dy, xn, r, g):
    dxn = dy * g
    dx = r * (dxn - xn * jnp.mean(dxn * xn, axis=-1, keepdims=True))
    return dx, jnp.sum(dy * xn, axis=0, keepdims=True)


def _tile_heads(a):
    return jnp.concatenate([a] * MLA_HEADS, axis=1)


def _row_call(name, body, n_rows, ts, tiled_in, full_in, weights, tiled_out, acc_out, reverse=False, scratch=(),
              tiled_out_t=()):
    n = n_rows // ts
    assert n * ts == n_rows, (name, n_rows, ts)
    nt, nf, nw = len(tiled_in), len(full_in), len(weights)
    nto, nao = len(tiled_out) + len(tiled_out_t), len(acc_out)
    if reverse:
        imap = lambda i: (n - 1 - i, 0)
    else:
        imap = lambda i: (i, 0)
    const = lambda i: (0, 0)

    def kern(*refs):
        ins = refs[:nt + nf]
        w_hbm = refs[nt + nf:nt + nf + nw]
        outs = refs[nt + nf + nw:nt + nf + nw + nto + nao]
        scr = refs[nt + nf + nw + nto + nao:]
        w_vmem = scr[:nw]
        extra = scr[nw + 1:] if nw else scr
        if nw:
            sem = scr[nw]

            @pl.when(pl.program_id(0) == 0)
            def _():
                cps = [pltpu.make_async_copy(w_hbm[k], w_vmem[k], sem.at[k]) for k in range(nw)]
                for cp in cps:
                    cp.start()
                for cp in cps:
                    cp.wait()
        body(*ins, *w_vmem, *outs, *extra)

    in_specs = [pl.BlockSpec((ts, a.shape[1]), imap) for a in tiled_in]
    in_specs += [pl.BlockSpec(a.shape, const) for a in full_in]
    in_specs += [pl.BlockSpec(memory_space=pl.ANY) for _ in weights]
    assert not (reverse and tiled_out_t)
    out_shape = [jax.ShapeDtypeStruct((n_rows, c), dt) for (c, dt) in tiled_out]
    out_shape += [jax.ShapeDtypeStruct((c, n_rows), dt) for (c, dt) in tiled_out_t]
    out_shape += [jax.ShapeDtypeStruct(shp, F32) for shp in acc_out]
    out_specs = [pl.BlockSpec((ts, c), imap) for (c, dt) in tiled_out]
    out_specs += [pl.BlockSpec((c, ts), lambda i: (0, i)) for (c, dt) in tiled_out_t]
    out_specs += [pl.BlockSpec(shp, const) for shp in acc_out]
    scratch_shapes = [pltpu.VMEM(w.shape, w.dtype) for w in weights]
    if nw:
        scratch_shapes.append(pltpu.SemaphoreType.DMA((nw,)))
    scratch_shapes += list(scratch)
    return pl.pallas_call(
        kern, name=name, grid=(n,), in_specs=in_specs, out_specs=out_specs, out_shape=out_shape,
        scratch_shapes=scratch_shapes,
        compiler_params=pltpu.CompilerParams(dimension_semantics=("arbitrary",), vmem_limit_bytes=VMEM_LIMIT),
    )(*tiled_in, *full_in, *weights)


def _mem_kv(mem, w_xk, w_xv):
    m = mem.shape[0]

    def body(mem_ref, wk_ref, wv_ref, k_ref, v_ref):
        mb = mem_ref[...]
        k_ref[...] = _mm(mb, wk_ref[...]).astype(BF16)
        v_ref[...] = _mm(mb, wv_ref[...]).astype(BF16)

    return pl.pallas_call(
        body, name="mem_kv",
        out_shape=[jax.ShapeDtypeStruct((m, D_MODEL), BF16)] * 2,
        compiler_params=pltpu.CompilerParams(vmem_limit_bytes=VMEM_LIMIT),
    )(mem, w_xk, w_xv)


def _in_fwd(x, tab, ln_g, ln_b, qg, kvg, pw, w_inx, w_b, w_q2, w_kv2):
    ts = min(ROW_TILE, x.shape[0])

    def body(x_ref, tab_ref, lng_ref, lnb_ref, qg_ref, kvg_ref, pw_ref, winx, wb, wq2, wkv2,
             h0_ref, z3_ref, gate_ref, h_ref, q_ref, k_ref, v_ref, kt_ref, vt_ref, bu_scr, carry):
        @pl.when(pl.program_id(0) == 0)
        def _():
            carry[...] = jnp.zeros_like(carry)

        h0, _, _ = _ln_fwd(x_ref[...], lng_ref[...], lnb_ref[...])
        h0_ref[...] = h0
        z = _mm(h0, winx[...])
        z3_ref[...] = z[:, :768]
        gate_ref[...] = z[:, 1024:].astype(BF16)
        bu_scr[...] = _mm(z[:, :256], wb[...])
        _scan_fwd_rows(pw_ref, bu_scr, h_ref, carry, ts // 8)
        tab_v = tab_ref[...]
        cq1, cq2, ck1 = tab_v[:, :128], tab_v[:, 128:], tab_v[:, :128]
        cqn, _, _ = _rms_fwd(z[:, 256:512], qg_ref[...])
        q2 = _mm(cqn, wq2[...])
        q_ref[...] = (q2[:, :1024] * _tile_heads(cq1) + q2[:, 1024:] * _tile_heads(cq2)).astype(BF16)
        ckvn, _, _ = _rms_fwd(z[:, 512:768], kvg_ref[...])
        kv2 = _mm(ckvn, wkv2[...])
        krp = z[:, 768:896] * ck1 + z[:, 896:1024] * cq2
        kf = kv2[:, :1024] + _tile_heads(krp)
        k_ref[...] = kf.astype(BF16)
        v_ref[...] = kv2[:, 1024:].astype(BF16)
        kt_ref[...] = kf.T.astype(BF16)
        vt_ref[...] = kv2[:, 1024:].T.astype(BF16)

    s = x.shape[0]
    return _row_call(
        "in_fwd", body, s, ts, [x, tab], [ln_g, ln_b, qg, kvg, pw], [w_inx, w_b, w_q2, w_kv2],
        [(1024, F32), (768, F32), (2048, BF16), (2048, F32), (1024, BF16), (1024, BF16), (1024, BF16)], [],
        tiled_out_t=[(1024, BF16), (1024, BF16)], scratch=[pltpu.VMEM((ts, 2 * NS), F32), pltpu.VMEM((8, 2 * NS), F32)])


def _scan_coeffs(pw_ref, rows, forward):
    row = lax.broadcasted_iota(jnp.int32, (8, NS), 0)
    out = []
    for k, r in zip((1, 2, 4), rows):
        keep = (row >= k) if forward else (row < 8 - k)
        out.append((jnp.where(keep, pw_ref[r:r + 1, :NS], 0.0), jnp.where(keep, pw_ref[r:r + 1, NS:], 0.0), k))
    return out


def _scan_level(xr, xi, coeff, forward):
    ar, ai, k = coeff
    shift = k if forward else 8 - k
    sr, si = pltpu.roll(xr, shift, 0), pltpu.roll(xi, shift, 0)
    return xr + ar * sr - ai * si, xi + ar * si + ai * sr


def _scan_fwd_rows(pw_ref, bu_ref, h_ref, carry, nblk):
    coeffs = _scan_coeffs(pw_ref, (0, 1, 3), True)

    def blk(b, _):
        r0 = pl.multiple_of(b * 8, 8)
        xr = bu_ref[pl.ds(r0, 8), :NS]
        xi = bu_ref[pl.ds(r0, 8), NS:]
        for coeff in coeffs:
            xr, xi = _scan_level(xr, xi, coeff, True)
        cr, ci = carry[7:8, :NS], carry[7:8, NS:]
        pr, pi = pw_ref[:, :NS], pw_ref[:, NS:]
        hr = xr + pr * cr - pi * ci
        hi = xi + pr * ci + pi * cr
        h_ref[pl.ds(r0, 8), :NS] = hr
        h_ref[pl.ds(r0, 8), NS:] = hi
        carry[:, :NS] = hr
        carry[:, NS:] = hi
        return 0

    lax.fori_loop(0, nblk, blk, 0)


def _scan_bwd_rows(pw_ref, g_ref, h_ref, lam_ref, carry, acc, nblk):
    row = lax.broadcasted_iota(jnp.int32, (8, NS), 0)
    coeffs = _scan_coeffs(pw_ref, (7, 6, 4), False)

    def blk(bb, _):
        r0 = pl.multiple_of((nblk - 1 - bb) * 8, 8)
        xr = g_ref[pl.ds(r0, 8), :NS]
        xi = g_ref[pl.ds(r0, 8), NS:]
        for coeff in coeffs:
            xr, xi = _scan_level(xr, xi, coeff, False)
        cr, ci = carry[0:1, :NS], carry[0:1, NS:]
        pr, pi = pw_ref[:, :NS], pw_ref[:, NS:]
        lr = xr + pr * cr - pi * ci
        li = xi + pr * ci + pi * cr
        lam_ref[pl.ds(r0, 8), :NS] = lr
        lam_ref[pl.ds(r0, 8), NS:] = li
        nr = jnp.where(row < 7, pltpu.roll(lr, 7, 0), cr)
        ni = jnp.where(row < 7, pltpu.roll(li, 7, 0), ci)
        hr = h_ref[pl.ds(r0, 8), :NS]
        hi = h_ref[pl.ds(r0, 8), NS:]
        acc[:, :NS] += nr * hr + ni * hi
        acc[:, NS:] += ni * hr - nr * hi
        carry[:, :NS] = lr
        carry[:, NS:] = li
        return 0

    lax.fori_loop(0, nblk, blk, 0)


def _s5out_fwd(h, z3, d_skip, w_c, w_glu):
    def body(h_ref, z3_ref, d_ref, wc, wglu, yl_ref, so_ref):
        yl = _mm(h_ref[...], wc[...]) + d_ref[...] * z3_ref[:, :256]
        yl_ref[...] = yl
        y12 = _mm(_gelu(yl), wglu[...])
        so_ref[...] = y12[:, :1024] * _sigmoid(y12[:, 1024:])

    s = h.shape[0]
    return _row_call("s5out_fwd", body, s, min(ROW_TILE, s), [h, z3], [d_skip], [w_c, w_glu],
                     [(256, F32), (1024, F32)], [])


def _causal_mask_t(t):
    row = lax.broadcasted_iota(jnp.int32, (t, t), 0)
    col = lax.broadcasted_iota(jnp.int32, (t, t), 1)
    return row <= col


def _attn_fwd(q, k, vt):
    s = q.shape[0]
    t = min(ATT_TILE, s)
    nq = s // t
    scale = MLA_QK ** -0.5
    c2 = scale * LOG2E

    def body(q_ref, k_ref, vt_ref, o_ref, lse_ref, s0, s1):
        i = pl.program_id(1)
        qb = q_ref[...]

        def scores(kb, dst):
            st = _mm_nt(k_ref[pl.ds(pl.multiple_of(kb * t, t), t), :], qb)
            dst[...] = st
            return jnp.max(st, axis=0, keepdims=True)

        def update(kb, src, mt, state, masked):
            m, l, acc = state
            if masked:
                keep = _causal_mask_t(t)
                mt = jnp.max(jnp.where(keep, src[...], NEG_INF), axis=0, keepdims=True)
            m_new = jnp.maximum(m, mt)
            if masked:
                p = jnp.exp2((jnp.where(keep, src[...], NEG_INF) - m_new) * c2)
            else:
                p = jnp.exp2((src[...] - m_new) * c2)
            a = jnp.exp2((m - m_new) * c2)
            l = a * l + jnp.sum(p, axis=0, keepdims=True)
            acc = a * acc + _mm(vt_ref[:, pl.ds(pl.multiple_of(kb * t, t), t)], p)
            return m_new, l, acc

        def pair(jj, carry):
            mt_a, state = carry
            mt_b = scores(2 * jj + 1, s1)
            state = update(2 * jj, s0, mt_a, state, False)
            mt_a = scores(2 * jj + 2, s0)
            return mt_a, update(2 * jj + 1, s1, mt_b, state, False)

        def odd_tail(carry):
            mt_a, state = carry
            mt_b = scores(i, s1)
            state = update(i - 1, s0, mt_a, state, False)
            return update(i, s1, mt_b, state, True)

        def even_tail(carry):
            mt_a, state = carry
            return update(i, s0, mt_a, state, True)

        init = (jnp.full((1, t), NEG_INF, F32), jnp.zeros((1, t), F32), jnp.zeros((HEAD_PAD, t), F32))
        carry = lax.fori_loop(0, i // 2, pair, (scores(0, s0), init))
        m, l, acc = lax.cond(i % 2 == 1, odd_tail, even_tail, carry)
        o_ref[...] = (acc / l).T
        lse_ref[...] = m * scale + jnp.log(l)

    return pl.pallas_call(
        body, name="attn_fwd", grid=(MLA_HEADS, nq),
        in_specs=[pl.BlockSpec((t, HEAD_PAD), lambda h, i: (i, h)),
                  pl.BlockSpec((s, HEAD_PAD), lambda h, i: (0, h)),
                  pl.BlockSpec((HEAD_PAD, s), lambda h, i: (h, 0))],
        out_specs=[pl.BlockSpec((t, HEAD_PAD), lambda h, i: (i, h)),
                   pl.BlockSpec((None, 1, t), lambda h, i: (h, 0, i))],
        out_shape=[jax.ShapeDtypeStruct((s, MLA_HEADS * HEAD_PAD), F32),
                   jax.ShapeDtypeStruct((MLA_HEADS, 1, s), F32)],
        scratch_shapes=[pltpu.VMEM((t, t), F32)] * 2,
        compiler_params=pltpu.CompilerParams(dimension_semantics=("arbitrary", "arbitrary"),
                                             vmem_limit_bytes=VMEM_LIMIT),
    )(q, k, vt)


def _attn_bwd(q, k, v, kt, do, lse_t, dl_t):
    s = q.shape[0]
    t = min(ATT_TILE, s)
    nq = s // t
    scale = MLA_QK ** -0.5
    c2 = scale * LOG2E

    def body(q_ref, k_ref, v_ref, kt_ref, do_ref, lse_ref, dl_ref, dq_ref, dk_ref, dv_ref, dqt, s0, p0, s1, p1):
        j = pl.program_id(1)
        n = nq - 1 - j

        @pl.when(j == 0)
        def _():
            dqt[...] = jnp.zeros_like(dqt)

        kk = k_ref[...]
        vv = v_ref[...]
        ktb = kt_ref[:MLA_QK, :]

        def rows(m):
            return pl.ds(pl.multiple_of(jnp.where(m < n, j + 1 + m, j) * t, t), t)

        def first(m, sbuf, pbuf):
            r = rows(m)
            sbuf[...] = _mm_nt(kk, q_ref[r, :])
            pbuf[...] = _mm_nt(vv, do_ref[r, :])

        def finish(m, sbuf, pbuf, acc, masked):
            dk, dv = acc
            r = rows(m)
            pt = jnp.exp2(sbuf[...] * c2 - lse_ref[:, r] * LOG2E)
            if masked:
                pt = jnp.where(_causal_mask_t(t), pt, 0.0)
            dv = dv + _mm(pt, do_ref[r, :])
            dst = (pt * (pbuf[...] - dl_ref[:, r])).astype(BF16)
            dk = dk + _mm(dst, q_ref[r, :])
            dqt[:MLA_QK, r] += _mm(ktb, dst)
            return dk, dv

        def pair(jj, acc):
            first(2 * jj + 1, s1, p1)
            acc = finish(2 * jj, s0, p0, acc, False)
            first(2 * jj + 2, s0, p0)
            return finish(2 * jj + 1, s1, p1, acc, False)

        def odd_tail(acc):
            first(n, s1, p1)
            acc = finish(n - 1, s0, p0, acc, False)
            return finish(n, s1, p1, acc, True)

        def even_tail(acc):
            return finish(n, s0, p0, acc, True)

        zero = jnp.zeros((t, HEAD_PAD), F32)
        first(0, s0, p0)
        acc = lax.fori_loop(0, n // 2, pair, (zero, zero))
        dk, dv = lax.cond(n % 2 == 1, odd_tail, even_tail, acc)
        dk_ref[...] = (dk * scale).astype(BF16)
        dv_ref[...] = dv.astype(BF16)

        @pl.when(j == nq - 1)
        def _():
            for cc in range(nq):
                dq_ref[cc * t:(cc + 1) * t, :] = (dqt[:, cc * t:(cc + 1) * t].T * scale).astype(BF16)

    full = pl.BlockSpec((s, HEAD_PAD), lambda h, j: (0, h))
    tile = pl.BlockSpec((t, HEAD_PAD), lambda h, j: (j, h))
    stat = pl.BlockSpec((None, 1, s), lambda h, j: (h, 0, 0))
    return pl.pallas_call(
        body, name="attn_bwd", grid=(MLA_HEADS, nq),
        in_specs=[full, tile, tile, pl.BlockSpec((HEAD_PAD, t), lambda h, j: (h, j)), full, stat, stat],
        out_specs=[full, tile, tile],
        out_shape=[jax.ShapeDtypeStruct((s, MLA_HEADS * HEAD_PAD), BF16)] * 3,
        scratch_shapes=[pltpu.VMEM((HEAD_PAD, s), F32)] + [pltpu.VMEM((t, t), F32)] * 4,
        compiler_params=pltpu.CompilerParams(dimension_semantics=("arbitrary", "arbitrary"),
                                             vmem_limit_bytes=VMEM_LIMIT),
    )(q, k, v, kt, do, lse_t, dl_t)


def _xattn_probs(qxb, memk, hh):
    sl = slice(hh * XATTN_HD, (hh + 1) * XATTN_HD)
    sc = _mm_nt(qxb[:, sl], memk[:, sl]) * (XATTN_HD ** -0.5)
    e = jnp.exp(sc - jnp.max(sc, axis=1, keepdims=True))
    return e / jnp.sum(e, axis=1, keepdims=True)


def _post_forward(o, s_out, gate, h0, memk, memv, w_oa, w_o, w_xq, w_xo, g1, b1, g2, b2):
    a_out = _mm(o, w_oa)
    sg_s = _sigmoid(gate[:, :1024])
    sg_a = _sigmoid(gate[:, 1024:])
    mixin = sg_s * s_out + sg_a * a_out
    r1 = DN_ALPHA * h0 + _mm(mixin, w_o)
    h1, xh1, rs1 = _ln_fwd(r1, g1, b1)
    qxb = _mm(h1, w_xq).astype(BF16)
    ox = jnp.concatenate([_mm(_xattn_probs(qxb, memk, hh), memv[:, hh * XATTN_HD:(hh + 1) * XATTN_HD])
                          for hh in range(XATTN_HEADS)], axis=1)
    r2 = DN_ALPHA * h1 + _mm(ox, w_xo)
    return dict(a_out=a_out, mixin=mixin, r1=r1, h1=h1, qxb=qxb, ox=ox, r2=r2)


def _post_fwd(o, s_out, gate, h0, memk, memv, lns, w_oa, w_o, w_xq, w_xo):
    def body(o_ref, so_ref, gate_ref, h0_ref, mk_ref, mv_ref, g1, b1, g2, b2, woa, wo, wxq, wxo,
             r2_ref, mixin_ref, h1_ref, ox_ref, aout_ref, r1_ref, qx_ref):
        f = _post_forward(o_ref[...], so_ref[...], gate_ref[...].astype(F32), h0_ref[...], mk_ref[...], mv_ref[...],
                          woa[...], wo[...], wxq[...], wxo[...], g1[...], b1[...], g2[...], b2[...])
        r2_ref[...] = f["r2"]
        mixin_ref[...] = f["mixin"].astype(BF16)
        h1_ref[...] = f["h1"].astype(BF16)
        ox_ref[...] = f["ox"].astype(BF16)
        aout_ref[...] = f["a_out"]
        r1_ref[...] = f["r1"]
        qx_ref[...] = f["qxb"]

    s = o.shape[0]
    return _row_call("post_fwd", body, s, min(ROW_TILE, s), [o, s_out, gate, h0], [memk, memv, *lns],
                     [w_oa, w_o, w_xq, w_xo],
                     [(1024, F32), (1024, BF16), (1024, BF16), (1024, BF16), (1024, F32), (1024, F32), (1024, BF16)], [])


def _mlp(r2, target, g2, b2, g3, b3, w_up, w_down):
    cw = 1024
    n_chunk = MLP_HIDDEN // cw

    def body(r2_ref, tgt_ref, g2_ref, b2_ref, g3_ref, b3_ref, wup, wdn,
             dh2_ref, h2b_ref, dup_ref, act_ref, dff_ref, loss_ref, dg3_ref, db3_ref, up_scr):
        @pl.when(pl.program_id(0) == 0)
        def _():
            loss_ref[...] = jnp.zeros_like(loss_ref)
            dg3_ref[...] = jnp.zeros_like(dg3_ref)
            db3_ref[...] = jnp.zeros_like(db3_ref)

        h2, _, _ = _ln_fwd(r2_ref[...], g2_ref[...], b2_ref[...])
        h2b = h2.astype(BF16)
        h2b_ref[...] = h2b
        ff = jnp.zeros(h2.shape, F32)
        for c in range(n_chunk):
            sl = slice(c * cw, (c + 1) * cw)
            a = jnp.maximum(_mm(h2b, wup[:, sl]), 0.0)
            up_scr[:, sl] = a
            actb = (a * a).astype(BF16)
            act_ref[:, sl] = actb
            ff = ff + _mm(actb, wdn[sl, :])
        h3, xh3, rs3 = _ln_fwd(DN_ALPHA * h2 + ff, g3_ref[...], b3_ref[...])
        err = h3 - tgt_ref[...]
        loss_ref[...] += 0.5 * jnp.sum(err * err) * (1.0 / D_MODEL)
        dr3, dg3, db3 = _ln_bwd(err * (1.0 / D_MODEL), xh3, rs3, g3_ref[...])
        dg3_ref[...] += dg3
        db3_ref[...] += db3
        dffb = dr3.astype(BF16)
        dff_ref[...] = dffb
        dh2 = DN_ALPHA * dr3
        for c in range(n_chunk):
            sl = slice(c * cw, (c + 1) * cw)
            dupb = (_mm_nt(dffb, wdn[sl, :]) * (2.0 * up_scr[:, sl])).astype(BF16)
            dup_ref[:, sl] = dupb
            dh2 = dh2 + _mm_nt(dupb, wup[:, sl])
        dh2_ref[...] = dh2

    s = r2.shape[0]
    ts = min(ROW_TILE, s)
    return _row_call("mlp", body, s, ts, [r2, target], [g2, b2, g3, b3], [w_up, w_down],
                     [(1024, F32), (1024, BF16), (MLP_HIDDEN, BF16), (MLP_HIDDEN, BF16), (1024, BF16)],
                     [(8, 128), (1, 1024), (1, 1024)], scratch=[pltpu.VMEM((ts, MLP_HIDDEN), F32)])


def _post_bwd(dh2, o, s_out, gate, a_out, r1, r2, qx, memk, memv, lns, w_oa, w_o, w_xq, w_xo):
    def body(dh2_ref, o_ref, so_ref, gate_ref, aout_ref, r1_ref, r2_ref, qx_ref, mk_ref, mv_ref, g1, b1, g2, b2,
             woa, wo, wxq, wxo,
             do_ref, dl_ref, dso_ref, dgate_ref, dh0_ref, daout_ref, dmix_ref, dqx_ref, dxa_ref,
             dmk_ref, dmv_ref, dg1_ref, db1_ref, dg2_ref, db2_ref):
        @pl.when(pl.program_id(0) == 0)
        def _():
            for r in (dmk_ref, dmv_ref, dg1_ref, db1_ref, dg2_ref, db2_ref):
                r[...] = jnp.zeros_like(r)

        o = o_ref[...]
        s_out = so_ref[...]
        memk, memv = mk_ref[...], mv_ref[...]
        qxb = qx_ref[...]
        _, xh1, rs1 = _ln_fwd(r1_ref[...], g1[...], b1[...])
        _, xh2, rs2 = _ln_fwd(r2_ref[...], g2[...], b2[...])
        dr2, dg2, db2 = _ln_bwd(dh2_ref[...], xh2, rs2, g2[...])
        dg2_ref[...] += dg2
        db2_ref[...] += db2
        dxab = dr2.astype(BF16)
        dxa_ref[...] = dxab
        dox = _mm_nt(dxab, wxo[...])
        dqs = []
        for hh in range(XATTN_HEADS):
            sl = slice(hh * XATTN_HD, (hh + 1) * XATTN_HD)
            p = _xattn_probs(qxb, memk, hh)
            doxh = dox[:, sl].astype(BF16)
            dp = _mm_nt(doxh, memv[:, sl])
            ds = (p * (dp - jnp.sum(dp * p, axis=1, keepdims=True)) * (XATTN_HD ** -0.5)).astype(BF16)
            dqs.append(_mm(ds, memk[:, sl]))
            dmk_ref[:, sl] += _mm_tn(ds, qxb[:, sl])
            dmv_ref[:, sl] += _mm_tn(p, doxh)
        dqxb = jnp.concatenate(dqs, axis=1).astype(BF16)
        dqx_ref[...] = dqxb
        dh1 = DN_ALPHA * dr2 + _mm_nt(dqxb, wxq[...])
        dr1, dg1, db1 = _ln_bwd(dh1, xh1, rs1, g1[...])
        dg1_ref[...] += dg1
        db1_ref[...] += db1
        dh0_ref[...] = DN_ALPHA * dr1
        dmixb = dr1.astype(BF16)
        dmix_ref[...] = dmixb
        dmixin = _mm_nt(dmixb, wo[...])
        sg_s = _sigmoid(gate_ref[:, :1024].astype(F32))
        sg_a = _sigmoid(gate_ref[:, 1024:].astype(F32))
        dso_ref[...] = dmixin * sg_s
        daoutb = (dmixin * sg_a).astype(BF16)
        daout_ref[...] = daoutb
        dgate_ref[:, :1024] = (dmixin * s_out * sg_s * (1.0 - sg_s)).astype(BF16)
        dgate_ref[:, 1024:] = (dmixin * aout_ref[...] * sg_a * (1.0 - sg_a)).astype(BF16)
        d_o = _mm_nt(daoutb, woa[...])
        do_ref[...] = d_o.astype(BF16)
        lane = lax.broadcasted_iota(jnp.int32, (o.shape[0], HEAD_PAD), 1)
        dlc = jnp.zeros((o.shape[0], HEAD_PAD), F32)
        for hh in range(MLA_HEADS):
            sl = slice(hh * HEAD_PAD, (hh + 1) * HEAD_PAD)
            dl = jnp.sum(d_o[:, sl] * o[:, sl], axis=1, keepdims=True)
            dlc = dlc + jnp.where(lane == hh, dl, 0.0)
        dl_ref[...] = dlc

    s = o.shape[0]
    m = memk.shape[0]
    return _row_call(
        "post_bwd", body, s, min(ROW_TILE, s), [dh2, o, s_out, gate, a_out, r1, r2, qx], [memk, memv, *lns],
        [w_oa, w_o, w_xq, w_xo],
        [(1024, BF16), (HEAD_PAD, F32), (1024, F32), (2048, BF16), (1024, F32),
         (1024, BF16), (1024, BF16), (1024, BF16), (1024, BF16)],
        [(m, 1024), (m, 1024), (1, 1024), (1, 1024), (1, 1024), (1, 1024)])


def _s5out_bwd(d_so, yl, z3, h, d_skip, pwb, w_c, w_glu):
    s = d_so.shape[0]
    ts = min(ROW_TILE, s)
    n_tiles = s // ts

    def body(dso_ref, yl_ref, z3_ref, h_ref, d_ref, pw_ref, wc, wglu,
             lam_ref, dup_ref, dd_ref, da_ref, dwglu_ref, dwc_ref, gh_scr, carry, acc):
        @pl.when(pl.program_id(0) == 0)
        def _():
            dd_ref[...] = jnp.zeros_like(dd_ref)
            dwglu_ref[...] = jnp.zeros_like(dwglu_ref)
            dwc_ref[...] = jnp.zeros_like(dwc_ref)
            carry[...] = jnp.zeros_like(carry)
            acc[...] = jnp.zeros_like(acc)

        yl = yl_ref[...]
        ygb = _gelu(yl).astype(BF16)
        y12 = _mm(ygb, wglu[...])
        sg = _sigmoid(y12[:, 1024:])
        dso = dso_ref[...]
        dy12b = jnp.concatenate([dso * sg, dso * y12[:, :1024] * sg * (1.0 - sg)], axis=1).astype(BF16)
        dwglu_ref[...] += _mm_tn(ygb, dy12b)
        dyl = _mm_nt(dy12b, wglu[...]) * _gelu_grad(yl)
        dylb = dyl.astype(BF16)
        dwc_ref[...] += _mm_tn(h_ref[...], dylb)
        gh_scr[...] = _mm_nt(dylb, wc[...])
        _scan_bwd_rows(pw_ref, gh_scr, h_ref, lam_ref, carry, acc, ts // 8)
        dup_ref[...] = dyl * d_ref[...]
        dd_ref[...] += jnp.sum(dyl * z3_ref[:, :256], axis=0, keepdims=True)

        @pl.when(pl.program_id(0) == n_tiles - 1)
        def _():
            da_ref[...] = jnp.sum(acc[...], axis=0, keepdims=True)

    return _row_call("s5out_bwd", body, s, ts, [d_so, yl, z3, h], [d_skip, pwb], [w_c, w_glu],
                     [(2 * NS, F32), (256, F32)], [(1, 256), (1, 2 * NS), (S5_WIDTH, 2048), (2 * NS, S5_WIDTH)],
                     reverse=True, scratch=[pltpu.VMEM((ts, 2 * NS), F32), pltpu.VMEM((8, 2 * NS), F32),
                                            pltpu.VMEM((8, 2 * NS), F32)])


def _in_bwd(x, tab, z3, dq, dk, dv, lam, du_p, dgate, dh0p, ln_g, ln_b, qg, kvg, w_inx, w_b, w_q2, w_kv2):
    def body(x_ref, tab_ref, z3_ref, dq_ref, dk_ref, dv_ref, lam_ref, dup_ref, dgate_ref, dh0p_ref,
             lng_ref, lnb_ref, qg_ref, kvg_ref, winx, wb, wq2, wkv2,
             dx_ref, h0b_ref, dz_ref, dlng_ref, dlnb_ref, dqg_ref, dkvg_ref, dwq_ref, dwkv_ref, dwb_ref):
        @pl.when(pl.program_id(0) == 0)
        def _():
            for r in (dlng_ref, dlnb_ref, dqg_ref, dkvg_ref, dwq_ref, dwkv_ref, dwb_ref):
                r[...] = jnp.zeros_like(r)

        h0, xh0, rs0 = _ln_fwd(x_ref[...], lng_ref[...], lnb_ref[...])
        h0b_ref[...] = h0.astype(BF16)
        z3 = z3_ref[...]
        tab_v = tab_ref[...]
        cq1, cq2, ck1 = tab_v[:, :128], tab_v[:, 128:], tab_v[:, :128]
        lamb = lam_ref[...].astype(BF16)
        du = _mm_nt(lamb, wb[...]) + dup_ref[...]
        dwb_ref[...] += _mm_tn(z3[:, :256], lamb)
        dq = dq_ref[...].astype(F32)
        dq2b = jnp.concatenate([dq * _tile_heads(cq1), dq * _tile_heads(cq2)], axis=1).astype(BF16)
        cqg, cqn, rq = _rms_fwd(z3[:, 256:512], qg_ref[...])
        dwq_ref[...] += _mm_tn(cqg, dq2b)
        dcq, dqg = _rms_bwd(_mm_nt(dq2b, wq2[...]), cqn, rq, qg_ref[...])
        dqg_ref[...] += dqg
        dkv2b = jnp.concatenate([dk_ref[...], dv_ref[...]], axis=1)
        dk = dk_ref[...].astype(F32)
        ckvg, ckvn, rkv = _rms_fwd(z3[:, 512:768], kvg_ref[...])
        dwkv_ref[...] += _mm_tn(ckvg, dkv2b)
        dckv, dkvg = _rms_bwd(_mm_nt(dkv2b, wkv2[...]), ckvn, rkv, kvg_ref[...])
        dkvg_ref[...] += dkvg
        dkrp = dk[:, :HEAD_PAD]
        for hh in range(1, MLA_HEADS):
            dkrp = dkrp + dk[:, hh * HEAD_PAD:(hh + 1) * HEAD_PAD]
        dzb = jnp.concatenate([a.astype(BF16) for a in (du, dcq, dckv, dkrp * ck1, dkrp * cq2)] + [dgate_ref[...]],
                              axis=1)
        dz_ref[...] = dzb
        dh0 = _mm_nt(dzb, winx[...]) + dh0p_ref[...]
        dx, dg, db = _ln_bwd(dh0, xh0, rs0, lng_ref[...])
        dx_ref[...] = dx
        dlng_ref[...] += dg
        dlnb_ref[...] += db

    s = x.shape[0]
    return _row_call(
        "in_bwd", body, s, min(ROW_TILE, s), [x, tab, z3, dq, dk, dv, lam, du_p, dgate, dh0p],
        [ln_g, ln_b, qg, kvg], [w_inx, w_b, w_q2, w_kv2],
        [(1024, F32), (1024, BF16), (3072, BF16)],
        [(1, 1024), (1, 1024), (1, 256), (1, 256), (256, 2048), (256, 2048), (S5_WIDTH, 2 * NS)])


def _wgrad(name, xs, dy, out_dtype=F32, blocks=1, after=None):
    s, k = xs.shape
    n = dy.shape[1]
    nb = n // blocks
    tk = min(k, 1024)
    tn = max(d for d in range(128, min(n, 2048) + 1, 128) if n % d == 0)
    ts = min(s, 2048)
    per_tile = tn // nb
    assert s % ts == 0 and k % tk == 0 and n % tn == 0 and (blocks == 1 or tn % nb == 0), (name, s, k, n)
    last = s // ts - 1

    def body(x_ref, dy_ref, *rest):
        o_ref, acc = rest[-2:]

        @pl.when(pl.program_id(2) == 0)
        def _():
            acc[...] = jnp.zeros_like(acc)

        acc[...] += _mm_tn(x_ref[...], dy_ref[...])

        @pl.when(pl.program_id(2) == last)
        def _():
            if blocks == 1:
                o_ref[...] = acc[...].astype(out_dtype)
            else:
                for d in range(per_tile):
                    o_ref[d] = acc[:, d * nb:(d + 1) * nb].astype(out_dtype)

    if blocks == 1:
        out_shape = jax.ShapeDtypeStruct((k, n), out_dtype)
        out_spec = pl.BlockSpec((tk, tn), lambda a, b, c: (a, b))
    else:
        out_shape = jax.ShapeDtypeStruct((blocks, k, nb), out_dtype)
        out_spec = pl.BlockSpec((per_tile, tk, nb), lambda a, b, c: (b, a, 0))
    return pl.pallas_call(
        body, name=name, grid=(k // tk, n // tn, s // ts),
        in_specs=[pl.BlockSpec((ts, tk), lambda a, b, c: (c, a)), pl.BlockSpec((ts, tn), lambda a, b, c: (c, b))]
        + ([] if after is None else [pl.BlockSpec(memory_space=pl.ANY)]),
        out_specs=out_spec, out_shape=out_shape, scratch_shapes=[pltpu.VMEM((tk, tn), F32)],
        compiler_params=pltpu.CompilerParams(dimension_semantics=("arbitrary", "arbitrary", "arbitrary"),
                                             vmem_limit_bytes=VMEM_LIMIT),
    )(xs, dy, *([] if after is None else [after]))


def _mesh_pos():
    x, y, c = lax.axis_index("x"), lax.axis_index("y"), lax.axis_index("c")
    return x, y, c


def _peer(x, y, c, k):
    px = 1 - x if k & 4 else x
    py = 1 - y if k & 2 else y
    pc = 1 - c if k & 1 else c
    return (px, py, pc), 4 * px + 2 * py + pc


def _all_gather_two_level(name, arrays):
    n = len(arrays)

    def body(*refs):
        src, dst = refs[:n], refs[n:2 * n]
        send_sems, recv_sems, local_sems = refs[2 * n:]
        x, y, c = _mesh_pos()
        me = 4 * x + 2 * y + c
        sibling = (x, y, 1 - c)

        def copy(i, k, source, slot, to):
            return pltpu.make_async_remote_copy(src_ref=source, dst_ref=dst[i].at[slot],
                                                send_sem=send_sems.at[(k - 1) * n + i],
                                                recv_sem=recv_sems.at[(k - 1) * n + i],
                                                device_id=to, device_id_type=MESH)

        own = [pltpu.make_async_copy(src[i], dst[i].at[me], local_sems.at[i]) for i in range(n)]
        for cp in own:
            cp.start()
        sends = []
        for k in (1, 2, 4, 6):
            peer, _ = _peer(x, y, c, k)
            for i in range(n):
                sends.append(copy(i, k, src[i], me, peer))
                sends[-1].start()
        for k in (2, 4, 6):
            peer, pid = _peer(x, y, c, k)
            for i in range(n):
                copy(i, k, src[i], pid, peer).wait_recv()
                sends.append(copy(i, k + 1, dst[i].at[pid], pid, sibling))
                sends[-1].start()
        for k in (1, 3, 5, 7):
            peer, pid = _peer(x, y, c, k)
            for i in range(n):
                copy(i, k, src[i], pid, peer).wait_recv()
        for cp in sends:
            cp.wait_send()
        for cp in own:
            cp.wait()

    n_sem = n * (N_DEV - 1)
    return pl.pallas_call(
        body, name=name,
        in_specs=[pl.BlockSpec(memory_space=pl.ANY)] * n,
        out_specs=[pl.BlockSpec(memory_space=pl.ANY)] * n,
        out_shape=[jax.ShapeDtypeStruct((N_DEV,) + a.shape, a.dtype) for a in arrays],
        scratch_shapes=[pltpu.SemaphoreType.DMA((n_sem,)), pltpu.SemaphoreType.DMA((n_sem,)),
                        pltpu.SemaphoreType.DMA((n,))],
    )(*arrays)


_HBM = pl.BlockSpec(memory_space=pltpu.HBM)
_SEM = pl.BlockSpec(memory_space=pltpu.SEMAPHORE)


def _exchange_start(name, arrays, gather, after=None):
    n = len(arrays)
    n_sem = n * (N_DEV - 1)
    me = 4 * lax.axis_index("x") + 2 * lax.axis_index("y") + lax.axis_index("c")
    lands = []
    for a in arrays:
        own = a[None] if gather else lax.dynamic_slice_in_dim(a, me, 1, 0)
        lands.append(lax.dynamic_update_slice(lax.empty((N_DEV,) + a.shape[-2:], a.dtype), own, (me, 0, 0)))
    n_after = 0 if after is None else 1

    def body(*refs):
        src, land = refs[:n], refs[n:2 * n]
        send_sems, recv_sems = refs[2 * n + n_after], refs[2 * n + n_after + 1]
        token = refs[-1]
        x, y, c = _mesh_pos()
        me_in = 4 * x + 2 * y + c
        for k in range(1, N_DEV):
            peer, pid = _peer(x, y, c, k)
            for i in range(n):
                idx = (k - 1) * n + i
                pltpu.make_async_remote_copy(src_ref=src[i] if gather else src[i].at[pid], dst_ref=land[i].at[me_in],
                                             send_sem=send_sems.at[idx], recv_sem=recv_sems.at[idx],
                                             device_id=peer, device_id_type=MESH).start()
        token[...] = jnp.zeros_like(token)

    operands = [pltpu.with_memory_space_constraint(a, pltpu.HBM) for a in list(arrays) + lands]
    outs = pl.pallas_call(
        body, name=name,
        out_shape=(pltpu.SemaphoreType.DMA((n_sem,)), pltpu.SemaphoreType.DMA((n_sem,)),
                   *[pltpu.HBM(a.shape, a.dtype) for a in list(arrays) + lands],
                   jax.ShapeDtypeStruct((8, 128), F32)),
        in_specs=[_HBM] * (2 * n) + [pl.BlockSpec(memory_space=pl.ANY)] * n_after,
        out_specs=(_SEM, _SEM, *[_HBM] * (2 * n), pl.BlockSpec(memory_space=pltpu.VMEM)),
        input_output_aliases={i: 2 + i for i in range(2 * n)},
        compiler_params=pltpu.CompilerParams(has_side_effects=pltpu.SideEffectType.DATAFLOW_SIDE_EFFECTING),
    )(*operands, *([after] if n_after else []))
    return (gather, outs[0], outs[1], outs[2:2 + n], outs[2 + n:2 + 2 * n]), outs[-1]


def _exchange_wait(name, handle, after):
    gather, send_sems, recv_sems, srcs, lands = handle
    n = len(srcs)

    def body(*refs):
        src, land = refs[:n], refs[n:2 * n]
        s_sems, r_sems = refs[2 * n], refs[2 * n + 1]
        x, y, c = _mesh_pos()
        for k in range(1, N_DEV):
            peer, pid = _peer(x, y, c, k)
            for i in range(n):
                idx = (k - 1) * n + i
                cp = pltpu.make_async_remote_copy(src_ref=src[i] if gather else src[i].at[pid], dst_ref=land[i].at[pid],
                                                  send_sem=s_sems.at[idx], recv_sem=r_sems.at[idx],
                                                  device_id=peer, device_id_type=MESH)
                cp.wait_send()
                cp.wait_recv()

    outs = pl.pallas_call(
        body, name=name,
        out_shape=tuple(pltpu.HBM(a.shape, a.dtype) for a in list(srcs) + list(lands)),
        in_specs=[_HBM] * (2 * n) + [_SEM, _SEM, pl.BlockSpec(memory_space=pl.ANY)],
        out_specs=tuple([_HBM] * (2 * n)),
        input_output_aliases={i: i for i in range(2 * n)},
        compiler_params=pltpu.CompilerParams(has_side_effects=pltpu.SideEffectType.DATAFLOW_SIDE_EFFECTING),
    )(*srcs, *lands, send_sems, recv_sems, after)
    return list(outs[n:])


def _adamw(name, parts, w, m, v):
    a_rows, b_cols = w.shape
    ta = min(a_rows, ADAM_TILE)
    assert a_rows % ta == 0
    c1 = 1.0 - ADAM_B1 ** ADAM_STEP
    c2 = 1.0 - ADAM_B2 ** ADAM_STEP

    def body(p_ref, w_ref, m_ref, v_ref, g_ref, d_ref, mo_ref, vo_ref):
        g = p_ref[0].astype(F32)
        for d in range(1, N_DEV):
            g = g + p_ref[d].astype(F32)
        g_ref[...] = g
        mn = ADAM_B1 * m_ref[...] + (1.0 - ADAM_B1) * g
        vn = ADAM_B2 * v_ref[...] + (1.0 - ADAM_B2) * (g * g)
        mo_ref[...] = mn
        vo_ref[...] = vn
        d_ref[...] = -ADAM_LR * ((mn / c1) / (jnp.sqrt(vn / c2) + ADAM_EPS) + ADAM_WD * w_ref[...])

    row = pl.BlockSpec((ta, b_cols), lambda i: (i, 0))
    return pl.pallas_call(
        body, name=name, grid=(a_rows // ta,),
        in_specs=[pl.BlockSpec((N_DEV, ta, b_cols), lambda i: (0, i, 0)), row, row, row],
        out_specs=[row] * 4,
        out_shape=[jax.ShapeDtypeStruct((a_rows, b_cols), F32)] * 4,
        compiler_params=pltpu.CompilerParams(dimension_semantics=("arbitrary",), vmem_limit_bytes=VMEM_LIMIT),
    )(parts, w, m, v)


def _pack_rows(arrays, rows):
    flat = jnp.concatenate([a.reshape(-1) for a in arrays])
    return jnp.pad(flat, (0, rows * LANES - flat.shape[0])).reshape(rows, LANES)


def _cols_from_blocks(g):
    return g.transpose(1, 0, 2).reshape(g.shape[1], N_DEV * g.shape[2])


def _blocks_from_cols(w):
    return w.reshape(w.shape[0], N_DEV, w.shape[1] // N_DEV).transpose(1, 0, 2)


def _unpack_flat(flat2d, shapes):
    flat = flat2d.reshape(-1)
    out, off = [], 0
    for shp in shapes:
        sz = math.prod(shp)
        out.append(flat[off:off + sz].reshape(shp))
        off += sz
    return out


def _s5_mats(lam_re, lam_im, log_dt, b_re, b_im, c_re, c_im):
    lr = jnp.minimum(lam_re, S5_MAX_RE)
    li = lam_im
    dt = jnp.exp(log_dt)[:, None]
    mag = jnp.exp(lr * dt)
    ang = li * dt
    ab_re = mag * jnp.cos(ang)
    ab_im = mag * jnp.sin(ang)
    den = lr * lr + li * li
    nr = ab_re - 1.0
    f_re = ((nr * lr + ab_im * li) / den)[..., None]
    f_im = ((ab_im * lr - nr * li) / den)[..., None]
    bb_re = f_re * b_re - f_im * b_im
    bb_im = f_re * b_im + f_im * b_re
    eye = jnp.eye(S5_GROUPS, dtype=F32)
    a = jnp.stack([ab_re.reshape(-1), ab_im.reshape(-1)])
    wb = jnp.concatenate([jnp.einsum("gph,gk->ghkp", bb_re, eye).reshape(S5_WIDTH, NS),
                          jnp.einsum("gph,gk->ghkp", bb_im, eye).reshape(S5_WIDTH, NS)], axis=1)
    wc = jnp.concatenate([jnp.einsum("ghp,gk->gpkh", c_re, eye).reshape(NS, S5_WIDTH),
                          -jnp.einsum("ghp,gk->gpkh", c_im, eye).reshape(NS, S5_WIDTH)], axis=0)
    return a, wb, wc


def _power_table(a):
    ar, ai = a[0], a[1]
    rows_r, rows_i = [ar], [ai]
    for _ in range(7):
        pr, pi = rows_r[-1], rows_i[-1]
        rows_r.append(pr * ar - pi * ai)
        rows_i.append(pr * ai + pi * ar)
    return jnp.concatenate([jnp.stack(rows_r), jnp.stack(rows_i)], axis=1)


def _rope_table(positions):
    inv = ROPE_THETA ** (-jnp.arange(0, MLA_ROPE, 2, dtype=F32) / MLA_ROPE)
    inv128 = jnp.concatenate([jnp.zeros((MLA_NOPE,), F32), inv, inv, jnp.zeros((32,), F32)])
    sign = jnp.concatenate([jnp.zeros((MLA_NOPE,), F32), -jnp.ones((16,), F32), jnp.ones((16,), F32),
                            jnp.zeros((32,), F32)])
    ang = positions.astype(F32)[:, None] * inv128
    return jnp.concatenate([jnp.cos(ang), jnp.sin(ang) * sign], axis=1)


def _derived_weights(full):
    w_in = full["w_in"]
    k1, k2 = w_in[:, 768:784], w_in[:, 784:800]
    z64, z32 = jnp.zeros((1024, 64), BF16), jnp.zeros((1024, 32), BF16)
    w_inx = jnp.concatenate([w_in[:, :768], z64, k1, k2, z32, z64, k2, k1, z32, w_in[:, 800:]], axis=1)
    uq = full["w_uq"].reshape(256, MLA_HEADS, MLA_QK)
    nope, r1, r2 = uq[:, :, :64], uq[:, :, 64:80], uq[:, :, 80:]
    zq64, zq32 = jnp.zeros((256, MLA_HEADS, 64), BF16), jnp.zeros((256, MLA_HEADS, 32), BF16)
    w_q2 = jnp.concatenate([jnp.concatenate([nope, r1, r2, zq32], axis=2).reshape(256, 1024),
                            jnp.concatenate([zq64, r2, r1, zq32], axis=2).reshape(256, 1024)], axis=1)
    ukv = full["w_ukv"].reshape(256, MLA_HEADS, 128)
    w_kv2 = jnp.concatenate([jnp.concatenate([ukv[:, :, :64], zq64], axis=2).reshape(256, 1024),
                             jnp.concatenate([ukv[:, :, 64:], zq64], axis=2).reshape(256, 1024)], axis=1)
    return w_inx, w_q2, w_kv2


def _oa_padded(w_oa):
    oa = w_oa.reshape(MLA_HEADS, MLA_V, 1024)
    return jnp.concatenate([oa, jnp.zeros_like(oa)], axis=1).reshape(1024, 1024)


def _fold_w_in(d_inx):
    d_k1 = d_inx[:, 832:848] + d_inx[:, 976:992]
    d_k2 = d_inx[:, 848:864] + d_inx[:, 960:976]
    return jnp.concatenate([d_inx[:, :768], d_k1, d_k2, d_inx[:, 1024:]], axis=1)


def _fold_qkv(d_q2, d_kv2):
    a = d_q2[:, :1024].reshape(256, MLA_HEADS, 128)
    b = d_q2[:, 1024:].reshape(256, MLA_HEADS, 128)
    d_w_uq = jnp.concatenate([a[:, :, :64], a[:, :, 64:80] + b[:, :, 80:96], a[:, :, 80:96] + b[:, :, 64:80]],
                             axis=2).reshape(256, MLA_HEADS * MLA_QK)
    kk = d_kv2[:, :1024].reshape(256, MLA_HEADS, 128)
    vv = d_kv2[:, 1024:].reshape(256, MLA_HEADS, 128)
    d_w_ukv = jnp.concatenate([kk[:, :, :64], vv[:, :, :64]], axis=2).reshape(256, 1024)
    return d_w_uq, d_w_ukv


def kernel(x, mem, positions, ln_in_g, ln_in_b, w_in, s5_lam_re, s5_lam_im, s5_log_dt, s5_b_re, s5_b_im, s5_c_re, s5_c_im, s5_d, w_glu, q_norm_g, w_uq, kv_norm_g, w_ukv, w_oa, w_o, ln1_g, ln1_b, w_xq, w_xk, w_xv, w_xo, ln2_g, ln2_b, w_up, w_down, ln3_g, ln3_b, loss_target, m_ln_in_g, m_ln_in_b, m_w_in, m_s5_lam_re, m_s5_lam_im, m_s5_log_dt, m_s5_b_re, m_s5_b_im, m_s5_c_re, m_s5_c_im, m_s5_d, m_w_glu, m_q_norm_g, m_w_uq, m_kv_norm_g, m_w_ukv, m_w_oa, m_w_o, m_ln1_g, m_ln1_b, m_w_xq, m_w_xk, m_w_xv, m_w_xo, m_ln2_g, m_ln2_b, m_w_up, m_w_down, m_ln3_g, m_ln3_b, v_ln_in_g, v_ln_in_b, v_w_in, v_s5_lam_re, v_s5_lam_im, v_s5_log_dt, v_s5_b_re, v_s5_b_im, v_s5_c_re, v_s5_c_im, v_s5_d, v_w_glu, v_q_norm_g, v_w_uq, v_kv_norm_g, v_w_ukv, v_w_oa, v_w_o, v_ln1_g, v_ln1_b, v_w_xq, v_w_xk, v_w_xv, v_w_xo, v_ln2_g, v_ln2_b, v_w_up, v_w_down, v_ln3_g, v_ln3_b):
    args = dict(locals())
    wts = {n: args[n] for n in WEIGHTS}
    mom = {n: args["m_" + n] for n in WEIGHTS}
    vel = {n: args["v_" + n] for n in WEIGHTS}
    xs, mems, tgt = x[0], mem[0], loss_target[0]
    s = xs.shape[0]
    names_big = [n for (n, _, _, _) in SHARDED]
    kind = {n: kd for (n, kd, _, _) in SHARDED}
    n_small = sum(math.prod(wts[n].shape) for n in SMALL)
    small_rows = _round_up(n_small + 1, 8 * LANES) // LANES

    late = [n for n in names_big if n not in EARLY]
    shard = {n: wts[n][0].astype(BF16) for n in names_big}
    g_early = dict(zip(EARLY, _all_gather_two_level("all_gather", [shard[n] for n in EARLY])))
    ag_handle, ag_token = _exchange_start("all_gather_late", [shard[n] for n in late], True, after=g_early["w_in"])
    full = {n: _cols_from_blocks(g_early[n]) for n in EARLY}
    w_inx, w_q2, w_kv2 = _derived_weights(full)
    s5_args = (s5_lam_re[0], s5_lam_im[0], s5_log_dt[0], s5_b_re[0], s5_b_im[0], s5_c_re[0], s5_c_im[0])
    (a_mat, w_b, w_c), s5_vjp = jax.vjp(_s5_mats, *s5_args)
    w_bb, w_cb = w_b.astype(BF16), w_c.astype(BF16)
    pw = _power_table(a_mat)
    pwb = _power_table(a_mat * jnp.array([[1.0], [-1.0]], F32))[::-1]
    tab = _rope_table(positions[0])
    row = lambda a: a.reshape(1, -1)
    ln_g, ln_b = row(ln_in_g) + ag_token[0:1, 0:1], row(ln_in_b)
    lns = [ln1_g, ln1_b, ln2_g, ln2_b]

    h0, z3, gate, h, q, k, v, kt, vt = _in_fwd(xs, tab, ln_g, ln_b, q_norm_g, kv_norm_g, pw, w_inx, w_bb, w_q2, w_kv2)
    yl, s_out = _s5out_fwd(h, z3, s5_d, w_cb, full["w_glu"])
    o, lse_t = _attn_fwd(q, k, vt)
    g_late = dict(zip(late, _exchange_wait("all_gather_late_wait", ag_handle, after=lse_t)))
    full.update({n: g_late[n].reshape(-1, g_late[n].shape[2]) if kind[n] == "row" else _cols_from_blocks(g_late[n])
                 for n in late})
    w_oap = _oa_padded(full["w_oa"])
    memk, memv = _mem_kv(mems, full["w_xk"], full["w_xv"])
    r2, mixin_b, h1_b, ox_b, a_out, r1, qx_b = _post_fwd(o, s_out, gate, h0, memk, memv, lns,
                                        w_oap, full["w_o"], full["w_xq"], full["w_xo"])
    dh2, h2_b, dup_b, act_b, dff_b, loss_acc, d_ln3_g, d_ln3_b = _mlp(
        r2, tgt, ln2_g, ln2_b, ln3_g, ln3_b, full["w_up"], full["w_down"])
    (do_b, delta, d_so, dgate, dh0p, daout_b, dmix_b, dqx_b, dxa_b,
     dmk, dmv, d_ln1_g, d_ln1_b, d_ln2_g, d_ln2_b) = _post_bwd(
        dh2, o, s_out, gate, a_out, r1, r2, qx_b, memk, memv, lns, w_oap, full["w_o"], full["w_xq"], full["w_xo"])
    rows8 = lambda g: g.reshape(N_DEV, g.shape[0] // N_DEV, g.shape[1])
    d_w_oa = _wgrad("wg_oa", o, daout_b).reshape(MLA_HEADS, 128, 1024)[:, :64].reshape(512, 1024)
    send = {
        "w_oa": _blocks_from_cols(d_w_oa).astype(BF16),
        "w_up": _wgrad("wg_up", h2_b, dup_b, BF16, N_DEV),
        "w_o": rows8(_wgrad("wg_o", mixin_b, dmix_b, BF16)),
        "w_xq": rows8(_wgrad("wg_xq", h1_b, dqx_b, BF16)),
        "w_xk": rows8(_wgrad("wg_xk", mems, dmk, BF16)),
        "w_xv": rows8(_wgrad("wg_xv", mems, dmv, BF16)),
        "w_xo": rows8(_wgrad("wg_xo", ox_b, dxa_b, BF16)),
        "w_down": rows8(_wgrad("wg_down", act_b, dff_b, BF16)),
    }
    rs_handle, rs_token = _exchange_start("grad_exchange_late", [send[n] for n in late], False)
    dl_t = delta.T[:MLA_HEADS].reshape(MLA_HEADS, 1, s) + rs_token[0, 0]
    dq, dk, dv = _attn_bwd(q, k, v, kt, do_b, lse_t, dl_t)
    lam, du_p, d_s5_d, d_a, d_w_glu, d_wc = _s5out_bwd(d_so, yl, z3, h, s5_d, pwb, w_cb, full["w_glu"])
    (dx, h0_b, dz_b, d_ln_g, d_ln_b, d_qg, d_kvg, d_q2, d_kv2, d_wb) = _in_bwd(
        xs, tab, z3, dq, dk, dv, lam, du_p, dgate, dh0p, ln_g, ln_b, q_norm_g, kv_norm_g, w_inx, w_bb, w_q2, w_kv2)

    d_w_in = _fold_w_in(_wgrad("wg_in", h0_b, dz_b))
    d_w_uq, d_w_ukv = _fold_qkv(d_q2, d_kv2)
    send.update({
        "w_in": _blocks_from_cols(d_w_in).astype(BF16), "w_uq": _blocks_from_cols(d_w_uq).astype(BF16),
        "w_ukv": _blocks_from_cols(d_w_ukv).astype(BF16), "w_glu": _blocks_from_cols(d_w_glu).astype(BF16),
    })
    re_handle, re_token = _exchange_start("grad_exchange_early", [send[n] for n in EARLY], False)
    d_s5 = s5_vjp((d_a.reshape(2, NS) + re_token[0, 0], d_wb, d_wc))
    small_grads = {
        "ln_in_g": d_ln_g, "ln_in_b": d_ln_b, "s5_lam_re": d_s5[0], "s5_lam_im": d_s5[1], "s5_log_dt": d_s5[2],
        "s5_b_re": d_s5[3], "s5_b_im": d_s5[4], "s5_c_re": d_s5[5], "s5_c_im": d_s5[6], "s5_d": d_s5_d,
        "q_norm_g": d_qg, "kv_norm_g": d_kvg, "ln1_g": d_ln1_g, "ln1_b": d_ln1_b, "ln2_g": d_ln2_g,
        "ln2_b": d_ln2_b, "ln3_g": d_ln3_g, "ln3_b": d_ln3_b,
    }
    recv_small = _all_gather_two_level(
        "grad_exchange", [_pack_rows([small_grads[n] for n in SMALL] + [loss_acc[0, :1]], small_rows)])[0]

    recv = dict(zip(late, _exchange_wait("grad_exchange_late_wait", rs_handle, after=recv_small)))
    results = [dict(), dict(), dict(), dict()]
    small_out = _adamw("adamw_small", recv_small,
                       *[_pack_rows([t[n] for n in SMALL], small_rows) for t in (wts, mom, vel)])
    for res, fs in zip(results, small_out):
        for n, a in zip(SMALL, _unpack_flat(fs, [wts[n].shape for n in SMALL])):
            res[n] = a
    for n in late + list(EARLY):
        if n == EARLY[0]:
            recv.update(zip(EARLY, _exchange_wait("grad_exchange_early_wait", re_handle, after=results[0][late[-1]])))
        outs = _adamw("adamw_" + n, recv[n], wts[n][0], mom[n][0], vel[n][0])
        for res, a in zip(results, outs):
            res[n] = a[None]

    loss = small_out[0].reshape(-1)[n_small]
    return (loss, dx[None], *[res[n] for res in results for n in WEIGHTS])
```

```python
import math

import jax
import jax.numpy as jnp
from jax import lax
from jax.experimental import pallas as pl
from jax.experimental.pallas import tpu as pltpu

F32 = jnp.float32
BF16 = jnp.bfloat16

D_MODEL = 1024
S5_WIDTH = 256
S5_GROUP_CH = 16
S5_GROUPS = 16
S5_STATE = 64
NS = S5_GROUPS * S5_STATE
S5_MAX_RE = -1e-4
MLA_HEADS = 8
MLA_NOPE = 64
MLA_ROPE = 32
MLA_QK = 96
MLA_V = 64
HEAD_PAD = 128
ROPE_THETA = 10000.0
XATTN_HEADS = 4
XATTN_HD = 256
MLP_HIDDEN = 4096
LN_EPS = 1e-5
RMS_EPS = 1e-6
NEG_INF = -1e30
LOG2E = 1.4426950408889634
DN_ALPHA = 2.0 ** 0.25
ADAM_LR = 0.001
ADAM_B1 = 0.9
ADAM_B2 = 0.999
ADAM_EPS = 1e-08
ADAM_WD = 0.01
ADAM_STEP = 10

N_DEV = 8
MESH = pl.DeviceIdType.MESH
LANES = 1024
VMEM_LIMIT = 60 * 1024 * 1024

ROW_TILE = 256
ATT_TILE = 1024
ADAM_TILE = 256

SHARDED = (
    ("w_in", "col", 1024, 2848), ("w_glu", "col", 256, 2048), ("w_uq", "col", 256, 768),
    ("w_ukv", "col", 256, 1024), ("w_oa", "col", 512, 1024), ("w_o", "row", 1024, 1024),
    ("w_xq", "row", 1024, 1024), ("w_xk", "row", 1024, 1024), ("w_xv", "row", 1024, 1024),
    ("w_xo", "row", 1024, 1024), ("w_up", "col", 1024, 4096), ("w_down", "row", 4096, 1024),
)
EARLY = ("w_in", "w_glu", "w_uq", "w_ukv")
SMALL = ("ln_in_g", "ln_in_b", "s5_lam_re", "s5_lam_im", "s5_log_dt", "s5_b_re", "s5_b_im", "s5_c_re",
         "s5_c_im", "s5_d", "q_norm_g", "kv_norm_g", "ln1_g", "ln1_b", "ln2_g", "ln2_b", "ln3_g", "ln3_b")
WEIGHTS = ("ln_in_g", "ln_in_b", "w_in", "s5_lam_re", "s5_lam_im", "s5_log_dt", "s5_b_re", "s5_b_im",
           "s5_c_re", "s5_c_im", "s5_d", "w_glu", "q_norm_g", "w_uq", "kv_norm_g", "w_ukv", "w_oa", "w_o",
           "ln1_g", "ln1_b", "w_xq", "w_xk", "w_xv", "w_xo", "ln2_g", "ln2_b", "w_up", "w_down", "ln3_g", "ln3_b")


def _round_up(n, m):
    return (n + m - 1) // m * m


def _bf(a):
    return a.astype(BF16)


def _mm(a, b):
    return jnp.dot(_bf(a), _bf(b), preferred_element_type=F32)


def _mm_nt(a, b):
    return lax.dot_general(_bf(a), _bf(b), (((1,), (1,)), ((), ())), preferred_element_type=F32)


def _mm_tn(a, b):
    return lax.dot_general(_bf(a), _bf(b), (((0,), (0,)), ((), ())), preferred_element_type=F32)


def _sigmoid(a):
    return 1.0 / (1.0 + jnp.exp(-a))


def _gelu(a):
    return 0.5 * a * (1.0 + lax.erf(a * (2.0 ** -0.5)))


def _gelu_grad(a):
    return 0.5 * (1.0 + lax.erf(a * (2.0 ** -0.5))) + a * jnp.exp(-0.5 * a * a) * (1.0 / math.sqrt(2.0 * math.pi))


def _ln_fwd(a, g, b):
    mu = jnp.mean(a, axis=-1, keepdims=True)
    ac = a - mu
    var = jnp.mean(ac * ac, axis=-1, keepdims=True)
    rstd = lax.rsqrt(var + LN_EPS)
    xhat = ac * rstd
    return xhat * g + b, xhat, rstd


def _ln_bwd(dy, xhat, rstd, g):
    dxh = dy * g
    m1 = jnp.mean(dxh, axis=-1, keepdims=True)
    m2 = jnp.mean(dxh * xhat, axis=-1, keepdims=True)
    dx = rstd * (dxh - m1 - xhat * m2)
    return dx, jnp.sum(dy * xhat, axis=0, keepdims=True), jnp.sum(dy, axis=0, keepdims=True)


def _rms_fwd(a, g):
    r = lax.rsqrt(jnp.mean(a * a, axis=-1, keepdims=True) + RMS_EPS)
    xn = a * r
    return xn * g, xn, r


def _rms_bwd(dy, xn, r, g):
    dxn = dy * g
    dx = r * (dxn - xn * jnp.mean(dxn * xn, axis=-1, keepdims=True))
    return dx, jnp.sum(dy * xn, axis=0, keepdims=True)


def _tile_heads(a):
    return jnp.concatenate([a] * MLA_HEADS, axis=1)


def _row_call(name, body, n_rows, ts, tiled_in, full_in, weights, tiled_out, acc_out, reverse=False, scratch=(),
              tiled_out_t=()):
    n = n_rows // ts
    assert n * ts == n_rows, (name, n_rows, ts)
    nt, nf, nw = len(tiled_in), len(full_in), len(weights)
    nto, nao = len(tiled_out) + len(tiled_out_t), len(acc_out)
    if reverse:
        imap = lambda i: (n - 1 - i, 0)
    else:
        imap = lambda i: (i, 0)
    const = lambda i: (0, 0)

    def kern(*refs):
        ins = refs[:nt + nf]
        w_hbm = refs[nt + nf:nt + nf + nw]
        outs = refs[nt + nf + nw:nt + nf + nw + nto + nao]
        scr = refs[nt + nf + nw + nto + nao:]
        w_vmem = scr[:nw]
        extra = scr[nw + 1:] if nw else scr
        if nw:
            sem = scr[nw]

            @pl.when(pl.program_id(0) == 0)
            def _():
                cps = [pltpu.make_async_copy(w_hbm[k], w_vmem[k], sem.at[k]) for k in range(nw)]
                for cp in cps:
                    cp.start()
                for cp in cps:
                    cp.wait()
        body(*ins, *w_vmem, *outs, *extra)

    in_specs = [pl.BlockSpec((ts, a.shape[1]), imap) for a in tiled_in]
    in_specs += [pl.BlockSpec(a.shape, const) for a in full_in]
    in_specs += [pl.BlockSpec(memory_space=pl.ANY) for _ in weights]
    assert not (reverse and tiled_out_t)
    out_shape = [jax.ShapeDtypeStruct((n_rows, c), dt) for (c, dt) in tiled_out]
    out_shape += [jax.ShapeDtypeStruct((c, n_rows), dt) for (c, dt) in tiled_out_t]
    out_shape += [jax.ShapeDtypeStruct(shp, F32) for shp in acc_out]
    out_specs = [pl.BlockSpec((ts, c), imap) for (c, dt) in tiled_out]
    out_specs += [pl.BlockSpec((c, ts), lambda i: (0, i)) for (c, dt) in tiled_out_t]
    out_specs += [pl.BlockSpec(shp, const) for shp in acc_out]
    scratch_shapes = [pltpu.VMEM(w.shape, w.dtype) for w in weights]
    if nw:
        scratch_shapes.append(pltpu.SemaphoreType.DMA((nw,)))
    scratch_shapes += list(scratch)
    return pl.pallas_call(
        kern, name=name, grid=(n,), in_specs=in_specs, out_specs=out_specs, out_shape=out_shape,
        scratch_shapes=scratch_shapes,
        compiler_params=pltpu.CompilerParams(dimension_semantics=("arbitrary",), vmem_limit_bytes=VMEM_LIMIT),
    )(*tiled_in, *full_in, *weights)


def _mem_kv(mem, w_xk, w_xv):
    m = mem.shape[0]

    def body(mem_ref, wk_ref, wv_ref, k_ref, v_ref):
        mb = mem_ref[...]
        k_ref[...] = _mm(mb, wk_ref[...]).astype(BF16)
        v_ref[...] = _mm(mb, wv_ref[...]).astype(BF16)

    return pl.pallas_call(
        body, name="mem_kv",
        out_shape=[jax.ShapeDtypeStruct((m, D_MODEL), BF16)] * 2,
        compiler_params=pltpu.CompilerParams(vmem_limit_bytes=VMEM_LIMIT),
    )(mem, w_xk, w_xv)


def _in_fwd(x, tab, ln_g, ln_b, qg, kvg, pw, w_inx, w_b, w_q2, w_kv2):
    ts = min(ROW_TILE, x.shape[0])

    def body(x_ref, tab_ref, lng_ref, lnb_ref, qg_ref, kvg_ref, pw_ref, winx, wb, wq2, wkv2,
             h0_ref, z3_ref, gate_ref, h_ref, q_ref, k_ref, v_ref, kt_ref, vt_ref, bu_scr, carry):
        @pl.when(pl.program_id(0) == 0)
        def _():
            carry[...] = jnp.zeros_like(carry)

        h0, _, _ = _ln_fwd(x_ref[...], lng_ref[...], lnb_ref[...])
        h0_ref[...] = h0
        z = _mm(h0, winx[...])
        z3_ref[...] = z[:, :768]
        gate_ref[...] = z[:, 1024:].astype(BF16)
        bu_scr[...] = _mm(z[:, :256], wb[...])
        _scan_fwd_rows(pw_ref, bu_scr, h_ref, carry, ts // 8)
        tab_v = tab_ref[...]
        cq1, cq2, ck1 = tab_v[:, :128], tab_v[:, 128:], tab_v[:, :128]
        cqn, _, _ = _rms_fwd(z[:, 256:512], qg_ref[...])
        q2 = _mm(cqn, wq2[...])
        q_ref[...] = (q2[:, :1024] * _tile_heads(cq1) + q2[:, 1024:] * _tile_heads(cq2)).astype(BF16)
        ckvn, _, _ = _rms_fwd(z[:, 512:768], kvg_ref[...])
        kv2 = _mm(ckvn, wkv2[...])
        krp = z[:, 768:896] * ck1 + z[:, 896:1024] * cq2
        kf = kv2[:, :1024] + _tile_heads(krp)
        k_ref[...] = kf.astype(BF16)
        v_ref[...] = kv2[:, 1024:].astype(BF16)
        kt_ref[...] = kf.T.astype(BF16)
        vt_ref[...] = kv2[:, 1024:].T.astype(BF16)

    s = x.shape[0]
    return _row_call(
        "in_fwd", body, s, ts, [x, tab], [ln_g, ln_b, qg, kvg, pw], [w_inx, w_b, w_q2, w_kv2],
        [(1024, F32), (768, F32), (2048, BF16), (2048, F32), (1024, BF16), (1024, BF16), (1024, BF16)], [],
        tiled_out_t=[(1024, BF16), (1024, BF16)], scratch=[pltpu.VMEM((ts, 2 * NS), F32), pltpu.VMEM((8, 2 * NS), F32)])


def _scan_coeffs(pw_ref, rows, forward):
    row = lax.broadcasted_iota(jnp.int32, (8, NS), 0)
    out = []
    for k, r in zip((1, 2, 4), rows):
        keep = (row >= k) if forward else (row < 8 - k)
        out.append((jnp.where(keep, pw_ref[r:r + 1, :NS], 0.0), jnp.where(keep, pw_ref[r:r + 1, NS:], 0.0), k))
    return out


def _scan_level(xr, xi, coeff, forward):
    ar, ai, k = coeff
    shift = k if forward else 8 - k
    sr, si = pltpu.roll(xr, shift, 0), pltpu.roll(xi, shift, 0)
    return xr + ar * sr - ai * si, xi + ar * si + ai * sr


def _scan_fwd_rows(pw_ref, bu_ref, h_ref, carry, nblk):
    coeffs = _scan_coeffs(pw_ref, (0, 1, 3), True)

    def blk(b, _):
        r0 = pl.multiple_of(b * 8, 8)
        xr = bu_ref[pl.ds(r0, 8), :NS]
        xi = bu_ref[pl.ds(r0, 8), NS:]
        for coeff in coeffs:
            xr, xi = _scan_level(xr, xi, coeff, True)
        cr, ci = carry[7:8, :NS], carry[7:8, NS:]
        pr, pi = pw_ref[:, :NS], pw_ref[:, NS:]
        hr = xr + pr * cr - pi * ci
        hi = xi + pr * ci + pi * cr
        h_ref[pl.ds(r0, 8), :NS] = hr
        h_ref[pl.ds(r0, 8), NS:] = hi
        carry[:, :NS] = hr
        carry[:, NS:] = hi
        return 0

    lax.fori_loop(0, nblk, blk, 0)


def _scan_bwd_rows(pw_ref, g_ref, h_ref, lam_ref, carry, acc, nblk):
    row = lax.broadcasted_iota(jnp.int32, (8, NS), 0)
    coeffs = _scan_coeffs(pw_ref, (7, 6, 4), False)

    def blk(bb, _):
        r0 = pl.multiple_of((nblk - 1 - bb) * 8, 8)
        xr = g_ref[pl.ds(r0, 8), :NS]
        xi = g_ref[pl.ds(r0, 8), NS:]
        for coeff in coeffs:
            xr, xi = _scan_level(xr, xi, coeff, False)
        cr, ci = carry[0:1, :NS], carry[0:1, NS:]
        pr, pi = pw_ref[:, :NS], pw_ref[:, NS:]
        lr = xr + pr * cr - pi * ci
        li = xi + pr * ci + pi * cr
        lam_ref[pl.ds(r0, 8), :NS] = lr
        lam_ref[pl.ds(r0, 8), NS:] = li
        nr = jnp.where(row < 7, pltpu.roll(lr, 7, 0), cr)
        ni = jnp.where(row < 7, pltpu.roll(li, 7, 0), ci)
        hr = h_ref[pl.ds(r0, 8), :NS]
        hi = h_ref[pl.ds(r0, 8), NS:]
        acc[:, :NS] += nr * hr + ni * hi
        acc[:, NS:] += ni * hr - nr * hi
        carry[:, :NS] = lr
        carry[:, NS:] = li
        return 0

    lax.fori_loop(0, nblk, blk, 0)


def _s5out_fwd(h, z3, d_skip, w_c, w_glu):
    def body(h_ref, z3_ref, d_ref, wc, wglu, yl_ref, so_ref):
        yl = _mm(h_ref[...], wc[...]) + d_ref[...] * z3_ref[:, :256]
        yl_ref[...] = yl
        y12 = _mm(_gelu(yl), wglu[...])
        so_ref[...] = y12[:, :1024] * _sigmoid(y12[:, 1024:])

    s = h.shape[0]
    return _row_call("s5out_fwd", body, s, min(ROW_TILE, s), [h, z3], [d_skip], [w_c, w_glu],
                     [(256, F32), (1024, F32)], [])


def _causal_mask_t(t):
    row = lax.broadcasted_iota(jnp.int32, (t, t), 0)
    col = lax.broadcasted_iota(jnp.int32, (t, t), 1)
    return row <= col


def _attn_fwd(q, k, vt):
    s = q.shape[0]
    t = min(ATT_TILE, s)
    nq = s // t
    scale = MLA_QK ** -0.5
    c2 = scale * LOG2E

    def body(q_ref, k_ref, vt_ref, o_ref, lse_ref, s0, s1):
        i = pl.program_id(1)
        qb = q_ref[...]

        def scores(kb, dst):
            st = _mm_nt(k_ref[pl.ds(pl.multiple_of(kb * t, t), t), :], qb)
            dst[...] = st
            return jnp.max(st, axis=0, keepdims=True)

        def update(kb, src, mt, state, masked):
            m, l, acc = state
            if masked:
                keep = _causal_mask_t(t)
                mt = jnp.max(jnp.where(keep, src[...], NEG_INF), axis=0, keepdims=True)
            m_new = jnp.maximum(m, mt)
            if masked:
                p = jnp.exp2((jnp.where(keep, src[...], NEG_INF) - m_new) * c2)
            else:
                p = jnp.exp2((src[...] - m_new) * c2)
            a = jnp.exp2((m - m_new) * c2)
            l = a * l + jnp.sum(p, axis=0, keepdims=True)
            acc = a * acc + _mm(vt_ref[:, pl.ds(pl.multiple_of(kb * t, t), t)], p)
            return m_new, l, acc

        def pair(jj, carry):
            mt_a, state = carry
            mt_b = scores(2 * jj + 1, s1)
            state = update(2 * jj, s0, mt_a, state, False)
            mt_a = scores(2 * jj + 2, s0)
            return mt_a, update(2 * jj + 1, s1, mt_b, state, False)

        def odd_tail(carry):
            mt_a, state = carry
            mt_b = scores(i, s1)
            state = update(i - 1, s0, mt_a, state, False)
            return update(i, s1, mt_b, state, True)

        def even_tail(carry):
            mt_a, state = carry
            return update(i, s0, mt_a, state, True)

        init = (jnp.full((1, t), NEG_INF, F32), jnp.zeros((1, t), F32), jnp.zeros((HEAD_PAD, t), F32))
        carry = lax.fori_loop(0, i // 2, pair, (scores(0, s0), init))
        m, l, acc = lax.cond(i % 2 == 1, odd_tail, even_tail, carry)
        o_ref[...] = (acc / l).T
        lse_ref[...] = m * scale + jnp.log(l)

    return pl.pallas_call(
        body, name="attn_fwd", grid=(MLA_HEADS, nq),
        in_specs=[pl.BlockSpec((t, HEAD_PAD), lambda h, i: (i, h)),
                  pl.BlockSpec((s, HEAD_PAD), lambda h, i: (0, h)),
                  pl.BlockSpec((HEAD_PAD, s), lambda h, i: (h, 0))],
        out_specs=[pl.BlockSpec((t, HEAD_PAD), lambda h, i: (i, h)),
                   pl.BlockSpec((None, 1, t), lambda h, i: (h, 0, i))],
        out_shape=[jax.ShapeDtypeStruct((s, MLA_HEADS * HEAD_PAD), F32),
                   jax.ShapeDtypeStruct((MLA_HEADS, 1, s), F32)],
        scratch_shapes=[pltpu.VMEM((t, t), F32)] * 2,
        compiler_params=pltpu.CompilerParams(dimension_semantics=("arbitrary", "arbitrary"),
                                             vmem_limit_bytes=VMEM_LIMIT),
    )(q, k, vt)


def _attn_bwd(q, k, v, kt, do, lse_t, dl_t):
    s = q.shape[0]
    t = min(ATT_TILE, s)
    nq = s // t
    scale = MLA_QK ** -0.5
    c2 = scale * LOG2E

    def body(q_ref, k_ref, v_ref, kt_ref, do_ref, lse_ref, dl_ref, dq_ref, dk_ref, dv_ref, dqt, s0, p0, s1, p1):
        j = pl.program_id(1)
        n = nq - 1 - j

        @pl.when(j == 0)
        def _():
            dqt[...] = jnp.zeros_like(dqt)

        kk = k_ref[...]
        vv = v_ref[...]
        ktb = kt_ref[:MLA_QK, :]

        def rows(m):
            return pl.ds(pl.multiple_of(jnp.where(m < n, j + 1 + m, j) * t, t), t)

        def first(m, sbuf, pbuf):
            r = rows(m)
            sbuf[...] = _mm_nt(kk, q_ref[r, :])
            pbuf[...] = _mm_nt(vv, do_ref[r, :])

        def finish(m, sbuf, pbuf, acc, masked):
            dk, dv = acc
            r = rows(m)
            pt = jnp.exp2(sbuf[...] * c2 - lse_ref[:, r] * LOG2E)
            if masked:
                pt = jnp.where(_causal_mask_t(t), pt, 0.0)
            dv = dv + _mm(pt, do_ref[r, :])
            dst = (pt * (pbuf[...] - dl_ref[:, r])).astype(BF16)
            dk = dk + _mm(dst, q_ref[r, :])
            dqt[:MLA_QK, r] += _mm(ktb, dst)
            return dk, dv

        def pair(jj, acc):
            first(2 * jj + 1, s1, p1)
            acc = finish(2 * jj, s0, p0, acc, False)
            first(2 * jj + 2, s0, p0)
            return finish(2 * jj + 1, s1, p1, acc, False)

        def odd_tail(acc):
            first(n, s1, p1)
            acc = finish(n - 1, s0, p0, acc, False)
            return finish(n, s1, p1, acc, True)

        def even_tail(acc):
            return finish(n, s0, p0, acc, True)

        zero = jnp.zeros((t, HEAD_PAD), F32)
        first(0, s0, p0)
        acc = lax.fori_loop(0, n // 2, pair, (zero, zero))
        dk, dv = lax.cond(n % 2 == 1, odd_tail, even_tail, acc)
        dk_ref[...] = (dk * scale).astype(BF16)
        dv_ref[...] = dv.astype(BF16)

        @pl.when(j == nq - 1)
        def _():
            for cc in range(nq):
                dq_ref[cc * t:(cc + 1) * t, :] = (dqt[:, cc * t:(cc + 1) * t].T * scale).astype(BF16)

    full = pl.BlockSpec((s, HEAD_PAD), lambda h, j: (0, h))
    tile = pl.BlockSpec((t, HEAD_PAD), lambda h, j: (j, h))
    stat = pl.BlockSpec((None, 1, s), lambda h, j: (h, 0, 0))
    return pl.pallas_call(
        body, name="attn_bwd", grid=(MLA_HEADS, nq),
        in_specs=[full, tile, tile, pl.BlockSpec((HEAD_PAD, t), lambda h, j: (h, j)), full, stat, stat],
        out_specs=[full, tile, tile],
        out_shape=[jax.ShapeDtypeStruct((s, MLA_HEADS * HEAD_PAD), BF16)] * 3,
        scratch_shapes=[pltpu.VMEM((HEAD_PAD, s), F32)] + [pltpu.VMEM((t, t), F32)] * 4,
        compiler_params=pltpu.CompilerParams(dimension_semantics=("arbitrary", "arbitrary"),
                                             vmem_limit_bytes=VMEM_LIMIT),
    )(q, k, v, kt, do, lse_t, dl_t)


def _xattn_probs(qxb, memk, hh):
    sl = slice(hh * XATTN_HD, (hh + 1) * XATTN_HD)
    sc = _mm_nt(qxb[:, sl], memk[:, sl]) * (XATTN_HD ** -0.5)
    e = jnp.exp(sc - jnp.max(sc, axis=1, keepdims=True))
    return e / jnp.sum(e, axis=1, keepdims=True)


def _post_forward(o, s_out, gate, h0, memk, memv, w_oa, w_o, w_xq, w_xo, g1, b1, g2, b2):
    a_out = _mm(o, w_oa)
    sg_s = _sigmoid(gate[:, :1024])
    sg_a = _sigmoid(gate[:, 1024:])
    mixin = sg_s * s_out + sg_a * a_out
    r1 = DN_ALPHA * h0 + _mm(mixin, w_o)
    h1, xh1, rs1 = _ln_fwd(r1, g1, b1)
    qxb = _mm(h1, w_xq).astype(BF16)
    ox = jnp.concatenate([_mm(_xattn_probs(qxb, memk, hh), memv[:, hh * XATTN_HD:(hh + 1) * XATTN_HD])
                          for hh in range(XATTN_HEADS)], axis=1)
    r2 = DN_ALPHA * h1 + _mm(ox, w_xo)
    return dict(a_out=a_out, mixin=mixin, r1=r1, h1=h1, qxb=qxb, ox=ox, r2=r2)


def _post_fwd(o, s_out, gate, h0, memk, memv, lns, w_oa, w_o, w_xq, w_xo):
    def body(o_ref, so_ref, gate_ref, h0_ref, mk_ref, mv_ref, g1, b1, g2, b2, woa, wo, wxq, wxo,
             r2_ref, mixin_ref, h1_ref, ox_ref, aout_ref, r1_ref, qx_ref):
        f = _post_forward(o_ref[...], so_ref[...], gate_ref[...].astype(F32), h0_ref[...], mk_ref[...], mv_ref[...],
                          woa[...], wo[...], wxq[...], wxo[...], g1[...], b1[...], g2[...], b2[...])
        r2_ref[...] = f["r2"]
        mixin_ref[...] = f["mixin"].astype(BF16)
        h1_ref[...] = f["h1"].astype(BF16)
        ox_ref[...] = f["ox"].astype(BF16)
        aout_ref[...] = f["a_out"]
        r1_ref[...] = f["r1"]
        qx_ref[...] = f["qxb"]

    s = o.shape[0]
    return _row_call("post_fwd", body, s, min(ROW_TILE, s), [o, s_out, gate, h0], [memk, memv, *lns],
                     [w_oa, w_o, w_xq, w_xo],
                     [(1024, F32), (1024, BF16), (1024, BF16), (1024, BF16), (1024, F32), (1024, F32), (1024, BF16)], [])


def _mlp(r2, target, g2, b2, g3, b3, w_up, w_down):
    cw = 1024
    n_chunk = MLP_HIDDEN // cw

    def body(r2_ref, tgt_ref, g2_ref, b2_ref, g3_ref, b3_ref, wup, wdn,
             dh2_ref, h2b_ref, dup_ref, act_ref, dff_ref, loss_ref, dg3_ref, db3_ref, up_scr):
        @pl.when(pl.program_id(0) == 0)
        def _():
            loss_ref[...] = jnp.zeros_like(loss_ref)
            dg3_ref[...] = jnp.zeros_like(dg3_ref)
            db3_ref[...] = jnp.zeros_like(db3_ref)

        h2, _, _ = _ln_fwd(r2_ref[...], g2_ref[...], b2_ref[...])
        h2b = h2.astype(BF16)
        h2b_ref[...] = h2b
        ff = jnp.zeros(h2.shape, F32)
        for c in range(n_chunk):
            sl = slice(c * cw, (c + 1) * cw)
            a = jnp.maximum(_mm(h2b, wup[:, sl]), 0.0)
            up_scr[:, sl] = a
            actb = (a * a).astype(BF16)
            act_ref[:, sl] = actb
            ff = ff + _mm(actb, wdn[sl, :])
        h3, xh3, rs3 = _ln_fwd(DN_ALPHA * h2 + ff, g3_ref[...], b3_ref[...])
        err = h3 - tgt_ref[...]
        loss_ref[...] += 0.5 * jnp.sum(err * err) * (1.0 / D_MODEL)
        dr3, dg3, db3 = _ln_bwd(err * (1.0 / D_MODEL), xh3, rs3, g3_ref[...])
        dg3_ref[...] += dg3
        db3_ref[...] += db3
        dffb = dr3.astype(BF16)
        dff_ref[...] = dffb
        dh2 = DN_ALPHA * dr3
        for c in range(n_chunk):
            sl = slice(c * cw, (c + 1) * cw)
            dupb = (_mm_nt(dffb, wdn[sl, :]) * (2.0 * up_scr[:, sl])).astype(BF16)
            dup_ref[:, sl] = dupb
            dh2 = dh2 + _mm_nt(dupb, wup[:, sl])
        dh2_ref[...] = dh2

    s = r2.shape[0]
    ts = min(ROW_TILE, s)
    return _row_call("mlp", body, s, ts, [r2, target], [g2, b2, g3, b3], [w_up, w_down],
                     [(1024, F32), (1024, BF16), (MLP_HIDDEN, BF16), (MLP_HIDDEN, BF16), (1024, BF16)],
                     [(8, 128), (1, 1024), (1, 1024)], scratch=[pltpu.VMEM((ts, MLP_HIDDEN), F32)])


def _post_bwd(dh2, o, s_out, gate, a_out, r1, r2, qx, memk, memv, lns, w_oa, w_o, w_xq, w_xo):
    def body(dh2_ref, o_ref, so_ref, gate_ref, aout_ref, r1_ref, r2_ref, qx_ref, mk_ref, mv_ref, g1, b1, g2, b2,
             woa, wo, wxq, wxo,
             do_ref, dl_ref, dso_ref, dgate_ref, dh0_ref, daout_ref, dmix_ref, dqx_ref, dxa_ref,
             dmk_ref, dmv_ref, dg1_ref, db1_ref, dg2_ref, db2_ref):
        @pl.when(pl.program_id(0) == 0)
        def _():
            for r in (dmk_ref, dmv_ref, dg1_ref, db1_ref, dg2_ref, db2_ref):
                r[...] = jnp.zeros_like(r)

        o = o_ref[...]
        s_out = so_ref[...]
        memk, memv = mk_ref[...], mv_ref[...]
        qxb = qx_ref[...]
        _, xh1, rs1 = _ln_fwd(r1_ref[...], g1[...], b1[...])
        _, xh2, rs2 = _ln_fwd(r2_ref[...], g2[...], b2[...])
        dr2, dg2, db2 = _ln_bwd(dh2_ref[...], xh2, rs2, g2[...])
        dg2_ref[...] += dg2
        db2_ref[...] += db2
        dxab = dr2.astype(BF16)
        dxa_ref[...] = dxab
        dox = _mm_nt(dxab, wxo[...])
        dqs = []
        for hh in range(XATTN_HEADS):
            sl = slice(hh * XATTN_HD, (hh + 1) * XATTN_HD)
            p = _xattn_probs(qxb, memk, hh)
            doxh = dox[:, sl].astype(BF16)
            dp = _mm_nt(doxh, memv[:, sl])
            ds = (p * (dp - jnp.sum(dp * p, axis=1, keepdims=True)) * (XATTN_HD ** -0.5)).astype(BF16)
            dqs.append(_mm(ds, memk[:, sl]))
            dmk_ref[:, sl] += _mm_tn(ds, qxb[:, sl])
            dmv_ref[:, sl] += _mm_tn(p, doxh)
        dqxb = jnp.concatenate(dqs, axis=1).astype(BF16)
        dqx_ref[...] = dqxb
        dh1 = DN_ALPHA * dr2 + _mm_nt(dqxb, wxq[...])
        dr1, dg1, db1 = _ln_bwd(dh1, xh1, rs1, g1[...])
        dg1_ref[...] += dg1
        db1_ref[...] += db1
        dh0_ref[...] = DN_ALPHA * dr1
        dmixb = dr1.astype(BF16)
        dmix_ref[...] = dmixb
        dmixin = _mm_nt(dmixb, wo[...])
        sg_s = _sigmoid(gate_ref[:, :1024].astype(F32))
        sg_a = _sigmoid(gate_ref[:, 1024:].astype(F32))
        dso_ref[...] = dmixin * sg_s
        daoutb = (dmixin * sg_a).astype(BF16)
        daout_ref[...] = daoutb
        dgate_ref[:, :1024] = (dmixin * s_out * sg_s * (1.0 - sg_s)).astype(BF16)
        dgate_ref[:, 1024:] = (dmixin * aout_ref[...] * sg_a * (1.0 - sg_a)).astype(BF16)
        d_o = _mm_nt(daoutb, woa[...])
        do_ref[...] = d_o.astype(BF16)
        lane = lax.broadcasted_iota(jnp.int32, (o.shape[0], HEAD_PAD), 1)
        dlc = jnp.zeros((o.shape[0], HEAD_PAD), F32)
        for hh in range(MLA_HEADS):
            sl = slice(hh * HEAD_PAD, (hh + 1) * HEAD_PAD)
            dl = jnp.sum(d_o[:, sl] * o[:, sl], axis=1, keepdims=True)
            dlc = dlc + jnp.where(lane == hh, dl, 0.0)
        dl_ref[...] = dlc

    s = o.shape[0]
    m = memk.shape[0]
    return _row_call(
        "post_bwd", body, s, min(ROW_TILE, s), [dh2, o, s_out, gate, a_out, r1, r2, qx], [memk, memv, *lns],
        [w_oa, w_o, w_xq, w_xo],
        [(1024, BF16), (HEAD_PAD, F32), (1024, F32), (2048, BF16), (1024, F32),
         (1024, BF16), (1024, BF16), (1024, BF16), (1024, BF16)],
        [(m, 1024), (m, 1024), (1, 1024), (1, 1024), (1, 1024), (1, 1024)])


def _s5out_bwd(d_so, yl, z3, h, d_skip, pwb, w_c, w_glu):
    s = d_so.shape[0]
    ts = min(ROW_TILE, s)
    n_tiles = s // ts

    def body(dso_ref, yl_ref, z3_ref, h_ref, d_ref, pw_ref, wc, wglu,
             lam_ref, dup_ref, dd_ref, da_ref, dwglu_ref, dwc_ref, gh_scr, carry, acc):
        @pl.when(pl.program_id(0) == 0)
        def _():
            dd_ref[...] = jnp.zeros_like(dd_ref)
            dwglu_ref[...] = jnp.zeros_like(dwglu_ref)
            dwc_ref[...] = jnp.zeros_like(dwc_ref)
            carry[...] = jnp.zeros_like(carry)
            acc[...] = jnp.zeros_like(acc)

        yl = yl_ref[...]
        ygb = _gelu(yl).astype(BF16)
        y12 = _mm(ygb, wglu[...])
        sg = _sigmoid(y12[:, 1024:])
        dso = dso_ref[...]
        dy12b = jnp.concatenate([dso * sg, dso * y12[:, :1024] * sg * (1.0 - sg)], axis=1).astype(BF16)
        dwglu_ref[...] += _mm_tn(ygb, dy12b)
        dyl = _mm_nt(dy12b, wglu[...]) * _gelu_grad(yl)
        dylb = dyl.astype(BF16)
        dwc_ref[...] += _mm_tn(h_ref[...], dylb)
        gh_scr[...] = _mm_nt(dylb, wc[...])
        _scan_bwd_rows(pw_ref, gh_scr, h_ref, lam_ref, carry, acc, ts // 8)
        dup_ref[...] = dyl * d_ref[...]
        dd_ref[...] += jnp.sum(dyl * z3_ref[:, :256], axis=0, keepdims=True)

        @pl.when(pl.program_id(0) == n_tiles - 1)
        def _():
            da_ref[...] = jnp.sum(acc[...], axis=0, keepdims=True)

    return _row_call("s5out_bwd", body, s, ts, [d_so, yl, z3, h], [d_skip, pwb], [w_c, w_glu],
                     [(2 * NS, F32), (256, F32)], [(1, 256), (1, 2 * NS), (S5_WIDTH, 2048), (2 * NS, S5_WIDTH)],
                     reverse=True, scratch=[pltpu.VMEM((ts, 2 * NS), F32), pltpu.VMEM((8, 2 * NS), F32),
                                            pltpu.VMEM((8, 2 * NS), F32)])


def _in_bwd(x, tab, z3, dq, dk, dv, lam, du_p, dgate, dh0p, ln_g, ln_b, qg, kvg, w_inx, w_b, w_q2, w_kv2):
    def body(x_ref, tab_ref, z3_ref, dq_ref, dk_ref, dv_ref, lam_ref, dup_ref, dgate_ref, dh0p_ref,
             lng_ref, lnb_ref, qg_ref, kvg_ref, winx, wb, wq2, wkv2,
             dx_ref, h0b_ref, dz_ref, dlng_ref, dlnb_ref, dqg_ref, dkvg_ref, dwq_ref, dwkv_ref, dwb_ref):
        @pl.when(pl.program_id(0) == 0)
        def _():
            for r in (dlng_ref, dlnb_ref, dqg_ref, dkvg_ref, dwq_ref, dwkv_ref, dwb_ref):
                r[...] = jnp.zeros_like(r)

        h0, xh0, rs0 = _ln_fwd(x_ref[...], lng_ref[...], lnb_ref[...])
        h0b_ref[...] = h0.astype(BF16)
        z3 = z3_ref[...]
        tab_v = tab_ref[...]
        cq1, cq2, ck1 = tab_v[:, :128], tab_v[:, 128:], tab_v[:, :128]
        lamb = lam_ref[...].astype(BF16)
        du = _mm_nt(lamb, wb[...]) + dup_ref[...]
        dwb_ref[...] += _mm_tn(z3[:, :256], lamb)
        dq = dq_ref[...].astype(F32)
        dq2b = jnp.concatenate([dq * _tile_heads(cq1), dq * _tile_heads(cq2)], axis=1).astype(BF16)
        cqg, cqn, rq = _rms_fwd(z3[:, 256:512], qg_ref[...])
        dwq_ref[...] += _mm_tn(cqg, dq2b)
        dcq, dqg = _rms_bwd(_mm_nt(dq2b, wq2[...]), cqn, rq, qg_ref[...])
        dqg_ref[...] += dqg
        dkv2b = jnp.concatenate([dk_ref[...], dv_ref[...]], axis=1)
        dk = dk_ref[...].astype(F32)
        ckvg, ckvn, rkv = _rms_fwd(z3[:, 512:768], kvg_ref[...])
        dwkv_ref[...] += _mm_tn(ckvg, dkv2b)
        dckv, dkvg = _rms_bwd(_mm_nt(dkv2b, wkv2[...]), ckvn, rkv, kvg_ref[...])
        dkvg_ref[...] += dkvg
        dkrp = dk[:, :HEAD_PAD]
        for hh in range(1, MLA_HEADS):
            dkrp = dkrp + dk[:, hh * HEAD_PAD:(hh + 1) * HEAD_PAD]
        dzb = jnp.concatenate([a.astype(BF16) for a in (du, dcq, dckv, dkrp * ck1, dkrp * cq2)] + [dgate_ref[...]],
                              axis=1)
        dz_ref[...] = dzb
        dh0 = _mm_nt(dzb, winx[...]) + dh0p_ref[...]
        dx, dg, db = _ln_bwd(dh0, xh0, rs0, lng_ref[...])
        dx_ref[...] = dx
        dlng_ref[...] += dg
        dlnb_ref[...] += db

    s = x.shape[0]
    return _row_call(
        "in_bwd", body, s, min(ROW_TILE, s), [x, tab, z3, dq, dk, dv, lam, du_p, dgate, dh0p],
        [ln_g, ln_b, qg, kvg], [w_inx, w_b, w_q2, w_kv2],
        [(1024, F32), (1024, BF16), (3072, BF16)],
        [(1, 1024), (1, 1024), (1, 256), (1, 256), (256, 2048), (256, 2048), (S5_WIDTH, 2 * NS)])


def _wgrad(name, xs, dy, out_dtype=F32, blocks=1, after=None):
    s, k = xs.shape
    n = dy.shape[1]
    nb = n // blocks
    tk = min(k, 1024)
    tn = max(d for d in range(128, min(n, 2048) + 1, 128) if n % d == 0)
    ts = min(s, 2048)
    per_tile = tn // nb
    assert s % ts == 0 and k % tk == 0 and n % tn == 0 and (blocks == 1 or tn % nb == 0), (name, s, k, n)
    last = s // ts - 1

    def body(x_ref, dy_ref, *rest):
        o_ref, acc = rest[-2:]

        @pl.when(pl.program_id(2) == 0)
        def _():
            acc[...] = jnp.zeros_like(acc)

        acc[...] += _mm_tn(x_ref[...], dy_ref[...])

        @pl.when(pl.program_id(2) == last)
        def _():
            if blocks == 1:
                o_ref[...] = acc[...].astype(out_dtype)
            else:
                for d in range(per_tile):
                    o_ref[d] = acc[:, d * nb:(d + 1) * nb].astype(out_dtype)

    if blocks == 1:
        out_shape = jax.ShapeDtypeStruct((k, n), out_dtype)
        out_spec = pl.BlockSpec((tk, tn), lambda a, b, c: (a, b))
    else:
        out_shape = jax.ShapeDtypeStruct((blocks, k, nb), out_dtype)
        out_spec = pl.BlockSpec((per_tile, tk, nb), lambda a, b, c: (b, a, 0))
    return pl.pallas_call(
        body, name=name, grid=(k // tk, n // tn, s // ts),
        in_specs=[pl.BlockSpec((ts, tk), lambda a, b, c: (c, a)), pl.BlockSpec((ts, tn), lambda a, b, c: (c, b))]
        + ([] if after is None else [pl.BlockSpec(memory_space=pl.ANY)]),
        out_specs=out_spec, out_shape=out_shape, scratch_shapes=[pltpu.VMEM((tk, tn), F32)],
        compiler_params=pltpu.CompilerParams(dimension_semantics=("arbitrary", "arbitrary", "arbitrary"),
                                             vmem_limit_bytes=VMEM_LIMIT),
    )(xs, dy, *([] if after is None else [after]))


def _mesh_pos():
    x, y, c = lax.axis_index("x"), lax.axis_index("y"), lax.axis_index("c")
    return x, y, c


def _peer(x, y, c, k):
    px = 1 - x if k & 4 else x
    py = 1 - y if k & 2 else y
    pc = 1 - c if k & 1 else c
    return (px, py, pc), 4 * px + 2 * py + pc


def _all_gather_two_level(name, arrays):
    n = len(arrays)

    def body(*refs):
        src, dst = refs[:n], refs[n:2 * n]
        send_sems, recv_sems, local_sems = refs[2 * n:]
        x, y, c = _mesh_pos()
        me = 4 * x + 2 * y + c
        sibling = (x, y, 1 - c)

        def copy(i, k, source, slot, to):
            return pltpu.make_async_remote_copy(src_ref=source, dst_ref=dst[i].at[slot],
                                                send_sem=send_sems.at[(k - 1) * n + i],
                                                recv_sem=recv_sems.at[(k - 1) * n + i],
                                                device_id=to, device_id_type=MESH)

        own = [pltpu.make_async_copy(src[i], dst[i].at[me], local_sems.at[i]) for i in range(n)]
        for cp in own:
            cp.start()
        sends = []
        for k in (1, 2, 4, 6):
            peer, _ = _peer(x, y, c, k)
            for i in range(n):
                sends.append(copy(i, k, src[i], me, peer))
                sends[-1].start()
        for k in (2, 4, 6):
            peer, pid = _peer(x, y, c, k)
            for i in range(n):
                copy(i, k, src[i], pid, peer).wait_recv()
                sends.append(copy(i, k + 1, dst[i].at[pid], pid, sibling))
                sends[-1].start()
        for k in (1, 3, 5, 7):
            peer, pid = _peer(x, y, c, k)
            for i in range(n):
                copy(i, k, src[i], pid, peer).wait_recv()
        for cp in sends:
            cp.wait_send()
        for cp in own:
            cp.wait()

    n_sem = n * (N_DEV - 1)
    return pl.pallas_call(
        body, name=name,
        in_specs=[pl.BlockSpec(memory_space=pl.ANY)] * n,
        out_specs=[pl.BlockSpec(memory_space=pl.ANY)] * n,
        out_shape=[jax.ShapeDtypeStruct((N_DEV,) + a.shape, a.dtype) for a in arrays],
        scratch_shapes=[pltpu.SemaphoreType.DMA((n_sem,)), pltpu.SemaphoreType.DMA((n_sem,)),
                        pltpu.SemaphoreType.DMA((n,))],
    )(*arrays)


_HBM = pl.BlockSpec(memory_space=pltpu.HBM)
_SEM = pl.BlockSpec(memory_space=pltpu.SEMAPHORE)


def _exchange_start(name, arrays, gather, after=None):
    n = len(arrays)
    n_sem = n * (N_DEV - 1)
    me = 4 * lax.axis_index("x") + 2 * lax.axis_index("y") + lax.axis_index("c")
    lands = []
    for a in arrays:
        own = a[None] if gather else lax.dynamic_slice_in_dim(a, me, 1, 0)
        lands.append(lax.dynamic_update_slice(lax.empty((N_DEV,) + a.shape[-2:], a.dtype), own, (me, 0, 0)))
    n_after = 0 if after is None else 1

    def body(*refs):
        src, land = refs[:n], refs[n:2 * n]
        send_sems, recv_sems = refs[2 * n + n_after], refs[2 * n + n_after + 1]
        token = refs[-1]
        x, y, c = _mesh_pos()
        me_in = 4 * x + 2 * y + c
        for k in range(1, N_DEV):
            peer, pid = _peer(x, y, c, k)
            for i in range(n):
                idx = (k - 1) * n + i
                pltpu.make_async_remote_copy(src_ref=src[i] if gather else src[i].at[pid], dst_ref=land[i].at[me_in],
                                             send_sem=send_sems.at[idx], recv_sem=recv_sems.at[idx],
                                             device_id=peer, device_id_type=MESH).start()
        token[...] = jnp.zeros_like(token)

    operands = [pltpu.with_memory_space_constraint(a, pltpu.HBM) for a in list(arrays) + lands]
    outs = pl.pallas_call(
        body, name=name,
        out_shape=(pltpu.SemaphoreType.DMA((n_sem,)), pltpu.SemaphoreType.DMA((n_sem,)),
                   *[pltpu.HBM(a.shape, a.dtype) for a in list(arrays) + lands],
                   jax.ShapeDtypeStruct((8, 128), F32)),
        in_specs=[_HBM] * (2 * n) + [pl.BlockSpec(memory_space=pl.ANY)] * n_after,
        out_specs=(_SEM, _SEM, *[_HBM] * (2 * n), pl.BlockSpec(memory_space=pltpu.VMEM)),
        input_output_aliases={i: 2 + i for i in range(2 * n)},
        compiler_params=pltpu.CompilerParams(has_side_effects=pltpu.SideEffectType.DATAFLOW_SIDE_EFFECTING),
    )(*operands, *([after] if n_after else []))
    return (gather, outs[0], outs[1], outs[2:2 + n], outs[2 + n:2 + 2 * n]), outs[-1]


def _exchange_wait(name, handle, after):
    gather, send_sems, recv_sems, srcs, lands = handle
    n = len(srcs)

    def body(*refs):
        src, land = refs[:n], refs[n:2 * n]
        s_sems, r_sems = refs[2 * n], refs[2 * n + 1]
        x, y, c = _mesh_pos()
        for k in range(1, N_DEV):
            peer, pid = _peer(x, y, c, k)
            for i in range(n):
                idx = (k - 1) * n + i
                cp = pltpu.make_async_remote_copy(src_ref=src[i] if gather else src[i].at[pid], dst_ref=land[i].at[pid],
                                                  send_sem=s_sems.at[idx], recv_sem=r_sems.at[idx],
                                                  device_id=peer, device_id_type=MESH)
                cp.wait_send()
                cp.wait_recv()

    outs = pl.pallas_call(
        body, name=name,
        out_shape=tuple(pltpu.HBM(a.shape, a.dtype) for a in list(srcs) + list(lands)),
        in_specs=[_HBM] * (2 * n) + [_SEM, _SEM, pl.BlockSpec(memory_space=pl.ANY)],
        out_specs=tuple([_HBM] * (2 * n)),
        input_output_aliases={i: i for i in range(2 * n)},
        compiler_params=pltpu.CompilerParams(has_side_effects=pltpu.SideEffectType.DATAFLOW_SIDE_EFFECTING),
    )(*srcs, *lands, send_sems, recv_sems, after)
    return list(outs[n:])


def _adamw(name, parts, w, m, v):
    a_rows, b_cols = w.shape
    ta = min(a_rows, ADAM_TILE)
    assert a_rows % ta == 0
    c1 = 1.0 - ADAM_B1 ** ADAM_STEP
    c2 = 1.0 - ADAM_B2 ** ADAM_STEP

    def body(p_ref, w_ref, m_ref, v_ref, g_ref, d_ref, mo_ref, vo_ref):
        g = p_ref[0].astype(F32)
        for d in range(1, N_DEV):
            g = g + p_ref[d].astype(F32)
        g_ref[...] = g
        mn = ADAM_B1 * m_ref[...] + (1.0 - ADAM_B1) * g
        vn = ADAM_B2 * v_ref[...] + (1.0 - ADAM_B2) * (g * g)
        mo_ref[...] = mn
        vo_ref[...] = vn
        d_ref[...] = -ADAM_LR * ((mn / c1) / (jnp.sqrt(vn / c2) + ADAM_EPS) + ADAM_WD * w_ref[...])

    row = pl.BlockSpec((ta, b_cols), lambda i: (i, 0))
    return pl.pallas_call(
        body, name=name, grid=(a_rows // ta,),
        in_specs=[pl.BlockSpec((N_DEV, ta, b_cols), lambda i: (0, i, 0)), row, row, row],
        out_specs=[row] * 4,
        out_shape=[jax.ShapeDtypeStruct((a_rows, b_cols), F32)] * 4,
        compiler_params=pltpu.CompilerParams(dimension_semantics=("arbitrary",), vmem_limit_bytes=VMEM_LIMIT),
    )(parts, w, m, v)


def _pack_rows(arrays, rows):
    flat = jnp.concatenate([a.reshape(-1) for a in arrays])
    return jnp.pad(flat, (0, rows * LANES - flat.shape[0])).reshape(rows, LANES)


def _cols_from_blocks(g):
    return g.transpose(1, 0, 2).reshape(g.shape[1], N_DEV * g.shape[2])


def _blocks_from_cols(w):
    return w.reshape(w.shape[0], N_DEV, w.shape[1] // N_DEV).transpose(1, 0, 2)


def _unpack_flat(flat2d, shapes):
    flat = flat2d.reshape(-1)
    out, off = [], 0
    for shp in shapes:
        sz = math.prod(shp)
        out.append(flat[off:off + sz].reshape(shp))
        off += sz
    return out


def _s5_mats(lam_re, lam_im, log_dt, b_re, b_im, c_re, c_im):
    lr = jnp.minimum(lam_re, S5_MAX_RE)
    li = lam_im
    dt = jnp.exp(log_dt)[:, None]
    mag = jnp.exp(lr * dt)
    ang = li * dt
    ab_re = mag * jnp.cos(ang)
    ab_im = mag * jnp.sin(ang)
    den = lr * lr + li * li
    nr = ab_re - 1.0
    f_re = ((nr * lr + ab_im * li) / den)[..., None]
    f_im = ((ab_im * lr - nr * li) / den)[..., None]
    bb_re = f_re * b_re - f_im * b_im
    bb_im = f_re * b_im + f_im * b_re
    eye = jnp.eye(S5_GROUPS, dtype=F32)
    a = jnp.stack([ab_re.reshape(-1), ab_im.reshape(-1)])
    wb = jnp.concatenate([jnp.einsum("gph,gk->ghkp", bb_re, eye).reshape(S5_WIDTH, NS),
                          jnp.einsum("gph,gk->ghkp", bb_im, eye).reshape(S5_WIDTH, NS)], axis=1)
    wc = jnp.concatenate([jnp.einsum("ghp,gk->gpkh", c_re, eye).reshape(NS, S5_WIDTH),
                          -jnp.einsum("ghp,gk->gpkh", c_im, eye).reshape(NS, S5_WIDTH)], axis=0)
    return a, wb, wc


def _power_table(a):
    ar, ai = a[0], a[1]
    rows_r, rows_i = [ar], [ai]
    for _ in range(7):
        pr, pi = rows_r[-1], rows_i[-1]
        rows_r.append(pr * ar - pi * ai)
        rows_i.append(pr * ai + pi * ar)
    return jnp.concatenate([jnp.stack(rows_r), jnp.stack(rows_i)], axis=1)


def _rope_table(positions):
    inv = ROPE_THETA ** (-jnp.arange(0, MLA_ROPE, 2, dtype=F32) / MLA_ROPE)
    inv128 = jnp.concatenate([jnp.zeros((MLA_NOPE,), F32), inv, inv, jnp.zeros((32,), F32)])
    sign = jnp.concatenate([jnp.zeros((MLA_NOPE,), F32), -jnp.ones((16,), F32), jnp.ones((16,), F32),
                            jnp.zeros((32,), F32)])
    ang = positions.astype(F32)[:, None] * inv128
    return jnp.concatenate([jnp.cos(ang), jnp.sin(ang) * sign], axis=1)


def _derived_weights(full):
    w_in = full["w_in"]
    k1, k2 = w_in[:, 768:784], w_in[:, 784:800]
    z64, z32 = jnp.zeros((1024, 64), BF16), jnp.zeros((1024, 32), BF16)
    w_inx = jnp.concatenate([w_in[:, :768], z64, k1, k2, z32, z64, k2, k1, z32, w_in[:, 800:]], axis=1)
    uq = full["w_uq"].reshape(256, MLA_HEADS, MLA_QK)
    nope, r1, r2 = uq[:, :, :64], uq[:, :, 64:80], uq[:, :, 80:]
    zq64, zq32 = jnp.zeros((256, MLA_HEADS, 64), BF16), jnp.zeros((256, MLA_HEADS, 32), BF16)
    w_q2 = jnp.concatenate([jnp.concatenate([nope, r1, r2, zq32], axis=2).reshape(256, 1024),
                            jnp.concatenate([zq64, r2, r1, zq32], axis=2).reshape(256, 1024)], axis=1)
    ukv = full["w_ukv"].reshape(256, MLA_HEADS, 128)
    w_kv2 = jnp.concatenate([jnp.concatenate([ukv[:, :, :64], zq64], axis=2).reshape(256, 1024),
                             jnp.concatenate([ukv[:, :, 64:], zq64], axis=2).reshape(256, 1024)], axis=1)
    return w_inx, w_q2, w_kv2


def _oa_padded(w_oa):
    oa = w_oa.reshape(MLA_HEADS, MLA_V, 1024)
    return jnp.concatenate([oa, jnp.zeros_like(oa)], axis=1).reshape(1024, 1024)


def _fold_w_in(d_inx):
    d_k1 = d_inx[:, 832:848] + d_inx[:, 976:992]
    d_k2 = d_inx[:, 848:864] + d_inx[:, 960:976]
    return jnp.concatenate([d_inx[:, :768], d_k1, d_k2, d_inx[:, 1024:]], axis=1)


def _fold_qkv(d_q2, d_kv2):
    a = d_q2[:, :1024].reshape(256, MLA_HEADS, 128)
    b = d_q2[:, 1024:].reshape(256, MLA_HEADS, 128)
    d_w_uq = jnp.concatenate([a[:, :, :64], a[:, :, 64:80] + b[:, :, 80:96], a[:, :, 80:96] + b[:, :, 64:80]],
                             axis=2).reshape(256, MLA_HEADS * MLA_QK)
    kk = d_kv2[:, :1024].reshape(256, MLA_HEADS, 128)
    vv = d_kv2[:, 1024:].reshape(256, MLA_HEADS, 128)
    d_w_ukv = jnp.concatenate([kk[:, :, :64], vv[:, :, :64]], axis=2).reshape(256, 1024)
    return d_w_uq, d_w_ukv


def kernel(x, mem, positions, ln_in_g, ln_in_b, w_in, s5_lam_re, s5_lam_im, s5_log_dt, s5_b_re, s5_b_im, s5_c_re, s5_c_im, s5_d, w_glu, q_norm_g, w_uq, kv_norm_g, w_ukv, w_oa, w_o, ln1_g, ln1_b, w_xq, w_xk, w_xv, w_xo, ln2_g, ln2_b, w_up, w_down, ln3_g, ln3_b, loss_target, m_ln_in_g, m_ln_in_b, m_w_in, m_s5_lam_re, m_s5_lam_im, m_s5_log_dt, m_s5_b_re, m_s5_b_im, m_s5_c_re, m_s5_c_im, m_s5_d, m_w_glu, m_q_norm_g, m_w_uq, m_kv_norm_g, m_w_ukv, m_w_oa, m_w_o, m_ln1_g, m_ln1_b, m_w_xq, m_w_xk, m_w_xv, m_w_xo, m_ln2_g, m_ln2_b, m_w_up, m_w_down, m_ln3_g, m_ln3_b, v_ln_in_g, v_ln_in_b, v_w_in, v_s5_lam_re, v_s5_lam_im, v_s5_log_dt, v_s5_b_re, v_s5_b_im, v_s5_c_re, v_s5_c_im, v_s5_d, v_w_glu, v_q_norm_g, v_w_uq, v_kv_norm_g, v_w_ukv, v_w_oa, v_w_o, v_ln1_g, v_ln1_b, v_w_xq, v_w_xk, v_w_xv, v_w_xo, v_ln2_g, v_ln2_b, v_w_up, v_w_down, v_ln3_g, v_ln3_b):
    args = dict(locals())
    wts = {n: args[n] for n in WEIGHTS}
    mom = {n: args["m_" + n] for n in WEIGHTS}
    vel = {n: args["v_" + n] for n in WEIGHTS}
    xs, mems, tgt = x[0], mem[0], loss_target[0]
    s = xs.shape[0]
    names_big = [n for (n, _, _, _) in SHARDED]
    kind = {n: kd for (n, kd, _, _) in SHARDED}
    n_small = sum(math.prod(wts[n].shape) for n in SMALL)
    small_rows = _round_up(n_small + 1, 8 * LANES) // LANES

    late = [n for n in names_big if n not in EARLY]
    shard = {n: wts[n][0].astype(BF16) for n in names_big}
    g_early = dict(zip(EARLY, _all_gather_two_level("all_gather", [shard[n] for n in EARLY])))
    ag_handle, ag_token = _exchange_start("all_gather_late", [shard[n] for n in late], True, after=g_early["w_in"])
    full = {n: _cols_from_blocks(g_early[n]) for n in EARLY}
    w_inx, w_q2, w_kv2 = _derived_weights(full)
    s5_args = (s5_lam_re[0], s5_lam_im[0], s5_log_dt[0], s5_b_re[0], s5_b_im[0], s5_c_re[0], s5_c_im[0])
    (a_mat, w_b, w_c), s5_vjp = jax.vjp(_s5_mats, *s5_args)
    w_bb, w_cb = w_b.astype(BF16), w_c.astype(BF16)
    pw = _power_table(a_mat)
    pwb = _power_table(a_mat * jnp.array([[1.0], [-1.0]], F32))[::-1]
    tab = _rope_table(positions[0])
    row = lambda a: a.reshape(1, -1)
    ln_g, ln_b = row(ln_in_g) + ag_token[0:1, 0:1], row(ln_in_b)
    lns = [ln1_g, ln1_b, ln2_g, ln2_b]

    h0, z3, gate, h, q, k, v, kt, vt = _in_fwd(xs, tab, ln_g, ln_b, q_norm_g, kv_norm_g, pw, w_inx, w_bb, w_q2, w_kv2)
    yl, s_out = _s5out_fwd(h, z3, s5_d, w_cb, full["w_glu"])
    o, lse_t = _attn_fwd(q, k, vt)
    g_late = dict(zip(late, _exchange_wait("all_gather_late_wait", ag_handle, after=lse_t)))
    full.update({n: g_late[n].reshape(-1, g_late[n].shape[2]) if kind[n] == "row" else _cols_from_blocks(g_late[n])
                 for n in late})
    w_oap = _oa_padded(full["w_oa"])
    memk, memv = _mem_kv(mems, full["w_xk"], full["w_xv"])
    r2, mixin_b, h1_b, ox_b, a_out, r1, qx_b = _post_fwd(o, s_out, gate, h0, memk, memv, lns,
                                        w_oap, full["w_o"], full["w_xq"], full["w_xo"])
    dh2, h2_b, dup_b, act_b, dff_b, loss_acc, d_ln3_g, d_ln3_b = _mlp(
        r2, tgt, ln2_g, ln2_b, ln3_g, ln3_b, full["w_up"], full["w_down"])
    (do_b, delta, d_so, dgate, dh0p, daout_b, dmix_b, dqx_b, dxa_b,
     dmk, dmv, d_ln1_g, d_ln1_b, d_ln2_g, d_ln2_b) = _post_bwd(
        dh2, o, s_out, gate, a_out, r1, r2, qx_b, memk, memv, lns, w_oap, full["w_o"], full["w_xq"], full["w_xo"])
    rows8 = lambda g: g.reshape(N_DEV, g.shape[0] // N_DEV, g.shape[1])
    d_w_oa = _wgrad("wg_oa", o, daout_b).reshape(MLA_HEADS, 128, 1024)[:, :64].reshape(512, 1024)
    send = {
        "w_oa": _blocks_from_cols(d_w_oa).astype(BF16),
        "w_up": _wgrad("wg_up", h2_b, dup_b, BF16, N_DEV),
        "w_o": rows8(_wgrad("wg_o", mixin_b, dmix_b, BF16)),
        "w_xq": rows8(_wgrad("wg_xq", h1_b, dqx_b, BF16)),
        "w_xk": rows8(_wgrad("wg_xk", mems, dmk, BF16)),
        "w_xv": rows8(_wgrad("wg_xv", mems, dmv, BF16)),
        "w_xo": rows8(_wgrad("wg_xo", ox_b, dxa_b, BF16)),
        "w_down": rows8(_wgrad("wg_down", act_b, dff_b, BF16)),
    }
    rs_handle, rs_token = _exchange_start("grad_exchange_late", [send[n] for n in late], False)
    dl_t = delta.T[:MLA_HEADS].reshape(MLA_HEADS, 1, s) + rs_token[0, 0]
    dq, dk, dv = _attn_bwd(q, k, v, kt, do_b, lse_t, dl_t)
    lam, du_p, d_s5_d, d_a, d_w_glu, d_wc = _s5out_bwd(d_so, yl, z3, h, s5_d, pwb, w_cb, full["w_glu"])
    (dx, h0_b, dz_b, d_ln_g, d_ln_b, d_qg, d_kvg, d_q2, d_kv2, d_wb) = _in_bwd(
        xs, tab, z3, dq, dk, dv, lam, du_p, dgate, dh0p, ln_g, ln_b, q_norm_g, kv_norm_g, w_inx, w_bb, w_q2, w_kv2)

    d_s5 = s5_vjp((d_a.reshape(2, NS), d_wb, d_wc))
    small_grads = {
        "ln_in_g": d_ln_g, "ln_in_b": d_ln_b, "s5_lam_re": d_s5[0], "s5_lam_im": d_s5[1], "s5_log_dt": d_s5[2],
        "s5_b_re": d_s5[3], "s5_b_im": d_s5[4], "s5_c_re": d_s5[5], "s5_c_im": d_s5[6], "s5_d": d_s5_d,
        "q_norm_g": d_qg, "kv_norm_g": d_kvg, "ln1_g": d_ln1_g, "ln1_b": d_ln1_b, "ln2_g": d_ln2_g,
        "ln2_b": d_ln2_b, "ln3_g": d_ln3_g, "ln3_b": d_ln3_b,
    }
    sm_handle, sm_token = _exchange_start(
        "grad_exchange_small", [_pack_rows([small_grads[n] for n in SMALL] + [loss_acc[0, :1]], small_rows)], True)

    d_w_in = _fold_w_in(_wgrad("wg_in", h0_b, dz_b, after=sm_token))
    d_w_uq, d_w_ukv = _fold_qkv(d_q2, d_kv2)
    send.update({
        "w_in": _blocks_from_cols(d_w_in).astype(BF16), "w_uq": _blocks_from_cols(d_w_uq).astype(BF16),
        "w_ukv": _blocks_from_cols(d_w_ukv).astype(BF16), "w_glu": _blocks_from_cols(d_w_glu).astype(BF16),
    })
    re_handle, re_token = _exchange_start("grad_exchange_early", [send[n] for n in EARLY], False)

    recv = dict(zip(late, _exchange_wait("grad_exchange_late_wait", rs_handle, after=re_token)))
    results = [dict(), dict(), dict(), dict()]
    for n in late + list(EARLY):
        if n == EARLY[0]:
            recv.update(zip(EARLY, _exchange_wait("grad_exchange_early_wait", re_handle, after=small_out[0])))
        outs = _adamw("adamw_" + n, recv[n], wts[n][0], mom[n][0], vel[n][0])
        for res, a in zip(results, outs):
            res[n] = a[None]
        if n == late[-1]:
            recv_small = _exchange_wait("grad_exchange_small_wait", sm_handle, after=results[0][n])[0]
            small_out = _adamw("adamw_small", recv_small,
                               *[_pack_rows([t[n] for n in SMALL], small_rows) for t in (wts, mom, vel)])
            for res, fs in zip(results, small_out):
                for ns, a in zip(SMALL, _unpack_flat(fs, [wts[ns].shape for ns in SMALL])):
                    res[ns] = a

    loss = small_out[0].reshape(-1)[n_small]
    return (loss, dx[None], *[res[n] for res in results for n in WEIGHTS])
```
